```python
import math
import jax, jax.numpy as jnp
from jax import lax
import numpy as np

D_MODEL = 1024
BATCH = 8
SEQ = 8192
DEPTH = 1

CTX_LEN = 256
GRID_W = 64
N_HEADS = 8
QK_NOPE = 64
QK_ROPE = 32
QK_DIM = QK_NOPE + QK_ROPE
V_DIM = 64
Q_LORA = 384
KV_LORA = 256
ATTN_WIDTH = N_HEADS * V_DIM
ROPE_THETA = 10000.0
ROPE_AXIS_PAIRS = QK_ROPE // 4
Q_BLOCK = 128
SSM_WIDTH = 512
SSM_GROUP = 16
SSM_GROUPS = SSM_WIDTH // SSM_GROUP
SSM_STATE = 64
FFN_HIDDEN = 2816
CONV_W = 3
N_BRANCH = 2
EPS = 1e-6
IN_SPLITS = (Q_LORA, Q_LORA + KV_LORA, Q_LORA + KV_LORA + QK_ROPE,
             Q_LORA + KV_LORA + QK_ROPE + SSM_WIDTH)
IN_WIDTH = Q_LORA + KV_LORA + QK_ROPE + SSM_WIDTH + N_BRANCH * D_MODEL

kernel_name = "hybrid_mla_s5_convffn_prefix_ctx"


def rmsnorm(x, g):
    xf = x.astype(jnp.float32)
    y = xf * lax.rsqrt(jnp.mean(xf * xf, axis=-1, keepdims=True) + EPS)
    return (y * g.astype(jnp.float32)).astype(x.dtype)


def modulate(h, shift, scale):
    return h * (1.0 + scale) + shift


def axial_rope(rows):
    row = jnp.repeat(jnp.arange(rows), GRID_W)
    col = jnp.tile(jnp.arange(GRID_W), rows)
    freqs = ROPE_THETA ** (-jnp.arange(ROPE_AXIS_PAIRS, dtype=jnp.float32) / ROPE_AXIS_PAIRS)
    ang = jnp.concatenate([row[:, None] * freqs, col[:, None] * freqs], axis=-1)
    return jnp.cos(ang), jnp.sin(ang)


def apply_rope(t, cos, sin):
    nope, rope = t[..., :QK_NOPE], t[..., QK_NOPE:]
    r1, r2 = jnp.split(rope, 2, axis=-1)
    cs = cos[:, None, :].astype(t.dtype)
    sn = sin[:, None, :].astype(t.dtype)
    return jnp.concatenate([nope, r1 * cs - r2 * sn, r1 * sn + r2 * cs], axis=-1)


def mla_q(cq, p, cos, sin):
    b, n = cq.shape[:2]
    q = (rmsnorm(cq, p["q_a_g"]) @ p["w_uq"]).reshape(b, n, N_HEADS, QK_DIM)
    q = rmsnorm(q, p["q_norm_g"])
    return q if cos is None else apply_rope(q, cos, sin)


def mla_kv(ckv, krope, p, cos, sin):
    b, n = ckv.shape[:2]
    kv = (rmsnorm(ckv, p["kv_a_g"]) @ p["w_ukv"]).reshape(b, n, N_HEADS, QK_NOPE + V_DIM)
    k_nope, v = jnp.split(kv, [QK_NOPE], axis=-1)
    k_pe = jnp.broadcast_to(krope[:, :, None, :], (b, n, N_HEADS, QK_ROPE))
    k = rmsnorm(jnp.concatenate([k_nope, k_pe], axis=-1), p["k_norm_g"])
    if cos is not None:
        k = apply_rope(k, cos, sin)
    return k, v


def attend(q, k, v):
    s = jnp.einsum("bqhd,bkhd->bhqk", q, k, preferred_element_type=jnp.float32) * (QK_DIM ** -0.5)
    w = jax.nn.softmax(s, axis=-1).astype(v.dtype)
    return jnp.einsum("bhqk,bkhd->bqhd", w, v)


def latent_attention(q, k_lat, v_lat, k_ctx, v_ctx):
    k = jnp.concatenate([k_lat, k_ctx], axis=1)
    v = jnp.concatenate([v_lat, v_ctx], axis=1)
    b, n = q.shape[:2]
    qb = q.reshape(b, n // Q_BLOCK, Q_BLOCK, N_HEADS, QK_DIM).swapaxes(0, 1)
    o = lax.map(lambda qi: attend(qi, k, v), qb)
    return o.swapaxes(0, 1).reshape(b, n, ATTN_WIDTH)


def _ssm_combine(e_i, e_j):
    a_i, b_i = e_i
    a_j, b_j = e_j
    return a_j * a_i, a_j * b_i + b_j


def s5_states(u, p, init_f, init_b):
    b, n = u.shape[:2]
    ug = u.astype(jnp.float32).reshape(b, n, SSM_GROUPS, SSM_GROUP)
    bmat = lax.complex(p["b_re"].astype(jnp.float32), p["b_im"].astype(jnp.float32))
    out = []
    for sfx, init, reverse in (("f", init_f, False), ("b", init_b, True)):
        lam = lax.complex(p["lam_re_" + sfx].astype(jnp.float32), p["lam_im_" + sfx].astype(jnp.float32))
        dt = jnp.exp(p["log_dt_" + sfx].astype(jnp.float32))[:, None]
        lam_bar = jnp.exp(lam * dt)
        b_bar = ((lam_bar - 1.0) / lam)[..., None] * bmat
        bu = jnp.einsum("gnc,bsgc->bsgn", b_bar, ug)
        if init is not None:
            edge = n - 1 if reverse else 0
            bu = bu.at[:, edge].add(lam_bar * init)
        a = jnp.broadcast_to(lam_bar, bu.shape)
        _, xs = lax.associative_scan(_ssm_combine, (a, bu), reverse=reverse, axis=1)
        out.append(xs)
    return out[0], out[1]


def s5_readout(u, xs_f, xs_b, p):
    b, n = u.shape[:2]
    y = u.astype(jnp.float32) * p["d_skip"].astype(jnp.float32)
    for sfx, xs in (("f", xs_f), ("b", xs_b)):
        cm = lax.complex(p["c_re_" + sfx].astype(jnp.float32), p["c_im_" + sfx].astype(jnp.float32))
        y = y + jnp.einsum("gcn,bsgn->bsgc", cm, xs).real.reshape(b, n, SSM_WIDTH)
    return y.astype(u.dtype)


def ssm_glu(y, w_glu):
    val, gate = jnp.split(jax.nn.gelu(y) @ w_glu, 2, axis=-1)
    return val * jax.nn.sigmoid(gate)


def merge_branches(a, s, gate_logits, w_out):
    ga, gs = jnp.split(gate_logits, N_BRANCH, axis=-1)
    return (jax.nn.sigmoid(ga) * a + jax.nn.sigmoid(gs) * s) @ w_out


def dwconv3(u, w, bias):
    up = jnp.pad(u, ((0, 0), (1, 1), (0, 0)))
    return up[:, :-2] * w[0] + up[:, 1:-1] * w[1] + up[:, 2:] * w[2] + bias


def conv_ffn(h, p):
    u = dwconv3(h @ p["w_up"], p["conv_w"], p["conv_b"])
    val, gate = jnp.split(u, 2, axis=-1)
    return (jax.nn.silu(gate) * val) @ p["w_down"]


def hybrid_layer(x, ctx, c, c_ctx, p, cos, sin, update_ctx):
    mod = jax.nn.silu(c) @ p["w_mod"] + p["b_mod"]
    mod_ctx = jax.nn.silu(c_ctx) @ p["w_mod"] + p["b_mod"]
    sh1, sc1, g1, sh2, sc2, g2 = jnp.split(mod[:, None, :], 6, axis=-1)
    csh1, csc1, cg1, csh2, csc2, cg2 = jnp.split(mod_ctx, 6, axis=-1)

    h = modulate(rmsnorm(x, p["norm1_g"]), sh1, sc1)
    hc = modulate(rmsnorm(ctx, p["norm1_g"]), csh1, csc1)
    cq, ckv, kr, u, gl = jnp.split(h @ p["w_in"], IN_SPLITS, axis=-1)
    ccq, cckv, ckr, cu, cgl = jnp.split(hc @ p["w_in"], IN_SPLITS, axis=-1)

    k_c, v_c = mla_kv(cckv, ckr, p, None, None)
    xs_cf, xs_cb = s5_states(cu, p, None, None)

    q_l = mla_q(cq, p, cos, sin)
    k_l, v_l = mla_kv(ckv, kr, p, cos, sin)
    a_l = latent_attention(q_l, k_l, v_l, k_c, v_c) @ p["w_o_attn"]
    xs_f, xs_b = s5_states(u, p, xs_cf[:, -1], xs_cb[:, 0])
    s_l = ssm_glu(s5_readout(u, xs_f, xs_b, p), p["w_glu"])
    x = x + g1 * merge_branches(a_l, s_l, gl, p["w_out"])

    if update_ctx:
        a_c = attend(mla_q(ccq, p, None, None), k_c, v_c).reshape(ctx.shape[0], ctx.shape[1], ATTN_WIDTH)
        s_c = ssm_glu(s5_readout(cu, xs_cf, xs_cb, p), p["w_glu"])
        ctx = ctx + cg1 * merge_branches(a_c @ p["w_o_attn"], s_c, cgl, p["w_out"])

    x = x + g2 * conv_ffn(modulate(rmsnorm(x, p["norm2_g"]), sh2, sc2), p)
    if update_ctx:
        ctx = ctx + cg2 * conv_ffn(modulate(rmsnorm(ctx, p["norm2_g"]), csh2, csc2), p)
    return x, ctx


def _fwd_setup_inputs(seed: int = 0) -> dict:
    key = jax.random.key(seed)
    ks = iter(jax.random.split(key, 48))
    L, D, G, N, F = DEPTH, D_MODEL, SSM_GROUPS, SSM_STATE, FFN_HIDDEN

    def nrm(shape, scale):
        return jax.random.normal(next(ks), shape, jnp.float32) * scale

    def gain(n):
        return 1.0 + nrm((L, n), 0.02)

    n_idx = jnp.arange(N, dtype=jnp.float32)
    out = {}
    out["x"] = nrm((BATCH, SEQ, D), 1.0)
    out["c"] = nrm((BATCH, D), 1.0)
    out["ctx"] = nrm((BATCH, CTX_LEN, D), 1.0)
    out["c_ctx"] = nrm((D,), 1.0)
    out["w_mod"] = nrm((L, D, 6 * D), 0.5 * D ** -0.5)
    out["b_mod"] = nrm((L, 6 * D), 0.01)
    out["norm1_g"] = gain(D)
    out["norm2_g"] = gain(D)
    out["w_in"] = nrm((L, D, IN_WIDTH), D ** -0.5)
    out["q_a_g"] = gain(Q_LORA)
    out["w_uq"] = nrm((L, Q_LORA, N_HEADS * QK_DIM), Q_LORA ** -0.5)
    out["kv_a_g"] = gain(KV_LORA)
    out["w_ukv"] = nrm((L, KV_LORA, N_HEADS * (QK_NOPE + V_DIM)), KV_LORA ** -0.5)
    out["q_norm_g"] = gain(QK_DIM)
    out["k_norm_g"] = gain(QK_DIM)
    out["w_o_attn"] = nrm((L, ATTN_WIDTH, D), ATTN_WIDTH ** -0.5)
    for sfx in ("f", "b"):
        out["lam_re_" + sfx] = -0.5 + nrm((L, G, N), 0.01)
        out["lam_im_" + sfx] = math.pi * n_idx + nrm((L, G, N), 0.01)
        out["log_dt_" + sfx] = jax.random.uniform(next(ks), (L, G), jnp.float32,
                                                  math.log(1e-3), math.log(1e-1))
        out["c_re_" + sfx] = nrm((L, G, SSM_GROUP, N), (2.0 * N) ** -0.5)
        out["c_im_" + sfx] = nrm((L, G, SSM_GROUP, N), (2.0 * N) ** -0.5)
    out["b_re"] = nrm((L, G, N, SSM_GROUP), (2.0 * SSM_GROUP) ** -0.5)
    out["b_im"] = nrm((L, G, N, SSM_GROUP), (2.0 * SSM_GROUP) ** -0.5)
    out["d_skip"] = nrm((L, SSM_WIDTH), 1.0)
    out["w_glu"] = nrm((L, SSM_WIDTH, 2 * D), SSM_WIDTH ** -0.5)
    out["w_out"] = nrm((L, D, D), D ** -0.5)
    out["w_up"] = nrm((L, D, 2 * F), D ** -0.5)
    out["conv_w"] = nrm((L, CONV_W, 2 * F), CONV_W ** -0.5)
    out["conv_b"] = nrm((L, 2 * F), 0.01)
    out["w_down"] = nrm((L, F, D), F ** -0.5)
    return out


def _fwd_reference(x, c, ctx, c_ctx, w_mod, b_mod, norm1_g, norm2_g, w_in, q_a_g, w_uq, kv_a_g, w_ukv,
              q_norm_g, k_norm_g, w_o_attn,
              lam_re_f, lam_im_f, log_dt_f, c_re_f, c_im_f,
              lam_re_b, lam_im_b, log_dt_b, c_re_b, c_im_b,
              b_re, b_im, d_skip, w_glu, w_out, w_up, conv_w, conv_b, w_down):
    rows = x.shape[1] // GRID_W
    cos, sin = axial_rope(rows)
    for l in range(DEPTH):
        p = dict(w_mod=w_mod[l], b_mod=b_mod[l], norm1_g=norm1_g[l], norm2_g=norm2_g[l], w_in=w_in[l],
                 q_a_g=q_a_g[l], w_uq=w_uq[l], kv_a_g=kv_a_g[l], w_ukv=w_ukv[l],
                 q_norm_g=q_norm_g[l], k_norm_g=k_norm_g[l], w_o_attn=w_o_attn[l],
                 lam_re_f=lam_re_f[l], lam_im_f=lam_im_f[l], log_dt_f=log_dt_f[l],
                 c_re_f=c_re_f[l], c_im_f=c_im_f[l],
                 lam_re_b=lam_re_b[l], lam_im_b=lam_im_b[l], log_dt_b=log_dt_b[l],
                 c_re_b=c_re_b[l], c_im_b=c_im_b[l],
                 b_re=b_re[l], b_im=b_im[l], d_skip=d_skip[l], w_glu=w_glu[l], w_out=w_out[l],
                 w_up=w_up[l], conv_w=conv_w[l], conv_b=conv_b[l], w_down=w_down[l])
        x, ctx = hybrid_layer(x, ctx, c, c_ctx, p, cos, sin, update_ctx=(l < DEPTH - 1))
    return x


import jax as _jax
import jax.numpy as _jnp

TWIN_FORMAT = 'train_step'
FWD_PARAMS = ['x', 'c', 'ctx', 'c_ctx', 'w_mod', 'b_mod', 'norm1_g', 'norm2_g', 'w_in', 'q_a_g', 'w_uq', 'kv_a_g', 'w_ukv', 'q_norm_g', 'k_norm_g', 'w_o_attn', 'lam_re_f', 'lam_im_f', 'log_dt_f', 'c_re_f', 'c_im_f', 'lam_re_b', 'lam_im_b', 'log_dt_b', 'c_re_b', 'c_im_b', 'b_re', 'b_im', 'd_skip', 'w_glu', 'w_out', 'w_up', 'conv_w', 'conv_b', 'w_down']
TWIN_WEIGHTS = ['c_ctx', 'w_mod', 'b_mod', 'norm1_g', 'norm2_g', 'w_in', 'q_a_g', 'w_uq', 'kv_a_g', 'w_ukv', 'q_norm_g', 'k_norm_g', 'w_o_attn', 'lam_re_f', 'lam_im_f', 'log_dt_f', 'c_re_f', 'c_im_f', 'lam_re_b', 'lam_im_b', 'log_dt_b', 'c_re_b', 'c_im_b', 'b_re', 'b_im', 'd_skip', 'w_glu', 'w_out', 'w_up', 'conv_w', 'conv_b', 'w_down']
TWIN_DIFF_INPUT = 'x'
TWIN_INPUTS = ['x', 'c', 'ctx', 'c_ctx', 'w_mod', 'b_mod', 'norm1_g', 'norm2_g', 'w_in', 'q_a_g', 'w_uq', 'kv_a_g', 'w_ukv', 'q_norm_g', 'k_norm_g', 'w_o_attn', 'lam_re_f', 'lam_im_f', 'log_dt_f', 'c_re_f', 'c_im_f', 'lam_re_b', 'lam_im_b', 'log_dt_b', 'c_re_b', 'c_im_b', 'b_re', 'b_im', 'd_skip', 'w_glu', 'w_out', 'w_up', 'conv_w', 'conv_b', 'w_down', 'loss_target', 'm_c_ctx', 'm_w_mod', 'm_b_mod', 'm_norm1_g', 'm_norm2_g', 'm_w_in', 'm_q_a_g', 'm_w_uq', 'm_kv_a_g', 'm_w_ukv', 'm_q_norm_g', 'm_k_norm_g', 'm_w_o_attn', 'm_lam_re_f', 'm_lam_im_f', 'm_log_dt_f', 'm_c_re_f', 'm_c_im_f', 'm_lam_re_b', 'm_lam_im_b', 'm_log_dt_b', 'm_c_re_b', 'm_c_im_b', 'm_b_re', 'm_b_im', 'm_d_skip', 'm_w_glu', 'm_w_out', 'm_w_up', 'm_conv_w', 'm_conv_b', 'm_w_down', 'v_c_ctx', 'v_w_mod', 'v_b_mod', 'v_norm1_g', 'v_norm2_g', 'v_w_in', 'v_q_a_g', 'v_w_uq', 'v_kv_a_g', 'v_w_ukv', 'v_q_norm_g', 'v_k_norm_g', 'v_w_o_attn', 'v_lam_re_f', 'v_lam_im_f', 'v_log_dt_f', 'v_c_re_f', 'v_c_im_f', 'v_lam_re_b', 'v_lam_im_b', 'v_log_dt_b', 'v_c_re_b', 'v_c_im_b', 'v_b_re', 'v_b_im', 'v_d_skip', 'v_w_glu', 'v_w_out', 'v_w_up', 'v_conv_w', 'v_conv_b', 'v_w_down']
TWIN_OUTPUTS = ['loss', 'grad_x', 'grad_c_ctx', 'grad_w_mod', 'grad_b_mod', 'grad_norm1_g', 'grad_norm2_g', 'grad_w_in', 'grad_q_a_g', 'grad_w_uq', 'grad_kv_a_g', 'grad_w_ukv', 'grad_q_norm_g', 'grad_k_norm_g', 'grad_w_o_attn', 'grad_lam_re_f', 'grad_lam_im_f', 'grad_log_dt_f', 'grad_c_re_f', 'grad_c_im_f', 'grad_lam_re_b', 'grad_lam_im_b', 'grad_log_dt_b', 'grad_c_re_b', 'grad_c_im_b', 'grad_b_re', 'grad_b_im', 'grad_d_skip', 'grad_w_glu', 'grad_w_out', 'grad_w_up', 'grad_conv_w', 'grad_conv_b', 'grad_w_down', 'delta_c_ctx', 'delta_w_mod', 'delta_b_mod', 'delta_norm1_g', 'delta_norm2_g', 'delta_w_in', 'delta_q_a_g', 'delta_w_uq', 'delta_kv_a_g', 'delta_w_ukv', 'delta_q_norm_g', 'delta_k_norm_g', 'delta_w_o_attn', 'delta_lam_re_f', 'delta_lam_im_f', 'delta_log_dt_f', 'delta_c_re_f', 'delta_c_im_f', 'delta_lam_re_b', 'delta_lam_im_b', 'delta_log_dt_b', 'delta_c_re_b', 'delta_c_im_b', 'delta_b_re', 'delta_b_im', 'delta_d_skip', 'delta_w_glu', 'delta_w_out', 'delta_w_up', 'delta_conv_w', 'delta_conv_b', 'delta_w_down', 'new_m_c_ctx', 'new_m_w_mod', 'new_m_b_mod', 'new_m_norm1_g', 'new_m_norm2_g', 'new_m_w_in', 'new_m_q_a_g', 'new_m_w_uq', 'new_m_kv_a_g', 'new_m_w_ukv', 'new_m_q_norm_g', 'new_m_k_norm_g', 'new_m_w_o_attn', 'new_m_lam_re_f', 'new_m_lam_im_f', 'new_m_log_dt_f', 'new_m_c_re_f', 'new_m_c_im_f', 'new_m_lam_re_b', 'new_m_lam_im_b', 'new_m_log_dt_b', 'new_m_c_re_b', 'new_m_c_im_b', 'new_m_b_re', 'new_m_b_im', 'new_m_d_skip', 'new_m_w_glu', 'new_m_w_out', 'new_m_w_up', 'new_m_conv_w', 'new_m_conv_b', 'new_m_w_down', 'new_v_c_ctx', 'new_v_w_mod', 'new_v_b_mod', 'new_v_norm1_g', 'new_v_norm2_g', 'new_v_w_in', 'new_v_q_a_g', 'new_v_w_uq', 'new_v_kv_a_g', 'new_v_w_ukv', 'new_v_q_norm_g', 'new_v_k_norm_g', 'new_v_w_o_attn', 'new_v_lam_re_f', 'new_v_lam_im_f', 'new_v_log_dt_f', 'new_v_c_re_f', 'new_v_c_im_f', 'new_v_lam_re_b', 'new_v_lam_im_b', 'new_v_log_dt_b', 'new_v_c_re_b', 'new_v_c_im_b', 'new_v_b_re', 'new_v_b_im', 'new_v_d_skip', 'new_v_w_glu', 'new_v_w_out', 'new_v_w_up', 'new_v_conv_w', 'new_v_conv_b', 'new_v_w_down']
TWIN_LEAF_KINDS = {'loss': 'loss', 'grad_x': 'grad_x', 'grad_c_ctx': 'grad_w', 'grad_w_mod': 'grad_w', 'grad_b_mod': 'grad_w', 'grad_norm1_g': 'grad_w', 'grad_norm2_g': 'grad_w', 'grad_w_in': 'grad_w', 'grad_q_a_g': 'grad_w', 'grad_w_uq': 'grad_w', 'grad_kv_a_g': 'grad_w', 'grad_w_ukv': 'grad_w', 'grad_q_norm_g': 'grad_w', 'grad_k_norm_g': 'grad_w', 'grad_w_o_attn': 'grad_w', 'grad_lam_re_f': 'grad_w', 'grad_lam_im_f': 'grad_w', 'grad_log_dt_f': 'grad_w', 'grad_c_re_f': 'grad_w', 'grad_c_im_f': 'grad_w', 'grad_lam_re_b': 'grad_w', 'grad_lam_im_b': 'grad_w', 'grad_log_dt_b': 'grad_w', 'grad_c_re_b': 'grad_w', 'grad_c_im_b': 'grad_w', 'grad_b_re': 'grad_w', 'grad_b_im': 'grad_w', 'grad_d_skip': 'grad_w', 'grad_w_glu': 'grad_w', 'grad_w_out': 'grad_w', 'grad_w_up': 'grad_w', 'grad_conv_w': 'grad_w', 'grad_conv_b': 'grad_w', 'grad_w_down': 'grad_w', 'delta_c_ctx': 'delta_w', 'delta_w_mod': 'delta_w', 'delta_b_mod': 'delta_w', 'delta_norm1_g': 'delta_w', 'delta_norm2_g': 'delta_w', 'delta_w_in': 'delta_w', 'delta_q_a_g': 'delta_w', 'delta_w_uq': 'delta_w', 'delta_kv_a_g': 'delta_w', 'delta_w_ukv': 'delta_w', 'delta_q_norm_g': 'delta_w', 'delta_k_norm_g': 'delta_w', 'delta_w_o_attn': 'delta_w', 'delta_lam_re_f': 'delta_w', 'delta_lam_im_f': 'delta_w', 'delta_log_dt_f': 'delta_w', 'delta_c_re_f': 'delta_w', 'delta_c_im_f': 'delta_w', 'delta_lam_re_b': 'delta_w', 'delta_lam_im_b': 'delta_w', 'delta_log_dt_b': 'delta_w', 'delta_c_re_b': 'delta_w', 'delta_c_im_b': 'delta_w', 'delta_b_re': 'delta_w', 'delta_b_im': 'delta_w', 'delta_d_skip': 'delta_w', 'delta_w_glu': 'delta_w', 'delta_w_out': 'delta_w', 'delta_w_up': 'delta_w', 'delta_conv_w': 'delta_w', 'delta_conv_b': 'delta_w', 'delta_w_down': 'delta_w', 'new_m_c_ctx': 'new_m', 'new_m_w_mod': 'new_m', 'new_m_b_mod': 'new_m', 'new_m_norm1_g': 'new_m', 'new_m_norm2_g': 'new_m', 'new_m_w_in': 'new_m', 'new_m_q_a_g': 'new_m', 'new_m_w_uq': 'new_m', 'new_m_kv_a_g': 'new_m', 'new_m_w_ukv': 'new_m', 'new_m_q_norm_g': 'new_m', 'new_m_k_norm_g': 'new_m', 'new_m_w_o_attn': 'new_m', 'new_m_lam_re_f': 'new_m', 'new_m_lam_im_f': 'new_m', 'new_m_log_dt_f': 'new_m', 'new_m_c_re_f': 'new_m', 'new_m_c_im_f': 'new_m', 'new_m_lam_re_b': 'new_m', 'new_m_lam_im_b': 'new_m', 'new_m_log_dt_b': 'new_m', 'new_m_c_re_b': 'new_m', 'new_m_c_im_b': 'new_m', 'new_m_b_re': 'new_m', 'new_m_b_im': 'new_m', 'new_m_d_skip': 'new_m', 'new_m_w_glu': 'new_m', 'new_m_w_out': 'new_m', 'new_m_w_up': 'new_m', 'new_m_conv_w': 'new_m', 'new_m_conv_b': 'new_m', 'new_m_w_down': 'new_m', 'new_v_c_ctx': 'new_v', 'new_v_w_mod': 'new_v', 'new_v_b_mod': 'new_v', 'new_v_norm1_g': 'new_v', 'new_v_norm2_g': 'new_v', 'new_v_w_in': 'new_v', 'new_v_q_a_g': 'new_v', 'new_v_w_uq': 'new_v', 'new_v_kv_a_g': 'new_v', 'new_v_w_ukv': 'new_v', 'new_v_q_norm_g': 'new_v', 'new_v_k_norm_g': 'new_v', 'new_v_w_o_attn': 'new_v', 'new_v_lam_re_f': 'new_v', 'new_v_lam_im_f': 'new_v', 'new_v_log_dt_f': 'new_v', 'new_v_c_re_f': 'new_v', 'new_v_c_im_f': 'new_v', 'new_v_lam_re_b': 'new_v', 'new_v_lam_im_b': 'new_v', 'new_v_log_dt_b': 'new_v', 'new_v_c_re_b': 'new_v', 'new_v_c_im_b': 'new_v', 'new_v_b_re': 'new_v', 'new_v_b_im': 'new_v', 'new_v_d_skip': 'new_v', 'new_v_w_glu': 'new_v', 'new_v_w_out': 'new_v', 'new_v_w_up': 'new_v', 'new_v_conv_w': 'new_v', 'new_v_conv_b': 'new_v', 'new_v_w_down': 'new_v'}


def _forward(args):
    return _fwd_reference(*[args[k] for k in FWD_PARAMS])


def _output_shape():
    def fwd():
        inp = _fwd_setup_inputs(0)
        return _fwd_reference(*[inp[k] for k in FWD_PARAMS])
    out = _jax.eval_shape(fwd)
    return out.shape, out.dtype

N_MICROBATCH = 1
ADAM_LR = 0.001
ADAM_B1 = 0.9
ADAM_B2 = 0.999
ADAM_EPS = 1e-08
ADAM_WD = 0.01
ADAM_STEP = 10
PER_EXAMPLE_BATCH_AXIS = {'x': 0, 'c': 0, 'ctx': 0, 'loss_target': 0}
SHARED_INPUTS = []
_WEIGHT_DTYPES = {'c_ctx': _jnp.float32, 'w_mod': _jnp.float32, 'b_mod': _jnp.float32, 'norm1_g': _jnp.float32, 'norm2_g': _jnp.float32, 'w_in': _jnp.float32, 'q_a_g': _jnp.float32, 'w_uq': _jnp.float32, 'kv_a_g': _jnp.float32, 'w_ukv': _jnp.float32, 'q_norm_g': _jnp.float32, 'k_norm_g': _jnp.float32, 'w_o_attn': _jnp.float32, 'lam_re_f': _jnp.float32, 'lam_im_f': _jnp.float32, 'log_dt_f': _jnp.float32, 'c_re_f': _jnp.float32, 'c_im_f': _jnp.float32, 'lam_re_b': _jnp.float32, 'lam_im_b': _jnp.float32, 'log_dt_b': _jnp.float32, 'c_re_b': _jnp.float32, 'c_im_b': _jnp.float32, 'b_re': _jnp.float32, 'b_im': _jnp.float32, 'd_skip': _jnp.float32, 'w_glu': _jnp.float32, 'w_out': _jnp.float32, 'w_up': _jnp.float32, 'conv_w': _jnp.float32, 'conv_b': _jnp.float32, 'w_down': _jnp.float32}
MOMENT_SCALE = {'c_ctx': 2.842966e-02, 'w_mod': 1.253164e+00, 'b_mod': 3.426097e+00, 'norm1_g': 2.267736e-01, 'norm2_g': 6.672185e+00, 'w_in': 7.537437e-02, 'q_a_g': 1.076911e-02, 'w_uq': 7.652699e-03, 'kv_a_g': 4.682219e-01, 'w_ukv': 9.064923e-02, 'q_norm_g': 4.978090e-02, 'k_norm_g': 4.947244e-02, 'w_o_attn': 8.063148e-02, 'lam_re_f': 1.574342e-02, 'lam_im_f': 1.691077e-02, 'log_dt_f': 1.564134e+00, 'c_re_f': 1.304285e-02, 'c_im_f': 1.072775e-02, 'lam_re_b': 1.428856e-02, 'lam_im_b': 9.598527e-03, 'log_dt_b': 9.020406e-01, 'c_re_b': 1.289106e-02, 'c_im_b': 1.120330e-02, 'b_re': 6.848333e-03, 'b_im': 8.165482e-03, 'd_skip': 6.376174e-01, 'w_glu': 8.108971e-02, 'w_out': 1.089344e-01, 'w_up': 1.550854e-01, 'conv_w': 9.876785e-01, 'conv_b': 8.248664e-01, 'w_down': 1.179548e-01}


def _to_microbatches(a, axis):
    t = _jnp.moveaxis(a, axis, 0)
    t = t.reshape((N_MICROBATCH, t.shape[0] // N_MICROBATCH) + t.shape[1:])
    return _jnp.moveaxis(t, 1, axis + 1)


def setup_inputs(seed: int = 0) -> dict:
    inp = _fwd_setup_inputs(seed)
    key = _jax.random.fold_in(_jax.random.key(seed), 7919)
    shape, _ = _output_shape()
    out = dict(inp)
    out["loss_target"] = _jax.random.normal(_jax.random.fold_in(key, 0), shape, _jnp.float32)
    for i, name in enumerate(TWIN_WEIGHTS):
        w = inp[name].astype(_jnp.float32)
        if MOMENT_SCALE is None:
            s = _jnp.sqrt(_jnp.mean(_jnp.square(w)) + 1e-30)
        else:
            s = MOMENT_SCALE[name]
        km, kv = _jax.random.split(_jax.random.fold_in(key, i + 1))
        out[name] = w
        out["m_" + name] = s * _jax.random.normal(km, w.shape, _jnp.float32)
        out["v_" + name] = (s * s) * _jax.random.uniform(kv, w.shape, _jnp.float32, 0.5, 1.5)
    if N_MICROBATCH > 1:
        for name, axis in PER_EXAMPLE_BATCH_AXIS.items():
            out[name] = _to_microbatches(out[name], axis)
    return {'x': out['x'], 'c': out['c'], 'ctx': out['ctx'], 'c_ctx': out['c_ctx'], 'w_mod': out['w_mod'], 'b_mod': out['b_mod'], 'norm1_g': out['norm1_g'], 'norm2_g': out['norm2_g'], 'w_in': out['w_in'], 'q_a_g': out['q_a_g'], 'w_uq': out['w_uq'], 'kv_a_g': out['kv_a_g'], 'w_ukv': out['w_ukv'], 'q_norm_g': out['q_norm_g'], 'k_norm_g': out['k_norm_g'], 'w_o_attn': out['w_o_attn'], 'lam_re_f': out['lam_re_f'], 'lam_im_f': out['lam_im_f'], 'log_dt_f': out['log_dt_f'], 'c_re_f': out['c_re_f'], 'c_im_f': out['c_im_f'], 'lam_re_b': out['lam_re_b'], 'lam_im_b': out['lam_im_b'], 'log_dt_b': out['log_dt_b'], 'c_re_b': out['c_re_b'], 'c_im_b': out['c_im_b'], 'b_re': out['b_re'], 'b_im': out['b_im'], 'd_skip': out['d_skip'], 'w_glu': out['w_glu'], 'w_out': out['w_out'], 'w_up': out['w_up'], 'conv_w': out['conv_w'], 'conv_b': out['conv_b'], 'w_down': out['w_down'], 'loss_target': out['loss_target'], 'm_c_ctx': out['m_c_ctx'], 'm_w_mod': out['m_w_mod'], 'm_b_mod': out['m_b_mod'], 'm_norm1_g': out['m_norm1_g'], 'm_norm2_g': out['m_norm2_g'], 'm_w_in': out['m_w_in'], 'm_q_a_g': out['m_q_a_g'], 'm_w_uq': out['m_w_uq'], 'm_kv_a_g': out['m_kv_a_g'], 'm_w_ukv': out['m_w_ukv'], 'm_q_norm_g': out['m_q_norm_g'], 'm_k_norm_g': out['m_k_norm_g'], 'm_w_o_attn': out['m_w_o_attn'], 'm_lam_re_f': out['m_lam_re_f'], 'm_lam_im_f': out['m_lam_im_f'], 'm_log_dt_f': out['m_log_dt_f'], 'm_c_re_f': out['m_c_re_f'], 'm_c_im_f': out['m_c_im_f'], 'm_lam_re_b': out['m_lam_re_b'], 'm_lam_im_b': out['m_lam_im_b'], 'm_log_dt_b': out['m_log_dt_b'], 'm_c_re_b': out['m_c_re_b'], 'm_c_im_b': out['m_c_im_b'], 'm_b_re': out['m_b_re'], 'm_b_im': out['m_b_im'], 'm_d_skip': out['m_d_skip'], 'm_w_glu': out['m_w_glu'], 'm_w_out': out['m_w_out'], 'm_w_up': out['m_w_up'], 'm_conv_w': out['m_conv_w'], 'm_conv_b': out['m_conv_b'], 'm_w_down': out['m_w_down'], 'v_c_ctx': out['v_c_ctx'], 'v_w_mod': out['v_w_mod'], 'v_b_mod': out['v_b_mod'], 'v_norm1_g': out['v_norm1_g'], 'v_norm2_g': out['v_norm2_g'], 'v_w_in': out['v_w_in'], 'v_q_a_g': out['v_q_a_g'], 'v_w_uq': out['v_w_uq'], 'v_kv_a_g': out['v_kv_a_g'], 'v_w_ukv': out['v_w_ukv'], 'v_q_norm_g': out['v_q_norm_g'], 'v_k_norm_g': out['v_k_norm_g'], 'v_w_o_attn': out['v_w_o_attn'], 'v_lam_re_f': out['v_lam_re_f'], 'v_lam_im_f': out['v_lam_im_f'], 'v_log_dt_f': out['v_log_dt_f'], 'v_c_re_f': out['v_c_re_f'], 'v_c_im_f': out['v_c_im_f'], 'v_lam_re_b': out['v_lam_re_b'], 'v_lam_im_b': out['v_lam_im_b'], 'v_log_dt_b': out['v_log_dt_b'], 'v_c_re_b': out['v_c_re_b'], 'v_c_im_b': out['v_c_im_b'], 'v_b_re': out['v_b_re'], 'v_b_im': out['v_b_im'], 'v_d_skip': out['v_d_skip'], 'v_w_glu': out['v_w_glu'], 'v_w_out': out['v_w_out'], 'v_w_up': out['v_w_up'], 'v_conv_w': out['v_conv_w'], 'v_conv_b': out['v_conv_b'], 'v_w_down': out['v_w_down']}


def _loss(weights, diff, rest, loss_target):
    with _jax.named_scope("forward"):
        args = {**rest, TWIN_DIFF_INPUT: diff, **{k: w.astype(_WEIGHT_DTYPES[k]) for k, w in weights.items()}}
        y = _forward(args)
    with _jax.named_scope("loss_head"):
        err = _jnp.square(y.astype(_jnp.float32) - loss_target)
        return 0.5 * _jnp.sum(_jnp.mean(err, axis=-1)) if err.ndim else 0.5 * err


def _adamw(w, g, m, v):
    m = ADAM_B1 * m + (1.0 - ADAM_B1) * g
    v = ADAM_B2 * v + (1.0 - ADAM_B2) * _jnp.square(g)
    m_hat = m / (1.0 - ADAM_B1 ** ADAM_STEP)
    v_hat = v / (1.0 - ADAM_B2 ** ADAM_STEP)
    delta = -ADAM_LR * (m_hat / (_jnp.sqrt(v_hat) + ADAM_EPS) + ADAM_WD * w)
    return delta, m, v


def reference(x, c, ctx, c_ctx, w_mod, b_mod, norm1_g, norm2_g, w_in, q_a_g, w_uq, kv_a_g, w_ukv, q_norm_g, k_norm_g, w_o_attn, lam_re_f, lam_im_f, log_dt_f, c_re_f, c_im_f, lam_re_b, lam_im_b, log_dt_b, c_re_b, c_im_b, b_re, b_im, d_skip, w_glu, w_out, w_up, conv_w, conv_b, w_down, loss_target, m_c_ctx, m_w_mod, m_b_mod, m_norm1_g, m_norm2_g, m_w_in, m_q_a_g, m_w_uq, m_kv_a_g, m_w_ukv, m_q_norm_g, m_k_norm_g, m_w_o_attn, m_lam_re_f, m_lam_im_f, m_log_dt_f, m_c_re_f, m_c_im_f, m_lam_re_b, m_lam_im_b, m_log_dt_b, m_c_re_b, m_c_im_b, m_b_re, m_b_im, m_d_skip, m_w_glu, m_w_out, m_w_up, m_conv_w, m_conv_b, m_w_down, v_c_ctx, v_w_mod, v_b_mod, v_norm1_g, v_norm2_g, v_w_in, v_q_a_g, v_w_uq, v_kv_a_g, v_w_ukv, v_q_norm_g, v_k_norm_g, v_w_o_attn, v_lam_re_f, v_lam_im_f, v_log_dt_f, v_c_re_f, v_c_im_f, v_lam_re_b, v_lam_im_b, v_log_dt_b, v_c_re_b, v_c_im_b, v_b_re, v_b_im, v_d_skip, v_w_glu, v_w_out, v_w_up, v_conv_w, v_conv_b, v_w_down):
    given = dict(x=x, c=c, ctx=ctx, c_ctx=c_ctx, w_mod=w_mod, b_mod=b_mod, norm1_g=norm1_g, norm2_g=norm2_g, w_in=w_in, q_a_g=q_a_g, w_uq=w_uq, kv_a_g=kv_a_g, w_ukv=w_ukv, q_norm_g=q_norm_g, k_norm_g=k_norm_g, w_o_attn=w_o_attn, lam_re_f=lam_re_f, lam_im_f=lam_im_f, log_dt_f=log_dt_f, c_re_f=c_re_f, c_im_f=c_im_f, lam_re_b=lam_re_b, lam_im_b=lam_im_b, log_dt_b=log_dt_b, c_re_b=c_re_b, c_im_b=c_im_b, b_re=b_re, b_im=b_im, d_skip=d_skip, w_glu=w_glu, w_out=w_out, w_up=w_up, conv_w=conv_w, conv_b=conv_b, w_down=w_down, loss_target=loss_target, m_c_ctx=m_c_ctx, m_w_mod=m_w_mod, m_b_mod=m_b_mod, m_norm1_g=m_norm1_g, m_norm2_g=m_norm2_g, m_w_in=m_w_in, m_q_a_g=m_q_a_g, m_w_uq=m_w_uq, m_kv_a_g=m_kv_a_g, m_w_ukv=m_w_ukv, m_q_norm_g=m_q_norm_g, m_k_norm_g=m_k_norm_g, m_w_o_attn=m_w_o_attn, m_lam_re_f=m_lam_re_f, m_lam_im_f=m_lam_im_f, m_log_dt_f=m_log_dt_f, m_c_re_f=m_c_re_f, m_c_im_f=m_c_im_f, m_lam_re_b=m_lam_re_b, m_lam_im_b=m_lam_im_b, m_log_dt_b=m_log_dt_b, m_c_re_b=m_c_re_b, m_c_im_b=m_c_im_b, m_b_re=m_b_re, m_b_im=m_b_im, m_d_skip=m_d_skip, m_w_glu=m_w_glu, m_w_out=m_w_out, m_w_up=m_w_up, m_conv_w=m_conv_w, m_conv_b=m_conv_b, m_w_down=m_w_down, v_c_ctx=v_c_ctx, v_w_mod=v_w_mod, v_b_mod=v_b_mod, v_norm1_g=v_norm1_g, v_norm2_g=v_norm2_g, v_w_in=v_w_in, v_q_a_g=v_q_a_g, v_w_uq=v_w_uq, v_kv_a_g=v_kv_a_g, v_w_ukv=v_w_ukv, v_q_norm_g=v_q_norm_g, v_k_norm_g=v_k_norm_g, v_w_o_attn=v_w_o_attn, v_lam_re_f=v_lam_re_f, v_lam_im_f=v_lam_im_f, v_log_dt_f=v_log_dt_f, v_c_re_f=v_c_re_f, v_c_im_f=v_c_im_f, v_lam_re_b=v_lam_re_b, v_lam_im_b=v_lam_im_b, v_log_dt_b=v_log_dt_b, v_c_re_b=v_c_re_b, v_c_im_b=v_c_im_b, v_b_re=v_b_re, v_b_im=v_b_im, v_d_skip=v_d_skip, v_w_glu=v_w_glu, v_w_out=v_w_out, v_w_up=v_w_up, v_conv_w=v_conv_w, v_conv_b=v_conv_b, v_w_down=v_w_down)
    weights = {n: given[n] for n in TWIN_WEIGHTS}
    shared = {n: given[n] for n in SHARED_INPUTS}
    per_example = {n: given[n] for n in ['x', 'c', 'ctx']}
    grad_fn = _jax.value_and_grad(_loss, argnums=(0, 1))

    def one_microbatch(ex, loss_target):
        ex = dict(ex)
        diff = ex.pop(TWIN_DIFF_INPUT)
        return grad_fn(weights, diff, {**shared, **ex}, loss_target)

    if N_MICROBATCH == 1:
        loss, (grad_w, grad_x) = one_microbatch(per_example, given["loss_target"])
    else:
        def body(carry, xs):
            loss_sum, grad_sum = carry
            l_k, (gw_k, gx_k) = one_microbatch(xs[0], xs[1])
            with _jax.named_scope("update"):
                return (loss_sum + l_k, _jax.tree.map(_jnp.add, grad_sum, gw_k)), gx_k

        init = (_jnp.zeros((), _jnp.float32), _jax.tree.map(_jnp.zeros_like, weights))
        (loss, grad_w), grad_x = _jax.lax.scan(body, init, (per_example, given["loss_target"]))
    with _jax.named_scope("update"):
        delta_w, new_m, new_v = {}, {}, {}
        for n in TWIN_WEIGHTS:
            delta_w[n], new_m[n], new_v[n] = _adamw(weights[n], grad_w[n], given["m_" + n], given["v_" + n])
    return (loss, grad_x, *[grad_w[n] for n in TWIN_WEIGHTS], *[delta_w[n] for n in TWIN_WEIGHTS],
            *[new_m[n] for n in TWIN_WEIGHTS], *[new_v[n] for n in TWIN_WEIGHTS])
```

```python
import functools
import math

import jax
import jax.numpy as jnp
from jax import lax
from jax.experimental import pallas as pl
from jax.experimental.pallas import tpu as pltpu

F32 = jnp.float32
BF16 = jnp.bfloat16

N_DEV = 8
MESH_AXES = ("x", "y", "c")
N_HEADS = 8
QK_NOPE = 64
QK_ROPE = 32
QK_DIM = QK_NOPE + QK_ROPE
V_DIM = 64
HEAD_PAD = 128
HEADS_W = N_HEADS * HEAD_PAD
GRID_W = 64
ROPE_THETA = 10000.0
SSM_GROUP = 16
SSM_STATE = 64
EPS = 1e-6
LANES = 128
SUBLANES = 8
PACK_W = 1024
VMEM_LIMIT = 56 * 1024 * 1024
MM_TILES = (1024, 768, 1408, 512, 384, 256, 128)

ADAM_LR = 0.001
ADAM_B1 = 0.9
ADAM_B2 = 0.999
ADAM_EPS = 1e-08
ADAM_WD = 0.01
ADAM_STEP = 10

WEIGHT_NAMES = ['c_ctx', 'w_mod', 'b_mod', 'norm1_g', 'norm2_g', 'w_in', 'q_a_g', 'w_uq', 'kv_a_g', 'w_ukv',
                'q_norm_g', 'k_norm_g', 'w_o_attn', 'lam_re_f', 'lam_im_f', 'log_dt_f', 'c_re_f', 'c_im_f',
                'lam_re_b', 'lam_im_b', 'log_dt_b', 'c_re_b', 'c_im_b', 'b_re', 'b_im', 'd_skip', 'w_glu',
                'w_out', 'w_up', 'conv_w', 'conv_b', 'w_down']
GATHERED = {'w_in': 1, 'w_uq': 1, 'w_ukv': 1, 'w_o_attn': 1, 'w_glu': 1, 'w_out': 0, 'w_up': 1, 'conv_w': 1,
            'w_down': 0}
REPLICATED = [n for n in WEIGHT_NAMES if n not in GATHERED and n != 'w_mod']


def _tile(n, prefs):
    for t in prefs:
        if n % t == 0:
            return t
    return n


def _cparams(sem=None):
    return pltpu.CompilerParams(dimension_semantics=sem, vmem_limit_bytes=VMEM_LIMIT)


@jax.custom_vjp
def bdot(a, w):
    return jnp.dot(a.astype(BF16), w.astype(BF16), preferred_element_type=F32)


def _bdot_fwd(a, w):
    return bdot(a, w), (a, w)


def _bdot_bwd(res, g):
    a, w = res
    gb = g.astype(BF16)
    da = lax.dot_general(gb, w.astype(BF16), (((1,), (1,)), ((), ())), preferred_element_type=F32)
    dw = lax.dot_general(a.astype(BF16), gb, (((0,), (0,)), ((), ())), preferred_element_type=F32)
    return da.astype(a.dtype), dw.astype(w.dtype)


bdot.defvjp(_bdot_fwd, _bdot_bwd)


def _dot_nt(a, b):
    return lax.dot_general(a, b, (((1,), (1,)), ((), ())), preferred_element_type=F32)


def _dot_tn(a, b):
    return lax.dot_general(a, b, (((0,), (0,)), ((), ())), preferred_element_type=F32)


def matmul(a, b, mode, out_dtype, name):
    if mode == 'nn':
        (m, k), n = a.shape, b.shape[1]
    elif mode == 'nt':
        (m, k), n = a.shape, b.shape[0]
    else:
        (k, m), n = a.shape, b.shape[1]
    tm = _tile(m, MM_TILES)
    tn = _tile(n, MM_TILES)
    tk = _tile(k, MM_TILES)
    nk = k // tk

    def body(a_ref, b_ref, o_ref, acc_ref):
        kk = pl.program_id(2)

        @pl.when(kk == 0)
        def _():
            acc_ref[...] = jnp.zeros_like(acc_ref)

        if mode == 'nn':
            acc_ref[...] += jnp.dot(a_ref[...], b_ref[...], preferred_element_type=F32)
        elif mode == 'nt':
            acc_ref[...] += _dot_nt(a_ref[...], b_ref[...])
        else:
            acc_ref[...] += _dot_tn(a_ref[...], b_ref[...])

        @pl.when(kk == nk - 1)
        def _():
            o_ref[...] = acc_ref[...].astype(o_ref.dtype)

    if mode == 'nn':
        a_spec = pl.BlockSpec((tm, tk), lambda i, j, kk: (i, kk))
        b_spec = pl.BlockSpec((tk, tn), lambda i, j, kk: (kk, j))
    elif mode == 'nt':
        a_spec = pl.BlockSpec((tm, tk), lambda i, j, kk: (i, kk))
        b_spec = pl.BlockSpec((tn, tk), lambda i, j, kk: (j, kk))
    else:
        a_spec = pl.BlockSpec((tk, tm), lambda i, j, kk: (kk, i))
        b_spec = pl.BlockSpec((tk, tn), lambda i, j, kk: (kk, j))
    return pl.pallas_call(
        body, name=name, grid=(m // tm, n // tn, nk),
        in_specs=[a_spec, b_spec],
        out_specs=pl.BlockSpec((tm, tn), lambda i, j, kk: (i, j)),
        out_shape=jax.ShapeDtypeStruct((m, n), out_dtype),
        scratch_shapes=[pltpu.VMEM((tm, tn), F32)],
        compiler_params=_cparams(("parallel", "parallel", "arbitrary")),
    )(a, b)


def _me_and_peers():
    x, y, c = lax.axis_index("x"), lax.axis_index("y"), lax.axis_index("c")
    me = 4 * x + 2 * y + c
    peers = []
    for r in range(1, N_DEV):
        px = 1 - x if r & 4 else x
        py = 1 - y if r & 2 else y
        pc = 1 - c if r & 1 else c
        peers.append(((px, py, pc), 4 * px + 2 * py + pc))
    return me, peers


def exchange(arrays, kinds, name):
    n_arr = len(arrays)

    def body(*refs):
        srcs, dsts = refs[:n_arr], refs[n_arr:2 * n_arr]
        send_sems, recv_sems, local_sems = refs[2 * n_arr:]
        me, peers = _me_and_peers()
        started = []
        for a in range(n_arr):
            gather = kinds[a] == 'gather'
            mine = srcs[a] if gather else srcs[a].at[me]
            local = pltpu.make_async_copy(mine, dsts[a].at[me], local_sems.at[a])
            local.start()
            started.append(local)
            for r, (peer, peer_idx) in enumerate(peers, start=1):
                k = a * N_DEV + r
                pltpu.make_async_remote_copy(
                    src_ref=srcs[a] if gather else srcs[a].at[peer_idx], dst_ref=dsts[a].at[me],
                    send_sem=send_sems.at[k], recv_sem=recv_sems.at[k],
                    device_id=peer, device_id_type=pl.DeviceIdType.MESH).start()
        for a in range(n_arr):
            gather = kinds[a] == 'gather'
            for r, (peer, peer_idx) in enumerate(peers, start=1):
                k = a * N_DEV + r
                cp = pltpu.make_async_remote_copy(
                    src_ref=srcs[a] if gather else srcs[a].at[peer_idx], dst_ref=dsts[a].at[peer_idx],
                    send_sem=send_sems.at[k], recv_sem=recv_sems.at[k],
                    device_id=peer, device_id_type=pl.DeviceIdType.MESH)
                cp.wait_send()
                cp.wait_recv()
        for local in started:
            local.wait()

    out_shape = []
    for a, arr in enumerate(arrays):
        shp = (N_DEV,) + tuple(arr.shape) if kinds[a] == 'gather' else tuple(arr.shape)
        out_shape.append(jax.ShapeDtypeStruct(shp, arr.dtype))
    any_spec = pl.BlockSpec(memory_space=pl.ANY)
    return pl.pallas_call(
        body, name=name,
        in_specs=[any_spec] * n_arr, out_specs=[any_spec] * n_arr, out_shape=out_shape,
        scratch_shapes=[pltpu.SemaphoreType.DMA((n_arr * N_DEV,)), pltpu.SemaphoreType.DMA((n_arr * N_DEV,)),
                        pltpu.SemaphoreType.DMA((n_arr,))],
        compiler_params=pltpu.CompilerParams(has_side_effects=True),
    )(*arrays)


def _pack(parts, row_mult):
    rows, spans, r = [], [], 0
    for p in parts:
        flat = p.reshape(-1)
        nr = -(-flat.shape[0] // (PACK_W * row_mult)) * row_mult
        flat = jnp.pad(flat, (0, nr * PACK_W - flat.shape[0]))
        rows.append(flat.reshape(nr, PACK_W))
        spans.append((r, nr))
        r += nr
    return jnp.concatenate(rows, axis=0), spans


def _unpack(buf, spans, shapes):
    out = []
    for (r, nr), shp in zip(spans, shapes):
        size = math.prod(shp)
        out.append(buf[..., r:r + nr, :].reshape(buf.shape[:-2] + (nr * PACK_W,))[..., :size]
                   .reshape(buf.shape[:-2] + tuple(shp)))
    return out


def _to_shards(full, axis):
    r, c = full.shape
    if axis == 0:
        return full.reshape(N_DEV, r // N_DEV, c)
    return full.reshape(r, N_DEV, c // N_DEV).transpose(1, 0, 2)


def _from_shards(sh, axis):
    _, r, c = sh.shape
    if axis == 0:
        return sh.reshape(N_DEV * r, c)
    return sh.transpose(1, 0, 2).reshape(r, N_DEV * c)


def _rms(x, g, n):
    ms = jnp.sum(x * x, axis=-1, keepdims=True) * (1.0 / n)
    return x * lax.rsqrt(ms + EPS) * g


def _norm_mod_fn(x, g, sc, sh):
    return _rms(x, g, x.shape[-1]) * (1.0 + sc) + sh


@jax.custom_vjp
def _rope(t, ct, s1, s2):
    return t * ct + pltpu.roll(t, 16, 1) * s1 + pltpu.roll(t, HEAD_PAD - 16, 1) * s2


def _rope_fwd(t, ct, s1, s2):
    return _rope(t, ct, s1, s2), (ct, s1, s2)


def _rope_bwd(res, d):
    ct, s1, s2 = res
    dt = d * ct + pltpu.roll(d * s1, HEAD_PAD - 16, 1) + pltpu.roll(d * s2, 16, 1)
    return dt, jnp.zeros_like(ct), jnp.zeros_like(s1), jnp.zeros_like(s2)


_rope.defvjp(_rope_fwd, _rope_bwd)


def _qkv_fn(cq, ckv, krsec, qag, kvag, qng, kng, wuq, wk, wv, ct, s1, s2):
    q_raw = bdot(_rms(cq, qag, cq.shape[-1]), wuq)
    ckvn = _rms(ckv, kvag, ckv.shape[-1])
    k_raw = bdot(ckvn, wk)
    v = bdot(ckvn, wv)
    qs, ks = [], []
    for h in range(N_HEADS):
        sl = slice(h * HEAD_PAD, (h + 1) * HEAD_PAD)
        qs.append(_rope(_rms(q_raw[:, sl], qng, QK_DIM), ct, s1, s2))
        ks.append(_rope(_rms(k_raw[:, sl] + krsec, kng, QK_DIM), ct, s1, s2))
    return jnp.concatenate(qs, axis=1), jnp.concatenate(ks, axis=1), v


def _glu_fn(u, yf, yb, dskip, wglu):
    y = u * dskip + yf + yb
    vg = bdot(jax.nn.gelu(y), wglu)
    d = vg.shape[-1] // 2
    return vg[:, :d] * jax.nn.sigmoid(vg[:, d:])


def _merge_fn(o, s_l, gl, x, g1, wo, wout):
    d = x.shape[-1]
    a = bdot(o, wo)
    mix = jax.nn.sigmoid(gl[:, :d]) * a + jax.nn.sigmoid(gl[:, d:]) * s_l
    return x + g1 * bdot(mix, wout)


def _mod_fn(cmat, w):
    return bdot(jax.nn.silu(cmat), w)


def _ssm_prep_fn(lam_re, lam_im, logdt, bre, bim):
    dt = jnp.exp(logdt)
    ar, ai = lam_re * dt, lam_im * dt
    e = jnp.exp(ar)
    lbr, lbi = e * jnp.cos(ai), e * jnp.sin(ai)
    nr, ni = lbr - 1.0, lbi
    den = lam_re * lam_re + lam_im * lam_im
    qr = (nr * lam_re + ni * lam_im) / den
    qi = (ni * lam_re - nr * lam_im) / den
    return lbr, lbi, qr * bre - qi * bim, qr * bim + qi * bre


def _rows(tm, w, col=0):
    return pl.BlockSpec((tm, w), lambda i: (i, col))


def _full(shape):
    nd = len(shape)
    return pl.BlockSpec(tuple(shape), lambda i: (0,) * nd)


def _acc_add(i, ref, val):
    @pl.when(i == 0)
    def _():
        ref[...] = jnp.zeros_like(ref)
    ref[...] += val


def _pad_rows(v, rows=SUBLANES):
    sel = lax.broadcasted_iota(jnp.int32, (rows, v.shape[-1]), 0) == 0
    return jnp.where(sel, jnp.broadcast_to(v, (rows, v.shape[-1])), 0.0)


def norm_mod_fwd(xa, g, mods, n_lat, tm, name):
    r, d = xa.shape
    lat_tiles = n_lat // tm

    def body(x_ref, g_ref, m_ref, o_ref):
        lat = pl.program_id(0) < lat_tiles
        sc = jnp.where(lat, m_ref[0:1, :], m_ref[2:3, :])
        sh = jnp.where(lat, m_ref[1:2, :], m_ref[3:4, :])
        o_ref[...] = _norm_mod_fn(x_ref[...], g_ref[...], sc, sh).astype(o_ref.dtype)

    return pl.pallas_call(
        body, name=name, grid=(r // tm,),
        in_specs=[_rows(tm, d), _full(g.shape), _full(mods.shape)],
        out_specs=_rows(tm, d), out_shape=jax.ShapeDtypeStruct((r, d), BF16),
        compiler_params=_cparams(("parallel",)),
    )(xa, g, mods)


def norm_mod_bwd(xa, dh, dres, g, mods, n_lat, tm, name):
    r, d = xa.shape
    lat_tiles = n_lat // tm

    def body(x_ref, dh_ref, dres_ref, g_ref, m_ref, dx_ref, dg_ref, dm_ref):
        i = pl.program_id(0)
        lat = i < lat_tiles
        sc = jnp.where(lat, m_ref[0:1, :], m_ref[2:3, :])
        sh = jnp.where(lat, m_ref[1:2, :], m_ref[3:4, :])
        _, vjp = jax.vjp(_norm_mod_fn, x_ref[...], g_ref[...], sc, sh)
        dx, dg, dsc, dsh = vjp(dh_ref[...])
        dx_ref[...] = dx + jnp.where(lat, dres_ref[...], 0.0)
        _acc_add(i, dg_ref, _pad_rows(dg))
        row = lax.broadcasted_iota(jnp.int32, (SUBLANES, d), 0)
        base = jnp.where(lat, 0, 2)
        upd = jnp.where(row == base, jnp.broadcast_to(dsc, (SUBLANES, d)), 0.0)
        upd = upd + jnp.where(row == base + 1, jnp.broadcast_to(dsh, (SUBLANES, d)), 0.0)
        _acc_add(i, dm_ref, upd)

    return pl.pallas_call(
        body, name=name, grid=(r // tm,),
        in_specs=[_rows(tm, d), _rows(tm, d),
                  pl.BlockSpec((tm, d), lambda i: (jnp.minimum(i, lat_tiles - 1), 0)),
                  _full(g.shape), _full(mods.shape)],
        out_specs=[_rows(tm, d), _full((SUBLANES, d)), _full((SUBLANES, d))],
        out_shape=[jax.ShapeDtypeStruct((r, d), F32), jax.ShapeDtypeStruct((SUBLANES, d), F32),
                   jax.ShapeDtypeStruct((SUBLANES, d), F32)],
        compiler_params=_cparams(("arbitrary",)),
    )(xa, dh, dres, g, mods)


def qkv_fwd(proj, lay, gains, wuq, wk, wv, tabs, tm, name):
    r = proj.shape[0]
    q_w, kv_w = lay['q'], lay['kv']

    def body(cq_ref, ckv_ref, kr_ref, qag, kvag, qng, kng, wuq_ref, wk_ref, wv_ref, ct, s1, s2, q_ref, k_ref,
             v_ref):
        q, k, v = _qkv_fn(cq_ref[...], ckv_ref[...], kr_ref[...], qag[...], kvag[...], qng[...], kng[...],
                          wuq_ref[...], wk_ref[...], wv_ref[...], ct[...], s1[...], s2[...])
        q_ref[...] = q.astype(BF16)
        k_ref[...] = k.astype(BF16)
        v_ref[...] = v.astype(BF16)

    out = jax.ShapeDtypeStruct((r, HEADS_W), BF16)
    return pl.pallas_call(
        body, name=name, grid=(r // tm,),
        in_specs=[_rows(tm, q_w, lay['o_cq'] // q_w), _rows(tm, kv_w, lay['o_ckv'] // kv_w),
                  _rows(tm, LANES, lay['o_kr'] // LANES)]
        + [_full(a.shape) for a in gains] + [_full(wuq.shape), _full(wk.shape), _full(wv.shape)]
        + [_rows(tm, HEAD_PAD)] * 3,
        out_specs=[_rows(tm, HEADS_W)] * 3, out_shape=[out, out, out],
        compiler_params=_cparams(("parallel",)),
    )(proj, proj, proj, *gains, wuq, wk, wv, *tabs)


def qkv_bwd(proj, lay, gains, wuq, wk, wv, tabs, dq, dk, dv, tm, name):
    r = proj.shape[0]
    q_w, kv_w = lay['q'], lay['kv']

    def body(cq_ref, ckv_ref, kr_ref, qag, kvag, qng, kng, wuq_ref, wk_ref, wv_ref, ct, s1, s2, dq_ref, dk_ref,
             dv_ref, dcq_ref, dckv_ref, dkr_ref, dqag, dkvag, dqng, dkng, dwuq, dwk, dwv):
        i = pl.program_id(0)
        tables = (ct[...], s1[...], s2[...])
        fn = lambda *a: _qkv_fn(*a, *tables)
        _, vjp = jax.vjp(fn, cq_ref[...], ckv_ref[...], kr_ref[...], qag[...], kvag[...], qng[...], kng[...],
                         wuq_ref[...].astype(F32), wk_ref[...].astype(F32), wv_ref[...].astype(F32))
        g = vjp((dq_ref[...], dk_ref[...], dv_ref[...]))
        dcq_ref[...] = g[0]
        dckv_ref[...] = g[1]
        dkr_ref[...] = g[2]
        for ref, val in zip((dqag, dkvag, dqng, dkng), g[3:7]):
            _acc_add(i, ref, _pad_rows(val))
        for ref, val in zip((dwuq, dwk, dwv), g[7:10]):
            _acc_add(i, ref, val)

    acc_shapes = [(SUBLANES, a.shape[1]) for a in gains] + [wuq.shape, wk.shape, wv.shape]
    return pl.pallas_call(
        body, name=name, grid=(r // tm,),
        in_specs=[_rows(tm, q_w, lay['o_cq'] // q_w), _rows(tm, kv_w, lay['o_ckv'] // kv_w),
                  _rows(tm, LANES, lay['o_kr'] // LANES)]
        + [_full(a.shape) for a in gains] + [_full(wuq.shape), _full(wk.shape), _full(wv.shape)]
        + [_rows(tm, HEAD_PAD)] * 3 + [_rows(tm, HEADS_W)] * 3,
        out_specs=[_rows(tm, q_w), _rows(tm, kv_w), _rows(tm, LANES)] + [_full(s) for s in acc_shapes],
        out_shape=[jax.ShapeDtypeStruct((r, q_w), F32), jax.ShapeDtypeStruct((r, kv_w), F32),
                   jax.ShapeDtypeStruct((r, LANES), F32)] + [jax.ShapeDtypeStruct(s, F32) for s in acc_shapes],
        compiler_params=_cparams(("arbitrary",)),
    )(proj, proj, proj, *gains, wuq, wk, wv, *tabs, dq, dk, dv)


def glu_fwd(proj, lay, yf, yb, dskip, wglu, n, tm, name):
    sw, d = wglu.shape[0], wglu.shape[1] // 2

    def body(u_ref, yf_ref, yb_ref, ds_ref, w_ref, o_ref):
        o_ref[...] = _glu_fn(u_ref[...], yf_ref[...], yb_ref[...], ds_ref[...], w_ref[...])

    return pl.pallas_call(
        body, name=name, grid=(n // tm,),
        in_specs=[_rows(tm, sw, lay['o_u'] // sw), _rows(tm, sw), _rows(tm, sw), _full(dskip.shape),
                  _full(wglu.shape)],
        out_specs=_rows(tm, d), out_shape=jax.ShapeDtypeStruct((n, d), F32),
        compiler_params=_cparams(("parallel",)),
    )(proj, yf, yb, dskip, wglu)


def glu_bwd(proj, lay, yf, yb, dskip, wglu, ds_l, n, tm, name):
    sw, d = wglu.shape[0], wglu.shape[1] // 2

    def body(u_ref, yf_ref, yb_ref, ds_ref, w_ref, g_ref, du_ref, dy_ref, dds_ref, dw_ref):
        i = pl.program_id(0)
        _, vjp = jax.vjp(_glu_fn, u_ref[...], yf_ref[...], yb_ref[...], ds_ref[...], w_ref[...].astype(F32))
        du, dyf, _, dds, dw = vjp(g_ref[...])
        du_ref[...] = du
        dy_ref[...] = dyf
        _acc_add(i, dds_ref, _pad_rows(dds))
        _acc_add(i, dw_ref, dw)

    return pl.pallas_call(
        body, name=name, grid=(n // tm,),
        in_specs=[_rows(tm, sw, lay['o_u'] // sw), _rows(tm, sw), _rows(tm, sw), _full(dskip.shape),
                  _full(wglu.shape), _rows(tm, d)],
        out_specs=[_rows(tm, sw), _rows(tm, sw), _full((SUBLANES, sw)), _full(wglu.shape)],
        out_shape=[jax.ShapeDtypeStruct((n, sw), F32), jax.ShapeDtypeStruct((n, sw), F32),
                   jax.ShapeDtypeStruct((SUBLANES, sw), F32), jax.ShapeDtypeStruct(wglu.shape, F32)],
        compiler_params=_cparams(("arbitrary",)),
    )(proj, yf, yb, dskip, wglu, ds_l)


def merge_fwd(o, s_l, proj, xa, g1, wo, wout, n, tm, name):
    d = xa.shape[1]

    def body(o_ref, s_ref, gl_ref, x_ref, g1_ref, wo_ref, wout_ref, x1_ref):
        x1_ref[...] = _merge_fn(o_ref[...], s_ref[...], gl_ref[...], x_ref[...], g1_ref[...], wo_ref[...],
                                wout_ref[...])

    return pl.pallas_call(
        body, name=name, grid=(n // tm,),
        in_specs=[_rows(tm, HEADS_W), _rows(tm, d), _rows(tm, 2 * d), _rows(tm, d), _full(g1.shape),
                  _full(wo.shape), _full(wout.shape)],
        out_specs=_rows(tm, d), out_shape=jax.ShapeDtypeStruct((n, d), F32),
        compiler_params=_cparams(("parallel",)),
    )(o, s_l, proj, xa, g1, wo, wout)


def merge_bwd(o, s_l, proj, xa, g1, wo, wout, dx1, n, tm, name):
    d = xa.shape[1]

    def body(o_ref, s_ref, gl_ref, x_ref, g1_ref, wo_ref, wout_ref, dx1_ref, do_ref, ds_ref, dgl_ref, dg1_ref,
             dwo_ref, dwout_ref):
        i = pl.program_id(0)
        _, vjp = jax.vjp(_merge_fn, o_ref[...], s_ref[...], gl_ref[...], x_ref[...], g1_ref[...],
                         wo_ref[...].astype(F32), wout_ref[...].astype(F32))
        do, ds, dgl, _, dg1, dwo, dwout = vjp(dx1_ref[...])
        do_ref[...] = do
        ds_ref[...] = ds
        dgl_ref[...] = dgl
        _acc_add(i, dg1_ref, _pad_rows(dg1))
        _acc_add(i, dwo_ref, dwo)
        _acc_add(i, dwout_ref, dwout)

    return pl.pallas_call(
        body, name=name, grid=(n // tm,),
        in_specs=[_rows(tm, HEADS_W), _rows(tm, d), _rows(tm, 2 * d), _rows(tm, d), _full(g1.shape),
                  _full(wo.shape), _full(wout.shape), _rows(tm, d)],
        out_specs=[_rows(tm, HEADS_W), _rows(tm, d), _rows(tm, 2 * d), _full((SUBLANES, d)), _full(wo.shape),
                   _full(wout.shape)],
        out_shape=[jax.ShapeDtypeStruct((n, HEADS_W), BF16), jax.ShapeDtypeStruct((n, d), F32),
                   jax.ShapeDtypeStruct((n, 2 * d), F32), jax.ShapeDtypeStruct((SUBLANES, d), F32),
                   jax.ShapeDtypeStruct(wo.shape, F32), jax.ShapeDtypeStruct(wout.shape, F32)],
        compiler_params=_cparams(("arbitrary",)),
    )(o, s_l, proj, xa, g1, wo, wout, dx1)


def _halo_specs(tm, w, n):
    nb = n // SUBLANES
    per = tm // SUBLANES
    prev = pl.BlockSpec((SUBLANES, w), lambda i: (jnp.maximum(i * per - 1, 0), 0))
    nxt = pl.BlockSpec((SUBLANES, w), lambda i: (jnp.minimum((i + 1) * per, nb - 1), 0))
    return prev, nxt


def _shifted(t, prev_blk, next_blk, i, n_tiles):
    tm = t.shape[0]
    row = lax.broadcasted_iota(jnp.int32, t.shape, 0)
    prev_row = jnp.where(i > 0, prev_blk[SUBLANES - 1:SUBLANES, :], 0.0)
    next_row = jnp.where(i < n_tiles - 1, next_blk[0:1, :], 0.0)
    before = jnp.where(row == 0, prev_row, pltpu.roll(t, 1, 0))
    after = jnp.where(row == tm - 1, next_row, pltpu.roll(t, tm - 1, 0))
    return before, after


def _conv_u2(up, before, after, cw, cb):
    return before * cw[0:1, :] + up * cw[1:2, :] + after * cw[2:3, :] + cb


def conv_act_fwd(up, cw, cb, tm, name):
    n, w2 = up.shape
    f = w2 // 2
    n_tiles = n // tm
    prev, nxt = _halo_specs(tm, w2, n)

    def body(up_ref, prev_ref, next_ref, cw_ref, cb_ref, act_ref):
        i = pl.program_id(0)
        t = up_ref[...]
        before, after = _shifted(t, prev_ref[...], next_ref[...], i, n_tiles)
        u2 = _conv_u2(t, before, after, cw_ref[...], cb_ref[...])
        act_ref[...] = (jax.nn.silu(u2[:, f:]) * u2[:, :f]).astype(BF16)

    return pl.pallas_call(
        body, name=name, grid=(n_tiles,),
        in_specs=[_rows(tm, w2), prev, nxt, _full(cw.shape), _full(cb.shape)],
        out_specs=_rows(tm, f), out_shape=jax.ShapeDtypeStruct((n, f), BF16),
        compiler_params=_cparams(("parallel",)),
    )(up, up, up, cw, cb)


def down_loss(act, wdown, x1, g2, target, tm, name):
    n, d = x1.shape
    f = act.shape[1]

    def body(act_ref, w_ref, x1_ref, g2_ref, t_ref, dy_ref, ffn_ref, loss_ref):
        ffn = jnp.dot(act_ref[...], w_ref[...], preferred_element_type=F32)
        err = x1_ref[...] + g2_ref[...] * ffn - t_ref[...]
        ffn_ref[...] = ffn
        dy_ref[...] = err * (1.0 / d)
        part = 0.5 * jnp.sum(jnp.sum(err * err, axis=-1, keepdims=True) * (1.0 / d), axis=0, keepdims=True)
        loss_ref[0] = jnp.broadcast_to(part, (SUBLANES, LANES))

    return pl.pallas_call(
        body, name=name, grid=(n // tm,),
        in_specs=[_rows(tm, f), _full(wdown.shape), _rows(tm, d), _full(g2.shape), _rows(tm, d)],
        out_specs=[_rows(tm, d), _rows(tm, d), pl.BlockSpec((1, SUBLANES, LANES), lambda i: (i, 0, 0))],
        out_shape=[jax.ShapeDtypeStruct((n, d), F32), jax.ShapeDtypeStruct((n, d), F32),
                   jax.ShapeDtypeStruct((n // tm, SUBLANES, LANES), F32)],
        compiler_params=_cparams(("parallel",)),
    )(act, wdown, x1, g2, target)


def down_bwd(dy, ffn, up, cw, cb, wdown, g2, tm, name):
    n, d = dy.shape
    w2 = up.shape[1]
    f = w2 // 2
    n_tiles = n // tm
    prev, nxt = _halo_specs(tm, w2, n)

    def body(dy_ref, ffn_ref, up_ref, prev_ref, next_ref, cw_ref, cb_ref, w_ref, g2_ref, du2_ref, dffn_ref,
             dg2_ref):
        i = pl.program_id(0)
        dyv = dy_ref[...]
        dffn = (dyv * g2_ref[...]).astype(BF16)
        dffn_ref[...] = dffn
        _acc_add(i, dg2_ref, _pad_rows(jnp.sum(dyv * ffn_ref[...], axis=0, keepdims=True)))
        dact = _dot_nt(dffn, w_ref[...])
        t = up_ref[...]
        before, after = _shifted(t, prev_ref[...], next_ref[...], i, n_tiles)
        u2 = _conv_u2(t, before, after, cw_ref[...], cb_ref[...])
        val, gate = u2[:, :f], u2[:, f:]
        sg = jax.nn.sigmoid(gate)
        du2_ref[:, :f] = dact * (gate * sg)
        du2_ref[:, f:] = dact * val * (sg * (1.0 + gate * (1.0 - sg)))

    return pl.pallas_call(
        body, name=name, grid=(n_tiles,),
        in_specs=[_rows(tm, d), _rows(tm, d), _rows(tm, w2), prev, nxt, _full(cw.shape), _full(cb.shape),
                  _full(wdown.shape), _full(g2.shape)],
        out_specs=[_rows(tm, w2), _rows(tm, d), _full((SUBLANES, d))],
        out_shape=[jax.ShapeDtypeStruct((n, w2), F32), jax.ShapeDtypeStruct((n, d), BF16),
                   jax.ShapeDtypeStruct((SUBLANES, d), F32)],
        compiler_params=_cparams(("arbitrary",)),
    )(dy, ffn, up, up, up, cw, cb, wdown, g2)


def conv_bwd(du2, up, cw, tm, name):
    n, w2 = up.shape
    n_tiles = n // tm
    prev, nxt = _halo_specs(tm, w2, n)

    def body(d_ref, prev_ref, next_ref, up_ref, cw_ref, dup_ref, dcw_ref):
        i = pl.program_id(0)
        dv = d_ref[...]
        before, after = _shifted(dv, prev_ref[...], next_ref[...], i, n_tiles)
        cwv = cw_ref[...]
        dup_ref[...] = (after * cwv[0:1, :] + dv * cwv[1:2, :] + before * cwv[2:3, :]).astype(BF16)
        t = up_ref[...]
        row = lax.broadcasted_iota(jnp.int32, (SUBLANES, w2), 0)
        upd = jnp.zeros((SUBLANES, w2), F32)
        for k, term in enumerate((after * t, dv * t, before * t, dv)):
            upd = upd + jnp.where(row == k, jnp.broadcast_to(jnp.sum(term, axis=0, keepdims=True),
                                                             (SUBLANES, w2)), 0.0)
        _acc_add(i, dcw_ref, upd)

    return pl.pallas_call(
        body, name=name, grid=(n_tiles,),
        in_specs=[_rows(tm, w2), prev, nxt, _rows(tm, w2), _full(cw.shape)],
        out_specs=[_rows(tm, w2), _full((SUBLANES, w2))],
        out_shape=[jax.ShapeDtypeStruct((n, w2), BF16), jax.ShapeDtypeStruct((SUBLANES, w2), F32)],
        compiler_params=_cparams(("arbitrary",)),
    )(du2, du2, du2, up, cw)


def attn_fwd(q, k, v, n, name):
    nk = k.shape[0]
    tq = _tile(n, (512, 256, 128))
    tk = _tile(nk, (768, 384, 256, 128))
    n_kv = nk // tk
    scale = QK_DIM ** -0.5

    def body(q_ref, k_ref, v_ref, o_ref, lse_ref, m_ref, l_ref, acc_ref):
        j = pl.program_id(2)

        @pl.when(j == 0)
        def _():
            m_ref[...] = jnp.full_like(m_ref, -jnp.inf)
            l_ref[...] = jnp.zeros_like(l_ref)
            acc_ref[...] = jnp.zeros_like(acc_ref)

        s = _dot_nt(q_ref[...], k_ref[...]) * scale
        m_prev = m_ref[:, 0:1]
        m_new = jnp.maximum(m_prev, jnp.max(s, axis=-1, keepdims=True))
        alpha = jnp.exp(m_prev - m_new)
        p = jnp.exp(s - m_new)
        l_new = alpha * l_ref[:, 0:1] + jnp.sum(p, axis=-1, keepdims=True)
        acc_ref[...] = alpha * acc_ref[...] + jnp.dot(p.astype(BF16), v_ref[...], preferred_element_type=F32)
        m_ref[...] = jnp.broadcast_to(m_new, m_ref.shape)
        l_ref[...] = jnp.broadcast_to(l_new, l_ref.shape)

        @pl.when(j == n_kv - 1)
        def _():
            o_ref[...] = (acc_ref[...] / l_ref[...]).astype(BF16)
            lse_ref[...] = m_ref[...] + jnp.log(l_ref[...])

    qspec = pl.BlockSpec((tq, HEAD_PAD), lambda h, i, j: (i, h))
    kspec = pl.BlockSpec((tk, HEAD_PAD), lambda h, i, j: (j, h))
    return pl.pallas_call(
        body, name=name, grid=(N_HEADS, n // tq, n_kv),
        in_specs=[qspec, kspec, kspec], out_specs=[qspec, qspec],
        out_shape=[jax.ShapeDtypeStruct((n, HEADS_W), BF16), jax.ShapeDtypeStruct((n, HEADS_W), F32)],
        scratch_shapes=[pltpu.VMEM((tq, HEAD_PAD), F32)] * 3,
        compiler_params=_cparams(("parallel", "parallel", "arbitrary")),
    )(q, k, v)


def attn_bwd(q, k, v, o, do, lse, n, name):
    nk = k.shape[0]
    tq = _tile(n, (512, 256, 128))
    tk = _tile(nk, (768, 384, 256, 128))
    scale = QK_DIM ** -0.5

    def body(q_ref, k_ref, v_ref, o_ref, do_ref, lse_ref, dq_ref, dk_ref, dv_ref):
        j, i = pl.program_id(1), pl.program_id(2)

        @pl.when((j == 0) & (i == 0))
        def _():
            dq_ref[...] = jnp.zeros_like(dq_ref)

        @pl.when(i == 0)
        def _():
            dk_ref[...] = jnp.zeros_like(dk_ref)
            dv_ref[...] = jnp.zeros_like(dv_ref)

        qv, kv, dov = q_ref[...], k_ref[...], do_ref[...]
        s = _dot_nt(qv, kv) * scale
        p = jnp.exp(s - lse_ref[:, 0:1])
        dv_ref[...] += _dot_tn(p.astype(BF16), dov)
        dp = _dot_nt(dov, v_ref[...])
        delta = jnp.sum(dov.astype(F32) * o_ref[...].astype(F32), axis=-1, keepdims=True)
        ds = (p * (dp - delta) * scale).astype(BF16)
        dk_ref[...] += _dot_tn(ds, qv)
        rows = pl.ds(pl.multiple_of(i * tq, tq), tq)
        dq_ref[rows, :] += jnp.dot(ds, kv, preferred_element_type=F32)

    qspec = pl.BlockSpec((tq, HEAD_PAD), lambda h, j, i: (i, h))
    kspec = pl.BlockSpec((tk, HEAD_PAD), lambda h, j, i: (j, h))
    return pl.pallas_call(
        body, name=name, grid=(N_HEADS, nk // tk, n // tq),
        in_specs=[qspec, kspec, kspec, qspec, qspec, qspec],
        out_specs=[pl.BlockSpec((n, HEAD_PAD), lambda h, j, i: (0, h)), kspec, kspec],
        out_shape=[jax.ShapeDtypeStruct((n, HEADS_W), F32), jax.ShapeDtypeStruct((nk, HEADS_W), F32),
                   jax.ShapeDtypeStruct((nk, HEADS_W), F32)],
        compiler_params=_cparams(("parallel", "arbitrary", "arbitrary")),
    )(q, k, v, o, do, lse)


SCAN_LEVELS = (1, 2, 4)
SCAN_LANES = 512


def _scan_chunk(xr, xi, car, m_ref, p_ref, reverse):
    t_len, gn = xr.shape
    n_slab = t_len // SUBLANES
    for lb in range(gn // SCAN_LANES):
        ls = pl.ds(lb * SCAN_LANES, SCAN_LANES)

        def step(s, carry, ls=ls):
            cr, ci = carry
            slab = (n_slab - 1 - s) if reverse else s
            rows = pl.ds(pl.multiple_of(slab * SUBLANES, SUBLANES), SUBLANES)
            br, bi = xr[rows, ls], xi[rows, ls]
            for lvl, d in enumerate(SCAN_LEVELS):
                shift = SUBLANES - d if reverse else d
                sr, si = pltpu.roll(br, shift, 0), pltpu.roll(bi, shift, 0)
                mr, mi = m_ref[lvl, 0, :, ls], m_ref[lvl, 1, :, ls]
                br, bi = br + mr * sr - mi * si, bi + mr * si + mi * sr
            pr, pi = p_ref[0, :, ls], p_ref[1, :, ls]
            br, bi = br + pr * cr - pi * ci, bi + pr * ci + pi * cr
            xr[rows, ls] = br
            xi[rows, ls] = bi
            last = 0 if reverse else SUBLANES - 1
            return br[last:last + 1, :], bi[last:last + 1, :]

        cr, ci = lax.fori_loop(0, n_slab, step, (car[0:1, ls], car[1:2, ls]))
        car[0:1, ls] = cr
        car[1:2, ls] = ci


def _dir_spec(shape):
    nd = len(shape)
    return pl.BlockSpec((None,) + tuple(shape), lambda d, k: (d,) + (0,) * nd)


def ssm_fwd(useq, bre, bim, cre, ncim, mtab, ptab, t_len, name):
    _, l, sw = useq.shape
    gn = bre.shape[-1]
    n_chunk = l // t_len

    def body(u_ref, bre_ref, bim_ref, cre_ref, ncim_ref, m_ref, p_ref, y_ref, xb_ref, xr, xi, car):
        @pl.when(pl.program_id(1) == 0)
        def _():
            car[...] = jnp.zeros_like(car)

        xb_ref[...] = car[...]
        u = u_ref[...]
        xr[...] = jnp.dot(u, bre_ref[...], preferred_element_type=F32)
        xi[...] = jnp.dot(u, bim_ref[...], preferred_element_type=F32)
        _scan_chunk(xr, xi, car, m_ref, p_ref, False)
        y_ref[...] = (jnp.dot(xr[...].astype(BF16), cre_ref[...], preferred_element_type=F32)
                      + jnp.dot(xi[...].astype(BF16), ncim_ref[...], preferred_element_type=F32))

    chunk = pl.BlockSpec((None, t_len, sw), lambda d, k: (d, k, 0))
    return pl.pallas_call(
        body, name=name, grid=(2, n_chunk),
        in_specs=[chunk, _dir_spec(bre.shape[1:]), _dir_spec(bim.shape[1:]), _dir_spec(cre.shape[1:]),
                  _dir_spec(ncim.shape[1:]), _dir_spec(mtab.shape[1:]), _dir_spec(ptab.shape[1:])],
        out_specs=[chunk, pl.BlockSpec((None, None, 2, gn), lambda d, k: (d, k, 0, 0))],
        out_shape=[jax.ShapeDtypeStruct((2, l, sw), F32), jax.ShapeDtypeStruct((2, n_chunk, 2, gn), F32)],
        scratch_shapes=[pltpu.VMEM((t_len, gn), F32), pltpu.VMEM((t_len, gn), F32), pltpu.VMEM((2, gn), F32)],
        compiler_params=_cparams(("parallel", "arbitrary")),
    )(useq, bre, bim, cre, ncim, mtab, ptab)


def ssm_bwd(useq, dyseq, xb, bre, bim, cre, ncim, mtab, ptab, mtab_r, ptab_r, t_len, name):
    _, l, sw = useq.shape
    gn = bre.shape[-1]
    n_chunk = l // t_len

    def body(u_ref, dy_ref, xb_ref, bre_ref, bim_ref, cre_ref, ncim_ref, m_ref, p_ref, mr_ref, pr_ref,
             du_ref, gr_ref, gi_ref, xr_ref, xi_ref, dlam_ref, xr, xi, gr, gi, car, acar):
        k = pl.program_id(1)

        @pl.when(k == 0)
        def _():
            acar[...] = jnp.zeros_like(acar)
            dlam_ref[...] = jnp.zeros_like(dlam_ref)

        u, dy = u_ref[...], dy_ref[...]
        xr[...] = jnp.dot(u, bre_ref[...], preferred_element_type=F32)
        xi[...] = jnp.dot(u, bim_ref[...], preferred_element_type=F32)
        car[...] = xb_ref[...]
        _scan_chunk(xr, xi, car, m_ref, p_ref, False)
        gr[...] = _dot_nt(dy, cre_ref[...])
        gi[...] = _dot_nt(dy, ncim_ref[...])
        _scan_chunk(gr, gi, acar, mr_ref, pr_ref, True)
        xrv, xiv, grv, giv = xr[...], xi[...], gr[...], gi[...]
        row = lax.broadcasted_iota(jnp.int32, (t_len, gn), 0)
        xpr = jnp.where(row == 0, xb_ref[0:1, :], pltpu.roll(xrv, 1, 0))
        xpi = jnp.where(row == 0, xb_ref[1:2, :], pltpu.roll(xiv, 1, 0))
        dlr = grv * xpr + giv * xpi
        dli = giv * xpr - grv * xpi
        dlam_ref[0] += jnp.sum(dlr.reshape(t_len // SUBLANES, SUBLANES, gn), axis=0)
        dlam_ref[1] += jnp.sum(dli.reshape(t_len // SUBLANES, SUBLANES, gn), axis=0)
        grb, gib = grv.astype(BF16), giv.astype(BF16)
        du_ref[...] = _dot_nt(grb, bre_ref[...]) + _dot_nt(gib, bim_ref[...])
        gr_ref[...] = grb
        gi_ref[...] = gib
        xr_ref[...] = xrv.astype(BF16)
        xi_ref[...] = xiv.astype(BF16)

    def rev(width):
        return pl.BlockSpec((None, t_len, width), lambda d, k: (d, n_chunk - 1 - k, 0))

    state = jax.ShapeDtypeStruct((2, l, gn), BF16)
    return pl.pallas_call(
        body, name=name, grid=(2, n_chunk),
        in_specs=[rev(sw), rev(sw), pl.BlockSpec((None, None, 2, gn), lambda d, k: (d, n_chunk - 1 - k, 0, 0)),
                  _dir_spec(bre.shape[1:]), _dir_spec(bim.shape[1:]), _dir_spec(cre.shape[1:]),
                  _dir_spec(ncim.shape[1:]), _dir_spec(mtab.shape[1:]), _dir_spec(ptab.shape[1:]),
                  _dir_spec(mtab_r.shape[1:]), _dir_spec(ptab_r.shape[1:])],
        out_specs=[rev(sw), rev(gn), rev(gn), rev(gn), rev(gn), _dir_spec((2, SUBLANES, gn))],
        out_shape=[jax.ShapeDtypeStruct((2, l, sw), F32), state, state, state, state,
                   jax.ShapeDtypeStruct((2, 2, SUBLANES, gn), F32)],
        scratch_shapes=[pltpu.VMEM((t_len, gn), F32)] * 4 + [pltpu.VMEM((2, gn), F32)] * 2,
        compiler_params=_cparams(("parallel", "arbitrary")),
    )(useq, dyseq, xb, bre, bim, cre, ncim, mtab, ptab, mtab_r, ptab_r)


def ssm_prep(lam_re, lam_im, logdt, bre, bim, name):
    gn = lam_re.shape[1]

    def body(lr_ref, li_ref, dt_ref, br_ref, bi_ref, pwr_ref, pwi_ref, bbr_ref, bbi_ref):
        _, _, bbr, bbi = _ssm_prep_fn(lr_ref[...], li_ref[...], dt_ref[...], br_ref[...], bi_ref[...])
        bbr_ref[...] = bbr
        bbi_ref[...] = bbi
        kk = (lax.broadcasted_iota(jnp.int32, (SUBLANES, gn), 0) + 1).astype(F32)
        dt = jnp.exp(dt_ref[...])
        ar, ai = lr_ref[...] * dt * kk, li_ref[...] * dt * kk
        e = jnp.exp(ar)
        pwr_ref[...] = e * jnp.cos(ai)
        pwi_ref[...] = e * jnp.sin(ai)

    ins = (lam_re, lam_im, logdt, bre, bim)
    return pl.pallas_call(
        body, name=name,
        out_shape=[jax.ShapeDtypeStruct((SUBLANES, gn), F32)] * 2 + [jax.ShapeDtypeStruct(bre.shape, F32)] * 2,
        compiler_params=_cparams(),
    )(*ins)


def ssm_prep_bwd(lam_re, lam_im, logdt, bre, bim, dlbr, dlbi, dbbr, dbbi, name):
    def body(lr_ref, li_ref, dt_ref, br_ref, bi_ref, g0, g1, g2, g3, o0, o1, o2, o3, o4):
        _, vjp = jax.vjp(_ssm_prep_fn, lr_ref[...], li_ref[...], dt_ref[...], br_ref[...], bi_ref[...])
        for ref, val in zip((o0, o1, o2, o3, o4), vjp((g0[...], g1[...], g2[...], g3[...]))):
            ref[...] = val

    ins = (lam_re, lam_im, logdt, bre, bim)
    return pl.pallas_call(
        body, name=name,
        out_shape=[jax.ShapeDtypeStruct(a.shape, F32) for a in ins],
        compiler_params=_cparams(),
    )(*ins, dlbr, dlbi, dbbr, dbbi)


def mod_fwd(cmat, w, b, name):
    def body(c_ref, w_ref, b_ref, o_ref):
        o_ref[...] = _mod_fn(c_ref[...], w_ref[...]) + b_ref[...]

    return pl.pallas_call(body, name=name, out_shape=jax.ShapeDtypeStruct((cmat.shape[0], w.shape[1]), F32),
                          compiler_params=_cparams())(cmat, w, b)


def mod_bwd(cmat, w, g_lat, g_ctx, g_all, name):
    def body(c_ref, w_ref, gl_ref, gc_ref, ga_ref, dw_ref, dc_ref, db_ref):
        gc = jnp.sum(gc_ref[...], axis=0, keepdims=True)
        dm = jnp.concatenate([gl_ref[...], _pad_rows(gc)], axis=0)
        _, vjp = jax.vjp(_mod_fn, c_ref[...], w_ref[...])
        dc, dw = vjp(dm)
        dw_ref[...] = dw
        dc_ref[...] = dc
        db_ref[...] = _pad_rows(jnp.sum(ga_ref[...], axis=0, keepdims=True))

    return pl.pallas_call(
        body, name=name,
        out_shape=[jax.ShapeDtypeStruct(w.shape, F32), jax.ShapeDtypeStruct(cmat.shape, F32),
                   jax.ShapeDtypeStruct((SUBLANES, g_all.shape[1]), F32)],
        compiler_params=_cparams(),
    )(cmat, w, g_lat, g_ctx, g_all)


def reduce_adamw(parts, w, m, v, name):
    s, r, _ = parts.shape
    tr = _tile(r, (256, 128, 64, 32, 16, 8))

    def body(p_ref, w_ref, m_ref, v_ref, g_ref, d_ref, nm_ref, nv_ref):
        g = p_ref[0]
        for k in range(1, s):
            g = g + p_ref[k]
        mm = ADAM_B1 * m_ref[...] + (1.0 - ADAM_B1) * g
        vv = ADAM_B2 * v_ref[...] + (1.0 - ADAM_B2) * jnp.square(g)
        m_hat = mm / (1.0 - ADAM_B1 ** ADAM_STEP)
        v_hat = vv / (1.0 - ADAM_B2 ** ADAM_STEP)
        g_ref[...] = g
        d_ref[...] = -ADAM_LR * (m_hat / (jnp.sqrt(v_hat) + ADAM_EPS) + ADAM_WD * w_ref[...])
        nm_ref[...] = mm
        nv_ref[...] = vv

    blk = _rows(tr, PACK_W)
    out = jax.ShapeDtypeStruct((r, PACK_W), F32)
    return pl.pallas_call(
        body, name=name, grid=(r // tr,),
        in_specs=[pl.BlockSpec((s, tr, PACK_W), lambda i: (0, i, 0)), blk, blk, blk],
        out_specs=[blk] * 4, out_shape=[out] * 4,
        compiler_params=_cparams(("parallel",)),
    )(parts, w, m, v)


def _in_layout(d, q, kv, sw):
    o_u = 2 * d
    o_ckv = o_u + sw
    o_kr = o_ckv + kv
    o_cq = -(-(o_kr + LANES) // q) * q
    assert o_u % sw == 0 and o_ckv % kv == 0 and o_kr % LANES == 0
    assert q % LANES == 0 and kv % LANES == 0 and sw % LANES == 0
    return dict(d=d, q=q, kv=kv, sw=sw, o_gl=0, o_u=o_u, o_ckv=o_ckv, o_kr=o_kr, o_cq=o_cq, width=o_cq + q)


def _pad_w_in(w_in, lay):
    q, kv, sw, d = lay['q'], lay['kv'], lay['sw'], lay['d']
    cq, ckv, kr, u, gl = jnp.split(w_in, [q, q + kv, q + kv + QK_ROPE, q + kv + QK_ROPE + sw], axis=1)
    z = lambda w: jnp.zeros((w_in.shape[0], w), w_in.dtype)
    hole = lay['o_cq'] - lay['o_kr'] - LANES
    return jnp.concatenate([gl, u, ckv, z(QK_NOPE), kr, z(LANES - QK_DIM), z(hole), cq], axis=1)


def _unpad_w_in(g, lay):
    q, kv, sw, d = lay['q'], lay['kv'], lay['sw'], lay['d']
    kr0 = lay['o_kr'] + QK_NOPE
    return jnp.concatenate([g[:, lay['o_cq']:lay['o_cq'] + q], g[:, lay['o_ckv']:lay['o_ckv'] + kv],
                            g[:, kr0:kr0 + QK_ROPE], g[:, lay['o_u']:lay['o_u'] + sw], g[:, :2 * d]], axis=1)


def _pad_heads(w, width):
    k = w.shape[0]
    return jnp.pad(w.reshape(k, N_HEADS, width), ((0, 0), (0, 0), (0, HEAD_PAD - width))).reshape(k, HEADS_W)


def _unpad_heads(w, width):
    k = w.shape[0]
    return w.reshape(k, N_HEADS, HEAD_PAD)[:, :, :width].reshape(k, N_HEADS * width)


def _rope_tables(n, nc):
    rows = n // GRID_W
    row = jnp.repeat(jnp.arange(rows), GRID_W)
    col = jnp.tile(jnp.arange(GRID_W), rows)
    pairs = QK_ROPE // 4
    freqs = ROPE_THETA ** (-jnp.arange(pairs, dtype=F32) / pairs)
    ang = jnp.concatenate([row[:, None] * freqs, col[:, None] * freqs], axis=-1)
    cos = jnp.concatenate([jnp.cos(ang), jnp.ones((nc, 2 * pairs), F32)], axis=0)
    sin = jnp.concatenate([jnp.sin(ang), jnp.zeros((nc, 2 * pairs), F32)], axis=0)
    l = n + nc
    half = QK_ROPE // 2
    ct = jnp.concatenate([jnp.ones((l, QK_NOPE), F32), cos, cos, jnp.zeros((l, HEAD_PAD - QK_DIM), F32)], axis=1)
    s1 = jnp.concatenate([jnp.zeros((l, QK_NOPE + half), F32), sin, jnp.zeros((l, HEAD_PAD - QK_DIM), F32)],
                         axis=1)
    s2 = jnp.concatenate([jnp.zeros((l, QK_NOPE), F32), -sin, jnp.zeros((l, HEAD_PAD - QK_NOPE - half), F32)],
                         axis=1)
    return ct, s1, s2


def _block_diag(m):
    g, a, b = m.shape
    eye = jnp.eye(g, dtype=m.dtype)
    return (m[:, :, None, :] * eye[:, None, :, None]).reshape(g * a, g * b)


def _diag_blocks(m, g):
    a, b = m.shape[0] // g, m.shape[1] // g
    m4 = m.reshape(g, a, g, b)
    return jnp.stack([m4[i, :, i, :] for i in range(g)], axis=0)


def _scan_tables(pwr, pwi):
    row = jnp.arange(SUBLANES)[:, None]
    zero = jnp.zeros_like(pwr)

    def levels(im_sign, keep):
        out = []
        for d in SCAN_LEVELS:
            out.append(jnp.stack([jnp.where(keep(d), pwr[d - 1:d, :], zero),
                                  jnp.where(keep(d), im_sign * pwi[d - 1:d, :], zero)], axis=0))
        return jnp.stack(out, axis=0)

    m_f = levels(1.0, lambda d: row >= d)
    m_r = levels(-1.0, lambda d: row < SUBLANES - d)
    p_f = jnp.stack([pwr, pwi], axis=0)
    p_r = jnp.stack([pwr[::-1], -pwi[::-1]], axis=0)
    return m_f, p_f, m_r, p_r


def _step(inp):
    x, c, ctx = inp['x'][0], inp['c'], inp['ctx'][0]
    target = inp['loss_target'][0]
    n, d = x.shape
    nc = ctx.shape[0]
    l = n + nc
    q_w, kv_w = inp['q_a_g'].shape[1], inp['kv_a_g'].shape[1]
    sw = inp['d_skip'].shape[1]
    n_grp = sw // SSM_GROUP
    gn = n_grp * SSM_STATE
    lay = _in_layout(d, q_w, kv_w, sw)
    tm = _tile(math.gcd(n, nc), (256, 128))
    me = 4 * lax.axis_index("x") + 2 * lax.axis_index("y") + lax.axis_index("c")
    strip = lambda a: a if a.ndim <= 2 else a[0]
    w = {k: strip(inp[k]) for k in WEIGHT_NAMES}

    gathered_names = list(GATHERED)
    wpack, wspans = _pack([w[k].astype(BF16) for k in gathered_names], 16)
    c_all, wg = exchange([jnp.broadcast_to(c, (SUBLANES, d)), wpack], ['gather', 'gather'], "gather_weights")
    shard_shapes = [w[k].shape for k in gathered_names]
    full = {k: _from_shards(s, GATHERED[k]) for k, s in zip(gathered_names, _unpack(wg, wspans, shard_shapes))}

    w_in_p = _pad_w_in(full['w_in'], lay)
    wuq_p = _pad_heads(full['w_uq'], QK_DIM)
    ukv = full['w_ukv'].reshape(kv_w, N_HEADS, QK_NOPE + V_DIM)
    wk_p = _pad_heads(ukv[:, :, :QK_NOPE].reshape(kv_w, -1), QK_NOPE)
    wv_p = _pad_heads(ukv[:, :, QK_NOPE:].reshape(kv_w, -1), V_DIM)
    wo_p = _pad_heads(full['w_o_attn'].T, V_DIM).T
    w_glu, w_out, w_up, w_down = full['w_glu'], full['w_out'], full['w_up'], full['w_down']
    conv_w = jnp.pad(full['conv_w'].astype(F32), ((0, SUBLANES - 3), (0, 0)))
    conv_b = w['conv_b']
    f2 = w_up.shape[1]

    cmat = jnp.concatenate([c_all[:, 0, :], w['c_ctx'][None, :], jnp.zeros((SUBLANES - 1, d), F32)], axis=0)
    mcols = w['w_mod'].shape[1]
    b_cols = lax.dynamic_slice(w['b_mod'], (0, me * mcols), (1, mcols))
    mod_part = mod_fwd(cmat, w['w_mod'], b_cols, "mod_fwd")
    (mod_all,) = exchange([mod_part], ['gather'], "gather_mod")
    mod_me = lax.dynamic_index_in_dim(mod_all, me, axis=1, keepdims=False).reshape(6, d)
    mod_ctx = mod_all[:, SUBLANES, :].reshape(6, d)
    sh1, sc1, g1, sh2, sc2, g2 = [mod_me[k:k + 1] for k in range(6)]
    mods1 = jnp.concatenate([sc1, sh1, mod_ctx[1:2], mod_ctx[0:1], jnp.zeros((4, d), F32)], axis=0)
    mods2 = jnp.concatenate([sc2, sh2, jnp.zeros((6, d), F32)], axis=0)

    xa = jnp.concatenate([x, ctx], axis=0)
    h = norm_mod_fwd(xa, w['norm1_g'], mods1, n, tm, "norm1_fwd")
    proj = matmul(h, w_in_p, 'nn', F32, "in_proj")
    tabs = _rope_tables(n, nc)
    pad_g = lambda g: jnp.pad(g, ((0, 0), (0, HEAD_PAD - QK_DIM)))
    gains = (w['q_a_g'], w['kv_a_g'], pad_g(w['q_norm_g']), pad_g(w['k_norm_g']))
    q, k, v = qkv_fwd(proj, lay, gains, wuq_p, wk_p, wv_p, tabs, tm, "qkv_fwd")
    o, lse = attn_fwd(q, k, v, n, "attn_fwd")

    b_t = lambda a: a.transpose(2, 0, 1).reshape(SSM_GROUP, gn)
    c_t = lambda a: a.transpose(1, 0, 2).reshape(SSM_GROUP, gn)
    bre_t, bim_t = b_t(w['b_re']), b_t(w['b_im'])
    ssm_in, prep = [], []
    for sfx in ('f', 'b'):
        lam_re, lam_im = w['lam_re_' + sfx].reshape(1, gn), w['lam_im_' + sfx].reshape(1, gn)
        logdt = jnp.repeat(w['log_dt_' + sfx], SSM_STATE, axis=1)
        ssm_in.append((lam_re, lam_im, logdt))
        prep.append(ssm_prep(lam_re, lam_im, logdt, bre_t, bim_t, "ssm_prep_" + sfx))

    def blk_b(bb):
        return _block_diag(bb.reshape(SSM_GROUP, n_grp, SSM_STATE).transpose(1, 0, 2)).astype(BF16)

    def blk_c(cc):
        return _block_diag(cc.transpose(0, 2, 1)).astype(BF16)

    bre_blk = jnp.stack([blk_b(p[2]) for p in prep])
    bim_blk = jnp.stack([blk_b(p[3]) for p in prep])
    cre_blk = jnp.stack([blk_c(w['c_re_' + s]) for s in ('f', 'b')])
    ncim_blk = jnp.stack([blk_c(-w['c_im_' + s]) for s in ('f', 'b')])
    tables = [_scan_tables(p[0], p[1]) for p in prep]
    mtab, ptab, mtab_r, ptab_r = [jnp.stack([t[k] for t in tables]) for k in range(4)]

    u_lat = proj[:n, lay['o_u']:lay['o_u'] + sw]
    u_ctx = proj[n:, lay['o_u']:lay['o_u'] + sw]
    useq = jnp.stack([jnp.concatenate([u_ctx, u_lat], axis=0),
                      jnp.concatenate([u_lat, u_ctx], axis=0)[::-1]]).astype(BF16)
    t_len = _tile(l, (256, 128))
    yseq, xbound = ssm_fwd(useq, bre_blk, bim_blk, cre_blk, ncim_blk, mtab, ptab, t_len, "ssm_fwd")
    yf = yseq[0, nc:]
    yb = yseq[1, ::-1][:n]
    s_l = glu_fwd(proj, lay, yf, yb, w['d_skip'], w_glu, n, tm, "glu_fwd")
    x1 = merge_fwd(o, s_l, proj, xa, g1, wo_p, w_out, n, tm, "merge_fwd")

    h2 = norm_mod_fwd(x1, w['norm2_g'], mods2, n, tm, "norm2_fwd")
    up = matmul(h2, w_up, 'nn', F32, "up_proj")
    tw = _tile(n, (128,))
    act = conv_act_fwd(up, conv_w, conv_b, tw, "conv_act_fwd")
    dy, ffn, loss_parts = down_loss(act, w_down, x1, g2, target, tm, "down_loss")
    loss = lax.psum(jnp.sum(loss_parts[:, 0, 0]), MESH_AXES)

    du2, dffn, dg2 = down_bwd(dy, ffn, up, conv_w, conv_b, w_down, g2, tw, "down_bwd")
    g_w_down = matmul(act, dffn, 'tn', F32, "dw_down")
    dup, dconv = conv_bwd(du2, up, conv_w, tw, "conv_bwd")
    dh2 = matmul(dup, w_up, 'nt', F32, "dh2")
    g_w_up = matmul(h2, dup, 'tn', F32, "dw_up")
    dx1, dn2g, dmods2 = norm_mod_bwd(x1, dh2, dy, w['norm2_g'], mods2, n, tm, "norm2_bwd")

    do, ds_l, dgl, dg1, g_wo_p, g_w_out = merge_bwd(o, s_l, proj, xa, g1, wo_p, w_out, dx1, n, tm, "merge_bwd")
    du_direct, dyr, dds, g_w_glu = glu_bwd(proj, lay, yf, yb, w['d_skip'], w_glu, ds_l, n, tm, "glu_bwd")
    zc = jnp.zeros((nc, sw), F32)
    dyseq = jnp.stack([jnp.concatenate([zc, dyr], axis=0),
                       jnp.concatenate([dyr, zc], axis=0)[::-1]]).astype(BF16)
    dseq, s_gr, s_gi, s_xr, s_xi, dlam = ssm_bwd(useq, dyseq, xbound, bre_blk, bim_blk, cre_blk, ncim_blk,
                                                 mtab, ptab, mtab_r, ptab_r, t_len, "ssm_bwd")
    dseq_b = dseq[1, ::-1]
    du_lat = du_direct + dseq[0, nc:] + dseq_b[:n]
    du_ctx = dseq[0, :nc] + dseq_b[n:]

    dq, dk, dv = attn_bwd(q, k, v, o, do, lse, n, "attn_bwd")
    dq_all = jnp.concatenate([dq, jnp.zeros((nc, HEADS_W), F32)], axis=0)
    (dcq, dckv, dkr, dqag, dkvag, dqng, dkng, g_wuq_p, g_wk_p, g_wv_p) = qkv_bwd(
        proj, lay, gains, wuq_p, wk_p, wv_p, tabs, dq_all, dk, dv, tm, "qkv_bwd")
    zrows = lambda wd: jnp.zeros((nc, wd), F32)
    hole = lay['o_cq'] - lay['o_kr'] - LANES
    dproj = jnp.concatenate([
        jnp.concatenate([dgl, zrows(2 * d)], axis=0), jnp.concatenate([du_lat, du_ctx], axis=0), dckv, dkr,
        jnp.zeros((l, hole), F32), jnp.concatenate([dcq[:n], zrows(q_w)], axis=0)], axis=1).astype(BF16)
    dh = matmul(dproj, w_in_p, 'nt', F32, "dh")
    g_w_in_p = matmul(h, dproj, 'tn', F32, "dw_in")
    dxa, dn1g, dmods1 = norm_mod_bwd(xa, dh, dx1, w['norm1_g'], mods1, n, tm, "norm1_bwd")
    grad_x = dxa[:n]

    grads = {}
    d_bbar = [None, None]
    for di, sfx in enumerate(('f', 'b')):
        g_bre = _diag_blocks(matmul(useq[di], s_gr[di], 'tn', F32, "ssm_db_re_" + sfx), n_grp)
        g_bim = _diag_blocks(matmul(useq[di], s_gi[di], 'tn', F32, "ssm_db_im_" + sfx), n_grp)
        g_cre = _diag_blocks(matmul(dyseq[di], s_xr[di], 'tn', F32, "ssm_dc_re_" + sfx), n_grp)
        g_cim = _diag_blocks(matmul(dyseq[di], s_xi[di], 'tn', F32, "ssm_dc_im_" + sfx), n_grp)
        grads['c_re_' + sfx] = g_cre
        grads['c_im_' + sfx] = -g_cim
        to_t = lambda a: a.transpose(1, 0, 2).reshape(SSM_GROUP, gn)
        dlam_re = jnp.sum(dlam[di, 0], axis=0, keepdims=True)
        dlam_im = jnp.sum(dlam[di, 1], axis=0, keepdims=True)
        lam_re, lam_im, logdt = ssm_in[di]
        g_lr, g_li, g_dt, g_br, g_bi = ssm_prep_bwd(lam_re, lam_im, logdt, bre_t, bim_t, dlam_re, dlam_im,
                                                    to_t(g_bre), to_t(g_bim), "ssm_prep_bwd_" + sfx)
        grads['lam_re_' + sfx] = g_lr.reshape(n_grp, SSM_STATE)
        grads['lam_im_' + sfx] = g_li.reshape(n_grp, SSM_STATE)
        grads['log_dt_' + sfx] = jnp.sum(g_dt.reshape(n_grp, SSM_STATE), axis=1)[None, :]
        d_bbar[di] = (g_br, g_bi)
    from_t = lambda a: a.reshape(SSM_GROUP, n_grp, SSM_STATE).transpose(1, 2, 0)
    grads['b_re'] = from_t(d_bbar[0][0]) + from_t(d_bbar[1][0])
    grads['b_im'] = from_t(d_bbar[0][1]) + from_t(d_bbar[1][1])

    dmod = jnp.concatenate([dmods1[1:2], dmods1[0:1], dg1[0:1], dmods2[1:2], dmods2[0:1], dg2[0:1]], axis=1)
    dmod_ctx = jnp.concatenate([dmods1[3:4], dmods1[2:3], jnp.zeros((1, 4 * d), F32)], axis=1)
    dm_send = jnp.concatenate([dmod, dmod_ctx, jnp.zeros((SUBLANES - 2, 6 * d), F32)], axis=0)
    (dm_all,) = exchange([dm_send], ['gather'], "gather_dmod")
    g_all = jnp.concatenate([dm_all[:, 0, :], dm_all[:, 1, :]], axis=0)
    cols = lax.dynamic_slice(g_all.reshape(2 * N_DEV, N_DEV, mcols), (0, me, 0), (2 * N_DEV, 1, mcols))[:, 0, :]
    g_w_mod, dcmat, g_b_mod = mod_bwd(cmat, w['w_mod'], cols[:N_DEV], cols[N_DEV:], g_all, "mod_bwd")

    ukv_g = jnp.concatenate([_unpad_heads(g_wk_p, QK_NOPE).reshape(kv_w, N_HEADS, QK_NOPE),
                             _unpad_heads(g_wv_p, V_DIM).reshape(kv_w, N_HEADS, V_DIM)], axis=2)
    full_g = {'w_in': _unpad_w_in(g_w_in_p, lay), 'w_uq': _unpad_heads(g_wuq_p, QK_DIM),
              'w_ukv': ukv_g.reshape(kv_w, -1), 'w_o_attn': _unpad_heads(g_wo_p.T, V_DIM).T, 'w_glu': g_w_glu,
              'w_out': g_w_out, 'w_up': g_w_up, 'conv_w': dconv[0:3], 'w_down': g_w_down}
    shard_g = [_to_shards(full_g[k], GATHERED[k]) for k in gathered_names]
    gpack = jnp.stack([_pack([s[j] for s in shard_g], 16)[0] for j in range(N_DEV)])
    grads.update({'c_ctx': dcmat[SUBLANES], 'b_mod': g_b_mod[0:1], 'norm1_g': dn1g[0:1], 'norm2_g': dn2g[0:1],
                  'q_a_g': dqag[0:1], 'kv_a_g': dkvag[0:1], 'q_norm_g': dqng[0:1, :QK_DIM],
                  'k_norm_g': dkng[0:1, :QK_DIM], 'd_skip': dds[0:1], 'conv_b': dconv[3:4]})
    first = (me == 0).astype(F32)
    rep_parts = [grads[k] * first if k == 'b_mod' else grads[k] for k in REPLICATED]
    rpack, rspans = _pack(rep_parts, SUBLANES)
    g_recv, r_recv = exchange([gpack, rpack], ['a2a', 'gather'], "exchange_grads")

    def moments(prefix, names, row_mult):
        return _pack([strip(inp[prefix + k]) for k in names], row_mult)[0]

    outs = {}
    packs = [
        (g_recv, gathered_names, wspans, shard_shapes, 16, "adamw_sharded"),
        (r_recv, REPLICATED, rspans, [w[k].shape for k in REPLICATED], SUBLANES, "adamw_replicated"),
        (_pack([g_w_mod], SUBLANES)[0][None], ['w_mod'], _pack([g_w_mod], SUBLANES)[1], [w['w_mod'].shape],
         SUBLANES, "adamw_mod"),
    ]
    for parts, names, spans, shapes, row_mult, name in packs:
        wp = _pack([w[k] for k in names], row_mult)[0]
        res = reduce_adamw(parts, wp, moments('m_', names, row_mult), moments('v_', names, row_mult), name)
        for kind, buf in zip(('grad_', 'delta_', 'new_m_', 'new_v_'), res):
            for k, a in zip(names, _unpack(buf, spans, shapes)):
                outs[kind + k] = a if inp[k].ndim <= 2 else a[None]
    result = [loss, grad_x[None]]
    for kind in ('grad_', 'delta_', 'new_m_', 'new_v_'):
        result += [outs[kind + k] for k in WEIGHT_NAMES]
    return tuple(result)


_ARG_NAMES = (['x', 'c', 'ctx'] + WEIGHT_NAMES + ['loss_target'] + ['m_' + k for k in WEIGHT_NAMES]
              + ['v_' + k for k in WEIGHT_NAMES])


def kernel(*args):
    assert len(args) == len(_ARG_NAMES)
    return _step(dict(zip(_ARG_NAMES, args)))
```

```python
import functools
import math

import jax
import jax.numpy as jnp
from jax import lax
from jax.experimental import pallas as pl
from jax.experimental.pallas import tpu as pltpu

F32 = jnp.float32
BF16 = jnp.bfloat16

N_DEV = 8
MESH_AXES = ("x", "y", "c")
N_HEADS = 8
QK_NOPE = 64
QK_ROPE = 32
QK_DIM = QK_NOPE + QK_ROPE
V_DIM = 64
HEAD_PAD = 128
HEADS_W = N_HEADS * HEAD_PAD
GRID_W = 64
ROPE_THETA = 10000.0
SSM_GROUP = 16
SSM_STATE = 64
EPS = 1e-6
LANES = 128
SUBLANES = 8
PACK_W = 1024
VMEM_LIMIT = 56 * 1024 * 1024
MM_TILES = (1024, 768, 1408, 512, 384, 256, 128)

ADAM_LR = 0.001
ADAM_B1 = 0.9
ADAM_B2 = 0.999
ADAM_EPS = 1e-08
ADAM_WD = 0.01
ADAM_STEP = 10

WEIGHT_NAMES = ['c_ctx', 'w_mod', 'b_mod', 'norm1_g', 'norm2_g', 'w_in', 'q_a_g', 'w_uq', 'kv_a_g', 'w_ukv',
                'q_norm_g', 'k_norm_g', 'w_o_attn', 'lam_re_f', 'lam_im_f', 'log_dt_f', 'c_re_f', 'c_im_f',
                'lam_re_b', 'lam_im_b', 'log_dt_b', 'c_re_b', 'c_im_b', 'b_re', 'b_im', 'd_skip', 'w_glu',
                'w_out', 'w_up', 'conv_w', 'conv_b', 'w_down']
GATHERED = {'w_in': 1, 'w_uq': 1, 'w_ukv': 1, 'w_o_attn': 1, 'w_glu': 1, 'w_out': 0, 'w_up': 1, 'conv_w': 1,
            'w_down': 0}
REPLICATED = [n for n in WEIGHT_NAMES if n not in GATHERED and n != 'w_mod']


def _tile(n, prefs):
    for t in prefs:
        if n % t == 0:
            return t
    return n


def _cparams(sem=None):
    return pltpu.CompilerParams(dimension_semantics=sem, vmem_limit_bytes=VMEM_LIMIT)


@jax.custom_vjp
def bdot(a, w):
    return jnp.dot(a.astype(BF16), w.astype(BF16), preferred_element_type=F32)


def _bdot_fwd(a, w):
    return bdot(a, w), (a, w)


def _bdot_bwd(res, g):
    a, w = res
    gb = g.astype(BF16)
    da = lax.dot_general(gb, w.astype(BF16), (((1,), (1,)), ((), ())), preferred_element_type=F32)
    dw = lax.dot_general(a.astype(BF16), gb, (((0,), (0,)), ((), ())), preferred_element_type=F32)
    return da.astype(a.dtype), dw.astype(w.dtype)


bdot.defvjp(_bdot_fwd, _bdot_bwd)


def _dot_nt(a, b):
    return lax.dot_general(a, b, (((1,), (1,)), ((), ())), preferred_element_type=F32)


def _dot_tn(a, b):
    return lax.dot_general(a, b, (((0,), (0,)), ((), ())), preferred_element_type=F32)


def matmul(a, b, mode, out_dtype, name, a_col0=0, a_cols=None):
    if mode == 'nn':
        (m, k), n = a.shape, b.shape[1]
    elif mode == 'nt':
        (m, k), n = a.shape, b.shape[0]
    else:
        (k, m), n = a.shape, b.shape[1]
        m = a_cols or m
    tm = _tile(m, MM_TILES)
    tn = _tile(n, MM_TILES)
    tk = _tile(k, MM_TILES)
    nk = k // tk
    assert a_col0 % tm == 0
    col0 = a_col0 // tm

    def body(a_ref, b_ref, o_ref, acc_ref):
        kk = pl.program_id(2)

        @pl.when(kk == 0)
        def _():
            acc_ref[...] = jnp.zeros_like(acc_ref)

        av, bv = a_ref[...].astype(BF16), b_ref[...].astype(BF16)
        if mode == 'nn':
            acc_ref[...] += jnp.dot(av, bv, preferred_element_type=F32)
        elif mode == 'nt':
            acc_ref[...] += _dot_nt(av, bv)
        else:
            acc_ref[...] += _dot_tn(av, bv)

        @pl.when(kk == nk - 1)
        def _():
            o_ref[...] = acc_ref[...].astype(o_ref.dtype)

    if mode == 'nn':
        a_spec = pl.BlockSpec((tm, tk), lambda i, j, kk: (i, kk))
        b_spec = pl.BlockSpec((tk, tn), lambda i, j, kk: (kk, j))
    elif mode == 'nt':
        a_spec = pl.BlockSpec((tm, tk), lambda i, j, kk: (i, kk))
        b_spec = pl.BlockSpec((tn, tk), lambda i, j, kk: (j, kk))
    else:
        a_spec = pl.BlockSpec((tk, tm), lambda i, j, kk: (kk, i + col0))
        b_spec = pl.BlockSpec((tk, tn), lambda i, j, kk: (kk, j))
    return pl.pallas_call(
        body, name=name, grid=(m // tm, n // tn, nk),
        in_specs=[a_spec, b_spec],
        out_specs=pl.BlockSpec((tm, tn), lambda i, j, kk: (i, j)),
        out_shape=jax.ShapeDtypeStruct((m, n), out_dtype),
        scratch_shapes=[pltpu.VMEM((tm, tn), F32)],
        compiler_params=_cparams(("parallel", "parallel", "arbitrary")),
    )(a, b)


def _me_and_peers():
    x, y, c = lax.axis_index("x"), lax.axis_index("y"), lax.axis_index("c")
    me = 4 * x + 2 * y + c
    peers = []
    for r in range(1, N_DEV):
        px = 1 - x if r & 4 else x
        py = 1 - y if r & 2 else y
        pc = 1 - c if r & 1 else c
        peers.append(((px, py, pc), 4 * px + 2 * py + pc))
    return me, peers


def exchange(arrays, kinds, name):
    n_arr = len(arrays)

    def body(*refs):
        srcs, dsts = refs[:n_arr], refs[n_arr:2 * n_arr]
        send_sems, recv_sems, local_sems = refs[2 * n_arr:]
        me, peers = _me_and_peers()
        started = []
        for a in range(n_arr):
            gather = kinds[a] == 'gather'
            mine = srcs[a] if gather else srcs[a].at[me]
            local = pltpu.make_async_copy(mine, dsts[a].at[me], local_sems.at[a])
            local.start()
            started.append(local)
            for r, (peer, peer_idx) in enumerate(peers, start=1):
                k = a * N_DEV + r
                pltpu.make_async_remote_copy(
                    src_ref=srcs[a] if gather else srcs[a].at[peer_idx], dst_ref=dsts[a].at[me],
                    send_sem=send_sems.at[k], recv_sem=recv_sems.at[k],
                    device_id=peer, device_id_type=pl.DeviceIdType.MESH).start()
        for a in range(n_arr):
            gather = kinds[a] == 'gather'
            for r, (peer, peer_idx) in enumerate(peers, start=1):
                k = a * N_DEV + r
                cp = pltpu.make_async_remote_copy(
                    src_ref=srcs[a] if gather else srcs[a].at[peer_idx], dst_ref=dsts[a].at[peer_idx],
                    send_sem=send_sems.at[k], recv_sem=recv_sems.at[k],
                    device_id=peer, device_id_type=pl.DeviceIdType.MESH)
                cp.wait_send()
                cp.wait_recv()
        for local in started:
            local.wait()

    out_shape = []
    for a, arr in enumerate(arrays):
        shp = (N_DEV,) + tuple(arr.shape) if kinds[a] == 'gather' else tuple(arr.shape)
        out_shape.append(jax.ShapeDtypeStruct(shp, arr.dtype))
    any_spec = pl.BlockSpec(memory_space=pl.ANY)
    return pl.pallas_call(
        body, name=name,
        in_specs=[any_spec] * n_arr, out_specs=[any_spec] * n_arr, out_shape=out_shape,
        scratch_shapes=[pltpu.SemaphoreType.DMA((n_arr * N_DEV,)), pltpu.SemaphoreType.DMA((n_arr * N_DEV,)),
                        pltpu.SemaphoreType.DMA((n_arr,))],
        compiler_params=pltpu.CompilerParams(has_side_effects=True),
    )(*arrays)


def _pack(parts, row_mult):
    rows, spans, r = [], [], 0
    for p in parts:
        flat = p.reshape(-1)
        nr = -(-flat.shape[0] // (PACK_W * row_mult)) * row_mult
        flat = jnp.pad(flat, (0, nr * PACK_W - flat.shape[0]))
        rows.append(flat.reshape(nr, PACK_W))
        spans.append((r, nr))
        r += nr
    return jnp.concatenate(rows, axis=0), spans


def _unpack(buf, spans, shapes):
    out = []
    for (r, nr), shp in zip(spans, shapes):
        size = math.prod(shp)
        out.append(buf[..., r:r + nr, :].reshape(buf.shape[:-2] + (nr * PACK_W,))[..., :size]
                   .reshape(buf.shape[:-2] + tuple(shp)))
    return out


def _to_shards(full, axis):
    r, c = full.shape
    if axis == 0:
        return full.reshape(N_DEV, r // N_DEV, c)
    return full.reshape(r, N_DEV, c // N_DEV).transpose(1, 0, 2)


def _from_shards(sh, axis):
    _, r, c = sh.shape
    if axis == 0:
        return sh.reshape(N_DEV * r, c)
    return sh.transpose(1, 0, 2).reshape(r, N_DEV * c)


def _rms(x, g, n):
    ms = jnp.sum(x * x, axis=-1, keepdims=True) * (1.0 / n)
    return x * lax.rsqrt(ms + EPS) * g


def _norm_mod_fn(x, g, sc, sh):
    return _rms(x, g, x.shape[-1]) * (1.0 + sc) + sh


@jax.custom_vjp
def _rope(t, ct, s1, s2):
    return t * ct + pltpu.roll(t, 16, 1) * s1 + pltpu.roll(t, HEAD_PAD - 16, 1) * s2


def _rope_fwd(t, ct, s1, s2):
    return _rope(t, ct, s1, s2), (ct, s1, s2)


def _rope_bwd(res, d):
    ct, s1, s2 = res
    dt = d * ct + pltpu.roll(d * s1, HEAD_PAD - 16, 1) + pltpu.roll(d * s2, 16, 1)
    return dt, jnp.zeros_like(ct), jnp.zeros_like(s1), jnp.zeros_like(s2)


_rope.defvjp(_rope_fwd, _rope_bwd)


def _qkv_fn(cq, ckv, krsec, qag, kvag, qng, kng, wuq, wk, wv, ct, s1, s2):
    q_raw = bdot(_rms(cq, qag, cq.shape[-1]), wuq)
    ckvn = _rms(ckv, kvag, ckv.shape[-1])
    k_raw = bdot(ckvn, wk)
    v = bdot(ckvn, wv)
    qs, ks = [], []
    for h in range(N_HEADS):
        sl = slice(h * HEAD_PAD, (h + 1) * HEAD_PAD)
        qs.append(_rope(_rms(q_raw[:, sl], qng, QK_DIM), ct, s1, s2))
        ks.append(_rope(_rms(k_raw[:, sl] + krsec, kng, QK_DIM), ct, s1, s2))
    return jnp.concatenate(qs, axis=1), jnp.concatenate(ks, axis=1), v


def _glu_fn(u, yf, yb, dskip, wglu):
    y = u * dskip + yf + yb
    vg = bdot(jax.nn.gelu(y), wglu)
    d = vg.shape[-1] // 2
    return vg[:, :d] * jax.nn.sigmoid(vg[:, d:])


def _merge_fn(o, s_l, gl, x, g1, wo, wout):
    d = x.shape[-1]
    a = bdot(o, wo)
    mix = jax.nn.sigmoid(gl[:, :d]) * a + jax.nn.sigmoid(gl[:, d:]) * s_l
    return x + g1 * bdot(mix, wout)


def _mod_fn(cmat, w):
    return bdot(jax.nn.silu(cmat), w)


def _ssm_prep_fn(lam_re, lam_im, logdt, bre, bim):
    dt = jnp.exp(logdt)
    ar, ai = lam_re * dt, lam_im * dt
    e = jnp.exp(ar)
    lbr, lbi = e * jnp.cos(ai), e * jnp.sin(ai)
    nr, ni = lbr - 1.0, lbi
    den = lam_re * lam_re + lam_im * lam_im
    qr = (nr * lam_re + ni * lam_im) / den
    qi = (ni * lam_re - nr * lam_im) / den
    return lbr, lbi, qr * bre - qi * bim, qr * bim + qi * bre


def _rows(tm, w, col=0):
    return pl.BlockSpec((tm, w), lambda i: (i, col))


def _full(shape):
    nd = len(shape)
    return pl.BlockSpec(tuple(shape), lambda i: (0,) * nd)


def _acc_add(i, ref, val):
    @pl.when(i == 0)
    def _():
        ref[...] = jnp.zeros_like(ref)
    ref[...] += val


def _pad_rows(v, rows=SUBLANES):
    sel = lax.broadcasted_iota(jnp.int32, (rows, v.shape[-1]), 0) == 0
    return jnp.where(sel, jnp.broadcast_to(v, (rows, v.shape[-1])), 0.0)


def norm_mod_fwd(xa, g, mods, n_lat, tm, name):
    r, d = xa.shape
    lat_tiles = n_lat // tm

    def body(x_ref, g_ref, m_ref, o_ref):
        lat = pl.program_id(0) < lat_tiles
        sc = jnp.where(lat, m_ref[0:1, :], m_ref[2:3, :])
        sh = jnp.where(lat, m_ref[1:2, :], m_ref[3:4, :])
        o_ref[...] = _norm_mod_fn(x_ref[...], g_ref[...], sc, sh).astype(o_ref.dtype)

    return pl.pallas_call(
        body, name=name, grid=(r // tm,),
        in_specs=[_rows(tm, d), _full(g.shape), _full(mods.shape)],
        out_specs=_rows(tm, d), out_shape=jax.ShapeDtypeStruct((r, d), BF16),
        compiler_params=_cparams(("parallel",)),
    )(xa, g, mods)


def norm_mod_bwd(xa, dh, dres, g, mods, n_lat, tm, name):
    r, d = xa.shape
    lat_tiles = n_lat // tm

    def body(x_ref, dh_ref, dres_ref, g_ref, m_ref, dx_ref, dg_ref, dm_ref):
        i = pl.program_id(0)
        lat = i < lat_tiles
        sc = jnp.where(lat, m_ref[0:1, :], m_ref[2:3, :])
        sh = jnp.where(lat, m_ref[1:2, :], m_ref[3:4, :])
        _, vjp = jax.vjp(_norm_mod_fn, x_ref[...], g_ref[...], sc, sh)
        dx, dg, dsc, dsh = vjp(dh_ref[...])
        dx_ref[...] = dx + jnp.where(lat, dres_ref[...], 0.0)
        _acc_add(i, dg_ref, _pad_rows(dg))
        row = lax.broadcasted_iota(jnp.int32, (SUBLANES, d), 0)
        base = jnp.where(lat, 0, 2)
        upd = jnp.where(row == base, jnp.broadcast_to(dsc, (SUBLANES, d)), 0.0)
        upd = upd + jnp.where(row == base + 1, jnp.broadcast_to(dsh, (SUBLANES, d)), 0.0)
        _acc_add(i, dm_ref, upd)

    return pl.pallas_call(
        body, name=name, grid=(r // tm,),
        in_specs=[_rows(tm, d), _rows(tm, d),
                  pl.BlockSpec((tm, d), lambda i: (jnp.minimum(i, lat_tiles - 1), 0)),
                  _full(g.shape), _full(mods.shape)],
        out_specs=[_rows(tm, d), _full((SUBLANES, d)), _full((SUBLANES, d))],
        out_shape=[jax.ShapeDtypeStruct((r, d), F32), jax.ShapeDtypeStruct((SUBLANES, d), F32),
                   jax.ShapeDtypeStruct((SUBLANES, d), F32)],
        compiler_params=_cparams(("arbitrary",)),
    )(xa, dh, dres, g, mods)


def qkv_fwd(proj, lay, gains, wuq, wk, wv, tabs, tm, name):
    r = proj.shape[0]
    q_w, kv_w = lay['q'], lay['kv']

    def body(cq_ref, ckv_ref, kr_ref, qag, kvag, qng, kng, wuq_ref, wk_ref, wv_ref, ct, s1, s2, q_ref, k_ref,
             v_ref):
        q, k, v = _qkv_fn(cq_ref[...], ckv_ref[...], kr_ref[...], qag[...], kvag[...], qng[...], kng[...],
                          wuq_ref[...], wk_ref[...], wv_ref[...], ct[...], s1[...], s2[...])
        q_ref[...] = q.astype(BF16)
        k_ref[...] = k.astype(BF16)
        v_ref[...] = v.astype(BF16)

    out = jax.ShapeDtypeStruct((r, HEADS_W), BF16)
    return pl.pallas_call(
        body, name=name, grid=(r // tm,),
        in_specs=[_rows(tm, q_w, lay['o_cq'] // q_w), _rows(tm, kv_w, lay['o_ckv'] // kv_w),
                  _rows(tm, LANES, lay['o_kr'] // LANES)]
        + [_full(a.shape) for a in gains] + [_full(wuq.shape), _full(wk.shape), _full(wv.shape)]
        + [_rows(tm, HEAD_PAD)] * 3,
        out_specs=[_rows(tm, HEADS_W)] * 3, out_shape=[out, out, out],
        compiler_params=_cparams(("parallel",)),
    )(proj, proj, proj, *gains, wuq, wk, wv, *tabs)


def qkv_bwd(proj, lay, gains, wuq, wk, wv, tabs, dq, dk, dv, dgl, du_direct, du_f, du_b, n, tm, name):
    r = proj.shape[0]
    q_w, kv_w, sw, d = lay['q'], lay['kv'], lay['sw'], lay['d']
    lat_tiles = n // tm

    def body(cq_ref, ckv_ref, kr_ref, qag, kvag, qng, kng, wuq_ref, wk_ref, wv_ref, ct, s1, s2, dq_ref, dk_ref,
             dv_ref, dgl_ref, dud_ref, duf_ref, dub_ref, dp_ref, dqag, dkvag, dqng, dkng, dwuq, dwk, dwv):
        i = pl.program_id(0)
        lat = i < lat_tiles
        tables = (ct[...], s1[...], s2[...])
        fn = lambda *a: _qkv_fn(*a, *tables)
        _, vjp = jax.vjp(fn, cq_ref[...], ckv_ref[...], kr_ref[...], qag[...], kvag[...], qng[...], kng[...],
                         wuq_ref[...].astype(F32), wk_ref[...].astype(F32), wv_ref[...].astype(F32))
        g = vjp((jnp.where(lat, dq_ref[...], 0.0), dk_ref[...], dv_ref[...]))
        dp_ref[:, 0:2 * d] = jnp.where(lat, dgl_ref[...], 0.0).astype(BF16)
        dp_ref[:, lay['o_u']:lay['o_u'] + sw] = (duf_ref[...] + dub_ref[...]
                                                 + jnp.where(lat, dud_ref[...], 0.0)).astype(BF16)
        dp_ref[:, lay['o_ckv']:lay['o_ckv'] + kv_w] = g[1].astype(BF16)
        dp_ref[:, lay['o_kr']:lay['o_kr'] + LANES] = g[2].astype(BF16)
        hole0 = lay['o_kr'] + LANES
        if lay['o_cq'] > hole0:
            dp_ref[:, hole0:lay['o_cq']] = jnp.zeros((tm, lay['o_cq'] - hole0), BF16)
        dp_ref[:, lay['o_cq']:lay['o_cq'] + q_w] = g[0].astype(BF16)
        for ref, val in zip((dqag, dkvag, dqng, dkng), g[3:7]):
            _acc_add(i, ref, _pad_rows(val))
        for ref, val in zip((dwuq, dwk, dwv), g[7:10]):
            _acc_add(i, ref, val)

    def lat_rows(w):
        return pl.BlockSpec((tm, w), lambda i: (jnp.minimum(i, lat_tiles - 1), 0))

    acc_shapes = [(SUBLANES, a.shape[1]) for a in gains] + [wuq.shape, wk.shape, wv.shape]
    return pl.pallas_call(
        body, name=name, grid=(r // tm,),
        in_specs=[_rows(tm, q_w, lay['o_cq'] // q_w), _rows(tm, kv_w, lay['o_ckv'] // kv_w),
                  _rows(tm, LANES, lay['o_kr'] // LANES)]
        + [_full(a.shape) for a in gains] + [_full(wuq.shape), _full(wk.shape), _full(wv.shape)]
        + [_rows(tm, HEAD_PAD)] * 3 + [lat_rows(HEADS_W), _rows(tm, HEADS_W), _rows(tm, HEADS_W)]
        + [lat_rows(2 * d), lat_rows(sw), _rows(tm, sw), _rows(tm, sw)],
        out_specs=[_rows(tm, lay['width'])] + [_full(s) for s in acc_shapes],
        out_shape=[jax.ShapeDtypeStruct((r, lay['width']), BF16)] + [jax.ShapeDtypeStruct(s, F32) for s in acc_shapes],
        compiler_params=_cparams(("arbitrary",)),
    )(proj, proj, proj, *gains, wuq, wk, wv, *tabs, dq, dk, dv, dgl, du_direct, du_f, du_b)


def glu_fwd(proj, lay, yf, yb, dskip, wglu, n, tm, name):
    sw, d = wglu.shape[0], wglu.shape[1] // 2

    def body(u_ref, yf_ref, yb_ref, ds_ref, w_ref, o_ref):
        o_ref[...] = _glu_fn(u_ref[...], yf_ref[...], yb_ref[...], ds_ref[...], w_ref[...])

    return pl.pallas_call(
        body, name=name, grid=(n // tm,),
        in_specs=[_rows(tm, sw, lay['o_u'] // sw), _rows(tm, sw), _rows(tm, sw), _full(dskip.shape),
                  _full(wglu.shape)],
        out_specs=_rows(tm, d), out_shape=jax.ShapeDtypeStruct((n, d), F32),
        compiler_params=_cparams(("parallel",)),
    )(proj, yf, yb, dskip, wglu)


def glu_bwd(proj, lay, yf, yb, dskip, wglu, ds_l, n, tm, name):
    sw, d = wglu.shape[0], wglu.shape[1] // 2

    def body(u_ref, yf_ref, yb_ref, ds_ref, w_ref, g_ref, du_ref, dy_ref, dds_ref, dw_ref):
        i = pl.program_id(0)
        _, vjp = jax.vjp(_glu_fn, u_ref[...], yf_ref[...], yb_ref[...], ds_ref[...], w_ref[...].astype(F32))
        du, dyf, _, dds, dw = vjp(g_ref[...])
        du_ref[...] = du
        dy_ref[...] = dyf
        _acc_add(i, dds_ref, _pad_rows(dds))
        _acc_add(i, dw_ref, dw)

    return pl.pallas_call(
        body, name=name, grid=(n // tm,),
        in_specs=[_rows(tm, sw, lay['o_u'] // sw), _rows(tm, sw), _rows(tm, sw), _full(dskip.shape),
                  _full(wglu.shape), _rows(tm, d)],
        out_specs=[_rows(tm, sw), _rows(tm, sw), _full((SUBLANES, sw)), _full(wglu.shape)],
        out_shape=[jax.ShapeDtypeStruct((n, sw), F32), jax.ShapeDtypeStruct((n, sw), F32),
                   jax.ShapeDtypeStruct((SUBLANES, sw), F32), jax.ShapeDtypeStruct(wglu.shape, F32)],
        compiler_params=_cparams(("arbitrary",)),
    )(proj, yf, yb, dskip, wglu, ds_l)


def merge_fwd(o, s_l, proj, xa, g1, wo, wout, n, tm, name):
    d = xa.shape[1]

    def body(o_ref, s_ref, gl_ref, x_ref, g1_ref, wo_ref, wout_ref, x1_ref):
        x1_ref[...] = _merge_fn(o_ref[...], s_ref[...], gl_ref[...], x_ref[...], g1_ref[...], wo_ref[...],
                                wout_ref[...])

    return pl.pallas_call(
        body, name=name, grid=(n // tm,),
        in_specs=[_rows(tm, HEADS_W), _rows(tm, d), _rows(tm, 2 * d), _rows(tm, d), _full(g1.shape),
                  _full(wo.shape), _full(wout.shape)],
        out_specs=_rows(tm, d), out_shape=jax.ShapeDtypeStruct((n, d), F32),
        compiler_params=_cparams(("parallel",)),
    )(o, s_l, proj, xa, g1, wo, wout)


def merge_bwd(o, s_l, proj, xa, g1, wo, wout, dx1, n, tm, name):
    d = xa.shape[1]

    def body(o_ref, s_ref, gl_ref, x_ref, g1_ref, wo_ref, wout_ref, dx1_ref, do_ref, ds_ref, dgl_ref, dg1_ref,
             dwo_ref, dwout_ref):
        i = pl.program_id(0)
        _, vjp = jax.vjp(_merge_fn, o_ref[...], s_ref[...], gl_ref[...], x_ref[...], g1_ref[...],
                         wo_ref[...].astype(F32), wout_ref[...].astype(F32))
        do, ds, dgl, _, dg1, dwo, dwout = vjp(dx1_ref[...])
        do_ref[...] = do
        ds_ref[...] = ds
        dgl_ref[...] = dgl
        _acc_add(i, dg1_ref, _pad_rows(dg1))
        _acc_add(i, dwo_ref, dwo)
        _acc_add(i, dwout_ref, dwout)

    return pl.pallas_call(
        body, name=name, grid=(n // tm,),
        in_specs=[_rows(tm, HEADS_W), _rows(tm, d), _rows(tm, 2 * d), _rows(tm, d), _full(g1.shape),
                  _full(wo.shape), _full(wout.shape), _rows(tm, d)],
        out_specs=[_rows(tm, HEADS_W), _rows(tm, d), _rows(tm, 2 * d), _full((SUBLANES, d)), _full(wo.shape),
                   _full(wout.shape)],
        out_shape=[jax.ShapeDtypeStruct((n, HEADS_W), BF16), jax.ShapeDtypeStruct((n, d), F32),
                   jax.ShapeDtypeStruct((n, 2 * d), F32), jax.ShapeDtypeStruct((SUBLANES, d), F32),
                   jax.ShapeDtypeStruct(wo.shape, F32), jax.ShapeDtypeStruct(wout.shape, F32)],
        compiler_params=_cparams(("arbitrary",)),
    )(o, s_l, proj, xa, g1, wo, wout, dx1)


def _halo_specs(tm, w, n):
    nb = n // SUBLANES
    per = tm // SUBLANES
    prev = pl.BlockSpec((SUBLANES, w), lambda i: (jnp.maximum(i * per - 1, 0), 0))
    nxt = pl.BlockSpec((SUBLANES, w), lambda i: (jnp.minimum((i + 1) * per, nb - 1), 0))
    return prev, nxt


def _shifted(t, prev_blk, next_blk, i, n_tiles):
    tm = t.shape[0]
    row = lax.broadcasted_iota(jnp.int32, t.shape, 0)
    prev_row = jnp.where(i > 0, prev_blk[SUBLANES - 1:SUBLANES, :], 0.0)
    next_row = jnp.where(i < n_tiles - 1, next_blk[0:1, :], 0.0)
    before = jnp.where(row == 0, prev_row, pltpu.roll(t, 1, 0))
    after = jnp.where(row == tm - 1, next_row, pltpu.roll(t, tm - 1, 0))
    return before, after


def _conv_u2(up, before, after, cw, cb):
    return before * cw[0:1, :] + up * cw[1:2, :] + after * cw[2:3, :] + cb


def conv_act_fwd(up, cw, cb, tm, name):
    n, w2 = up.shape
    f = w2 // 2
    n_tiles = n // tm
    prev, nxt = _halo_specs(tm, w2, n)

    def body(up_ref, prev_ref, next_ref, cw_ref, cb_ref, act_ref):
        i = pl.program_id(0)
        t = up_ref[...]
        before, after = _shifted(t, prev_ref[...], next_ref[...], i, n_tiles)
        u2 = _conv_u2(t, before, after, cw_ref[...], cb_ref[...])
        act_ref[...] = (jax.nn.silu(u2[:, f:]) * u2[:, :f]).astype(BF16)

    return pl.pallas_call(
        body, name=name, grid=(n_tiles,),
        in_specs=[_rows(tm, w2), prev, nxt, _full(cw.shape), _full(cb.shape)],
        out_specs=_rows(tm, f), out_shape=jax.ShapeDtypeStruct((n, f), BF16),
        compiler_params=_cparams(("parallel",)),
    )(up, up, up, cw, cb)


def down_loss(act, wdown, x1, g2, target, tm, name):
    n, d = x1.shape
    f = act.shape[1]

    def body(act_ref, w_ref, x1_ref, g2_ref, t_ref, dy_ref, ffn_ref, loss_ref):
        ffn = jnp.dot(act_ref[...], w_ref[...], preferred_element_type=F32)
        err = x1_ref[...] + g2_ref[...] * ffn - t_ref[...]
        ffn_ref[...] = ffn
        dy_ref[...] = err * (1.0 / d)
        part = 0.5 * jnp.sum(jnp.sum(err * err, axis=-1, keepdims=True) * (1.0 / d), axis=0, keepdims=True)
        loss_ref[0] = jnp.broadcast_to(part, (SUBLANES, LANES))

    return pl.pallas_call(
        body, name=name, grid=(n // tm,),
        in_specs=[_rows(tm, f), _full(wdown.shape), _rows(tm, d), _full(g2.shape), _rows(tm, d)],
        out_specs=[_rows(tm, d), _rows(tm, d), pl.BlockSpec((1, SUBLANES, LANES), lambda i: (i, 0, 0))],
        out_shape=[jax.ShapeDtypeStruct((n, d), F32), jax.ShapeDtypeStruct((n, d), F32),
                   jax.ShapeDtypeStruct((n // tm, SUBLANES, LANES), F32)],
        compiler_params=_cparams(("parallel",)),
    )(act, wdown, x1, g2, target)


def down_bwd(dy, ffn, up, cw, cb, wdown, g2, tm, name):
    n, d = dy.shape
    w2 = up.shape[1]
    f = w2 // 2
    n_tiles = n // tm
    prev, nxt = _halo_specs(tm, w2, n)

    def body(dy_ref, ffn_ref, up_ref, prev_ref, next_ref, cw_ref, cb_ref, w_ref, g2_ref, du2_ref, dffn_ref,
             dg2_ref):
        i = pl.program_id(0)
        dyv = dy_ref[...]
        dffn = (dyv * g2_ref[...]).astype(BF16)
        dffn_ref[...] = dffn
        _acc_add(i, dg2_ref, _pad_rows(jnp.sum(dyv * ffn_ref[...], axis=0, keepdims=True)))
        dact = _dot_nt(dffn, w_ref[...])
        t = up_ref[...]
        before, after = _shifted(t, prev_ref[...], next_ref[...], i, n_tiles)
        u2 = _conv_u2(t, before, after, cw_ref[...], cb_ref[...])
        val, gate = u2[:, :f], u2[:, f:]
        sg = jax.nn.sigmoid(gate)
        du2_ref[:, :f] = dact * (gate * sg)
        du2_ref[:, f:] = dact * val * (sg * (1.0 + gate * (1.0 - sg)))

    return pl.pallas_call(
        body, name=name, grid=(n_tiles,),
        in_specs=[_rows(tm, d), _rows(tm, d), _rows(tm, w2), prev, nxt, _full(cw.shape), _full(cb.shape),
                  _full(wdown.shape), _full(g2.shape)],
        out_specs=[_rows(tm, w2), _rows(tm, d), _full((SUBLANES, d))],
        out_shape=[jax.ShapeDtypeStruct((n, w2), F32), jax.ShapeDtypeStruct((n, d), BF16),
                   jax.ShapeDtypeStruct((SUBLANES, d), F32)],
        compiler_params=_cparams(("arbitrary",)),
    )(dy, ffn, up, up, up, cw, cb, wdown, g2)


def conv_bwd(du2, up, cw, tm, name):
    n, w2 = up.shape
    n_tiles = n // tm
    prev, nxt = _halo_specs(tm, w2, n)

    def body(d_ref, prev_ref, next_ref, up_ref, cw_ref, dup_ref, dcw_ref):
        i = pl.program_id(0)
        dv = d_ref[...]
        before, after = _shifted(dv, prev_ref[...], next_ref[...], i, n_tiles)
        cwv = cw_ref[...]
        dup_ref[...] = (after * cwv[0:1, :] + dv * cwv[1:2, :] + before * cwv[2:3, :]).astype(BF16)
        t = up_ref[...]
        row = lax.broadcasted_iota(jnp.int32, (SUBLANES, w2), 0)
        upd = jnp.zeros((SUBLANES, w2), F32)
        for k, term in enumerate((after * t, dv * t, before * t, dv)):
            upd = upd + jnp.where(row == k, jnp.broadcast_to(jnp.sum(term, axis=0, keepdims=True),
                                                             (SUBLANES, w2)), 0.0)
        _acc_add(i, dcw_ref, upd)

    return pl.pallas_call(
        body, name=name, grid=(n_tiles,),
        in_specs=[_rows(tm, w2), prev, nxt, _rows(tm, w2), _full(cw.shape)],
        out_specs=[_rows(tm, w2), _full((SUBLANES, w2))],
        out_shape=[jax.ShapeDtypeStruct((n, w2), BF16), jax.ShapeDtypeStruct((SUBLANES, w2), F32)],
        compiler_params=_cparams(("arbitrary",)),
    )(du2, du2, du2, up, cw)


def attn_fwd(q, k, v, n, name):
    nk = k.shape[0]
    tq = _tile(n, (512, 256, 128))
    tk = _tile(nk, (768, 384, 256, 128))
    n_kv = nk // tk
    scale = QK_DIM ** -0.5
    c2 = scale * math.log2(math.e)

    def body(q_ref, k_ref, v_ref, o_ref, lse_ref):
        qv = q_ref[...]
        ones_col = (lax.broadcasted_iota(jnp.int32, (tk, HEAD_PAD), 1) == V_DIM).astype(BF16)

        def chunk(j, carry):
            m, acc = carry
            rows = pl.ds(pl.multiple_of(j * tk, tk), tk)
            s = _dot_nt(qv, k_ref[rows, :])
            m_new = jnp.maximum(m, jnp.max(s, axis=-1, keepdims=True))
            p = jnp.exp2(s * c2 - m_new * c2)
            alpha = jnp.exp2((m - m_new) * c2)
            pv = jnp.dot(p.astype(BF16), v_ref[rows, :] + ones_col, preferred_element_type=F32)
            return m_new, alpha * acc + pv

        m, acc = lax.fori_loop(0, n_kv, chunk, (jnp.full((tq, 1), -jnp.inf, F32),
                                                jnp.zeros((tq, HEAD_PAD), F32)), unroll=True)
        l = acc[:, V_DIM:V_DIM + 1]
        lane = lax.broadcasted_iota(jnp.int32, (tq, HEAD_PAD), 1)
        o_ref[...] = jnp.where(lane < V_DIM, acc / l, 0.0).astype(BF16)
        lse_ref[...] = jnp.broadcast_to(m * scale + jnp.log(l), (tq, HEAD_PAD))

    qspec = pl.BlockSpec((tq, HEAD_PAD), lambda h, i: (i, h))
    kspec = pl.BlockSpec((nk, HEAD_PAD), lambda h, i: (0, h))
    return pl.pallas_call(
        body, name=name, grid=(N_HEADS, n // tq),
        in_specs=[qspec, kspec, kspec], out_specs=[qspec, qspec],
        out_shape=[jax.ShapeDtypeStruct((n, HEADS_W), BF16), jax.ShapeDtypeStruct((n, HEADS_W), F32)],
        compiler_params=_cparams(("parallel", "parallel")),
    )(q, k, v)


def attn_bwd(q, k, v, o, do, lse, n, name):
    nk = k.shape[0]
    tq = _tile(n, (512, 256, 128))
    tk = _tile(nk, (768, 384, 256, 128))
    scale = QK_DIM ** -0.5
    log2e = math.log2(math.e)
    c2 = scale * log2e

    def body(q_ref, k_ref, v_ref, o_ref, do_ref, lse_ref, dq_ref, dk_ref, dv_ref):
        @pl.when(pl.program_id(1) == 0)
        def _():
            dq_ref[...] = jnp.zeros_like(dq_ref)

        kv, vv = k_ref[...], v_ref[...]

        def q_tile(i, carry):
            dk, dv = carry
            rows = pl.ds(pl.multiple_of(i * tq, tq), tq)
            qv, dov = q_ref[rows, :], do_ref[rows, :]
            p = jnp.exp2(_dot_nt(qv, kv) * c2 - lse_ref[rows, 0:1] * log2e)
            dv = dv + _dot_tn(p.astype(BF16), dov)
            dp = _dot_nt(dov, vv)
            delta = jnp.sum(dov.astype(F32) * o_ref[rows, :].astype(F32), axis=-1, keepdims=True)
            ds = (p * (dp - delta) * scale).astype(BF16)
            dk = dk + _dot_tn(ds, qv)
            dq_ref[rows, :] += jnp.dot(ds, kv, preferred_element_type=F32)
            return dk, dv

        zero = jnp.zeros((tk, HEAD_PAD), F32)
        dk, dv = lax.fori_loop(0, n // tq, q_tile, (zero, zero), unroll=2)
        dk_ref[...] = dk
        dv_ref[...] = dv

    qspec = pl.BlockSpec((n, HEAD_PAD), lambda h, j: (0, h))
    kspec = pl.BlockSpec((tk, HEAD_PAD), lambda h, j: (j, h))
    return pl.pallas_call(
        body, name=name, grid=(N_HEADS, nk // tk),
        in_specs=[qspec, kspec, kspec, qspec, qspec, qspec],
        out_specs=[qspec, kspec, kspec],
        out_shape=[jax.ShapeDtypeStruct((n, HEADS_W), F32), jax.ShapeDtypeStruct((nk, HEADS_W), F32),
                   jax.ShapeDtypeStruct((nk, HEADS_W), F32)],
        compiler_params=_cparams(("parallel", "arbitrary")),
    )(q, k, v, o, do, lse)


SCAN_LEVELS = (1, 2, 4)
SCAN_LANES = 512


def _scan_chunk(xr, xi, car, m_ref, p_ref, reverse):
    t_len, gn = xr.shape
    n_slab = t_len // SUBLANES
    for lb in range(gn // SCAN_LANES):
        ls = pl.ds(lb * SCAN_LANES, SCAN_LANES)

        def step(s, carry, ls=ls):
            cr, ci = carry
            slab = (n_slab - 1 - s) if reverse else s
            rows = pl.ds(pl.multiple_of(slab * SUBLANES, SUBLANES), SUBLANES)
            br, bi = xr[rows, ls], xi[rows, ls]
            for lvl, d in enumerate(SCAN_LEVELS):
                shift = SUBLANES - d if reverse else d
                sr, si = pltpu.roll(br, shift, 0), pltpu.roll(bi, shift, 0)
                mr, mi = m_ref[lvl, 0, :, ls], m_ref[lvl, 1, :, ls]
                br, bi = br + mr * sr - mi * si, bi + mr * si + mi * sr
            pr, pi = p_ref[0, :, ls], p_ref[1, :, ls]
            br, bi = br + pr * cr - pi * ci, bi + pr * ci + pi * cr
            xr[rows, ls] = br
            xi[rows, ls] = bi
            last = 0 if reverse else SUBLANES - 1
            return br[last:last + 1, :], bi[last:last + 1, :]

        cr, ci = lax.fori_loop(0, n_slab, step, (car[0:1, ls], car[1:2, ls]))
        car[0:1, ls] = cr
        car[1:2, ls] = ci


def _seq_block(step, n_chunk, lat_chunks, reverse):
    if reverse:
        return n_chunk - 1 - step
    return (step + lat_chunks) % n_chunk


def ssm_fwd(proj, lay, bre, bim, cre, ncim, tabs, n, t_len, reverse, name):
    l, sw = proj.shape[0], lay['sw']
    gn = bre.shape[-1]
    n_chunk, lat_chunks = l // t_len, n // t_len
    blk = lambda k: _seq_block(k, n_chunk, lat_chunks, reverse)
    mtab, ptab = tabs

    def body(u_ref, bre_ref, bim_ref, cre_ref, ncim_ref, m_ref, p_ref, y_ref, xb_ref, xr, xi, car):
        @pl.when(pl.program_id(0) == 0)
        def _():
            car[...] = jnp.zeros_like(car)

        xb_ref[...] = car[...]
        u = u_ref[...].astype(BF16)
        xr[...] = jnp.dot(u, bre_ref[...], preferred_element_type=F32)
        xi[...] = jnp.dot(u, bim_ref[...], preferred_element_type=F32)
        _scan_chunk(xr, xi, car, m_ref, p_ref, reverse)
        y_ref[...] = (jnp.dot(xr[...].astype(BF16), cre_ref[...], preferred_element_type=F32)
                      + jnp.dot(xi[...].astype(BF16), ncim_ref[...], preferred_element_type=F32))

    return pl.pallas_call(
        body, name=name, grid=(n_chunk,),
        in_specs=[pl.BlockSpec((t_len, sw), lambda k: (blk(k), lay['o_u'] // sw)), _full(bre.shape),
                  _full(bim.shape), _full(cre.shape), _full(ncim.shape), _full(mtab.shape), _full(ptab.shape)],
        out_specs=[pl.BlockSpec((t_len, sw), lambda k: (blk(k), 0)),
                   pl.BlockSpec((None, 2, gn), lambda k: (k, 0, 0))],
        out_shape=[jax.ShapeDtypeStruct((l, sw), F32), jax.ShapeDtypeStruct((n_chunk, 2, gn), F32)],
        scratch_shapes=[pltpu.VMEM((t_len, gn), F32), pltpu.VMEM((t_len, gn), F32), pltpu.VMEM((2, gn), F32)],
        compiler_params=_cparams(("arbitrary",)),
    )(proj, bre, bim, cre, ncim, mtab, ptab)


def ssm_bwd(proj, lay, dyr, xb, bre, bim, cre, ncim, tabs, adj_tabs, n, t_len, reverse, name):
    l, sw = proj.shape[0], lay['sw']
    gn = bre.shape[-1]
    n_chunk, lat_chunks = l // t_len, n // t_len
    fwd_step = lambda k: n_chunk - 1 - k
    blk = lambda k: _seq_block(fwd_step(k), n_chunk, lat_chunks, reverse)
    (mtab, ptab), (mtab_r, ptab_r) = tabs, adj_tabs

    def body(u_ref, dy_ref, xb_ref, bre_ref, bim_ref, cre_ref, ncim_ref, m_ref, p_ref, mr_ref, pr_ref,
             du_ref, gr_ref, gi_ref, xr_ref, xi_ref, dlam_ref, xr, xi, gr, gi, car, acar):
        k = pl.program_id(0)

        @pl.when(k == 0)
        def _():
            acar[...] = jnp.zeros_like(acar)
            dlam_ref[...] = jnp.zeros_like(dlam_ref)

        u = u_ref[...].astype(BF16)
        dy = jnp.where(blk(k) < lat_chunks, dy_ref[...], 0.0).astype(BF16)
        xr[...] = jnp.dot(u, bre_ref[...], preferred_element_type=F32)
        xi[...] = jnp.dot(u, bim_ref[...], preferred_element_type=F32)
        car[...] = xb_ref[...]
        _scan_chunk(xr, xi, car, m_ref, p_ref, reverse)
        gr[...] = _dot_nt(dy, cre_ref[...])
        gi[...] = _dot_nt(dy, ncim_ref[...])
        _scan_chunk(gr, gi, acar, mr_ref, pr_ref, not reverse)
        xrv, xiv, grv, giv = xr[...], xi[...], gr[...], gi[...]
        row = lax.broadcasted_iota(jnp.int32, (t_len, gn), 0)
        first, shift = (t_len - 1, t_len - 1) if reverse else (0, 1)
        xpr = jnp.where(row == first, xb_ref[0:1, :], pltpu.roll(xrv, shift, 0))
        xpi = jnp.where(row == first, xb_ref[1:2, :], pltpu.roll(xiv, shift, 0))
        dlr = grv * xpr + giv * xpi
        dli = giv * xpr - grv * xpi
        dlam_ref[0] += jnp.sum(dlr.reshape(t_len // SUBLANES, SUBLANES, gn), axis=0)
        dlam_ref[1] += jnp.sum(dli.reshape(t_len // SUBLANES, SUBLANES, gn), axis=0)
        grb, gib = grv.astype(BF16), giv.astype(BF16)
        du_ref[...] = _dot_nt(grb, bre_ref[...]) + _dot_nt(gib, bim_ref[...])
        gr_ref[...] = grb
        gi_ref[...] = gib
        xr_ref[...] = xrv.astype(BF16)
        xi_ref[...] = xiv.astype(BF16)

    def at_blk(width, col=0):
        return pl.BlockSpec((t_len, width), lambda k: (blk(k), col))

    state = jax.ShapeDtypeStruct((l, gn), BF16)
    return pl.pallas_call(
        body, name=name, grid=(n_chunk,),
        in_specs=[at_blk(sw, lay['o_u'] // sw),
                  pl.BlockSpec((t_len, sw), lambda k: (jnp.minimum(blk(k), lat_chunks - 1), 0)),
                  pl.BlockSpec((None, 2, gn), lambda k: (fwd_step(k), 0, 0)),
                  _full(bre.shape), _full(bim.shape), _full(cre.shape), _full(ncim.shape), _full(mtab.shape),
                  _full(ptab.shape), _full(mtab_r.shape), _full(ptab_r.shape)],
        out_specs=[at_blk(sw), at_blk(gn), at_blk(gn), at_blk(gn), at_blk(gn), _full((2, SUBLANES, gn))],
        out_shape=[jax.ShapeDtypeStruct((l, sw), F32), state, state, state, state,
                   jax.ShapeDtypeStruct((2, SUBLANES, gn), F32)],
        scratch_shapes=[pltpu.VMEM((t_len, gn), F32)] * 4 + [pltpu.VMEM((2, gn), F32)] * 2,
        compiler_params=_cparams(("arbitrary",)),
    )(proj, dyr, xb, bre, bim, cre, ncim, mtab, ptab, mtab_r, ptab_r)


def ssm_prep(lam_re, lam_im, logdt, bre, bim, name):
    gn = lam_re.shape[1]

    def body(lr_ref, li_ref, dt_ref, br_ref, bi_ref, pwr_ref, pwi_ref, bbr_ref, bbi_ref):
        _, _, bbr, bbi = _ssm_prep_fn(lr_ref[...], li_ref[...], dt_ref[...], br_ref[...], bi_ref[...])
        bbr_ref[...] = bbr
        bbi_ref[...] = bbi
        kk = (lax.broadcasted_iota(jnp.int32, (SUBLANES, gn), 0) + 1).astype(F32)
        dt = jnp.exp(dt_ref[...])
        ar, ai = lr_ref[...] * dt * kk, li_ref[...] * dt * kk
        e = jnp.exp(ar)
        pwr_ref[...] = e * jnp.cos(ai)
        pwi_ref[...] = e * jnp.sin(ai)

    ins = (lam_re, lam_im, logdt, bre, bim)
    return pl.pallas_call(
        body, name=name,
        out_shape=[jax.ShapeDtypeStruct((SUBLANES, gn), F32)] * 2 + [jax.ShapeDtypeStruct(bre.shape, F32)] * 2,
        compiler_params=_cparams(),
    )(*ins)


def ssm_prep_bwd(lam_re, lam_im, logdt, bre, bim, dlbr, dlbi, dbbr, dbbi, name):
    def body(lr_ref, li_ref, dt_ref, br_ref, bi_ref, g0, g1, g2, g3, o0, o1, o2, o3, o4):
        _, vjp = jax.vjp(_ssm_prep_fn, lr_ref[...], li_ref[...], dt_ref[...], br_ref[...], bi_ref[...])
        for ref, val in zip((o0, o1, o2, o3, o4), vjp((g0[...], g1[...], g2[...], g3[...]))):
            ref[...] = val

    ins = (lam_re, lam_im, logdt, bre, bim)
    return pl.pallas_call(
        body, name=name,
        out_shape=[jax.ShapeDtypeStruct(a.shape, F32) for a in ins],
        compiler_params=_cparams(),
    )(*ins, dlbr, dlbi, dbbr, dbbi)


def mod_fwd(cmat, w, b, name):
    def body(c_ref, w_ref, b_ref, o_ref):
        o_ref[...] = _mod_fn(c_ref[...], w_ref[...]) + b_ref[...]

    return pl.pallas_call(body, name=name, out_shape=jax.ShapeDtypeStruct((cmat.shape[0], w.shape[1]), F32),
                          compiler_params=_cparams())(cmat, w, b)


def mod_bwd(cmat, w, g_lat, g_ctx, g_all, name):
    def body(c_ref, w_ref, gl_ref, gc_ref, ga_ref, dw_ref, dc_ref, db_ref):
        gc = jnp.sum(gc_ref[...], axis=0, keepdims=True)
        dm = jnp.concatenate([gl_ref[...], _pad_rows(gc)], axis=0)
        _, vjp = jax.vjp(_mod_fn, c_ref[...], w_ref[...])
        dc, dw = vjp(dm)
        dw_ref[...] = dw
        dc_ref[...] = dc
        db_ref[...] = _pad_rows(jnp.sum(ga_ref[...], axis=0, keepdims=True))

    return pl.pallas_call(
        body, name=name,
        out_shape=[jax.ShapeDtypeStruct(w.shape, F32), jax.ShapeDtypeStruct(cmat.shape, F32),
                   jax.ShapeDtypeStruct((SUBLANES, g_all.shape[1]), F32)],
        compiler_params=_cparams(),
    )(cmat, w, g_lat, g_ctx, g_all)


def reduce_adamw(parts, w, m, v, name):
    s, r, _ = parts.shape
    tr = _tile(r, (256, 128, 64, 32, 16, 8))

    def body(p_ref, w_ref, m_ref, v_ref, g_ref, d_ref, nm_ref, nv_ref):
        g = p_ref[0]
        for k in range(1, s):
            g = g + p_ref[k]
        mm = ADAM_B1 * m_ref[...] + (1.0 - ADAM_B1) * g
        vv = ADAM_B2 * v_ref[...] + (1.0 - ADAM_B2) * jnp.square(g)
        m_hat = mm / (1.0 - ADAM_B1 ** ADAM_STEP)
        v_hat = vv / (1.0 - ADAM_B2 ** ADAM_STEP)
        g_ref[...] = g
        d_ref[...] = -ADAM_LR * (m_hat / (jnp.sqrt(v_hat) + ADAM_EPS) + ADAM_WD * w_ref[...])
        nm_ref[...] = mm
        nv_ref[...] = vv

    blk = _rows(tr, PACK_W)
    out = jax.ShapeDtypeStruct((r, PACK_W), F32)
    return pl.pallas_call(
        body, name=name, grid=(r // tr,),
        in_specs=[pl.BlockSpec((s, tr, PACK_W), lambda i: (0, i, 0)), blk, blk, blk],
        out_specs=[blk] * 4, out_shape=[out] * 4,
        compiler_params=_cparams(("parallel",)),
    )(parts, w, m, v)


def _in_layout(d, q, kv, sw):
    o_u = 2 * d
    o_ckv = o_u + sw
    o_kr = o_ckv + kv
    o_cq = -(-(o_kr + LANES) // q) * q
    assert o_u % sw == 0 and o_ckv % kv == 0 and o_kr % LANES == 0
    assert q % LANES == 0 and kv % LANES == 0 and sw % LANES == 0
    return dict(d=d, q=q, kv=kv, sw=sw, o_gl=0, o_u=o_u, o_ckv=o_ckv, o_kr=o_kr, o_cq=o_cq, width=o_cq + q)


def _pad_w_in(w_in, lay):
    q, kv, sw, d = lay['q'], lay['kv'], lay['sw'], lay['d']
    cq, ckv, kr, u, gl = jnp.split(w_in, [q, q + kv, q + kv + QK_ROPE, q + kv + QK_ROPE + sw], axis=1)
    z = lambda w: jnp.zeros((w_in.shape[0], w), w_in.dtype)
    hole = lay['o_cq'] - lay['o_kr'] - LANES
    return jnp.concatenate([gl, u, ckv, z(QK_NOPE), kr, z(LANES - QK_DIM), z(hole), cq], axis=1)


def _unpad_w_in(g, lay):
    q, kv, sw, d = lay['q'], lay['kv'], lay['sw'], lay['d']
    kr0 = lay['o_kr'] + QK_NOPE
    return jnp.concatenate([g[:, lay['o_cq']:lay['o_cq'] + q], g[:, lay['o_ckv']:lay['o_ckv'] + kv],
                            g[:, kr0:kr0 + QK_ROPE], g[:, lay['o_u']:lay['o_u'] + sw], g[:, :2 * d]], axis=1)


def _pad_heads(w, width):
    k = w.shape[0]
    return jnp.pad(w.reshape(k, N_HEADS, width), ((0, 0), (0, 0), (0, HEAD_PAD - width))).reshape(k, HEADS_W)


def _unpad_heads(w, width):
    k = w.shape[0]
    return w.reshape(k, N_HEADS, HEAD_PAD)[:, :, :width].reshape(k, N_HEADS * width)


def _rope_tables(n, nc):
    rows = n // GRID_W
    row = jnp.repeat(jnp.arange(rows), GRID_W)
    col = jnp.tile(jnp.arange(GRID_W), rows)
    pairs = QK_ROPE // 4
    freqs = ROPE_THETA ** (-jnp.arange(pairs, dtype=F32) / pairs)
    ang = jnp.concatenate([row[:, None] * freqs, col[:, None] * freqs], axis=-1)
    cos = jnp.concatenate([jnp.cos(ang), jnp.ones((nc, 2 * pairs), F32)], axis=0)
    sin = jnp.concatenate([jnp.sin(ang), jnp.zeros((nc, 2 * pairs), F32)], axis=0)
    l = n + nc
    half = QK_ROPE // 2
    ct = jnp.concatenate([jnp.ones((l, QK_NOPE), F32), cos, cos, jnp.zeros((l, HEAD_PAD - QK_DIM), F32)], axis=1)
    s1 = jnp.concatenate([jnp.zeros((l, QK_NOPE + half), F32), sin, jnp.zeros((l, HEAD_PAD - QK_DIM), F32)],
                         axis=1)
    s2 = jnp.concatenate([jnp.zeros((l, QK_NOPE), F32), -sin, jnp.zeros((l, HEAD_PAD - QK_NOPE - half), F32)],
                         axis=1)
    return ct, s1, s2


def _block_diag(m):
    g, a, b = m.shape
    eye = jnp.eye(g, dtype=m.dtype)
    return (m[:, :, None, :] * eye[:, None, :, None]).reshape(g * a, g * b)


def _diag_blocks(m, g):
    a, b = m.shape[0] // g, m.shape[1] // g
    m4 = m.reshape(g, a, g, b)
    return jnp.sum(m4 * jnp.eye(g, dtype=m.dtype)[:, None, :, None], axis=2)


def _scan_tables(pwr, pwi, reverse):
    row = jnp.arange(SUBLANES)[:, None]
    zero = jnp.zeros_like(pwr)
    levels = []
    for d in SCAN_LEVELS:
        keep = (row < SUBLANES - d) if reverse else (row >= d)
        levels.append(jnp.stack([jnp.where(keep, pwr[d - 1:d, :], zero), jnp.where(keep, pwi[d - 1:d, :], zero)]))
    carry = jnp.stack([pwr[::-1], pwi[::-1]]) if reverse else jnp.stack([pwr, pwi])
    return jnp.stack(levels), carry


def _step(inp):
    x, c, ctx = inp['x'][0], inp['c'], inp['ctx'][0]
    target = inp['loss_target'][0]
    n, d = x.shape
    nc = ctx.shape[0]
    l = n + nc
    q_w, kv_w = inp['q_a_g'].shape[1], inp['kv_a_g'].shape[1]
    sw = inp['d_skip'].shape[1]
    n_grp = sw // SSM_GROUP
    gn = n_grp * SSM_STATE
    lay = _in_layout(d, q_w, kv_w, sw)
    tm = _tile(math.gcd(n, nc), (256, 128))
    me = 4 * lax.axis_index("x") + 2 * lax.axis_index("y") + lax.axis_index("c")
    strip = lambda a: a if a.ndim <= 2 else a[0]
    w = {k: strip(inp[k]) for k in WEIGHT_NAMES}

    gathered_names = list(GATHERED)
    wpack, wspans = _pack([w[k].astype(BF16) for k in gathered_names], 16)
    c_all, wg = exchange([jnp.broadcast_to(c, (SUBLANES, d)), wpack], ['gather', 'gather'], "gather_weights")
    shard_shapes = [w[k].shape for k in gathered_names]
    full = {k: _from_shards(s, GATHERED[k]) for k, s in zip(gathered_names, _unpack(wg, wspans, shard_shapes))}

    w_in_p = _pad_w_in(full['w_in'], lay)
    wuq_p = _pad_heads(full['w_uq'], QK_DIM)
    ukv = full['w_ukv'].reshape(kv_w, N_HEADS, QK_NOPE + V_DIM)
    wk_p = _pad_heads(ukv[:, :, :QK_NOPE].reshape(kv_w, -1), QK_NOPE)
    wv_p = _pad_heads(ukv[:, :, QK_NOPE:].reshape(kv_w, -1), V_DIM)
    wo_p = _pad_heads(full['w_o_attn'].T, V_DIM).T
    w_glu, w_out, w_up, w_down = full['w_glu'], full['w_out'], full['w_up'], full['w_down']
    conv_w = jnp.pad(full['conv_w'].astype(F32), ((0, SUBLANES - 3), (0, 0)))
    conv_b = w['conv_b']
    f2 = w_up.shape[1]

    cmat = jnp.concatenate([c_all[:, 0, :], w['c_ctx'][None, :], jnp.zeros((SUBLANES - 1, d), F32)], axis=0)
    mcols = w['w_mod'].shape[1]
    b_cols = lax.dynamic_slice(w['b_mod'], (0, me * mcols), (1, mcols))
    mod_part = mod_fwd(cmat, w['w_mod'], b_cols, "mod_fwd")
    (mod_all,) = exchange([mod_part], ['gather'], "gather_mod")
    mod_me = lax.dynamic_index_in_dim(mod_all, me, axis=1, keepdims=False).reshape(6, d)
    mod_ctx = mod_all[:, SUBLANES, :].reshape(6, d)
    sh1, sc1, g1, sh2, sc2, g2 = [mod_me[k:k + 1] for k in range(6)]
    mods1 = jnp.concatenate([sc1, sh1, mod_ctx[1:2], mod_ctx[0:1], jnp.zeros((4, d), F32)], axis=0)
    mods2 = jnp.concatenate([sc2, sh2, jnp.zeros((6, d), F32)], axis=0)

    xa = jnp.concatenate([x, ctx], axis=0)
    h = norm_mod_fwd(xa, w['norm1_g'], mods1, n, tm, "norm1_fwd")
    proj = matmul(h, w_in_p, 'nn', F32, "in_proj")
    tabs = _rope_tables(n, nc)
    pad_g = lambda g: jnp.pad(g, ((0, 0), (0, HEAD_PAD - QK_DIM)))
    gains = (w['q_a_g'], w['kv_a_g'], pad_g(w['q_norm_g']), pad_g(w['k_norm_g']))
    q, k, v = qkv_fwd(proj, lay, gains, wuq_p, wk_p, wv_p, tabs, tm, "qkv_fwd")
    o, lse = attn_fwd(q, k, v, n, "attn_fwd")

    b_t = lambda a: a.transpose(2, 0, 1).reshape(SSM_GROUP, gn)
    c_t = lambda a: a.transpose(1, 0, 2).reshape(SSM_GROUP, gn)
    bre_t, bim_t = b_t(w['b_re']), b_t(w['b_im'])
    ssm_in, prep = [], []
    for sfx in ('f', 'b'):
        lam_re, lam_im = w['lam_re_' + sfx].reshape(1, gn), w['lam_im_' + sfx].reshape(1, gn)
        logdt = jnp.repeat(w['log_dt_' + sfx], SSM_STATE, axis=1)
        ssm_in.append((lam_re, lam_im, logdt))
        prep.append(ssm_prep(lam_re, lam_im, logdt, bre_t, bim_t, "ssm_prep_" + sfx))

    def blk_b(bb):
        return _block_diag(bb.reshape(SSM_GROUP, n_grp, SSM_STATE).transpose(1, 0, 2)).astype(BF16)

    def blk_c(cc):
        return _block_diag(cc.transpose(0, 2, 1)).astype(BF16)

    t_len = _tile(math.gcd(n, nc), (256, 128))
    ssm = []
    for di, sfx in enumerate(('f', 'b')):
        reverse = di == 1
        pwr, pwi, bbr, bbi = prep[di]
        ssm.append(dict(
            sfx=sfx, reverse=reverse, blocks=(blk_b(bbr), blk_b(bbi), blk_c(w['c_re_' + sfx]),
                                              blk_c(-w['c_im_' + sfx])),
            tabs=_scan_tables(pwr, pwi, reverse), adj_tabs=_scan_tables(pwr, -pwi, not reverse)))
    for s in ssm:
        s['y'], s['xb'] = ssm_fwd(proj, lay, *s['blocks'], s['tabs'], n, t_len, s['reverse'], "ssm_fwd_" + s['sfx'])
    yf, yb = ssm[0]['y'], ssm[1]['y']
    s_l = glu_fwd(proj, lay, yf, yb, w['d_skip'], w_glu, n, tm, "glu_fwd")
    x1 = merge_fwd(o, s_l, proj, xa, g1, wo_p, w_out, n, tm, "merge_fwd")

    h2 = norm_mod_fwd(x1, w['norm2_g'], mods2, n, tm, "norm2_fwd")
    up = matmul(h2, w_up, 'nn', F32, "up_proj")
    tw = _tile(n, (128,))
    act = conv_act_fwd(up, conv_w, conv_b, tw, "conv_act_fwd")
    dy, ffn, loss_parts = down_loss(act, w_down, x1, g2, target, tm, "down_loss")
    loss = lax.psum(jnp.sum(loss_parts[:, 0, 0]), MESH_AXES)

    du2, dffn, dg2 = down_bwd(dy, ffn, up, conv_w, conv_b, w_down, g2, tw, "down_bwd")
    g_w_down = matmul(act, dffn, 'tn', F32, "dw_down")
    dup, dconv = conv_bwd(du2, up, conv_w, tw, "conv_bwd")
    dh2 = matmul(dup, w_up, 'nt', F32, "dh2")
    g_w_up = matmul(h2, dup, 'tn', F32, "dw_up")
    dx1, dn2g, dmods2 = norm_mod_bwd(x1, dh2, dy, w['norm2_g'], mods2, n, tm, "norm2_bwd")

    do, ds_l, dgl, dg1, g_wo_p, g_w_out = merge_bwd(o, s_l, proj, xa, g1, wo_p, w_out, dx1, n, tm, "merge_bwd")
    du_direct, dyr, dds, g_w_glu = glu_bwd(proj, lay, yf, yb, w['d_skip'], w_glu, ds_l, n, tm, "glu_bwd")
    for s in ssm:
        s['du'], s['gr'], s['gi'], s['xr'], s['xi'], s['dlam'] = ssm_bwd(
            proj, lay, dyr, s['xb'], *s['blocks'], s['tabs'], s['adj_tabs'], n, t_len, s['reverse'],
            "ssm_bwd_" + s['sfx'])

    dq, dk, dv = attn_bwd(q, k, v, o, do, lse, n, "attn_bwd")
    (dproj, dqag, dkvag, dqng, dkng, g_wuq_p, g_wk_p, g_wv_p) = qkv_bwd(
        proj, lay, gains, wuq_p, wk_p, wv_p, tabs, dq, dk, dv, dgl, du_direct, ssm[0]['du'], ssm[1]['du'], n, tm,
        "qkv_bwd")
    dh = matmul(dproj, w_in_p, 'nt', F32, "dh")
    g_w_in_p = matmul(h, dproj, 'tn', F32, "dw_in")
    dxa, dn1g, dmods1 = norm_mod_bwd(xa, dh, dx1, w['norm1_g'], mods1, n, tm, "norm1_bwd")
    grad_x = dxa[:n]

    grads = {}
    d_bbar = [None, None]
    for di, s in enumerate(ssm):
        sfx = s['sfx']
        u_cols = dict(a_col0=lay['o_u'], a_cols=sw)
        g_bre = _diag_blocks(matmul(proj, s['gr'], 'tn', F32, "ssm_db_re_" + sfx, **u_cols), n_grp)
        g_bim = _diag_blocks(matmul(proj, s['gi'], 'tn', F32, "ssm_db_im_" + sfx, **u_cols), n_grp)
        g_cre = _diag_blocks(matmul(dyr, s['xr'], 'tn', F32, "ssm_dc_re_" + sfx), n_grp)
        g_cim = _diag_blocks(matmul(dyr, s['xi'], 'tn', F32, "ssm_dc_im_" + sfx), n_grp)
        grads['c_re_' + sfx] = g_cre
        grads['c_im_' + sfx] = -g_cim
        to_t = lambda a: a.transpose(1, 0, 2).reshape(SSM_GROUP, gn)
        dlam_re = jnp.sum(s['dlam'][0], axis=0, keepdims=True)
        dlam_im = jnp.sum(s['dlam'][1], axis=0, keepdims=True)
        lam_re, lam_im, logdt = ssm_in[di]
        g_lr, g_li, g_dt, g_br, g_bi = ssm_prep_bwd(lam_re, lam_im, logdt, bre_t, bim_t, dlam_re, dlam_im,
                                                    to_t(g_bre), to_t(g_bim), "ssm_prep_bwd_" + sfx)
        grads['lam_re_' + sfx] = g_lr.reshape(n_grp, SSM_STATE)
        grads['lam_im_' + sfx] = g_li.reshape(n_grp, SSM_STATE)
        grads['log_dt_' + sfx] = jnp.sum(g_dt.reshape(n_grp, SSM_STATE), axis=1)[None, :]
        d_bbar[di] = (g_br, g_bi)
    from_t = lambda a: a.reshape(SSM_GROUP, n_grp, SSM_STATE).transpose(1, 2, 0)
    grads['b_re'] = from_t(d_bbar[0][0]) + from_t(d_bbar[1][0])
    grads['b_im'] = from_t(d_bbar[0][1]) + from_t(d_bbar[1][1])

    dmod = jnp.concatenate([dmods1[1:2], dmods1[0:1], dg1[0:1], dmods2[1:2], dmods2[0:1], dg2[0:1]], axis=1)
    dmod_ctx = jnp.concatenate([dmods1[3:4], dmods1[2:3], jnp.zeros((1, 4 * d), F32)], axis=1)
    dm_send = jnp.concatenate([dmod, dmod_ctx, jnp.zeros((SUBLANES - 2, 6 * d), F32)], axis=0)
    (dm_all,) = exchange([dm_send], ['gather'], "gather_dmod")
    g_all = jnp.concatenate([dm_all[:, 0, :], dm_all[:, 1, :]], axis=0)
    cols = lax.dynamic_slice(g_all.reshape(2 * N_DEV, N_DEV, mcols), (0, me, 0), (2 * N_DEV, 1, mcols))[:, 0, :]
    g_w_mod, dcmat, g_b_mod = mod_bwd(cmat, w['w_mod'], cols[:N_DEV], cols[N_DEV:], g_all, "mod_bwd")

    ukv_g = jnp.concatenate([_unpad_heads(g_wk_p, QK_NOPE).reshape(kv_w, N_HEADS, QK_NOPE),
                             _unpad_heads(g_wv_p, V_DIM).reshape(kv_w, N_HEADS, V_DIM)], axis=2)
    full_g = {'w_in': _unpad_w_in(g_w_in_p, lay), 'w_uq': _unpad_heads(g_wuq_p, QK_DIM),
              'w_ukv': ukv_g.reshape(kv_w, -1), 'w_o_attn': _unpad_heads(g_wo_p.T, V_DIM).T, 'w_glu': g_w_glu,
              'w_out': g_w_out, 'w_up': g_w_up, 'conv_w': dconv[0:3], 'w_down': g_w_down}
    shard_g = [_to_shards(full_g[k], GATHERED[k]) for k in gathered_names]
    gpack = jnp.stack([_pack([s[j] for s in shard_g], 16)[0] for j in range(N_DEV)])
    grads.update({'c_ctx': dcmat[SUBLANES], 'b_mod': g_b_mod[0:1], 'norm1_g': dn1g[0:1], 'norm2_g': dn2g[0:1],
                  'q_a_g': dqag[0:1], 'kv_a_g': dkvag[0:1], 'q_norm_g': dqng[0:1, :QK_DIM],
                  'k_norm_g': dkng[0:1, :QK_DIM], 'd_skip': dds[0:1], 'conv_b': dconv[3:4]})
    first = (me == 0).astype(F32)
    rep_parts = [grads[k] * first if k == 'b_mod' else grads[k] for k in REPLICATED]
    rpack, rspans = _pack(rep_parts, SUBLANES)
    g_recv, r_recv = exchange([gpack, rpack], ['a2a', 'gather'], "exchange_grads")

    def moments(prefix, names, row_mult):
        return _pack([strip(inp[prefix + k]) for k in names], row_mult)[0]

    outs = {}
    packs = [
        (g_recv, gathered_names, wspans, shard_shapes, 16, "adamw_sharded"),
        (r_recv, REPLICATED, rspans, [w[k].shape for k in REPLICATED], SUBLANES, "adamw_replicated"),
        (_pack([g_w_mod], SUBLANES)[0][None], ['w_mod'], _pack([g_w_mod], SUBLANES)[1], [w['w_mod'].shape],
         SUBLANES, "adamw_mod"),
    ]
    for parts, names, spans, shapes, row_mult, name in packs:
        wp = _pack([w[k] for k in names], row_mult)[0]
        res = reduce_adamw(parts, wp, moments('m_', names, row_mult), moments('v_', names, row_mult), name)
        for kind, buf in zip(('grad_', 'delta_', 'new_m_', 'new_v_'), res):
            for k, a in zip(names, _unpack(buf, spans, shapes)):
                outs[kind + k] = a if inp[k].ndim <= 2 else a[None]
    result = [loss, grad_x[None]]
    for kind in ('grad_', 'delta_', 'new_m_', 'new_v_'):
        result += [outs[kind + k] for k in WEIGHT_NAMES]
    return tuple(result)


_ARG_NAMES = (['x', 'c', 'ctx'] + WEIGHT_NAMES + ['loss_target'] + ['m_' + k for k in WEIGHT_NAMES]
              + ['v_' + k for k in WEIGHT_NAMES])


def kernel(*args):
    assert len(args) == len(_ARG_NAMES)
    return _step(dict(zip(_ARG_NAMES, args)))
```

```python
import functools
import math

import jax
import jax.numpy as jnp
from jax import lax
from jax.experimental import pallas as pl
from jax.experimental.pallas import tpu as pltpu

F32 = jnp.float32
BF16 = jnp.bfloat16

N_DEV = 8
MESH_AXES = ("x", "y", "c")
N_HEADS = 8
QK_NOPE = 64
QK_ROPE = 32
QK_DIM = QK_NOPE + QK_ROPE
V_DIM = 64
HEAD_PAD = 128
HEADS_W = N_HEADS * HEAD_PAD
GRID_W = 64
ROPE_THETA = 10000.0
SSM_GROUP = 16
SSM_STATE = 64
EPS = 1e-6
LANES = 128
SUBLANES = 8
PACK_W = 1024
VMEM_LIMIT = 56 * 1024 * 1024
MM_TILES = (1024, 768, 1408, 512, 384, 256, 128)

ADAM_LR = 0.001
ADAM_B1 = 0.9
ADAM_B2 = 0.999
ADAM_EPS = 1e-08
ADAM_WD = 0.01
ADAM_STEP = 10

WEIGHT_NAMES = ['c_ctx', 'w_mod', 'b_mod', 'norm1_g', 'norm2_g', 'w_in', 'q_a_g', 'w_uq', 'kv_a_g', 'w_ukv',
                'q_norm_g', 'k_norm_g', 'w_o_attn', 'lam_re_f', 'lam_im_f', 'log_dt_f', 'c_re_f', 'c_im_f',
                'lam_re_b', 'lam_im_b', 'log_dt_b', 'c_re_b', 'c_im_b', 'b_re', 'b_im', 'd_skip', 'w_glu',
                'w_out', 'w_up', 'conv_w', 'conv_b', 'w_down']
GATHERED = {'w_in': 1, 'w_uq': 1, 'w_ukv': 1, 'w_o_attn': 1, 'w_glu': 1, 'w_out': 0, 'w_up': 1, 'conv_w': 1,
            'w_down': 0}
REPLICATED = [n for n in WEIGHT_NAMES if n not in GATHERED and n != 'w_mod']


def _tile(n, prefs):
    for t in prefs:
        if n % t == 0:
            return t
    return n


def _cparams(sem=None):
    return pltpu.CompilerParams(dimension_semantics=sem, vmem_limit_bytes=VMEM_LIMIT)


@jax.custom_vjp
def bdot(a, w):
    return jnp.dot(a.astype(BF16), w.astype(BF16), preferred_element_type=F32)


def _bdot_fwd(a, w):
    return bdot(a, w), (a, w)


def _bdot_bwd(res, g):
    a, w = res
    gb = g.astype(BF16)
    da = lax.dot_general(gb, w.astype(BF16), (((1,), (1,)), ((), ())), preferred_element_type=F32)
    dw = lax.dot_general(a.astype(BF16), gb, (((0,), (0,)), ((), ())), preferred_element_type=F32)
    return da.astype(a.dtype), dw.astype(w.dtype)


bdot.defvjp(_bdot_fwd, _bdot_bwd)


def _dot_nt(a, b):
    return lax.dot_general(a, b, (((1,), (1,)), ((), ())), preferred_element_type=F32)


def _dot_tn(a, b):
    return lax.dot_general(a, b, (((0,), (0,)), ((), ())), preferred_element_type=F32)


def matmul(a, b, mode, out_dtype, name, a_col0=0, a_cols=None):
    if mode == 'nn':
        (m, k), n = a.shape, b.shape[1]
    elif mode == 'nt':
        (m, k), n = a.shape, b.shape[0]
    else:
        (k, m), n = a.shape, b.shape[1]
        m = a_cols or m
    tm = _tile(m, MM_TILES)
    tn = _tile(n, MM_TILES)
    tk = _tile(k, MM_TILES)
    nk = k // tk
    assert a_col0 % tm == 0
    col0 = a_col0 // tm

    def body(a_ref, b_ref, o_ref, acc_ref):
        kk = pl.program_id(2)

        @pl.when(kk == 0)
        def _():
            acc_ref[...] = jnp.zeros_like(acc_ref)

        av, bv = a_ref[...].astype(BF16), b_ref[...].astype(BF16)
        if mode == 'nn':
            acc_ref[...] += jnp.dot(av, bv, preferred_element_type=F32)
        elif mode == 'nt':
            acc_ref[...] += _dot_nt(av, bv)
        else:
            acc_ref[...] += _dot_tn(av, bv)

        @pl.when(kk == nk - 1)
        def _():
            o_ref[...] = acc_ref[...].astype(o_ref.dtype)

    if mode == 'nn':
        a_spec = pl.BlockSpec((tm, tk), lambda i, j, kk: (i, kk))
        b_spec = pl.BlockSpec((tk, tn), lambda i, j, kk: (kk, j))
    elif mode == 'nt':
        a_spec = pl.BlockSpec((tm, tk), lambda i, j, kk: (i, kk))
        b_spec = pl.BlockSpec((tn, tk), lambda i, j, kk: (j, kk))
    else:
        a_spec = pl.BlockSpec((tk, tm), lambda i, j, kk: (kk, i + col0))
        b_spec = pl.BlockSpec((tk, tn), lambda i, j, kk: (kk, j))
    return pl.pallas_call(
        body, name=name, grid=(m // tm, n // tn, nk),
        in_specs=[a_spec, b_spec],
        out_specs=pl.BlockSpec((tm, tn), lambda i, j, kk: (i, j)),
        out_shape=jax.ShapeDtypeStruct((m, n), out_dtype),
        scratch_shapes=[pltpu.VMEM((tm, tn), F32)],
        compiler_params=_cparams(("parallel", "parallel", "arbitrary")),
    )(a, b)


N_CHIPS = 4


def _group(group):
    x, y, c = lax.axis_index("x"), lax.axis_index("y"), lax.axis_index("c")
    flips = {'all': [(r & 4, r & 2, r & 1) for r in range(1, 8)],
             'chips': [(0, 1, 0), (1, 0, 0), (1, 1, 0)], 'core': [(0, 0, 1)]}[group]
    index = {'all': lambda px, py, pc: 4 * px + 2 * py + pc, 'chips': lambda px, py, pc: 2 * px + py,
             'core': lambda px, py, pc: pc}[group]
    peers = []
    for fx, fy, fc in flips:
        p = (1 - x if fx else x, 1 - y if fy else y, 1 - c if fc else c)
        peers.append((p, index(*p)))
    return len(flips) + 1, index(x, y, c), peers


def exchange(arrays, kinds, name, group='all'):
    n_arr = len(arrays)
    size = {'all': N_DEV, 'chips': N_CHIPS, 'core': 2}[group]

    def body(*refs):
        srcs, dsts = refs[:n_arr], refs[n_arr:2 * n_arr]
        send_sems, recv_sems, local_sems = refs[2 * n_arr:]
        _, me, peers = _group(group)
        started = []
        for a in range(n_arr):
            gather = kinds[a] == 'gather'
            mine = srcs[a] if gather else srcs[a].at[me]
            local = pltpu.make_async_copy(mine, dsts[a].at[me], local_sems.at[a])
            local.start()
            started.append(local)
            for r, (peer, peer_idx) in enumerate(peers):
                k = a * size + r
                pltpu.make_async_remote_copy(
                    src_ref=srcs[a] if gather else srcs[a].at[peer_idx], dst_ref=dsts[a].at[me],
                    send_sem=send_sems.at[k], recv_sem=recv_sems.at[k],
                    device_id=peer, device_id_type=pl.DeviceIdType.MESH).start()
        for a in range(n_arr):
            gather = kinds[a] == 'gather'
            for r, (peer, peer_idx) in enumerate(peers):
                k = a * size + r
                cp = pltpu.make_async_remote_copy(
                    src_ref=srcs[a] if gather else srcs[a].at[peer_idx], dst_ref=dsts[a].at[peer_idx],
                    send_sem=send_sems.at[k], recv_sem=recv_sems.at[k],
                    device_id=peer, device_id_type=pl.DeviceIdType.MESH)
                cp.wait_send()
                cp.wait_recv()
        for local in started:
            local.wait()

    out_shape = []
    for a, arr in enumerate(arrays):
        shp = (size,) + tuple(arr.shape) if kinds[a] == 'gather' else tuple(arr.shape)
        out_shape.append(jax.ShapeDtypeStruct(shp, arr.dtype))
    any_spec = pl.BlockSpec(memory_space=pl.ANY)
    return pl.pallas_call(
        body, name=name,
        in_specs=[any_spec] * n_arr, out_specs=[any_spec] * n_arr, out_shape=out_shape,
        scratch_shapes=[pltpu.SemaphoreType.DMA((n_arr * size,)), pltpu.SemaphoreType.DMA((n_arr * size,)),
                        pltpu.SemaphoreType.DMA((n_arr,))],
        compiler_params=pltpu.CompilerParams(has_side_effects=True),
    )(*arrays)


def gather_two_level(arrays, name):
    n_arr = len(arrays)
    per = 7

    def body(*refs):
        srcs, outs = refs[:n_arr], refs[n_arr:2 * n_arr]
        send_sems, recv_sems, local_sems = refs[2 * n_arr:]
        x, y, c = lax.axis_index("x"), lax.axis_index("y"), lax.axis_index("c")
        me, sibling = (x, y, c), (x, y, 1 - c)
        chips = [(1 - x, y), (x, 1 - y), (1 - x, 1 - y)]
        slot = lambda p: 4 * p[0] + 2 * p[1] + p[2]

        def copy(a, k, block, to, src=None):
            dst = outs[a].at[slot(block)]
            return pltpu.make_async_remote_copy(
                src_ref=dst if src is None else src, dst_ref=dst,
                send_sem=send_sems.at[a * per + k], recv_sem=recv_sems.at[a * per + k],
                device_id=to, device_id_type=pl.DeviceIdType.MESH)

        mine, first, passed = [], [], []
        for a in range(n_arr):
            mine.append(pltpu.make_async_copy(srcs[a], outs[a].at[slot(me)], local_sems.at[a]))
            first.append(copy(a, 0, me, sibling, src=srcs[a]))
            first += [copy(a, 1 + j, me, (*chip, c), src=srcs[a]) for j, chip in enumerate(chips)]
        for cp in mine + first:
            cp.start()
        for j, chip in enumerate(chips):
            for a in range(n_arr):
                copy(a, 1 + j, (*chip, c), me).wait_recv()
                passed.append(copy(a, 4 + j, (*chip, c), sibling))
                passed[-1].start()
        for a in range(n_arr):
            copy(a, 0, sibling, me).wait_recv()
            for j, chip in enumerate(chips):
                copy(a, 4 + j, (*chip, 1 - c), me).wait_recv()
        for cp in first + passed:
            cp.wait_send()
        for cp in mine:
            cp.wait()

    any_spec = pl.BlockSpec(memory_space=pl.ANY)
    return pl.pallas_call(
        body, name=name,
        in_specs=[any_spec] * n_arr, out_specs=[any_spec] * n_arr,
        out_shape=[jax.ShapeDtypeStruct((N_DEV,) + tuple(a.shape), a.dtype) for a in arrays],
        scratch_shapes=[pltpu.SemaphoreType.DMA((n_arr * per,)), pltpu.SemaphoreType.DMA((n_arr * per,)),
                        pltpu.SemaphoreType.DMA((n_arr,))],
        compiler_params=pltpu.CompilerParams(has_side_effects=True),
    )(*arrays)


def _pack(parts, row_mult):
    rows, spans, r = [], [], 0
    for p in parts:
        flat = p.reshape(-1)
        nr = -(-flat.shape[0] // (PACK_W * row_mult)) * row_mult
        flat = jnp.pad(flat, (0, nr * PACK_W - flat.shape[0]))
        rows.append(flat.reshape(nr, PACK_W))
        spans.append((r, nr))
        r += nr
    return jnp.concatenate(rows, axis=0), spans


def _unpack(buf, spans, shapes):
    out = []
    for (r, nr), shp in zip(spans, shapes):
        size = math.prod(shp)
        out.append(buf[..., r:r + nr, :].reshape(buf.shape[:-2] + (nr * PACK_W,))[..., :size]
                   .reshape(buf.shape[:-2] + tuple(shp)))
    return out


def _to_shards(full, axis):
    r, c = full.shape
    if axis == 0:
        return full.reshape(N_DEV, r // N_DEV, c)
    return full.reshape(r, N_DEV, c // N_DEV).transpose(1, 0, 2)


def _from_shards(sh, axis):
    _, r, c = sh.shape
    if axis == 0:
        return sh.reshape(N_DEV * r, c)
    return sh.transpose(1, 0, 2).reshape(r, N_DEV * c)


def _rms(x, g, n):
    ms = jnp.sum(x * x, axis=-1, keepdims=True) * (1.0 / n)
    return x * lax.rsqrt(ms + EPS) * g


def _norm_mod_fn(x, g, sc, sh):
    return _rms(x, g, x.shape[-1]) * (1.0 + sc) + sh


@jax.custom_vjp
def _rope(t, ct, s1, s2):
    return t * ct + pltpu.roll(t, 16, 1) * s1 + pltpu.roll(t, HEAD_PAD - 16, 1) * s2


def _rope_fwd(t, ct, s1, s2):
    return _rope(t, ct, s1, s2), (ct, s1, s2)


def _rope_bwd(res, d):
    ct, s1, s2 = res
    dt = d * ct + pltpu.roll(d * s1, HEAD_PAD - 16, 1) + pltpu.roll(d * s2, 16, 1)
    return dt, jnp.zeros_like(ct), jnp.zeros_like(s1), jnp.zeros_like(s2)


_rope.defvjp(_rope_fwd, _rope_bwd)


def _qkv_fn(cq, ckv, krsec, qag, kvag, qng, kng, wuq, wk, wv, ct, s1, s2):
    q_raw = bdot(_rms(cq, qag, cq.shape[-1]), wuq)
    ckvn = _rms(ckv, kvag, ckv.shape[-1])
    k_raw = bdot(ckvn, wk)
    v = bdot(ckvn, wv)
    qs, ks = [], []
    for h in range(N_HEADS):
        sl = slice(h * HEAD_PAD, (h + 1) * HEAD_PAD)
        qs.append(_rope(_rms(q_raw[:, sl], qng, QK_DIM), ct, s1, s2))
        ks.append(_rope(_rms(k_raw[:, sl] + krsec, kng, QK_DIM), ct, s1, s2))
    return jnp.concatenate(qs, axis=1), jnp.concatenate(ks, axis=1), v


def _glu_fn(u, yf, yb, dskip, wglu):
    y = u * dskip + yf + yb
    vg = bdot(jax.nn.gelu(y), wglu)
    d = vg.shape[-1] // 2
    return vg[:, :d] * jax.nn.sigmoid(vg[:, d:])


def _merge_fn(o, s_l, gl, x, g1, wo, wout):
    d = x.shape[-1]
    a = bdot(o, wo)
    mix = jax.nn.sigmoid(gl[:, :d]) * a + jax.nn.sigmoid(gl[:, d:]) * s_l
    return x + g1 * bdot(mix, wout)


def _mod_fn(cmat, w):
    return bdot(jax.nn.silu(cmat), w)


def _ssm_prep_fn(lam_re, lam_im, logdt, bre, bim):
    dt = jnp.exp(logdt)
    ar, ai = lam_re * dt, lam_im * dt
    e = jnp.exp(ar)
    lbr, lbi = e * jnp.cos(ai), e * jnp.sin(ai)
    nr, ni = lbr - 1.0, lbi
    den = lam_re * lam_re + lam_im * lam_im
    qr = (nr * lam_re + ni * lam_im) / den
    qi = (ni * lam_re - nr * lam_im) / den
    return lbr, lbi, qr * bre - qi * bim, qr * bim + qi * bre


def _rows(tm, w, col=0):
    return pl.BlockSpec((tm, w), lambda i: (i, col))


def _full(shape):
    nd = len(shape)
    return pl.BlockSpec(tuple(shape), lambda i: (0,) * nd)


def _acc_add(i, ref, val):
    @pl.when(i == 0)
    def _():
        ref[...] = jnp.zeros_like(ref)
    ref[...] += val


def _pad_rows(v, rows=SUBLANES):
    sel = lax.broadcasted_iota(jnp.int32, (rows, v.shape[-1]), 0) == 0
    return jnp.where(sel, jnp.broadcast_to(v, (rows, v.shape[-1])), 0.0)


def norm_mod_fwd(xa, g, mods, n_lat, tm, name):
    r, d = xa.shape
    lat_tiles = n_lat // tm

    def body(x_ref, g_ref, m_ref, o_ref):
        lat = pl.program_id(0) < lat_tiles
        sc = jnp.where(lat, m_ref[0:1, :], m_ref[2:3, :])
        sh = jnp.where(lat, m_ref[1:2, :], m_ref[3:4, :])
        o_ref[...] = _norm_mod_fn(x_ref[...], g_ref[...], sc, sh).astype(o_ref.dtype)

    return pl.pallas_call(
        body, name=name, grid=(r // tm,),
        in_specs=[_rows(tm, d), _full(g.shape), _full(mods.shape)],
        out_specs=_rows(tm, d), out_shape=jax.ShapeDtypeStruct((r, d), BF16),
        compiler_params=_cparams(("parallel",)),
    )(xa, g, mods)


def norm_mod_bwd(xa, dh, dres, g, mods, n_lat, tm, name):
    r, d = xa.shape
    lat_tiles = n_lat // tm

    def body(x_ref, dh_ref, dres_ref, g_ref, m_ref, dx_ref, dg_ref, dm_ref):
        i = pl.program_id(0)
        lat = i < lat_tiles
        sc = jnp.where(lat, m_ref[0:1, :], m_ref[2:3, :])
        sh = jnp.where(lat, m_ref[1:2, :], m_ref[3:4, :])
        _, vjp = jax.vjp(_norm_mod_fn, x_ref[...], g_ref[...], sc, sh)
        dx, dg, dsc, dsh = vjp(dh_ref[...])
        dx_ref[...] = dx + jnp.where(lat, dres_ref[...], 0.0)
        _acc_add(i, dg_ref, _pad_rows(dg))
        row = lax.broadcasted_iota(jnp.int32, (SUBLANES, d), 0)
        base = jnp.where(lat, 0, 2)
        upd = jnp.where(row == base, jnp.broadcast_to(dsc, (SUBLANES, d)), 0.0)
        upd = upd + jnp.where(row == base + 1, jnp.broadcast_to(dsh, (SUBLANES, d)), 0.0)
        _acc_add(i, dm_ref, upd)

    return pl.pallas_call(
        body, name=name, grid=(r // tm,),
        in_specs=[_rows(tm, d), _rows(tm, d),
                  pl.BlockSpec((tm, d), lambda i: (jnp.minimum(i, lat_tiles - 1), 0)),
                  _full(g.shape), _full(mods.shape)],
        out_specs=[_rows(tm, d), _full((SUBLANES, d)), _full((SUBLANES, d))],
        out_shape=[jax.ShapeDtypeStruct((r, d), F32), jax.ShapeDtypeStruct((SUBLANES, d), F32),
                   jax.ShapeDtypeStruct((SUBLANES, d), F32)],
        compiler_params=_cparams(("arbitrary",)),
    )(xa, dh, dres, g, mods)


def qkv_fwd(proj, lay, gains, wuq, wk, wv, tabs, tm, name):
    r = proj.shape[0]
    q_w, kv_w = lay['q'], lay['kv']

    def body(cq_ref, ckv_ref, kr_ref, qag, kvag, qng, kng, wuq_ref, wk_ref, wv_ref, ct, s1, s2, q_ref, k_ref,
             v_ref):
        q, k, v = _qkv_fn(cq_ref[...], ckv_ref[...], kr_ref[...], qag[...], kvag[...], qng[...], kng[...],
                          wuq_ref[...], wk_ref[...], wv_ref[...], ct[...], s1[...], s2[...])
        q_ref[...] = q.astype(BF16)
        k_ref[...] = k.astype(BF16)
        v_ref[...] = v.astype(BF16)

    out = jax.ShapeDtypeStruct((r, HEADS_W), BF16)
    return pl.pallas_call(
        body, name=name, grid=(r // tm,),
        in_specs=[_rows(tm, q_w, lay['o_cq'] // q_w), _rows(tm, kv_w, lay['o_ckv'] // kv_w),
                  _rows(tm, LANES, lay['o_kr'] // LANES)]
        + [_full(a.shape) for a in gains] + [_full(wuq.shape), _full(wk.shape), _full(wv.shape)]
        + [_rows(tm, HEAD_PAD)] * 3,
        out_specs=[_rows(tm, HEADS_W)] * 3, out_shape=[out, out, out],
        compiler_params=_cparams(("parallel",)),
    )(proj, proj, proj, *gains, wuq, wk, wv, *tabs)


def qkv_bwd(proj, lay, gains, wuq, wk, wv, tabs, dq, dk, dv, dgl, du_direct, du_f, du_b, n, tm, name):
    r = proj.shape[0]
    q_w, kv_w, sw, d = lay['q'], lay['kv'], lay['sw'], lay['d']
    lat_tiles = n // tm

    def body(cq_ref, ckv_ref, kr_ref, qag, kvag, qng, kng, wuq_ref, wk_ref, wv_ref, ct, s1, s2, dq_ref, dk_ref,
             dv_ref, dgl_ref, dud_ref, duf_ref, dub_ref, dp_ref, dqag, dkvag, dqng, dkng, dwuq, dwk, dwv):
        i = pl.program_id(0)
        lat = i < lat_tiles
        tables = (ct[...], s1[...], s2[...])
        fn = lambda *a: _qkv_fn(*a, *tables)
        _, vjp = jax.vjp(fn, cq_ref[...], ckv_ref[...], kr_ref[...], qag[...], kvag[...], qng[...], kng[...],
                         wuq_ref[...].astype(F32), wk_ref[...].astype(F32), wv_ref[...].astype(F32))
        g = vjp((jnp.where(lat, dq_ref[...], 0.0), dk_ref[...], dv_ref[...]))
        dp_ref[:, 0:2 * d] = jnp.where(lat, dgl_ref[...], 0.0).astype(BF16)
        dp_ref[:, lay['o_u']:lay['o_u'] + sw] = (duf_ref[...] + dub_ref[...]
                                                 + jnp.where(lat, dud_ref[...], 0.0)).astype(BF16)
        dp_ref[:, lay['o_ckv']:lay['o_ckv'] + kv_w] = g[1].astype(BF16)
        dp_ref[:, lay['o_kr']:lay['o_kr'] + LANES] = g[2].astype(BF16)
        hole0 = lay['o_kr'] + LANES
        if lay['o_cq'] > hole0:
            dp_ref[:, hole0:lay['o_cq']] = jnp.zeros((tm, lay['o_cq'] - hole0), BF16)
        dp_ref[:, lay['o_cq']:lay['o_cq'] + q_w] = g[0].astype(BF16)
        for ref, val in zip((dqag, dkvag, dqng, dkng), g[3:7]):
            _acc_add(i, ref, _pad_rows(val))
        for ref, val in zip((dwuq, dwk, dwv), g[7:10]):
            _acc_add(i, ref, val)

    def lat_rows(w):
        return pl.BlockSpec((tm, w), lambda i: (jnp.minimum(i, lat_tiles - 1), 0))

    acc_shapes = [(SUBLANES, a.shape[1]) for a in gains] + [wuq.shape, wk.shape, wv.shape]
    return pl.pallas_call(
        body, name=name, grid=(r // tm,),
        in_specs=[_rows(tm, q_w, lay['o_cq'] // q_w), _rows(tm, kv_w, lay['o_ckv'] // kv_w),
                  _rows(tm, LANES, lay['o_kr'] // LANES)]
        + [_full(a.shape) for a in gains] + [_full(wuq.shape), _full(wk.shape), _full(wv.shape)]
        + [_rows(tm, HEAD_PAD)] * 3 + [lat_rows(HEADS_W), _rows(tm, HEADS_W), _rows(tm, HEADS_W)]
        + [lat_rows(2 * d), lat_rows(sw), _rows(tm, sw), _rows(tm, sw)],
        out_specs=[_rows(tm, lay['width'])] + [_full(s) for s in acc_shapes],
        out_shape=[jax.ShapeDtypeStruct((r, lay['width']), BF16)] + [jax.ShapeDtypeStruct(s, F32) for s in acc_shapes],
        compiler_params=_cparams(("arbitrary",)),
    )(proj, proj, proj, *gains, wuq, wk, wv, *tabs, dq, dk, dv, dgl, du_direct, du_f, du_b)


def glu_fwd(proj, lay, yf, yb, dskip, wglu, n, tm, name):
    sw, d = wglu.shape[0], wglu.shape[1] // 2

    def body(u_ref, yf_ref, yb_ref, ds_ref, w_ref, o_ref):
        o_ref[...] = _glu_fn(u_ref[...], yf_ref[...], yb_ref[...], ds_ref[...], w_ref[...])

    return pl.pallas_call(
        body, name=name, grid=(n // tm,),
        in_specs=[_rows(tm, sw, lay['o_u'] // sw), _rows(tm, sw), _rows(tm, sw), _full(dskip.shape),
                  _full(wglu.shape)],
        out_specs=_rows(tm, d), out_shape=jax.ShapeDtypeStruct((n, d), F32),
        compiler_params=_cparams(("parallel",)),
    )(proj, yf, yb, dskip, wglu)


def glu_bwd(proj, lay, yf, yb, dskip, wglu, ds_l, n, tm, name):
    sw, d = wglu.shape[0], wglu.shape[1] // 2

    def body(u_ref, yf_ref, yb_ref, ds_ref, w_ref, g_ref, du_ref, dy_ref, dds_ref, dw_ref):
        i = pl.program_id(0)
        _, vjp = jax.vjp(_glu_fn, u_ref[...], yf_ref[...], yb_ref[...], ds_ref[...], w_ref[...].astype(F32))
        du, dyf, _, dds, dw = vjp(g_ref[...])
        du_ref[...] = du
        dy_ref[...] = dyf
        _acc_add(i, dds_ref, _pad_rows(dds))
        _acc_add(i, dw_ref, dw)

    return pl.pallas_call(
        body, name=name, grid=(n // tm,),
        in_specs=[_rows(tm, sw, lay['o_u'] // sw), _rows(tm, sw), _rows(tm, sw), _full(dskip.shape),
                  _full(wglu.shape), _rows(tm, d)],
        out_specs=[_rows(tm, sw), _rows(tm, sw), _full((SUBLANES, sw)), _full(wglu.shape)],
        out_shape=[jax.ShapeDtypeStruct((n, sw), F32), jax.ShapeDtypeStruct((n, sw), F32),
                   jax.ShapeDtypeStruct((SUBLANES, sw), F32), jax.ShapeDtypeStruct(wglu.shape, F32)],
        compiler_params=_cparams(("arbitrary",)),
    )(proj, yf, yb, dskip, wglu, ds_l)


def merge_fwd(o, s_l, proj, xa, g1, wo, wout, n, tm, name):
    d = xa.shape[1]

    def body(o_ref, s_ref, gl_ref, x_ref, g1_ref, wo_ref, wout_ref, x1_ref):
        x1_ref[...] = _merge_fn(o_ref[...], s_ref[...], gl_ref[...], x_ref[...], g1_ref[...], wo_ref[...],
                                wout_ref[...])

    return pl.pallas_call(
        body, name=name, grid=(n // tm,),
        in_specs=[_rows(tm, HEADS_W), _rows(tm, d), _rows(tm, 2 * d), _rows(tm, d), _full(g1.shape),
                  _full(wo.shape), _full(wout.shape)],
        out_specs=_rows(tm, d), out_shape=jax.ShapeDtypeStruct((n, d), F32),
        compiler_params=_cparams(("parallel",)),
    )(o, s_l, proj, xa, g1, wo, wout)


def merge_bwd(o, s_l, proj, xa, g1, wo, wout, dx1, n, tm, name):
    d = xa.shape[1]

    def body(o_ref, s_ref, gl_ref, x_ref, g1_ref, wo_ref, wout_ref, dx1_ref, do_ref, ds_ref, dgl_ref, dg1_ref,
             dwo_ref, dwout_ref):
        i = pl.program_id(0)
        _, vjp = jax.vjp(_merge_fn, o_ref[...], s_ref[...], gl_ref[...], x_ref[...], g1_ref[...],
                         wo_ref[...].astype(F32), wout_ref[...].astype(F32))
        do, ds, dgl, _, dg1, dwo, dwout = vjp(dx1_ref[...])
        do_ref[...] = do
        ds_ref[...] = ds
        dgl_ref[...] = dgl
        _acc_add(i, dg1_ref, _pad_rows(dg1))
        _acc_add(i, dwo_ref, dwo)
        _acc_add(i, dwout_ref, dwout)

    return pl.pallas_call(
        body, name=name, grid=(n // tm,),
        in_specs=[_rows(tm, HEADS_W), _rows(tm, d), _rows(tm, 2 * d), _rows(tm, d), _full(g1.shape),
                  _full(wo.shape), _full(wout.shape), _rows(tm, d)],
        out_specs=[_rows(tm, HEADS_W), _rows(tm, d), _rows(tm, 2 * d), _full((SUBLANES, d)), _full(wo.shape),
                   _full(wout.shape)],
        out_shape=[jax.ShapeDtypeStruct((n, HEADS_W), BF16), jax.ShapeDtypeStruct((n, d), F32),
                   jax.ShapeDtypeStruct((n, 2 * d), F32), jax.ShapeDtypeStruct((SUBLANES, d), F32),
                   jax.ShapeDtypeStruct(wo.shape, F32), jax.ShapeDtypeStruct(wout.shape, F32)],
        compiler_params=_cparams(("arbitrary",)),
    )(o, s_l, proj, xa, g1, wo, wout, dx1)


def _halo_specs(tm, w, n):
    nb = n // SUBLANES
    per = tm // SUBLANES
    prev = pl.BlockSpec((SUBLANES, w), lambda i: (jnp.maximum(i * per - 1, 0), 0))
    nxt = pl.BlockSpec((SUBLANES, w), lambda i: (jnp.minimum((i + 1) * per, nb - 1), 0))
    return prev, nxt


def _shifted(t, prev_blk, next_blk, i, n_tiles):
    tm = t.shape[0]
    row = lax.broadcasted_iota(jnp.int32, t.shape, 0)
    prev_row = jnp.where(i > 0, prev_blk[SUBLANES - 1:SUBLANES, :], 0.0)
    next_row = jnp.where(i < n_tiles - 1, next_blk[0:1, :], 0.0)
    before = jnp.where(row == 0, prev_row, pltpu.roll(t, 1, 0))
    after = jnp.where(row == tm - 1, next_row, pltpu.roll(t, tm - 1, 0))
    return before, after


def _conv_u2(up, before, after, cw, cb):
    return before * cw[0:1, :] + up * cw[1:2, :] + after * cw[2:3, :] + cb


def conv_act_fwd(up, cw, cb, tm, name):
    n, w2 = up.shape
    f = w2 // 2
    n_tiles = n // tm
    prev, nxt = _halo_specs(tm, w2, n)

    def body(up_ref, prev_ref, next_ref, cw_ref, cb_ref, act_ref):
        i = pl.program_id(0)
        t = up_ref[...]
        before, after = _shifted(t, prev_ref[...], next_ref[...], i, n_tiles)
        u2 = _conv_u2(t, before, after, cw_ref[...], cb_ref[...])
        act_ref[...] = (jax.nn.silu(u2[:, f:]) * u2[:, :f]).astype(BF16)

    return pl.pallas_call(
        body, name=name, grid=(n_tiles,),
        in_specs=[_rows(tm, w2), prev, nxt, _full(cw.shape), _full(cb.shape)],
        out_specs=_rows(tm, f), out_shape=jax.ShapeDtypeStruct((n, f), BF16),
        compiler_params=_cparams(("parallel",)),
    )(up, up, up, cw, cb)


def down_loss(act, wdown, x1, g2, target, tm, name):
    n, d = x1.shape
    f = act.shape[1]

    def body(act_ref, w_ref, x1_ref, g2_ref, t_ref, dy_ref, ffn_ref, loss_ref):
        ffn = jnp.dot(act_ref[...], w_ref[...], preferred_element_type=F32)
        err = x1_ref[...] + g2_ref[...] * ffn - t_ref[...]
        ffn_ref[...] = ffn
        dy_ref[...] = err * (1.0 / d)
        part = 0.5 * jnp.sum(jnp.sum(err * err, axis=-1, keepdims=True) * (1.0 / d), axis=0, keepdims=True)
        loss_ref[0] = jnp.broadcast_to(part, (SUBLANES, LANES))

    return pl.pallas_call(
        body, name=name, grid=(n // tm,),
        in_specs=[_rows(tm, f), _full(wdown.shape), _rows(tm, d), _full(g2.shape), _rows(tm, d)],
        out_specs=[_rows(tm, d), _rows(tm, d), pl.BlockSpec((1, SUBLANES, LANES), lambda i: (i, 0, 0))],
        out_shape=[jax.ShapeDtypeStruct((n, d), F32), jax.ShapeDtypeStruct((n, d), F32),
                   jax.ShapeDtypeStruct((n // tm, SUBLANES, LANES), F32)],
        compiler_params=_cparams(("parallel",)),
    )(act, wdown, x1, g2, target)


def down_bwd(dy, ffn, up, cw, cb, wdown, g2, tm, name):
    n, d = dy.shape
    w2 = up.shape[1]
    f = w2 // 2
    n_tiles = n // tm
    prev, nxt = _halo_specs(tm, w2, n)

    def body(dy_ref, ffn_ref, up_ref, prev_ref, next_ref, cw_ref, cb_ref, w_ref, g2_ref, du2_ref, dffn_ref,
             dg2_ref):
        i = pl.program_id(0)
        dyv = dy_ref[...]
        dffn = (dyv * g2_ref[...]).astype(BF16)
        dffn_ref[...] = dffn
        _acc_add(i, dg2_ref, _pad_rows(jnp.sum(dyv * ffn_ref[...], axis=0, keepdims=True)))
        dact = _dot_nt(dffn, w_ref[...])
        t = up_ref[...]
        before, after = _shifted(t, prev_ref[...], next_ref[...], i, n_tiles)
        u2 = _conv_u2(t, before, after, cw_ref[...], cb_ref[...])
        val, gate = u2[:, :f], u2[:, f:]
        sg = jax.nn.sigmoid(gate)
        du2_ref[:, :f] = dact * (gate * sg)
        du2_ref[:, f:] = dact * val * (sg * (1.0 + gate * (1.0 - sg)))

    return pl.pallas_call(
        body, name=name, grid=(n_tiles,),
        in_specs=[_rows(tm, d), _rows(tm, d), _rows(tm, w2), prev, nxt, _full(cw.shape), _full(cb.shape),
                  _full(wdown.shape), _full(g2.shape)],
        out_specs=[_rows(tm, w2), _rows(tm, d), _full((SUBLANES, d))],
        out_shape=[jax.ShapeDtypeStruct((n, w2), F32), jax.ShapeDtypeStruct((n, d), BF16),
                   jax.ShapeDtypeStruct((SUBLANES, d), F32)],
        compiler_params=_cparams(("arbitrary",)),
    )(dy, ffn, up, up, up, cw, cb, wdown, g2)


def conv_bwd(du2, up, cw, tm, name):
    n, w2 = up.shape
    n_tiles = n // tm
    prev, nxt = _halo_specs(tm, w2, n)

    def body(d_ref, prev_ref, next_ref, up_ref, cw_ref, dup_ref, dcw_ref):
        i = pl.program_id(0)
        dv = d_ref[...]
        before, after = _shifted(dv, prev_ref[...], next_ref[...], i, n_tiles)
        cwv = cw_ref[...]
        dup_ref[...] = (after * cwv[0:1, :] + dv * cwv[1:2, :] + before * cwv[2:3, :]).astype(BF16)
        t = up_ref[...]
        row = lax.broadcasted_iota(jnp.int32, (SUBLANES, w2), 0)
        upd = jnp.zeros((SUBLANES, w2), F32)
        for k, term in enumerate((after * t, dv * t, before * t, dv)):
            upd = upd + jnp.where(row == k, jnp.broadcast_to(jnp.sum(term, axis=0, keepdims=True),
                                                             (SUBLANES, w2)), 0.0)
        _acc_add(i, dcw_ref, upd)

    return pl.pallas_call(
        body, name=name, grid=(n_tiles,),
        in_specs=[_rows(tm, w2), prev, nxt, _rows(tm, w2), _full(cw.shape)],
        out_specs=[_rows(tm, w2), _full((SUBLANES, w2))],
        out_shape=[jax.ShapeDtypeStruct((n, w2), BF16), jax.ShapeDtypeStruct((SUBLANES, w2), F32)],
        compiler_params=_cparams(("arbitrary",)),
    )(du2, du2, du2, up, cw)


def attn_fwd(q, k, v, n, name):
    nk = k.shape[0]
    tq = _tile(n, (512, 256, 128))
    tk = _tile(nk, (768, 384, 256, 128))
    n_kv = nk // tk
    scale = QK_DIM ** -0.5
    c2 = scale * math.log2(math.e)

    def body(q_ref, k_ref, v_ref, o_ref, lse_ref):
        qv = q_ref[...]
        ones_col = (lax.broadcasted_iota(jnp.int32, (tk, HEAD_PAD), 1) == V_DIM).astype(BF16)

        def chunk(j, carry):
            m, acc = carry
            rows = pl.ds(pl.multiple_of(j * tk, tk), tk)
            s = _dot_nt(qv, k_ref[rows, :])
            m_new = jnp.maximum(m, jnp.max(s, axis=-1, keepdims=True))
            p = jnp.exp2(s * c2 - m_new * c2)
            alpha = jnp.exp2((m - m_new) * c2)
            pv = jnp.dot(p.astype(BF16), v_ref[rows, :] + ones_col, preferred_element_type=F32)
            return m_new, alpha * acc + pv

        m, acc = lax.fori_loop(0, n_kv, chunk, (jnp.full((tq, 1), -jnp.inf, F32),
                                                jnp.zeros((tq, HEAD_PAD), F32)), unroll=True)
        l = acc[:, V_DIM:V_DIM + 1]
        lane = lax.broadcasted_iota(jnp.int32, (tq, HEAD_PAD), 1)
        o_ref[...] = jnp.where(lane < V_DIM, acc / l, 0.0).astype(BF16)
        lse_ref[...] = jnp.broadcast_to(m * scale + jnp.log(l), (tq, HEAD_PAD))

    qspec = pl.BlockSpec((tq, HEAD_PAD), lambda h, i: (i, h))
    kspec = pl.BlockSpec((nk, HEAD_PAD), lambda h, i: (0, h))
    return pl.pallas_call(
        body, name=name, grid=(N_HEADS, n // tq),
        in_specs=[qspec, kspec, kspec], out_specs=[qspec, qspec],
        out_shape=[jax.ShapeDtypeStruct((n, HEADS_W), BF16), jax.ShapeDtypeStruct((n, HEADS_W), F32)],
        compiler_params=_cparams(("parallel", "parallel")),
    )(q, k, v)


def attn_bwd(q, k, v, o, do, lse, n, name):
    nk = k.shape[0]
    tq = _tile(n, (512, 256, 128))
    tk = _tile(nk, (768, 384, 256, 128))
    scale = QK_DIM ** -0.5
    log2e = math.log2(math.e)
    c2 = scale * log2e

    def body(q_ref, k_ref, v_ref, o_ref, do_ref, lse_ref, dq_ref, dk_ref, dv_ref):
        @pl.when(pl.program_id(1) == 0)
        def _():
            dq_ref[...] = jnp.zeros_like(dq_ref)

        kv, vv = k_ref[...], v_ref[...]

        def q_tile(i, carry):
            dk, dv = carry
            rows = pl.ds(pl.multiple_of(i * tq, tq), tq)
            qv, dov = q_ref[rows, :], do_ref[rows, :]
            p = jnp.exp2(_dot_nt(qv, kv) * c2 - lse_ref[rows, 0:1] * log2e)
            dv = dv + _dot_tn(p.astype(BF16), dov)
            dp = _dot_nt(dov, vv)
            delta = jnp.sum(dov.astype(F32) * o_ref[rows, :].astype(F32), axis=-1, keepdims=True)
            ds = (p * (dp - delta) * scale).astype(BF16)
            dk = dk + _dot_tn(ds, qv)
            dq_ref[rows, :] += jnp.dot(ds, kv, preferred_element_type=F32)
            return dk, dv

        zero = jnp.zeros((tk, HEAD_PAD), F32)
        dk, dv = lax.fori_loop(0, n // tq, q_tile, (zero, zero), unroll=2)
        dk_ref[...] = dk
        dv_ref[...] = dv

    qspec = pl.BlockSpec((n, HEAD_PAD), lambda h, j: (0, h))
    kspec = pl.BlockSpec((tk, HEAD_PAD), lambda h, j: (j, h))
    return pl.pallas_call(
        body, name=name, grid=(N_HEADS, nk // tk),
        in_specs=[qspec, kspec, kspec, qspec, qspec, qspec],
        out_specs=[qspec, kspec, kspec],
        out_shape=[jax.ShapeDtypeStruct((n, HEADS_W), F32), jax.ShapeDtypeStruct((nk, HEADS_W), F32),
                   jax.ShapeDtypeStruct((nk, HEADS_W), F32)],
        compiler_params=_cparams(("parallel", "arbitrary")),
    )(q, k, v, o, do, lse)


SCAN_LEVELS = (1, 2, 4)
SCAN_LANES = 512


def _scan_chunk(xr, xi, car, m_ref, p_ref, reverse):
    t_len, gn = xr.shape
    n_slab = t_len // SUBLANES
    for lb in range(gn // SCAN_LANES):
        ls = pl.ds(lb * SCAN_LANES, SCAN_LANES)

        def step(s, carry, ls=ls):
            cr, ci = carry
            slab = (n_slab - 1 - s) if reverse else s
            rows = pl.ds(pl.multiple_of(slab * SUBLANES, SUBLANES), SUBLANES)
            br, bi = xr[rows, ls], xi[rows, ls]
            for lvl, d in enumerate(SCAN_LEVELS):
                shift = SUBLANES - d if reverse else d
                sr, si = pltpu.roll(br, shift, 0), pltpu.roll(bi, shift, 0)
                mr, mi = m_ref[lvl, 0, :, ls], m_ref[lvl, 1, :, ls]
                br, bi = br + mr * sr - mi * si, bi + mr * si + mi * sr
            pr, pi = p_ref[0, :, ls], p_ref[1, :, ls]
            br, bi = br + pr * cr - pi * ci, bi + pr * ci + pi * cr
            xr[rows, ls] = br
            xi[rows, ls] = bi
            last = 0 if reverse else SUBLANES - 1
            return br[last:last + 1, :], bi[last:last + 1, :]

        cr, ci = lax.fori_loop(0, n_slab, step, (car[0:1, ls], car[1:2, ls]))
        car[0:1, ls] = cr
        car[1:2, ls] = ci


def _seq_block(step, n_chunk, lat_chunks, reverse):
    if reverse:
        return n_chunk - 1 - step
    return (step + lat_chunks) % n_chunk


def ssm_fwd(proj, lay, bre, bim, cre, ncim, tabs, n, t_len, reverse, name):
    l, sw = proj.shape[0], lay['sw']
    gn = bre.shape[-1]
    n_chunk, lat_chunks = l // t_len, n // t_len
    blk = lambda k: _seq_block(k, n_chunk, lat_chunks, reverse)
    mtab, ptab = tabs

    def body(u_ref, bre_ref, bim_ref, cre_ref, ncim_ref, m_ref, p_ref, y_ref, xb_ref, xr, xi, car):
        @pl.when(pl.program_id(0) == 0)
        def _():
            car[...] = jnp.zeros_like(car)

        xb_ref[...] = car[...]
        u = u_ref[...].astype(BF16)
        xr[...] = jnp.dot(u, bre_ref[...], preferred_element_type=F32)
        xi[...] = jnp.dot(u, bim_ref[...], preferred_element_type=F32)
        _scan_chunk(xr, xi, car, m_ref, p_ref, reverse)
        y_ref[...] = (jnp.dot(xr[...].astype(BF16), cre_ref[...], preferred_element_type=F32)
                      + jnp.dot(xi[...].astype(BF16), ncim_ref[...], preferred_element_type=F32))

    return pl.pallas_call(
        body, name=name, grid=(n_chunk,),
        in_specs=[pl.BlockSpec((t_len, sw), lambda k: (blk(k), lay['o_u'] // sw)), _full(bre.shape),
                  _full(bim.shape), _full(cre.shape), _full(ncim.shape), _full(mtab.shape), _full(ptab.shape)],
        out_specs=[pl.BlockSpec((t_len, sw), lambda k: (blk(k), 0)),
                   pl.BlockSpec((None, 2, gn), lambda k: (k, 0, 0))],
        out_shape=[jax.ShapeDtypeStruct((l, sw), F32), jax.ShapeDtypeStruct((n_chunk, 2, gn), F32)],
        scratch_shapes=[pltpu.VMEM((t_len, gn), F32), pltpu.VMEM((t_len, gn), F32), pltpu.VMEM((2, gn), F32)],
        compiler_params=_cparams(("arbitrary",)),
    )(proj, bre, bim, cre, ncim, mtab, ptab)


def ssm_bwd(proj, lay, dyr, xb, bre, bim, cre, ncim, tabs, adj_tabs, n, t_len, reverse, name):
    l, sw = proj.shape[0], lay['sw']
    gn = bre.shape[-1]
    n_chunk, lat_chunks = l // t_len, n // t_len
    fwd_step = lambda k: n_chunk - 1 - k
    blk = lambda k: _seq_block(fwd_step(k), n_chunk, lat_chunks, reverse)
    (mtab, ptab), (mtab_r, ptab_r) = tabs, adj_tabs

    def body(u_ref, dy_ref, xb_ref, bre_ref, bim_ref, cre_ref, ncim_ref, m_ref, p_ref, mr_ref, pr_ref,
             du_ref, gr_ref, gi_ref, xr_ref, xi_ref, dlam_ref, xr, xi, gr, gi, car, acar):
        k = pl.program_id(0)

        @pl.when(k == 0)
        def _():
            acar[...] = jnp.zeros_like(acar)
            dlam_ref[...] = jnp.zeros_like(dlam_ref)

        u = u_ref[...].astype(BF16)
        dy = jnp.where(blk(k) < lat_chunks, dy_ref[...], 0.0).astype(BF16)
        xr[...] = jnp.dot(u, bre_ref[...], preferred_element_type=F32)
        xi[...] = jnp.dot(u, bim_ref[...], preferred_element_type=F32)
        car[...] = xb_ref[...]
        _scan_chunk(xr, xi, car, m_ref, p_ref, reverse)
        gr[...] = _dot_nt(dy, cre_ref[...])
        gi[...] = _dot_nt(dy, ncim_ref[...])
        _scan_chunk(gr, gi, acar, mr_ref, pr_ref, not reverse)
        xrv, xiv, grv, giv = xr[...], xi[...], gr[...], gi[...]
        row = lax.broadcasted_iota(jnp.int32, (t_len, gn), 0)
        first, shift = (t_len - 1, t_len - 1) if reverse else (0, 1)
        xpr = jnp.where(row == first, xb_ref[0:1, :], pltpu.roll(xrv, shift, 0))
        xpi = jnp.where(row == first, xb_ref[1:2, :], pltpu.roll(xiv, shift, 0))
        dlr = grv * xpr + giv * xpi
        dli = giv * xpr - grv * xpi
        dlam_ref[0] += jnp.sum(dlr.reshape(t_len // SUBLANES, SUBLANES, gn), axis=0)
        dlam_ref[1] += jnp.sum(dli.reshape(t_len // SUBLANES, SUBLANES, gn), axis=0)
        grb, gib = grv.astype(BF16), giv.astype(BF16)
        du_ref[...] = _dot_nt(grb, bre_ref[...]) + _dot_nt(gib, bim_ref[...])
        gr_ref[...] = grb
        gi_ref[...] = gib
        xr_ref[...] = xrv.astype(BF16)
        xi_ref[...] = xiv.astype(BF16)

    def at_blk(width, col=0):
        return pl.BlockSpec((t_len, width), lambda k: (blk(k), col))

    state = jax.ShapeDtypeStruct((l, gn), BF16)
    return pl.pallas_call(
        body, name=name, grid=(n_chunk,),
        in_specs=[at_blk(sw, lay['o_u'] // sw),
                  pl.BlockSpec((t_len, sw), lambda k: (jnp.minimum(blk(k), lat_chunks - 1), 0)),
                  pl.BlockSpec((None, 2, gn), lambda k: (fwd_step(k), 0, 0)),
                  _full(bre.shape), _full(bim.shape), _full(cre.shape), _full(ncim.shape), _full(mtab.shape),
                  _full(ptab.shape), _full(mtab_r.shape), _full(ptab_r.shape)],
        out_specs=[at_blk(sw), at_blk(gn), at_blk(gn), at_blk(gn), at_blk(gn), _full((2, SUBLANES, gn))],
        out_shape=[jax.ShapeDtypeStruct((l, sw), F32), state, state, state, state,
                   jax.ShapeDtypeStruct((2, SUBLANES, gn), F32)],
        scratch_shapes=[pltpu.VMEM((t_len, gn), F32)] * 4 + [pltpu.VMEM((2, gn), F32)] * 2,
        compiler_params=_cparams(("arbitrary",)),
    )(proj, dyr, xb, bre, bim, cre, ncim, mtab, ptab, mtab_r, ptab_r)


def ssm_prep(lam_re, lam_im, logdt, bre, bim, name):
    gn = lam_re.shape[1]

    def body(lr_ref, li_ref, dt_ref, br_ref, bi_ref, pwr_ref, pwi_ref, bbr_ref, bbi_ref):
        _, _, bbr, bbi = _ssm_prep_fn(lr_ref[...], li_ref[...], dt_ref[...], br_ref[...], bi_ref[...])
        bbr_ref[...] = bbr
        bbi_ref[...] = bbi
        kk = (lax.broadcasted_iota(jnp.int32, (SUBLANES, gn), 0) + 1).astype(F32)
        dt = jnp.exp(dt_ref[...])
        ar, ai = lr_ref[...] * dt * kk, li_ref[...] * dt * kk
        e = jnp.exp(ar)
        pwr_ref[...] = e * jnp.cos(ai)
        pwi_ref[...] = e * jnp.sin(ai)

    ins = (lam_re, lam_im, logdt, bre, bim)
    return pl.pallas_call(
        body, name=name,
        out_shape=[jax.ShapeDtypeStruct((SUBLANES, gn), F32)] * 2 + [jax.ShapeDtypeStruct(bre.shape, F32)] * 2,
        compiler_params=_cparams(),
    )(*ins)


def ssm_prep_bwd(lam_re, lam_im, logdt, bre, bim, dlbr, dlbi, dbbr, dbbi, name):
    def body(lr_ref, li_ref, dt_ref, br_ref, bi_ref, g0, g1, g2, g3, o0, o1, o2, o3, o4):
        _, vjp = jax.vjp(_ssm_prep_fn, lr_ref[...], li_ref[...], dt_ref[...], br_ref[...], bi_ref[...])
        for ref, val in zip((o0, o1, o2, o3, o4), vjp((g0[...], g1[...], g2[...], g3[...]))):
            ref[...] = val

    ins = (lam_re, lam_im, logdt, bre, bim)
    return pl.pallas_call(
        body, name=name,
        out_shape=[jax.ShapeDtypeStruct(a.shape, F32) for a in ins],
        compiler_params=_cparams(),
    )(*ins, dlbr, dlbi, dbbr, dbbi)


def mod_fwd(cmat, w, b, name):
    def body(c_ref, w_ref, b_ref, o_ref):
        o_ref[...] = _mod_fn(c_ref[...], w_ref[...]) + b_ref[...]

    return pl.pallas_call(body, name=name, out_shape=jax.ShapeDtypeStruct((cmat.shape[0], w.shape[1]), F32),
                          compiler_params=_cparams())(cmat, w, b)


def mod_bwd(cmat, w, g_lat, g_ctx, g_all, name):
    def body(c_ref, w_ref, gl_ref, gc_ref, ga_ref, dw_ref, dc_ref, db_ref):
        gc = jnp.sum(gc_ref[...], axis=0, keepdims=True)
        dm = jnp.concatenate([gl_ref[...], _pad_rows(gc)], axis=0)
        _, vjp = jax.vjp(_mod_fn, c_ref[...], w_ref[...])
        dc, dw = vjp(dm)
        dw_ref[...] = dw
        dc_ref[...] = dc
        db_ref[...] = _pad_rows(jnp.sum(ga_ref[...], axis=0, keepdims=True))

    return pl.pallas_call(
        body, name=name,
        out_shape=[jax.ShapeDtypeStruct(w.shape, F32), jax.ShapeDtypeStruct(cmat.shape, F32),
                   jax.ShapeDtypeStruct((SUBLANES, g_all.shape[1]), F32)],
        compiler_params=_cparams(),
    )(cmat, w, g_lat, g_ctx, g_all)


def add_pair(pair, out_dtype, name):
    _, s, r, c = pair.shape

    def body(p_ref, o_ref):
        o_ref[...] = (p_ref[0] + p_ref[1]).astype(o_ref.dtype)

    return pl.pallas_call(
        body, name=name, grid=(s,),
        in_specs=[pl.BlockSpec((2, None, r, c), lambda k: (0, k, 0, 0))],
        out_specs=pl.BlockSpec((None, r, c), lambda k: (k, 0, 0)),
        out_shape=jax.ShapeDtypeStruct((s, r, c), out_dtype),
        compiler_params=_cparams(("parallel",)),
    )(pair)


def reduce_adamw(parts, w, m, v, name):
    s, r, c = parts.shape
    tr = _tile(r, (256, 128, 64, 32, 16, 8))

    def body(p_ref, w_ref, m_ref, v_ref, g_ref, d_ref, nm_ref, nv_ref):
        g = p_ref[0].astype(F32)
        for k in range(1, s):
            g = g + p_ref[k].astype(F32)
        mm = ADAM_B1 * m_ref[...] + (1.0 - ADAM_B1) * g
        vv = ADAM_B2 * v_ref[...] + (1.0 - ADAM_B2) * jnp.square(g)
        m_hat = mm / (1.0 - ADAM_B1 ** ADAM_STEP)
        v_hat = vv / (1.0 - ADAM_B2 ** ADAM_STEP)
        g_ref[...] = g
        d_ref[...] = -ADAM_LR * (m_hat / (jnp.sqrt(v_hat) + ADAM_EPS) + ADAM_WD * w_ref[...])
        nm_ref[...] = mm
        nv_ref[...] = vv

    blk = _rows(tr, c)
    out = jax.ShapeDtypeStruct((r, c), F32)
    return pl.pallas_call(
        body, name=name, grid=(r // tr,),
        in_specs=[pl.BlockSpec((s, tr, c), lambda i: (0, i, 0)), blk, blk, blk],
        out_specs=[blk] * 4, out_shape=[out] * 4,
        compiler_params=_cparams(("parallel",)),
    )(parts, w, m, v)


def _in_layout(d, q, kv, sw):
    o_u = 2 * d
    o_ckv = o_u + sw
    o_kr = o_ckv + kv
    o_cq = -(-(o_kr + LANES) // q) * q
    assert o_u % sw == 0 and o_ckv % kv == 0 and o_kr % LANES == 0
    assert q % LANES == 0 and kv % LANES == 0 and sw % LANES == 0
    return dict(d=d, q=q, kv=kv, sw=sw, o_gl=0, o_u=o_u, o_ckv=o_ckv, o_kr=o_kr, o_cq=o_cq, width=o_cq + q)


def _pad_w_in(w_in, lay):
    q, kv, sw, d = lay['q'], lay['kv'], lay['sw'], lay['d']
    cq, ckv, kr, u, gl = jnp.split(w_in, [q, q + kv, q + kv + QK_ROPE, q + kv + QK_ROPE + sw], axis=1)
    z = lambda w: jnp.zeros((w_in.shape[0], w), w_in.dtype)
    hole = lay['o_cq'] - lay['o_kr'] - LANES
    return jnp.concatenate([gl, u, ckv, z(QK_NOPE), kr, z(LANES - QK_DIM), z(hole), cq], axis=1)


def _unpad_w_in(g, lay):
    q, kv, sw, d = lay['q'], lay['kv'], lay['sw'], lay['d']
    kr0 = lay['o_kr'] + QK_NOPE
    return jnp.concatenate([g[:, lay['o_cq']:lay['o_cq'] + q], g[:, lay['o_ckv']:lay['o_ckv'] + kv],
                            g[:, kr0:kr0 + QK_ROPE], g[:, lay['o_u']:lay['o_u'] + sw], g[:, :2 * d]], axis=1)


def _pad_heads(w, width):
    k = w.shape[0]
    return jnp.pad(w.reshape(k, N_HEADS, width), ((0, 0), (0, 0), (0, HEAD_PAD - width))).reshape(k, HEADS_W)


def _unpad_heads(w, width):
    k = w.shape[0]
    return w.reshape(k, N_HEADS, HEAD_PAD)[:, :, :width].reshape(k, N_HEADS * width)


def _rope_tables(n, nc):
    rows = n // GRID_W
    row = jnp.repeat(jnp.arange(rows), GRID_W)
    col = jnp.tile(jnp.arange(GRID_W), rows)
    pairs = QK_ROPE // 4
    freqs = ROPE_THETA ** (-jnp.arange(pairs, dtype=F32) / pairs)
    ang = jnp.concatenate([row[:, None] * freqs, col[:, None] * freqs], axis=-1)
    cos = jnp.concatenate([jnp.cos(ang), jnp.ones((nc, 2 * pairs), F32)], axis=0)
    sin = jnp.concatenate([jnp.sin(ang), jnp.zeros((nc, 2 * pairs), F32)], axis=0)
    l = n + nc
    half = QK_ROPE // 2
    ct = jnp.concatenate([jnp.ones((l, QK_NOPE), F32), cos, cos, jnp.zeros((l, HEAD_PAD - QK_DIM), F32)], axis=1)
    s1 = jnp.concatenate([jnp.zeros((l, QK_NOPE + half), F32), sin, jnp.zeros((l, HEAD_PAD - QK_DIM), F32)],
                         axis=1)
    s2 = jnp.concatenate([jnp.zeros((l, QK_NOPE), F32), -sin, jnp.zeros((l, HEAD_PAD - QK_NOPE - half), F32)],
                         axis=1)
    return ct, s1, s2


def _block_diag(m):
    g, a, b = m.shape
    eye = jnp.eye(g, dtype=m.dtype)
    return (m[:, :, None, :] * eye[:, None, :, None]).reshape(g * a, g * b)


def _diag_blocks(m, g):
    a, b = m.shape[0] // g, m.shape[1] // g
    m4 = m.reshape(g, a, g, b)
    return jnp.sum(m4 * jnp.eye(g, dtype=m.dtype)[:, None, :, None], axis=2)


def _scan_tables(pwr, pwi, reverse):
    row = jnp.arange(SUBLANES)[:, None]
    zero = jnp.zeros_like(pwr)
    levels = []
    for d in SCAN_LEVELS:
        keep = (row < SUBLANES - d) if reverse else (row >= d)
        levels.append(jnp.stack([jnp.where(keep, pwr[d - 1:d, :], zero), jnp.where(keep, pwi[d - 1:d, :], zero)]))
    carry = jnp.stack([pwr[::-1], pwi[::-1]]) if reverse else jnp.stack([pwr, pwi])
    return jnp.stack(levels), carry


def _step(inp):
    x, c, ctx = inp['x'][0], inp['c'], inp['ctx'][0]
    target = inp['loss_target'][0]
    n, d = x.shape
    nc = ctx.shape[0]
    l = n + nc
    q_w, kv_w = inp['q_a_g'].shape[1], inp['kv_a_g'].shape[1]
    sw = inp['d_skip'].shape[1]
    n_grp = sw // SSM_GROUP
    gn = n_grp * SSM_STATE
    lay = _in_layout(d, q_w, kv_w, sw)
    tm = _tile(math.gcd(n, nc), (256, 128))
    me = 4 * lax.axis_index("x") + 2 * lax.axis_index("y") + lax.axis_index("c")
    strip = lambda a: a if a.ndim <= 2 else a[0]
    w = {k: strip(inp[k]) for k in WEIGHT_NAMES}

    gathered_names = list(GATHERED)
    c_all, *wg = gather_two_level([jnp.broadcast_to(c, (SUBLANES, d))] + [w[k].astype(BF16) for k in gathered_names],
                                  "gather_weights")
    full = {k: _from_shards(s, GATHERED[k]) for k, s in zip(gathered_names, wg)}

    w_in_p = _pad_w_in(full['w_in'], lay)
    wuq_p = _pad_heads(full['w_uq'], QK_DIM)
    ukv = full['w_ukv'].reshape(kv_w, N_HEADS, QK_NOPE + V_DIM)
    wk_p = _pad_heads(ukv[:, :, :QK_NOPE].reshape(kv_w, -1), QK_NOPE)
    wv_p = _pad_heads(ukv[:, :, QK_NOPE:].reshape(kv_w, -1), V_DIM)
    wo_p = _pad_heads(full['w_o_attn'].T, V_DIM).T
    w_glu, w_out, w_up, w_down = full['w_glu'], full['w_out'], full['w_up'], full['w_down']
    conv_w = jnp.pad(full['conv_w'].astype(F32), ((0, SUBLANES - 3), (0, 0)))
    conv_b = w['conv_b']
    f2 = w_up.shape[1]

    cmat = jnp.concatenate([c_all[:, 0, :], w['c_ctx'][None, :], jnp.zeros((SUBLANES - 1, d), F32)], axis=0)
    mcols = w['w_mod'].shape[1]
    b_cols = lax.dynamic_slice(w['b_mod'], (0, me * mcols), (1, mcols))
    mod_part = mod_fwd(cmat, w['w_mod'], b_cols, "mod_fwd")
    (mod_all,) = exchange([mod_part], ['gather'], "gather_mod")
    mod_me = lax.dynamic_index_in_dim(mod_all, me, axis=1, keepdims=False).reshape(6, d)
    mod_ctx = mod_all[:, SUBLANES, :].reshape(6, d)
    sh1, sc1, g1, sh2, sc2, g2 = [mod_me[k:k + 1] for k in range(6)]
    mods1 = jnp.concatenate([sc1, sh1, mod_ctx[1:2], mod_ctx[0:1], jnp.zeros((4, d), F32)], axis=0)
    mods2 = jnp.concatenate([sc2, sh2, jnp.zeros((6, d), F32)], axis=0)

    xa = jnp.concatenate([x, ctx], axis=0)
    h = norm_mod_fwd(xa, w['norm1_g'], mods1, n, tm, "norm1_fwd")
    proj = matmul(h, w_in_p, 'nn', F32, "in_proj")
    tabs = _rope_tables(n, nc)
    pad_g = lambda g: jnp.pad(g, ((0, 0), (0, HEAD_PAD - QK_DIM)))
    gains = (w['q_a_g'], w['kv_a_g'], pad_g(w['q_norm_g']), pad_g(w['k_norm_g']))
    q, k, v = qkv_fwd(proj, lay, gains, wuq_p, wk_p, wv_p, tabs, tm, "qkv_fwd")
    o, lse = attn_fwd(q, k, v, n, "attn_fwd")

    b_t = lambda a: a.transpose(2, 0, 1).reshape(SSM_GROUP, gn)
    c_t = lambda a: a.transpose(1, 0, 2).reshape(SSM_GROUP, gn)
    bre_t, bim_t = b_t(w['b_re']), b_t(w['b_im'])
    ssm_in, prep = [], []
    for sfx in ('f', 'b'):
        lam_re, lam_im = w['lam_re_' + sfx].reshape(1, gn), w['lam_im_' + sfx].reshape(1, gn)
        logdt = jnp.repeat(w['log_dt_' + sfx], SSM_STATE, axis=1)
        ssm_in.append((lam_re, lam_im, logdt))
        prep.append(ssm_prep(lam_re, lam_im, logdt, bre_t, bim_t, "ssm_prep_" + sfx))

    def blk_b(bb):
        return _block_diag(bb.reshape(SSM_GROUP, n_grp, SSM_STATE).transpose(1, 0, 2)).astype(BF16)

    def blk_c(cc):
        return _block_diag(cc.transpose(0, 2, 1)).astype(BF16)

    t_len = _tile(math.gcd(n, nc), (256, 128))
    ssm = []
    for di, sfx in enumerate(('f', 'b')):
        reverse = di == 1
        pwr, pwi, bbr, bbi = prep[di]
        ssm.append(dict(
            sfx=sfx, reverse=reverse, blocks=(blk_b(bbr), blk_b(bbi), blk_c(w['c_re_' + sfx]),
                                              blk_c(-w['c_im_' + sfx])),
            tabs=_scan_tables(pwr, pwi, reverse), adj_tabs=_scan_tables(pwr, -pwi, not reverse)))
    for s in ssm:
        s['y'], s['xb'] = ssm_fwd(proj, lay, *s['blocks'], s['tabs'], n, t_len, s['reverse'], "ssm_fwd_" + s['sfx'])
    yf, yb = ssm[0]['y'], ssm[1]['y']
    s_l = glu_fwd(proj, lay, yf, yb, w['d_skip'], w_glu, n, tm, "glu_fwd")
    x1 = merge_fwd(o, s_l, proj, xa, g1, wo_p, w_out, n, tm, "merge_fwd")

    h2 = norm_mod_fwd(x1, w['norm2_g'], mods2, n, tm, "norm2_fwd")
    up = matmul(h2, w_up, 'nn', F32, "up_proj")
    tw = _tile(n, (128,))
    act = conv_act_fwd(up, conv_w, conv_b, tw, "conv_act_fwd")
    dy, ffn, loss_parts = down_loss(act, w_down, x1, g2, target, tm, "down_loss")
    loss = lax.psum(jnp.sum(loss_parts[:, 0, 0]), MESH_AXES)

    du2, dffn, dg2 = down_bwd(dy, ffn, up, conv_w, conv_b, w_down, g2, tw, "down_bwd")
    g_w_down = matmul(act, dffn, 'tn', F32, "dw_down")
    dup, dconv = conv_bwd(du2, up, conv_w, tw, "conv_bwd")
    dh2 = matmul(dup, w_up, 'nt', F32, "dh2")
    g_w_up = matmul(h2, dup, 'tn', F32, "dw_up")
    dx1, dn2g, dmods2 = norm_mod_bwd(x1, dh2, dy, w['norm2_g'], mods2, n, tm, "norm2_bwd")

    do, ds_l, dgl, dg1, g_wo_p, g_w_out = merge_bwd(o, s_l, proj, xa, g1, wo_p, w_out, dx1, n, tm, "merge_bwd")
    du_direct, dyr, dds, g_w_glu = glu_bwd(proj, lay, yf, yb, w['d_skip'], w_glu, ds_l, n, tm, "glu_bwd")
    for s in ssm:
        s['du'], s['gr'], s['gi'], s['xr'], s['xi'], s['dlam'] = ssm_bwd(
            proj, lay, dyr, s['xb'], *s['blocks'], s['tabs'], s['adj_tabs'], n, t_len, s['reverse'],
            "ssm_bwd_" + s['sfx'])

    dq, dk, dv = attn_bwd(q, k, v, o, do, lse, n, "attn_bwd")
    (dproj, dqag, dkvag, dqng, dkng, g_wuq_p, g_wk_p, g_wv_p) = qkv_bwd(
        proj, lay, gains, wuq_p, wk_p, wv_p, tabs, dq, dk, dv, dgl, du_direct, ssm[0]['du'], ssm[1]['du'], n, tm,
        "qkv_bwd")
    dh = matmul(dproj, w_in_p, 'nt', F32, "dh")
    g_w_in_p = matmul(h, dproj, 'tn', F32, "dw_in")
    dxa, dn1g, dmods1 = norm_mod_bwd(xa, dh, dx1, w['norm1_g'], mods1, n, tm, "norm1_bwd")
    grad_x = dxa[:n]

    grads = {}
    d_bbar = [None, None]
    for di, s in enumerate(ssm):
        sfx = s['sfx']
        u_cols = dict(a_col0=lay['o_u'], a_cols=sw)
        g_bre = _diag_blocks(matmul(proj, s['gr'], 'tn', F32, "ssm_db_re_" + sfx, **u_cols), n_grp)
        g_bim = _diag_blocks(matmul(proj, s['gi'], 'tn', F32, "ssm_db_im_" + sfx, **u_cols), n_grp)
        g_cre = _diag_blocks(matmul(dyr, s['xr'], 'tn', F32, "ssm_dc_re_" + sfx), n_grp)
        g_cim = _diag_blocks(matmul(dyr, s['xi'], 'tn', F32, "ssm_dc_im_" + sfx), n_grp)
        grads['c_re_' + sfx] = g_cre
        grads['c_im_' + sfx] = -g_cim
        to_t = lambda a: a.transpose(1, 0, 2).reshape(SSM_GROUP, gn)
        dlam_re = jnp.sum(s['dlam'][0], axis=0, keepdims=True)
        dlam_im = jnp.sum(s['dlam'][1], axis=0, keepdims=True)
        lam_re, lam_im, logdt = ssm_in[di]
        g_lr, g_li, g_dt, g_br, g_bi = ssm_prep_bwd(lam_re, lam_im, logdt, bre_t, bim_t, dlam_re, dlam_im,
                                                    to_t(g_bre), to_t(g_bim), "ssm_prep_bwd_" + sfx)
        grads['lam_re_' + sfx] = g_lr.reshape(n_grp, SSM_STATE)
        grads['lam_im_' + sfx] = g_li.reshape(n_grp, SSM_STATE)
        grads['log_dt_' + sfx] = jnp.sum(g_dt.reshape(n_grp, SSM_STATE), axis=1)[None, :]
        d_bbar[di] = (g_br, g_bi)
    from_t = lambda a: a.reshape(SSM_GROUP, n_grp, SSM_STATE).transpose(1, 2, 0)
    grads['b_re'] = from_t(d_bbar[0][0]) + from_t(d_bbar[1][0])
    grads['b_im'] = from_t(d_bbar[0][1]) + from_t(d_bbar[1][1])

    dmod = jnp.concatenate([dmods1[1:2], dmods1[0:1], dg1[0:1], dmods2[1:2], dmods2[0:1], dg2[0:1]], axis=1)
    dmod_ctx = jnp.concatenate([dmods1[3:4], dmods1[2:3], jnp.zeros((1, 4 * d), F32)], axis=1)
    dm_send = jnp.concatenate([dmod, dmod_ctx, jnp.zeros((SUBLANES - 2, 6 * d), F32)], axis=0)
    (dm_all,) = exchange([dm_send], ['gather'], "gather_dmod")
    g_all = jnp.concatenate([dm_all[:, 0, :], dm_all[:, 1, :]], axis=0)
    cols = lax.dynamic_slice(g_all.reshape(2 * N_DEV, N_DEV, mcols), (0, me, 0), (2 * N_DEV, 1, mcols))[:, 0, :]
    g_w_mod, dcmat, g_b_mod = mod_bwd(cmat, w['w_mod'], cols[:N_DEV], cols[N_DEV:], g_all, "mod_bwd")

    ukv_g = jnp.concatenate([_unpad_heads(g_wk_p, QK_NOPE).reshape(kv_w, N_HEADS, QK_NOPE),
                             _unpad_heads(g_wv_p, V_DIM).reshape(kv_w, N_HEADS, V_DIM)], axis=2)
    full_g = {'w_in': _unpad_w_in(g_w_in_p, lay), 'w_uq': _unpad_heads(g_wuq_p, QK_DIM),
              'w_ukv': ukv_g.reshape(kv_w, -1), 'w_o_attn': _unpad_heads(g_wo_p.T, V_DIM).T, 'w_glu': g_w_glu,
              'w_out': g_w_out, 'w_up': g_w_up, 'conv_w': dconv[0:3], 'w_down': g_w_down}
    def by_core(full_grad, axis):
        sh = _to_shards(full_grad, axis)
        return sh.reshape((N_CHIPS, 2) + sh.shape[1:]).swapaxes(0, 1)

    sends = [by_core(full_g[k], GATHERED[k]) for k in gathered_names]
    grads.update({'c_ctx': dcmat[SUBLANES], 'b_mod': g_b_mod[0:1], 'norm1_g': dn1g[0:1], 'norm2_g': dn2g[0:1],
                  'q_a_g': dqag[0:1], 'kv_a_g': dkvag[0:1], 'q_norm_g': dqng[0:1, :QK_DIM],
                  'k_norm_g': dkng[0:1, :QK_DIM], 'd_skip': dds[0:1], 'conv_b': dconv[3:4]})
    first = (me == 0).astype(F32)
    rep_parts = [grads[k] * first if k == 'b_mod' else grads[k] for k in REPLICATED]
    rpack, rspans = _pack(rep_parts, SUBLANES)
    *pairs, r_pair = exchange(sends + [rpack], ['a2a'] * len(sends) + ['gather'], "grads_in_chip", group='core')
    chip_sums = [add_pair(p, BF16, "chip_sum_" + k) for p, k in zip(pairs, gathered_names)]
    r_sum = add_pair(r_pair[:, None], F32, "chip_sum_replicated")[0]
    *g_recv, r_recv = exchange(chip_sums + [r_sum], ['a2a'] * len(sends) + ['gather'], "grads_between_chips",
                               group='chips')

    outs = {}

    def update(parts, k, w_k, m_k, v_k):
        res = reduce_adamw(parts, w_k, m_k, v_k, "adamw_" + k)
        return dict(zip(('grad_', 'delta_', 'new_m_', 'new_v_'), res))

    for k, parts in zip(gathered_names + ['w_mod'], g_recv + [g_w_mod[None]]):
        for kind, a in update(parts, k, w[k], strip(inp['m_' + k]), strip(inp['v_' + k])).items():
            outs[kind + k] = a[None]
    rep = lambda prefix: _pack([strip(inp[prefix + k]) for k in REPLICATED], SUBLANES)[0]
    for kind, buf in update(r_recv, "replicated", rep(''), rep('m_'), rep('v_')).items():
        for k, a in zip(REPLICATED, _unpack(buf, rspans, [w[k].shape for k in REPLICATED])):
            outs[kind + k] = a if inp[k].ndim <= 2 else a[None]
    result = [loss, grad_x[None]]
    for kind in ('grad_', 'delta_', 'new_m_', 'new_v_'):
        result += [outs[kind + k] for k in WEIGHT_NAMES]
    return tuple(result)


_ARG_NAMES = (['x', 'c', 'ctx'] + WEIGHT_NAMES + ['loss_target'] + ['m_' + k for k in WEIGHT_NAMES]
              + ['v_' + k for k in WEIGHT_NAMES])


def kernel(*args):
    assert len(args) == len(_ARG_NAMES)
    return _step(dict(zip(_ARG_NAMES, args)))
```

```python
import functools
import math

import jax
import jax.numpy as jnp
from jax import lax
from jax.experimental import pallas as pl
from jax.experimental.pallas import tpu as pltpu

F32 = jnp.float32
BF16 = jnp.bfloat16

N_DEV = 8
MESH_AXES = ("x", "y", "c")
N_HEADS = 8
QK_NOPE = 64
QK_ROPE = 32
QK_DIM = QK_NOPE + QK_ROPE
V_DIM = 64
HEAD_PAD = 128
HEADS_W = N_HEADS * HEAD_PAD
GRID_W = 64
ROPE_THETA = 10000.0
SSM_GROUP = 16
SSM_STATE = 64
EPS = 1e-6
LANES = 128
SUBLANES = 8
PACK_W = 1024
VMEM_LIMIT = 56 * 1024 * 1024
MM_TILES = (1024, 768, 1408, 512, 384, 256, 128)

ADAM_LR = 0.001
ADAM_B1 = 0.9
ADAM_B2 = 0.999
ADAM_EPS = 1e-08
ADAM_WD = 0.01
ADAM_STEP = 10

WEIGHT_NAMES = ['c_ctx', 'w_mod', 'b_mod', 'norm1_g', 'norm2_g', 'w_in', 'q_a_g', 'w_uq', 'kv_a_g', 'w_ukv',
                'q_norm_g', 'k_norm_g', 'w_o_attn', 'lam_re_f', 'lam_im_f', 'log_dt_f', 'c_re_f', 'c_im_f',
                'lam_re_b', 'lam_im_b', 'log_dt_b', 'c_re_b', 'c_im_b', 'b_re', 'b_im', 'd_skip', 'w_glu',
                'w_out', 'w_up', 'conv_w', 'conv_b', 'w_down']
GATHERED = {'w_in': 1, 'w_uq': 1, 'w_ukv': 1, 'w_o_attn': 1, 'w_glu': 1, 'w_out': 0, 'w_up': 1, 'conv_w': 1,
            'w_down': 0}
REPLICATED = [n for n in WEIGHT_NAMES if n not in GATHERED and n != 'w_mod']


def _tile(n, prefs):
    for t in prefs:
        if n % t == 0:
            return t
    return n


def _cparams(sem=None):
    return pltpu.CompilerParams(dimension_semantics=sem, vmem_limit_bytes=VMEM_LIMIT)


@jax.custom_vjp
def bdot(a, w):
    return jnp.dot(a.astype(BF16), w.astype(BF16), preferred_element_type=F32)


def _bdot_fwd(a, w):
    return bdot(a, w), (a, w)


def _bdot_bwd(res, g):
    a, w = res
    gb = g.astype(BF16)
    da = lax.dot_general(gb, w.astype(BF16), (((1,), (1,)), ((), ())), preferred_element_type=F32)
    dw = lax.dot_general(a.astype(BF16), gb, (((0,), (0,)), ((), ())), preferred_element_type=F32)
    return da.astype(a.dtype), dw.astype(w.dtype)


bdot.defvjp(_bdot_fwd, _bdot_bwd)


def _dot_nt(a, b):
    return lax.dot_general(a, b, (((1,), (1,)), ((), ())), preferred_element_type=F32)


def _dot_tn(a, b):
    return lax.dot_general(a, b, (((0,), (0,)), ((), ())), preferred_element_type=F32)


def matmul(a, b, mode, out_dtype, name, a_col0=0, a_cols=None, b_col0=0, b_cols=None):
    if mode == 'nn':
        (m, k), n = a.shape, b.shape[1]
    elif mode == 'nt':
        (m, k), n = a.shape, b.shape[0]
    else:
        (k, m), n = a.shape, b.shape[1]
        m, n = a_cols or m, b_cols or n
    tm = _tile(m, MM_TILES)
    tn = _tile(n, MM_TILES)
    tk = _tile(k, MM_TILES)
    nk = k // tk
    assert a_col0 % tm == 0 and b_col0 % tn == 0
    col0, bcol0 = a_col0 // tm, b_col0 // tn

    def body(a_ref, b_ref, o_ref, acc_ref):
        kk = pl.program_id(2)

        @pl.when(kk == 0)
        def _():
            acc_ref[...] = jnp.zeros_like(acc_ref)

        av, bv = a_ref[...].astype(BF16), b_ref[...].astype(BF16)
        if mode == 'nn':
            acc_ref[...] += jnp.dot(av, bv, preferred_element_type=F32)
        elif mode == 'nt':
            acc_ref[...] += _dot_nt(av, bv)
        else:
            acc_ref[...] += _dot_tn(av, bv)

        @pl.when(kk == nk - 1)
        def _():
            o_ref[...] = acc_ref[...].astype(o_ref.dtype)

    if mode == 'nn':
        a_spec = pl.BlockSpec((tm, tk), lambda i, j, kk: (i, kk))
        b_spec = pl.BlockSpec((tk, tn), lambda i, j, kk: (kk, j))
    elif mode == 'nt':
        a_spec = pl.BlockSpec((tm, tk), lambda i, j, kk: (i, kk))
        b_spec = pl.BlockSpec((tn, tk), lambda i, j, kk: (j, kk))
    else:
        a_spec = pl.BlockSpec((tk, tm), lambda i, j, kk: (kk, i + col0))
        b_spec = pl.BlockSpec((tk, tn), lambda i, j, kk: (kk, j + bcol0))
    return pl.pallas_call(
        body, name=name, grid=(m // tm, n // tn, nk),
        in_specs=[a_spec, b_spec],
        out_specs=pl.BlockSpec((tm, tn), lambda i, j, kk: (i, j)),
        out_shape=jax.ShapeDtypeStruct((m, n), out_dtype),
        scratch_shapes=[pltpu.VMEM((tm, tn), F32)],
        compiler_params=_cparams(("parallel", "parallel", "arbitrary")),
    )(a, b)


N_CHIPS = 4


def _group(group):
    x, y, c = lax.axis_index("x"), lax.axis_index("y"), lax.axis_index("c")
    flips = {'all': [(r & 4, r & 2, r & 1) for r in range(1, 8)],
             'chips': [(0, 1, 0), (1, 0, 0), (1, 1, 0)], 'core': [(0, 0, 1)]}[group]
    index = {'all': lambda px, py, pc: 4 * px + 2 * py + pc, 'chips': lambda px, py, pc: 2 * px + py,
             'core': lambda px, py, pc: pc}[group]
    peers = []
    for fx, fy, fc in flips:
        p = (1 - x if fx else x, 1 - y if fy else y, 1 - c if fc else c)
        peers.append((p, index(*p)))
    return len(flips) + 1, index(x, y, c), peers


def exchange(arrays, kinds, name, group='all'):
    n_arr = len(arrays)
    size = {'all': N_DEV, 'chips': N_CHIPS, 'core': 2}[group]

    def body(*refs):
        srcs, dsts = refs[:n_arr], refs[n_arr:2 * n_arr]
        send_sems, recv_sems, local_sems = refs[2 * n_arr:]
        _, me, peers = _group(group)

        def copy(a, r, peer, peer_idx, receiving):
            src = srcs[a] if kinds[a] == 'gather' else srcs[a].at[peer_idx]
            if kinds[a] == 'others':
                dst = dsts[a].at[r]
            else:
                dst = dsts[a].at[peer_idx if receiving else me]
            return pltpu.make_async_remote_copy(
                src_ref=src, dst_ref=dst, send_sem=send_sems.at[a * size + r], recv_sem=recv_sems.at[a * size + r],
                device_id=peer, device_id_type=pl.DeviceIdType.MESH)

        started = []
        for a in range(n_arr):
            if kinds[a] != 'others':
                mine = srcs[a] if kinds[a] == 'gather' else srcs[a].at[me]
                started.append(pltpu.make_async_copy(mine, dsts[a].at[me], local_sems.at[a]))
                started[-1].start()
            for r, (peer, peer_idx) in enumerate(peers):
                copy(a, r, peer, peer_idx, False).start()
        for a in range(n_arr):
            for r, (peer, peer_idx) in enumerate(peers):
                cp = copy(a, r, peer, peer_idx, True)
                cp.wait_send()
                cp.wait_recv()
        for local in started:
            local.wait()

    out_shape = []
    for a, arr in enumerate(arrays):
        shp = {'gather': (size,) + tuple(arr.shape), 'a2a': tuple(arr.shape),
               'others': (size - 1,) + tuple(arr.shape[1:])}[kinds[a]]
        out_shape.append(jax.ShapeDtypeStruct(shp, arr.dtype))
    any_spec = pl.BlockSpec(memory_space=pl.ANY)
    return pl.pallas_call(
        body, name=name,
        in_specs=[any_spec] * n_arr, out_specs=[any_spec] * n_arr, out_shape=out_shape,
        scratch_shapes=[pltpu.SemaphoreType.DMA((n_arr * size,)), pltpu.SemaphoreType.DMA((n_arr * size,)),
                        pltpu.SemaphoreType.DMA((n_arr,))],
        compiler_params=pltpu.CompilerParams(has_side_effects=True),
    )(*arrays)


def gather_two_level(arrays, name):
    n_arr = len(arrays)
    per = 7

    def body(*refs):
        srcs, outs = refs[:n_arr], refs[n_arr:2 * n_arr]
        send_sems, recv_sems, local_sems = refs[2 * n_arr:]
        x, y, c = lax.axis_index("x"), lax.axis_index("y"), lax.axis_index("c")
        me, sibling = (x, y, c), (x, y, 1 - c)
        chips = [(1 - x, y), (x, 1 - y), (1 - x, 1 - y)]
        slot = lambda p: 4 * p[0] + 2 * p[1] + p[2]

        def copy(a, k, block, to, src=None):
            dst = outs[a].at[slot(block)]
            return pltpu.make_async_remote_copy(
                src_ref=dst if src is None else src, dst_ref=dst,
                send_sem=send_sems.at[a * per + k], recv_sem=recv_sems.at[a * per + k],
                device_id=to, device_id_type=pl.DeviceIdType.MESH)

        mine, first, passed = [], [], []
        for a in range(n_arr):
            mine.append(pltpu.make_async_copy(srcs[a], outs[a].at[slot(me)], local_sems.at[a]))
            first.append(copy(a, 0, me, sibling, src=srcs[a]))
            first += [copy(a, 1 + j, me, (*chip, c), src=srcs[a]) for j, chip in enumerate(chips)]
        for cp in mine + first:
            cp.start()
        for j, chip in enumerate(chips):
            for a in range(n_arr):
                copy(a, 1 + j, (*chip, c), me).wait_recv()
                passed.append(copy(a, 4 + j, (*chip, c), sibling))
                passed[-1].start()
        for a in range(n_arr):
            copy(a, 0, sibling, me).wait_recv()
            for j, chip in enumerate(chips):
                copy(a, 4 + j, (*chip, 1 - c), me).wait_recv()
        for cp in first + passed:
            cp.wait_send()
        for cp in mine:
            cp.wait()

    any_spec = pl.BlockSpec(memory_space=pl.ANY)
    return pl.pallas_call(
        body, name=name,
        in_specs=[any_spec] * n_arr, out_specs=[any_spec] * n_arr,
        out_shape=[jax.ShapeDtypeStruct((N_DEV,) + tuple(a.shape), a.dtype) for a in arrays],
        scratch_shapes=[pltpu.SemaphoreType.DMA((n_arr * per,)), pltpu.SemaphoreType.DMA((n_arr * per,)),
                        pltpu.SemaphoreType.DMA((n_arr,))],
        compiler_params=pltpu.CompilerParams(has_side_effects=True),
    )(*arrays)


def _pack(parts, row_mult):
    rows, spans, r = [], [], 0
    for p in parts:
        flat = p.reshape(-1)
        nr = -(-flat.shape[0] // (PACK_W * row_mult)) * row_mult
        flat = jnp.pad(flat, (0, nr * PACK_W - flat.shape[0]))
        rows.append(flat.reshape(nr, PACK_W))
        spans.append((r, nr))
        r += nr
    return jnp.concatenate(rows, axis=0), spans


def _unpack(buf, spans, shapes):
    out = []
    for (r, nr), shp in zip(spans, shapes):
        size = math.prod(shp)
        out.append(buf[..., r:r + nr, :].reshape(buf.shape[:-2] + (nr * PACK_W,))[..., :size]
                   .reshape(buf.shape[:-2] + tuple(shp)))
    return out


def _to_shards(full, axis):
    r, c = full.shape
    if axis == 0:
        return full.reshape(N_DEV, r // N_DEV, c)
    return full.reshape(r, N_DEV, c // N_DEV).transpose(1, 0, 2)


def _from_shards(sh, axis):
    _, r, c = sh.shape
    if axis == 0:
        return sh.reshape(N_DEV * r, c)
    return sh.transpose(1, 0, 2).reshape(r, N_DEV * c)


def _rms(x, g, n):
    ms = jnp.sum(x * x, axis=-1, keepdims=True) * (1.0 / n)
    return x * lax.rsqrt(ms + EPS) * g


def _norm_mod_fn(x, g, sc, sh):
    return _rms(x, g, x.shape[-1]) * (1.0 + sc) + sh


@jax.custom_vjp
def _rope(t, ct, s1, s2):
    return t * ct + pltpu.roll(t, 16, 1) * s1 + pltpu.roll(t, HEAD_PAD - 16, 1) * s2


def _rope_fwd(t, ct, s1, s2):
    return _rope(t, ct, s1, s2), (ct, s1, s2)


def _rope_bwd(res, d):
    ct, s1, s2 = res
    dt = d * ct + pltpu.roll(d * s1, HEAD_PAD - 16, 1) + pltpu.roll(d * s2, 16, 1)
    return dt, jnp.zeros_like(ct), jnp.zeros_like(s1), jnp.zeros_like(s2)


_rope.defvjp(_rope_fwd, _rope_bwd)


def _qkv_fn(cq, ckv, krsec, qag, kvag, qng, kng, wuq, wk, wv, ct, s1, s2):
    q_raw = bdot(_rms(cq, qag, cq.shape[-1]), wuq)
    ckvn = _rms(ckv, kvag, ckv.shape[-1])
    k_raw = bdot(ckvn, wk)
    v = bdot(ckvn, wv)
    qs, ks = [], []
    for h in range(N_HEADS):
        sl = slice(h * HEAD_PAD, (h + 1) * HEAD_PAD)
        qs.append(_rope(_rms(q_raw[:, sl], qng, QK_DIM), ct, s1, s2))
        ks.append(_rope(_rms(k_raw[:, sl] + krsec, kng, QK_DIM), ct, s1, s2))
    return jnp.concatenate(qs, axis=1), jnp.concatenate(ks, axis=1), v


def _glu_fn(u, yf, yb, dskip, wglu):
    y = u * dskip + yf + yb
    vg = bdot(jax.nn.gelu(y), wglu)
    d = vg.shape[-1] // 2
    return vg[:, :d] * jax.nn.sigmoid(vg[:, d:])


def _merge_fn(o, s_l, gl, x, g1, wo, wout):
    d = x.shape[-1]
    a = bdot(o, wo)
    mix = jax.nn.sigmoid(gl[:, :d]) * a + jax.nn.sigmoid(gl[:, d:]) * s_l
    return x + g1 * bdot(mix, wout)


def _mod_fn(cmat, w):
    return bdot(jax.nn.silu(cmat), w)


def _ssm_prep_fn(lam_re, lam_im, logdt, bre, bim):
    dt = jnp.exp(logdt)
    ar, ai = lam_re * dt, lam_im * dt
    e = jnp.exp(ar)
    lbr, lbi = e * jnp.cos(ai), e * jnp.sin(ai)
    nr, ni = lbr - 1.0, lbi
    den = lam_re * lam_re + lam_im * lam_im
    qr = (nr * lam_re + ni * lam_im) / den
    qi = (ni * lam_re - nr * lam_im) / den
    return lbr, lbi, qr * bre - qi * bim, qr * bim + qi * bre


def _rows(tm, w, col=0):
    return pl.BlockSpec((tm, w), lambda i: (i, col))


def _full(shape):
    nd = len(shape)
    return pl.BlockSpec(tuple(shape), lambda i: (0,) * nd)


def _acc_add(i, ref, val):
    @pl.when(i == 0)
    def _():
        ref[...] = jnp.zeros_like(ref)
    ref[...] += val


def _pad_rows(v, rows=SUBLANES):
    sel = lax.broadcasted_iota(jnp.int32, (rows, v.shape[-1]), 0) == 0
    return jnp.where(sel, jnp.broadcast_to(v, (rows, v.shape[-1])), 0.0)


def norm_mod_fwd(xa, g, mods, n_lat, tm, name):
    r, d = xa.shape
    lat_tiles = n_lat // tm

    def body(x_ref, g_ref, m_ref, o_ref):
        lat = pl.program_id(0) < lat_tiles
        sc = jnp.where(lat, m_ref[0:1, :], m_ref[2:3, :])
        sh = jnp.where(lat, m_ref[1:2, :], m_ref[3:4, :])
        o_ref[...] = _norm_mod_fn(x_ref[...], g_ref[...], sc, sh).astype(o_ref.dtype)

    return pl.pallas_call(
        body, name=name, grid=(r // tm,),
        in_specs=[_rows(tm, d), _full(g.shape), _full(mods.shape)],
        out_specs=_rows(tm, d), out_shape=jax.ShapeDtypeStruct((r, d), BF16),
        compiler_params=_cparams(("parallel",)),
    )(xa, g, mods)


def norm_mod_bwd(xa, dh, dres, g, mods, n_lat, tm, name):
    r, d = xa.shape
    lat_tiles = n_lat // tm

    def body(x_ref, dh_ref, dres_ref, g_ref, m_ref, dx_ref, dg_ref, dm_ref):
        i = pl.program_id(0)
        lat = i < lat_tiles
        sc = jnp.where(lat, m_ref[0:1, :], m_ref[2:3, :])
        sh = jnp.where(lat, m_ref[1:2, :], m_ref[3:4, :])
        _, vjp = jax.vjp(_norm_mod_fn, x_ref[...], g_ref[...], sc, sh)
        dx, dg, dsc, dsh = vjp(dh_ref[...])
        dx_ref[...] = dx + jnp.where(lat, dres_ref[...], 0.0)
        _acc_add(i, dg_ref, _pad_rows(dg))
        row = lax.broadcasted_iota(jnp.int32, (SUBLANES, d), 0)
        base = jnp.where(lat, 0, 2)
        upd = jnp.where(row == base, jnp.broadcast_to(dsc, (SUBLANES, d)), 0.0)
        upd = upd + jnp.where(row == base + 1, jnp.broadcast_to(dsh, (SUBLANES, d)), 0.0)
        _acc_add(i, dm_ref, upd)

    return pl.pallas_call(
        body, name=name, grid=(r // tm,),
        in_specs=[_rows(tm, d), _rows(tm, d),
                  pl.BlockSpec((tm, d), lambda i: (jnp.minimum(i, lat_tiles - 1), 0)),
                  _full(g.shape), _full(mods.shape)],
        out_specs=[_rows(tm, d), _full((SUBLANES, d)), _full((SUBLANES, d))],
        out_shape=[jax.ShapeDtypeStruct((r, d), F32), jax.ShapeDtypeStruct((SUBLANES, d), F32),
                   jax.ShapeDtypeStruct((SUBLANES, d), F32)],
        compiler_params=_cparams(("arbitrary",)),
    )(xa, dh, dres, g, mods)


def qkv_fwd(proj, lay, gains, wuq, wk, wv, tabs, tm, name):
    r = proj.shape[0]
    q_w, kv_w = lay['q'], lay['kv']

    def body(cq_ref, ckv_ref, kr_ref, qag, kvag, qng, kng, wuq_ref, wk_ref, wv_ref, ct, s1, s2, q_ref, k_ref,
             v_ref):
        q, k, v = _qkv_fn(cq_ref[...], ckv_ref[...], kr_ref[...], qag[...], kvag[...], qng[...], kng[...],
                          wuq_ref[...], wk_ref[...], wv_ref[...], ct[...], s1[...], s2[...])
        q_ref[...] = q.astype(BF16)
        k_ref[...] = k.astype(BF16)
        v_ref[...] = v.astype(BF16)

    out = jax.ShapeDtypeStruct((r, HEADS_W), BF16)
    return pl.pallas_call(
        body, name=name, grid=(r // tm,),
        in_specs=[_rows(tm, q_w, lay['o_cq'] // q_w), _rows(tm, kv_w, lay['o_ckv'] // kv_w),
                  _rows(tm, LANES, lay['o_kr'] // LANES)]
        + [_full(a.shape) for a in gains] + [_full(wuq.shape), _full(wk.shape), _full(wv.shape)]
        + [_rows(tm, HEAD_PAD)] * 3,
        out_specs=[_rows(tm, HEADS_W)] * 3, out_shape=[out, out, out],
        compiler_params=_cparams(("parallel",)),
    )(proj, proj, proj, *gains, wuq, wk, wv, *tabs)


def qkv_bwd(proj, lay, gains, wuq, wk, wv, tabs, dq, dk, dv, dgl, du_direct, du_f, du_b, n, tm, name):
    r = proj.shape[0]
    q_w, kv_w, sw, d = lay['q'], lay['kv'], lay['sw'], lay['d']
    lat_tiles = n // tm

    def body(cq_ref, ckv_ref, kr_ref, qag, kvag, qng, kng, wuq_ref, wk_ref, wv_ref, ct, s1, s2, dq_ref, dk_ref,
             dv_ref, dgl_ref, dud_ref, duf_ref, dub_ref, dp_ref, dqag, dkvag, dqng, dkng, dwuq, dwk, dwv):
        i = pl.program_id(0)
        lat = i < lat_tiles
        tables = (ct[...], s1[...], s2[...])
        fn = lambda *a: _qkv_fn(*a, *tables)
        _, vjp = jax.vjp(fn, cq_ref[...], ckv_ref[...], kr_ref[...], qag[...], kvag[...], qng[...], kng[...],
                         wuq_ref[...].astype(F32), wk_ref[...].astype(F32), wv_ref[...].astype(F32))
        g = vjp((jnp.where(lat, dq_ref[...], 0.0), dk_ref[...], dv_ref[...]))
        dp_ref[:, 0:2 * d] = jnp.where(lat, dgl_ref[...], 0.0).astype(BF16)
        dp_ref[:, lay['o_u']:lay['o_u'] + sw] = (duf_ref[...] + dub_ref[...]
                                                 + jnp.where(lat, dud_ref[...], 0.0)).astype(BF16)
        dp_ref[:, lay['o_ckv']:lay['o_ckv'] + kv_w] = g[1].astype(BF16)
        dp_ref[:, lay['o_kr']:lay['o_kr'] + LANES] = g[2].astype(BF16)
        hole0 = lay['o_kr'] + LANES
        if lay['o_cq'] > hole0:
            dp_ref[:, hole0:lay['o_cq']] = jnp.zeros((tm, lay['o_cq'] - hole0), BF16)
        dp_ref[:, lay['o_cq']:lay['o_cq'] + q_w] = g[0].astype(BF16)
        for ref, val in zip((dqag, dkvag, dqng, dkng), g[3:7]):
            _acc_add(i, ref, _pad_rows(val))
        for ref, val in zip((dwuq, dwk, dwv), g[7:10]):
            _acc_add(i, ref, val)

    def lat_rows(w):
        return pl.BlockSpec((tm, w), lambda i: (jnp.minimum(i, lat_tiles - 1), 0))

    acc_shapes = [(SUBLANES, a.shape[1]) for a in gains] + [wuq.shape, wk.shape, wv.shape]
    return pl.pallas_call(
        body, name=name, grid=(r // tm,),
        in_specs=[_rows(tm, q_w, lay['o_cq'] // q_w), _rows(tm, kv_w, lay['o_ckv'] // kv_w),
                  _rows(tm, LANES, lay['o_kr'] // LANES)]
        + [_full(a.shape) for a in gains] + [_full(wuq.shape), _full(wk.shape), _full(wv.shape)]
        + [_rows(tm, HEAD_PAD)] * 3 + [lat_rows(HEADS_W), _rows(tm, HEADS_W), _rows(tm, HEADS_W)]
        + [lat_rows(2 * d), lat_rows(sw), _rows(tm, sw), _rows(tm, sw)],
        out_specs=[_rows(tm, lay['width'])] + [_full(s) for s in acc_shapes],
        out_shape=[jax.ShapeDtypeStruct((r, lay['width']), BF16)] + [jax.ShapeDtypeStruct(s, F32) for s in acc_shapes],
        compiler_params=_cparams(("arbitrary",)),
    )(proj, proj, proj, *gains, wuq, wk, wv, *tabs, dq, dk, dv, dgl, du_direct, du_f, du_b)


def glu_fwd(proj, lay, yf, yb, dskip, wglu, n, tm, name):
    sw, d = wglu.shape[0], wglu.shape[1] // 2

    def body(u_ref, yf_ref, yb_ref, ds_ref, w_ref, o_ref):
        o_ref[...] = _glu_fn(u_ref[...], yf_ref[...], yb_ref[...], ds_ref[...], w_ref[...])

    return pl.pallas_call(
        body, name=name, grid=(n // tm,),
        in_specs=[_rows(tm, sw, lay['o_u'] // sw), _rows(tm, sw), _rows(tm, sw), _full(dskip.shape),
                  _full(wglu.shape)],
        out_specs=_rows(tm, d), out_shape=jax.ShapeDtypeStruct((n, d), F32),
        compiler_params=_cparams(("parallel",)),
    )(proj, yf, yb, dskip, wglu)


def glu_bwd(proj, lay, yf, yb, dskip, wglu, ds_l, n, tm, name):
    sw, d = wglu.shape[0], wglu.shape[1] // 2

    def body(u_ref, yf_ref, yb_ref, ds_ref, w_ref, g_ref, du_ref, dy_ref, dds_ref, dw_ref):
        i = pl.program_id(0)
        _, vjp = jax.vjp(_glu_fn, u_ref[...], yf_ref[...], yb_ref[...], ds_ref[...], w_ref[...].astype(F32))
        du, dyf, _, dds, dw = vjp(g_ref[...])
        du_ref[...] = du
        dy_ref[...] = dyf
        _acc_add(i, dds_ref, _pad_rows(dds))
        _acc_add(i, dw_ref, dw)

    return pl.pallas_call(
        body, name=name, grid=(n // tm,),
        in_specs=[_rows(tm, sw, lay['o_u'] // sw), _rows(tm, sw), _rows(tm, sw), _full(dskip.shape),
                  _full(wglu.shape), _rows(tm, d)],
        out_specs=[_rows(tm, sw), _rows(tm, sw), _full((SUBLANES, sw)), _full(wglu.shape)],
        out_shape=[jax.ShapeDtypeStruct((n, sw), F32), jax.ShapeDtypeStruct((n, sw), F32),
                   jax.ShapeDtypeStruct((SUBLANES, sw), F32), jax.ShapeDtypeStruct(wglu.shape, F32)],
        compiler_params=_cparams(("arbitrary",)),
    )(proj, yf, yb, dskip, wglu, ds_l)


def merge_fwd(o, s_l, proj, xa, g1, wo, wout, n, tm, name):
    d = xa.shape[1]

    def body(o_ref, s_ref, gl_ref, x_ref, g1_ref, wo_ref, wout_ref, x1_ref):
        x1_ref[...] = _merge_fn(o_ref[...], s_ref[...], gl_ref[...], x_ref[...], g1_ref[...], wo_ref[...],
                                wout_ref[...])

    return pl.pallas_call(
        body, name=name, grid=(n // tm,),
        in_specs=[_rows(tm, HEADS_W), _rows(tm, d), _rows(tm, 2 * d), _rows(tm, d), _full(g1.shape),
                  _full(wo.shape), _full(wout.shape)],
        out_specs=_rows(tm, d), out_shape=jax.ShapeDtypeStruct((n, d), F32),
        compiler_params=_cparams(("parallel",)),
    )(o, s_l, proj, xa, g1, wo, wout)


def merge_bwd(o, s_l, proj, xa, g1, wo, wout, dx1, n, tm, name):
    d = xa.shape[1]

    def body(o_ref, s_ref, gl_ref, x_ref, g1_ref, wo_ref, wout_ref, dx1_ref, do_ref, ds_ref, dgl_ref, dg1_ref,
             dwo_ref, dwout_ref):
        i = pl.program_id(0)
        _, vjp = jax.vjp(_merge_fn, o_ref[...], s_ref[...], gl_ref[...], x_ref[...], g1_ref[...],
                         wo_ref[...].astype(F32), wout_ref[...].astype(F32))
        do, ds, dgl, _, dg1, dwo, dwout = vjp(dx1_ref[...])
        do_ref[...] = do
        ds_ref[...] = ds
        dgl_ref[...] = dgl
        _acc_add(i, dg1_ref, _pad_rows(dg1))
        _acc_add(i, dwo_ref, dwo)
        _acc_add(i, dwout_ref, dwout)

    return pl.pallas_call(
        body, name=name, grid=(n // tm,),
        in_specs=[_rows(tm, HEADS_W), _rows(tm, d), _rows(tm, 2 * d), _rows(tm, d), _full(g1.shape),
                  _full(wo.shape), _full(wout.shape), _rows(tm, d)],
        out_specs=[_rows(tm, HEADS_W), _rows(tm, d), _rows(tm, 2 * d), _full((SUBLANES, d)), _full(wo.shape),
                   _full(wout.shape)],
        out_shape=[jax.ShapeDtypeStruct((n, HEADS_W), BF16), jax.ShapeDtypeStruct((n, d), F32),
                   jax.ShapeDtypeStruct((n, 2 * d), F32), jax.ShapeDtypeStruct((SUBLANES, d), F32),
                   jax.ShapeDtypeStruct(wo.shape, F32), jax.ShapeDtypeStruct(wout.shape, F32)],
        compiler_params=_cparams(("arbitrary",)),
    )(o, s_l, proj, xa, g1, wo, wout, dx1)


def _halo_specs(tm, w, n):
    nb = n // SUBLANES
    per = tm // SUBLANES
    prev = pl.BlockSpec((SUBLANES, w), lambda i: (jnp.maximum(i * per - 1, 0), 0))
    nxt = pl.BlockSpec((SUBLANES, w), lambda i: (jnp.minimum((i + 1) * per, nb - 1), 0))
    return prev, nxt


def _shifted(t, prev_blk, next_blk, i, n_tiles):
    tm = t.shape[0]
    row = lax.broadcasted_iota(jnp.int32, t.shape, 0)
    prev_row = jnp.where(i > 0, prev_blk[SUBLANES - 1:SUBLANES, :], 0.0)
    next_row = jnp.where(i < n_tiles - 1, next_blk[0:1, :], 0.0)
    before = jnp.where(row == 0, prev_row, pltpu.roll(t, 1, 0))
    after = jnp.where(row == tm - 1, next_row, pltpu.roll(t, tm - 1, 0))
    return before, after


def _conv_u2(up, before, after, cw, cb):
    return before * cw[0:1, :] + up * cw[1:2, :] + after * cw[2:3, :] + cb


def conv_act_fwd(up, cw, cb, tm, name):
    n, w2 = up.shape
    f = w2 // 2
    n_tiles = n // tm
    prev, nxt = _halo_specs(tm, w2, n)

    def body(up_ref, prev_ref, next_ref, cw_ref, cb_ref, act_ref):
        i = pl.program_id(0)
        t = up_ref[...]
        before, after = _shifted(t, prev_ref[...], next_ref[...], i, n_tiles)
        u2 = _conv_u2(t, before, after, cw_ref[...], cb_ref[...])
        act_ref[...] = (jax.nn.silu(u2[:, f:]) * u2[:, :f]).astype(BF16)

    return pl.pallas_call(
        body, name=name, grid=(n_tiles,),
        in_specs=[_rows(tm, w2), prev, nxt, _full(cw.shape), _full(cb.shape)],
        out_specs=_rows(tm, f), out_shape=jax.ShapeDtypeStruct((n, f), BF16),
        compiler_params=_cparams(("parallel",)),
    )(up, up, up, cw, cb)


def down_loss(act, wdown, x1, g2, target, tm, name):
    n, d = x1.shape
    f = act.shape[1]

    def body(act_ref, w_ref, x1_ref, g2_ref, t_ref, dy_ref, ffn_ref, loss_ref):
        ffn = jnp.dot(act_ref[...], w_ref[...], preferred_element_type=F32)
        err = x1_ref[...] + g2_ref[...] * ffn - t_ref[...]
        ffn_ref[...] = ffn
        dy_ref[...] = err * (1.0 / d)
        part = 0.5 * jnp.sum(jnp.sum(err * err, axis=-1, keepdims=True) * (1.0 / d), axis=0, keepdims=True)
        loss_ref[0] = jnp.broadcast_to(part, (SUBLANES, LANES))

    return pl.pallas_call(
        body, name=name, grid=(n // tm,),
        in_specs=[_rows(tm, f), _full(wdown.shape), _rows(tm, d), _full(g2.shape), _rows(tm, d)],
        out_specs=[_rows(tm, d), _rows(tm, d), pl.BlockSpec((1, SUBLANES, LANES), lambda i: (i, 0, 0))],
        out_shape=[jax.ShapeDtypeStruct((n, d), F32), jax.ShapeDtypeStruct((n, d), F32),
                   jax.ShapeDtypeStruct((n // tm, SUBLANES, LANES), F32)],
        compiler_params=_cparams(("parallel",)),
    )(act, wdown, x1, g2, target)


def down_bwd(dy, ffn, up, cw, cb, wdown, g2, tm, name):
    n, d = dy.shape
    w2 = up.shape[1]
    f = w2 // 2
    n_tiles = n // tm
    prev, nxt = _halo_specs(tm, w2, n)

    def body(dy_ref, ffn_ref, up_ref, prev_ref, next_ref, cw_ref, cb_ref, w_ref, g2_ref, du2_ref, dffn_ref,
             dg2_ref):
        i = pl.program_id(0)
        dyv = dy_ref[...]
        dffn = (dyv * g2_ref[...]).astype(BF16)
        dffn_ref[...] = dffn
        _acc_add(i, dg2_ref, _pad_rows(jnp.sum(dyv * ffn_ref[...], axis=0, keepdims=True)))
        dact = _dot_nt(dffn, w_ref[...])
        t = up_ref[...]
        before, after = _shifted(t, prev_ref[...], next_ref[...], i, n_tiles)
        u2 = _conv_u2(t, before, after, cw_ref[...], cb_ref[...])
        val, gate = u2[:, :f], u2[:, f:]
        sg = jax.nn.sigmoid(gate)
        du2_ref[:, :f] = dact * (gate * sg)
        du2_ref[:, f:] = dact * val * (sg * (1.0 + gate * (1.0 - sg)))

    return pl.pallas_call(
        body, name=name, grid=(n_tiles,),
        in_specs=[_rows(tm, d), _rows(tm, d), _rows(tm, w2), prev, nxt, _full(cw.shape), _full(cb.shape),
                  _full(wdown.shape), _full(g2.shape)],
        out_specs=[_rows(tm, w2), _rows(tm, d), _full((SUBLANES, d))],
        out_shape=[jax.ShapeDtypeStruct((n, w2), F32), jax.ShapeDtypeStruct((n, d), BF16),
                   jax.ShapeDtypeStruct((SUBLANES, d), F32)],
        compiler_params=_cparams(("arbitrary",)),
    )(dy, ffn, up, up, up, cw, cb, wdown, g2)


def conv_bwd(du2, up, cw, tm, name):
    n, w2 = up.shape
    n_tiles = n // tm
    prev, nxt = _halo_specs(tm, w2, n)

    def body(d_ref, prev_ref, next_ref, up_ref, cw_ref, dup_ref, dcw_ref):
        i = pl.program_id(0)
        dv = d_ref[...]
        before, after = _shifted(dv, prev_ref[...], next_ref[...], i, n_tiles)
        cwv = cw_ref[...]
        dup_ref[...] = (after * cwv[0:1, :] + dv * cwv[1:2, :] + before * cwv[2:3, :]).astype(BF16)
        t = up_ref[...]
        row = lax.broadcasted_iota(jnp.int32, (SUBLANES, w2), 0)
        upd = jnp.zeros((SUBLANES, w2), F32)
        for k, term in enumerate((after * t, dv * t, before * t, dv)):
            upd = upd + jnp.where(row == k, jnp.broadcast_to(jnp.sum(term, axis=0, keepdims=True),
                                                             (SUBLANES, w2)), 0.0)
        _acc_add(i, dcw_ref, upd)

    return pl.pallas_call(
        body, name=name, grid=(n_tiles,),
        in_specs=[_rows(tm, w2), prev, nxt, _rows(tm, w2), _full(cw.shape)],
        out_specs=[_rows(tm, w2), _full((SUBLANES, w2))],
        out_shape=[jax.ShapeDtypeStruct((n, w2), BF16), jax.ShapeDtypeStruct((SUBLANES, w2), F32)],
        compiler_params=_cparams(("arbitrary",)),
    )(du2, du2, du2, up, cw)


def attn_fwd(q, k, v, n, name):
    nk = k.shape[0]
    tq = _tile(n, (512, 256, 128))
    tk = _tile(nk, (768, 384, 256, 128))
    n_kv = nk // tk
    scale = QK_DIM ** -0.5
    c2 = scale * math.log2(math.e)

    def body(q_ref, k_ref, v_ref, o_ref, lse_ref):
        qv = q_ref[...]
        ones_col = (lax.broadcasted_iota(jnp.int32, (tk, HEAD_PAD), 1) == V_DIM).astype(BF16)

        def chunk(j, carry):
            m, acc = carry
            rows = pl.ds(pl.multiple_of(j * tk, tk), tk)
            s = _dot_nt(qv, k_ref[rows, :])
            m_new = jnp.maximum(m, jnp.max(s, axis=-1, keepdims=True))
            p = jnp.exp2(s * c2 - m_new * c2)
            alpha = jnp.exp2((m - m_new) * c2)
            pv = jnp.dot(p.astype(BF16), v_ref[rows, :] + ones_col, preferred_element_type=F32)
            return m_new, alpha * acc + pv

        m, acc = lax.fori_loop(0, n_kv, chunk, (jnp.full((tq, 1), -jnp.inf, F32),
                                                jnp.zeros((tq, HEAD_PAD), F32)), unroll=True)
        l = acc[:, V_DIM:V_DIM + 1]
        lane = lax.broadcasted_iota(jnp.int32, (tq, HEAD_PAD), 1)
        o_ref[...] = jnp.where(lane < V_DIM, acc / l, 0.0).astype(BF16)
        lse_ref[...] = jnp.broadcast_to(m * scale + jnp.log(l), (tq, HEAD_PAD))

    qspec = pl.BlockSpec((tq, HEAD_PAD), lambda h, i: (i, h))
    kspec = pl.BlockSpec((nk, HEAD_PAD), lambda h, i: (0, h))
    return pl.pallas_call(
        body, name=name, grid=(N_HEADS, n // tq),
        in_specs=[qspec, kspec, kspec], out_specs=[qspec, qspec],
        out_shape=[jax.ShapeDtypeStruct((n, HEADS_W), BF16), jax.ShapeDtypeStruct((n, HEADS_W), F32)],
        compiler_params=_cparams(("parallel", "parallel")),
    )(q, k, v)


def attn_bwd(q, k, v, o, do, lse, n, name):
    nk = k.shape[0]
    tq = _tile(n, (512, 256, 128))
    tk = _tile(nk, (768, 384, 256, 128))
    scale = QK_DIM ** -0.5
    log2e = math.log2(math.e)
    c2 = scale * log2e

    def body(q_ref, k_ref, v_ref, o_ref, do_ref, lse_ref, dq_ref, dk_ref, dv_ref):
        @pl.when(pl.program_id(1) == 0)
        def _():
            dq_ref[...] = jnp.zeros_like(dq_ref)

        kv, vv = k_ref[...], v_ref[...]

        def q_tile(i, carry):
            dk, dv = carry
            rows = pl.ds(pl.multiple_of(i * tq, tq), tq)
            qv, dov = q_ref[rows, :], do_ref[rows, :]
            p = jnp.exp2(_dot_nt(qv, kv) * c2 - lse_ref[rows, 0:1] * log2e)
            dv = dv + _dot_tn(p.astype(BF16), dov)
            dp = _dot_nt(dov, vv)
            delta = jnp.sum(dov.astype(F32) * o_ref[rows, :].astype(F32), axis=-1, keepdims=True)
            ds = (p * (dp - delta) * scale).astype(BF16)
            dk = dk + _dot_tn(ds, qv)
            dq_ref[rows, :] += jnp.dot(ds, kv, preferred_element_type=F32)
            return dk, dv

        zero = jnp.zeros((tk, HEAD_PAD), F32)
        dk, dv = lax.fori_loop(0, n // tq, q_tile, (zero, zero), unroll=2)
        dk_ref[...] = dk
        dv_ref[...] = dv

    qspec = pl.BlockSpec((n, HEAD_PAD), lambda h, j: (0, h))
    kspec = pl.BlockSpec((tk, HEAD_PAD), lambda h, j: (j, h))
    return pl.pallas_call(
        body, name=name, grid=(N_HEADS, nk // tk),
        in_specs=[qspec, kspec, kspec, qspec, qspec, qspec],
        out_specs=[qspec, kspec, kspec],
        out_shape=[jax.ShapeDtypeStruct((n, HEADS_W), F32), jax.ShapeDtypeStruct((nk, HEADS_W), F32),
                   jax.ShapeDtypeStruct((nk, HEADS_W), F32)],
        compiler_params=_cparams(("parallel", "arbitrary")),
    )(q, k, v, o, do, lse)


SCAN_LEVELS = (1, 2, 4)
SCAN_LANES = 512


def _scan_chunk(xr, xi, car, m_ref, p_ref, reverse):
    t_len, gn = xr.shape
    n_slab = t_len // SUBLANES
    for lb in range(gn // SCAN_LANES):
        ls = pl.ds(lb * SCAN_LANES, SCAN_LANES)

        def step(s, carry, ls=ls):
            cr, ci = carry
            slab = (n_slab - 1 - s) if reverse else s
            rows = pl.ds(pl.multiple_of(slab * SUBLANES, SUBLANES), SUBLANES)
            br, bi = xr[rows, ls], xi[rows, ls]
            for lvl, d in enumerate(SCAN_LEVELS):
                shift = SUBLANES - d if reverse else d
                sr, si = pltpu.roll(br, shift, 0), pltpu.roll(bi, shift, 0)
                mr, mi = m_ref[lvl, 0, :, ls], m_ref[lvl, 1, :, ls]
                br, bi = br + mr * sr - mi * si, bi + mr * si + mi * sr
            pr, pi = p_ref[0, :, ls], p_ref[1, :, ls]
            br, bi = br + pr * cr - pi * ci, bi + pr * ci + pi * cr
            xr[rows, ls] = br
            xi[rows, ls] = bi
            last = 0 if reverse else SUBLANES - 1
            return br[last:last + 1, :], bi[last:last + 1, :]

        cr, ci = lax.fori_loop(0, n_slab, step, (car[0:1, ls], car[1:2, ls]))
        car[0:1, ls] = cr
        car[1:2, ls] = ci


SSM_SPLIT = 2


def _bd_dot(a, w_ref):
    k, n = w_ref.shape[0] // SSM_SPLIT, w_ref.shape[1] // SSM_SPLIT
    return jnp.concatenate([jnp.dot(a[:, p * k:(p + 1) * k], w_ref[p * k:(p + 1) * k, p * n:(p + 1) * n],
                                    preferred_element_type=F32) for p in range(SSM_SPLIT)], axis=1)


def _bd_dot_nt(a, w_ref):
    k, n = w_ref.shape[0] // SSM_SPLIT, w_ref.shape[1] // SSM_SPLIT
    return jnp.concatenate([_dot_nt(a[:, p * n:(p + 1) * n], w_ref[p * k:(p + 1) * k, p * n:(p + 1) * n])
                            for p in range(SSM_SPLIT)], axis=1)


def _seq_block(step, n_chunk, lat_chunks, reverse):
    if reverse:
        return n_chunk - 1 - step
    return (step + lat_chunks) % n_chunk


def ssm_fwd(proj, lay, bre, bim, cre, ncim, tabs, n, t_len, reverse, name):
    l, sw = proj.shape[0], lay['sw']
    gn = bre.shape[-1]
    n_chunk, lat_chunks = l // t_len, n // t_len
    blk = lambda k: _seq_block(k, n_chunk, lat_chunks, reverse)
    mtab, ptab = tabs

    def body(u_ref, bre_ref, bim_ref, cre_ref, ncim_ref, m_ref, p_ref, y_ref, xb_ref, xr, xi, car):
        @pl.when(pl.program_id(0) == 0)
        def _():
            car[...] = jnp.zeros_like(car)

        xb_ref[...] = car[...]
        u = u_ref[...].astype(BF16)
        xr[...] = _bd_dot(u, bre_ref)
        xi[...] = _bd_dot(u, bim_ref)
        _scan_chunk(xr, xi, car, m_ref, p_ref, reverse)
        y_ref[...] = _bd_dot(xr[...].astype(BF16), cre_ref) + _bd_dot(xi[...].astype(BF16), ncim_ref)

    return pl.pallas_call(
        body, name=name, grid=(n_chunk,),
        in_specs=[pl.BlockSpec((t_len, sw), lambda k: (blk(k), lay['o_u'] // sw)), _full(bre.shape),
                  _full(bim.shape), _full(cre.shape), _full(ncim.shape), _full(mtab.shape), _full(ptab.shape)],
        out_specs=[pl.BlockSpec((t_len, sw), lambda k: (blk(k), 0)),
                   pl.BlockSpec((None, 2, gn), lambda k: (k, 0, 0))],
        out_shape=[jax.ShapeDtypeStruct((l, sw), F32), jax.ShapeDtypeStruct((n_chunk, 2, gn), F32)],
        scratch_shapes=[pltpu.VMEM((t_len, gn), F32), pltpu.VMEM((t_len, gn), F32), pltpu.VMEM((2, gn), F32)],
        compiler_params=_cparams(("arbitrary",)),
    )(proj, bre, bim, cre, ncim, mtab, ptab)


def ssm_bwd(proj, lay, dyr, xb, bre, bim, cre, ncim, tabs, adj_tabs, n, t_len, reverse, name):
    l, sw = proj.shape[0], lay['sw']
    gn = bre.shape[-1]
    n_chunk, lat_chunks = l // t_len, n // t_len
    fwd_step = lambda k: n_chunk - 1 - k
    blk = lambda k: _seq_block(fwd_step(k), n_chunk, lat_chunks, reverse)
    (mtab, ptab), (mtab_r, ptab_r) = tabs, adj_tabs

    def body(u_ref, dy_ref, xb_ref, bre_ref, bim_ref, cre_ref, ncim_ref, m_ref, p_ref, mr_ref, pr_ref,
             du_ref, gr_ref, gi_ref, xr_ref, xi_ref, dlam_ref, xr, xi, gr, gi, car, acar):
        k = pl.program_id(0)

        @pl.when(k == 0)
        def _():
            acar[...] = jnp.zeros_like(acar)
            dlam_ref[...] = jnp.zeros_like(dlam_ref)

        u = u_ref[...].astype(BF16)
        dy = jnp.where(blk(k) < lat_chunks, dy_ref[...], 0.0).astype(BF16)
        xr[...] = _bd_dot(u, bre_ref)
        xi[...] = _bd_dot(u, bim_ref)
        car[...] = xb_ref[...]
        _scan_chunk(xr, xi, car, m_ref, p_ref, reverse)
        gr[...] = _bd_dot_nt(dy, cre_ref)
        gi[...] = _bd_dot_nt(dy, ncim_ref)
        _scan_chunk(gr, gi, acar, mr_ref, pr_ref, not reverse)
        xrv, xiv, grv, giv = xr[...], xi[...], gr[...], gi[...]
        row = lax.broadcasted_iota(jnp.int32, (t_len, gn), 0)
        first, shift = (t_len - 1, t_len - 1) if reverse else (0, 1)
        xpr = jnp.where(row == first, xb_ref[0:1, :], pltpu.roll(xrv, shift, 0))
        xpi = jnp.where(row == first, xb_ref[1:2, :], pltpu.roll(xiv, shift, 0))
        dlr = grv * xpr + giv * xpi
        dli = giv * xpr - grv * xpi
        dlam_ref[0] += jnp.sum(dlr.reshape(t_len // SUBLANES, SUBLANES, gn), axis=0)
        dlam_ref[1] += jnp.sum(dli.reshape(t_len // SUBLANES, SUBLANES, gn), axis=0)
        grb, gib = grv.astype(BF16), giv.astype(BF16)
        du_ref[...] = _bd_dot_nt(grb, bre_ref) + _bd_dot_nt(gib, bim_ref)
        gr_ref[...] = grb
        gi_ref[...] = gib
        xr_ref[...] = xrv.astype(BF16)
        xi_ref[...] = xiv.astype(BF16)

    def at_blk(width, col=0):
        return pl.BlockSpec((t_len, width), lambda k: (blk(k), col))

    state = jax.ShapeDtypeStruct((l, gn), BF16)
    return pl.pallas_call(
        body, name=name, grid=(n_chunk,),
        in_specs=[at_blk(sw, lay['o_u'] // sw),
                  pl.BlockSpec((t_len, sw), lambda k: (jnp.minimum(blk(k), lat_chunks - 1), 0)),
                  pl.BlockSpec((None, 2, gn), lambda k: (fwd_step(k), 0, 0)),
                  _full(bre.shape), _full(bim.shape), _full(cre.shape), _full(ncim.shape), _full(mtab.shape),
                  _full(ptab.shape), _full(mtab_r.shape), _full(ptab_r.shape)],
        out_specs=[at_blk(sw), at_blk(gn), at_blk(gn), at_blk(gn), at_blk(gn), _full((2, SUBLANES, gn))],
        out_shape=[jax.ShapeDtypeStruct((l, sw), F32), state, state, state, state,
                   jax.ShapeDtypeStruct((2, SUBLANES, gn), F32)],
        scratch_shapes=[pltpu.VMEM((t_len, gn), F32)] * 4 + [pltpu.VMEM((2, gn), F32)] * 2,
        compiler_params=_cparams(("arbitrary",)),
    )(proj, dyr, xb, bre, bim, cre, ncim, mtab, ptab, mtab_r, ptab_r)


def ssm_prep(lam_re, lam_im, logdt, bre, bim, name):
    gn = lam_re.shape[1]

    def body(lr_ref, li_ref, dt_ref, br_ref, bi_ref, pwr_ref, pwi_ref, bbr_ref, bbi_ref):
        _, _, bbr, bbi = _ssm_prep_fn(lr_ref[...], li_ref[...], dt_ref[...], br_ref[...], bi_ref[...])
        bbr_ref[...] = bbr
        bbi_ref[...] = bbi
        kk = (lax.broadcasted_iota(jnp.int32, (SUBLANES, gn), 0) + 1).astype(F32)
        dt = jnp.exp(dt_ref[...])
        ar, ai = lr_ref[...] * dt * kk, li_ref[...] * dt * kk
        e = jnp.exp(ar)
        pwr_ref[...] = e * jnp.cos(ai)
        pwi_ref[...] = e * jnp.sin(ai)

    ins = (lam_re, lam_im, logdt, bre, bim)
    return pl.pallas_call(
        body, name=name,
        out_shape=[jax.ShapeDtypeStruct((SUBLANES, gn), F32)] * 2 + [jax.ShapeDtypeStruct(bre.shape, F32)] * 2,
        compiler_params=_cparams(),
    )(*ins)


def ssm_prep_bwd(lam_re, lam_im, logdt, bre, bim, dlbr, dlbi, dbbr, dbbi, name):
    def body(lr_ref, li_ref, dt_ref, br_ref, bi_ref, g0, g1, g2, g3, o0, o1, o2, o3, o4):
        _, vjp = jax.vjp(_ssm_prep_fn, lr_ref[...], li_ref[...], dt_ref[...], br_ref[...], bi_ref[...])
        for ref, val in zip((o0, o1, o2, o3, o4), vjp((g0[...], g1[...], g2[...], g3[...]))):
            ref[...] = val

    ins = (lam_re, lam_im, logdt, bre, bim)
    return pl.pallas_call(
        body, name=name,
        out_shape=[jax.ShapeDtypeStruct(a.shape, F32) for a in ins],
        compiler_params=_cparams(),
    )(*ins, dlbr, dlbi, dbbr, dbbi)


def mod_fwd(cmat, w, b, name):
    def body(c_ref, w_ref, b_ref, o_ref):
        o_ref[...] = _mod_fn(c_ref[...], w_ref[...]) + b_ref[...]

    return pl.pallas_call(body, name=name, out_shape=jax.ShapeDtypeStruct((cmat.shape[0], w.shape[1]), F32),
                          compiler_params=_cparams())(cmat, w, b)


def mod_bwd(cmat, w, g_lat, g_ctx, g_all, name):
    def body(c_ref, w_ref, gl_ref, gc_ref, ga_ref, dw_ref, dc_ref, db_ref):
        gc = jnp.sum(gc_ref[...], axis=0, keepdims=True)
        dm = jnp.concatenate([gl_ref[...], _pad_rows(gc)], axis=0)
        _, vjp = jax.vjp(_mod_fn, c_ref[...], w_ref[...])
        dc, dw = vjp(dm)
        dw_ref[...] = dw
        dc_ref[...] = dc
        db_ref[...] = _pad_rows(jnp.sum(ga_ref[...], axis=0, keepdims=True))

    return pl.pallas_call(
        body, name=name,
        out_shape=[jax.ShapeDtypeStruct(w.shape, F32), jax.ShapeDtypeStruct(cmat.shape, F32),
                   jax.ShapeDtypeStruct((SUBLANES, g_all.shape[1]), F32)],
        compiler_params=_cparams(),
    )(cmat, w, g_lat, g_ctx, g_all)


def add_own(own, idx, recv, out_dtype, name):
    _, s, r, c = own.shape

    def body(idx_ref, own_ref, recv_ref, o_ref):
        o_ref[...] = (own_ref[...] + recv_ref[...]).astype(o_ref.dtype)

    return pl.pallas_call(
        body, name=name,
        grid_spec=pltpu.PrefetchScalarGridSpec(
            num_scalar_prefetch=1, grid=(s,),
            in_specs=[pl.BlockSpec((None, None, r, c), lambda k, idx_ref: (idx_ref[0], k, 0, 0)),
                      pl.BlockSpec((None, r, c), lambda k, idx_ref: (k, 0, 0))],
            out_specs=pl.BlockSpec((None, r, c), lambda k, idx_ref: (k, 0, 0))),
        out_shape=jax.ShapeDtypeStruct((s, r, c), out_dtype),
        compiler_params=_cparams(("parallel",)),
    )(idx, own, recv)


def reduce_adamw(parts, w, m, v, name, own=None, idx=None):
    s, r, c = parts.shape
    tr = _tile(r, (256, 128, 64, 32, 16, 8))

    def body(*refs):
        if own is None:
            p_ref, w_ref, m_ref, v_ref, g_ref, d_ref, nm_ref, nv_ref = refs
            g = p_ref[0].astype(F32)
        else:
            _, own_ref, p_ref, w_ref, m_ref, v_ref, g_ref, d_ref, nm_ref, nv_ref = refs
            g = own_ref[...].astype(F32) + p_ref[0].astype(F32)
        for k in range(1, s):
            g = g + p_ref[k].astype(F32)
        mm = ADAM_B1 * m_ref[...] + (1.0 - ADAM_B1) * g
        vv = ADAM_B2 * v_ref[...] + (1.0 - ADAM_B2) * jnp.square(g)
        m_hat = mm / (1.0 - ADAM_B1 ** ADAM_STEP)
        v_hat = vv / (1.0 - ADAM_B2 ** ADAM_STEP)
        g_ref[...] = g
        d_ref[...] = -ADAM_LR * (m_hat / (jnp.sqrt(v_hat) + ADAM_EPS) + ADAM_WD * w_ref[...])
        nm_ref[...] = mm
        nv_ref[...] = vv

    out = jax.ShapeDtypeStruct((r, c), F32)
    if own is None:
        blk = _rows(tr, c)
        return pl.pallas_call(
            body, name=name, grid=(r // tr,),
            in_specs=[pl.BlockSpec((s, tr, c), lambda i: (0, i, 0)), blk, blk, blk],
            out_specs=[blk] * 4, out_shape=[out] * 4,
            compiler_params=_cparams(("parallel",)),
        )(parts, w, m, v)
    blk = pl.BlockSpec((tr, c), lambda i, idx_ref: (i, 0))
    return pl.pallas_call(
        body, name=name,
        grid_spec=pltpu.PrefetchScalarGridSpec(
            num_scalar_prefetch=1, grid=(r // tr,),
            in_specs=[pl.BlockSpec((None, tr, c), lambda i, idx_ref: (idx_ref[0], i, 0)),
                      pl.BlockSpec((s, tr, c), lambda i, idx_ref: (0, i, 0)), blk, blk, blk],
            out_specs=[blk] * 4),
        out_shape=[out] * 4,
        compiler_params=_cparams(("parallel",)),
    )(idx, own, parts, w, m, v)


def _in_layout(d, q, kv, sw):
    o_u = 2 * d
    o_ckv = o_u + sw
    o_kr = o_ckv + kv
    o_cq = -(-(o_kr + LANES) // q) * q
    assert o_u % sw == 0 and o_ckv % kv == 0 and o_kr % LANES == 0
    assert q % LANES == 0 and kv % LANES == 0 and sw % LANES == 0
    return dict(d=d, q=q, kv=kv, sw=sw, o_gl=0, o_u=o_u, o_ckv=o_ckv, o_kr=o_kr, o_cq=o_cq, width=o_cq + q)


def _pad_w_in(w_in, lay):
    q, kv, sw, d = lay['q'], lay['kv'], lay['sw'], lay['d']
    cq, ckv, kr, u, gl = jnp.split(w_in, [q, q + kv, q + kv + QK_ROPE, q + kv + QK_ROPE + sw], axis=1)
    z = lambda w: jnp.zeros((w_in.shape[0], w), w_in.dtype)
    hole = lay['o_cq'] - lay['o_kr'] - LANES
    return jnp.concatenate([gl, u, ckv, z(QK_NOPE), kr, z(LANES - QK_DIM), z(hole), cq], axis=1)


def _unpad_w_in(g, lay):
    q, kv, sw, d = lay['q'], lay['kv'], lay['sw'], lay['d']
    kr0 = lay['o_kr'] + QK_NOPE
    return jnp.concatenate([g[:, lay['o_cq']:lay['o_cq'] + q], g[:, lay['o_ckv']:lay['o_ckv'] + kv],
                            g[:, kr0:kr0 + QK_ROPE], g[:, lay['o_u']:lay['o_u'] + sw], g[:, :2 * d]], axis=1)


def _pad_heads(w, width):
    k = w.shape[0]
    return jnp.pad(w.reshape(k, N_HEADS, width), ((0, 0), (0, 0), (0, HEAD_PAD - width))).reshape(k, HEADS_W)


def _unpad_heads(w, width):
    k = w.shape[0]
    return w.reshape(k, N_HEADS, HEAD_PAD)[:, :, :width].reshape(k, N_HEADS * width)


def _rope_tables(n, nc):
    rows = n // GRID_W
    row = jnp.repeat(jnp.arange(rows), GRID_W)
    col = jnp.tile(jnp.arange(GRID_W), rows)
    pairs = QK_ROPE // 4
    freqs = ROPE_THETA ** (-jnp.arange(pairs, dtype=F32) / pairs)
    ang = jnp.concatenate([row[:, None] * freqs, col[:, None] * freqs], axis=-1)
    cos = jnp.concatenate([jnp.cos(ang), jnp.ones((nc, 2 * pairs), F32)], axis=0)
    sin = jnp.concatenate([jnp.sin(ang), jnp.zeros((nc, 2 * pairs), F32)], axis=0)
    l = n + nc
    half = QK_ROPE // 2
    ct = jnp.concatenate([jnp.ones((l, QK_NOPE), F32), cos, cos, jnp.zeros((l, HEAD_PAD - QK_DIM), F32)], axis=1)
    s1 = jnp.concatenate([jnp.zeros((l, QK_NOPE + half), F32), sin, jnp.zeros((l, HEAD_PAD - QK_DIM), F32)],
                         axis=1)
    s2 = jnp.concatenate([jnp.zeros((l, QK_NOPE), F32), -sin, jnp.zeros((l, HEAD_PAD - QK_NOPE - half), F32)],
                         axis=1)
    return ct, s1, s2


def _block_diag(m):
    g, a, b = m.shape
    eye = jnp.eye(g, dtype=m.dtype)
    return (m[:, :, None, :] * eye[:, None, :, None]).reshape(g * a, g * b)


def _diag_blocks(m, g):
    a, b = m.shape[0] // g, m.shape[1] // g
    m4 = m.reshape(g, a, g, b)
    return jnp.sum(m4 * jnp.eye(g, dtype=m.dtype)[:, None, :, None], axis=2)


def _scan_tables(pwr, pwi, reverse):
    row = jnp.arange(SUBLANES)[:, None]
    zero = jnp.zeros_like(pwr)
    levels = []
    for d in SCAN_LEVELS:
        keep = (row < SUBLANES - d) if reverse else (row >= d)
        levels.append(jnp.stack([jnp.where(keep, pwr[d - 1:d, :], zero), jnp.where(keep, pwi[d - 1:d, :], zero)]))
    carry = jnp.stack([pwr[::-1], pwi[::-1]]) if reverse else jnp.stack([pwr, pwi])
    return jnp.stack(levels), carry


def _step(inp):
    x, c, ctx = inp['x'][0], inp['c'], inp['ctx'][0]
    target = inp['loss_target'][0]
    n, d = x.shape
    nc = ctx.shape[0]
    l = n + nc
    q_w, kv_w = inp['q_a_g'].shape[1], inp['kv_a_g'].shape[1]
    sw = inp['d_skip'].shape[1]
    n_grp = sw // SSM_GROUP
    gn = n_grp * SSM_STATE
    lay = _in_layout(d, q_w, kv_w, sw)
    tm = _tile(math.gcd(n, nc), (256, 128))
    me = 4 * lax.axis_index("x") + 2 * lax.axis_index("y") + lax.axis_index("c")
    strip = lambda a: a if a.ndim <= 2 else a[0]
    w = {k: strip(inp[k]) for k in WEIGHT_NAMES}

    gathered_names = list(GATHERED)
    c_all, *wg = gather_two_level([jnp.broadcast_to(c, (SUBLANES, d))] + [w[k].astype(BF16) for k in gathered_names],
                                  "gather_weights")
    full = {k: _from_shards(s, GATHERED[k]) for k, s in zip(gathered_names, wg)}

    w_in_p = _pad_w_in(full['w_in'], lay)
    wuq_p = _pad_heads(full['w_uq'], QK_DIM)
    ukv = full['w_ukv'].reshape(kv_w, N_HEADS, QK_NOPE + V_DIM)
    wk_p = _pad_heads(ukv[:, :, :QK_NOPE].reshape(kv_w, -1), QK_NOPE)
    wv_p = _pad_heads(ukv[:, :, QK_NOPE:].reshape(kv_w, -1), V_DIM)
    wo_p = _pad_heads(full['w_o_attn'].T, V_DIM).T
    w_glu, w_out, w_up, w_down = full['w_glu'], full['w_out'], full['w_up'], full['w_down']
    conv_w = jnp.pad(full['conv_w'].astype(F32), ((0, SUBLANES - 3), (0, 0)))
    conv_b = w['conv_b']
    f2 = w_up.shape[1]

    cmat = jnp.concatenate([c_all[:, 0, :], w['c_ctx'][None, :], jnp.zeros((SUBLANES - 1, d), F32)], axis=0)
    mcols = w['w_mod'].shape[1]
    b_cols = lax.dynamic_slice(w['b_mod'], (0, me * mcols), (1, mcols))
    mod_part = mod_fwd(cmat, w['w_mod'], b_cols, "mod_fwd")
    (mod_all,) = exchange([mod_part], ['gather'], "gather_mod")
    mod_me = lax.dynamic_index_in_dim(mod_all, me, axis=1, keepdims=False).reshape(6, d)
    mod_ctx = mod_all[:, SUBLANES, :].reshape(6, d)
    sh1, sc1, g1, sh2, sc2, g2 = [mod_me[k:k + 1] for k in range(6)]
    mods1 = jnp.concatenate([sc1, sh1, mod_ctx[1:2], mod_ctx[0:1], jnp.zeros((4, d), F32)], axis=0)
    mods2 = jnp.concatenate([sc2, sh2, jnp.zeros((6, d), F32)], axis=0)

    xa = jnp.concatenate([x, ctx], axis=0)
    h = norm_mod_fwd(xa, w['norm1_g'], mods1, n, tm, "norm1_fwd")
    proj = matmul(h, w_in_p, 'nn', F32, "in_proj")
    tabs = _rope_tables(n, nc)
    pad_g = lambda g: jnp.pad(g, ((0, 0), (0, HEAD_PAD - QK_DIM)))
    gains = (w['q_a_g'], w['kv_a_g'], pad_g(w['q_norm_g']), pad_g(w['k_norm_g']))
    q, k, v = qkv_fwd(proj, lay, gains, wuq_p, wk_p, wv_p, tabs, tm, "qkv_fwd")
    o, lse = attn_fwd(q, k, v, n, "attn_fwd")

    b_t = lambda a: a.transpose(2, 0, 1).reshape(SSM_GROUP, gn)
    c_t = lambda a: a.transpose(1, 0, 2).reshape(SSM_GROUP, gn)
    bre_t, bim_t = b_t(w['b_re']), b_t(w['b_im'])
    ssm_in, prep = [], []
    for sfx in ('f', 'b'):
        lam_re, lam_im = w['lam_re_' + sfx].reshape(1, gn), w['lam_im_' + sfx].reshape(1, gn)
        logdt = jnp.repeat(w['log_dt_' + sfx], SSM_STATE, axis=1)
        ssm_in.append((lam_re, lam_im, logdt))
        prep.append(ssm_prep(lam_re, lam_im, logdt, bre_t, bim_t, "ssm_prep_" + sfx))

    def blk_b(bb):
        return _block_diag(bb.reshape(SSM_GROUP, n_grp, SSM_STATE).transpose(1, 0, 2)).astype(BF16)

    def blk_c(cc):
        return _block_diag(cc.transpose(0, 2, 1)).astype(BF16)

    t_len = _tile(math.gcd(n, nc), (256, 128))
    ssm = []
    for di, sfx in enumerate(('f', 'b')):
        reverse = di == 1
        pwr, pwi, bbr, bbi = prep[di]
        ssm.append(dict(
            sfx=sfx, reverse=reverse, blocks=(blk_b(bbr), blk_b(bbi), blk_c(w['c_re_' + sfx]),
                                              blk_c(-w['c_im_' + sfx])),
            tabs=_scan_tables(pwr, pwi, reverse), adj_tabs=_scan_tables(pwr, -pwi, not reverse)))
    for s in ssm:
        s['y'], s['xb'] = ssm_fwd(proj, lay, *s['blocks'], s['tabs'], n, t_len, s['reverse'], "ssm_fwd_" + s['sfx'])
    yf, yb = ssm[0]['y'], ssm[1]['y']
    s_l = glu_fwd(proj, lay, yf, yb, w['d_skip'], w_glu, n, tm, "glu_fwd")
    x1 = merge_fwd(o, s_l, proj, xa, g1, wo_p, w_out, n, tm, "merge_fwd")

    h2 = norm_mod_fwd(x1, w['norm2_g'], mods2, n, tm, "norm2_fwd")
    up = matmul(h2, w_up, 'nn', F32, "up_proj")
    tw = _tile(n, (128,))
    act = conv_act_fwd(up, conv_w, conv_b, tw, "conv_act_fwd")
    dy, ffn, loss_parts = down_loss(act, w_down, x1, g2, target, tm, "down_loss")
    loss = lax.psum(jnp.sum(loss_parts[:, 0, 0]), MESH_AXES)

    du2, dffn, dg2 = down_bwd(dy, ffn, up, conv_w, conv_b, w_down, g2, tw, "down_bwd")
    g_w_down = matmul(act, dffn, 'tn', F32, "dw_down")
    dup, dconv = conv_bwd(du2, up, conv_w, tw, "conv_bwd")
    dh2 = matmul(dup, w_up, 'nt', F32, "dh2")
    g_w_up = matmul(h2, dup, 'tn', F32, "dw_up")
    dx1, dn2g, dmods2 = norm_mod_bwd(x1, dh2, dy, w['norm2_g'], mods2, n, tm, "norm2_bwd")

    do, ds_l, dgl, dg1, g_wo_p, g_w_out = merge_bwd(o, s_l, proj, xa, g1, wo_p, w_out, dx1, n, tm, "merge_bwd")
    du_direct, dyr, dds, g_w_glu = glu_bwd(proj, lay, yf, yb, w['d_skip'], w_glu, ds_l, n, tm, "glu_bwd")
    for s in ssm:
        s['du'], s['gr'], s['gi'], s['xr'], s['xi'], s['dlam'] = ssm_bwd(
            proj, lay, dyr, s['xb'], *s['blocks'], s['tabs'], s['adj_tabs'], n, t_len, s['reverse'],
            "ssm_bwd_" + s['sfx'])

    dq, dk, dv = attn_bwd(q, k, v, o, do, lse, n, "attn_bwd")
    (dproj, dqag, dkvag, dqng, dkng, g_wuq_p, g_wk_p, g_wv_p) = qkv_bwd(
        proj, lay, gains, wuq_p, wk_p, wv_p, tabs, dq, dk, dv, dgl, du_direct, ssm[0]['du'], ssm[1]['du'], n, tm,
        "qkv_bwd")
    dh = matmul(dproj, w_in_p, 'nt', F32, "dh")
    g_w_in_p = matmul(h, dproj, 'tn', F32, "dw_in")
    dxa, dn1g, dmods1 = norm_mod_bwd(xa, dh, dx1, w['norm1_g'], mods1, n, tm, "norm1_bwd")
    grad_x = dxa[:n]

    grads = {}
    d_bbar = [None, None]
    for di, s in enumerate(ssm):
        sfx = s['sfx']
        def outer(a, a_col0, b, nm):
            ka, kb, gp = sw // SSM_SPLIT, gn // SSM_SPLIT, n_grp // SSM_SPLIT
            return jnp.concatenate([_diag_blocks(
                matmul(a, b, 'tn', F32, "%s_%s%d" % (nm, sfx, p), a_col0=a_col0 + p * ka, a_cols=ka,
                       b_col0=p * kb, b_cols=kb), gp) for p in range(SSM_SPLIT)], axis=0)

        g_bre = outer(proj, lay['o_u'], s['gr'], "ssm_db_re")
        g_bim = outer(proj, lay['o_u'], s['gi'], "ssm_db_im")
        g_cre = outer(dyr, 0, s['xr'], "ssm_dc_re")
        g_cim = outer(dyr, 0, s['xi'], "ssm_dc_im")
        grads['c_re_' + sfx] = g_cre
        grads['c_im_' + sfx] = -g_cim
        to_t = lambda a: a.transpose(1, 0, 2).reshape(SSM_GROUP, gn)
        dlam_re = jnp.sum(s['dlam'][0], axis=0, keepdims=True)
        dlam_im = jnp.sum(s['dlam'][1], axis=0, keepdims=True)
        lam_re, lam_im, logdt = ssm_in[di]
        g_lr, g_li, g_dt, g_br, g_bi = ssm_prep_bwd(lam_re, lam_im, logdt, bre_t, bim_t, dlam_re, dlam_im,
                                                    to_t(g_bre), to_t(g_bim), "ssm_prep_bwd_" + sfx)
        grads['lam_re_' + sfx] = g_lr.reshape(n_grp, SSM_STATE)
        grads['lam_im_' + sfx] = g_li.reshape(n_grp, SSM_STATE)
        grads['log_dt_' + sfx] = jnp.sum(g_dt.reshape(n_grp, SSM_STATE), axis=1)[None, :]
        d_bbar[di] = (g_br, g_bi)
    from_t = lambda a: a.reshape(SSM_GROUP, n_grp, SSM_STATE).transpose(1, 2, 0)
    grads['b_re'] = from_t(d_bbar[0][0]) + from_t(d_bbar[1][0])
    grads['b_im'] = from_t(d_bbar[0][1]) + from_t(d_bbar[1][1])

    dmod = jnp.concatenate([dmods1[1:2], dmods1[0:1], dg1[0:1], dmods2[1:2], dmods2[0:1], dg2[0:1]], axis=1)
    dmod_ctx = jnp.concatenate([dmods1[3:4], dmods1[2:3], jnp.zeros((1, 4 * d), F32)], axis=1)
    dm_send = jnp.concatenate([dmod, dmod_ctx, jnp.zeros((SUBLANES - 2, 6 * d), F32)], axis=0)
    (dm_all,) = exchange([dm_send], ['gather'], "gather_dmod")
    g_all = jnp.concatenate([dm_all[:, 0, :], dm_all[:, 1, :]], axis=0)
    cols = lax.dynamic_slice(g_all.reshape(2 * N_DEV, N_DEV, mcols), (0, me, 0), (2 * N_DEV, 1, mcols))[:, 0, :]
    g_w_mod, dcmat, g_b_mod = mod_bwd(cmat, w['w_mod'], cols[:N_DEV], cols[N_DEV:], g_all, "mod_bwd")

    ukv_g = jnp.concatenate([_unpad_heads(g_wk_p, QK_NOPE).reshape(kv_w, N_HEADS, QK_NOPE),
                             _unpad_heads(g_wv_p, V_DIM).reshape(kv_w, N_HEADS, V_DIM)], axis=2)
    full_g = {'w_in': _unpad_w_in(g_w_in_p, lay), 'w_uq': _unpad_heads(g_wuq_p, QK_DIM),
              'w_ukv': ukv_g.reshape(kv_w, -1), 'w_o_attn': _unpad_heads(g_wo_p.T, V_DIM).T, 'w_glu': g_w_glu,
              'w_out': g_w_out, 'w_up': g_w_up, 'conv_w': dconv[0:3], 'w_down': g_w_down}
    def by_core(full_grad, axis):
        sh = _to_shards(full_grad, axis)
        return sh.reshape((N_CHIPS, 2) + sh.shape[1:]).swapaxes(0, 1)

    sends = [by_core(full_g[k], GATHERED[k]) for k in gathered_names]
    grads.update({'c_ctx': dcmat[SUBLANES], 'b_mod': g_b_mod[0:1], 'norm1_g': dn1g[0:1], 'norm2_g': dn2g[0:1],
                  'q_a_g': dqag[0:1], 'kv_a_g': dkvag[0:1], 'q_norm_g': dqng[0:1, :QK_DIM],
                  'k_norm_g': dkng[0:1, :QK_DIM], 'd_skip': dds[0:1], 'conv_b': dconv[3:4]})
    first = (me == 0).astype(F32)
    rep_parts = [grads[k] * first if k == 'b_mod' else grads[k] for k in REPLICATED]
    rpack, rspans = _pack(rep_parts, SUBLANES)
    my_core = lax.axis_index("c").astype(jnp.int32).reshape(1)
    my_chip = (2 * lax.axis_index("x") + lax.axis_index("y")).astype(jnp.int32).reshape(1)
    r_both = jnp.broadcast_to(rpack[None, None], (2, 1) + rpack.shape)
    from_sibling = exchange(sends + [r_both], ['others'] * (len(sends) + 1), "grads_in_chip", group='core')
    chip_sums = [add_own(s, my_core, got[0], BF16, "chip_sum_" + k)
                 for s, got, k in zip(sends, from_sibling, gathered_names)]
    r_sum = add_own(r_both, my_core, from_sibling[-1][0], F32, "chip_sum_replicated")[0]
    *g_recv, r_recv = exchange(chip_sums + [r_sum], ['others'] * len(sends) + ['gather'], "grads_between_chips",
                               group='chips')

    outs = {}

    def update(parts, k, w_k, m_k, v_k, **own):
        res = reduce_adamw(parts, w_k, m_k, v_k, "adamw_" + k, **own)
        return dict(zip(('grad_', 'delta_', 'new_m_', 'new_v_'), res))

    for k, parts, own in zip(gathered_names + ['w_mod'], g_recv + [g_w_mod[None]], chip_sums + [None]):
        own_args = {} if own is None else dict(own=own, idx=my_chip)
        for kind, a in update(parts, k, w[k], strip(inp['m_' + k]), strip(inp['v_' + k]), **own_args).items():
            outs[kind + k] = a[None]
    rep = lambda prefix: _pack([strip(inp[prefix + k]) for k in REPLICATED], SUBLANES)[0]
    for kind, buf in update(r_recv, "replicated", rep(''), rep('m_'), rep('v_')).items():
        for k, a in zip(REPLICATED, _unpack(buf, rspans, [w[k].shape for k in REPLICATED])):
            outs[kind + k] = a if inp[k].ndim <= 2 else a[None]
    result = [loss, grad_x[None]]
    for kind in ('grad_', 'delta_', 'new_m_', 'new_v_'):
        result += [outs[kind + k] for k in WEIGHT_NAMES]
    return tuple(result)


_ARG_NAMES = (['x', 'c', 'ctx'] + WEIGHT_NAMES + ['loss_target'] + ['m_' + k for k in WEIGHT_NAMES]
              + ['v_' + k for k in WEIGHT_NAMES])


def kernel(*args):
    assert len(args) == len(_ARG_NAMES)
    return _step(dict(zip(_ARG_NAMES, args)))
```

```python
import functools
import math

import jax
import jax.numpy as jnp
from jax import lax
from jax.experimental import pallas as pl
from jax.experimental.pallas import tpu as pltpu

F32 = jnp.float32
BF16 = jnp.bfloat16

N_DEV = 8
MESH_AXES = ("x", "y", "c")
N_HEADS = 8
QK_NOPE = 64
QK_ROPE = 32
QK_DIM = QK_NOPE + QK_ROPE
V_DIM = 64
HEAD_PAD = 128
HEADS_W = N_HEADS * HEAD_PAD
GRID_W = 64
ROPE_THETA = 10000.0
SSM_GROUP = 16
SSM_STATE = 64
EPS = 1e-6
LANES = 128
SUBLANES = 8
PACK_W = 1024
VMEM_LIMIT = 56 * 1024 * 1024
MM_TILES = (1024, 768, 1408, 512, 384, 256, 128)

ADAM_LR = 0.001
ADAM_B1 = 0.9
ADAM_B2 = 0.999
ADAM_EPS = 1e-08
ADAM_WD = 0.01
ADAM_STEP = 10

WEIGHT_NAMES = ['c_ctx', 'w_mod', 'b_mod', 'norm1_g', 'norm2_g', 'w_in', 'q_a_g', 'w_uq', 'kv_a_g', 'w_ukv',
                'q_norm_g', 'k_norm_g', 'w_o_attn', 'lam_re_f', 'lam_im_f', 'log_dt_f', 'c_re_f', 'c_im_f',
                'lam_re_b', 'lam_im_b', 'log_dt_b', 'c_re_b', 'c_im_b', 'b_re', 'b_im', 'd_skip', 'w_glu',
                'w_out', 'w_up', 'conv_w', 'conv_b', 'w_down']
GATHERED = {'w_in': 1, 'w_uq': 1, 'w_ukv': 1, 'w_o_attn': 1, 'w_glu': 1, 'w_out': 0, 'w_up': 1, 'conv_w': 1,
            'w_down': 0}
REPLICATED = [n for n in WEIGHT_NAMES if n not in GATHERED and n != 'w_mod']


def _tile(n, prefs):
    for t in prefs:
        if n % t == 0:
            return t
    return n


def _cparams(sem=None):
    return pltpu.CompilerParams(dimension_semantics=sem, vmem_limit_bytes=VMEM_LIMIT)


@jax.custom_vjp
def bdot(a, w):
    return jnp.dot(a.astype(BF16), w.astype(BF16), preferred_element_type=F32)


def _bdot_fwd(a, w):
    return bdot(a, w), (a, w)


def _bdot_bwd(res, g):
    a, w = res
    gb = g.astype(BF16)
    da = lax.dot_general(gb, w.astype(BF16), (((1,), (1,)), ((), ())), preferred_element_type=F32)
    dw = lax.dot_general(a.astype(BF16), gb, (((0,), (0,)), ((), ())), preferred_element_type=F32)
    return da.astype(a.dtype), dw.astype(w.dtype)


bdot.defvjp(_bdot_fwd, _bdot_bwd)


def _dot_nt(a, b):
    return lax.dot_general(a, b, (((1,), (1,)), ((), ())), preferred_element_type=F32)


def _dot_tn(a, b):
    return lax.dot_general(a, b, (((0,), (0,)), ((), ())), preferred_element_type=F32)


def matmul(a, b, mode, out_dtype, name, a_col0=0, a_cols=None, b_col0=0, b_cols=None):
    if mode == 'nn':
        (m, k), n = a.shape, b.shape[1]
    elif mode == 'nt':
        (m, k), n = a.shape, b.shape[0]
    else:
        (k, m), n = a.shape, b.shape[1]
        m, n = a_cols or m, b_cols or n
    tm = _tile(m, MM_TILES)
    tn = _tile(n, MM_TILES)
    tk = _tile(k, MM_TILES)
    nk = k // tk
    assert a_col0 % tm == 0 and b_col0 % tn == 0
    col0, bcol0 = a_col0 // tm, b_col0 // tn

    def body(a_ref, b_ref, o_ref, acc_ref):
        kk = pl.program_id(2)

        @pl.when(kk == 0)
        def _():
            acc_ref[...] = jnp.zeros_like(acc_ref)

        av, bv = a_ref[...].astype(BF16), b_ref[...].astype(BF16)
        if mode == 'nn':
            acc_ref[...] += jnp.dot(av, bv, preferred_element_type=F32)
        elif mode == 'nt':
            acc_ref[...] += _dot_nt(av, bv)
        else:
            acc_ref[...] += _dot_tn(av, bv)

        @pl.when(kk == nk - 1)
        def _():
            o_ref[...] = acc_ref[...].astype(o_ref.dtype)

    if mode == 'nn':
        a_spec = pl.BlockSpec((tm, tk), lambda i, j, kk: (i, kk))
        b_spec = pl.BlockSpec((tk, tn), lambda i, j, kk: (kk, j))
    elif mode == 'nt':
        a_spec = pl.BlockSpec((tm, tk), lambda i, j, kk: (i, kk))
        b_spec = pl.BlockSpec((tn, tk), lambda i, j, kk: (j, kk))
    else:
        a_spec = pl.BlockSpec((tk, tm), lambda i, j, kk: (kk, i + col0))
        b_spec = pl.BlockSpec((tk, tn), lambda i, j, kk: (kk, j + bcol0))
    return pl.pallas_call(
        body, name=name, grid=(m // tm, n // tn, nk),
        in_specs=[a_spec, b_spec],
        out_specs=pl.BlockSpec((tm, tn), lambda i, j, kk: (i, j)),
        out_shape=jax.ShapeDtypeStruct((m, n), out_dtype),
        scratch_shapes=[pltpu.VMEM((tm, tn), F32)],
        compiler_params=_cparams(("parallel", "parallel", "arbitrary")),
    )(a, b)


N_CHIPS = 4


def _group(group):
    x, y, c = lax.axis_index("x"), lax.axis_index("y"), lax.axis_index("c")
    flips = {'all': [(r & 4, r & 2, r & 1) for r in range(1, 8)],
             'chips': [(0, 1, 0), (1, 0, 0), (1, 1, 0)], 'core': [(0, 0, 1)]}[group]
    index = {'all': lambda px, py, pc: 4 * px + 2 * py + pc, 'chips': lambda px, py, pc: 2 * px + py,
             'core': lambda px, py, pc: pc}[group]
    peers = []
    for fx, fy, fc in flips:
        p = (1 - x if fx else x, 1 - y if fy else y, 1 - c if fc else c)
        peers.append((p, index(*p)))
    return len(flips) + 1, index(x, y, c), peers


def exchange(arrays, kinds, name, group='all'):
    n_arr = len(arrays)
    size = {'all': N_DEV, 'chips': N_CHIPS, 'core': 2}[group]

    def body(*refs):
        srcs, dsts = refs[:n_arr], refs[n_arr:2 * n_arr]
        send_sems, recv_sems, local_sems = refs[2 * n_arr:]
        _, me, peers = _group(group)

        def copy(a, r, peer, peer_idx, receiving):
            src = srcs[a] if kinds[a] == 'gather' else srcs[a].at[peer_idx]
            if kinds[a] == 'others':
                dst = dsts[a].at[r]
            else:
                dst = dsts[a].at[peer_idx if receiving else me]
            return pltpu.make_async_remote_copy(
                src_ref=src, dst_ref=dst, send_sem=send_sems.at[a * size + r], recv_sem=recv_sems.at[a * size + r],
                device_id=peer, device_id_type=pl.DeviceIdType.MESH)

        started = []
        for a in range(n_arr):
            if kinds[a] != 'others':
                mine = srcs[a] if kinds[a] == 'gather' else srcs[a].at[me]
                started.append(pltpu.make_async_copy(mine, dsts[a].at[me], local_sems.at[a]))
                started[-1].start()
            for r, (peer, peer_idx) in enumerate(peers):
                copy(a, r, peer, peer_idx, False).start()
        for a in range(n_arr):
            for r, (peer, peer_idx) in enumerate(peers):
                cp = copy(a, r, peer, peer_idx, True)
                cp.wait_send()
                cp.wait_recv()
        for local in started:
            local.wait()

    out_shape = []
    for a, arr in enumerate(arrays):
        shp = {'gather': (size,) + tuple(arr.shape), 'a2a': tuple(arr.shape),
               'others': (size - 1,) + tuple(arr.shape[1:])}[kinds[a]]
        out_shape.append(jax.ShapeDtypeStruct(shp, arr.dtype))
    any_spec = pl.BlockSpec(memory_space=pl.ANY)
    return pl.pallas_call(
        body, name=name,
        in_specs=[any_spec] * n_arr, out_specs=[any_spec] * n_arr, out_shape=out_shape,
        scratch_shapes=[pltpu.SemaphoreType.DMA((n_arr * size,)), pltpu.SemaphoreType.DMA((n_arr * size,)),
                        pltpu.SemaphoreType.DMA((n_arr,))],
        compiler_params=pltpu.CompilerParams(has_side_effects=True),
    )(*arrays)


def gather_two_level(arrays, name):
    n_arr = len(arrays)
    per = 7

    def body(*refs):
        srcs, outs = refs[:n_arr], refs[n_arr:2 * n_arr]
        send_sems, recv_sems, local_sems = refs[2 * n_arr:]
        x, y, c = lax.axis_index("x"), lax.axis_index("y"), lax.axis_index("c")
        me, sibling = (x, y, c), (x, y, 1 - c)
        chips = [(1 - x, y), (x, 1 - y), (1 - x, 1 - y)]
        slot = lambda p: 4 * p[0] + 2 * p[1] + p[2]

        def copy(a, k, block, to, src=None):
            dst = outs[a].at[slot(block)]
            return pltpu.make_async_remote_copy(
                src_ref=dst if src is None else src, dst_ref=dst,
                send_sem=send_sems.at[a * per + k], recv_sem=recv_sems.at[a * per + k],
                device_id=to, device_id_type=pl.DeviceIdType.MESH)

        mine, first, passed = [], [], []
        for a in range(n_arr):
            mine.append(pltpu.make_async_copy(srcs[a], outs[a].at[slot(me)], local_sems.at[a]))
            first.append(copy(a, 0, me, sibling, src=srcs[a]))
            first += [copy(a, 1 + j, me, (*chip, c), src=srcs[a]) for j, chip in enumerate(chips)]
        for cp in mine + first:
            cp.start()
        for j, chip in enumerate(chips):
            for a in range(n_arr):
                copy(a, 1 + j, (*chip, c), me).wait_recv()
                passed.append(copy(a, 4 + j, (*chip, c), sibling))
                passed[-1].start()
        for a in range(n_arr):
            copy(a, 0, sibling, me).wait_recv()
            for j, chip in enumerate(chips):
                copy(a, 4 + j, (*chip, 1 - c), me).wait_recv()
        for cp in first + passed:
            cp.wait_send()
        for cp in mine:
            cp.wait()

    any_spec = pl.BlockSpec(memory_space=pl.ANY)
    return pl.pallas_call(
        body, name=name,
        in_specs=[any_spec] * n_arr, out_specs=[any_spec] * n_arr,
        out_shape=[jax.ShapeDtypeStruct((N_DEV,) + tuple(a.shape), a.dtype) for a in arrays],
        scratch_shapes=[pltpu.SemaphoreType.DMA((n_arr * per,)), pltpu.SemaphoreType.DMA((n_arr * per,)),
                        pltpu.SemaphoreType.DMA((n_arr,))],
        compiler_params=pltpu.CompilerParams(has_side_effects=True),
    )(*arrays)


def _pack(parts, row_mult):
    rows, spans, r = [], [], 0
    for p in parts:
        flat = p.reshape(-1)
        nr = -(-flat.shape[0] // (PACK_W * row_mult)) * row_mult
        flat = jnp.pad(flat, (0, nr * PACK_W - flat.shape[0]))
        rows.append(flat.reshape(nr, PACK_W))
        spans.append((r, nr))
        r += nr
    return jnp.concatenate(rows, axis=0), spans


def _unpack(buf, spans, shapes):
    out = []
    for (r, nr), shp in zip(spans, shapes):
        size = math.prod(shp)
        out.append(buf[..., r:r + nr, :].reshape(buf.shape[:-2] + (nr * PACK_W,))[..., :size]
                   .reshape(buf.shape[:-2] + tuple(shp)))
    return out


def _to_shards(full, axis):
    r, c = full.shape
    if axis == 0:
        return full.reshape(N_DEV, r // N_DEV, c)
    return full.reshape(r, N_DEV, c // N_DEV).transpose(1, 0, 2)


def _from_shards(sh, axis):
    _, r, c = sh.shape
    if axis == 0:
        return sh.reshape(N_DEV * r, c)
    return sh.transpose(1, 0, 2).reshape(r, N_DEV * c)


def _rms(x, g, n):
    ms = jnp.sum(x * x, axis=-1, keepdims=True) * (1.0 / n)
    return x * lax.rsqrt(ms + EPS) * g


def _norm_mod_fn(x, g, sc, sh):
    return _rms(x, g, x.shape[-1]) * (1.0 + sc) + sh


@jax.custom_vjp
def _rope(t, ct, s1, s2):
    return t * ct + pltpu.roll(t, 16, 1) * s1 + pltpu.roll(t, HEAD_PAD - 16, 1) * s2


def _rope_fwd(t, ct, s1, s2):
    return _rope(t, ct, s1, s2), (ct, s1, s2)


def _rope_bwd(res, d):
    ct, s1, s2 = res
    dt = d * ct + pltpu.roll(d * s1, HEAD_PAD - 16, 1) + pltpu.roll(d * s2, 16, 1)
    return dt, jnp.zeros_like(ct), jnp.zeros_like(s1), jnp.zeros_like(s2)


_rope.defvjp(_rope_fwd, _rope_bwd)


def _qkv_fn(cq, ckv, krsec, qag, kvag, qng, kng, wuq, wk, wv, ct, s1, s2):
    q_raw = bdot(_rms(cq, qag, cq.shape[-1]), wuq)
    ckvn = _rms(ckv, kvag, ckv.shape[-1])
    k_raw = bdot(ckvn, wk)
    v = bdot(ckvn, wv)
    qs, ks = [], []
    for h in range(N_HEADS):
        sl = slice(h * HEAD_PAD, (h + 1) * HEAD_PAD)
        qs.append(_rope(_rms(q_raw[:, sl], qng, QK_DIM), ct, s1, s2))
        ks.append(_rope(_rms(k_raw[:, sl] + krsec, kng, QK_DIM), ct, s1, s2))
    return jnp.concatenate(qs, axis=1), jnp.concatenate(ks, axis=1), v


def _glu_fn(u, yf, yb, dskip, wglu):
    y = u * dskip + yf + yb
    vg = bdot(jax.nn.gelu(y), wglu)
    d = vg.shape[-1] // 2
    return vg[:, :d] * jax.nn.sigmoid(vg[:, d:])


def _merge_fn(o, s_l, gl, x, g1, wo, wout):
    d = x.shape[-1]
    a = bdot(o, wo)
    mix = jax.nn.sigmoid(gl[:, :d]) * a + jax.nn.sigmoid(gl[:, d:]) * s_l
    return x + g1 * bdot(mix, wout)


def _mod_fn(cmat, w):
    return bdot(jax.nn.silu(cmat), w)


def _ssm_prep_fn(lam_re, lam_im, logdt, bre, bim):
    dt = jnp.exp(logdt)
    ar, ai = lam_re * dt, lam_im * dt
    e = jnp.exp(ar)
    lbr, lbi = e * jnp.cos(ai), e * jnp.sin(ai)
    nr, ni = lbr - 1.0, lbi
    den = lam_re * lam_re + lam_im * lam_im
    qr = (nr * lam_re + ni * lam_im) / den
    qi = (ni * lam_re - nr * lam_im) / den
    return lbr, lbi, qr * bre - qi * bim, qr * bim + qi * bre


def _rows(tm, w, col=0):
    return pl.BlockSpec((tm, w), lambda i: (i, col))


def _full(shape):
    nd = len(shape)
    return pl.BlockSpec(tuple(shape), lambda i: (0,) * nd)


def _acc_add(i, ref, val):
    @pl.when(i == 0)
    def _():
        ref[...] = jnp.zeros_like(ref)
    ref[...] += val


def _pad_rows(v, rows=SUBLANES):
    sel = lax.broadcasted_iota(jnp.int32, (rows, v.shape[-1]), 0) == 0
    return jnp.where(sel, jnp.broadcast_to(v, (rows, v.shape[-1])), 0.0)


def norm_mod_fwd(xa, g, mods, n_lat, tm, name):
    r, d = xa.shape
    lat_tiles = n_lat // tm

    def body(x_ref, g_ref, m_ref, o_ref):
        lat = pl.program_id(0) < lat_tiles
        sc = jnp.where(lat, m_ref[0:1, :], m_ref[2:3, :])
        sh = jnp.where(lat, m_ref[1:2, :], m_ref[3:4, :])
        o_ref[...] = _norm_mod_fn(x_ref[...], g_ref[...], sc, sh).astype(o_ref.dtype)

    return pl.pallas_call(
        body, name=name, grid=(r // tm,),
        in_specs=[_rows(tm, d), _full(g.shape), _full(mods.shape)],
        out_specs=_rows(tm, d), out_shape=jax.ShapeDtypeStruct((r, d), BF16),
        compiler_params=_cparams(("parallel",)),
    )(xa, g, mods)


def norm_mod_bwd(xa, dh, dres, g, mods, n_lat, tm, name):
    r, d = xa.shape
    lat_tiles = n_lat // tm

    def body(x_ref, dh_ref, dres_ref, g_ref, m_ref, dx_ref, dg_ref, dm_ref):
        i = pl.program_id(0)
        lat = i < lat_tiles
        sc = jnp.where(lat, m_ref[0:1, :], m_ref[2:3, :])
        sh = jnp.where(lat, m_ref[1:2, :], m_ref[3:4, :])
        _, vjp = jax.vjp(_norm_mod_fn, x_ref[...], g_ref[...], sc, sh)
        dx, dg, dsc, dsh = vjp(dh_ref[...])
        dx_ref[...] = dx + jnp.where(lat, dres_ref[...], 0.0)
        _acc_add(i, dg_ref, _pad_rows(dg))
        row = lax.broadcasted_iota(jnp.int32, (SUBLANES, d), 0)
        base = jnp.where(lat, 0, 2)
        upd = jnp.where(row == base, jnp.broadcast_to(dsc, (SUBLANES, d)), 0.0)
        upd = upd + jnp.where(row == base + 1, jnp.broadcast_to(dsh, (SUBLANES, d)), 0.0)
        _acc_add(i, dm_ref, upd)

    return pl.pallas_call(
        body, name=name, grid=(r // tm,),
        in_specs=[_rows(tm, d), _rows(tm, d),
                  pl.BlockSpec((tm, d), lambda i: (jnp.minimum(i, lat_tiles - 1), 0)),
                  _full(g.shape), _full(mods.shape)],
        out_specs=[_rows(tm, d), _full((SUBLANES, d)), _full((SUBLANES, d))],
        out_shape=[jax.ShapeDtypeStruct((r, d), F32), jax.ShapeDtypeStruct((SUBLANES, d), F32),
                   jax.ShapeDtypeStruct((SUBLANES, d), F32)],
        compiler_params=_cparams(("arbitrary",)),
    )(xa, dh, dres, g, mods)


def qkv_fwd(proj, lay, gains, wuq, wk, wv, tabs, tm, name):
    r = proj.shape[0]
    q_w, kv_w = lay['q'], lay['kv']

    def body(cq_ref, ckv_ref, kr_ref, qag, kvag, qng, kng, wuq_ref, wk_ref, wv_ref, ct, s1, s2, q_ref, k_ref,
             v_ref):
        q, k, v = _qkv_fn(cq_ref[...], ckv_ref[...], kr_ref[...], qag[...], kvag[...], qng[...], kng[...],
                          wuq_ref[...], wk_ref[...], wv_ref[...], ct[...], s1[...], s2[...])
        q_ref[...] = q.astype(BF16)
        k_ref[...] = k.astype(BF16)
        v_ref[...] = v.astype(BF16)

    out = jax.ShapeDtypeStruct((r, HEADS_W), BF16)
    return pl.pallas_call(
        body, name=name, grid=(r // tm,),
        in_specs=[_rows(tm, q_w, lay['o_cq'] // q_w), _rows(tm, kv_w, lay['o_ckv'] // kv_w),
                  _rows(tm, LANES, lay['o_kr'] // LANES)]
        + [_full(a.shape) for a in gains] + [_full(wuq.shape), _full(wk.shape), _full(wv.shape)]
        + [_rows(tm, HEAD_PAD)] * 3,
        out_specs=[_rows(tm, HEADS_W)] * 3, out_shape=[out, out, out],
        compiler_params=_cparams(("parallel",)),
    )(proj, proj, proj, *gains, wuq, wk, wv, *tabs)


def qkv_bwd(proj, lay, gains, wuq, wk, wv, tabs, dq, dk, dv, dgl, du_direct, du_f, du_b, n, tm, name):
    r = proj.shape[0]
    q_w, kv_w, sw, d = lay['q'], lay['kv'], lay['sw'], lay['d']
    lat_tiles = n // tm

    def body(cq_ref, ckv_ref, kr_ref, qag, kvag, qng, kng, wuq_ref, wk_ref, wv_ref, ct, s1, s2, dq_ref, dk_ref,
             dv_ref, dgl_ref, dud_ref, duf_ref, dub_ref, dp_ref, dqag, dkvag, dqng, dkng, dwuq, dwk, dwv):
        i = pl.program_id(0)
        lat = i < lat_tiles
        tables = (ct[...], s1[...], s2[...])
        fn = lambda *a: _qkv_fn(*a, *tables)
        _, vjp = jax.vjp(fn, cq_ref[...], ckv_ref[...], kr_ref[...], qag[...], kvag[...], qng[...], kng[...],
                         wuq_ref[...].astype(F32), wk_ref[...].astype(F32), wv_ref[...].astype(F32))
        g = vjp((jnp.where(lat, dq_ref[...], 0.0), dk_ref[...], dv_ref[...]))
        dp_ref[:, 0:2 * d] = jnp.where(lat, dgl_ref[...], 0.0).astype(BF16)
        dp_ref[:, lay['o_u']:lay['o_u'] + sw] = (duf_ref[...] + dub_ref[...]
                                                 + jnp.where(lat, dud_ref[...], 0.0)).astype(BF16)
        dp_ref[:, lay['o_ckv']:lay['o_ckv'] + kv_w] = g[1].astype(BF16)
        dp_ref[:, lay['o_kr']:lay['o_kr'] + LANES] = g[2].astype(BF16)
        hole0 = lay['o_kr'] + LANES
        if lay['o_cq'] > hole0:
            dp_ref[:, hole0:lay['o_cq']] = jnp.zeros((tm, lay['o_cq'] - hole0), BF16)
        dp_ref[:, lay['o_cq']:lay['o_cq'] + q_w] = g[0].astype(BF16)
        for ref, val in zip((dqag, dkvag, dqng, dkng), g[3:7]):
            _acc_add(i, ref, _pad_rows(val))
        for ref, val in zip((dwuq, dwk, dwv), g[7:10]):
            _acc_add(i, ref, val)

    def lat_rows(w):
        return pl.BlockSpec((tm, w), lambda i: (jnp.minimum(i, lat_tiles - 1), 0))

    acc_shapes = [(SUBLANES, a.shape[1]) for a in gains] + [wuq.shape, wk.shape, wv.shape]
    return pl.pallas_call(
        body, name=name, grid=(r // tm,),
        in_specs=[_rows(tm, q_w, lay['o_cq'] // q_w), _rows(tm, kv_w, lay['o_ckv'] // kv_w),
                  _rows(tm, LANES, lay['o_kr'] // LANES)]
        + [_full(a.shape) for a in gains] + [_full(wuq.shape), _full(wk.shape), _full(wv.shape)]
        + [_rows(tm, HEAD_PAD)] * 3 + [lat_rows(HEADS_W), _rows(tm, HEADS_W), _rows(tm, HEADS_W)]
        + [lat_rows(2 * d), lat_rows(sw), _rows(tm, sw), _rows(tm, sw)],
        out_specs=[_rows(tm, lay['width'])] + [_full(s) for s in acc_shapes],
        out_shape=[jax.ShapeDtypeStruct((r, lay['width']), BF16)] + [jax.ShapeDtypeStruct(s, F32) for s in acc_shapes],
        compiler_params=_cparams(("arbitrary",)),
    )(proj, proj, proj, *gains, wuq, wk, wv, *tabs, dq, dk, dv, dgl, du_direct, du_f, du_b)


def glu_fwd(proj, lay, yf, yb, dskip, wglu, n, tm, name):
    sw, d = wglu.shape[0], wglu.shape[1] // 2

    def body(u_ref, yf_ref, yb_ref, ds_ref, w_ref, o_ref):
        o_ref[...] = _glu_fn(u_ref[...], yf_ref[...], yb_ref[...], ds_ref[...], w_ref[...])

    return pl.pallas_call(
        body, name=name, grid=(n // tm,),
        in_specs=[_rows(tm, sw, lay['o_u'] // sw), _rows(tm, sw), _rows(tm, sw), _full(dskip.shape),
                  _full(wglu.shape)],
        out_specs=_rows(tm, d), out_shape=jax.ShapeDtypeStruct((n, d), F32),
        compiler_params=_cparams(("parallel",)),
    )(proj, yf, yb, dskip, wglu)


def glu_bwd(proj, lay, yf, yb, dskip, wglu, ds_l, n, tm, name):
    sw, d = wglu.shape[0], wglu.shape[1] // 2

    def body(u_ref, yf_ref, yb_ref, ds_ref, w_ref, g_ref, du_ref, dy_ref, dds_ref, dw_ref):
        i = pl.program_id(0)
        _, vjp = jax.vjp(_glu_fn, u_ref[...], yf_ref[...], yb_ref[...], ds_ref[...], w_ref[...].astype(F32))
        du, dyf, _, dds, dw = vjp(g_ref[...])
        du_ref[...] = du
        dy_ref[...] = dyf
        _acc_add(i, dds_ref, _pad_rows(dds))
        _acc_add(i, dw_ref, dw)

    return pl.pallas_call(
        body, name=name, grid=(n // tm,),
        in_specs=[_rows(tm, sw, lay['o_u'] // sw), _rows(tm, sw), _rows(tm, sw), _full(dskip.shape),
                  _full(wglu.shape), _rows(tm, d)],
        out_specs=[_rows(tm, sw), _rows(tm, sw), _full((SUBLANES, sw)), _full(wglu.shape)],
        out_shape=[jax.ShapeDtypeStruct((n, sw), F32), jax.ShapeDtypeStruct((n, sw), F32),
                   jax.ShapeDtypeStruct((SUBLANES, sw), F32), jax.ShapeDtypeStruct(wglu.shape, F32)],
        compiler_params=_cparams(("arbitrary",)),
    )(proj, yf, yb, dskip, wglu, ds_l)


def merge_fwd(o, s_l, proj, xa, g1, wo, wout, n, tm, name):
    d = xa.shape[1]

    def body(o_ref, s_ref, gl_ref, x_ref, g1_ref, wo_ref, wout_ref, x1_ref):
        x1_ref[...] = _merge_fn(o_ref[...], s_ref[...], gl_ref[...], x_ref[...], g1_ref[...], wo_ref[...],
                                wout_ref[...])

    return pl.pallas_call(
        body, name=name, grid=(n // tm,),
        in_specs=[_rows(tm, HEADS_W), _rows(tm, d), _rows(tm, 2 * d), _rows(tm, d), _full(g1.shape),
                  _full(wo.shape), _full(wout.shape)],
        out_specs=_rows(tm, d), out_shape=jax.ShapeDtypeStruct((n, d), F32),
        compiler_params=_cparams(("parallel",)),
    )(o, s_l, proj, xa, g1, wo, wout)


def merge_bwd(o, s_l, proj, xa, g1, wo, wout, dx1, n, tm, name):
    d = xa.shape[1]

    def body(o_ref, s_ref, gl_ref, x_ref, g1_ref, wo_ref, wout_ref, dx1_ref, do_ref, ds_ref, dgl_ref, dg1_ref,
             dwo_ref, dwout_ref):
        i = pl.program_id(0)
        _, vjp = jax.vjp(_merge_fn, o_ref[...], s_ref[...], gl_ref[...], x_ref[...], g1_ref[...],
                         wo_ref[...].astype(F32), wout_ref[...].astype(F32))
        do, ds, dgl, _, dg1, dwo, dwout = vjp(dx1_ref[...])
        do_ref[...] = do
        ds_ref[...] = ds
        dgl_ref[...] = dgl
        _acc_add(i, dg1_ref, _pad_rows(dg1))
        _acc_add(i, dwo_ref, dwo)
        _acc_add(i, dwout_ref, dwout)

    return pl.pallas_call(
        body, name=name, grid=(n // tm,),
        in_specs=[_rows(tm, HEADS_W), _rows(tm, d), _rows(tm, 2 * d), _rows(tm, d), _full(g1.shape),
                  _full(wo.shape), _full(wout.shape), _rows(tm, d)],
        out_specs=[_rows(tm, HEADS_W), _rows(tm, d), _rows(tm, 2 * d), _full((SUBLANES, d)), _full(wo.shape),
                   _full(wout.shape)],
        out_shape=[jax.ShapeDtypeStruct((n, HEADS_W), BF16), jax.ShapeDtypeStruct((n, d), F32),
                   jax.ShapeDtypeStruct((n, 2 * d), F32), jax.ShapeDtypeStruct((SUBLANES, d), F32),
                   jax.ShapeDtypeStruct(wo.shape, F32), jax.ShapeDtypeStruct(wout.shape, F32)],
        compiler_params=_cparams(("arbitrary",)),
    )(o, s_l, proj, xa, g1, wo, wout, dx1)


def _halo_specs(tm, w, n):
    nb = n // SUBLANES
    per = tm // SUBLANES
    prev = pl.BlockSpec((SUBLANES, w), lambda i: (jnp.maximum(i * per - 1, 0), 0))
    nxt = pl.BlockSpec((SUBLANES, w), lambda i: (jnp.minimum((i + 1) * per, nb - 1), 0))
    return prev, nxt


def _shifted(t, prev_blk, next_blk, i, n_tiles):
    tm = t.shape[0]
    row = lax.broadcasted_iota(jnp.int32, t.shape, 0)
    prev_row = jnp.where(i > 0, prev_blk[SUBLANES - 1:SUBLANES, :], 0.0)
    next_row = jnp.where(i < n_tiles - 1, next_blk[0:1, :], 0.0)
    before = jnp.where(row == 0, prev_row, pltpu.roll(t, 1, 0))
    after = jnp.where(row == tm - 1, next_row, pltpu.roll(t, tm - 1, 0))
    return before, after


def _conv_u2(up, before, after, cw, cb):
    return before * cw[0:1, :] + up * cw[1:2, :] + after * cw[2:3, :] + cb


def conv_act_fwd(up, cw, cb, tm, name):
    n, w2 = up.shape
    f = w2 // 2
    n_tiles = n // tm
    prev, nxt = _halo_specs(tm, w2, n)

    def body(up_ref, prev_ref, next_ref, cw_ref, cb_ref, act_ref):
        i = pl.program_id(0)
        t = up_ref[...]
        before, after = _shifted(t, prev_ref[...], next_ref[...], i, n_tiles)
        u2 = _conv_u2(t, before, after, cw_ref[...], cb_ref[...])
        act_ref[...] = (jax.nn.silu(u2[:, f:]) * u2[:, :f]).astype(BF16)

    return pl.pallas_call(
        body, name=name, grid=(n_tiles,),
        in_specs=[_rows(tm, w2), prev, nxt, _full(cw.shape), _full(cb.shape)],
        out_specs=_rows(tm, f), out_shape=jax.ShapeDtypeStruct((n, f), BF16),
        compiler_params=_cparams(("parallel",)),
    )(up, up, up, cw, cb)


def down_loss(act, wdown, x1, g2, target, tm, name):
    n, d = x1.shape
    f = act.shape[1]

    def body(act_ref, w_ref, x1_ref, g2_ref, t_ref, dy_ref, ffn_ref, loss_ref):
        ffn = jnp.dot(act_ref[...], w_ref[...], preferred_element_type=F32)
        err = x1_ref[...] + g2_ref[...] * ffn - t_ref[...]
        ffn_ref[...] = ffn
        dy_ref[...] = err * (1.0 / d)
        part = 0.5 * jnp.sum(jnp.sum(err * err, axis=-1, keepdims=True) * (1.0 / d), axis=0, keepdims=True)
        loss_ref[0] = jnp.broadcast_to(part, (SUBLANES, LANES))

    return pl.pallas_call(
        body, name=name, grid=(n // tm,),
        in_specs=[_rows(tm, f), _full(wdown.shape), _rows(tm, d), _full(g2.shape), _rows(tm, d)],
        out_specs=[_rows(tm, d), _rows(tm, d), pl.BlockSpec((1, SUBLANES, LANES), lambda i: (i, 0, 0))],
        out_shape=[jax.ShapeDtypeStruct((n, d), F32), jax.ShapeDtypeStruct((n, d), F32),
                   jax.ShapeDtypeStruct((n // tm, SUBLANES, LANES), F32)],
        compiler_params=_cparams(("parallel",)),
    )(act, wdown, x1, g2, target)


FFN_BWD_PARTS = 2


def ffn_bwd(dy, ffn, up, cw, cb, wdown, g2, tm, name):
    n, d = dy.shape
    w2 = up.shape[1]
    f = w2 // 2
    fc = f // FFN_BWD_PARTS
    assert fc % LANES == 0
    n_tiles = n // tm
    ext = tm + 2 * SUBLANES
    inner = slice(SUBLANES, SUBLANES + tm)
    prev_w, next_w = _halo_specs(tm, w2, n)
    prev_d, next_d = _halo_specs(tm, d, n)

    def body(dy_ref, dyp_ref, dyn_ref, ffn_ref, up_ref, upp_ref, upn_ref, cw_ref, cb_ref, w_ref, g2_ref,
             dup_ref, dffn_ref, dg2_ref, dcw_ref):
        i = pl.program_id(0)
        has_prev, has_next = i > 0, i < n_tiles - 1

        def extended(prev, tile, nxt):
            return jnp.concatenate([jnp.where(has_prev, prev, 0.0), tile, jnp.where(has_next, nxt, 0.0)], axis=0)

        dyv = dy_ref[...]
        dffn = extended(dyp_ref[...], dyv, dyn_ref[...]) * g2_ref[...]
        dffn_ref[...] = dffn[inner].astype(BF16)
        dffn = dffn.astype(BF16)
        _acc_add(i, dg2_ref, _pad_rows(jnp.sum(dyv * ffn_ref[...], axis=0, keepdims=True)))
        row = lax.broadcasted_iota(jnp.int32, (SUBLANES, fc), 0)
        shift = lambda t: (pltpu.roll(t, 1, 0), pltpu.roll(t, ext - 1, 0))
        for part in range(FFN_BWD_PARTS):
            halves = []
            for col0 in (part * fc, f + part * fc):
                cols = slice(col0, col0 + fc)
                t = extended(upp_ref[:, cols], up_ref[:, cols], upn_ref[:, cols])
                before, after = shift(t)
                halves.append((cols, t, _conv_u2(t, before, after, cw_ref[:, cols], cb_ref[:, cols])))
            (_, _, val), (_, _, gate) = halves
            dact = _dot_nt(dffn, w_ref[part * fc:(part + 1) * fc, :])
            sg = jax.nn.sigmoid(gate)
            for (cols, t, _), du2 in zip(halves, (dact * (gate * sg),
                                                   dact * val * (sg * (1.0 + gate * (1.0 - sg))))):
                before, after = shift(du2)
                cwv = cw_ref[:, cols]
                dup_ref[:, cols] = (after * cwv[0:1, :] + du2 * cwv[1:2, :] + before * cwv[2:3, :])[inner].astype(BF16)
                upd = jnp.zeros((SUBLANES, fc), F32)
                for k, term in enumerate((after * t, du2 * t, before * t, du2)):
                    upd = upd + jnp.where(row == k, jnp.broadcast_to(
                        jnp.sum(term[inner], axis=0, keepdims=True), (SUBLANES, fc)), 0.0)

                @pl.when(i == 0)
                def _():
                    dcw_ref[:, cols] = jnp.zeros((SUBLANES, fc), F32)

                dcw_ref[:, cols] += upd

    return pl.pallas_call(
        body, name=name, grid=(n_tiles,),
        in_specs=[_rows(tm, d), prev_d, next_d, _rows(tm, d), _rows(tm, w2), prev_w, next_w, _full(cw.shape),
                  _full(cb.shape), _full(wdown.shape), _full(g2.shape)],
        out_specs=[_rows(tm, w2), _rows(tm, d), _full((SUBLANES, d)), _full((SUBLANES, w2))],
        out_shape=[jax.ShapeDtypeStruct((n, w2), BF16), jax.ShapeDtypeStruct((n, d), BF16),
                   jax.ShapeDtypeStruct((SUBLANES, d), F32), jax.ShapeDtypeStruct((SUBLANES, w2), F32)],
        compiler_params=_cparams(("arbitrary",)),
    )(dy, dy, dy, ffn, up, up, up, cw, cb, wdown, g2)


def attn_fwd(q, k, v, n, name):
    nk = k.shape[0]
    tq = _tile(n, (512, 256, 128))
    tk = _tile(nk, (768, 384, 256, 128))
    n_kv = nk // tk
    scale = QK_DIM ** -0.5
    c2 = scale * math.log2(math.e)

    def body(q_ref, k_ref, v_ref, o_ref, lse_ref):
        qv = q_ref[...]
        ones_col = (lax.broadcasted_iota(jnp.int32, (tk, HEAD_PAD), 1) == V_DIM).astype(BF16)

        def chunk(j, carry):
            m, acc = carry
            rows = pl.ds(pl.multiple_of(j * tk, tk), tk)
            s = _dot_nt(qv, k_ref[rows, :])
            m_new = jnp.maximum(m, jnp.max(s, axis=-1, keepdims=True))
            p = jnp.exp2(s * c2 - m_new * c2)
            alpha = jnp.exp2((m - m_new) * c2)
            pv = jnp.dot(p.astype(BF16), v_ref[rows, :] + ones_col, preferred_element_type=F32)
            return m_new, alpha * acc + pv

        m, acc = lax.fori_loop(0, n_kv, chunk, (jnp.full((tq, 1), -jnp.inf, F32),
                                                jnp.zeros((tq, HEAD_PAD), F32)), unroll=True)
        l = acc[:, V_DIM:V_DIM + 1]
        lane = lax.broadcasted_iota(jnp.int32, (tq, HEAD_PAD), 1)
        o_ref[...] = jnp.where(lane < V_DIM, acc / l, 0.0).astype(BF16)
        lse_ref[...] = jnp.broadcast_to(m * scale + jnp.log(l), (tq, HEAD_PAD))

    qspec = pl.BlockSpec((tq, HEAD_PAD), lambda h, i: (i, h))
    kspec = pl.BlockSpec((nk, HEAD_PAD), lambda h, i: (0, h))
    return pl.pallas_call(
        body, name=name, grid=(N_HEADS, n // tq),
        in_specs=[qspec, kspec, kspec], out_specs=[qspec, qspec],
        out_shape=[jax.ShapeDtypeStruct((n, HEADS_W), BF16), jax.ShapeDtypeStruct((n, HEADS_W), F32)],
        compiler_params=_cparams(("parallel", "parallel")),
    )(q, k, v)


def attn_bwd(q, k, v, o, do, lse, n, name):
    nk = k.shape[0]
    tq = _tile(n, (512, 256, 128))
    tk = _tile(nk, (768, 384, 256, 128))
    scale = QK_DIM ** -0.5
    log2e = math.log2(math.e)
    c2 = scale * log2e

    def body(q_ref, k_ref, v_ref, o_ref, do_ref, lse_ref, dq_ref, dk_ref, dv_ref):
        @pl.when(pl.program_id(1) == 0)
        def _():
            dq_ref[...] = jnp.zeros_like(dq_ref)

        kv, vv = k_ref[...], v_ref[...]

        def q_tile(i, carry):
            dk, dv = carry
            rows = pl.ds(pl.multiple_of(i * tq, tq), tq)
            qv, dov = q_ref[rows, :], do_ref[rows, :]
            p = jnp.exp2(_dot_nt(qv, kv) * c2 - lse_ref[rows, 0:1] * log2e)
            dv = dv + _dot_tn(p.astype(BF16), dov)
            dp = _dot_nt(dov, vv)
            delta = jnp.sum(dov.astype(F32) * o_ref[rows, :].astype(F32), axis=-1, keepdims=True)
            ds = (p * (dp - delta) * scale).astype(BF16)
            dk = dk + _dot_tn(ds, qv)
            dq_ref[rows, :] += jnp.dot(ds, kv, preferred_element_type=F32)
            return dk, dv

        zero = jnp.zeros((tk, HEAD_PAD), F32)
        dk, dv = lax.fori_loop(0, n // tq, q_tile, (zero, zero), unroll=2)
        dk_ref[...] = dk
        dv_ref[...] = dv

    qspec = pl.BlockSpec((n, HEAD_PAD), lambda h, j: (0, h))
    kspec = pl.BlockSpec((tk, HEAD_PAD), lambda h, j: (j, h))
    return pl.pallas_call(
        body, name=name, grid=(N_HEADS, nk // tk),
        in_specs=[qspec, kspec, kspec, qspec, qspec, qspec],
        out_specs=[qspec, kspec, kspec],
        out_shape=[jax.ShapeDtypeStruct((n, HEADS_W), F32), jax.ShapeDtypeStruct((nk, HEADS_W), F32),
                   jax.ShapeDtypeStruct((nk, HEADS_W), F32)],
        compiler_params=_cparams(("parallel", "arbitrary")),
    )(q, k, v, o, do, lse)


SCAN_LEVELS = (1, 2, 4)
SCAN_LANES = 512


def _scan_chunk(xr, xi, car, m_ref, p_ref, reverse):
    t_len, gn = xr.shape
    n_slab = t_len // SUBLANES
    for lb in range(gn // SCAN_LANES):
        ls = pl.ds(lb * SCAN_LANES, SCAN_LANES)

        def step(s, carry, ls=ls):
            cr, ci = carry
            slab = (n_slab - 1 - s) if reverse else s
            rows = pl.ds(pl.multiple_of(slab * SUBLANES, SUBLANES), SUBLANES)
            br, bi = xr[rows, ls], xi[rows, ls]
            for lvl, d in enumerate(SCAN_LEVELS):
                shift = SUBLANES - d if reverse else d
                sr, si = pltpu.roll(br, shift, 0), pltpu.roll(bi, shift, 0)
                mr, mi = m_ref[lvl, 0, :, ls], m_ref[lvl, 1, :, ls]
                br, bi = br + mr * sr - mi * si, bi + mr * si + mi * sr
            pr, pi = p_ref[0, :, ls], p_ref[1, :, ls]
            br, bi = br + pr * cr - pi * ci, bi + pr * ci + pi * cr
            xr[rows, ls] = br
            xi[rows, ls] = bi
            last = 0 if reverse else SUBLANES - 1
            return br[last:last + 1, :], bi[last:last + 1, :]

        cr, ci = lax.fori_loop(0, n_slab, step, (car[0:1, ls], car[1:2, ls]))
        car[0:1, ls] = cr
        car[1:2, ls] = ci


SSM_SPLIT = 2


def _bd_dot(a, w_ref):
    k, n = w_ref.shape[0] // SSM_SPLIT, w_ref.shape[1] // SSM_SPLIT
    return jnp.concatenate([jnp.dot(a[:, p * k:(p + 1) * k], w_ref[p * k:(p + 1) * k, p * n:(p + 1) * n],
                                    preferred_element_type=F32) for p in range(SSM_SPLIT)], axis=1)


def _bd_dot_nt(a, w_ref):
    k, n = w_ref.shape[0] // SSM_SPLIT, w_ref.shape[1] // SSM_SPLIT
    return jnp.concatenate([_dot_nt(a[:, p * n:(p + 1) * n], w_ref[p * k:(p + 1) * k, p * n:(p + 1) * n])
                            for p in range(SSM_SPLIT)], axis=1)


def _seq_block(step, n_chunk, lat_chunks, reverse):
    if reverse:
        return n_chunk - 1 - step
    return (step + lat_chunks) % n_chunk


def ssm_fwd(proj, lay, bre, bim, cre, ncim, tabs, n, t_len, reverse, name):
    l, sw = proj.shape[0], lay['sw']
    gn = bre.shape[-1]
    n_chunk, lat_chunks = l // t_len, n // t_len
    blk = lambda k: _seq_block(k, n_chunk, lat_chunks, reverse)
    mtab, ptab = tabs

    def body(u_ref, bre_ref, bim_ref, cre_ref, ncim_ref, m_ref, p_ref, y_ref, xb_ref, xr, xi, car):
        @pl.when(pl.program_id(0) == 0)
        def _():
            car[...] = jnp.zeros_like(car)

        xb_ref[...] = car[...]
        u = u_ref[...].astype(BF16)
        xr[...] = _bd_dot(u, bre_ref)
        xi[...] = _bd_dot(u, bim_ref)
        _scan_chunk(xr, xi, car, m_ref, p_ref, reverse)
        y_ref[...] = _bd_dot(xr[...].astype(BF16), cre_ref) + _bd_dot(xi[...].astype(BF16), ncim_ref)

    return pl.pallas_call(
        body, name=name, grid=(n_chunk,),
        in_specs=[pl.BlockSpec((t_len, sw), lambda k: (blk(k), lay['o_u'] // sw)), _full(bre.shape),
                  _full(bim.shape), _full(cre.shape), _full(ncim.shape), _full(mtab.shape), _full(ptab.shape)],
        out_specs=[pl.BlockSpec((t_len, sw), lambda k: (blk(k), 0)),
                   pl.BlockSpec((None, 2, gn), lambda k: (k, 0, 0))],
        out_shape=[jax.ShapeDtypeStruct((l, sw), F32), jax.ShapeDtypeStruct((n_chunk, 2, gn), F32)],
        scratch_shapes=[pltpu.VMEM((t_len, gn), F32), pltpu.VMEM((t_len, gn), F32), pltpu.VMEM((2, gn), F32)],
        compiler_params=_cparams(("arbitrary",)),
    )(proj, bre, bim, cre, ncim, mtab, ptab)


def ssm_bwd(proj, lay, dyr, xb, bre, bim, cre, ncim, tabs, adj_tabs, n, t_len, reverse, name):
    l, sw = proj.shape[0], lay['sw']
    gn = bre.shape[-1]
    n_chunk, lat_chunks = l // t_len, n // t_len
    fwd_step = lambda k: n_chunk - 1 - k
    blk = lambda k: _seq_block(fwd_step(k), n_chunk, lat_chunks, reverse)
    (mtab, ptab), (mtab_r, ptab_r) = tabs, adj_tabs

    def body(u_ref, dy_ref, xb_ref, bre_ref, bim_ref, cre_ref, ncim_ref, m_ref, p_ref, mr_ref, pr_ref,
             du_ref, gr_ref, gi_ref, xr_ref, xi_ref, dlam_ref, xr, xi, gr, gi, car, acar):
        k = pl.program_id(0)

        @pl.when(k == 0)
        def _():
            acar[...] = jnp.zeros_like(acar)
            dlam_ref[...] = jnp.zeros_like(dlam_ref)

        u = u_ref[...].astype(BF16)
        dy = jnp.where(blk(k) < lat_chunks, dy_ref[...], 0.0).astype(BF16)
        xr[...] = _bd_dot(u, bre_ref)
        xi[...] = _bd_dot(u, bim_ref)
        car[...] = xb_ref[...]
        _scan_chunk(xr, xi, car, m_ref, p_ref, reverse)
        gr[...] = _bd_dot_nt(dy, cre_ref)
        gi[...] = _bd_dot_nt(dy, ncim_ref)
        _scan_chunk(gr, gi, acar, mr_ref, pr_ref, not reverse)
        xrv, xiv, grv, giv = xr[...], xi[...], gr[...], gi[...]
        row = lax.broadcasted_iota(jnp.int32, (t_len, gn), 0)
        first, shift = (t_len - 1, t_len - 1) if reverse else (0, 1)
        xpr = jnp.where(row == first, xb_ref[0:1, :], pltpu.roll(xrv, shift, 0))
        xpi = jnp.where(row == first, xb_ref[1:2, :], pltpu.roll(xiv, shift, 0))
        dlr = grv * xpr + giv * xpi
        dli = giv * xpr - grv * xpi
        dlam_ref[0] += jnp.sum(dlr.reshape(t_len // SUBLANES, SUBLANES, gn), axis=0)
        dlam_ref[1] += jnp.sum(dli.reshape(t_len // SUBLANES, SUBLANES, gn), axis=0)
        grb, gib = grv.astype(BF16), giv.astype(BF16)
        du_ref[...] = _bd_dot_nt(grb, bre_ref) + _bd_dot_nt(gib, bim_ref)
        gr_ref[...] = grb
        gi_ref[...] = gib
        xr_ref[...] = xrv.astype(BF16)
        xi_ref[...] = xiv.astype(BF16)

    def at_blk(width, col=0):
        return pl.BlockSpec((t_len, width), lambda k: (blk(k), col))

    state = jax.ShapeDtypeStruct((l, gn), BF16)
    return pl.pallas_call(
        body, name=name, grid=(n_chunk,),
        in_specs=[at_blk(sw, lay['o_u'] // sw),
                  pl.BlockSpec((t_len, sw), lambda k: (jnp.minimum(blk(k), lat_chunks - 1), 0)),
                  pl.BlockSpec((None, 2, gn), lambda k: (fwd_step(k), 0, 0)),
                  _full(bre.shape), _full(bim.shape), _full(cre.shape), _full(ncim.shape), _full(mtab.shape),
                  _full(ptab.shape), _full(mtab_r.shape), _full(ptab_r.shape)],
        out_specs=[at_blk(sw), at_blk(gn), at_blk(gn), at_blk(gn), at_blk(gn), _full((2, SUBLANES, gn))],
        out_shape=[jax.ShapeDtypeStruct((l, sw), F32), state, state, state, state,
                   jax.ShapeDtypeStruct((2, SUBLANES, gn), F32)],
        scratch_shapes=[pltpu.VMEM((t_len, gn), F32)] * 4 + [pltpu.VMEM((2, gn), F32)] * 2,
        compiler_params=_cparams(("arbitrary",)),
    )(proj, dyr, xb, bre, bim, cre, ncim, mtab, ptab, mtab_r, ptab_r)


def ssm_prep(lam_re, lam_im, logdt, bre, bim, name):
    gn = lam_re.shape[1]

    def body(lr_ref, li_ref, dt_ref, br_ref, bi_ref, pwr_ref, pwi_ref, bbr_ref, bbi_ref):
        _, _, bbr, bbi = _ssm_prep_fn(lr_ref[...], li_ref[...], dt_ref[...], br_ref[...], bi_ref[...])
        bbr_ref[...] = bbr
        bbi_ref[...] = bbi
        kk = (lax.broadcasted_iota(jnp.int32, (SUBLANES, gn), 0) + 1).astype(F32)
        dt = jnp.exp(dt_ref[...])
        ar, ai = lr_ref[...] * dt * kk, li_ref[...] * dt * kk
        e = jnp.exp(ar)
        pwr_ref[...] = e * jnp.cos(ai)
        pwi_ref[...] = e * jnp.sin(ai)

    ins = (lam_re, lam_im, logdt, bre, bim)
    return pl.pallas_call(
        body, name=name,
        out_shape=[jax.ShapeDtypeStruct((SUBLANES, gn), F32)] * 2 + [jax.ShapeDtypeStruct(bre.shape, F32)] * 2,
        compiler_params=_cparams(),
    )(*ins)


def ssm_prep_bwd(lam_re, lam_im, logdt, bre, bim, dlbr, dlbi, dbbr, dbbi, name):
    def body(lr_ref, li_ref, dt_ref, br_ref, bi_ref, g0, g1, g2, g3, o0, o1, o2, o3, o4):
        _, vjp = jax.vjp(_ssm_prep_fn, lr_ref[...], li_ref[...], dt_ref[...], br_ref[...], bi_ref[...])
        for ref, val in zip((o0, o1, o2, o3, o4), vjp((g0[...], g1[...], g2[...], g3[...]))):
            ref[...] = val

    ins = (lam_re, lam_im, logdt, bre, bim)
    return pl.pallas_call(
        body, name=name,
        out_shape=[jax.ShapeDtypeStruct(a.shape, F32) for a in ins],
        compiler_params=_cparams(),
    )(*ins, dlbr, dlbi, dbbr, dbbi)


def mod_fwd(cmat, w, b, name):
    def body(c_ref, w_ref, b_ref, o_ref):
        o_ref[...] = _mod_fn(c_ref[...], w_ref[...]) + b_ref[...]

    return pl.pallas_call(body, name=name, out_shape=jax.ShapeDtypeStruct((cmat.shape[0], w.shape[1]), F32),
                          compiler_params=_cparams())(cmat, w, b)


def mod_bwd(cmat, w, g_lat, g_ctx, g_all, name):
    def body(c_ref, w_ref, gl_ref, gc_ref, ga_ref, dw_ref, dc_ref, db_ref):
        gc = jnp.sum(gc_ref[...], axis=0, keepdims=True)
        dm = jnp.concatenate([gl_ref[...], _pad_rows(gc)], axis=0)
        _, vjp = jax.vjp(_mod_fn, c_ref[...], w_ref[...])
        dc, dw = vjp(dm)
        dw_ref[...] = dw
        dc_ref[...] = dc
        db_ref[...] = _pad_rows(jnp.sum(ga_ref[...], axis=0, keepdims=True))

    return pl.pallas_call(
        body, name=name,
        out_shape=[jax.ShapeDtypeStruct(w.shape, F32), jax.ShapeDtypeStruct(cmat.shape, F32),
                   jax.ShapeDtypeStruct((SUBLANES, g_all.shape[1]), F32)],
        compiler_params=_cparams(),
    )(cmat, w, g_lat, g_ctx, g_all)


def add_own(own, idx, recv, out_dtype, name):
    _, s, r, c = own.shape

    def body(idx_ref, own_ref, recv_ref, o_ref):
        o_ref[...] = (own_ref[...] + recv_ref[...]).astype(o_ref.dtype)

    return pl.pallas_call(
        body, name=name,
        grid_spec=pltpu.PrefetchScalarGridSpec(
            num_scalar_prefetch=1, grid=(s,),
            in_specs=[pl.BlockSpec((None, None, r, c), lambda k, idx_ref: (idx_ref[0], k, 0, 0)),
                      pl.BlockSpec((None, r, c), lambda k, idx_ref: (k, 0, 0))],
            out_specs=pl.BlockSpec((None, r, c), lambda k, idx_ref: (k, 0, 0))),
        out_shape=jax.ShapeDtypeStruct((s, r, c), out_dtype),
        compiler_params=_cparams(("parallel",)),
    )(idx, own, recv)


def reduce_adamw(parts, w, m, v, name, own=None, idx=None):
    s, r, c = parts.shape
    tr = _tile(r, (256, 128, 64, 32, 16, 8))

    def body(*refs):
        if own is None:
            p_ref, w_ref, m_ref, v_ref, g_ref, d_ref, nm_ref, nv_ref = refs
            g = p_ref[0].astype(F32)
        else:
            _, own_ref, p_ref, w_ref, m_ref, v_ref, g_ref, d_ref, nm_ref, nv_ref = refs
            g = own_ref[...].astype(F32) + p_ref[0].astype(F32)
        for k in range(1, s):
            g = g + p_ref[k].astype(F32)
        mm = ADAM_B1 * m_ref[...] + (1.0 - ADAM_B1) * g
        vv = ADAM_B2 * v_ref[...] + (1.0 - ADAM_B2) * jnp.square(g)
        m_hat = mm / (1.0 - ADAM_B1 ** ADAM_STEP)
        v_hat = vv / (1.0 - ADAM_B2 ** ADAM_STEP)
        g_ref[...] = g
        d_ref[...] = -ADAM_LR * (m_hat / (jnp.sqrt(v_hat) + ADAM_EPS) + ADAM_WD * w_ref[...])
        nm_ref[...] = mm
        nv_ref[...] = vv

    out = jax.ShapeDtypeStruct((r, c), F32)
    if own is None:
        blk = _rows(tr, c)
        return pl.pallas_call(
            body, name=name, grid=(r // tr,),
            in_specs=[pl.BlockSpec((s, tr, c), lambda i: (0, i, 0)), blk, blk, blk],
            out_specs=[blk] * 4, out_shape=[out] * 4,
            compiler_params=_cparams(("parallel",)),
        )(parts, w, m, v)
    blk = pl.BlockSpec((tr, c), lambda i, idx_ref: (i, 0))
    return pl.pallas_call(
        body, name=name,
        grid_spec=pltpu.PrefetchScalarGridSpec(
            num_scalar_prefetch=1, grid=(r // tr,),
            in_specs=[pl.BlockSpec((None, tr, c), lambda i, idx_ref: (idx_ref[0], i, 0)),
                      pl.BlockSpec((s, tr, c), lambda i, idx_ref: (0, i, 0)), blk, blk, blk],
            out_specs=[blk] * 4),
        out_shape=[out] * 4,
        compiler_params=_cparams(("parallel",)),
    )(idx, own, parts, w, m, v)


def _in_layout(d, q, kv, sw):
    o_u = 2 * d
    o_ckv = o_u + sw
    o_kr = o_ckv + kv
    o_cq = -(-(o_kr + LANES) // q) * q
    assert o_u % sw == 0 and o_ckv % kv == 0 and o_kr % LANES == 0
    assert q % LANES == 0 and kv % LANES == 0 and sw % LANES == 0
    return dict(d=d, q=q, kv=kv, sw=sw, o_gl=0, o_u=o_u, o_ckv=o_ckv, o_kr=o_kr, o_cq=o_cq, width=o_cq + q)


def _pad_w_in(w_in, lay):
    q, kv, sw, d = lay['q'], lay['kv'], lay['sw'], lay['d']
    cq, ckv, kr, u, gl = jnp.split(w_in, [q, q + kv, q + kv + QK_ROPE, q + kv + QK_ROPE + sw], axis=1)
    z = lambda w: jnp.zeros((w_in.shape[0], w), w_in.dtype)
    hole = lay['o_cq'] - lay['o_kr'] - LANES
    return jnp.concatenate([gl, u, ckv, z(QK_NOPE), kr, z(LANES - QK_DIM), z(hole), cq], axis=1)


def _unpad_w_in(g, lay):
    q, kv, sw, d = lay['q'], lay['kv'], lay['sw'], lay['d']
    kr0 = lay['o_kr'] + QK_NOPE
    return jnp.concatenate([g[:, lay['o_cq']:lay['o_cq'] + q], g[:, lay['o_ckv']:lay['o_ckv'] + kv],
                            g[:, kr0:kr0 + QK_ROPE], g[:, lay['o_u']:lay['o_u'] + sw], g[:, :2 * d]], axis=1)


def _pad_heads(w, width):
    k = w.shape[0]
    return jnp.pad(w.reshape(k, N_HEADS, width), ((0, 0), (0, 0), (0, HEAD_PAD - width))).reshape(k, HEADS_W)


def _unpad_heads(w, width):
    k = w.shape[0]
    return w.reshape(k, N_HEADS, HEAD_PAD)[:, :, :width].reshape(k, N_HEADS * width)


def _rope_tables(n, nc):
    rows = n // GRID_W
    row = jnp.repeat(jnp.arange(rows), GRID_W)
    col = jnp.tile(jnp.arange(GRID_W), rows)
    pairs = QK_ROPE // 4
    freqs = ROPE_THETA ** (-jnp.arange(pairs, dtype=F32) / pairs)
    ang = jnp.concatenate([row[:, None] * freqs, col[:, None] * freqs], axis=-1)
    cos = jnp.concatenate([jnp.cos(ang), jnp.ones((nc, 2 * pairs), F32)], axis=0)
    sin = jnp.concatenate([jnp.sin(ang), jnp.zeros((nc, 2 * pairs), F32)], axis=0)
    l = n + nc
    half = QK_ROPE // 2
    ct = jnp.concatenate([jnp.ones((l, QK_NOPE), F32), cos, cos, jnp.zeros((l, HEAD_PAD - QK_DIM), F32)], axis=1)
    s1 = jnp.concatenate([jnp.zeros((l, QK_NOPE + half), F32), sin, jnp.zeros((l, HEAD_PAD - QK_DIM), F32)],
                         axis=1)
    s2 = jnp.concatenate([jnp.zeros((l, QK_NOPE), F32), -sin, jnp.zeros((l, HEAD_PAD - QK_NOPE - half), F32)],
                         axis=1)
    return ct, s1, s2


def _block_diag(m):
    bt, g, a, b = m.shape
    eye = jnp.eye(g, dtype=m.dtype)
    return (m[:, :, :, None, :] * eye[:, None, :, None]).reshape(bt, g * a, g * b)


def _diag_blocks(m, g):
    bt, a, b = m.shape[0], m.shape[1] // g, m.shape[2] // g
    m5 = m.reshape(bt, g, a, g, b)
    return jnp.sum(m5 * jnp.eye(g, dtype=m.dtype)[:, None, :, None], axis=3)


def _scan_tables(pwr, pwi, reverse):
    row = jnp.arange(SUBLANES)[:, None]
    zero = jnp.zeros_like(pwr)
    levels = []
    for d in SCAN_LEVELS:
        keep = (row < SUBLANES - d) if reverse else (row >= d)
        levels.append(jnp.stack([jnp.where(keep, pwr[d - 1:d, :], zero), jnp.where(keep, pwi[d - 1:d, :], zero)]))
    carry = jnp.stack([pwr[::-1], pwi[::-1]]) if reverse else jnp.stack([pwr, pwi])
    return jnp.stack(levels), carry


def _step(inp):
    x, c, ctx = inp['x'][0], inp['c'], inp['ctx'][0]
    target = inp['loss_target'][0]
    n, d = x.shape
    nc = ctx.shape[0]
    l = n + nc
    q_w, kv_w = inp['q_a_g'].shape[1], inp['kv_a_g'].shape[1]
    sw = inp['d_skip'].shape[1]
    n_grp = sw // SSM_GROUP
    gn = n_grp * SSM_STATE
    lay = _in_layout(d, q_w, kv_w, sw)
    tm = _tile(math.gcd(n, nc), (256, 128))
    me = 4 * lax.axis_index("x") + 2 * lax.axis_index("y") + lax.axis_index("c")
    strip = lambda a: a if a.ndim <= 2 else a[0]
    w = {k: strip(inp[k]) for k in WEIGHT_NAMES}

    gathered_names = list(GATHERED)
    c_all, *wg = gather_two_level([jnp.broadcast_to(c, (SUBLANES, d))] + [w[k].astype(BF16) for k in gathered_names],
                                  "gather_weights")
    full = {k: _from_shards(s, GATHERED[k]) for k, s in zip(gathered_names, wg)}

    w_in_p = _pad_w_in(full['w_in'], lay)
    wuq_p = _pad_heads(full['w_uq'], QK_DIM)
    ukv = full['w_ukv'].reshape(kv_w, N_HEADS, QK_NOPE + V_DIM)
    wk_p = _pad_heads(ukv[:, :, :QK_NOPE].reshape(kv_w, -1), QK_NOPE)
    wv_p = _pad_heads(ukv[:, :, QK_NOPE:].reshape(kv_w, -1), V_DIM)
    wo_p = _pad_heads(full['w_o_attn'].T, V_DIM).T
    w_glu, w_out, w_up, w_down = full['w_glu'], full['w_out'], full['w_up'], full['w_down']
    conv_w = jnp.pad(full['conv_w'].astype(F32), ((0, SUBLANES - 3), (0, 0)))
    conv_b = w['conv_b']
    f2 = w_up.shape[1]

    cmat = jnp.concatenate([c_all[:, 0, :], w['c_ctx'][None, :], jnp.zeros((SUBLANES - 1, d), F32)], axis=0)
    mcols = w['w_mod'].shape[1]
    b_cols = lax.dynamic_slice(w['b_mod'], (0, me * mcols), (1, mcols))
    mod_part = mod_fwd(cmat, w['w_mod'], b_cols, "mod_fwd")
    (mod_all,) = exchange([mod_part], ['gather'], "gather_mod")
    mod_me = lax.dynamic_index_in_dim(mod_all, me, axis=1, keepdims=False).reshape(6, d)
    mod_ctx = mod_all[:, SUBLANES, :].reshape(6, d)
    sh1, sc1, g1, sh2, sc2, g2 = [mod_me[k:k + 1] for k in range(6)]
    mods1 = jnp.concatenate([sc1, sh1, mod_ctx[1:2], mod_ctx[0:1], jnp.zeros((4, d), F32)], axis=0)
    mods2 = jnp.concatenate([sc2, sh2, jnp.zeros((6, d), F32)], axis=0)

    xa = jnp.concatenate([x, ctx], axis=0)
    h = norm_mod_fwd(xa, w['norm1_g'], mods1, n, tm, "norm1_fwd")
    proj = matmul(h, w_in_p, 'nn', F32, "in_proj")
    tabs = _rope_tables(n, nc)
    pad_g = lambda g: jnp.pad(g, ((0, 0), (0, HEAD_PAD - QK_DIM)))
    gains = (w['q_a_g'], w['kv_a_g'], pad_g(w['q_norm_g']), pad_g(w['k_norm_g']))
    q, k, v = qkv_fwd(proj, lay, gains, wuq_p, wk_p, wv_p, tabs, tm, "qkv_fwd")
    o, lse = attn_fwd(q, k, v, n, "attn_fwd")

    b_t = lambda a: a.transpose(2, 0, 1).reshape(SSM_GROUP, gn)
    c_t = lambda a: a.transpose(1, 0, 2).reshape(SSM_GROUP, gn)
    bre_t, bim_t = b_t(w['b_re']), b_t(w['b_im'])
    ssm_in, prep = [], []
    for sfx in ('f', 'b'):
        lam_re, lam_im = w['lam_re_' + sfx].reshape(1, gn), w['lam_im_' + sfx].reshape(1, gn)
        logdt = jnp.repeat(w['log_dt_' + sfx], SSM_STATE, axis=1)
        ssm_in.append((lam_re, lam_im, logdt))
        prep.append(ssm_prep(lam_re, lam_im, logdt, bre_t, bim_t, "ssm_prep_" + sfx))

    bb = jnp.stack([prep[0][2], prep[0][3], prep[1][2], prep[1][3]])
    b_blk = _block_diag(bb.reshape(4, SSM_GROUP, n_grp, SSM_STATE).transpose(0, 2, 1, 3)).astype(BF16)
    cc = jnp.stack([w['c_re_f'], -w['c_im_f'], w['c_re_b'], -w['c_im_b']])
    c_blk = _block_diag(cc.transpose(0, 1, 3, 2)).astype(BF16)

    t_len = _tile(math.gcd(n, nc), (256, 128))
    ssm = []
    for di, sfx in enumerate(('f', 'b')):
        reverse = di == 1
        pwr, pwi = prep[di][0], prep[di][1]
        ssm.append(dict(
            sfx=sfx, reverse=reverse, blocks=(b_blk[2 * di], b_blk[2 * di + 1], c_blk[2 * di], c_blk[2 * di + 1]),
            tabs=_scan_tables(pwr, pwi, reverse), adj_tabs=_scan_tables(pwr, -pwi, not reverse)))
    for s in ssm:
        s['y'], s['xb'] = ssm_fwd(proj, lay, *s['blocks'], s['tabs'], n, t_len, s['reverse'], "ssm_fwd_" + s['sfx'])
    yf, yb = ssm[0]['y'], ssm[1]['y']
    s_l = glu_fwd(proj, lay, yf, yb, w['d_skip'], w_glu, n, tm, "glu_fwd")
    x1 = merge_fwd(o, s_l, proj, xa, g1, wo_p, w_out, n, tm, "merge_fwd")

    h2 = norm_mod_fwd(x1, w['norm2_g'], mods2, n, tm, "norm2_fwd")
    up = matmul(h2, w_up, 'nn', F32, "up_proj")
    tw = _tile(n, (128,))
    act = conv_act_fwd(up, conv_w, conv_b, tw, "conv_act_fwd")
    dy, ffn, loss_parts = down_loss(act, w_down, x1, g2, target, tm, "down_loss")
    loss = lax.psum(jnp.sum(loss_parts[:, 0, 0]), MESH_AXES)

    dup, dffn, dg2, dconv = ffn_bwd(dy, ffn, up, conv_w, conv_b, w_down, g2, tm, "ffn_bwd")
    g_w_down = matmul(act, dffn, 'tn', F32, "dw_down")
    dh2 = matmul(dup, w_up, 'nt', F32, "dh2")
    g_w_up = matmul(h2, dup, 'tn', F32, "dw_up")
    dx1, dn2g, dmods2 = norm_mod_bwd(x1, dh2, dy, w['norm2_g'], mods2, n, tm, "norm2_bwd")

    do, ds_l, dgl, dg1, g_wo_p, g_w_out = merge_bwd(o, s_l, proj, xa, g1, wo_p, w_out, dx1, n, tm, "merge_bwd")
    du_direct, dyr, dds, g_w_glu = glu_bwd(proj, lay, yf, yb, w['d_skip'], w_glu, ds_l, n, tm, "glu_bwd")
    for s in ssm:
        s['du'], s['gr'], s['gi'], s['xr'], s['xi'], s['dlam'] = ssm_bwd(
            proj, lay, dyr, s['xb'], *s['blocks'], s['tabs'], s['adj_tabs'], n, t_len, s['reverse'],
            "ssm_bwd_" + s['sfx'])

    dq, dk, dv = attn_bwd(q, k, v, o, do, lse, n, "attn_bwd")
    (dproj, dqag, dkvag, dqng, dkng, g_wuq_p, g_wk_p, g_wv_p) = qkv_bwd(
        proj, lay, gains, wuq_p, wk_p, wv_p, tabs, dq, dk, dv, dgl, du_direct, ssm[0]['du'], ssm[1]['du'], n, tm,
        "qkv_bwd")
    dh = matmul(dproj, w_in_p, 'nt', F32, "dh")
    g_w_in_p = matmul(h, dproj, 'tn', F32, "dw_in")
    dxa, dn1g, dmods1 = norm_mod_bwd(xa, dh, dx1, w['norm1_g'], mods1, n, tm, "norm1_bwd")
    grad_x = dxa[:n]

    grads = {}
    d_bbar = [None, None]
    ka, kb, gp = sw // SSM_SPLIT, gn // SSM_SPLIT, n_grp // SSM_SPLIT
    products = []
    for s in ssm:
        for nm, a, a_col0, b in (("ssm_db_re", proj, lay['o_u'], s['gr']), ("ssm_db_im", proj, lay['o_u'], s['gi']),
                                 ("ssm_dc_re", dyr, 0, s['xr']), ("ssm_dc_im", dyr, 0, s['xi'])):
            products += [matmul(a, b, 'tn', F32, "%s_%s%d" % (nm, s['sfx'], p), a_col0=a_col0 + p * ka, a_cols=ka,
                                b_col0=p * kb, b_cols=kb) for p in range(SSM_SPLIT)]
    blocks = _diag_blocks(jnp.stack(products), gp).reshape(2, 4, n_grp, SSM_GROUP, SSM_STATE)
    for di, s in enumerate(ssm):
        sfx = s['sfx']
        g_bre, g_bim, g_cre, g_cim = [blocks[di, k] for k in range(4)]
        grads['c_re_' + sfx] = g_cre
        grads['c_im_' + sfx] = -g_cim
        to_t = lambda a: a.transpose(1, 0, 2).reshape(SSM_GROUP, gn)
        dlam_re = jnp.sum(s['dlam'][0], axis=0, keepdims=True)
        dlam_im = jnp.sum(s['dlam'][1], axis=0, keepdims=True)
        lam_re, lam_im, logdt = ssm_in[di]
        g_lr, g_li, g_dt, g_br, g_bi = ssm_prep_bwd(lam_re, lam_im, logdt, bre_t, bim_t, dlam_re, dlam_im,
                                                    to_t(g_bre), to_t(g_bim), "ssm_prep_bwd_" + sfx)
        grads['lam_re_' + sfx] = g_lr.reshape(n_grp, SSM_STATE)
        grads['lam_im_' + sfx] = g_li.reshape(n_grp, SSM_STATE)
        grads['log_dt_' + sfx] = jnp.sum(g_dt.reshape(n_grp, SSM_STATE), axis=1)[None, :]
        d_bbar[di] = (g_br, g_bi)
    from_t = lambda a: a.reshape(SSM_GROUP, n_grp, SSM_STATE).transpose(1, 2, 0)
    grads['b_re'] = from_t(d_bbar[0][0]) + from_t(d_bbar[1][0])
    grads['b_im'] = from_t(d_bbar[0][1]) + from_t(d_bbar[1][1])

    dmod = jnp.concatenate([dmods1[1:2], dmods1[0:1], dg1[0:1], dmods2[1:2], dmods2[0:1], dg2[0:1]], axis=1)
    dmod_ctx = jnp.concatenate([dmods1[3:4], dmods1[2:3], jnp.zeros((1, 4 * d), F32)], axis=1)
    dm_send = jnp.concatenate([dmod, dmod_ctx, jnp.zeros((SUBLANES - 2, 6 * d), F32)], axis=0)
    (dm_all,) = exchange([dm_send], ['gather'], "gather_dmod")
    g_all = jnp.concatenate([dm_all[:, 0, :], dm_all[:, 1, :]], axis=0)
    cols = lax.dynamic_slice(g_all.reshape(2 * N_DEV, N_DEV, mcols), (0, me, 0), (2 * N_DEV, 1, mcols))[:, 0, :]
    g_w_mod, dcmat, g_b_mod = mod_bwd(cmat, w['w_mod'], cols[:N_DEV], cols[N_DEV:], g_all, "mod_bwd")

    ukv_g = jnp.concatenate([_unpad_heads(g_wk_p, QK_NOPE).reshape(kv_w, N_HEADS, QK_NOPE),
                             _unpad_heads(g_wv_p, V_DIM).reshape(kv_w, N_HEADS, V_DIM)], axis=2)
    full_g = {'w_in': _unpad_w_in(g_w_in_p, lay), 'w_uq': _unpad_heads(g_wuq_p, QK_DIM),
              'w_ukv': ukv_g.reshape(kv_w, -1), 'w_o_attn': _unpad_heads(g_wo_p.T, V_DIM).T, 'w_glu': g_w_glu,
              'w_out': g_w_out, 'w_up': g_w_up, 'conv_w': dconv[0:3], 'w_down': g_w_down}
    def by_core(full_grad, axis):
        sh = _to_shards(full_grad, axis)
        return sh.reshape((N_CHIPS, 2) + sh.shape[1:]).swapaxes(0, 1)

    sends = [by_core(full_g[k], GATHERED[k]) for k in gathered_names]
    grads.update({'c_ctx': dcmat[SUBLANES], 'b_mod': g_b_mod[0:1], 'norm1_g': dn1g[0:1], 'norm2_g': dn2g[0:1],
                  'q_a_g': dqag[0:1], 'kv_a_g': dkvag[0:1], 'q_norm_g': dqng[0:1, :QK_DIM],
                  'k_norm_g': dkng[0:1, :QK_DIM], 'd_skip': dds[0:1], 'conv_b': dconv[3:4]})
    first = (me == 0).astype(F32)
    rep_parts = [grads[k] * first if k == 'b_mod' else grads[k] for k in REPLICATED]
    rpack, rspans = _pack(rep_parts, SUBLANES)
    my_core = lax.axis_index("c").astype(jnp.int32).reshape(1)
    my_chip = (2 * lax.axis_index("x") + lax.axis_index("y")).astype(jnp.int32).reshape(1)
    r_both = jnp.broadcast_to(rpack[None, None], (2, 1) + rpack.shape)
    from_sibling = exchange(sends + [r_both], ['others'] * (len(sends) + 1), "grads_in_chip", group='core')
    chip_sums = [add_own(s, my_core, got[0], BF16, "chip_sum_" + k)
                 for s, got, k in zip(sends, from_sibling, gathered_names)]
    r_sum = add_own(r_both, my_core, from_sibling[-1][0], F32, "chip_sum_replicated")[0]
    *g_recv, r_recv = exchange(chip_sums + [r_sum], ['others'] * len(sends) + ['gather'], "grads_between_chips",
                               group='chips')

    outs = {}

    def update(parts, k, w_k, m_k, v_k, **own):
        res = reduce_adamw(parts, w_k, m_k, v_k, "adamw_" + k, **own)
        return dict(zip(('grad_', 'delta_', 'new_m_', 'new_v_'), res))

    for k, parts, own in zip(gathered_names + ['w_mod'], g_recv + [g_w_mod[None]], chip_sums + [None]):
        own_args = {} if own is None else dict(own=own, idx=my_chip)
        for kind, a in update(parts, k, w[k], strip(inp['m_' + k]), strip(inp['v_' + k]), **own_args).items():
            outs[kind + k] = a[None]
    rep = lambda prefix: _pack([strip(inp[prefix + k]) for k in REPLICATED], SUBLANES)[0]
    for kind, buf in update(r_recv, "replicated", rep(''), rep('m_'), rep('v_')).items():
        for k, a in zip(REPLICATED, _unpack(buf, rspans, [w[k].shape for k in REPLICATED])):
            outs[kind + k] = a if inp[k].ndim <= 2 else a[None]
    result = [loss, grad_x[None]]
    for kind in ('grad_', 'delta_', 'new_m_', 'new_v_'):
        result += [outs[kind + k] for k in WEIGHT_NAMES]
    return tuple(result)


_ARG_NAMES = (['x', 'c', 'ctx'] + WEIGHT_NAMES + ['loss_target'] + ['m_' + k for k in WEIGHT_NAMES]
              + ['v_' + k for k in WEIGHT_NAMES])


def kernel(*args):
    assert len(args) == len(_ARG_NAMES)
    return _step(dict(zip(_ARG_NAMES, args)))
```

```python
import functools
import math

import jax
import jax.numpy as jnp
from jax import lax
from jax.experimental import pallas as pl
from jax.experimental.pallas import tpu as pltpu

F32 = jnp.float32
BF16 = jnp.bfloat16

N_DEV = 8
MESH_AXES = ("x", "y", "c")
N_HEADS = 8
QK_NOPE = 64
QK_ROPE = 32
QK_DIM = QK_NOPE + QK_ROPE
V_DIM = 64
HEAD_PAD = 128
HEADS_W = N_HEADS * HEAD_PAD
GRID_W = 64
ROPE_THETA = 10000.0
SSM_GROUP = 16
SSM_STATE = 64
EPS = 1e-6
LANES = 128
SUBLANES = 8
PACK_W = 1024
VMEM_LIMIT = 56 * 1024 * 1024
MM_TILES = (1024, 768, 1408, 512, 384, 256, 128)

ADAM_LR = 0.001
ADAM_B1 = 0.9
ADAM_B2 = 0.999
ADAM_EPS = 1e-08
ADAM_WD = 0.01
ADAM_STEP = 10

WEIGHT_NAMES = ['c_ctx', 'w_mod', 'b_mod', 'norm1_g', 'norm2_g', 'w_in', 'q_a_g', 'w_uq', 'kv_a_g', 'w_ukv',
                'q_norm_g', 'k_norm_g', 'w_o_attn', 'lam_re_f', 'lam_im_f', 'log_dt_f', 'c_re_f', 'c_im_f',
                'lam_re_b', 'lam_im_b', 'log_dt_b', 'c_re_b', 'c_im_b', 'b_re', 'b_im', 'd_skip', 'w_glu',
                'w_out', 'w_up', 'conv_w', 'conv_b', 'w_down']
GATHERED = {'w_in': 1, 'w_uq': 1, 'w_ukv': 1, 'w_o_attn': 1, 'w_glu': 1, 'w_out': 0, 'w_up': 1, 'conv_w': 1,
            'w_down': 0}
REPLICATED = [n for n in WEIGHT_NAMES if n not in GATHERED and n != 'w_mod']


def _tile(n, prefs):
    for t in prefs:
        if n % t == 0:
            return t
    return n


def _cparams(sem=None):
    return pltpu.CompilerParams(dimension_semantics=sem, vmem_limit_bytes=VMEM_LIMIT)


@jax.custom_vjp
def bdot(a, w):
    return jnp.dot(a.astype(BF16), w.astype(BF16), preferred_element_type=F32)


def _bdot_fwd(a, w):
    return bdot(a, w), (a, w)


def _bdot_bwd(res, g):
    a, w = res
    gb = g.astype(BF16)
    da = lax.dot_general(gb, w.astype(BF16), (((1,), (1,)), ((), ())), preferred_element_type=F32)
    dw = lax.dot_general(a.astype(BF16), gb, (((0,), (0,)), ((), ())), preferred_element_type=F32)
    return da.astype(a.dtype), dw.astype(w.dtype)


bdot.defvjp(_bdot_fwd, _bdot_bwd)


def _dot_nt(a, b):
    return lax.dot_general(a, b, (((1,), (1,)), ((), ())), preferred_element_type=F32)


def _dot_tn(a, b):
    return lax.dot_general(a, b, (((0,), (0,)), ((), ())), preferred_element_type=F32)


def matmul(a, b, mode, out_dtype, name, a_col0=0, a_cols=None, b_col0=0, b_cols=None):
    if mode == 'nn':
        (m, k), n = a.shape, b.shape[1]
    elif mode == 'nt':
        (m, k), n = a.shape, b.shape[0]
    else:
        (k, m), n = a.shape, b.shape[1]
        m, n = a_cols or m, b_cols or n
    tm = _tile(m, MM_TILES)
    tn = _tile(n, MM_TILES)
    tk = _tile(k, MM_TILES)
    nk = k // tk
    assert a_col0 % tm == 0 and b_col0 % tn == 0
    col0, bcol0 = a_col0 // tm, b_col0 // tn

    def body(a_ref, b_ref, o_ref, acc_ref):
        kk = pl.program_id(2)

        @pl.when(kk == 0)
        def _():
            acc_ref[...] = jnp.zeros_like(acc_ref)

        av, bv = a_ref[...].astype(BF16), b_ref[...].astype(BF16)
        if mode == 'nn':
            acc_ref[...] += jnp.dot(av, bv, preferred_element_type=F32)
        elif mode == 'nt':
            acc_ref[...] += _dot_nt(av, bv)
        else:
            acc_ref[...] += _dot_tn(av, bv)

        @pl.when(kk == nk - 1)
        def _():
            o_ref[...] = acc_ref[...].astype(o_ref.dtype)

    if mode == 'nn':
        a_spec = pl.BlockSpec((tm, tk), lambda i, j, kk: (i, kk))
        b_spec = pl.BlockSpec((tk, tn), lambda i, j, kk: (kk, j))
    elif mode == 'nt':
        a_spec = pl.BlockSpec((tm, tk), lambda i, j, kk: (i, kk))
        b_spec = pl.BlockSpec((tn, tk), lambda i, j, kk: (j, kk))
    else:
        a_spec = pl.BlockSpec((tk, tm), lambda i, j, kk: (kk, i + col0))
        b_spec = pl.BlockSpec((tk, tn), lambda i, j, kk: (kk, j + bcol0))
    return pl.pallas_call(
        body, name=name, grid=(m // tm, n // tn, nk),
        in_specs=[a_spec, b_spec],
        out_specs=pl.BlockSpec((tm, tn), lambda i, j, kk: (i, j)),
        out_shape=jax.ShapeDtypeStruct((m, n), out_dtype),
        scratch_shapes=[pltpu.VMEM((tm, tn), F32)],
        compiler_params=_cparams(("parallel", "parallel", "arbitrary")),
    )(a, b)


N_CHIPS = 4


def _group(group):
    x, y, c = lax.axis_index("x"), lax.axis_index("y"), lax.axis_index("c")
    flips = {'all': [(r & 4, r & 2, r & 1) for r in range(1, 8)],
             'chips': [(0, 1, 0), (1, 0, 0), (1, 1, 0)], 'core': [(0, 0, 1)]}[group]
    index = {'all': lambda px, py, pc: 4 * px + 2 * py + pc, 'chips': lambda px, py, pc: 2 * px + py,
             'core': lambda px, py, pc: pc}[group]
    peers = []
    for fx, fy, fc in flips:
        p = (1 - x if fx else x, 1 - y if fy else y, 1 - c if fc else c)
        peers.append((p, index(*p)))
    return len(flips) + 1, index(x, y, c), peers


def exchange(arrays, kinds, name, group='all'):
    n_arr = len(arrays)
    size = {'all': N_DEV, 'chips': N_CHIPS, 'core': 2}[group]

    def body(*refs):
        srcs, dsts = refs[:n_arr], refs[n_arr:2 * n_arr]
        send_sems, recv_sems, local_sems = refs[2 * n_arr:]
        _, me, peers = _group(group)

        def copy(a, r, peer, peer_idx, receiving):
            src = srcs[a] if kinds[a] == 'gather' else srcs[a].at[peer_idx]
            if kinds[a] == 'others':
                dst = dsts[a].at[r]
            else:
                dst = dsts[a].at[peer_idx if receiving else me]
            return pltpu.make_async_remote_copy(
                src_ref=src, dst_ref=dst, send_sem=send_sems.at[a * size + r], recv_sem=recv_sems.at[a * size + r],
                device_id=peer, device_id_type=pl.DeviceIdType.MESH)

        started = []
        for a in range(n_arr):
            if kinds[a] != 'others':
                mine = srcs[a] if kinds[a] == 'gather' else srcs[a].at[me]
                started.append(pltpu.make_async_copy(mine, dsts[a].at[me], local_sems.at[a]))
                started[-1].start()
            for r, (peer, peer_idx) in enumerate(peers):
                copy(a, r, peer, peer_idx, False).start()
        for a in range(n_arr):
            for r, (peer, peer_idx) in enumerate(peers):
                cp = copy(a, r, peer, peer_idx, True)
                cp.wait_send()
                cp.wait_recv()
        for local in started:
            local.wait()

    out_shape = []
    for a, arr in enumerate(arrays):
        shp = {'gather': (size,) + tuple(arr.shape), 'a2a': tuple(arr.shape),
               'others': (size - 1,) + tuple(arr.shape[1:])}[kinds[a]]
        out_shape.append(jax.ShapeDtypeStruct(shp, arr.dtype))
    any_spec = pl.BlockSpec(memory_space=pl.ANY)
    return pl.pallas_call(
        body, name=name,
        in_specs=[any_spec] * n_arr, out_specs=[any_spec] * n_arr, out_shape=out_shape,
        scratch_shapes=[pltpu.SemaphoreType.DMA((n_arr * size,)), pltpu.SemaphoreType.DMA((n_arr * size,)),
                        pltpu.SemaphoreType.DMA((n_arr,))],
        compiler_params=pltpu.CompilerParams(has_side_effects=True),
    )(*arrays)


GATHER_COPIES = 7


def _two_level_gather(srcs, outs, send_sems, recv_sems, local_sems):
    n_arr = len(srcs)
    x, y, c = lax.axis_index("x"), lax.axis_index("y"), lax.axis_index("c")
    me, sibling = (x, y, c), (x, y, 1 - c)
    chips = [(1 - x, y), (x, 1 - y), (1 - x, 1 - y)]
    slot = lambda p: 4 * p[0] + 2 * p[1] + p[2]

    def copy(a, k, block, to, own=False):
        dst = outs[a].at[slot(block)]
        return pltpu.make_async_remote_copy(
            src_ref=srcs[a] if own else dst, dst_ref=dst,
            send_sem=send_sems.at[a * GATHER_COPIES + k], recv_sem=recv_sems.at[a * GATHER_COPIES + k],
            device_id=to, device_id_type=pl.DeviceIdType.MESH)

    def own_copies(a):
        return [copy(a, 0, me, sibling, own=True)] + [copy(a, 1 + j, me, (*chip, c), own=True)
                                                      for j, chip in enumerate(chips)]

    local = lambda a: pltpu.make_async_copy(srcs[a], outs[a].at[slot(me)], local_sems.at[a])

    def start():
        for a in range(n_arr):
            local(a).start()
            for cp in own_copies(a):
                cp.start()

    def finish():
        passed = []
        for j, chip in enumerate(chips):
            for a in range(n_arr):
                copy(a, 1 + j, (*chip, c), me).wait_recv()
                passed.append(copy(a, 4 + j, (*chip, c), sibling))
                passed[-1].start()
        for a in range(n_arr):
            copy(a, 0, sibling, me).wait_recv()
            for j, chip in enumerate(chips):
                copy(a, 4 + j, (*chip, 1 - c), me).wait_recv()
            for cp in own_copies(a):
                cp.wait_send()
        for cp in passed:
            cp.wait_send()
        for a in range(n_arr):
            local(a).wait()

    return start, finish


def _gather_scratch(n_arr):
    return [pltpu.SemaphoreType.DMA((n_arr * GATHER_COPIES,)), pltpu.SemaphoreType.DMA((n_arr * GATHER_COPIES,)),
            pltpu.SemaphoreType.DMA((n_arr,))]


def gather_two_level(arrays, name):
    n_arr = len(arrays)

    def body(*refs):
        start, finish = _two_level_gather(refs[:n_arr], refs[n_arr:2 * n_arr], *refs[2 * n_arr:])
        start()
        finish()

    any_spec = pl.BlockSpec(memory_space=pl.ANY)
    return pl.pallas_call(
        body, name=name,
        in_specs=[any_spec] * n_arr, out_specs=[any_spec] * n_arr,
        out_shape=[jax.ShapeDtypeStruct((N_DEV,) + tuple(a.shape), a.dtype) for a in arrays],
        scratch_shapes=_gather_scratch(n_arr),
        compiler_params=pltpu.CompilerParams(has_side_effects=True),
    )(*arrays)


def _pack(parts, row_mult):
    rows, spans, r = [], [], 0
    for p in parts:
        flat = p.reshape(-1)
        nr = -(-flat.shape[0] // (PACK_W * row_mult)) * row_mult
        flat = jnp.pad(flat, (0, nr * PACK_W - flat.shape[0]))
        rows.append(flat.reshape(nr, PACK_W))
        spans.append((r, nr))
        r += nr
    return jnp.concatenate(rows, axis=0), spans


def _unpack(buf, spans, shapes):
    out = []
    for (r, nr), shp in zip(spans, shapes):
        size = math.prod(shp)
        out.append(buf[..., r:r + nr, :].reshape(buf.shape[:-2] + (nr * PACK_W,))[..., :size]
                   .reshape(buf.shape[:-2] + tuple(shp)))
    return out


def _to_shards(full, axis):
    r, c = full.shape
    if axis == 0:
        return full.reshape(N_DEV, r // N_DEV, c)
    return full.reshape(r, N_DEV, c // N_DEV).transpose(1, 0, 2)


def _from_shards(sh, axis):
    _, r, c = sh.shape
    if axis == 0:
        return sh.reshape(N_DEV * r, c)
    return sh.transpose(1, 0, 2).reshape(r, N_DEV * c)


def _rms(x, g, n):
    ms = jnp.sum(x * x, axis=-1, keepdims=True) * (1.0 / n)
    return x * lax.rsqrt(ms + EPS) * g


def _norm_mod_fn(x, g, sc, sh):
    return _rms(x, g, x.shape[-1]) * (1.0 + sc) + sh


@jax.custom_vjp
def _rope(t, ct, s1, s2):
    return t * ct + pltpu.roll(t, 16, 1) * s1 + pltpu.roll(t, HEAD_PAD - 16, 1) * s2


def _rope_fwd(t, ct, s1, s2):
    return _rope(t, ct, s1, s2), (ct, s1, s2)


def _rope_bwd(res, d):
    ct, s1, s2 = res
    dt = d * ct + pltpu.roll(d * s1, HEAD_PAD - 16, 1) + pltpu.roll(d * s2, 16, 1)
    return dt, jnp.zeros_like(ct), jnp.zeros_like(s1), jnp.zeros_like(s2)


_rope.defvjp(_rope_fwd, _rope_bwd)


def _qkv_fn(cq, ckv, krsec, qag, kvag, qng, kng, wuq, wk, wv, ct, s1, s2):
    q_raw = bdot(_rms(cq, qag, cq.shape[-1]), wuq)
    ckvn = _rms(ckv, kvag, ckv.shape[-1])
    k_raw = bdot(ckvn, wk)
    v = bdot(ckvn, wv)
    qs, ks = [], []
    for h in range(N_HEADS):
        sl = slice(h * HEAD_PAD, (h + 1) * HEAD_PAD)
        qs.append(_rope(_rms(q_raw[:, sl], qng, QK_DIM), ct, s1, s2))
        ks.append(_rope(_rms(k_raw[:, sl] + krsec, kng, QK_DIM), ct, s1, s2))
    return jnp.concatenate(qs, axis=1), jnp.concatenate(ks, axis=1), v


def _glu_fn(u, yf, yb, dskip, wglu):
    y = u * dskip + yf + yb
    vg = bdot(jax.nn.gelu(y), wglu)
    d = vg.shape[-1] // 2
    return vg[:, :d] * jax.nn.sigmoid(vg[:, d:])


def _merge_fn(o, s_l, gl, x, g1, wo, wout):
    d = x.shape[-1]
    a = bdot(o, wo)
    mix = jax.nn.sigmoid(gl[:, :d]) * a + jax.nn.sigmoid(gl[:, d:]) * s_l
    return x + g1 * bdot(mix, wout)


def _mod_fn(cmat, w):
    return bdot(jax.nn.silu(cmat), w)


def _ssm_prep_fn(lam_re, lam_im, logdt, bre, bim):
    dt = jnp.exp(logdt)
    ar, ai = lam_re * dt, lam_im * dt
    e = jnp.exp(ar)
    lbr, lbi = e * jnp.cos(ai), e * jnp.sin(ai)
    nr, ni = lbr - 1.0, lbi
    den = lam_re * lam_re + lam_im * lam_im
    qr = (nr * lam_re + ni * lam_im) / den
    qi = (ni * lam_re - nr * lam_im) / den
    return lbr, lbi, qr * bre - qi * bim, qr * bim + qi * bre


def _rows(tm, w, col=0):
    return pl.BlockSpec((tm, w), lambda i: (i, col))


def _full(shape):
    nd = len(shape)
    return pl.BlockSpec(tuple(shape), lambda i: (0,) * nd)


def _acc_add(i, ref, val):
    @pl.when(i == 0)
    def _():
        ref[...] = jnp.zeros_like(ref)
    ref[...] += val


def _pad_rows(v, rows=SUBLANES):
    sel = lax.broadcasted_iota(jnp.int32, (rows, v.shape[-1]), 0) == 0
    return jnp.where(sel, jnp.broadcast_to(v, (rows, v.shape[-1])), 0.0)


def norm_mod_fwd(xa, g, mods, n_lat, tm, name):
    r, d = xa.shape
    lat_tiles = n_lat // tm

    def body(x_ref, g_ref, m_ref, o_ref):
        lat = pl.program_id(0) < lat_tiles
        sc = jnp.where(lat, m_ref[0:1, :], m_ref[2:3, :])
        sh = jnp.where(lat, m_ref[1:2, :], m_ref[3:4, :])
        o_ref[...] = _norm_mod_fn(x_ref[...], g_ref[...], sc, sh).astype(o_ref.dtype)

    return pl.pallas_call(
        body, name=name, grid=(r // tm,),
        in_specs=[_rows(tm, d), _full(g.shape), _full(mods.shape)],
        out_specs=_rows(tm, d), out_shape=jax.ShapeDtypeStruct((r, d), BF16),
        compiler_params=_cparams(("parallel",)),
    )(xa, g, mods)


def norm_mod_bwd(xa, dh, dres, g, mods, n_lat, tm, name):
    r, d = xa.shape
    lat_tiles = n_lat // tm

    def body(x_ref, dh_ref, dres_ref, g_ref, m_ref, dx_ref, dg_ref, dm_ref):
        i = pl.program_id(0)
        lat = i < lat_tiles
        sc = jnp.where(lat, m_ref[0:1, :], m_ref[2:3, :])
        sh = jnp.where(lat, m_ref[1:2, :], m_ref[3:4, :])
        _, vjp = jax.vjp(_norm_mod_fn, x_ref[...], g_ref[...], sc, sh)
        dx, dg, dsc, dsh = vjp(dh_ref[...])
        dx_ref[...] = dx + jnp.where(lat, dres_ref[...], 0.0)
        _acc_add(i, dg_ref, _pad_rows(dg))
        row = lax.broadcasted_iota(jnp.int32, (SUBLANES, d), 0)
        base = jnp.where(lat, 0, 2)
        upd = jnp.where(row == base, jnp.broadcast_to(dsc, (SUBLANES, d)), 0.0)
        upd = upd + jnp.where(row == base + 1, jnp.broadcast_to(dsh, (SUBLANES, d)), 0.0)
        _acc_add(i, dm_ref, upd)

    return pl.pallas_call(
        body, name=name, grid=(r // tm,),
        in_specs=[_rows(tm, d), _rows(tm, d),
                  pl.BlockSpec((tm, d), lambda i: (jnp.minimum(i, lat_tiles - 1), 0)),
                  _full(g.shape), _full(mods.shape)],
        out_specs=[_rows(tm, d), _full((SUBLANES, d)), _full((SUBLANES, d))],
        out_shape=[jax.ShapeDtypeStruct((r, d), F32), jax.ShapeDtypeStruct((SUBLANES, d), F32),
                   jax.ShapeDtypeStruct((SUBLANES, d), F32)],
        compiler_params=_cparams(("arbitrary",)),
    )(xa, dh, dres, g, mods)


def qkv_fwd(proj, lay, gains, wuq, wk, wv, tabs, tm, name):
    r = proj.shape[0]
    q_w, kv_w = lay['q'], lay['kv']

    def body(cq_ref, ckv_ref, kr_ref, qag, kvag, qng, kng, wuq_ref, wk_ref, wv_ref, ct, s1, s2, q_ref, k_ref,
             v_ref):
        q, k, v = _qkv_fn(cq_ref[...], ckv_ref[...], kr_ref[...], qag[...], kvag[...], qng[...], kng[...],
                          wuq_ref[...], wk_ref[...], wv_ref[...], ct[...], s1[...], s2[...])
        q_ref[...] = q.astype(BF16)
        k_ref[...] = k.astype(BF16)
        v_ref[...] = v.astype(BF16)

    out = jax.ShapeDtypeStruct((r, HEADS_W), BF16)
    return pl.pallas_call(
        body, name=name, grid=(r // tm,),
        in_specs=[_rows(tm, q_w, lay['o_cq'] // q_w), _rows(tm, kv_w, lay['o_ckv'] // kv_w),
                  _rows(tm, LANES, lay['o_kr'] // LANES)]
        + [_full(a.shape) for a in gains] + [_full(wuq.shape), _full(wk.shape), _full(wv.shape)]
        + [_rows(tm, HEAD_PAD)] * 3,
        out_specs=[_rows(tm, HEADS_W)] * 3, out_shape=[out, out, out],
        compiler_params=_cparams(("parallel",)),
    )(proj, proj, proj, *gains, wuq, wk, wv, *tabs)


def qkv_bwd(proj, lay, gains, wuq, wk, wv, tabs, dq, dk, dv, dgl, du_direct, du_f, du_b, n, tm, name):
    r = proj.shape[0]
    q_w, kv_w, sw, d = lay['q'], lay['kv'], lay['sw'], lay['d']
    lat_tiles = n // tm

    def body(cq_ref, ckv_ref, kr_ref, qag, kvag, qng, kng, wuq_ref, wk_ref, wv_ref, ct, s1, s2, dq_ref, dk_ref,
             dv_ref, dgl_ref, dud_ref, duf_ref, dub_ref, dp_ref, dqag, dkvag, dqng, dkng, dwuq, dwk, dwv):
        i = pl.program_id(0)
        lat = i < lat_tiles
        tables = (ct[...], s1[...], s2[...])
        fn = lambda *a: _qkv_fn(*a, *tables)
        _, vjp = jax.vjp(fn, cq_ref[...], ckv_ref[...], kr_ref[...], qag[...], kvag[...], qng[...], kng[...],
                         wuq_ref[...].astype(F32), wk_ref[...].astype(F32), wv_ref[...].astype(F32))
        g = vjp((jnp.where(lat, dq_ref[...], 0.0), dk_ref[...], dv_ref[...]))
        dp_ref[:, 0:2 * d] = jnp.where(lat, dgl_ref[...], 0.0).astype(BF16)
        dp_ref[:, lay['o_u']:lay['o_u'] + sw] = (duf_ref[...] + dub_ref[...]
                                                 + jnp.where(lat, dud_ref[...], 0.0)).astype(BF16)
        dp_ref[:, lay['o_ckv']:lay['o_ckv'] + kv_w] = g[1].astype(BF16)
        dp_ref[:, lay['o_kr']:lay['o_kr'] + LANES] = g[2].astype(BF16)
        hole0 = lay['o_kr'] + LANES
        if lay['o_cq'] > hole0:
            dp_ref[:, hole0:lay['o_cq']] = jnp.zeros((tm, lay['o_cq'] - hole0), BF16)
        dp_ref[:, lay['o_cq']:lay['o_cq'] + q_w] = g[0].astype(BF16)
        for ref, val in zip((dqag, dkvag, dqng, dkng), g[3:7]):
            _acc_add(i, ref, _pad_rows(val))
        for ref, val in zip((dwuq, dwk, dwv), g[7:10]):
            _acc_add(i, ref, val)

    def lat_rows(w):
        return pl.BlockSpec((tm, w), lambda i: (jnp.minimum(i, lat_tiles - 1), 0))

    acc_shapes = [(SUBLANES, a.shape[1]) for a in gains] + [wuq.shape, wk.shape, wv.shape]
    return pl.pallas_call(
        body, name=name, grid=(r // tm,),
        in_specs=[_rows(tm, q_w, lay['o_cq'] // q_w), _rows(tm, kv_w, lay['o_ckv'] // kv_w),
                  _rows(tm, LANES, lay['o_kr'] // LANES)]
        + [_full(a.shape) for a in gains] + [_full(wuq.shape), _full(wk.shape), _full(wv.shape)]
        + [_rows(tm, HEAD_PAD)] * 3 + [lat_rows(HEADS_W), _rows(tm, HEADS_W), _rows(tm, HEADS_W)]
        + [lat_rows(2 * d), lat_rows(sw), _rows(tm, sw), _rows(tm, sw)],
        out_specs=[_rows(tm, lay['width'])] + [_full(s) for s in acc_shapes],
        out_shape=[jax.ShapeDtypeStruct((r, lay['width']), BF16)] + [jax.ShapeDtypeStruct(s, F32) for s in acc_shapes],
        compiler_params=_cparams(("arbitrary",)),
    )(proj, proj, proj, *gains, wuq, wk, wv, *tabs, dq, dk, dv, dgl, du_direct, du_f, du_b)


def glu_fwd(proj, lay, yf, yb, dskip, wglu, n, tm, name):
    sw, d = wglu.shape[0], wglu.shape[1] // 2

    def body(u_ref, yf_ref, yb_ref, ds_ref, w_ref, o_ref):
        o_ref[...] = _glu_fn(u_ref[...], yf_ref[...], yb_ref[...], ds_ref[...], w_ref[...])

    return pl.pallas_call(
        body, name=name, grid=(n // tm,),
        in_specs=[_rows(tm, sw, lay['o_u'] // sw), _rows(tm, sw), _rows(tm, sw), _full(dskip.shape),
                  _full(wglu.shape)],
        out_specs=_rows(tm, d), out_shape=jax.ShapeDtypeStruct((n, d), F32),
        compiler_params=_cparams(("parallel",)),
    )(proj, yf, yb, dskip, wglu)


def glu_bwd(proj, lay, yf, yb, dskip, wglu, ds_l, n, tm, name):
    sw, d = wglu.shape[0], wglu.shape[1] // 2

    def body(u_ref, yf_ref, yb_ref, ds_ref, w_ref, g_ref, du_ref, dy_ref, dds_ref, dw_ref):
        i = pl.program_id(0)
        _, vjp = jax.vjp(_glu_fn, u_ref[...], yf_ref[...], yb_ref[...], ds_ref[...], w_ref[...].astype(F32))
        du, dyf, _, dds, dw = vjp(g_ref[...])
        du_ref[...] = du
        dy_ref[...] = dyf
        _acc_add(i, dds_ref, _pad_rows(dds))
        _acc_add(i, dw_ref, dw)

    return pl.pallas_call(
        body, name=name, grid=(n // tm,),
        in_specs=[_rows(tm, sw, lay['o_u'] // sw), _rows(tm, sw), _rows(tm, sw), _full(dskip.shape),
                  _full(wglu.shape), _rows(tm, d)],
        out_specs=[_rows(tm, sw), _rows(tm, sw), _full((SUBLANES, sw)), _full(wglu.shape)],
        out_shape=[jax.ShapeDtypeStruct((n, sw), F32), jax.ShapeDtypeStruct((n, sw), F32),
                   jax.ShapeDtypeStruct((SUBLANES, sw), F32), jax.ShapeDtypeStruct(wglu.shape, F32)],
        compiler_params=_cparams(("arbitrary",)),
    )(proj, yf, yb, dskip, wglu, ds_l)


def merge_fwd(o, s_l, proj, xa, g1, wo, wout, n, tm, name):
    d = xa.shape[1]

    def body(o_ref, s_ref, gl_ref, x_ref, g1_ref, wo_ref, wout_ref, x1_ref):
        x1_ref[...] = _merge_fn(o_ref[...], s_ref[...], gl_ref[...], x_ref[...], g1_ref[...], wo_ref[...],
                                wout_ref[...])

    return pl.pallas_call(
        body, name=name, grid=(n // tm,),
        in_specs=[_rows(tm, HEADS_W), _rows(tm, d), _rows(tm, 2 * d), _rows(tm, d), _full(g1.shape),
                  _full(wo.shape), _full(wout.shape)],
        out_specs=_rows(tm, d), out_shape=jax.ShapeDtypeStruct((n, d), F32),
        compiler_params=_cparams(("parallel",)),
    )(o, s_l, proj, xa, g1, wo, wout)


def merge_bwd(o, s_l, proj, xa, g1, wo, wout, dx1, n, tm, name):
    d = xa.shape[1]

    def body(o_ref, s_ref, gl_ref, x_ref, g1_ref, wo_ref, wout_ref, dx1_ref, do_ref, ds_ref, dgl_ref, dg1_ref,
             dwo_ref, dwout_ref):
        i = pl.program_id(0)
        _, vjp = jax.vjp(_merge_fn, o_ref[...], s_ref[...], gl_ref[...], x_ref[...], g1_ref[...],
                         wo_ref[...].astype(F32), wout_ref[...].astype(F32))
        do, ds, dgl, _, dg1, dwo, dwout = vjp(dx1_ref[...])
        do_ref[...] = do
        ds_ref[...] = ds
        dgl_ref[...] = dgl
        _acc_add(i, dg1_ref, _pad_rows(dg1))
        _acc_add(i, dwo_ref, dwo)
        _acc_add(i, dwout_ref, dwout)

    return pl.pallas_call(
        body, name=name, grid=(n // tm,),
        in_specs=[_rows(tm, HEADS_W), _rows(tm, d), _rows(tm, 2 * d), _rows(tm, d), _full(g1.shape),
                  _full(wo.shape), _full(wout.shape), _rows(tm, d)],
        out_specs=[_rows(tm, HEADS_W), _rows(tm, d), _rows(tm, 2 * d), _full((SUBLANES, d)), _full(wo.shape),
                   _full(wout.shape)],
        out_shape=[jax.ShapeDtypeStruct((n, HEADS_W), BF16), jax.ShapeDtypeStruct((n, d), F32),
                   jax.ShapeDtypeStruct((n, 2 * d), F32), jax.ShapeDtypeStruct((SUBLANES, d), F32),
                   jax.ShapeDtypeStruct(wo.shape, F32), jax.ShapeDtypeStruct(wout.shape, F32)],
        compiler_params=_cparams(("arbitrary",)),
    )(o, s_l, proj, xa, g1, wo, wout, dx1)


def _halo_specs(tm, w, n):
    nb = n // SUBLANES
    per = tm // SUBLANES
    prev = pl.BlockSpec((SUBLANES, w), lambda i: (jnp.maximum(i * per - 1, 0), 0))
    nxt = pl.BlockSpec((SUBLANES, w), lambda i: (jnp.minimum((i + 1) * per, nb - 1), 0))
    return prev, nxt


def _shifted(t, prev_blk, next_blk, i, n_tiles):
    tm = t.shape[0]
    row = lax.broadcasted_iota(jnp.int32, t.shape, 0)
    prev_row = jnp.where(i > 0, prev_blk[SUBLANES - 1:SUBLANES, :], 0.0)
    next_row = jnp.where(i < n_tiles - 1, next_blk[0:1, :], 0.0)
    before = jnp.where(row == 0, prev_row, pltpu.roll(t, 1, 0))
    after = jnp.where(row == tm - 1, next_row, pltpu.roll(t, tm - 1, 0))
    return before, after


def _conv_u2(up, before, after, cw, cb):
    return before * cw[0:1, :] + up * cw[1:2, :] + after * cw[2:3, :] + cb


def conv_act_fwd(up, cw, cb, tm, name):
    n, w2 = up.shape
    f = w2 // 2
    n_tiles = n // tm
    prev, nxt = _halo_specs(tm, w2, n)

    def body(up_ref, prev_ref, next_ref, cw_ref, cb_ref, act_ref):
        i = pl.program_id(0)
        t = up_ref[...]
        before, after = _shifted(t, prev_ref[...], next_ref[...], i, n_tiles)
        u2 = _conv_u2(t, before, after, cw_ref[...], cb_ref[...])
        act_ref[...] = (jax.nn.silu(u2[:, f:]) * u2[:, :f]).astype(BF16)

    return pl.pallas_call(
        body, name=name, grid=(n_tiles,),
        in_specs=[_rows(tm, w2), prev, nxt, _full(cw.shape), _full(cb.shape)],
        out_specs=_rows(tm, f), out_shape=jax.ShapeDtypeStruct((n, f), BF16),
        compiler_params=_cparams(("parallel",)),
    )(up, up, up, cw, cb)


def down_loss(act, wdown, x1, g2, target, tm, name):
    n, d = x1.shape
    f = act.shape[1]

    def body(act_ref, w_ref, x1_ref, g2_ref, t_ref, dy_ref, ffn_ref, loss_ref):
        ffn = jnp.dot(act_ref[...], w_ref[...], preferred_element_type=F32)
        err = x1_ref[...] + g2_ref[...] * ffn - t_ref[...]
        ffn_ref[...] = ffn
        dy_ref[...] = err * (1.0 / d)
        part = 0.5 * jnp.sum(jnp.sum(err * err, axis=-1, keepdims=True) * (1.0 / d), axis=0, keepdims=True)
        loss_ref[0] = jnp.broadcast_to(part, (SUBLANES, LANES))

    return pl.pallas_call(
        body, name=name, grid=(n // tm,),
        in_specs=[_rows(tm, f), _full(wdown.shape), _rows(tm, d), _full(g2.shape), _rows(tm, d)],
        out_specs=[_rows(tm, d), _rows(tm, d), pl.BlockSpec((1, SUBLANES, LANES), lambda i: (i, 0, 0))],
        out_shape=[jax.ShapeDtypeStruct((n, d), F32), jax.ShapeDtypeStruct((n, d), F32),
                   jax.ShapeDtypeStruct((n // tm, SUBLANES, LANES), F32)],
        compiler_params=_cparams(("parallel",)),
    )(act, wdown, x1, g2, target)


FFN_BWD_PARTS = 2


def ffn_bwd(dy, ffn, up, cw, cb, wdown, g2, tm, name):
    n, d = dy.shape
    w2 = up.shape[1]
    f = w2 // 2
    fc = f // FFN_BWD_PARTS
    assert fc % LANES == 0
    n_tiles = n // tm
    ext = tm + 2 * SUBLANES
    inner = slice(SUBLANES, SUBLANES + tm)
    prev_w, next_w = _halo_specs(tm, w2, n)
    prev_d, next_d = _halo_specs(tm, d, n)

    def body(dy_ref, dyp_ref, dyn_ref, ffn_ref, up_ref, upp_ref, upn_ref, cw_ref, cb_ref, w_ref, g2_ref,
             dup_ref, dffn_ref, dg2_ref, dcw_ref):
        i = pl.program_id(0)
        has_prev, has_next = i > 0, i < n_tiles - 1

        def extended(prev, tile, nxt):
            return jnp.concatenate([jnp.where(has_prev, prev, 0.0), tile, jnp.where(has_next, nxt, 0.0)], axis=0)

        dyv = dy_ref[...]
        dffn = extended(dyp_ref[...], dyv, dyn_ref[...]) * g2_ref[...]
        dffn_ref[...] = dffn[inner].astype(BF16)
        dffn = dffn.astype(BF16)
        _acc_add(i, dg2_ref, _pad_rows(jnp.sum(dyv * ffn_ref[...], axis=0, keepdims=True)))
        row = lax.broadcasted_iota(jnp.int32, (SUBLANES, fc), 0)
        shift = lambda t: (pltpu.roll(t, 1, 0), pltpu.roll(t, ext - 1, 0))
        for part in range(FFN_BWD_PARTS):
            halves = []
            for col0 in (part * fc, f + part * fc):
                cols = slice(col0, col0 + fc)
                t = extended(upp_ref[:, cols], up_ref[:, cols], upn_ref[:, cols])
                before, after = shift(t)
                halves.append((cols, t, _conv_u2(t, before, after, cw_ref[:, cols], cb_ref[:, cols])))
            (_, _, val), (_, _, gate) = halves
            dact = _dot_nt(dffn, w_ref[part * fc:(part + 1) * fc, :])
            sg = jax.nn.sigmoid(gate)
            for (cols, t, _), du2 in zip(halves, (dact * (gate * sg),
                                                   dact * val * (sg * (1.0 + gate * (1.0 - sg))))):
                before, after = shift(du2)
                cwv = cw_ref[:, cols]
                dup_ref[:, cols] = (after * cwv[0:1, :] + du2 * cwv[1:2, :] + before * cwv[2:3, :])[inner].astype(BF16)
                upd = jnp.zeros((SUBLANES, fc), F32)
                for k, term in enumerate((after * t, du2 * t, before * t, du2)):
                    upd = upd + jnp.where(row == k, jnp.broadcast_to(
                        jnp.sum(term[inner], axis=0, keepdims=True), (SUBLANES, fc)), 0.0)

                @pl.when(i == 0)
                def _():
                    dcw_ref[:, cols] = jnp.zeros((SUBLANES, fc), F32)

                dcw_ref[:, cols] += upd

    return pl.pallas_call(
        body, name=name, grid=(n_tiles,),
        in_specs=[_rows(tm, d), prev_d, next_d, _rows(tm, d), _rows(tm, w2), prev_w, next_w, _full(cw.shape),
                  _full(cb.shape), _full(wdown.shape), _full(g2.shape)],
        out_specs=[_rows(tm, w2), _rows(tm, d), _full((SUBLANES, d)), _full((SUBLANES, w2))],
        out_shape=[jax.ShapeDtypeStruct((n, w2), BF16), jax.ShapeDtypeStruct((n, d), BF16),
                   jax.ShapeDtypeStruct((SUBLANES, d), F32), jax.ShapeDtypeStruct((SUBLANES, w2), F32)],
        compiler_params=_cparams(("arbitrary",)),
    )(dy, dy, dy, ffn, up, up, up, cw, cb, wdown, g2)


def attn_fwd(q, k, v, n, name, gather=()):
    nk = k.shape[0]
    tq = _tile(n, (512, 256, 128))
    tk = _tile(nk, (768, 384, 256, 128))
    n_kv = nk // tk
    n_q = n // tq
    n_g = len(gather)
    scale = QK_DIM ** -0.5
    c2 = scale * math.log2(math.e)

    def body(q_ref, k_ref, v_ref, *rest):
        o_ref, lse_ref = rest[n_g:n_g + 2]
        if n_g:
            start, finish = _two_level_gather(rest[:n_g], rest[n_g + 2:2 * n_g + 2], *rest[2 * n_g + 2:])
            step = pl.program_id(0) * n_q + pl.program_id(1)
            pl.when(step == 0)(start)
        qv = q_ref[...]
        ones_col = (lax.broadcasted_iota(jnp.int32, (tk, HEAD_PAD), 1) == V_DIM).astype(BF16)

        def chunk(j, carry):
            m, acc = carry
            rows = pl.ds(pl.multiple_of(j * tk, tk), tk)
            s = _dot_nt(qv, k_ref[rows, :])
            m_new = jnp.maximum(m, jnp.max(s, axis=-1, keepdims=True))
            p = jnp.exp2(s * c2 - m_new * c2)
            alpha = jnp.exp2((m - m_new) * c2)
            pv = jnp.dot(p.astype(BF16), v_ref[rows, :] + ones_col, preferred_element_type=F32)
            return m_new, alpha * acc + pv

        m, acc = lax.fori_loop(0, n_kv, chunk, (jnp.full((tq, 1), -jnp.inf, F32),
                                                jnp.zeros((tq, HEAD_PAD), F32)), unroll=True)
        l = acc[:, V_DIM:V_DIM + 1]
        lane = lax.broadcasted_iota(jnp.int32, (tq, HEAD_PAD), 1)
        o_ref[...] = jnp.where(lane < V_DIM, acc / l, 0.0).astype(BF16)
        lse_ref[...] = jnp.broadcast_to(m * scale + jnp.log(l), (tq, HEAD_PAD))
        if n_g:
            pl.when(step == N_HEADS * n_q - 1)(finish)

    qspec = pl.BlockSpec((tq, HEAD_PAD), lambda h, i: (i, h))
    kspec = pl.BlockSpec((nk, HEAD_PAD), lambda h, i: (0, h))
    any_spec = pl.BlockSpec(memory_space=pl.ANY)
    return pl.pallas_call(
        body, name=name, grid=(N_HEADS, n_q),
        in_specs=[qspec, kspec, kspec] + [any_spec] * n_g, out_specs=[qspec, qspec] + [any_spec] * n_g,
        out_shape=[jax.ShapeDtypeStruct((n, HEADS_W), BF16), jax.ShapeDtypeStruct((n, HEADS_W), F32)]
        + [jax.ShapeDtypeStruct((N_DEV,) + tuple(a.shape), a.dtype) for a in gather],
        scratch_shapes=_gather_scratch(n_g) if n_g else [],
        compiler_params=_cparams(("arbitrary", "arbitrary") if n_g else ("parallel", "parallel")),
    )(q, k, v, *gather)


def attn_bwd(q, k, v, o, do, lse, n, name):
    nk = k.shape[0]
    tq = _tile(n, (512, 256, 128))
    tk = _tile(nk, (768, 384, 256, 128))
    scale = QK_DIM ** -0.5
    log2e = math.log2(math.e)
    c2 = scale * log2e

    def body(q_ref, k_ref, v_ref, o_ref, do_ref, lse_ref, dq_ref, dk_ref, dv_ref):
        @pl.when(pl.program_id(1) == 0)
        def _():
            dq_ref[...] = jnp.zeros_like(dq_ref)

        kv, vv = k_ref[...], v_ref[...]

        def q_tile(i, carry):
            dk, dv = carry
            rows = pl.ds(pl.multiple_of(i * tq, tq), tq)
            qv, dov = q_ref[rows, :], do_ref[rows, :]
            p = jnp.exp2(_dot_nt(qv, kv) * c2 - lse_ref[rows, 0:1] * log2e)
            dv = dv + _dot_tn(p.astype(BF16), dov)
            dp = _dot_nt(dov, vv)
            delta = jnp.sum(dov.astype(F32) * o_ref[rows, :].astype(F32), axis=-1, keepdims=True)
            ds = (p * (dp - delta) * scale).astype(BF16)
            dk = dk + _dot_tn(ds, qv)
            dq_ref[rows, :] += jnp.dot(ds, kv, preferred_element_type=F32)
            return dk, dv

        zero = jnp.zeros((tk, HEAD_PAD), F32)
        dk, dv = lax.fori_loop(0, n // tq, q_tile, (zero, zero), unroll=2)
        dk_ref[...] = dk
        dv_ref[...] = dv

    qspec = pl.BlockSpec((n, HEAD_PAD), lambda h, j: (0, h))
    kspec = pl.BlockSpec((tk, HEAD_PAD), lambda h, j: (j, h))
    return pl.pallas_call(
        body, name=name, grid=(N_HEADS, nk // tk),
        in_specs=[qspec, kspec, kspec, qspec, qspec, qspec],
        out_specs=[qspec, kspec, kspec],
        out_shape=[jax.ShapeDtypeStruct((n, HEADS_W), F32), jax.ShapeDtypeStruct((nk, HEADS_W), F32),
                   jax.ShapeDtypeStruct((nk, HEADS_W), F32)],
        compiler_params=_cparams(("parallel", "arbitrary")),
    )(q, k, v, o, do, lse)


SCAN_LEVELS = (1, 2, 4)
SCAN_LANES = 512


def _scan_chunk(xr, xi, car, m_ref, p_ref, reverse):
    t_len, gn = xr.shape
    n_slab = t_len // SUBLANES
    for lb in range(gn // SCAN_LANES):
        ls = pl.ds(lb * SCAN_LANES, SCAN_LANES)

        def step(s, carry, ls=ls):
            cr, ci = carry
            slab = (n_slab - 1 - s) if reverse else s
            rows = pl.ds(pl.multiple_of(slab * SUBLANES, SUBLANES), SUBLANES)
            br, bi = xr[rows, ls], xi[rows, ls]
            for lvl, d in enumerate(SCAN_LEVELS):
                shift = SUBLANES - d if reverse else d
                sr, si = pltpu.roll(br, shift, 0), pltpu.roll(bi, shift, 0)
                mr, mi = m_ref[lvl, 0, :, ls], m_ref[lvl, 1, :, ls]
                br, bi = br + mr * sr - mi * si, bi + mr * si + mi * sr
            pr, pi = p_ref[0, :, ls], p_ref[1, :, ls]
            br, bi = br + pr * cr - pi * ci, bi + pr * ci + pi * cr
            xr[rows, ls] = br
            xi[rows, ls] = bi
            last = 0 if reverse else SUBLANES - 1
            return br[last:last + 1, :], bi[last:last + 1, :]

        cr, ci = lax.fori_loop(0, n_slab, step, (car[0:1, ls], car[1:2, ls]))
        car[0:1, ls] = cr
        car[1:2, ls] = ci


SSM_SPLIT = 2


def _bd_dot(a, w_ref):
    k, n = w_ref.shape[0] // SSM_SPLIT, w_ref.shape[1] // SSM_SPLIT
    return jnp.concatenate([jnp.dot(a[:, p * k:(p + 1) * k], w_ref[p * k:(p + 1) * k, p * n:(p + 1) * n],
                                    preferred_element_type=F32) for p in range(SSM_SPLIT)], axis=1)


def _bd_dot_nt(a, w_ref):
    k, n = w_ref.shape[0] // SSM_SPLIT, w_ref.shape[1] // SSM_SPLIT
    return jnp.concatenate([_dot_nt(a[:, p * n:(p + 1) * n], w_ref[p * k:(p + 1) * k, p * n:(p + 1) * n])
                            for p in range(SSM_SPLIT)], axis=1)


def _seq_block(step, n_chunk, lat_chunks, reverse):
    if reverse:
        return n_chunk - 1 - step
    return (step + lat_chunks) % n_chunk


def ssm_fwd(proj, lay, bre, bim, cre, ncim, tabs, n, t_len, reverse, name):
    l, sw = proj.shape[0], lay['sw']
    gn = bre.shape[-1]
    n_chunk, lat_chunks = l // t_len, n // t_len
    blk = lambda k: _seq_block(k, n_chunk, lat_chunks, reverse)
    mtab, ptab = tabs

    def body(u_ref, bre_ref, bim_ref, cre_ref, ncim_ref, m_ref, p_ref, y_ref, xb_ref, xr, xi, car):
        @pl.when(pl.program_id(0) == 0)
        def _():
            car[...] = jnp.zeros_like(car)

        xb_ref[...] = car[...]
        u = u_ref[...].astype(BF16)
        xr[...] = _bd_dot(u, bre_ref)
        xi[...] = _bd_dot(u, bim_ref)
        _scan_chunk(xr, xi, car, m_ref, p_ref, reverse)
        y_ref[...] = _bd_dot(xr[...].astype(BF16), cre_ref) + _bd_dot(xi[...].astype(BF16), ncim_ref)

    return pl.pallas_call(
        body, name=name, grid=(n_chunk,),
        in_specs=[pl.BlockSpec((t_len, sw), lambda k: (blk(k), lay['o_u'] // sw)), _full(bre.shape),
                  _full(bim.shape), _full(cre.shape), _full(ncim.shape), _full(mtab.shape), _full(ptab.shape)],
        out_specs=[pl.BlockSpec((t_len, sw), lambda k: (blk(k), 0)),
                   pl.BlockSpec((None, 2, gn), lambda k: (k, 0, 0))],
        out_shape=[jax.ShapeDtypeStruct((l, sw), F32), jax.ShapeDtypeStruct((n_chunk, 2, gn), F32)],
        scratch_shapes=[pltpu.VMEM((t_len, gn), F32), pltpu.VMEM((t_len, gn), F32), pltpu.VMEM((2, gn), F32)],
        compiler_params=_cparams(("arbitrary",)),
    )(proj, bre, bim, cre, ncim, mtab, ptab)


def ssm_bwd(proj, lay, dyr, xb, bre, bim, cre, ncim, tabs, adj_tabs, n, t_len, reverse, name):
    l, sw = proj.shape[0], lay['sw']
    gn = bre.shape[-1]
    n_chunk, lat_chunks = l // t_len, n // t_len
    fwd_step = lambda k: n_chunk - 1 - k
    blk = lambda k: _seq_block(fwd_step(k), n_chunk, lat_chunks, reverse)
    (mtab, ptab), (mtab_r, ptab_r) = tabs, adj_tabs

    def body(u_ref, dy_ref, xb_ref, bre_ref, bim_ref, cre_ref, ncim_ref, m_ref, p_ref, mr_ref, pr_ref,
             du_ref, gr_ref, gi_ref, xr_ref, xi_ref, dlam_ref, xr, xi, gr, gi, car, acar):
        k = pl.program_id(0)

        @pl.when(k == 0)
        def _():
            acar[...] = jnp.zeros_like(acar)
            dlam_ref[...] = jnp.zeros_like(dlam_ref)

        u = u_ref[...].astype(BF16)
        dy = jnp.where(blk(k) < lat_chunks, dy_ref[...], 0.0).astype(BF16)
        xr[...] = _bd_dot(u, bre_ref)
        xi[...] = _bd_dot(u, bim_ref)
        car[...] = xb_ref[...]
        _scan_chunk(xr, xi, car, m_ref, p_ref, reverse)
        gr[...] = _bd_dot_nt(dy, cre_ref)
        gi[...] = _bd_dot_nt(dy, ncim_ref)
        _scan_chunk(gr, gi, acar, mr_ref, pr_ref, not reverse)
        xrv, xiv, grv, giv = xr[...], xi[...], gr[...], gi[...]
        row = lax.broadcasted_iota(jnp.int32, (t_len, gn), 0)
        first, shift = (t_len - 1, t_len - 1) if reverse else (0, 1)
        xpr = jnp.where(row == first, xb_ref[0:1, :], pltpu.roll(xrv, shift, 0))
        xpi = jnp.where(row == first, xb_ref[1:2, :], pltpu.roll(xiv, shift, 0))
        dlr = grv * xpr + giv * xpi
        dli = giv * xpr - grv * xpi
        dlam_ref[0] += jnp.sum(dlr.reshape(t_len // SUBLANES, SUBLANES, gn), axis=0)
        dlam_ref[1] += jnp.sum(dli.reshape(t_len // SUBLANES, SUBLANES, gn), axis=0)
        grb, gib = grv.astype(BF16), giv.astype(BF16)
        du_ref[...] = _bd_dot_nt(grb, bre_ref) + _bd_dot_nt(gib, bim_ref)
        gr_ref[...] = grb
        gi_ref[...] = gib
        xr_ref[...] = xrv.astype(BF16)
        xi_ref[...] = xiv.astype(BF16)

    def at_blk(width, col=0):
        return pl.BlockSpec((t_len, width), lambda k: (blk(k), col))

    state = jax.ShapeDtypeStruct((l, gn), BF16)
    return pl.pallas_call(
        body, name=name, grid=(n_chunk,),
        in_specs=[at_blk(sw, lay['o_u'] // sw),
                  pl.BlockSpec((t_len, sw), lambda k: (jnp.minimum(blk(k), lat_chunks - 1), 0)),
                  pl.BlockSpec((None, 2, gn), lambda k: (fwd_step(k), 0, 0)),
                  _full(bre.shape), _full(bim.shape), _full(cre.shape), _full(ncim.shape), _full(mtab.shape),
                  _full(ptab.shape), _full(mtab_r.shape), _full(ptab_r.shape)],
        out_specs=[at_blk(sw), at_blk(gn), at_blk(gn), at_blk(gn), at_blk(gn), _full((2, SUBLANES, gn))],
        out_shape=[jax.ShapeDtypeStruct((l, sw), F32), state, state, state, state,
                   jax.ShapeDtypeStruct((2, SUBLANES, gn), F32)],
        scratch_shapes=[pltpu.VMEM((t_len, gn), F32)] * 4 + [pltpu.VMEM((2, gn), F32)] * 2,
        compiler_params=_cparams(("arbitrary",)),
    )(proj, dyr, xb, bre, bim, cre, ncim, mtab, ptab, mtab_r, ptab_r)


def ssm_prep(lam_re, lam_im, logdt, bre, bim, name):
    gn = lam_re.shape[1]

    def body(lr_ref, li_ref, dt_ref, br_ref, bi_ref, pwr_ref, pwi_ref, bbr_ref, bbi_ref):
        _, _, bbr, bbi = _ssm_prep_fn(lr_ref[...], li_ref[...], dt_ref[...], br_ref[...], bi_ref[...])
        bbr_ref[...] = bbr
        bbi_ref[...] = bbi
        kk = (lax.broadcasted_iota(jnp.int32, (SUBLANES, gn), 0) + 1).astype(F32)
        dt = jnp.exp(dt_ref[...])
        ar, ai = lr_ref[...] * dt * kk, li_ref[...] * dt * kk
        e = jnp.exp(ar)
        pwr_ref[...] = e * jnp.cos(ai)
        pwi_ref[...] = e * jnp.sin(ai)

    ins = (lam_re, lam_im, logdt, bre, bim)
    return pl.pallas_call(
        body, name=name,
        out_shape=[jax.ShapeDtypeStruct((SUBLANES, gn), F32)] * 2 + [jax.ShapeDtypeStruct(bre.shape, F32)] * 2,
        compiler_params=_cparams(),
    )(*ins)


def ssm_prep_bwd(lam_re, lam_im, logdt, bre, bim, dlbr, dlbi, dbbr, dbbi, name):
    def body(lr_ref, li_ref, dt_ref, br_ref, bi_ref, g0, g1, g2, g3, o0, o1, o2, o3, o4):
        _, vjp = jax.vjp(_ssm_prep_fn, lr_ref[...], li_ref[...], dt_ref[...], br_ref[...], bi_ref[...])
        for ref, val in zip((o0, o1, o2, o3, o4), vjp((g0[...], g1[...], g2[...], g3[...]))):
            ref[...] = val

    ins = (lam_re, lam_im, logdt, bre, bim)
    return pl.pallas_call(
        body, name=name,
        out_shape=[jax.ShapeDtypeStruct(a.shape, F32) for a in ins],
        compiler_params=_cparams(),
    )(*ins, dlbr, dlbi, dbbr, dbbi)


def mod_fwd(cmat, w, b, name):
    def body(c_ref, w_ref, b_ref, o_ref):
        o_ref[...] = _mod_fn(c_ref[...], w_ref[...]) + b_ref[...]

    return pl.pallas_call(body, name=name, out_shape=jax.ShapeDtypeStruct((cmat.shape[0], w.shape[1]), F32),
                          compiler_params=_cparams())(cmat, w, b)


def mod_bwd(cmat, w, g_lat, g_ctx, g_all, name):
    def body(c_ref, w_ref, gl_ref, gc_ref, ga_ref, dw_ref, dc_ref, db_ref):
        gc = jnp.sum(gc_ref[...], axis=0, keepdims=True)
        dm = jnp.concatenate([gl_ref[...], _pad_rows(gc)], axis=0)
        _, vjp = jax.vjp(_mod_fn, c_ref[...], w_ref[...])
        dc, dw = vjp(dm)
        dw_ref[...] = dw
        dc_ref[...] = dc
        db_ref[...] = _pad_rows(jnp.sum(ga_ref[...], axis=0, keepdims=True))

    return pl.pallas_call(
        body, name=name,
        out_shape=[jax.ShapeDtypeStruct(w.shape, F32), jax.ShapeDtypeStruct(cmat.shape, F32),
                   jax.ShapeDtypeStruct((SUBLANES, g_all.shape[1]), F32)],
        compiler_params=_cparams(),
    )(cmat, w, g_lat, g_ctx, g_all)


def add_own(own, idx, recv, out_dtype, name):
    _, s, r, c = own.shape

    def body(idx_ref, own_ref, recv_ref, o_ref):
        o_ref[...] = (own_ref[...] + recv_ref[...]).astype(o_ref.dtype)

    return pl.pallas_call(
        body, name=name,
        grid_spec=pltpu.PrefetchScalarGridSpec(
            num_scalar_prefetch=1, grid=(s,),
            in_specs=[pl.BlockSpec((None, None, r, c), lambda k, idx_ref: (idx_ref[0], k, 0, 0)),
                      pl.BlockSpec((None, r, c), lambda k, idx_ref: (k, 0, 0))],
            out_specs=pl.BlockSpec((None, r, c), lambda k, idx_ref: (k, 0, 0))),
        out_shape=jax.ShapeDtypeStruct((s, r, c), out_dtype),
        compiler_params=_cparams(("parallel",)),
    )(idx, own, recv)


def reduce_adamw(parts, w, m, v, name, own=None, idx=None):
    s, r, c = parts.shape
    tr = _tile(r, (256, 128, 64, 32, 16, 8))

    def body(*refs):
        if own is None:
            p_ref, w_ref, m_ref, v_ref, g_ref, d_ref, nm_ref, nv_ref = refs
            g = p_ref[0].astype(F32)
        else:
            _, own_ref, p_ref, w_ref, m_ref, v_ref, g_ref, d_ref, nm_ref, nv_ref = refs
            g = own_ref[...].astype(F32) + p_ref[0].astype(F32)
        for k in range(1, s):
            g = g + p_ref[k].astype(F32)
        mm = ADAM_B1 * m_ref[...] + (1.0 - ADAM_B1) * g
        vv = ADAM_B2 * v_ref[...] + (1.0 - ADAM_B2) * jnp.square(g)
        m_hat = mm / (1.0 - ADAM_B1 ** ADAM_STEP)
        v_hat = vv / (1.0 - ADAM_B2 ** ADAM_STEP)
        g_ref[...] = g
        d_ref[...] = -ADAM_LR * (m_hat / (jnp.sqrt(v_hat) + ADAM_EPS) + ADAM_WD * w_ref[...])
        nm_ref[...] = mm
        nv_ref[...] = vv

    out = jax.ShapeDtypeStruct((r, c), F32)
    if own is None:
        blk = _rows(tr, c)
        return pl.pallas_call(
            body, name=name, grid=(r // tr,),
            in_specs=[pl.BlockSpec((s, tr, c), lambda i: (0, i, 0)), blk, blk, blk],
            out_specs=[blk] * 4, out_shape=[out] * 4,
            compiler_params=_cparams(("parallel",)),
        )(parts, w, m, v)
    blk = pl.BlockSpec((tr, c), lambda i, idx_ref: (i, 0))
    return pl.pallas_call(
        body, name=name,
        grid_spec=pltpu.PrefetchScalarGridSpec(
            num_scalar_prefetch=1, grid=(r // tr,),
            in_specs=[pl.BlockSpec((None, tr, c), lambda i, idx_ref: (idx_ref[0], i, 0)),
                      pl.BlockSpec((s, tr, c), lambda i, idx_ref: (0, i, 0)), blk, blk, blk],
            out_specs=[blk] * 4),
        out_shape=[out] * 4,
        compiler_params=_cparams(("parallel",)),
    )(idx, own, parts, w, m, v)


def _in_layout(d, q, kv, sw):
    o_u = 2 * d
    o_ckv = o_u + sw
    o_kr = o_ckv + kv
    o_cq = -(-(o_kr + LANES) // q) * q
    assert o_u % sw == 0 and o_ckv % kv == 0 and o_kr % LANES == 0
    assert q % LANES == 0 and kv % LANES == 0 and sw % LANES == 0
    return dict(d=d, q=q, kv=kv, sw=sw, o_gl=0, o_u=o_u, o_ckv=o_ckv, o_kr=o_kr, o_cq=o_cq, width=o_cq + q)


def _pad_w_in(w_in, lay):
    q, kv, sw, d = lay['q'], lay['kv'], lay['sw'], lay['d']
    cq, ckv, kr, u, gl = jnp.split(w_in, [q, q + kv, q + kv + QK_ROPE, q + kv + QK_ROPE + sw], axis=1)
    z = lambda w: jnp.zeros((w_in.shape[0], w), w_in.dtype)
    hole = lay['o_cq'] - lay['o_kr'] - LANES
    return jnp.concatenate([gl, u, ckv, z(QK_NOPE), kr, z(LANES - QK_DIM), z(hole), cq], axis=1)


def _unpad_w_in(g, lay):
    q, kv, sw, d = lay['q'], lay['kv'], lay['sw'], lay['d']
    kr0 = lay['o_kr'] + QK_NOPE
    return jnp.concatenate([g[:, lay['o_cq']:lay['o_cq'] + q], g[:, lay['o_ckv']:lay['o_ckv'] + kv],
                            g[:, kr0:kr0 + QK_ROPE], g[:, lay['o_u']:lay['o_u'] + sw], g[:, :2 * d]], axis=1)


def _pad_heads(w, width):
    k = w.shape[0]
    return jnp.pad(w.reshape(k, N_HEADS, width), ((0, 0), (0, 0), (0, HEAD_PAD - width))).reshape(k, HEADS_W)


def _unpad_heads(w, width):
    k = w.shape[0]
    return w.reshape(k, N_HEADS, HEAD_PAD)[:, :, :width].reshape(k, N_HEADS * width)


def _rope_tables(n, nc):
    rows = n // GRID_W
    row = jnp.repeat(jnp.arange(rows), GRID_W)
    col = jnp.tile(jnp.arange(GRID_W), rows)
    pairs = QK_ROPE // 4
    freqs = ROPE_THETA ** (-jnp.arange(pairs, dtype=F32) / pairs)
    ang = jnp.concatenate([row[:, None] * freqs, col[:, None] * freqs], axis=-1)
    cos = jnp.concatenate([jnp.cos(ang), jnp.ones((nc, 2 * pairs), F32)], axis=0)
    sin = jnp.concatenate([jnp.sin(ang), jnp.zeros((nc, 2 * pairs), F32)], axis=0)
    l = n + nc
    half = QK_ROPE // 2
    ct = jnp.concatenate([jnp.ones((l, QK_NOPE), F32), cos, cos, jnp.zeros((l, HEAD_PAD - QK_DIM), F32)], axis=1)
    s1 = jnp.concatenate([jnp.zeros((l, QK_NOPE + half), F32), sin, jnp.zeros((l, HEAD_PAD - QK_DIM), F32)],
                         axis=1)
    s2 = jnp.concatenate([jnp.zeros((l, QK_NOPE), F32), -sin, jnp.zeros((l, HEAD_PAD - QK_NOPE - half), F32)],
                         axis=1)
    return ct, s1, s2


def _group_mask(rows, cols, g):
    return (jnp.arange(rows)[:, None] // (rows // g)) == (jnp.arange(cols)[None, :] // (cols // g))


def _block_diag_rows(m, g):
    return jnp.where(_group_mask(g * m.shape[0], m.shape[1], g), jnp.tile(m, (g, 1)), 0)


def _block_diag_cols(m, g):
    return jnp.where(_group_mask(m.shape[0], g * m.shape[1], g), jnp.tile(m, (1, g)), 0)


def _diag_blocks(m, g):
    a, b = m.shape[0] // g, m.shape[1] // g
    masked = jnp.where(_group_mask(m.shape[0], m.shape[1], g), m, 0)
    return masked.reshape(m.shape[0], g, b).sum(axis=1).reshape(g, a, b)


def _scan_tables(pwr, pwi, reverse):
    row = jnp.arange(SUBLANES)[:, None]
    zero = jnp.zeros_like(pwr)
    levels = []
    for d in SCAN_LEVELS:
        keep = (row < SUBLANES - d) if reverse else (row >= d)
        levels.append(jnp.stack([jnp.where(keep, pwr[d - 1:d, :], zero), jnp.where(keep, pwi[d - 1:d, :], zero)]))
    carry = jnp.stack([pwr[::-1], pwi[::-1]]) if reverse else jnp.stack([pwr, pwi])
    return jnp.stack(levels), carry


def _step(inp):
    x, c, ctx = inp['x'][0], inp['c'], inp['ctx'][0]
    target = inp['loss_target'][0]
    n, d = x.shape
    nc = ctx.shape[0]
    l = n + nc
    q_w, kv_w = inp['q_a_g'].shape[1], inp['kv_a_g'].shape[1]
    sw = inp['d_skip'].shape[1]
    n_grp = sw // SSM_GROUP
    gn = n_grp * SSM_STATE
    lay = _in_layout(d, q_w, kv_w, sw)
    tm = _tile(math.gcd(n, nc), (256, 128))
    me = 4 * lax.axis_index("x") + 2 * lax.axis_index("y") + lax.axis_index("c")
    strip = lambda a: a if a.ndim <= 2 else a[0]
    w = {k: strip(inp[k]) for k in WEIGHT_NAMES}

    gathered_names = list(GATHERED)
    early_names = ['w_in', 'w_uq', 'w_ukv', 'w_o_attn']
    late_names = [k for k in gathered_names if k not in early_names]
    c_all, *wg = gather_two_level([jnp.broadcast_to(c, (SUBLANES, d))] + [w[k].astype(BF16) for k in early_names],
                                  "gather_weights")
    full = {k: _from_shards(s, GATHERED[k]) for k, s in zip(early_names, wg)}

    w_in_p = _pad_w_in(full['w_in'], lay)
    wuq_p = _pad_heads(full['w_uq'], QK_DIM)
    ukv = full['w_ukv'].reshape(kv_w, N_HEADS, QK_NOPE + V_DIM)
    wk_p = _pad_heads(ukv[:, :, :QK_NOPE].reshape(kv_w, -1), QK_NOPE)
    wv_p = _pad_heads(ukv[:, :, QK_NOPE:].reshape(kv_w, -1), V_DIM)
    wo_p = _pad_heads(full['w_o_attn'].T, V_DIM).T
    conv_b = w['conv_b']

    cmat = jnp.concatenate([c_all[:, 0, :], w['c_ctx'][None, :], jnp.zeros((SUBLANES - 1, d), F32)], axis=0)
    mcols = w['w_mod'].shape[1]
    b_cols = lax.dynamic_slice(w['b_mod'], (0, me * mcols), (1, mcols))
    mod_part = mod_fwd(cmat, w['w_mod'], b_cols, "mod_fwd")
    (mod_all,) = exchange([mod_part], ['gather'], "gather_mod")
    mod_me = lax.dynamic_index_in_dim(mod_all, me, axis=1, keepdims=False).reshape(6, d)
    mod_ctx = mod_all[:, SUBLANES, :].reshape(6, d)
    sh1, sc1, g1, sh2, sc2, g2 = [mod_me[k:k + 1] for k in range(6)]
    mods1 = jnp.concatenate([sc1, sh1, mod_ctx[1:2], mod_ctx[0:1], jnp.zeros((4, d), F32)], axis=0)
    mods2 = jnp.concatenate([sc2, sh2, jnp.zeros((6, d), F32)], axis=0)

    xa = jnp.concatenate([x, ctx], axis=0)
    h = norm_mod_fwd(xa, w['norm1_g'], mods1, n, tm, "norm1_fwd")
    proj = matmul(h, w_in_p, 'nn', F32, "in_proj")
    tabs = _rope_tables(n, nc)
    pad_g = lambda g: jnp.pad(g, ((0, 0), (0, HEAD_PAD - QK_DIM)))
    gains = (w['q_a_g'], w['kv_a_g'], pad_g(w['q_norm_g']), pad_g(w['k_norm_g']))
    q, k, v = qkv_fwd(proj, lay, gains, wuq_p, wk_p, wv_p, tabs, tm, "qkv_fwd")
    o, lse, *wg = attn_fwd(q, k, v, n, "attn_fwd", gather=[w[k].astype(BF16) for k in late_names])
    full.update({k: _from_shards(s, GATHERED[k]) for k, s in zip(late_names, wg)})
    w_glu, w_out, w_up, w_down = full['w_glu'], full['w_out'], full['w_up'], full['w_down']
    conv_w = jnp.pad(full['conv_w'].astype(F32), ((0, SUBLANES - 3), (0, 0)))

    b_t = lambda a: a.transpose(2, 0, 1).reshape(SSM_GROUP, gn)
    c_t = lambda a: a.transpose(1, 0, 2).reshape(SSM_GROUP, gn)
    bre_t, bim_t = b_t(w['b_re']), b_t(w['b_im'])
    ssm_in, prep = [], []
    for sfx in ('f', 'b'):
        lam_re, lam_im = w['lam_re_' + sfx].reshape(1, gn), w['lam_im_' + sfx].reshape(1, gn)
        logdt = jnp.repeat(w['log_dt_' + sfx], SSM_STATE, axis=1)
        ssm_in.append((lam_re, lam_im, logdt))
        prep.append(ssm_prep(lam_re, lam_im, logdt, bre_t, bim_t, "ssm_prep_" + sfx))

    def blk_b(bb):
        return _block_diag_rows(bb, n_grp).astype(BF16)

    def blk_c(cc):
        return _block_diag_cols(cc.transpose(0, 2, 1).reshape(gn, SSM_GROUP), n_grp).astype(BF16)

    t_len = _tile(math.gcd(n, nc), (256, 128))
    ssm = []
    for di, sfx in enumerate(('f', 'b')):
        reverse = di == 1
        pwr, pwi, bbr, bbi = prep[di]
        ssm.append(dict(
            sfx=sfx, reverse=reverse, blocks=(blk_b(bbr), blk_b(bbi), blk_c(w['c_re_' + sfx]),
                                              blk_c(-w['c_im_' + sfx])),
            tabs=_scan_tables(pwr, pwi, reverse), adj_tabs=_scan_tables(pwr, -pwi, not reverse)))
    for s in ssm:
        s['y'], s['xb'] = ssm_fwd(proj, lay, *s['blocks'], s['tabs'], n, t_len, s['reverse'], "ssm_fwd_" + s['sfx'])
    yf, yb = ssm[0]['y'], ssm[1]['y']
    s_l = glu_fwd(proj, lay, yf, yb, w['d_skip'], w_glu, n, tm, "glu_fwd")
    x1 = merge_fwd(o, s_l, proj, xa, g1, wo_p, w_out, n, tm, "merge_fwd")

    h2 = norm_mod_fwd(x1, w['norm2_g'], mods2, n, tm, "norm2_fwd")
    up = matmul(h2, w_up, 'nn', F32, "up_proj")
    tw = _tile(n, (128,))
    act = conv_act_fwd(up, conv_w, conv_b, tw, "conv_act_fwd")
    dy, ffn, loss_parts = down_loss(act, w_down, x1, g2, target, tm, "down_loss")
    loss = lax.psum(jnp.sum(loss_parts[:, 0, 0]), MESH_AXES)

    dup, dffn, dg2, dconv = ffn_bwd(dy, ffn, up, conv_w, conv_b, w_down, g2, tm, "ffn_bwd")
    g_w_down = matmul(act, dffn, 'tn', F32, "dw_down")
    dh2 = matmul(dup, w_up, 'nt', F32, "dh2")
    g_w_up = matmul(h2, dup, 'tn', F32, "dw_up")
    dx1, dn2g, dmods2 = norm_mod_bwd(x1, dh2, dy, w['norm2_g'], mods2, n, tm, "norm2_bwd")

    do, ds_l, dgl, dg1, g_wo_p, g_w_out = merge_bwd(o, s_l, proj, xa, g1, wo_p, w_out, dx1, n, tm, "merge_bwd")
    du_direct, dyr, dds, g_w_glu = glu_bwd(proj, lay, yf, yb, w['d_skip'], w_glu, ds_l, n, tm, "glu_bwd")
    for s in ssm:
        s['du'], s['gr'], s['gi'], s['xr'], s['xi'], s['dlam'] = ssm_bwd(
            proj, lay, dyr, s['xb'], *s['blocks'], s['tabs'], s['adj_tabs'], n, t_len, s['reverse'],
            "ssm_bwd_" + s['sfx'])

    dq, dk, dv = attn_bwd(q, k, v, o, do, lse, n, "attn_bwd")
    (dproj, dqag, dkvag, dqng, dkng, g_wuq_p, g_wk_p, g_wv_p) = qkv_bwd(
        proj, lay, gains, wuq_p, wk_p, wv_p, tabs, dq, dk, dv, dgl, du_direct, ssm[0]['du'], ssm[1]['du'], n, tm,
        "qkv_bwd")
    dh = matmul(dproj, w_in_p, 'nt', F32, "dh")
    g_w_in_p = matmul(h, dproj, 'tn', F32, "dw_in")
    dxa, dn1g, dmods1 = norm_mod_bwd(xa, dh, dx1, w['norm1_g'], mods1, n, tm, "norm1_bwd")
    grad_x = dxa[:n]

    grads = {}
    d_bbar = [None, None]
    ka, kb, gp = sw // SSM_SPLIT, gn // SSM_SPLIT, n_grp // SSM_SPLIT
    products = []
    for s in ssm:
        for nm, a, a_col0, b in (("ssm_db_re", proj, lay['o_u'], s['gr']), ("ssm_db_im", proj, lay['o_u'], s['gi']),
                                 ("ssm_dc_re", dyr, 0, s['xr']), ("ssm_dc_im", dyr, 0, s['xi'])):
            products += [matmul(a, b, 'tn', F32, "%s_%s%d" % (nm, s['sfx'], p), a_col0=a_col0 + p * ka, a_cols=ka,
                                b_col0=p * kb, b_cols=kb) for p in range(SSM_SPLIT)]
    blocks = [_diag_blocks(m, gp) for m in products]
    for di, s in enumerate(ssm):
        sfx = s['sfx']
        g_bre, g_bim, g_cre, g_cim = [
            jnp.concatenate(blocks[(4 * di + k) * SSM_SPLIT:(4 * di + k + 1) * SSM_SPLIT], axis=0) for k in range(4)]
        grads['c_re_' + sfx] = g_cre
        grads['c_im_' + sfx] = -g_cim
        to_t = lambda a: a.transpose(1, 0, 2).reshape(SSM_GROUP, gn)
        dlam_re = jnp.sum(s['dlam'][0], axis=0, keepdims=True)
        dlam_im = jnp.sum(s['dlam'][1], axis=0, keepdims=True)
        lam_re, lam_im, logdt = ssm_in[di]
        g_lr, g_li, g_dt, g_br, g_bi = ssm_prep_bwd(lam_re, lam_im, logdt, bre_t, bim_t, dlam_re, dlam_im,
                                                    to_t(g_bre), to_t(g_bim), "ssm_prep_bwd_" + sfx)
        grads['lam_re_' + sfx] = g_lr.reshape(n_grp, SSM_STATE)
        grads['lam_im_' + sfx] = g_li.reshape(n_grp, SSM_STATE)
        grads['log_dt_' + sfx] = jnp.sum(g_dt.reshape(n_grp, SSM_STATE), axis=1)[None, :]
        d_bbar[di] = (g_br, g_bi)
    from_t = lambda a: a.reshape(SSM_GROUP, n_grp, SSM_STATE).transpose(1, 2, 0)
    grads['b_re'] = from_t(d_bbar[0][0]) + from_t(d_bbar[1][0])
    grads['b_im'] = from_t(d_bbar[0][1]) + from_t(d_bbar[1][1])

    dmod = jnp.concatenate([dmods1[1:2], dmods1[0:1], dg1[0:1], dmods2[1:2], dmods2[0:1], dg2[0:1]], axis=1)
    dmod_ctx = jnp.concatenate([dmods1[3:4], dmods1[2:3], jnp.zeros((1, 4 * d), F32)], axis=1)
    dm_send = jnp.concatenate([dmod, dmod_ctx, jnp.zeros((SUBLANES - 2, 6 * d), F32)], axis=0)
    (dm_all,) = exchange([dm_send], ['gather'], "gather_dmod")
    g_all = jnp.concatenate([dm_all[:, 0, :], dm_all[:, 1, :]], axis=0)
    cols = lax.dynamic_slice(g_all.reshape(2 * N_DEV, N_DEV, mcols), (0, me, 0), (2 * N_DEV, 1, mcols))[:, 0, :]
    g_w_mod, dcmat, g_b_mod = mod_bwd(cmat, w['w_mod'], cols[:N_DEV], cols[N_DEV:], g_all, "mod_bwd")

    ukv_g = jnp.concatenate([_unpad_heads(g_wk_p, QK_NOPE).reshape(kv_w, N_HEADS, QK_NOPE),
                             _unpad_heads(g_wv_p, V_DIM).reshape(kv_w, N_HEADS, V_DIM)], axis=2)
    full_g = {'w_in': _unpad_w_in(g_w_in_p, lay), 'w_uq': _unpad_heads(g_wuq_p, QK_DIM),
              'w_ukv': ukv_g.reshape(kv_w, -1), 'w_o_attn': _unpad_heads(g_wo_p.T, V_DIM).T, 'w_glu': g_w_glu,
              'w_out': g_w_out, 'w_up': g_w_up, 'conv_w': dconv[0:3], 'w_down': g_w_down}
    def by_core(full_grad, axis):
        sh = _to_shards(full_grad, axis)
        return sh.reshape((N_CHIPS, 2) + sh.shape[1:]).swapaxes(0, 1)

    sends = [by_core(full_g[k], GATHERED[k]) for k in gathered_names]
    grads.update({'c_ctx': dcmat[SUBLANES], 'b_mod': g_b_mod[0:1], 'norm1_g': dn1g[0:1], 'norm2_g': dn2g[0:1],
                  'q_a_g': dqag[0:1], 'kv_a_g': dkvag[0:1], 'q_norm_g': dqng[0:1, :QK_DIM],
                  'k_norm_g': dkng[0:1, :QK_DIM], 'd_skip': dds[0:1], 'conv_b': dconv[3:4]})
    first = (me == 0).astype(F32)
    rep_parts = [grads[k] * first if k == 'b_mod' else grads[k] for k in REPLICATED]
    rpack, rspans = _pack(rep_parts, SUBLANES)
    my_core = lax.axis_index("c").astype(jnp.int32).reshape(1)
    my_chip = (2 * lax.axis_index("x") + lax.axis_index("y")).astype(jnp.int32).reshape(1)
    r_both = jnp.broadcast_to(rpack[None, None], (2, 1) + rpack.shape)
    from_sibling = exchange(sends + [r_both], ['others'] * (len(sends) + 1), "grads_in_chip", group='core')
    chip_sums = [add_own(s, my_core, got[0], BF16, "chip_sum_" + k)
                 for s, got, k in zip(sends, from_sibling, gathered_names)]
    r_sum = add_own(r_both, my_core, from_sibling[-1][0], F32, "chip_sum_replicated")[0]
    *g_recv, r_recv = exchange(chip_sums + [r_sum], ['others'] * len(sends) + ['gather'], "grads_between_chips",
                               group='chips')

    outs = {}

    def update(parts, k, w_k, m_k, v_k, **own):
        res = reduce_adamw(parts, w_k, m_k, v_k, "adamw_" + k, **own)
        return dict(zip(('grad_', 'delta_', 'new_m_', 'new_v_'), res))

    for k, parts, own in zip(gathered_names + ['w_mod'], g_recv + [g_w_mod[None]], chip_sums + [None]):
        own_args = {} if own is None else dict(own=own, idx=my_chip)
        for kind, a in update(parts, k, w[k], strip(inp['m_' + k]), strip(inp['v_' + k]), **own_args).items():
            outs[kind + k] = a[None]
    rep = lambda prefix: _pack([strip(inp[prefix + k]) for k in REPLICATED], SUBLANES)[0]
    for kind, buf in update(r_recv, "replicated", rep(''), rep('m_'), rep('v_')).items():
        for k, a in zip(REPLICATED, _unpack(buf, rspans, [w[k].shape for k in REPLICATED])):
            outs[kind + k] = a if inp[k].ndim <= 2 else a[None]
    result = [loss, grad_x[None]]
    for kind in ('grad_', 'delta_', 'new_m_', 'new_v_'):
        result += [outs[kind + k] for k in WEIGHT_NAMES]
    return tuple(result)


_ARG_NAMES = (['x', 'c', 'ctx'] + WEIGHT_NAMES + ['loss_target'] + ['m_' + k for k in WEIGHT_NAMES]
              + ['v_' + k for k in WEIGHT_NAMES])


def kernel(*args):
    assert len(args) == len(_ARG_NAMES)
    return _step(dict(zip(_ARG_NAMES, args)))
```

```python
import functools
import math

import jax
import jax.numpy as jnp
from jax import lax
from jax.experimental import pallas as pl
from jax.experimental.pallas import tpu as pltpu

F32 = jnp.float32
BF16 = jnp.bfloat16

N_DEV = 8
MESH_AXES = ("x", "y", "c")
N_HEADS = 8
QK_NOPE = 64
QK_ROPE = 32
QK_DIM = QK_NOPE + QK_ROPE
V_DIM = 64
HEAD_PAD = 128
HEADS_W = N_HEADS * HEAD_PAD
GRID_W = 64
ROPE_THETA = 10000.0
SSM_GROUP = 16
SSM_STATE = 64
EPS = 1e-6
LANES = 128
SUBLANES = 8
PACK_W = 1024
VMEM_LIMIT = 56 * 1024 * 1024
MM_TILES = (1024, 768, 1408, 512, 384, 256, 128)

ADAM_LR = 0.001
ADAM_B1 = 0.9
ADAM_B2 = 0.999
ADAM_EPS = 1e-08
ADAM_WD = 0.01
ADAM_STEP = 10

WEIGHT_NAMES = ['c_ctx', 'w_mod', 'b_mod', 'norm1_g', 'norm2_g', 'w_in', 'q_a_g', 'w_uq', 'kv_a_g', 'w_ukv',
                'q_norm_g', 'k_norm_g', 'w_o_attn', 'lam_re_f', 'lam_im_f', 'log_dt_f', 'c_re_f', 'c_im_f',
                'lam_re_b', 'lam_im_b', 'log_dt_b', 'c_re_b', 'c_im_b', 'b_re', 'b_im', 'd_skip', 'w_glu',
                'w_out', 'w_up', 'conv_w', 'conv_b', 'w_down']
GATHERED = {'w_in': 1, 'w_uq': 1, 'w_ukv': 1, 'w_o_attn': 1, 'w_glu': 1, 'w_out': 0, 'w_up': 1, 'conv_w': 1,
            'w_down': 0}
REPLICATED = [n for n in WEIGHT_NAMES if n not in GATHERED and n != 'w_mod']


def _tile(n, prefs):
    for t in prefs:
        if n % t == 0:
            return t
    return n


def _cparams(sem=None):
    return pltpu.CompilerParams(dimension_semantics=sem, vmem_limit_bytes=VMEM_LIMIT)


@jax.custom_vjp
def bdot(a, w):
    return jnp.dot(a.astype(BF16), w.astype(BF16), preferred_element_type=F32)


def _bdot_fwd(a, w):
    return bdot(a, w), (a, w)


def _bdot_bwd(res, g):
    a, w = res
    gb = g.astype(BF16)
    da = lax.dot_general(gb, w.astype(BF16), (((1,), (1,)), ((), ())), preferred_element_type=F32)
    dw = lax.dot_general(a.astype(BF16), gb, (((0,), (0,)), ((), ())), preferred_element_type=F32)
    return da.astype(a.dtype), dw.astype(w.dtype)


bdot.defvjp(_bdot_fwd, _bdot_bwd)


def _dot_nt(a, b):
    return lax.dot_general(a, b, (((1,), (1,)), ((), ())), preferred_element_type=F32)


def _dot_tn(a, b):
    return lax.dot_general(a, b, (((0,), (0,)), ((), ())), preferred_element_type=F32)


def matmul(a, b, mode, out_dtype, name, a_col0=0, a_cols=None, b_col0=0, b_cols=None):
    if mode == 'nn':
        (m, k), n = a.shape, b.shape[1]
    elif mode == 'nt':
        (m, k), n = a.shape, b.shape[0]
    else:
        (k, m), n = a.shape, b.shape[1]
        m, n = a_cols or m, b_cols or n
    tm = _tile(m, MM_TILES)
    tn = _tile(n, MM_TILES)
    tk = _tile(k, MM_TILES)
    nk = k // tk
    assert a_col0 % tm == 0 and b_col0 % tn == 0
    col0, bcol0 = a_col0 // tm, b_col0 // tn

    def body(a_ref, b_ref, o_ref, acc_ref):
        kk = pl.program_id(2)

        @pl.when(kk == 0)
        def _():
            acc_ref[...] = jnp.zeros_like(acc_ref)

        av, bv = a_ref[...].astype(BF16), b_ref[...].astype(BF16)
        if mode == 'nn':
            acc_ref[...] += jnp.dot(av, bv, preferred_element_type=F32)
        elif mode == 'nt':
            acc_ref[...] += _dot_nt(av, bv)
        else:
            acc_ref[...] += _dot_tn(av, bv)

        @pl.when(kk == nk - 1)
        def _():
            o_ref[...] = acc_ref[...].astype(o_ref.dtype)

    if mode == 'nn':
        a_spec = pl.BlockSpec((tm, tk), lambda i, j, kk: (i, kk))
        b_spec = pl.BlockSpec((tk, tn), lambda i, j, kk: (kk, j))
    elif mode == 'nt':
        a_spec = pl.BlockSpec((tm, tk), lambda i, j, kk: (i, kk))
        b_spec = pl.BlockSpec((tn, tk), lambda i, j, kk: (j, kk))
    else:
        a_spec = pl.BlockSpec((tk, tm), lambda i, j, kk: (kk, i + col0))
        b_spec = pl.BlockSpec((tk, tn), lambda i, j, kk: (kk, j + bcol0))
    return pl.pallas_call(
        body, name=name, grid=(m // tm, n // tn, nk),
        in_specs=[a_spec, b_spec],
        out_specs=pl.BlockSpec((tm, tn), lambda i, j, kk: (i, j)),
        out_shape=jax.ShapeDtypeStruct((m, n), out_dtype),
        scratch_shapes=[pltpu.VMEM((tm, tn), F32)],
        compiler_params=_cparams(("parallel", "parallel", "arbitrary")),
    )(a, b)


N_CHIPS = 4


def _group(group):
    x, y, c = lax.axis_index("x"), lax.axis_index("y"), lax.axis_index("c")
    flips = {'all': [(r & 4, r & 2, r & 1) for r in range(1, 8)],
             'chips': [(0, 1, 0), (1, 0, 0), (1, 1, 0)], 'core': [(0, 0, 1)]}[group]
    index = {'all': lambda px, py, pc: 4 * px + 2 * py + pc, 'chips': lambda px, py, pc: 2 * px + py,
             'core': lambda px, py, pc: pc}[group]
    peers = []
    for fx, fy, fc in flips:
        p = (1 - x if fx else x, 1 - y if fy else y, 1 - c if fc else c)
        peers.append((p, index(*p)))
    return len(flips) + 1, index(x, y, c), peers


def exchange(arrays, kinds, name, group='all'):
    n_arr = len(arrays)

    def body(*refs):
        start, finish = _exchange_ops(refs[:n_arr], refs[n_arr:2 * n_arr], *refs[2 * n_arr:], kinds, group)
        start()
        finish()

    any_spec = pl.BlockSpec(memory_space=pl.ANY)
    return pl.pallas_call(
        body, name=name,
        in_specs=[any_spec] * n_arr, out_specs=[any_spec] * n_arr, out_shape=_exchange_shapes(arrays, kinds, group),
        scratch_shapes=_exchange_scratch(n_arr, group),
        compiler_params=pltpu.CompilerParams(has_side_effects=True),
    )(*arrays)


GROUP_SIZE = {'all': N_DEV, 'chips': N_CHIPS, 'core': 2}


def _exchange_shapes(arrays, kinds, group):
    size = GROUP_SIZE[group]
    return [jax.ShapeDtypeStruct({'gather': (size,) + tuple(arr.shape), 'a2a': tuple(arr.shape),
                                  'others': (size - 1,) + tuple(arr.shape[1:])}[kind], arr.dtype)
            for arr, kind in zip(arrays, kinds)]


def _exchange_scratch(n_arr, group):
    size = GROUP_SIZE[group]
    return [pltpu.SemaphoreType.DMA((n_arr * size,)), pltpu.SemaphoreType.DMA((n_arr * size,)),
            pltpu.SemaphoreType.DMA((n_arr,))]


def _exchange_ops(srcs, dsts, send_sems, recv_sems, local_sems, kinds, group):
    n_arr = len(srcs)
    size, me, peers = _group(group)

    def copy(a, r, peer, peer_idx, receiving):
        src = srcs[a] if kinds[a] == 'gather' else srcs[a].at[peer_idx]
        if kinds[a] == 'others':
            dst = dsts[a].at[r]
        else:
            dst = dsts[a].at[peer_idx if receiving else me]
        return pltpu.make_async_remote_copy(
            src_ref=src, dst_ref=dst, send_sem=send_sems.at[a * size + r], recv_sem=recv_sems.at[a * size + r],
            device_id=peer, device_id_type=pl.DeviceIdType.MESH)

    def local(a):
        mine = srcs[a] if kinds[a] == 'gather' else srcs[a].at[me]
        return pltpu.make_async_copy(mine, dsts[a].at[me], local_sems.at[a])

    def start():
        for a in range(n_arr):
            if kinds[a] != 'others':
                local(a).start()
            for r, (peer, peer_idx) in enumerate(peers):
                copy(a, r, peer, peer_idx, False).start()

    def finish():
        for a in range(n_arr):
            for r, (peer, peer_idx) in enumerate(peers):
                cp = copy(a, r, peer, peer_idx, True)
                cp.wait_send()
                cp.wait_recv()
            if kinds[a] != 'others':
                local(a).wait()

    return start, finish


GATHER_COPIES = 7


def _two_level_gather(srcs, outs, send_sems, recv_sems, local_sems):
    n_arr = len(srcs)
    x, y, c = lax.axis_index("x"), lax.axis_index("y"), lax.axis_index("c")
    me, sibling = (x, y, c), (x, y, 1 - c)
    chips = [(1 - x, y), (x, 1 - y), (1 - x, 1 - y)]
    slot = lambda p: 4 * p[0] + 2 * p[1] + p[2]

    def copy(a, k, block, to, own=False):
        dst = outs[a].at[slot(block)]
        return pltpu.make_async_remote_copy(
            src_ref=srcs[a] if own else dst, dst_ref=dst,
            send_sem=send_sems.at[a * GATHER_COPIES + k], recv_sem=recv_sems.at[a * GATHER_COPIES + k],
            device_id=to, device_id_type=pl.DeviceIdType.MESH)

    def own_copies(a):
        return [copy(a, 0, me, sibling, own=True)] + [copy(a, 1 + j, me, (*chip, c), own=True)
                                                      for j, chip in enumerate(chips)]

    local = lambda a: pltpu.make_async_copy(srcs[a], outs[a].at[slot(me)], local_sems.at[a])

    def start():
        for a in range(n_arr):
            local(a).start()
            for cp in own_copies(a):
                cp.start()

    def finish():
        passed = []
        for j, chip in enumerate(chips):
            for a in range(n_arr):
                copy(a, 1 + j, (*chip, c), me).wait_recv()
                passed.append(copy(a, 4 + j, (*chip, c), sibling))
                passed[-1].start()
        for a in range(n_arr):
            copy(a, 0, sibling, me).wait_recv()
            for j, chip in enumerate(chips):
                copy(a, 4 + j, (*chip, 1 - c), me).wait_recv()
            for cp in own_copies(a):
                cp.wait_send()
        for cp in passed:
            cp.wait_send()
        for a in range(n_arr):
            local(a).wait()

    return start, finish


def _gather_scratch(n_arr):
    return [pltpu.SemaphoreType.DMA((n_arr * GATHER_COPIES,)), pltpu.SemaphoreType.DMA((n_arr * GATHER_COPIES,)),
            pltpu.SemaphoreType.DMA((n_arr,))]


def gather_two_level(arrays, name):
    n_arr = len(arrays)

    def body(*refs):
        start, finish = _two_level_gather(refs[:n_arr], refs[n_arr:2 * n_arr], *refs[2 * n_arr:])
        start()
        finish()

    any_spec = pl.BlockSpec(memory_space=pl.ANY)
    return pl.pallas_call(
        body, name=name,
        in_specs=[any_spec] * n_arr, out_specs=[any_spec] * n_arr,
        out_shape=[jax.ShapeDtypeStruct((N_DEV,) + tuple(a.shape), a.dtype) for a in arrays],
        scratch_shapes=_gather_scratch(n_arr),
        compiler_params=pltpu.CompilerParams(has_side_effects=True),
    )(*arrays)


def _pack(parts, row_mult):
    rows, spans, r = [], [], 0
    for p in parts:
        flat = p.reshape(-1)
        nr = -(-flat.shape[0] // (PACK_W * row_mult)) * row_mult
        flat = jnp.pad(flat, (0, nr * PACK_W - flat.shape[0]))
        rows.append(flat.reshape(nr, PACK_W))
        spans.append((r, nr))
        r += nr
    return jnp.concatenate(rows, axis=0), spans


def _unpack(buf, spans, shapes):
    out = []
    for (r, nr), shp in zip(spans, shapes):
        size = math.prod(shp)
        out.append(buf[..., r:r + nr, :].reshape(buf.shape[:-2] + (nr * PACK_W,))[..., :size]
                   .reshape(buf.shape[:-2] + tuple(shp)))
    return out


def _to_shards(full, axis):
    r, c = full.shape
    if axis == 0:
        return full.reshape(N_DEV, r // N_DEV, c)
    return full.reshape(r, N_DEV, c // N_DEV).transpose(1, 0, 2)


def _from_shards(sh, axis):
    _, r, c = sh.shape
    if axis == 0:
        return sh.reshape(N_DEV * r, c)
    return sh.transpose(1, 0, 2).reshape(r, N_DEV * c)


def _rms(x, g, n):
    ms = jnp.sum(x * x, axis=-1, keepdims=True) * (1.0 / n)
    return x * lax.rsqrt(ms + EPS) * g


def _norm_mod_fn(x, g, sc, sh):
    return _rms(x, g, x.shape[-1]) * (1.0 + sc) + sh


@jax.custom_vjp
def _rope(t, ct, s1, s2):
    return t * ct + pltpu.roll(t, 16, 1) * s1 + pltpu.roll(t, HEAD_PAD - 16, 1) * s2


def _rope_fwd(t, ct, s1, s2):
    return _rope(t, ct, s1, s2), (ct, s1, s2)


def _rope_bwd(res, d):
    ct, s1, s2 = res
    dt = d * ct + pltpu.roll(d * s1, HEAD_PAD - 16, 1) + pltpu.roll(d * s2, 16, 1)
    return dt, jnp.zeros_like(ct), jnp.zeros_like(s1), jnp.zeros_like(s2)


_rope.defvjp(_rope_fwd, _rope_bwd)


def _qkv_fn(cq, ckv, krsec, qag, kvag, qng, kng, wuq, wk, wv, ct, s1, s2):
    q_raw = bdot(_rms(cq, qag, cq.shape[-1]), wuq)
    ckvn = _rms(ckv, kvag, ckv.shape[-1])
    k_raw = bdot(ckvn, wk)
    v = bdot(ckvn, wv)
    qs, ks = [], []
    for h in range(N_HEADS):
        sl = slice(h * HEAD_PAD, (h + 1) * HEAD_PAD)
        qs.append(_rope(_rms(q_raw[:, sl], qng, QK_DIM), ct, s1, s2))
        ks.append(_rope(_rms(k_raw[:, sl] + krsec, kng, QK_DIM), ct, s1, s2))
    return jnp.concatenate(qs, axis=1), jnp.concatenate(ks, axis=1), v


def _glu_fn(u, yf, yb, dskip, wglu):
    y = u * dskip + yf + yb
    vg = bdot(jax.nn.gelu(y), wglu)
    d = vg.shape[-1] // 2
    return vg[:, :d] * jax.nn.sigmoid(vg[:, d:])


def _merge_fn(o, s_l, gl, x, g1, wo, wout):
    d = x.shape[-1]
    a = bdot(o, wo)
    mix = jax.nn.sigmoid(gl[:, :d]) * a + jax.nn.sigmoid(gl[:, d:]) * s_l
    return x + g1 * bdot(mix, wout)


def _mod_fn(cmat, w):
    return bdot(jax.nn.silu(cmat), w)


def _ssm_prep_fn(lam_re, lam_im, logdt, bre, bim):
    dt = jnp.exp(logdt)
    ar, ai = lam_re * dt, lam_im * dt
    e = jnp.exp(ar)
    lbr, lbi = e * jnp.cos(ai), e * jnp.sin(ai)
    nr, ni = lbr - 1.0, lbi
    den = lam_re * lam_re + lam_im * lam_im
    qr = (nr * lam_re + ni * lam_im) / den
    qi = (ni * lam_re - nr * lam_im) / den
    return lbr, lbi, qr * bre - qi * bim, qr * bim + qi * bre


def _rows(tm, w, col=0):
    return pl.BlockSpec((tm, w), lambda i: (i, col))


def _full(shape):
    nd = len(shape)
    return pl.BlockSpec(tuple(shape), lambda i: (0,) * nd)


def _acc_add(i, ref, val):
    @pl.when(i == 0)
    def _():
        ref[...] = jnp.zeros_like(ref)
    ref[...] += val


def _pad_rows(v, rows=SUBLANES):
    sel = lax.broadcasted_iota(jnp.int32, (rows, v.shape[-1]), 0) == 0
    return jnp.where(sel, jnp.broadcast_to(v, (rows, v.shape[-1])), 0.0)


def norm_mod_fwd(xa, g, mods, n_lat, tm, name):
    r, d = xa.shape
    lat_tiles = n_lat // tm

    def body(x_ref, g_ref, m_ref, o_ref):
        lat = pl.program_id(0) < lat_tiles
        sc = jnp.where(lat, m_ref[0:1, :], m_ref[2:3, :])
        sh = jnp.where(lat, m_ref[1:2, :], m_ref[3:4, :])
        o_ref[...] = _norm_mod_fn(x_ref[...], g_ref[...], sc, sh).astype(o_ref.dtype)

    return pl.pallas_call(
        body, name=name, grid=(r // tm,),
        in_specs=[_rows(tm, d), _full(g.shape), _full(mods.shape)],
        out_specs=_rows(tm, d), out_shape=jax.ShapeDtypeStruct((r, d), BF16),
        compiler_params=_cparams(("parallel",)),
    )(xa, g, mods)


def norm_mod_bwd(xa, dh, dres, g, mods, n_lat, tm, name):
    r, d = xa.shape
    lat_tiles = n_lat // tm

    def body(x_ref, dh_ref, dres_ref, g_ref, m_ref, dx_ref, dg_ref, dm_ref):
        i = pl.program_id(0)
        lat = i < lat_tiles
        sc = jnp.where(lat, m_ref[0:1, :], m_ref[2:3, :])
        sh = jnp.where(lat, m_ref[1:2, :], m_ref[3:4, :])
        _, vjp = jax.vjp(_norm_mod_fn, x_ref[...], g_ref[...], sc, sh)
        dx, dg, dsc, dsh = vjp(dh_ref[...])
        dx_ref[...] = dx + jnp.where(lat, dres_ref[...], 0.0)
        _acc_add(i, dg_ref, _pad_rows(dg))
        row = lax.broadcasted_iota(jnp.int32, (SUBLANES, d), 0)
        base = jnp.where(lat, 0, 2)
        upd = jnp.where(row == base, jnp.broadcast_to(dsc, (SUBLANES, d)), 0.0)
        upd = upd + jnp.where(row == base + 1, jnp.broadcast_to(dsh, (SUBLANES, d)), 0.0)
        _acc_add(i, dm_ref, upd)

    return pl.pallas_call(
        body, name=name, grid=(r // tm,),
        in_specs=[_rows(tm, d), _rows(tm, d),
                  pl.BlockSpec((tm, d), lambda i: (jnp.minimum(i, lat_tiles - 1), 0)),
                  _full(g.shape), _full(mods.shape)],
        out_specs=[_rows(tm, d), _full((SUBLANES, d)), _full((SUBLANES, d))],
        out_shape=[jax.ShapeDtypeStruct((r, d), F32), jax.ShapeDtypeStruct((SUBLANES, d), F32),
                   jax.ShapeDtypeStruct((SUBLANES, d), F32)],
        compiler_params=_cparams(("arbitrary",)),
    )(xa, dh, dres, g, mods)


def qkv_fwd(proj, lay, gains, wuq, wk, wv, tabs, tm, name):
    r = proj.shape[0]
    q_w, kv_w = lay['q'], lay['kv']

    def body(cq_ref, ckv_ref, kr_ref, qag, kvag, qng, kng, wuq_ref, wk_ref, wv_ref, ct, s1, s2, q_ref, k_ref,
             v_ref):
        q, k, v = _qkv_fn(cq_ref[...], ckv_ref[...], kr_ref[...], qag[...], kvag[...], qng[...], kng[...],
                          wuq_ref[...], wk_ref[...], wv_ref[...], ct[...], s1[...], s2[...])
        q_ref[...] = q.astype(BF16)
        k_ref[...] = k.astype(BF16)
        v_ref[...] = v.astype(BF16)

    out = jax.ShapeDtypeStruct((r, HEADS_W), BF16)
    return pl.pallas_call(
        body, name=name, grid=(r // tm,),
        in_specs=[_rows(tm, q_w, lay['o_cq'] // q_w), _rows(tm, kv_w, lay['o_ckv'] // kv_w),
                  _rows(tm, LANES, lay['o_kr'] // LANES)]
        + [_full(a.shape) for a in gains] + [_full(wuq.shape), _full(wk.shape), _full(wv.shape)]
        + [_rows(tm, HEAD_PAD)] * 3,
        out_specs=[_rows(tm, HEADS_W)] * 3, out_shape=[out, out, out],
        compiler_params=_cparams(("parallel",)),
    )(proj, proj, proj, *gains, wuq, wk, wv, *tabs)


def qkv_bwd(proj, lay, gains, wuq, wk, wv, tabs, dq, dk, dv, dgl, du_direct, du_f, du_b, n, tm, name):
    r = proj.shape[0]
    q_w, kv_w, sw, d = lay['q'], lay['kv'], lay['sw'], lay['d']
    lat_tiles = n // tm

    def body(cq_ref, ckv_ref, kr_ref, qag, kvag, qng, kng, wuq_ref, wk_ref, wv_ref, ct, s1, s2, dq_ref, dk_ref,
             dv_ref, dgl_ref, dud_ref, duf_ref, dub_ref, dp_ref, dqag, dkvag, dqng, dkng, dwuq, dwk, dwv):
        i = pl.program_id(0)
        lat = i < lat_tiles
        tables = (ct[...], s1[...], s2[...])
        fn = lambda *a: _qkv_fn(*a, *tables)
        _, vjp = jax.vjp(fn, cq_ref[...], ckv_ref[...], kr_ref[...], qag[...], kvag[...], qng[...], kng[...],
                         wuq_ref[...].astype(F32), wk_ref[...].astype(F32), wv_ref[...].astype(F32))
        g = vjp((jnp.where(lat, dq_ref[...], 0.0), dk_ref[...], dv_ref[...]))
        dp_ref[:, 0:2 * d] = jnp.where(lat, dgl_ref[...], 0.0).astype(BF16)
        dp_ref[:, lay['o_u']:lay['o_u'] + sw] = (duf_ref[...] + dub_ref[...]
                                                 + jnp.where(lat, dud_ref[...], 0.0)).astype(BF16)
        dp_ref[:, lay['o_ckv']:lay['o_ckv'] + kv_w] = g[1].astype(BF16)
        dp_ref[:, lay['o_kr']:lay['o_kr'] + LANES] = g[2].astype(BF16)
        hole0 = lay['o_kr'] + LANES
        if lay['o_cq'] > hole0:
            dp_ref[:, hole0:lay['o_cq']] = jnp.zeros((tm, lay['o_cq'] - hole0), BF16)
        dp_ref[:, lay['o_cq']:lay['o_cq'] + q_w] = g[0].astype(BF16)
        for ref, val in zip((dqag, dkvag, dqng, dkng), g[3:7]):
            _acc_add(i, ref, _pad_rows(val))
        for ref, val in zip((dwuq, dwk, dwv), g[7:10]):
            _acc_add(i, ref, val)

    def lat_rows(w):
        return pl.BlockSpec((tm, w), lambda i: (jnp.minimum(i, lat_tiles - 1), 0))

    acc_shapes = [(SUBLANES, a.shape[1]) for a in gains] + [wuq.shape, wk.shape, wv.shape]
    return pl.pallas_call(
        body, name=name, grid=(r // tm,),
        in_specs=[_rows(tm, q_w, lay['o_cq'] // q_w), _rows(tm, kv_w, lay['o_ckv'] // kv_w),
                  _rows(tm, LANES, lay['o_kr'] // LANES)]
        + [_full(a.shape) for a in gains] + [_full(wuq.shape), _full(wk.shape), _full(wv.shape)]
        + [_rows(tm, HEAD_PAD)] * 3 + [lat_rows(HEADS_W), _rows(tm, HEADS_W), _rows(tm, HEADS_W)]
        + [lat_rows(2 * d), lat_rows(sw), _rows(tm, sw), _rows(tm, sw)],
        out_specs=[_rows(tm, lay['width'])] + [_full(s) for s in acc_shapes],
        out_shape=[jax.ShapeDtypeStruct((r, lay['width']), BF16)] + [jax.ShapeDtypeStruct(s, F32) for s in acc_shapes],
        compiler_params=_cparams(("arbitrary",)),
    )(proj, proj, proj, *gains, wuq, wk, wv, *tabs, dq, dk, dv, dgl, du_direct, du_f, du_b)


def glu_fwd(proj, lay, yf, yb, dskip, wglu, n, tm, name):
    sw, d = wglu.shape[0], wglu.shape[1] // 2

    def body(u_ref, yf_ref, yb_ref, ds_ref, w_ref, o_ref):
        o_ref[...] = _glu_fn(u_ref[...], yf_ref[...], yb_ref[...], ds_ref[...], w_ref[...])

    return pl.pallas_call(
        body, name=name, grid=(n // tm,),
        in_specs=[_rows(tm, sw, lay['o_u'] // sw), _rows(tm, sw), _rows(tm, sw), _full(dskip.shape),
                  _full(wglu.shape)],
        out_specs=_rows(tm, d), out_shape=jax.ShapeDtypeStruct((n, d), F32),
        compiler_params=_cparams(("parallel",)),
    )(proj, yf, yb, dskip, wglu)


def glu_bwd(proj, lay, yf, yb, dskip, wglu, ds_l, n, tm, name):
    sw, d = wglu.shape[0], wglu.shape[1] // 2

    def body(u_ref, yf_ref, yb_ref, ds_ref, w_ref, g_ref, du_ref, dy_ref, dds_ref, dw_ref):
        i = pl.program_id(0)
        _, vjp = jax.vjp(_glu_fn, u_ref[...], yf_ref[...], yb_ref[...], ds_ref[...], w_ref[...].astype(F32))
        du, dyf, _, dds, dw = vjp(g_ref[...])
        du_ref[...] = du
        dy_ref[...] = dyf
        _acc_add(i, dds_ref, _pad_rows(dds))
        _acc_add(i, dw_ref, dw)

    return pl.pallas_call(
        body, name=name, grid=(n // tm,),
        in_specs=[_rows(tm, sw, lay['o_u'] // sw), _rows(tm, sw), _rows(tm, sw), _full(dskip.shape),
                  _full(wglu.shape), _rows(tm, d)],
        out_specs=[_rows(tm, sw), _rows(tm, sw), _full((SUBLANES, sw)), _full(wglu.shape)],
        out_shape=[jax.ShapeDtypeStruct((n, sw), F32), jax.ShapeDtypeStruct((n, sw), F32),
                   jax.ShapeDtypeStruct((SUBLANES, sw), F32), jax.ShapeDtypeStruct(wglu.shape, F32)],
        compiler_params=_cparams(("arbitrary",)),
    )(proj, yf, yb, dskip, wglu, ds_l)


def merge_fwd(o, s_l, proj, xa, g1, wo, wout, n, tm, name):
    d = xa.shape[1]

    def body(o_ref, s_ref, gl_ref, x_ref, g1_ref, wo_ref, wout_ref, x1_ref):
        x1_ref[...] = _merge_fn(o_ref[...], s_ref[...], gl_ref[...], x_ref[...], g1_ref[...], wo_ref[...],
                                wout_ref[...])

    return pl.pallas_call(
        body, name=name, grid=(n // tm,),
        in_specs=[_rows(tm, HEADS_W), _rows(tm, d), _rows(tm, 2 * d), _rows(tm, d), _full(g1.shape),
                  _full(wo.shape), _full(wout.shape)],
        out_specs=_rows(tm, d), out_shape=jax.ShapeDtypeStruct((n, d), F32),
        compiler_params=_cparams(("parallel",)),
    )(o, s_l, proj, xa, g1, wo, wout)


def merge_bwd(o, s_l, proj, xa, g1, wo, wout, dx1, n, tm, name):
    d = xa.shape[1]

    def body(o_ref, s_ref, gl_ref, x_ref, g1_ref, wo_ref, wout_ref, dx1_ref, do_ref, ds_ref, dgl_ref, dg1_ref,
             dwo_ref, dwout_ref):
        i = pl.program_id(0)
        _, vjp = jax.vjp(_merge_fn, o_ref[...], s_ref[...], gl_ref[...], x_ref[...], g1_ref[...],
                         wo_ref[...].astype(F32), wout_ref[...].astype(F32))
        do, ds, dgl, _, dg1, dwo, dwout = vjp(dx1_ref[...])
        do_ref[...] = do
        ds_ref[...] = ds
        dgl_ref[...] = dgl
        _acc_add(i, dg1_ref, _pad_rows(dg1))
        _acc_add(i, dwo_ref, dwo)
        _acc_add(i, dwout_ref, dwout)

    return pl.pallas_call(
        body, name=name, grid=(n // tm,),
        in_specs=[_rows(tm, HEADS_W), _rows(tm, d), _rows(tm, 2 * d), _rows(tm, d), _full(g1.shape),
                  _full(wo.shape), _full(wout.shape), _rows(tm, d)],
        out_specs=[_rows(tm, HEADS_W), _rows(tm, d), _rows(tm, 2 * d), _full((SUBLANES, d)), _full(wo.shape),
                   _full(wout.shape)],
        out_shape=[jax.ShapeDtypeStruct((n, HEADS_W), BF16), jax.ShapeDtypeStruct((n, d), F32),
                   jax.ShapeDtypeStruct((n, 2 * d), F32), jax.ShapeDtypeStruct((SUBLANES, d), F32),
                   jax.ShapeDtypeStruct(wo.shape, F32), jax.ShapeDtypeStruct(wout.shape, F32)],
        compiler_params=_cparams(("arbitrary",)),
    )(o, s_l, proj, xa, g1, wo, wout, dx1)


def _halo_specs(tm, w, n):
    nb = n // SUBLANES
    per = tm // SUBLANES
    prev = pl.BlockSpec((SUBLANES, w), lambda i: (jnp.maximum(i * per - 1, 0), 0))
    nxt = pl.BlockSpec((SUBLANES, w), lambda i: (jnp.minimum((i + 1) * per, nb - 1), 0))
    return prev, nxt


def _shifted(t, prev_blk, next_blk, i, n_tiles):
    tm = t.shape[0]
    row = lax.broadcasted_iota(jnp.int32, t.shape, 0)
    prev_row = jnp.where(i > 0, prev_blk[SUBLANES - 1:SUBLANES, :], 0.0)
    next_row = jnp.where(i < n_tiles - 1, next_blk[0:1, :], 0.0)
    before = jnp.where(row == 0, prev_row, pltpu.roll(t, 1, 0))
    after = jnp.where(row == tm - 1, next_row, pltpu.roll(t, tm - 1, 0))
    return before, after


def _conv_u2(up, before, after, cw, cb):
    return before * cw[0:1, :] + up * cw[1:2, :] + after * cw[2:3, :] + cb


def conv_act_fwd(up, cw, cb, tm, name):
    n, w2 = up.shape
    f = w2 // 2
    n_tiles = n // tm
    prev, nxt = _halo_specs(tm, w2, n)

    def body(up_ref, prev_ref, next_ref, cw_ref, cb_ref, act_ref):
        i = pl.program_id(0)
        t = up_ref[...]
        before, after = _shifted(t, prev_ref[...], next_ref[...], i, n_tiles)
        u2 = _conv_u2(t, before, after, cw_ref[...], cb_ref[...])
        act_ref[...] = (jax.nn.silu(u2[:, f:]) * u2[:, :f]).astype(BF16)

    return pl.pallas_call(
        body, name=name, grid=(n_tiles,),
        in_specs=[_rows(tm, w2), prev, nxt, _full(cw.shape), _full(cb.shape)],
        out_specs=_rows(tm, f), out_shape=jax.ShapeDtypeStruct((n, f), BF16),
        compiler_params=_cparams(("parallel",)),
    )(up, up, up, cw, cb)


def down_loss(act, wdown, x1, g2, target, tm, name):
    n, d = x1.shape
    f = act.shape[1]

    def body(act_ref, w_ref, x1_ref, g2_ref, t_ref, dy_ref, ffn_ref, loss_ref):
        ffn = jnp.dot(act_ref[...], w_ref[...], preferred_element_type=F32)
        err = x1_ref[...] + g2_ref[...] * ffn - t_ref[...]
        ffn_ref[...] = ffn
        dy_ref[...] = err * (1.0 / d)
        part = 0.5 * jnp.sum(jnp.sum(err * err, axis=-1, keepdims=True) * (1.0 / d), axis=0, keepdims=True)
        loss_ref[0] = jnp.broadcast_to(part, (SUBLANES, LANES))

    return pl.pallas_call(
        body, name=name, grid=(n // tm,),
        in_specs=[_rows(tm, f), _full(wdown.shape), _rows(tm, d), _full(g2.shape), _rows(tm, d)],
        out_specs=[_rows(tm, d), _rows(tm, d), pl.BlockSpec((1, SUBLANES, LANES), lambda i: (i, 0, 0))],
        out_shape=[jax.ShapeDtypeStruct((n, d), F32), jax.ShapeDtypeStruct((n, d), F32),
                   jax.ShapeDtypeStruct((n // tm, SUBLANES, LANES), F32)],
        compiler_params=_cparams(("parallel",)),
    )(act, wdown, x1, g2, target)


FFN_BWD_PARTS = 2


def ffn_bwd(dy, ffn, up, cw, cb, wdown, g2, tm, name):
    n, d = dy.shape
    w2 = up.shape[1]
    f = w2 // 2
    fc = f // FFN_BWD_PARTS
    assert fc % LANES == 0
    n_tiles = n // tm
    ext = tm + 2 * SUBLANES
    inner = slice(SUBLANES, SUBLANES + tm)
    prev_w, next_w = _halo_specs(tm, w2, n)
    prev_d, next_d = _halo_specs(tm, d, n)

    def body(dy_ref, dyp_ref, dyn_ref, ffn_ref, up_ref, upp_ref, upn_ref, cw_ref, cb_ref, w_ref, g2_ref,
             dup_ref, dffn_ref, dg2_ref, dcw_ref):
        i = pl.program_id(0)
        has_prev, has_next = i > 0, i < n_tiles - 1

        def extended(prev, tile, nxt):
            return jnp.concatenate([jnp.where(has_prev, prev, 0.0), tile, jnp.where(has_next, nxt, 0.0)], axis=0)

        dyv = dy_ref[...]
        dffn = extended(dyp_ref[...], dyv, dyn_ref[...]) * g2_ref[...]
        dffn_ref[...] = dffn[inner].astype(BF16)
        dffn = dffn.astype(BF16)
        _acc_add(i, dg2_ref, _pad_rows(jnp.sum(dyv * ffn_ref[...], axis=0, keepdims=True)))
        row = lax.broadcasted_iota(jnp.int32, (SUBLANES, fc), 0)
        shift = lambda t: (pltpu.roll(t, 1, 0), pltpu.roll(t, ext - 1, 0))
        for part in range(FFN_BWD_PARTS):
            halves = []
            for col0 in (part * fc, f + part * fc):
                cols = slice(col0, col0 + fc)
                t = extended(upp_ref[:, cols], up_ref[:, cols], upn_ref[:, cols])
                before, after = shift(t)
                halves.append((cols, t, _conv_u2(t, before, after, cw_ref[:, cols], cb_ref[:, cols])))
            (_, _, val), (_, _, gate) = halves
            dact = _dot_nt(dffn, w_ref[part * fc:(part + 1) * fc, :])
            sg = jax.nn.sigmoid(gate)
            for (cols, t, _), du2 in zip(halves, (dact * (gate * sg),
                                                   dact * val * (sg * (1.0 + gate * (1.0 - sg))))):
                before, after = shift(du2)
                cwv = cw_ref[:, cols]
                dup_ref[:, cols] = (after * cwv[0:1, :] + du2 * cwv[1:2, :] + before * cwv[2:3, :])[inner].astype(BF16)
                upd = jnp.zeros((SUBLANES, fc), F32)
                for k, term in enumerate((after * t, du2 * t, before * t, du2)):
                    upd = upd + jnp.where(row == k, jnp.broadcast_to(
                        jnp.sum(term[inner], axis=0, keepdims=True), (SUBLANES, fc)), 0.0)

                @pl.when(i == 0)
                def _():
                    dcw_ref[:, cols] = jnp.zeros((SUBLANES, fc), F32)

                dcw_ref[:, cols] += upd

    return pl.pallas_call(
        body, name=name, grid=(n_tiles,),
        in_specs=[_rows(tm, d), prev_d, next_d, _rows(tm, d), _rows(tm, w2), prev_w, next_w, _full(cw.shape),
                  _full(cb.shape), _full(wdown.shape), _full(g2.shape)],
        out_specs=[_rows(tm, w2), _rows(tm, d), _full((SUBLANES, d)), _full((SUBLANES, w2))],
        out_shape=[jax.ShapeDtypeStruct((n, w2), BF16), jax.ShapeDtypeStruct((n, d), BF16),
                   jax.ShapeDtypeStruct((SUBLANES, d), F32), jax.ShapeDtypeStruct((SUBLANES, w2), F32)],
        compiler_params=_cparams(("arbitrary",)),
    )(dy, dy, dy, ffn, up, up, up, cw, cb, wdown, g2)


def attn_fwd(q, k, v, n, name, gather=()):
    nk = k.shape[0]
    tq = _tile(n, (512, 256, 128))
    tk = _tile(nk, (768, 384, 256, 128))
    n_kv = nk // tk
    n_q = n // tq
    n_g = len(gather)
    scale = QK_DIM ** -0.5
    c2 = scale * math.log2(math.e)

    def body(q_ref, k_ref, v_ref, *rest):
        o_ref, lse_ref = rest[n_g:n_g + 2]
        if n_g:
            start, finish = _two_level_gather(rest[:n_g], rest[n_g + 2:2 * n_g + 2], *rest[2 * n_g + 2:])
            step = pl.program_id(0) * n_q + pl.program_id(1)
            pl.when(step == 0)(start)
        qv = q_ref[...]
        ones_col = (lax.broadcasted_iota(jnp.int32, (tk, HEAD_PAD), 1) == V_DIM).astype(BF16)

        def chunk(j, carry):
            m, acc = carry
            rows = pl.ds(pl.multiple_of(j * tk, tk), tk)
            s = _dot_nt(qv, k_ref[rows, :])
            m_new = jnp.maximum(m, jnp.max(s, axis=-1, keepdims=True))
            p = jnp.exp2(s * c2 - m_new * c2)
            alpha = jnp.exp2((m - m_new) * c2)
            pv = jnp.dot(p.astype(BF16), v_ref[rows, :] + ones_col, preferred_element_type=F32)
            return m_new, alpha * acc + pv

        m, acc = lax.fori_loop(0, n_kv, chunk, (jnp.full((tq, 1), -jnp.inf, F32),
                                                jnp.zeros((tq, HEAD_PAD), F32)), unroll=True)
        l = acc[:, V_DIM:V_DIM + 1]
        lane = lax.broadcasted_iota(jnp.int32, (tq, HEAD_PAD), 1)
        o_ref[...] = jnp.where(lane < V_DIM, acc / l, 0.0).astype(BF16)
        lse_ref[...] = jnp.broadcast_to(m * scale + jnp.log(l), (tq, HEAD_PAD))
        if n_g:
            pl.when(step == N_HEADS * n_q - 1)(finish)

    qspec = pl.BlockSpec((tq, HEAD_PAD), lambda h, i: (i, h))
    kspec = pl.BlockSpec((nk, HEAD_PAD), lambda h, i: (0, h))
    any_spec = pl.BlockSpec(memory_space=pl.ANY)
    return pl.pallas_call(
        body, name=name, grid=(N_HEADS, n_q),
        in_specs=[qspec, kspec, kspec] + [any_spec] * n_g, out_specs=[qspec, qspec] + [any_spec] * n_g,
        out_shape=[jax.ShapeDtypeStruct((n, HEADS_W), BF16), jax.ShapeDtypeStruct((n, HEADS_W), F32)]
        + [jax.ShapeDtypeStruct((N_DEV,) + tuple(a.shape), a.dtype) for a in gather],
        scratch_shapes=_gather_scratch(n_g) if n_g else [],
        compiler_params=_cparams(("arbitrary", "arbitrary") if n_g else ("parallel", "parallel")),
    )(q, k, v, *gather)


def attn_bwd(q, k, v, o, do, lse, n, name, a2a=()):
    nk = k.shape[0]
    tq = _tile(n, (512, 256, 128))
    tk = _tile(nk, (768, 384, 256, 128))
    n_kv = nk // tk
    n_x = len(a2a)
    kinds = ['a2a'] * n_x
    scale = QK_DIM ** -0.5
    log2e = math.log2(math.e)
    c2 = scale * log2e

    def body(q_ref, k_ref, v_ref, o_ref, do_ref, lse_ref, *rest):
        dq_ref, dk_ref, dv_ref = rest[n_x:n_x + 3]
        if n_x:
            start, finish = _exchange_ops(rest[:n_x], rest[n_x + 3:2 * n_x + 3], *rest[2 * n_x + 3:], kinds, 'all')
            step = pl.program_id(0) * n_kv + pl.program_id(1)
            pl.when(step == 0)(start)

        @pl.when(pl.program_id(1) == 0)
        def _():
            dq_ref[...] = jnp.zeros_like(dq_ref)

        kv, vv = k_ref[...], v_ref[...]

        def q_tile(i, carry):
            dk, dv = carry
            rows = pl.ds(pl.multiple_of(i * tq, tq), tq)
            qv, dov = q_ref[rows, :], do_ref[rows, :]
            p = jnp.exp2(_dot_nt(qv, kv) * c2 - lse_ref[rows, 0:1] * log2e)
            dv = dv + _dot_tn(p.astype(BF16), dov)
            dp = _dot_nt(dov, vv)
            delta = jnp.sum(dov.astype(F32) * o_ref[rows, :].astype(F32), axis=-1, keepdims=True)
            ds = (p * (dp - delta) * scale).astype(BF16)
            dk = dk + _dot_tn(ds, qv)
            dq_ref[rows, :] += jnp.dot(ds, kv, preferred_element_type=F32)
            return dk, dv

        zero = jnp.zeros((tk, HEAD_PAD), F32)
        dk, dv = lax.fori_loop(0, n // tq, q_tile, (zero, zero), unroll=2)
        dk_ref[...] = dk
        dv_ref[...] = dv
        if n_x:
            pl.when(step == N_HEADS * n_kv - 1)(finish)

    qspec = pl.BlockSpec((n, HEAD_PAD), lambda h, j: (0, h))
    kspec = pl.BlockSpec((tk, HEAD_PAD), lambda h, j: (j, h))
    any_spec = pl.BlockSpec(memory_space=pl.ANY)
    return pl.pallas_call(
        body, name=name, grid=(N_HEADS, n_kv),
        in_specs=[qspec, kspec, kspec, qspec, qspec, qspec] + [any_spec] * n_x,
        out_specs=[qspec, kspec, kspec] + [any_spec] * n_x,
        out_shape=[jax.ShapeDtypeStruct((n, HEADS_W), F32), jax.ShapeDtypeStruct((nk, HEADS_W), F32),
                   jax.ShapeDtypeStruct((nk, HEADS_W), F32)] + _exchange_shapes(a2a, kinds, 'all'),
        scratch_shapes=_exchange_scratch(n_x, 'all') if n_x else [],
        compiler_params=_cparams(("arbitrary", "arbitrary") if n_x else ("parallel", "arbitrary")),
    )(q, k, v, o, do, lse, *a2a)


SCAN_LEVELS = (1, 2, 4)
SCAN_LANES = 512


def _scan_chunk(xr, xi, car, m_ref, p_ref, reverse):
    t_len, gn = xr.shape
    n_slab = t_len // SUBLANES
    for lb in range(gn // SCAN_LANES):
        ls = pl.ds(lb * SCAN_LANES, SCAN_LANES)

        def step(s, carry, ls=ls):
            cr, ci = carry
            slab = (n_slab - 1 - s) if reverse else s
            rows = pl.ds(pl.multiple_of(slab * SUBLANES, SUBLANES), SUBLANES)
            br, bi = xr[rows, ls], xi[rows, ls]
            for lvl, d in enumerate(SCAN_LEVELS):
                shift = SUBLANES - d if reverse else d
                sr, si = pltpu.roll(br, shift, 0), pltpu.roll(bi, shift, 0)
                mr, mi = m_ref[lvl, 0, :, ls], m_ref[lvl, 1, :, ls]
                br, bi = br + mr * sr - mi * si, bi + mr * si + mi * sr
            pr, pi = p_ref[0, :, ls], p_ref[1, :, ls]
            br, bi = br + pr * cr - pi * ci, bi + pr * ci + pi * cr
            xr[rows, ls] = br
            xi[rows, ls] = bi
            last = 0 if reverse else SUBLANES - 1
            return br[last:last + 1, :], bi[last:last + 1, :]

        cr, ci = lax.fori_loop(0, n_slab, step, (car[0:1, ls], car[1:2, ls]))
        car[0:1, ls] = cr
        car[1:2, ls] = ci


SSM_SPLIT = 2


def _bd_dot(a, w_ref):
    k, n = w_ref.shape[0] // SSM_SPLIT, w_ref.shape[1] // SSM_SPLIT
    return jnp.concatenate([jnp.dot(a[:, p * k:(p + 1) * k], w_ref[p * k:(p + 1) * k, p * n:(p + 1) * n],
                                    preferred_element_type=F32) for p in range(SSM_SPLIT)], axis=1)


def _bd_dot_nt(a, w_ref):
    k, n = w_ref.shape[0] // SSM_SPLIT, w_ref.shape[1] // SSM_SPLIT
    return jnp.concatenate([_dot_nt(a[:, p * n:(p + 1) * n], w_ref[p * k:(p + 1) * k, p * n:(p + 1) * n])
                            for p in range(SSM_SPLIT)], axis=1)


def _seq_block(step, n_chunk, lat_chunks, reverse):
    if reverse:
        return n_chunk - 1 - step
    return (step + lat_chunks) % n_chunk


def ssm_fwd(proj, lay, bre, bim, cre, ncim, tabs, n, t_len, reverse, name):
    l, sw = proj.shape[0], lay['sw']
    gn = bre.shape[-1]
    n_chunk, lat_chunks = l // t_len, n // t_len
    blk = lambda k: _seq_block(k, n_chunk, lat_chunks, reverse)
    mtab, ptab = tabs

    def body(u_ref, bre_ref, bim_ref, cre_ref, ncim_ref, m_ref, p_ref, y_ref, xb_ref, xr, xi, car):
        @pl.when(pl.program_id(0) == 0)
        def _():
            car[...] = jnp.zeros_like(car)

        xb_ref[...] = car[...]
        u = u_ref[...].astype(BF16)
        xr[...] = _bd_dot(u, bre_ref)
        xi[...] = _bd_dot(u, bim_ref)
        _scan_chunk(xr, xi, car, m_ref, p_ref, reverse)
        y_ref[...] = _bd_dot(xr[...].astype(BF16), cre_ref) + _bd_dot(xi[...].astype(BF16), ncim_ref)

    return pl.pallas_call(
        body, name=name, grid=(n_chunk,),
        in_specs=[pl.BlockSpec((t_len, sw), lambda k: (blk(k), lay['o_u'] // sw)), _full(bre.shape),
                  _full(bim.shape), _full(cre.shape), _full(ncim.shape), _full(mtab.shape), _full(ptab.shape)],
        out_specs=[pl.BlockSpec((t_len, sw), lambda k: (blk(k), 0)),
                   pl.BlockSpec((None, 2, gn), lambda k: (k, 0, 0))],
        out_shape=[jax.ShapeDtypeStruct((l, sw), F32), jax.ShapeDtypeStruct((n_chunk, 2, gn), F32)],
        scratch_shapes=[pltpu.VMEM((t_len, gn), F32), pltpu.VMEM((t_len, gn), F32), pltpu.VMEM((2, gn), F32)],
        compiler_params=_cparams(("arbitrary",)),
    )(proj, bre, bim, cre, ncim, mtab, ptab)


def ssm_bwd(proj, lay, dyr, xb, bre, bim, cre, ncim, tabs, adj_tabs, n, t_len, reverse, name):
    l, sw = proj.shape[0], lay['sw']
    gn = bre.shape[-1]
    n_chunk, lat_chunks = l // t_len, n // t_len
    fwd_step = lambda k: n_chunk - 1 - k
    blk = lambda k: _seq_block(fwd_step(k), n_chunk, lat_chunks, reverse)
    (mtab, ptab), (mtab_r, ptab_r) = tabs, adj_tabs

    def body(u_ref, dy_ref, xb_ref, bre_ref, bim_ref, cre_ref, ncim_ref, m_ref, p_ref, mr_ref, pr_ref,
             du_ref, gr_ref, gi_ref, xr_ref, xi_ref, dlam_ref, xr, xi, gr, gi, car, acar):
        k = pl.program_id(0)

        @pl.when(k == 0)
        def _():
            acar[...] = jnp.zeros_like(acar)
            dlam_ref[...] = jnp.zeros_like(dlam_ref)

        u = u_ref[...].astype(BF16)
        dy = jnp.where(blk(k) < lat_chunks, dy_ref[...], 0.0).astype(BF16)
        xr[...] = _bd_dot(u, bre_ref)
        xi[...] = _bd_dot(u, bim_ref)
        car[...] = xb_ref[...]
        _scan_chunk(xr, xi, car, m_ref, p_ref, reverse)
        gr[...] = _bd_dot_nt(dy, cre_ref)
        gi[...] = _bd_dot_nt(dy, ncim_ref)
        _scan_chunk(gr, gi, acar, mr_ref, pr_ref, not reverse)
        xrv, xiv, grv, giv = xr[...], xi[...], gr[...], gi[...]
        row = lax.broadcasted_iota(jnp.int32, (t_len, gn), 0)
        first, shift = (t_len - 1, t_len - 1) if reverse else (0, 1)
        xpr = jnp.where(row == first, xb_ref[0:1, :], pltpu.roll(xrv, shift, 0))
        xpi = jnp.where(row == first, xb_ref[1:2, :], pltpu.roll(xiv, shift, 0))
        dlr = grv * xpr + giv * xpi
        dli = giv * xpr - grv * xpi
        dlam_ref[0] += jnp.sum(dlr.reshape(t_len // SUBLANES, SUBLANES, gn), axis=0)
        dlam_ref[1] += jnp.sum(dli.reshape(t_len // SUBLANES, SUBLANES, gn), axis=0)
        grb, gib = grv.astype(BF16), giv.astype(BF16)
        du_ref[...] = _bd_dot_nt(grb, bre_ref) + _bd_dot_nt(gib, bim_ref)
        gr_ref[...] = grb
        gi_ref[...] = gib
        xr_ref[...] = xrv.astype(BF16)
        xi_ref[...] = xiv.astype(BF16)

    def at_blk(width, col=0):
        return pl.BlockSpec((t_len, width), lambda k: (blk(k), col))

    state = jax.ShapeDtypeStruct((l, gn), BF16)
    return pl.pallas_call(
        body, name=name, grid=(n_chunk,),
        in_specs=[at_blk(sw, lay['o_u'] // sw),
                  pl.BlockSpec((t_len, sw), lambda k: (jnp.minimum(blk(k), lat_chunks - 1), 0)),
                  pl.BlockSpec((None, 2, gn), lambda k: (fwd_step(k), 0, 0)),
                  _full(bre.shape), _full(bim.shape), _full(cre.shape), _full(ncim.shape), _full(mtab.shape),
                  _full(ptab.shape), _full(mtab_r.shape), _full(ptab_r.shape)],
        out_specs=[at_blk(sw), at_blk(gn), at_blk(gn), at_blk(gn), at_blk(gn), _full((2, SUBLANES, gn))],
        out_shape=[jax.ShapeDtypeStruct((l, sw), F32), state, state, state, state,
                   jax.ShapeDtypeStruct((2, SUBLANES, gn), F32)],
        scratch_shapes=[pltpu.VMEM((t_len, gn), F32)] * 4 + [pltpu.VMEM((2, gn), F32)] * 2,
        compiler_params=_cparams(("arbitrary",)),
    )(proj, dyr, xb, bre, bim, cre, ncim, mtab, ptab, mtab_r, ptab_r)


def ssm_prep(lam_re, lam_im, logdt, bre, bim, name):
    gn = lam_re.shape[1]

    def body(lr_ref, li_ref, dt_ref, br_ref, bi_ref, pwr_ref, pwi_ref, bbr_ref, bbi_ref):
        _, _, bbr, bbi = _ssm_prep_fn(lr_ref[...], li_ref[...], dt_ref[...], br_ref[...], bi_ref[...])
        bbr_ref[...] = bbr
        bbi_ref[...] = bbi
        kk = (lax.broadcasted_iota(jnp.int32, (SUBLANES, gn), 0) + 1).astype(F32)
        dt = jnp.exp(dt_ref[...])
        ar, ai = lr_ref[...] * dt * kk, li_ref[...] * dt * kk
        e = jnp.exp(ar)
        pwr_ref[...] = e * jnp.cos(ai)
        pwi_ref[...] = e * jnp.sin(ai)

    ins = (lam_re, lam_im, logdt, bre, bim)
    return pl.pallas_call(
        body, name=name,
        out_shape=[jax.ShapeDtypeStruct((SUBLANES, gn), F32)] * 2 + [jax.ShapeDtypeStruct(bre.shape, F32)] * 2,
        compiler_params=_cparams(),
    )(*ins)


def ssm_prep_bwd(lam_re, lam_im, logdt, bre, bim, dlbr, dlbi, dbbr, dbbi, name):
    def body(lr_ref, li_ref, dt_ref, br_ref, bi_ref, g0, g1, g2, g3, o0, o1, o2, o3, o4):
        _, vjp = jax.vjp(_ssm_prep_fn, lr_ref[...], li_ref[...], dt_ref[...], br_ref[...], bi_ref[...])
        for ref, val in zip((o0, o1, o2, o3, o4), vjp((g0[...], g1[...], g2[...], g3[...]))):
            ref[...] = val

    ins = (lam_re, lam_im, logdt, bre, bim)
    return pl.pallas_call(
        body, name=name,
        out_shape=[jax.ShapeDtypeStruct(a.shape, F32) for a in ins],
        compiler_params=_cparams(),
    )(*ins, dlbr, dlbi, dbbr, dbbi)


def mod_fwd(cmat, w, b, name):
    def body(c_ref, w_ref, b_ref, o_ref):
        o_ref[...] = _mod_fn(c_ref[...], w_ref[...]) + b_ref[...]

    return pl.pallas_call(body, name=name, out_shape=jax.ShapeDtypeStruct((cmat.shape[0], w.shape[1]), F32),
                          compiler_params=_cparams())(cmat, w, b)


def mod_bwd(cmat, w, g_lat, g_ctx, g_all, name):
    def body(c_ref, w_ref, gl_ref, gc_ref, ga_ref, dw_ref, dc_ref, db_ref):
        gc = jnp.sum(gc_ref[...], axis=0, keepdims=True)
        dm = jnp.concatenate([gl_ref[...], _pad_rows(gc)], axis=0)
        _, vjp = jax.vjp(_mod_fn, c_ref[...], w_ref[...])
        dc, dw = vjp(dm)
        dw_ref[...] = dw
        dc_ref[...] = dc
        db_ref[...] = _pad_rows(jnp.sum(ga_ref[...], axis=0, keepdims=True))

    return pl.pallas_call(
        body, name=name,
        out_shape=[jax.ShapeDtypeStruct(w.shape, F32), jax.ShapeDtypeStruct(cmat.shape, F32),
                   jax.ShapeDtypeStruct((SUBLANES, g_all.shape[1]), F32)],
        compiler_params=_cparams(),
    )(cmat, w, g_lat, g_ctx, g_all)


def add_own(own, idx, recv, out_dtype, name):
    _, s, r, c = own.shape

    def body(idx_ref, own_ref, recv_ref, o_ref):
        o_ref[...] = (own_ref[...] + recv_ref[...]).astype(o_ref.dtype)

    return pl.pallas_call(
        body, name=name,
        grid_spec=pltpu.PrefetchScalarGridSpec(
            num_scalar_prefetch=1, grid=(s,),
            in_specs=[pl.BlockSpec((None, None, r, c), lambda k, idx_ref: (idx_ref[0], k, 0, 0)),
                      pl.BlockSpec((None, r, c), lambda k, idx_ref: (k, 0, 0))],
            out_specs=pl.BlockSpec((None, r, c), lambda k, idx_ref: (k, 0, 0))),
        out_shape=jax.ShapeDtypeStruct((s, r, c), out_dtype),
        compiler_params=_cparams(("parallel",)),
    )(idx, own, recv)


def reduce_adamw(parts, w, m, v, name, own=None, idx=None):
    s, r, c = parts.shape
    tr = _tile(r, (256, 128, 64, 32, 16, 8))

    def body(*refs):
        if own is None:
            p_ref, w_ref, m_ref, v_ref, g_ref, d_ref, nm_ref, nv_ref = refs
            g = p_ref[0].astype(F32)
        else:
            _, own_ref, p_ref, w_ref, m_ref, v_ref, g_ref, d_ref, nm_ref, nv_ref = refs
            g = own_ref[...].astype(F32) + p_ref[0].astype(F32)
        for k in range(1, s):
            g = g + p_ref[k].astype(F32)
        mm = ADAM_B1 * m_ref[...] + (1.0 - ADAM_B1) * g
        vv = ADAM_B2 * v_ref[...] + (1.0 - ADAM_B2) * jnp.square(g)
        m_hat = mm / (1.0 - ADAM_B1 ** ADAM_STEP)
        v_hat = vv / (1.0 - ADAM_B2 ** ADAM_STEP)
        g_ref[...] = g
        d_ref[...] = -ADAM_LR * (m_hat / (jnp.sqrt(v_hat) + ADAM_EPS) + ADAM_WD * w_ref[...])
        nm_ref[...] = mm
        nv_ref[...] = vv

    out = jax.ShapeDtypeStruct((r, c), F32)
    if own is None:
        blk = _rows(tr, c)
        return pl.pallas_call(
            body, name=name, grid=(r // tr,),
            in_specs=[pl.BlockSpec((s, tr, c), lambda i: (0, i, 0)), blk, blk, blk],
            out_specs=[blk] * 4, out_shape=[out] * 4,
            compiler_params=_cparams(("parallel",)),
        )(parts, w, m, v)
    blk = pl.BlockSpec((tr, c), lambda i, idx_ref: (i, 0))
    return pl.pallas_call(
        body, name=name,
        grid_spec=pltpu.PrefetchScalarGridSpec(
            num_scalar_prefetch=1, grid=(r // tr,),
            in_specs=[pl.BlockSpec((None, tr, c), lambda i, idx_ref: (idx_ref[0], i, 0)),
                      pl.BlockSpec((s, tr, c), lambda i, idx_ref: (0, i, 0)), blk, blk, blk],
            out_specs=[blk] * 4),
        out_shape=[out] * 4,
        compiler_params=_cparams(("parallel",)),
    )(idx, own, parts, w, m, v)


def _in_layout(d, q, kv, sw):
    o_u = 2 * d
    o_ckv = o_u + sw
    o_kr = o_ckv + kv
    o_cq = -(-(o_kr + LANES) // q) * q
    assert o_u % sw == 0 and o_ckv % kv == 0 and o_kr % LANES == 0
    assert q % LANES == 0 and kv % LANES == 0 and sw % LANES == 0
    return dict(d=d, q=q, kv=kv, sw=sw, o_gl=0, o_u=o_u, o_ckv=o_ckv, o_kr=o_kr, o_cq=o_cq, width=o_cq + q)


def _pad_w_in(w_in, lay):
    q, kv, sw, d = lay['q'], lay['kv'], lay['sw'], lay['d']
    cq, ckv, kr, u, gl = jnp.split(w_in, [q, q + kv, q + kv + QK_ROPE, q + kv + QK_ROPE + sw], axis=1)
    z = lambda w: jnp.zeros((w_in.shape[0], w), w_in.dtype)
    hole = lay['o_cq'] - lay['o_kr'] - LANES
    return jnp.concatenate([gl, u, ckv, z(QK_NOPE), kr, z(LANES - QK_DIM), z(hole), cq], axis=1)


def _unpad_w_in(g, lay):
    q, kv, sw, d = lay['q'], lay['kv'], lay['sw'], lay['d']
    kr0 = lay['o_kr'] + QK_NOPE
    return jnp.concatenate([g[:, lay['o_cq']:lay['o_cq'] + q], g[:, lay['o_ckv']:lay['o_ckv'] + kv],
                            g[:, kr0:kr0 + QK_ROPE], g[:, lay['o_u']:lay['o_u'] + sw], g[:, :2 * d]], axis=1)


def _pad_heads(w, width):
    k = w.shape[0]
    return jnp.pad(w.reshape(k, N_HEADS, width), ((0, 0), (0, 0), (0, HEAD_PAD - width))).reshape(k, HEADS_W)


def _unpad_heads(w, width):
    k = w.shape[0]
    return w.reshape(k, N_HEADS, HEAD_PAD)[:, :, :width].reshape(k, N_HEADS * width)


def _rope_tables(n, nc):
    rows = n // GRID_W
    row = jnp.repeat(jnp.arange(rows), GRID_W)
    col = jnp.tile(jnp.arange(GRID_W), rows)
    pairs = QK_ROPE // 4
    freqs = ROPE_THETA ** (-jnp.arange(pairs, dtype=F32) / pairs)
    ang = jnp.concatenate([row[:, None] * freqs, col[:, None] * freqs], axis=-1)
    cos = jnp.concatenate([jnp.cos(ang), jnp.ones((nc, 2 * pairs), F32)], axis=0)
    sin = jnp.concatenate([jnp.sin(ang), jnp.zeros((nc, 2 * pairs), F32)], axis=0)
    l = n + nc
    half = QK_ROPE // 2
    ct = jnp.concatenate([jnp.ones((l, QK_NOPE), F32), cos, cos, jnp.zeros((l, HEAD_PAD - QK_DIM), F32)], axis=1)
    s1 = jnp.concatenate([jnp.zeros((l, QK_NOPE + half), F32), sin, jnp.zeros((l, HEAD_PAD - QK_DIM), F32)],
                         axis=1)
    s2 = jnp.concatenate([jnp.zeros((l, QK_NOPE), F32), -sin, jnp.zeros((l, HEAD_PAD - QK_NOPE - half), F32)],
                         axis=1)
    return ct, s1, s2


def _group_mask(rows, cols, g):
    return (jnp.arange(rows)[:, None] // (rows // g)) == (jnp.arange(cols)[None, :] // (cols // g))


def _block_diag_rows(m, g):
    return jnp.where(_group_mask(g * m.shape[0], m.shape[1], g), jnp.tile(m, (g, 1)), 0)


def _block_diag_cols(m, g):
    return jnp.where(_group_mask(m.shape[0], g * m.shape[1], g), jnp.tile(m, (1, g)), 0)


def _diag_blocks(m, g):
    a, b = m.shape[0] // g, m.shape[1] // g
    masked = jnp.where(_group_mask(m.shape[0], m.shape[1], g), m, 0)
    return masked.reshape(m.shape[0], g, b).sum(axis=1).reshape(g, a, b)


def _scan_tables(pwr, pwi, reverse):
    row = jnp.arange(SUBLANES)[:, None]
    zero = jnp.zeros_like(pwr)
    levels = []
    for d in SCAN_LEVELS:
        keep = (row < SUBLANES - d) if reverse else (row >= d)
        levels.append(jnp.stack([jnp.where(keep, pwr[d - 1:d, :], zero), jnp.where(keep, pwi[d - 1:d, :], zero)]))
    carry = jnp.stack([pwr[::-1], pwi[::-1]]) if reverse else jnp.stack([pwr, pwi])
    return jnp.stack(levels), carry


def _step(inp):
    x, c, ctx = inp['x'][0], inp['c'], inp['ctx'][0]
    target = inp['loss_target'][0]
    n, d = x.shape
    nc = ctx.shape[0]
    l = n + nc
    q_w, kv_w = inp['q_a_g'].shape[1], inp['kv_a_g'].shape[1]
    sw = inp['d_skip'].shape[1]
    n_grp = sw // SSM_GROUP
    gn = n_grp * SSM_STATE
    lay = _in_layout(d, q_w, kv_w, sw)
    tm = _tile(math.gcd(n, nc), (256, 128))
    me = 4 * lax.axis_index("x") + 2 * lax.axis_index("y") + lax.axis_index("c")
    strip = lambda a: a if a.ndim <= 2 else a[0]
    w = {k: strip(inp[k]) for k in WEIGHT_NAMES}

    gathered_names = list(GATHERED)
    early_names = ['w_in', 'w_uq', 'w_ukv', 'w_o_attn']
    late_names = [k for k in gathered_names if k not in early_names]
    c_all, *wg = gather_two_level([jnp.broadcast_to(c, (SUBLANES, d))] + [w[k].astype(BF16) for k in early_names],
                                  "gather_weights")
    full = {k: _from_shards(s, GATHERED[k]) for k, s in zip(early_names, wg)}

    w_in_p = _pad_w_in(full['w_in'], lay)
    wuq_p = _pad_heads(full['w_uq'], QK_DIM)
    ukv = full['w_ukv'].reshape(kv_w, N_HEADS, QK_NOPE + V_DIM)
    wk_p = _pad_heads(ukv[:, :, :QK_NOPE].reshape(kv_w, -1), QK_NOPE)
    wv_p = _pad_heads(ukv[:, :, QK_NOPE:].reshape(kv_w, -1), V_DIM)
    wo_p = _pad_heads(full['w_o_attn'].T, V_DIM).T
    conv_b = w['conv_b']

    cmat = jnp.concatenate([c_all[:, 0, :], w['c_ctx'][None, :], jnp.zeros((SUBLANES - 1, d), F32)], axis=0)
    mcols = w['w_mod'].shape[1]
    b_cols = lax.dynamic_slice(w['b_mod'], (0, me * mcols), (1, mcols))
    mod_part = mod_fwd(cmat, w['w_mod'], b_cols, "mod_fwd")
    (mod_all,) = exchange([mod_part], ['gather'], "gather_mod")
    mod_me = lax.dynamic_index_in_dim(mod_all, me, axis=1, keepdims=False).reshape(6, d)
    mod_ctx = mod_all[:, SUBLANES, :].reshape(6, d)
    sh1, sc1, g1, sh2, sc2, g2 = [mod_me[k:k + 1] for k in range(6)]
    mods1 = jnp.concatenate([sc1, sh1, mod_ctx[1:2], mod_ctx[0:1], jnp.zeros((4, d), F32)], axis=0)
    mods2 = jnp.concatenate([sc2, sh2, jnp.zeros((6, d), F32)], axis=0)

    xa = jnp.concatenate([x, ctx], axis=0)
    h = norm_mod_fwd(xa, w['norm1_g'], mods1, n, tm, "norm1_fwd")
    proj = matmul(h, w_in_p, 'nn', F32, "in_proj")
    tabs = _rope_tables(n, nc)
    pad_g = lambda g: jnp.pad(g, ((0, 0), (0, HEAD_PAD - QK_DIM)))
    gains = (w['q_a_g'], w['kv_a_g'], pad_g(w['q_norm_g']), pad_g(w['k_norm_g']))
    q, k, v = qkv_fwd(proj, lay, gains, wuq_p, wk_p, wv_p, tabs, tm, "qkv_fwd")
    o, lse, *wg = attn_fwd(q, k, v, n, "attn_fwd", gather=[w[k].astype(BF16) for k in late_names])
    full.update({k: _from_shards(s, GATHERED[k]) for k, s in zip(late_names, wg)})
    w_glu, w_out, w_up, w_down = full['w_glu'], full['w_out'], full['w_up'], full['w_down']
    conv_w = jnp.pad(full['conv_w'].astype(F32), ((0, SUBLANES - 3), (0, 0)))

    b_t = lambda a: a.transpose(2, 0, 1).reshape(SSM_GROUP, gn)
    c_t = lambda a: a.transpose(1, 0, 2).reshape(SSM_GROUP, gn)
    bre_t, bim_t = b_t(w['b_re']), b_t(w['b_im'])
    ssm_in, prep = [], []
    for sfx in ('f', 'b'):
        lam_re, lam_im = w['lam_re_' + sfx].reshape(1, gn), w['lam_im_' + sfx].reshape(1, gn)
        logdt = jnp.repeat(w['log_dt_' + sfx], SSM_STATE, axis=1)
        ssm_in.append((lam_re, lam_im, logdt))
        prep.append(ssm_prep(lam_re, lam_im, logdt, bre_t, bim_t, "ssm_prep_" + sfx))

    def blk_b(bb):
        return _block_diag_rows(bb, n_grp).astype(BF16)

    def blk_c(cc):
        return _block_diag_cols(cc.transpose(0, 2, 1).reshape(gn, SSM_GROUP), n_grp).astype(BF16)

    t_len = _tile(math.gcd(n, nc), (256, 128))
    ssm = []
    for di, sfx in enumerate(('f', 'b')):
        reverse = di == 1
        pwr, pwi, bbr, bbi = prep[di]
        ssm.append(dict(
            sfx=sfx, reverse=reverse, blocks=(blk_b(bbr), blk_b(bbi), blk_c(w['c_re_' + sfx]),
                                              blk_c(-w['c_im_' + sfx])),
            tabs=_scan_tables(pwr, pwi, reverse), adj_tabs=_scan_tables(pwr, -pwi, not reverse)))
    for s in ssm:
        s['y'], s['xb'] = ssm_fwd(proj, lay, *s['blocks'], s['tabs'], n, t_len, s['reverse'], "ssm_fwd_" + s['sfx'])
    yf, yb = ssm[0]['y'], ssm[1]['y']
    s_l = glu_fwd(proj, lay, yf, yb, w['d_skip'], w_glu, n, tm, "glu_fwd")
    x1 = merge_fwd(o, s_l, proj, xa, g1, wo_p, w_out, n, tm, "merge_fwd")

    h2 = norm_mod_fwd(x1, w['norm2_g'], mods2, n, tm, "norm2_fwd")
    up = matmul(h2, w_up, 'nn', F32, "up_proj")
    tw = _tile(n, (128,))
    act = conv_act_fwd(up, conv_w, conv_b, tw, "conv_act_fwd")
    dy, ffn, loss_parts = down_loss(act, w_down, x1, g2, target, tm, "down_loss")
    loss = lax.psum(jnp.sum(loss_parts[:, 0, 0]), MESH_AXES)

    dup, dffn, dg2, dconv = ffn_bwd(dy, ffn, up, conv_w, conv_b, w_down, g2, tm, "ffn_bwd")
    g_w_down = matmul(act, dffn, 'tn', F32, "dw_down")
    dh2 = matmul(dup, w_up, 'nt', F32, "dh2")
    g_w_up = matmul(h2, dup, 'tn', F32, "dw_up")
    dx1, dn2g, dmods2 = norm_mod_bwd(x1, dh2, dy, w['norm2_g'], mods2, n, tm, "norm2_bwd")

    do, ds_l, dgl, dg1, g_wo_p, g_w_out = merge_bwd(o, s_l, proj, xa, g1, wo_p, w_out, dx1, n, tm, "merge_bwd")
    du_direct, dyr, dds, g_w_glu = glu_bwd(proj, lay, yf, yb, w['d_skip'], w_glu, ds_l, n, tm, "glu_bwd")
    for s in ssm:
        s['du'], s['gr'], s['gi'], s['xr'], s['xi'], s['dlam'] = ssm_bwd(
            proj, lay, dyr, s['xb'], *s['blocks'], s['tabs'], s['adj_tabs'], n, t_len, s['reverse'],
            "ssm_bwd_" + s['sfx'])

    early_g = {'w_o_attn': _unpad_heads(g_wo_p.T, V_DIM).T, 'w_glu': g_w_glu, 'w_out': g_w_out, 'w_up': g_w_up,
               'conv_w': dconv[0:3], 'w_down': g_w_down}
    dq, dk, dv, *early_recv = attn_bwd(q, k, v, o, do, lse, n, "attn_bwd",
                                       a2a=[_to_shards(g, GATHERED[k]).astype(BF16) for k, g in early_g.items()])
    (dproj, dqag, dkvag, dqng, dkng, g_wuq_p, g_wk_p, g_wv_p) = qkv_bwd(
        proj, lay, gains, wuq_p, wk_p, wv_p, tabs, dq, dk, dv, dgl, du_direct, ssm[0]['du'], ssm[1]['du'], n, tm,
        "qkv_bwd")
    dh = matmul(dproj, w_in_p, 'nt', F32, "dh")
    g_w_in_p = matmul(h, dproj, 'tn', F32, "dw_in")
    dxa, dn1g, dmods1 = norm_mod_bwd(xa, dh, dx1, w['norm1_g'], mods1, n, tm, "norm1_bwd")
    grad_x = dxa[:n]

    grads = {}
    d_bbar = [None, None]
    ka, kb, gp = sw // SSM_SPLIT, gn // SSM_SPLIT, n_grp // SSM_SPLIT
    products = []
    for s in ssm:
        for nm, a, a_col0, b in (("ssm_db_re", proj, lay['o_u'], s['gr']), ("ssm_db_im", proj, lay['o_u'], s['gi']),
                                 ("ssm_dc_re", dyr, 0, s['xr']), ("ssm_dc_im", dyr, 0, s['xi'])):
            products += [matmul(a, b, 'tn', F32, "%s_%s%d" % (nm, s['sfx'], p), a_col0=a_col0 + p * ka, a_cols=ka,
                                b_col0=p * kb, b_cols=kb) for p in range(SSM_SPLIT)]
    blocks = [_diag_blocks(m, gp) for m in products]
    for di, s in enumerate(ssm):
        sfx = s['sfx']
        g_bre, g_bim, g_cre, g_cim = [
            jnp.concatenate(blocks[(4 * di + k) * SSM_SPLIT:(4 * di + k + 1) * SSM_SPLIT], axis=0) for k in range(4)]
        grads['c_re_' + sfx] = g_cre
        grads['c_im_' + sfx] = -g_cim
        to_t = lambda a: a.transpose(1, 0, 2).reshape(SSM_GROUP, gn)
        dlam_re = jnp.sum(s['dlam'][0], axis=0, keepdims=True)
        dlam_im = jnp.sum(s['dlam'][1], axis=0, keepdims=True)
        lam_re, lam_im, logdt = ssm_in[di]
        g_lr, g_li, g_dt, g_br, g_bi = ssm_prep_bwd(lam_re, lam_im, logdt, bre_t, bim_t, dlam_re, dlam_im,
                                                    to_t(g_bre), to_t(g_bim), "ssm_prep_bwd_" + sfx)
        grads['lam_re_' + sfx] = g_lr.reshape(n_grp, SSM_STATE)
        grads['lam_im_' + sfx] = g_li.reshape(n_grp, SSM_STATE)
        grads['log_dt_' + sfx] = jnp.sum(g_dt.reshape(n_grp, SSM_STATE), axis=1)[None, :]
        d_bbar[di] = (g_br, g_bi)
    from_t = lambda a: a.reshape(SSM_GROUP, n_grp, SSM_STATE).transpose(1, 2, 0)
    grads['b_re'] = from_t(d_bbar[0][0]) + from_t(d_bbar[1][0])
    grads['b_im'] = from_t(d_bbar[0][1]) + from_t(d_bbar[1][1])

    dmod = jnp.concatenate([dmods1[1:2], dmods1[0:1], dg1[0:1], dmods2[1:2], dmods2[0:1], dg2[0:1]], axis=1)
    dmod_ctx = jnp.concatenate([dmods1[3:4], dmods1[2:3], jnp.zeros((1, 4 * d), F32)], axis=1)
    dm_send = jnp.concatenate([dmod, dmod_ctx, jnp.zeros((SUBLANES - 2, 6 * d), F32)], axis=0)
    (dm_all,) = exchange([dm_send], ['gather'], "gather_dmod")
    g_all = jnp.concatenate([dm_all[:, 0, :], dm_all[:, 1, :]], axis=0)
    cols = lax.dynamic_slice(g_all.reshape(2 * N_DEV, N_DEV, mcols), (0, me, 0), (2 * N_DEV, 1, mcols))[:, 0, :]
    g_w_mod, dcmat, g_b_mod = mod_bwd(cmat, w['w_mod'], cols[:N_DEV], cols[N_DEV:], g_all, "mod_bwd")

    ukv_g = jnp.concatenate([_unpad_heads(g_wk_p, QK_NOPE).reshape(kv_w, N_HEADS, QK_NOPE),
                             _unpad_heads(g_wv_p, V_DIM).reshape(kv_w, N_HEADS, V_DIM)], axis=2)
    late_g = {'w_in': _unpad_w_in(g_w_in_p, lay), 'w_uq': _unpad_heads(g_wuq_p, QK_DIM),
              'w_ukv': ukv_g.reshape(kv_w, -1)}

    def by_core(full_grad, axis):
        sh = _to_shards(full_grad, axis)
        return sh.reshape((N_CHIPS, 2) + sh.shape[1:]).swapaxes(0, 1)

    sends = [by_core(g, GATHERED[k]) for k, g in late_g.items()]
    grads.update({'c_ctx': dcmat[SUBLANES], 'b_mod': g_b_mod[0:1], 'norm1_g': dn1g[0:1], 'norm2_g': dn2g[0:1],
                  'q_a_g': dqag[0:1], 'kv_a_g': dkvag[0:1], 'q_norm_g': dqng[0:1, :QK_DIM],
                  'k_norm_g': dkng[0:1, :QK_DIM], 'd_skip': dds[0:1], 'conv_b': dconv[3:4]})
    first = (me == 0).astype(F32)
    rep_parts = [grads[k] * first if k == 'b_mod' else grads[k] for k in REPLICATED]
    rpack, rspans = _pack(rep_parts, SUBLANES)
    my_core = lax.axis_index("c").astype(jnp.int32).reshape(1)
    my_chip = (2 * lax.axis_index("x") + lax.axis_index("y")).astype(jnp.int32).reshape(1)
    r_both = jnp.broadcast_to(rpack[None, None], (2, 1) + rpack.shape)
    from_sibling = exchange(sends + [r_both], ['others'] * (len(sends) + 1), "grads_in_chip", group='core')
    chip_sums = [add_own(s, my_core, got[0], BF16, "chip_sum_" + k) for s, got, k in zip(sends, from_sibling, late_g)]
    r_sum = add_own(r_both, my_core, from_sibling[-1][0], F32, "chip_sum_replicated")[0]
    *g_recv, r_recv = exchange(chip_sums + [r_sum], ['others'] * len(sends) + ['gather'], "grads_between_chips",
                               group='chips')

    outs = {}

    def update(parts, k, w_k, m_k, v_k, **own):
        res = reduce_adamw(parts, w_k, m_k, v_k, "adamw_" + k, **own)
        return dict(zip(('grad_', 'delta_', 'new_m_', 'new_v_'), res))

    per_tensor = ([(k, parts, own) for k, parts, own in zip(late_g, g_recv, chip_sums)]
                  + [(k, parts, None) for k, parts in zip(early_g, early_recv)] + [('w_mod', g_w_mod[None], None)])
    for k, parts, own in per_tensor:
        own_args = {} if own is None else dict(own=own, idx=my_chip)
        for kind, a in update(parts, k, w[k], strip(inp['m_' + k]), strip(inp['v_' + k]), **own_args).items():
            outs[kind + k] = a[None]
    rep = lambda prefix: _pack([strip(inp[prefix + k]) for k in REPLICATED], SUBLANES)[0]
    for kind, buf in update(r_recv, "replicated", rep(''), rep('m_'), rep('v_')).items():
        for k, a in zip(REPLICATED, _unpack(buf, rspans, [w[k].shape for k in REPLICATED])):
            outs[kind + k] = a if inp[k].ndim <= 2 else a[None]
    result = [loss, grad_x[None]]
    for kind in ('grad_', 'delta_', 'new_m_', 'new_v_'):
        result += [outs[kind + k] for k in WEIGHT_NAMES]
    return tuple(result)


_ARG_NAMES = (['x', 'c', 'ctx'] + WEIGHT_NAMES + ['loss_target'] + ['m_' + k for k in WEIGHT_NAMES]
              + ['v_' + k for k in WEIGHT_NAMES])


def kernel(*args):
    assert len(args) == len(_ARG_NAMES)
    return _step(dict(zip(_ARG_NAMES, args)))
```

```python
import functools
import math

import jax
import jax.numpy as jnp
from jax import lax
from jax.experimental import pallas as pl
from jax.experimental.pallas import tpu as pltpu

F32 = jnp.float32
BF16 = jnp.bfloat16

N_DEV = 8
MESH_AXES = ("x", "y", "c")
N_HEADS = 8
QK_NOPE = 64
QK_ROPE = 32
QK_DIM = QK_NOPE + QK_ROPE
V_DIM = 64
HEAD_PAD = 128
HEADS_W = N_HEADS * HEAD_PAD
GRID_W = 64
ROPE_THETA = 10000.0
SSM_GROUP = 16
SSM_STATE = 64
EPS = 1e-6
LANES = 128
SUBLANES = 8
PACK_W = 1024
VMEM_LIMIT = 56 * 1024 * 1024
MM_TILES = (1024, 768, 1408, 512, 384, 256, 128)

ADAM_LR = 0.001
ADAM_B1 = 0.9
ADAM_B2 = 0.999
ADAM_EPS = 1e-08
ADAM_WD = 0.01
ADAM_STEP = 10

WEIGHT_NAMES = ['c_ctx', 'w_mod', 'b_mod', 'norm1_g', 'norm2_g', 'w_in', 'q_a_g', 'w_uq', 'kv_a_g', 'w_ukv',
                'q_norm_g', 'k_norm_g', 'w_o_attn', 'lam_re_f', 'lam_im_f', 'log_dt_f', 'c_re_f', 'c_im_f',
                'lam_re_b', 'lam_im_b', 'log_dt_b', 'c_re_b', 'c_im_b', 'b_re', 'b_im', 'd_skip', 'w_glu',
                'w_out', 'w_up', 'conv_w', 'conv_b', 'w_down']
GATHERED = {'w_in': 1, 'w_uq': 1, 'w_ukv': 1, 'w_o_attn': 1, 'w_glu': 1, 'w_out': 0, 'w_up': 1, 'conv_w': 1,
            'w_down': 0}
REPLICATED = [n for n in WEIGHT_NAMES if n not in GATHERED and n != 'w_mod']


def _tile(n, prefs):
    for t in prefs:
        if n % t == 0:
            return t
    return n


def _cparams(sem=None):
    return pltpu.CompilerParams(dimension_semantics=sem, vmem_limit_bytes=VMEM_LIMIT)


@jax.custom_vjp
def bdot(a, w):
    return jnp.dot(a.astype(BF16), w.astype(BF16), preferred_element_type=F32)


def _bdot_fwd(a, w):
    return bdot(a, w), (a, w)


def _bdot_bwd(res, g):
    a, w = res
    gb = g.astype(BF16)
    da = lax.dot_general(gb, w.astype(BF16), (((1,), (1,)), ((), ())), preferred_element_type=F32)
    dw = lax.dot_general(a.astype(BF16), gb, (((0,), (0,)), ((), ())), preferred_element_type=F32)
    return da.astype(a.dtype), dw.astype(w.dtype)


bdot.defvjp(_bdot_fwd, _bdot_bwd)


def _dot_nt(a, b):
    return lax.dot_general(a, b, (((1,), (1,)), ((), ())), preferred_element_type=F32)


def _dot_tn(a, b):
    return lax.dot_general(a, b, (((0,), (0,)), ((), ())), preferred_element_type=F32)


def matmul(a, b, mode, out_dtype, name, a_col0=0, a_cols=None, b_col0=0, b_cols=None, a2a=()):
    n_x = len(a2a)
    kinds = ['a2a'] * n_x
    if mode == 'nn':
        (m, k), n = a.shape, b.shape[1]
    elif mode == 'nt':
        (m, k), n = a.shape, b.shape[0]
    else:
        (k, m), n = a.shape, b.shape[1]
        m, n = a_cols or m, b_cols or n
    tm = _tile(m, MM_TILES)
    tn = _tile(n, MM_TILES)
    tk = _tile(k, MM_TILES)
    nk = k // tk
    assert a_col0 % tm == 0 and b_col0 % tn == 0
    col0, bcol0 = a_col0 // tm, b_col0 // tn

    def body(a_ref, b_ref, *rest):
        o_ref, acc_ref = rest[n_x], rest[2 * n_x + 1]
        kk = pl.program_id(2)
        if n_x:
            start, finish = _exchange_ops(rest[:n_x], rest[n_x + 1:2 * n_x + 1], *rest[2 * n_x + 2:], kinds, 'all')
            step = (pl.program_id(0) * (n // tn) + pl.program_id(1)) * nk + kk
            pl.when(step == 0)(start)

        @pl.when(kk == 0)
        def _():
            acc_ref[...] = jnp.zeros_like(acc_ref)

        av, bv = a_ref[...].astype(BF16), b_ref[...].astype(BF16)
        if mode == 'nn':
            acc_ref[...] += jnp.dot(av, bv, preferred_element_type=F32)
        elif mode == 'nt':
            acc_ref[...] += _dot_nt(av, bv)
        else:
            acc_ref[...] += _dot_tn(av, bv)

        @pl.when(kk == nk - 1)
        def _():
            o_ref[...] = acc_ref[...].astype(o_ref.dtype)

        if n_x:
            pl.when(step == (m // tm) * (n // tn) * nk - 1)(finish)

    if mode == 'nn':
        a_spec = pl.BlockSpec((tm, tk), lambda i, j, kk: (i, kk))
        b_spec = pl.BlockSpec((tk, tn), lambda i, j, kk: (kk, j))
    elif mode == 'nt':
        a_spec = pl.BlockSpec((tm, tk), lambda i, j, kk: (i, kk))
        b_spec = pl.BlockSpec((tn, tk), lambda i, j, kk: (j, kk))
    else:
        a_spec = pl.BlockSpec((tk, tm), lambda i, j, kk: (kk, i + col0))
        b_spec = pl.BlockSpec((tk, tn), lambda i, j, kk: (kk, j + bcol0))
    any_spec = pl.BlockSpec(memory_space=pl.ANY)
    res = pl.pallas_call(
        body, name=name, grid=(m // tm, n // tn, nk),
        in_specs=[a_spec, b_spec] + [any_spec] * n_x,
        out_specs=[pl.BlockSpec((tm, tn), lambda i, j, kk: (i, j))] + [any_spec] * n_x,
        out_shape=[jax.ShapeDtypeStruct((m, n), out_dtype)] + (_exchange_shapes(a2a, kinds, 'all') if n_x else []),
        scratch_shapes=[pltpu.VMEM((tm, tn), F32)] + (_exchange_scratch(n_x, 'all') if n_x else []),
        compiler_params=_cparams(("arbitrary",) * 3 if n_x else ("parallel", "parallel", "arbitrary")),
    )(a, b, *a2a)
    return res if n_x else res[0]


N_CHIPS = 4


def _group(group):
    x, y, c = lax.axis_index("x"), lax.axis_index("y"), lax.axis_index("c")
    flips = {'all': [(r & 4, r & 2, r & 1) for r in range(1, 8)],
             'chips': [(0, 1, 0), (1, 0, 0), (1, 1, 0)], 'core': [(0, 0, 1)]}[group]
    index = {'all': lambda px, py, pc: 4 * px + 2 * py + pc, 'chips': lambda px, py, pc: 2 * px + py,
             'core': lambda px, py, pc: pc}[group]
    peers = []
    for fx, fy, fc in flips:
        p = (1 - x if fx else x, 1 - y if fy else y, 1 - c if fc else c)
        peers.append((p, index(*p)))
    return len(flips) + 1, index(x, y, c), peers


def exchange(arrays, kinds, name, group='all'):
    n_arr = len(arrays)

    def body(*refs):
        start, finish = _exchange_ops(refs[:n_arr], refs[n_arr:2 * n_arr], *refs[2 * n_arr:], kinds, group)
        start()
        finish()

    any_spec = pl.BlockSpec(memory_space=pl.ANY)
    return pl.pallas_call(
        body, name=name,
        in_specs=[any_spec] * n_arr, out_specs=[any_spec] * n_arr, out_shape=_exchange_shapes(arrays, kinds, group),
        scratch_shapes=_exchange_scratch(n_arr, group),
        compiler_params=pltpu.CompilerParams(has_side_effects=True),
    )(*arrays)


GROUP_SIZE = {'all': N_DEV, 'chips': N_CHIPS, 'core': 2}


def _exchange_shapes(arrays, kinds, group):
    size = GROUP_SIZE[group]
    return [jax.ShapeDtypeStruct({'gather': (size,) + tuple(arr.shape), 'a2a': tuple(arr.shape),
                                  'others': (size - 1,) + tuple(arr.shape[1:])}[kind], arr.dtype)
            for arr, kind in zip(arrays, kinds)]


def _exchange_scratch(n_arr, group):
    size = GROUP_SIZE[group]
    return [pltpu.SemaphoreType.DMA((n_arr * size,)), pltpu.SemaphoreType.DMA((n_arr * size,)),
            pltpu.SemaphoreType.DMA((n_arr,))]


def _exchange_ops(srcs, dsts, send_sems, recv_sems, local_sems, kinds, group):
    n_arr = len(srcs)
    size, me, peers = _group(group)

    def copy(a, r, peer, peer_idx, receiving):
        src = srcs[a] if kinds[a] == 'gather' else srcs[a].at[peer_idx]
        if kinds[a] == 'others':
            dst = dsts[a].at[r]
        else:
            dst = dsts[a].at[peer_idx if receiving else me]
        return pltpu.make_async_remote_copy(
            src_ref=src, dst_ref=dst, send_sem=send_sems.at[a * size + r], recv_sem=recv_sems.at[a * size + r],
            device_id=peer, device_id_type=pl.DeviceIdType.MESH)

    def local(a):
        mine = srcs[a] if kinds[a] == 'gather' else srcs[a].at[me]
        return pltpu.make_async_copy(mine, dsts[a].at[me], local_sems.at[a])

    def start():
        for a in range(n_arr):
            if kinds[a] != 'others':
                local(a).start()
            for r, (peer, peer_idx) in enumerate(peers):
                copy(a, r, peer, peer_idx, False).start()

    def finish():
        for a in range(n_arr):
            for r, (peer, peer_idx) in enumerate(peers):
                cp = copy(a, r, peer, peer_idx, True)
                cp.wait_send()
                cp.wait_recv()
            if kinds[a] != 'others':
                local(a).wait()

    return start, finish


GATHER_COPIES = 7


def _two_level_gather(srcs, outs, send_sems, recv_sems, local_sems):
    n_arr = len(srcs)
    x, y, c = lax.axis_index("x"), lax.axis_index("y"), lax.axis_index("c")
    me, sibling = (x, y, c), (x, y, 1 - c)
    chips = [(1 - x, y), (x, 1 - y), (1 - x, 1 - y)]
    slot = lambda p: 4 * p[0] + 2 * p[1] + p[2]

    def copy(a, k, block, to, own=False):
        dst = outs[a].at[slot(block)]
        return pltpu.make_async_remote_copy(
            src_ref=srcs[a] if own else dst, dst_ref=dst,
            send_sem=send_sems.at[a * GATHER_COPIES + k], recv_sem=recv_sems.at[a * GATHER_COPIES + k],
            device_id=to, device_id_type=pl.DeviceIdType.MESH)

    def own_copies(a):
        return [copy(a, 0, me, sibling, own=True)] + [copy(a, 1 + j, me, (*chip, c), own=True)
                                                      for j, chip in enumerate(chips)]

    local = lambda a: pltpu.make_async_copy(srcs[a], outs[a].at[slot(me)], local_sems.at[a])

    def start():
        for a in range(n_arr):
            local(a).start()
            for cp in own_copies(a):
                cp.start()

    def finish():
        passed = []
        for j, chip in enumerate(chips):
            for a in range(n_arr):
                copy(a, 1 + j, (*chip, c), me).wait_recv()
                passed.append(copy(a, 4 + j, (*chip, c), sibling))
                passed[-1].start()
        for a in range(n_arr):
            copy(a, 0, sibling, me).wait_recv()
            for j, chip in enumerate(chips):
                copy(a, 4 + j, (*chip, 1 - c), me).wait_recv()
            for cp in own_copies(a):
                cp.wait_send()
        for cp in passed:
            cp.wait_send()
        for a in range(n_arr):
            local(a).wait()

    return start, finish


def _gather_scratch(n_arr):
    return [pltpu.SemaphoreType.DMA((n_arr * GATHER_COPIES,)), pltpu.SemaphoreType.DMA((n_arr * GATHER_COPIES,)),
            pltpu.SemaphoreType.DMA((n_arr,))]


def gather_two_level(arrays, name):
    n_arr = len(arrays)

    def body(*refs):
        start, finish = _two_level_gather(refs[:n_arr], refs[n_arr:2 * n_arr], *refs[2 * n_arr:])
        start()
        finish()

    any_spec = pl.BlockSpec(memory_space=pl.ANY)
    return pl.pallas_call(
        body, name=name,
        in_specs=[any_spec] * n_arr, out_specs=[any_spec] * n_arr,
        out_shape=[jax.ShapeDtypeStruct((N_DEV,) + tuple(a.shape), a.dtype) for a in arrays],
        scratch_shapes=_gather_scratch(n_arr),
        compiler_params=pltpu.CompilerParams(has_side_effects=True),
    )(*arrays)


def _pack(parts, row_mult):
    rows, spans, r = [], [], 0
    for p in parts:
        flat = p.reshape(-1)
        nr = -(-flat.shape[0] // (PACK_W * row_mult)) * row_mult
        flat = jnp.pad(flat, (0, nr * PACK_W - flat.shape[0]))
        rows.append(flat.reshape(nr, PACK_W))
        spans.append((r, nr))
        r += nr
    return jnp.concatenate(rows, axis=0), spans


def _unpack(buf, spans, shapes):
    out = []
    for (r, nr), shp in zip(spans, shapes):
        size = math.prod(shp)
        out.append(buf[..., r:r + nr, :].reshape(buf.shape[:-2] + (nr * PACK_W,))[..., :size]
                   .reshape(buf.shape[:-2] + tuple(shp)))
    return out


def _to_shards(full, axis):
    r, c = full.shape
    if axis == 0:
        return full.reshape(N_DEV, r // N_DEV, c)
    return full.reshape(r, N_DEV, c // N_DEV).transpose(1, 0, 2)


def _from_shards(sh, axis):
    _, r, c = sh.shape
    if axis == 0:
        return sh.reshape(N_DEV * r, c)
    return sh.transpose(1, 0, 2).reshape(r, N_DEV * c)


def _rms(x, g, n):
    ms = jnp.sum(x * x, axis=-1, keepdims=True) * (1.0 / n)
    return x * lax.rsqrt(ms + EPS) * g


def _norm_mod_fn(x, g, sc, sh):
    return _rms(x, g, x.shape[-1]) * (1.0 + sc) + sh


@jax.custom_vjp
def _rope(t, ct, s1, s2):
    return t * ct + pltpu.roll(t, 16, 1) * s1 + pltpu.roll(t, HEAD_PAD - 16, 1) * s2


def _rope_fwd(t, ct, s1, s2):
    return _rope(t, ct, s1, s2), (ct, s1, s2)


def _rope_bwd(res, d):
    ct, s1, s2 = res
    dt = d * ct + pltpu.roll(d * s1, HEAD_PAD - 16, 1) + pltpu.roll(d * s2, 16, 1)
    return dt, jnp.zeros_like(ct), jnp.zeros_like(s1), jnp.zeros_like(s2)


_rope.defvjp(_rope_fwd, _rope_bwd)


def _qkv_fn(cq, ckv, krsec, qag, kvag, qng, kng, wuq, wk, wv, ct, s1, s2):
    q_raw = bdot(_rms(cq, qag, cq.shape[-1]), wuq)
    ckvn = _rms(ckv, kvag, ckv.shape[-1])
    k_raw = bdot(ckvn, wk)
    v = bdot(ckvn, wv)
    qs, ks = [], []
    for h in range(N_HEADS):
        sl = slice(h * HEAD_PAD, (h + 1) * HEAD_PAD)
        qs.append(_rope(_rms(q_raw[:, sl], qng, QK_DIM), ct, s1, s2))
        ks.append(_rope(_rms(k_raw[:, sl] + krsec, kng, QK_DIM), ct, s1, s2))
    return jnp.concatenate(qs, axis=1), jnp.concatenate(ks, axis=1), v


def _glu_fn(u, yf, yb, dskip, wglu):
    y = u * dskip + yf + yb
    vg = bdot(jax.nn.gelu(y), wglu)
    d = vg.shape[-1] // 2
    return vg[:, :d] * jax.nn.sigmoid(vg[:, d:])


def _merge_fn(o, s_l, gl, x, g1, wo, wout):
    d = x.shape[-1]
    a = bdot(o, wo)
    mix = jax.nn.sigmoid(gl[:, :d]) * a + jax.nn.sigmoid(gl[:, d:]) * s_l
    return x + g1 * bdot(mix, wout)


def _mod_fn(cmat, w):
    return bdot(jax.nn.silu(cmat), w)


def _ssm_prep_fn(lam_re, lam_im, logdt, bre, bim):
    dt = jnp.exp(logdt)
    ar, ai = lam_re * dt, lam_im * dt
    e = jnp.exp(ar)
    lbr, lbi = e * jnp.cos(ai), e * jnp.sin(ai)
    nr, ni = lbr - 1.0, lbi
    den = lam_re * lam_re + lam_im * lam_im
    qr = (nr * lam_re + ni * lam_im) / den
    qi = (ni * lam_re - nr * lam_im) / den
    return lbr, lbi, qr * bre - qi * bim, qr * bim + qi * bre


def _rows(tm, w, col=0):
    return pl.BlockSpec((tm, w), lambda i: (i, col))


def _full(shape):
    nd = len(shape)
    return pl.BlockSpec(tuple(shape), lambda i: (0,) * nd)


def _acc_add(i, ref, val):
    @pl.when(i == 0)
    def _():
        ref[...] = jnp.zeros_like(ref)
    ref[...] += val


def _pad_rows(v, rows=SUBLANES):
    sel = lax.broadcasted_iota(jnp.int32, (rows, v.shape[-1]), 0) == 0
    return jnp.where(sel, jnp.broadcast_to(v, (rows, v.shape[-1])), 0.0)


def norm_mod_fwd(xa, g, mods, n_lat, tm, name):
    r, d = xa.shape
    lat_tiles = n_lat // tm

    def body(x_ref, g_ref, m_ref, o_ref):
        lat = pl.program_id(0) < lat_tiles
        sc = jnp.where(lat, m_ref[0:1, :], m_ref[2:3, :])
        sh = jnp.where(lat, m_ref[1:2, :], m_ref[3:4, :])
        o_ref[...] = _norm_mod_fn(x_ref[...], g_ref[...], sc, sh).astype(o_ref.dtype)

    return pl.pallas_call(
        body, name=name, grid=(r // tm,),
        in_specs=[_rows(tm, d), _full(g.shape), _full(mods.shape)],
        out_specs=_rows(tm, d), out_shape=jax.ShapeDtypeStruct((r, d), BF16),
        compiler_params=_cparams(("parallel",)),
    )(xa, g, mods)


def norm_mod_bwd(xa, dh, dres, g, mods, n_lat, tm, name):
    r, d = xa.shape
    lat_tiles = n_lat // tm

    def body(x_ref, dh_ref, dres_ref, g_ref, m_ref, dx_ref, dg_ref, dm_ref):
        i = pl.program_id(0)
        lat = i < lat_tiles
        sc = jnp.where(lat, m_ref[0:1, :], m_ref[2:3, :])
        sh = jnp.where(lat, m_ref[1:2, :], m_ref[3:4, :])
        _, vjp = jax.vjp(_norm_mod_fn, x_ref[...], g_ref[...], sc, sh)
        dx, dg, dsc, dsh = vjp(dh_ref[...])
        dx_ref[...] = dx + jnp.where(lat, dres_ref[...], 0.0)
        _acc_add(i, dg_ref, _pad_rows(dg))
        row = lax.broadcasted_iota(jnp.int32, (SUBLANES, d), 0)
        base = jnp.where(lat, 0, 2)
        upd = jnp.where(row == base, jnp.broadcast_to(dsc, (SUBLANES, d)), 0.0)
        upd = upd + jnp.where(row == base + 1, jnp.broadcast_to(dsh, (SUBLANES, d)), 0.0)
        _acc_add(i, dm_ref, upd)

    return pl.pallas_call(
        body, name=name, grid=(r // tm,),
        in_specs=[_rows(tm, d), _rows(tm, d),
                  pl.BlockSpec((tm, d), lambda i: (jnp.minimum(i, lat_tiles - 1), 0)),
                  _full(g.shape), _full(mods.shape)],
        out_specs=[_rows(tm, d), _full((SUBLANES, d)), _full((SUBLANES, d))],
        out_shape=[jax.ShapeDtypeStruct((r, d), F32), jax.ShapeDtypeStruct((SUBLANES, d), F32),
                   jax.ShapeDtypeStruct((SUBLANES, d), F32)],
        compiler_params=_cparams(("arbitrary",)),
    )(xa, dh, dres, g, mods)


def qkv_fwd(proj, lay, gains, wuq, wk, wv, tabs, tm, name):
    r = proj.shape[0]
    q_w, kv_w = lay['q'], lay['kv']

    def body(cq_ref, ckv_ref, kr_ref, qag, kvag, qng, kng, wuq_ref, wk_ref, wv_ref, ct, s1, s2, q_ref, k_ref,
             v_ref):
        q, k, v = _qkv_fn(cq_ref[...], ckv_ref[...], kr_ref[...], qag[...], kvag[...], qng[...], kng[...],
                          wuq_ref[...], wk_ref[...], wv_ref[...], ct[...], s1[...], s2[...])
        q_ref[...] = q.astype(BF16)
        k_ref[...] = k.astype(BF16)
        v_ref[...] = v.astype(BF16)

    out = jax.ShapeDtypeStruct((r, HEADS_W), BF16)
    return pl.pallas_call(
        body, name=name, grid=(r // tm,),
        in_specs=[_rows(tm, q_w, lay['o_cq'] // q_w), _rows(tm, kv_w, lay['o_ckv'] // kv_w),
                  _rows(tm, LANES, lay['o_kr'] // LANES)]
        + [_full(a.shape) for a in gains] + [_full(wuq.shape), _full(wk.shape), _full(wv.shape)]
        + [_rows(tm, HEAD_PAD)] * 3,
        out_specs=[_rows(tm, HEADS_W)] * 3, out_shape=[out, out, out],
        compiler_params=_cparams(("parallel",)),
    )(proj, proj, proj, *gains, wuq, wk, wv, *tabs)


def qkv_bwd(proj, lay, gains, wuq, wk, wv, tabs, dq, dk, dv, dgl, du_direct, du_f, du_b, n, tm, name):
    r = proj.shape[0]
    q_w, kv_w, sw, d = lay['q'], lay['kv'], lay['sw'], lay['d']
    lat_tiles = n // tm

    def body(cq_ref, ckv_ref, kr_ref, qag, kvag, qng, kng, wuq_ref, wk_ref, wv_ref, ct, s1, s2, dq_ref, dk_ref,
             dv_ref, dgl_ref, dud_ref, duf_ref, dub_ref, dp_ref, dqag, dkvag, dqng, dkng, dwuq, dwk, dwv):
        i = pl.program_id(0)
        lat = i < lat_tiles
        tables = (ct[...], s1[...], s2[...])
        fn = lambda *a: _qkv_fn(*a, *tables)
        _, vjp = jax.vjp(fn, cq_ref[...], ckv_ref[...], kr_ref[...], qag[...], kvag[...], qng[...], kng[...],
                         wuq_ref[...].astype(F32), wk_ref[...].astype(F32), wv_ref[...].astype(F32))
        g = vjp((jnp.where(lat, dq_ref[...], 0.0), dk_ref[...], dv_ref[...]))
        dp_ref[:, 0:2 * d] = jnp.where(lat, dgl_ref[...], 0.0).astype(BF16)
        dp_ref[:, lay['o_u']:lay['o_u'] + sw] = (duf_ref[...] + dub_ref[...]
                                                 + jnp.where(lat, dud_ref[...], 0.0)).astype(BF16)
        dp_ref[:, lay['o_ckv']:lay['o_ckv'] + kv_w] = g[1].astype(BF16)
        dp_ref[:, lay['o_kr']:lay['o_kr'] + LANES] = g[2].astype(BF16)
        hole0 = lay['o_kr'] + LANES
        if lay['o_cq'] > hole0:
            dp_ref[:, hole0:lay['o_cq']] = jnp.zeros((tm, lay['o_cq'] - hole0), BF16)
        dp_ref[:, lay['o_cq']:lay['o_cq'] + q_w] = g[0].astype(BF16)
        for ref, val in zip((dqag, dkvag, dqng, dkng), g[3:7]):
            _acc_add(i, ref, _pad_rows(val))
        for ref, val in zip((dwuq, dwk, dwv), g[7:10]):
            _acc_add(i, ref, val)

    def lat_rows(w):
        return pl.BlockSpec((tm, w), lambda i: (jnp.minimum(i, lat_tiles - 1), 0))

    acc_shapes = [(SUBLANES, a.shape[1]) for a in gains] + [wuq.shape, wk.shape, wv.shape]
    return pl.pallas_call(
        body, name=name, grid=(r // tm,),
        in_specs=[_rows(tm, q_w, lay['o_cq'] // q_w), _rows(tm, kv_w, lay['o_ckv'] // kv_w),
                  _rows(tm, LANES, lay['o_kr'] // LANES)]
        + [_full(a.shape) for a in gains] + [_full(wuq.shape), _full(wk.shape), _full(wv.shape)]
        + [_rows(tm, HEAD_PAD)] * 3 + [lat_rows(HEADS_W), _rows(tm, HEADS_W), _rows(tm, HEADS_W)]
        + [lat_rows(2 * d), lat_rows(sw), _rows(tm, sw), _rows(tm, sw)],
        out_specs=[_rows(tm, lay['width'])] + [_full(s) for s in acc_shapes],
        out_shape=[jax.ShapeDtypeStruct((r, lay['width']), BF16)] + [jax.ShapeDtypeStruct(s, F32) for s in acc_shapes],
        compiler_params=_cparams(("arbitrary",)),
    )(proj, proj, proj, *gains, wuq, wk, wv, *tabs, dq, dk, dv, dgl, du_direct, du_f, du_b)


def glu_fwd(proj, lay, yf, yb, dskip, wglu, n, tm, name):
    sw, d = wglu.shape[0], wglu.shape[1] // 2

    def body(u_ref, yf_ref, yb_ref, ds_ref, w_ref, o_ref):
        o_ref[...] = _glu_fn(u_ref[...], yf_ref[...], yb_ref[...], ds_ref[...], w_ref[...])

    return pl.pallas_call(
        body, name=name, grid=(n // tm,),
        in_specs=[_rows(tm, sw, lay['o_u'] // sw), _rows(tm, sw), _rows(tm, sw), _full(dskip.shape),
                  _full(wglu.shape)],
        out_specs=_rows(tm, d), out_shape=jax.ShapeDtypeStruct((n, d), F32),
        compiler_params=_cparams(("parallel",)),
    )(proj, yf, yb, dskip, wglu)


def glu_bwd(proj, lay, yf, yb, dskip, wglu, ds_l, n, tm, name):
    sw, d = wglu.shape[0], wglu.shape[1] // 2

    def body(u_ref, yf_ref, yb_ref, ds_ref, w_ref, g_ref, du_ref, dy_ref, dds_ref, dw_ref):
        i = pl.program_id(0)
        _, vjp = jax.vjp(_glu_fn, u_ref[...], yf_ref[...], yb_ref[...], ds_ref[...], w_ref[...].astype(F32))
        du, dyf, _, dds, dw = vjp(g_ref[...])
        du_ref[...] = du
        dy_ref[...] = dyf
        _acc_add(i, dds_ref, _pad_rows(dds))
        _acc_add(i, dw_ref, dw)

    return pl.pallas_call(
        body, name=name, grid=(n // tm,),
        in_specs=[_rows(tm, sw, lay['o_u'] // sw), _rows(tm, sw), _rows(tm, sw), _full(dskip.shape),
                  _full(wglu.shape), _rows(tm, d)],
        out_specs=[_rows(tm, sw), _rows(tm, sw), _full((SUBLANES, sw)), _full(wglu.shape)],
        out_shape=[jax.ShapeDtypeStruct((n, sw), F32), jax.ShapeDtypeStruct((n, sw), F32),
                   jax.ShapeDtypeStruct((SUBLANES, sw), F32), jax.ShapeDtypeStruct(wglu.shape, F32)],
        compiler_params=_cparams(("arbitrary",)),
    )(proj, yf, yb, dskip, wglu, ds_l)


def merge_fwd(o, s_l, proj, xa, g1, wo, wout, n, tm, name):
    d = xa.shape[1]

    def body(o_ref, s_ref, gl_ref, x_ref, g1_ref, wo_ref, wout_ref, x1_ref):
        x1_ref[...] = _merge_fn(o_ref[...], s_ref[...], gl_ref[...], x_ref[...], g1_ref[...], wo_ref[...],
                                wout_ref[...])

    return pl.pallas_call(
        body, name=name, grid=(n // tm,),
        in_specs=[_rows(tm, HEADS_W), _rows(tm, d), _rows(tm, 2 * d), _rows(tm, d), _full(g1.shape),
                  _full(wo.shape), _full(wout.shape)],
        out_specs=_rows(tm, d), out_shape=jax.ShapeDtypeStruct((n, d), F32),
        compiler_params=_cparams(("parallel",)),
    )(o, s_l, proj, xa, g1, wo, wout)


def merge_bwd(o, s_l, proj, xa, g1, wo, wout, dx1, n, tm, name):
    d = xa.shape[1]

    def body(o_ref, s_ref, gl_ref, x_ref, g1_ref, wo_ref, wout_ref, dx1_ref, do_ref, ds_ref, dgl_ref, dg1_ref,
             dwo_ref, dwout_ref):
        i = pl.program_id(0)
        _, vjp = jax.vjp(_merge_fn, o_ref[...], s_ref[...], gl_ref[...], x_ref[...], g1_ref[...],
                         wo_ref[...].astype(F32), wout_ref[...].astype(F32))
        do, ds, dgl, _, dg1, dwo, dwout = vjp(dx1_ref[...])
        do_ref[...] = do
        ds_ref[...] = ds
        dgl_ref[...] = dgl
        _acc_add(i, dg1_ref, _pad_rows(dg1))
        _acc_add(i, dwo_ref, dwo)
        _acc_add(i, dwout_ref, dwout)

    return pl.pallas_call(
        body, name=name, grid=(n // tm,),
        in_specs=[_rows(tm, HEADS_W), _rows(tm, d), _rows(tm, 2 * d), _rows(tm, d), _full(g1.shape),
                  _full(wo.shape), _full(wout.shape), _rows(tm, d)],
        out_specs=[_rows(tm, HEADS_W), _rows(tm, d), _rows(tm, 2 * d), _full((SUBLANES, d)), _full(wo.shape),
                   _full(wout.shape)],
        out_shape=[jax.ShapeDtypeStruct((n, HEADS_W), BF16), jax.ShapeDtypeStruct((n, d), F32),
                   jax.ShapeDtypeStruct((n, 2 * d), F32), jax.ShapeDtypeStruct((SUBLANES, d), F32),
                   jax.ShapeDtypeStruct(wo.shape, F32), jax.ShapeDtypeStruct(wout.shape, F32)],
        compiler_params=_cparams(("arbitrary",)),
    )(o, s_l, proj, xa, g1, wo, wout, dx1)


def _halo_specs(tm, w, n):
    nb = n // SUBLANES
    per = tm // SUBLANES
    prev = pl.BlockSpec((SUBLANES, w), lambda i: (jnp.maximum(i * per - 1, 0), 0))
    nxt = pl.BlockSpec((SUBLANES, w), lambda i: (jnp.minimum((i + 1) * per, nb - 1), 0))
    return prev, nxt


def _shifted(t, prev_blk, next_blk, i, n_tiles):
    tm = t.shape[0]
    row = lax.broadcasted_iota(jnp.int32, t.shape, 0)
    prev_row = jnp.where(i > 0, prev_blk[SUBLANES - 1:SUBLANES, :], 0.0)
    next_row = jnp.where(i < n_tiles - 1, next_blk[0:1, :], 0.0)
    before = jnp.where(row == 0, prev_row, pltpu.roll(t, 1, 0))
    after = jnp.where(row == tm - 1, next_row, pltpu.roll(t, tm - 1, 0))
    return before, after


def _conv_u2(up, before, after, cw, cb):
    return before * cw[0:1, :] + up * cw[1:2, :] + after * cw[2:3, :] + cb


def conv_act_fwd(up, cw, cb, tm, name):
    n, w2 = up.shape
    f = w2 // 2
    n_tiles = n // tm
    prev, nxt = _halo_specs(tm, w2, n)

    def body(up_ref, prev_ref, next_ref, cw_ref, cb_ref, act_ref):
        i = pl.program_id(0)
        t = up_ref[...]
        before, after = _shifted(t, prev_ref[...], next_ref[...], i, n_tiles)
        u2 = _conv_u2(t, before, after, cw_ref[...], cb_ref[...])
        act_ref[...] = (jax.nn.silu(u2[:, f:]) * u2[:, :f]).astype(BF16)

    return pl.pallas_call(
        body, name=name, grid=(n_tiles,),
        in_specs=[_rows(tm, w2), prev, nxt, _full(cw.shape), _full(cb.shape)],
        out_specs=_rows(tm, f), out_shape=jax.ShapeDtypeStruct((n, f), BF16),
        compiler_params=_cparams(("parallel",)),
    )(up, up, up, cw, cb)


def down_loss(act, wdown, x1, g2, target, tm, name):
    n, d = x1.shape
    f = act.shape[1]

    def body(act_ref, w_ref, x1_ref, g2_ref, t_ref, dy_ref, ffn_ref, loss_ref):
        ffn = jnp.dot(act_ref[...], w_ref[...], preferred_element_type=F32)
        err = x1_ref[...] + g2_ref[...] * ffn - t_ref[...]
        ffn_ref[...] = ffn
        dy_ref[...] = err * (1.0 / d)
        part = 0.5 * jnp.sum(jnp.sum(err * err, axis=-1, keepdims=True) * (1.0 / d), axis=0, keepdims=True)
        loss_ref[0] = jnp.broadcast_to(part, (SUBLANES, LANES))

    return pl.pallas_call(
        body, name=name, grid=(n // tm,),
        in_specs=[_rows(tm, f), _full(wdown.shape), _rows(tm, d), _full(g2.shape), _rows(tm, d)],
        out_specs=[_rows(tm, d), _rows(tm, d), pl.BlockSpec((1, SUBLANES, LANES), lambda i: (i, 0, 0))],
        out_shape=[jax.ShapeDtypeStruct((n, d), F32), jax.ShapeDtypeStruct((n, d), F32),
                   jax.ShapeDtypeStruct((n // tm, SUBLANES, LANES), F32)],
        compiler_params=_cparams(("parallel",)),
    )(act, wdown, x1, g2, target)


FFN_BWD_PARTS = 2


def ffn_bwd(dy, ffn, up, cw, cb, wdown, g2, tm, name):
    n, d = dy.shape
    w2 = up.shape[1]
    f = w2 // 2
    fc = f // FFN_BWD_PARTS
    assert fc % LANES == 0
    n_tiles = n // tm
    ext = tm + 2 * SUBLANES
    inner = slice(SUBLANES, SUBLANES + tm)
    prev_w, next_w = _halo_specs(tm, w2, n)
    prev_d, next_d = _halo_specs(tm, d, n)

    def body(dy_ref, dyp_ref, dyn_ref, ffn_ref, up_ref, upp_ref, upn_ref, cw_ref, cb_ref, w_ref, g2_ref,
             dup_ref, dffn_ref, dg2_ref, dcw_ref):
        i = pl.program_id(0)
        has_prev, has_next = i > 0, i < n_tiles - 1

        def extended(prev, tile, nxt):
            return jnp.concatenate([jnp.where(has_prev, prev, 0.0), tile, jnp.where(has_next, nxt, 0.0)], axis=0)

        dyv = dy_ref[...]
        dffn = extended(dyp_ref[...], dyv, dyn_ref[...]) * g2_ref[...]
        dffn_ref[...] = dffn[inner].astype(BF16)
        dffn = dffn.astype(BF16)
        _acc_add(i, dg2_ref, _pad_rows(jnp.sum(dyv * ffn_ref[...], axis=0, keepdims=True)))
        row = lax.broadcasted_iota(jnp.int32, (SUBLANES, fc), 0)
        shift = lambda t: (pltpu.roll(t, 1, 0), pltpu.roll(t, ext - 1, 0))
        for part in range(FFN_BWD_PARTS):
            halves = []
            for col0 in (part * fc, f + part * fc):
                cols = slice(col0, col0 + fc)
                t = extended(upp_ref[:, cols], up_ref[:, cols], upn_ref[:, cols])
                before, after = shift(t)
                halves.append((cols, t, _conv_u2(t, before, after, cw_ref[:, cols], cb_ref[:, cols])))
            (_, _, val), (_, _, gate) = halves
            dact = _dot_nt(dffn, w_ref[part * fc:(part + 1) * fc, :])
            sg = jax.nn.sigmoid(gate)
            for (cols, t, _), du2 in zip(halves, (dact * (gate * sg),
                                                   dact * val * (sg * (1.0 + gate * (1.0 - sg))))):
                before, after = shift(du2)
                cwv = cw_ref[:, cols]
                dup_ref[:, cols] = (after * cwv[0:1, :] + du2 * cwv[1:2, :] + before * cwv[2:3, :])[inner].astype(BF16)
                upd = jnp.zeros((SUBLANES, fc), F32)
                for k, term in enumerate((after * t, du2 * t, before * t, du2)):
                    upd = upd + jnp.where(row == k, jnp.broadcast_to(
                        jnp.sum(term[inner], axis=0, keepdims=True), (SUBLANES, fc)), 0.0)

                @pl.when(i == 0)
                def _():
                    dcw_ref[:, cols] = jnp.zeros((SUBLANES, fc), F32)

                dcw_ref[:, cols] += upd

    return pl.pallas_call(
        body, name=name, grid=(n_tiles,),
        in_specs=[_rows(tm, d), prev_d, next_d, _rows(tm, d), _rows(tm, w2), prev_w, next_w, _full(cw.shape),
                  _full(cb.shape), _full(wdown.shape), _full(g2.shape)],
        out_specs=[_rows(tm, w2), _rows(tm, d), _full((SUBLANES, d)), _full((SUBLANES, w2))],
        out_shape=[jax.ShapeDtypeStruct((n, w2), BF16), jax.ShapeDtypeStruct((n, d), BF16),
                   jax.ShapeDtypeStruct((SUBLANES, d), F32), jax.ShapeDtypeStruct((SUBLANES, w2), F32)],
        compiler_params=_cparams(("arbitrary",)),
    )(dy, dy, dy, ffn, up, up, up, cw, cb, wdown, g2)


def attn_fwd(q, k, v, n, name, gather=()):
    nk = k.shape[0]
    tq = _tile(n, (512, 256, 128))
    tk = _tile(nk, (768, 384, 256, 128))
    n_kv = nk // tk
    n_q = n // tq
    n_g = len(gather)
    scale = QK_DIM ** -0.5
    c2 = scale * math.log2(math.e)

    def body(q_ref, k_ref, v_ref, *rest):
        o_ref, lse_ref = rest[n_g:n_g + 2]
        if n_g:
            start, finish = _two_level_gather(rest[:n_g], rest[n_g + 2:2 * n_g + 2], *rest[2 * n_g + 2:])
            step = pl.program_id(0) * n_q + pl.program_id(1)
            pl.when(step == 0)(start)
        qv = q_ref[...]
        ones_col = (lax.broadcasted_iota(jnp.int32, (tk, HEAD_PAD), 1) == V_DIM).astype(BF16)

        def chunk(j, carry):
            m, acc = carry
            rows = pl.ds(pl.multiple_of(j * tk, tk), tk)
            s = _dot_nt(qv, k_ref[rows, :])
            m_new = jnp.maximum(m, jnp.max(s, axis=-1, keepdims=True))
            p = jnp.exp2(s * c2 - m_new * c2)
            alpha = jnp.exp2((m - m_new) * c2)
            pv = jnp.dot(p.astype(BF16), v_ref[rows, :] + ones_col, preferred_element_type=F32)
            return m_new, alpha * acc + pv

        m, acc = lax.fori_loop(0, n_kv, chunk, (jnp.full((tq, 1), -jnp.inf, F32),
                                                jnp.zeros((tq, HEAD_PAD), F32)), unroll=True)
        l = acc[:, V_DIM:V_DIM + 1]
        lane = lax.broadcasted_iota(jnp.int32, (tq, HEAD_PAD), 1)
        o_ref[...] = jnp.where(lane < V_DIM, acc / l, 0.0).astype(BF16)
        lse_ref[...] = jnp.broadcast_to(m * scale + jnp.log(l), (tq, HEAD_PAD))
        if n_g:
            pl.when(step == N_HEADS * n_q - 1)(finish)

    qspec = pl.BlockSpec((tq, HEAD_PAD), lambda h, i: (i, h))
    kspec = pl.BlockSpec((nk, HEAD_PAD), lambda h, i: (0, h))
    any_spec = pl.BlockSpec(memory_space=pl.ANY)
    return pl.pallas_call(
        body, name=name, grid=(N_HEADS, n_q),
        in_specs=[qspec, kspec, kspec] + [any_spec] * n_g, out_specs=[qspec, qspec] + [any_spec] * n_g,
        out_shape=[jax.ShapeDtypeStruct((n, HEADS_W), BF16), jax.ShapeDtypeStruct((n, HEADS_W), F32)]
        + [jax.ShapeDtypeStruct((N_DEV,) + tuple(a.shape), a.dtype) for a in gather],
        scratch_shapes=_gather_scratch(n_g) if n_g else [],
        compiler_params=_cparams(("arbitrary", "arbitrary") if n_g else ("parallel", "parallel")),
    )(q, k, v, *gather)


def attn_bwd(q, k, v, o, do, lse, n, name, a2a=()):
    nk = k.shape[0]
    tq = _tile(n, (512, 256, 128))
    tk = _tile(nk, (768, 384, 256, 128))
    n_kv = nk // tk
    n_x = len(a2a)
    kinds = ['a2a'] * n_x
    scale = QK_DIM ** -0.5
    log2e = math.log2(math.e)
    c2 = scale * log2e

    def body(q_ref, k_ref, v_ref, o_ref, do_ref, lse_ref, *rest):
        dq_ref, dk_ref, dv_ref = rest[n_x:n_x + 3]
        if n_x:
            start, finish = _exchange_ops(rest[:n_x], rest[n_x + 3:2 * n_x + 3], *rest[2 * n_x + 3:], kinds, 'all')
            step = pl.program_id(0) * n_kv + pl.program_id(1)
            pl.when(step == 0)(start)

        @pl.when(pl.program_id(1) == 0)
        def _():
            dq_ref[...] = jnp.zeros_like(dq_ref)

        kv, vv = k_ref[...], v_ref[...]

        def q_tile(i, carry):
            dk, dv = carry
            rows = pl.ds(pl.multiple_of(i * tq, tq), tq)
            qv, dov = q_ref[rows, :], do_ref[rows, :]
            p = jnp.exp2(_dot_nt(qv, kv) * c2 - lse_ref[rows, 0:1] * log2e)
            dv = dv + _dot_tn(p.astype(BF16), dov)
            dp = _dot_nt(dov, vv)
            delta = jnp.sum(dov.astype(F32) * o_ref[rows, :].astype(F32), axis=-1, keepdims=True)
            ds = (p * (dp - delta) * scale).astype(BF16)
            dk = dk + _dot_tn(ds, qv)
            dq_ref[rows, :] += jnp.dot(ds, kv, preferred_element_type=F32)
            return dk, dv

        zero = jnp.zeros((tk, HEAD_PAD), F32)
        dk, dv = lax.fori_loop(0, n // tq, q_tile, (zero, zero), unroll=2)
        dk_ref[...] = dk
        dv_ref[...] = dv
        if n_x:
            pl.when(step == N_HEADS * n_kv - 1)(finish)

    qspec = pl.BlockSpec((n, HEAD_PAD), lambda h, j: (0, h))
    kspec = pl.BlockSpec((tk, HEAD_PAD), lambda h, j: (j, h))
    any_spec = pl.BlockSpec(memory_space=pl.ANY)
    return pl.pallas_call(
        body, name=name, grid=(N_HEADS, n_kv),
        in_specs=[qspec, kspec, kspec, qspec, qspec, qspec] + [any_spec] * n_x,
        out_specs=[qspec, kspec, kspec] + [any_spec] * n_x,
        out_shape=[jax.ShapeDtypeStruct((n, HEADS_W), F32), jax.ShapeDtypeStruct((nk, HEADS_W), F32),
                   jax.ShapeDtypeStruct((nk, HEADS_W), F32)] + _exchange_shapes(a2a, kinds, 'all'),
        scratch_shapes=_exchange_scratch(n_x, 'all') if n_x else [],
        compiler_params=_cparams(("arbitrary", "arbitrary") if n_x else ("parallel", "arbitrary")),
    )(q, k, v, o, do, lse, *a2a)


SCAN_LEVELS = (1, 2, 4)
SCAN_LANES = 512


def _scan_chunk(xr, xi, car, m_ref, p_ref, reverse):
    t_len, gn = xr.shape
    n_slab = t_len // SUBLANES
    for lb in range(gn // SCAN_LANES):
        ls = pl.ds(lb * SCAN_LANES, SCAN_LANES)

        def step(s, carry, ls=ls):
            cr, ci = carry
            slab = (n_slab - 1 - s) if reverse else s
            rows = pl.ds(pl.multiple_of(slab * SUBLANES, SUBLANES), SUBLANES)
            br, bi = xr[rows, ls], xi[rows, ls]
            for lvl, d in enumerate(SCAN_LEVELS):
                shift = SUBLANES - d if reverse else d
                sr, si = pltpu.roll(br, shift, 0), pltpu.roll(bi, shift, 0)
                mr, mi = m_ref[lvl, 0, :, ls], m_ref[lvl, 1, :, ls]
                br, bi = br + mr * sr - mi * si, bi + mr * si + mi * sr
            pr, pi = p_ref[0, :, ls], p_ref[1, :, ls]
            br, bi = br + pr * cr - pi * ci, bi + pr * ci + pi * cr
            xr[rows, ls] = br
            xi[rows, ls] = bi
            last = 0 if reverse else SUBLANES - 1
            return br[last:last + 1, :], bi[last:last + 1, :]

        cr, ci = lax.fori_loop(0, n_slab, step, (car[0:1, ls], car[1:2, ls]))
        car[0:1, ls] = cr
        car[1:2, ls] = ci


SSM_SPLIT = 2


def _bd_dot(a, w_ref):
    k, n = w_ref.shape[0] // SSM_SPLIT, w_ref.shape[1] // SSM_SPLIT
    return jnp.concatenate([jnp.dot(a[:, p * k:(p + 1) * k], w_ref[p * k:(p + 1) * k, p * n:(p + 1) * n],
                                    preferred_element_type=F32) for p in range(SSM_SPLIT)], axis=1)


def _bd_dot_nt(a, w_ref):
    k, n = w_ref.shape[0] // SSM_SPLIT, w_ref.shape[1] // SSM_SPLIT
    return jnp.concatenate([_dot_nt(a[:, p * n:(p + 1) * n], w_ref[p * k:(p + 1) * k, p * n:(p + 1) * n])
                            for p in range(SSM_SPLIT)], axis=1)


def _seq_block(step, n_chunk, lat_chunks, reverse):
    if reverse:
        return n_chunk - 1 - step
    return (step + lat_chunks) % n_chunk


def ssm_fwd(proj, lay, bre, bim, cre, ncim, tabs, n, t_len, reverse, name):
    l, sw = proj.shape[0], lay['sw']
    gn = bre.shape[-1]
    n_chunk, lat_chunks = l // t_len, n // t_len
    blk = lambda k: _seq_block(k, n_chunk, lat_chunks, reverse)
    mtab, ptab = tabs

    def body(u_ref, bre_ref, bim_ref, cre_ref, ncim_ref, m_ref, p_ref, y_ref, xb_ref, xr, xi, car):
        @pl.when(pl.program_id(0) == 0)
        def _():
            car[...] = jnp.zeros_like(car)

        xb_ref[...] = car[...]
        u = u_ref[...].astype(BF16)
        xr[...] = _bd_dot(u, bre_ref)
        xi[...] = _bd_dot(u, bim_ref)
        _scan_chunk(xr, xi, car, m_ref, p_ref, reverse)
        y_ref[...] = _bd_dot(xr[...].astype(BF16), cre_ref) + _bd_dot(xi[...].astype(BF16), ncim_ref)

    return pl.pallas_call(
        body, name=name, grid=(n_chunk,),
        in_specs=[pl.BlockSpec((t_len, sw), lambda k: (blk(k), lay['o_u'] // sw)), _full(bre.shape),
                  _full(bim.shape), _full(cre.shape), _full(ncim.shape), _full(mtab.shape), _full(ptab.shape)],
        out_specs=[pl.BlockSpec((t_len, sw), lambda k: (blk(k), 0)),
                   pl.BlockSpec((None, 2, gn), lambda k: (k, 0, 0))],
        out_shape=[jax.ShapeDtypeStruct((l, sw), F32), jax.ShapeDtypeStruct((n_chunk, 2, gn), F32)],
        scratch_shapes=[pltpu.VMEM((t_len, gn), F32), pltpu.VMEM((t_len, gn), F32), pltpu.VMEM((2, gn), F32)],
        compiler_params=_cparams(("arbitrary",)),
    )(proj, bre, bim, cre, ncim, mtab, ptab)


def ssm_bwd(proj, lay, dyr, xb, bre, bim, cre, ncim, tabs, adj_tabs, n, t_len, reverse, name):
    l, sw = proj.shape[0], lay['sw']
    gn = bre.shape[-1]
    n_chunk, lat_chunks = l // t_len, n // t_len
    fwd_step = lambda k: n_chunk - 1 - k
    blk = lambda k: _seq_block(fwd_step(k), n_chunk, lat_chunks, reverse)
    (mtab, ptab), (mtab_r, ptab_r) = tabs, adj_tabs

    def body(u_ref, dy_ref, xb_ref, bre_ref, bim_ref, cre_ref, ncim_ref, m_ref, p_ref, mr_ref, pr_ref,
             du_ref, gr_ref, gi_ref, xr_ref, xi_ref, dlam_ref, xr, xi, gr, gi, car, acar):
        k = pl.program_id(0)

        @pl.when(k == 0)
        def _():
            acar[...] = jnp.zeros_like(acar)
            dlam_ref[...] = jnp.zeros_like(dlam_ref)

        u = u_ref[...].astype(BF16)
        dy = jnp.where(blk(k) < lat_chunks, dy_ref[...], 0.0).astype(BF16)
        xr[...] = _bd_dot(u, bre_ref)
        xi[...] = _bd_dot(u, bim_ref)
        car[...] = xb_ref[...]
        _scan_chunk(xr, xi, car, m_ref, p_ref, reverse)
        gr[...] = _bd_dot_nt(dy, cre_ref)
        gi[...] = _bd_dot_nt(dy, ncim_ref)
        _scan_chunk(gr, gi, acar, mr_ref, pr_ref, not reverse)
        xrv, xiv, grv, giv = xr[...], xi[...], gr[...], gi[...]
        row = lax.broadcasted_iota(jnp.int32, (t_len, gn), 0)
        first, shift = (t_len - 1, t_len - 1) if reverse else (0, 1)
        xpr = jnp.where(row == first, xb_ref[0:1, :], pltpu.roll(xrv, shift, 0))
        xpi = jnp.where(row == first, xb_ref[1:2, :], pltpu.roll(xiv, shift, 0))
        dlr = grv * xpr + giv * xpi
        dli = giv * xpr - grv * xpi
        dlam_ref[0] += jnp.sum(dlr.reshape(t_len // SUBLANES, SUBLANES, gn), axis=0)
        dlam_ref[1] += jnp.sum(dli.reshape(t_len // SUBLANES, SUBLANES, gn), axis=0)
        grb, gib = grv.astype(BF16), giv.astype(BF16)
        du_ref[...] = _bd_dot_nt(grb, bre_ref) + _bd_dot_nt(gib, bim_ref)
        gr_ref[...] = grb
        gi_ref[...] = gib
        xr_ref[...] = xrv.astype(BF16)
        xi_ref[...] = xiv.astype(BF16)

    def at_blk(width, col=0):
        return pl.BlockSpec((t_len, width), lambda k: (blk(k), col))

    state = jax.ShapeDtypeStruct((l, gn), BF16)
    return pl.pallas_call(
        body, name=name, grid=(n_chunk,),
        in_specs=[at_blk(sw, lay['o_u'] // sw),
                  pl.BlockSpec((t_len, sw), lambda k: (jnp.minimum(blk(k), lat_chunks - 1), 0)),
                  pl.BlockSpec((None, 2, gn), lambda k: (fwd_step(k), 0, 0)),
                  _full(bre.shape), _full(bim.shape), _full(cre.shape), _full(ncim.shape), _full(mtab.shape),
                  _full(ptab.shape), _full(mtab_r.shape), _full(ptab_r.shape)],
        out_specs=[at_blk(sw), at_blk(gn), at_blk(gn), at_blk(gn), at_blk(gn), _full((2, SUBLANES, gn))],
        out_shape=[jax.ShapeDtypeStruct((l, sw), F32), state, state, state, state,
                   jax.ShapeDtypeStruct((2, SUBLANES, gn), F32)],
        scratch_shapes=[pltpu.VMEM((t_len, gn), F32)] * 4 + [pltpu.VMEM((2, gn), F32)] * 2,
        compiler_params=_cparams(("arbitrary",)),
    )(proj, dyr, xb, bre, bim, cre, ncim, mtab, ptab, mtab_r, ptab_r)


def ssm_prep(lam_re, lam_im, logdt, bre, bim, name):
    gn = lam_re.shape[1]

    def body(lr_ref, li_ref, dt_ref, br_ref, bi_ref, pwr_ref, pwi_ref, bbr_ref, bbi_ref):
        _, _, bbr, bbi = _ssm_prep_fn(lr_ref[...], li_ref[...], dt_ref[...], br_ref[...], bi_ref[...])
        bbr_ref[...] = bbr
        bbi_ref[...] = bbi
        kk = (lax.broadcasted_iota(jnp.int32, (SUBLANES, gn), 0) + 1).astype(F32)
        dt = jnp.exp(dt_ref[...])
        ar, ai = lr_ref[...] * dt * kk, li_ref[...] * dt * kk
        e = jnp.exp(ar)
        pwr_ref[...] = e * jnp.cos(ai)
        pwi_ref[...] = e * jnp.sin(ai)

    ins = (lam_re, lam_im, logdt, bre, bim)
    return pl.pallas_call(
        body, name=name,
        out_shape=[jax.ShapeDtypeStruct((SUBLANES, gn), F32)] * 2 + [jax.ShapeDtypeStruct(bre.shape, F32)] * 2,
        compiler_params=_cparams(),
    )(*ins)


def ssm_prep_bwd(lam_re, lam_im, logdt, bre, bim, dlbr, dlbi, dbbr, dbbi, name):
    def body(lr_ref, li_ref, dt_ref, br_ref, bi_ref, g0, g1, g2, g3, o0, o1, o2, o3, o4):
        _, vjp = jax.vjp(_ssm_prep_fn, lr_ref[...], li_ref[...], dt_ref[...], br_ref[...], bi_ref[...])
        for ref, val in zip((o0, o1, o2, o3, o4), vjp((g0[...], g1[...], g2[...], g3[...]))):
            ref[...] = val

    ins = (lam_re, lam_im, logdt, bre, bim)
    return pl.pallas_call(
        body, name=name,
        out_shape=[jax.ShapeDtypeStruct(a.shape, F32) for a in ins],
        compiler_params=_cparams(),
    )(*ins, dlbr, dlbi, dbbr, dbbi)


def mod_fwd(cmat, w, b, name):
    def body(c_ref, w_ref, b_ref, o_ref):
        o_ref[...] = _mod_fn(c_ref[...], w_ref[...]) + b_ref[...]

    return pl.pallas_call(body, name=name, out_shape=jax.ShapeDtypeStruct((cmat.shape[0], w.shape[1]), F32),
                          compiler_params=_cparams())(cmat, w, b)


def mod_bwd(cmat, w, g_lat, g_ctx, g_all, name):
    def body(c_ref, w_ref, gl_ref, gc_ref, ga_ref, dw_ref, dc_ref, db_ref):
        gc = jnp.sum(gc_ref[...], axis=0, keepdims=True)
        dm = jnp.concatenate([gl_ref[...], _pad_rows(gc)], axis=0)
        _, vjp = jax.vjp(_mod_fn, c_ref[...], w_ref[...])
        dc, dw = vjp(dm)
        dw_ref[...] = dw
        dc_ref[...] = dc
        db_ref[...] = _pad_rows(jnp.sum(ga_ref[...], axis=0, keepdims=True))

    return pl.pallas_call(
        body, name=name,
        out_shape=[jax.ShapeDtypeStruct(w.shape, F32), jax.ShapeDtypeStruct(cmat.shape, F32),
                   jax.ShapeDtypeStruct((SUBLANES, g_all.shape[1]), F32)],
        compiler_params=_cparams(),
    )(cmat, w, g_lat, g_ctx, g_all)


def add_own(own, idx, recv, out_dtype, name):
    _, s, r, c = own.shape

    def body(idx_ref, own_ref, recv_ref, o_ref):
        o_ref[...] = (own_ref[...] + recv_ref[...]).astype(o_ref.dtype)

    return pl.pallas_call(
        body, name=name,
        grid_spec=pltpu.PrefetchScalarGridSpec(
            num_scalar_prefetch=1, grid=(s,),
            in_specs=[pl.BlockSpec((None, None, r, c), lambda k, idx_ref: (idx_ref[0], k, 0, 0)),
                      pl.BlockSpec((None, r, c), lambda k, idx_ref: (k, 0, 0))],
            out_specs=pl.BlockSpec((None, r, c), lambda k, idx_ref: (k, 0, 0))),
        out_shape=jax.ShapeDtypeStruct((s, r, c), out_dtype),
        compiler_params=_cparams(("parallel",)),
    )(idx, own, recv)


def reduce_adamw(parts, w, m, v, name):
    s, r, c = parts.shape
    tr = _tile(r, (256, 128, 64, 32, 16, 8))

    def body(p_ref, w_ref, m_ref, v_ref, g_ref, d_ref, nm_ref, nv_ref):
        g = p_ref[0].astype(F32)
        for k in range(1, s):
            g = g + p_ref[k].astype(F32)
        mm = ADAM_B1 * m_ref[...] + (1.0 - ADAM_B1) * g
        vv = ADAM_B2 * v_ref[...] + (1.0 - ADAM_B2) * jnp.square(g)
        m_hat = mm / (1.0 - ADAM_B1 ** ADAM_STEP)
        v_hat = vv / (1.0 - ADAM_B2 ** ADAM_STEP)
        g_ref[...] = g
        d_ref[...] = -ADAM_LR * (m_hat / (jnp.sqrt(v_hat) + ADAM_EPS) + ADAM_WD * w_ref[...])
        nm_ref[...] = mm
        nv_ref[...] = vv

    out = jax.ShapeDtypeStruct((r, c), F32)
    blk = _rows(tr, c)
    return pl.pallas_call(
        body, name=name, grid=(r // tr,),
        in_specs=[pl.BlockSpec((s, tr, c), lambda i: (0, i, 0)), blk, blk, blk],
        out_specs=[blk] * 4, out_shape=[out] * 4,
        compiler_params=_cparams(("parallel",)),
    )(parts, w, m, v)


def _in_layout(d, q, kv, sw):
    o_u = 2 * d
    o_ckv = o_u + sw
    o_kr = o_ckv + kv
    o_cq = -(-(o_kr + LANES) // q) * q
    assert o_u % sw == 0 and o_ckv % kv == 0 and o_kr % LANES == 0
    assert q % LANES == 0 and kv % LANES == 0 and sw % LANES == 0
    return dict(d=d, q=q, kv=kv, sw=sw, o_gl=0, o_u=o_u, o_ckv=o_ckv, o_kr=o_kr, o_cq=o_cq, width=o_cq + q)


def _pad_w_in(w_in, lay):
    q, kv, sw, d = lay['q'], lay['kv'], lay['sw'], lay['d']
    cq, ckv, kr, u, gl = jnp.split(w_in, [q, q + kv, q + kv + QK_ROPE, q + kv + QK_ROPE + sw], axis=1)
    z = lambda w: jnp.zeros((w_in.shape[0], w), w_in.dtype)
    hole = lay['o_cq'] - lay['o_kr'] - LANES
    return jnp.concatenate([gl, u, ckv, z(QK_NOPE), kr, z(LANES - QK_DIM), z(hole), cq], axis=1)


def _unpad_w_in(g, lay):
    q, kv, sw, d = lay['q'], lay['kv'], lay['sw'], lay['d']
    kr0 = lay['o_kr'] + QK_NOPE
    return jnp.concatenate([g[:, lay['o_cq']:lay['o_cq'] + q], g[:, lay['o_ckv']:lay['o_ckv'] + kv],
                            g[:, kr0:kr0 + QK_ROPE], g[:, lay['o_u']:lay['o_u'] + sw], g[:, :2 * d]], axis=1)


def _pad_heads(w, width):
    k = w.shape[0]
    return jnp.pad(w.reshape(k, N_HEADS, width), ((0, 0), (0, 0), (0, HEAD_PAD - width))).reshape(k, HEADS_W)


def _unpad_heads(w, width):
    k = w.shape[0]
    return w.reshape(k, N_HEADS, HEAD_PAD)[:, :, :width].reshape(k, N_HEADS * width)


def _rope_tables(n, nc):
    rows = n // GRID_W
    row = jnp.repeat(jnp.arange(rows), GRID_W)
    col = jnp.tile(jnp.arange(GRID_W), rows)
    pairs = QK_ROPE // 4
    freqs = ROPE_THETA ** (-jnp.arange(pairs, dtype=F32) / pairs)
    ang = jnp.concatenate([row[:, None] * freqs, col[:, None] * freqs], axis=-1)
    cos = jnp.concatenate([jnp.cos(ang), jnp.ones((nc, 2 * pairs), F32)], axis=0)
    sin = jnp.concatenate([jnp.sin(ang), jnp.zeros((nc, 2 * pairs), F32)], axis=0)
    l = n + nc
    half = QK_ROPE // 2
    ct = jnp.concatenate([jnp.ones((l, QK_NOPE), F32), cos, cos, jnp.zeros((l, HEAD_PAD - QK_DIM), F32)], axis=1)
    s1 = jnp.concatenate([jnp.zeros((l, QK_NOPE + half), F32), sin, jnp.zeros((l, HEAD_PAD - QK_DIM), F32)],
                         axis=1)
    s2 = jnp.concatenate([jnp.zeros((l, QK_NOPE), F32), -sin, jnp.zeros((l, HEAD_PAD - QK_NOPE - half), F32)],
                         axis=1)
    return ct, s1, s2


def _group_mask(rows, cols, g):
    return (jnp.arange(rows)[:, None] // (rows // g)) == (jnp.arange(cols)[None, :] // (cols // g))


def _block_diag_rows(m, g):
    return jnp.where(_group_mask(g * m.shape[0], m.shape[1], g), jnp.tile(m, (g, 1)), 0)


def _block_diag_cols(m, g):
    return jnp.where(_group_mask(m.shape[0], g * m.shape[1], g), jnp.tile(m, (1, g)), 0)


def _diag_blocks(m, g):
    a, b = m.shape[0] // g, m.shape[1] // g
    masked = jnp.where(_group_mask(m.shape[0], m.shape[1], g), m, 0)
    return masked.reshape(m.shape[0], g, b).sum(axis=1).reshape(g, a, b)


def _scan_tables(pwr, pwi, reverse):
    row = jnp.arange(SUBLANES)[:, None]
    zero = jnp.zeros_like(pwr)
    levels = []
    for d in SCAN_LEVELS:
        keep = (row < SUBLANES - d) if reverse else (row >= d)
        levels.append(jnp.stack([jnp.where(keep, pwr[d - 1:d, :], zero), jnp.where(keep, pwi[d - 1:d, :], zero)]))
    carry = jnp.stack([pwr[::-1], pwi[::-1]]) if reverse else jnp.stack([pwr, pwi])
    return jnp.stack(levels), carry


def _step(inp):
    x, c, ctx = inp['x'][0], inp['c'], inp['ctx'][0]
    target = inp['loss_target'][0]
    n, d = x.shape
    nc = ctx.shape[0]
    l = n + nc
    q_w, kv_w = inp['q_a_g'].shape[1], inp['kv_a_g'].shape[1]
    sw = inp['d_skip'].shape[1]
    n_grp = sw // SSM_GROUP
    gn = n_grp * SSM_STATE
    lay = _in_layout(d, q_w, kv_w, sw)
    tm = _tile(math.gcd(n, nc), (256, 128))
    me = 4 * lax.axis_index("x") + 2 * lax.axis_index("y") + lax.axis_index("c")
    strip = lambda a: a if a.ndim <= 2 else a[0]
    w = {k: strip(inp[k]) for k in WEIGHT_NAMES}

    gathered_names = list(GATHERED)
    early_names = ['w_in', 'w_uq', 'w_ukv', 'w_o_attn']
    late_names = [k for k in gathered_names if k not in early_names]
    c_all, *wg = gather_two_level([jnp.broadcast_to(c, (SUBLANES, d))] + [w[k].astype(BF16) for k in early_names],
                                  "gather_weights")
    full = {k: _from_shards(s, GATHERED[k]) for k, s in zip(early_names, wg)}

    w_in_p = _pad_w_in(full['w_in'], lay)
    wuq_p = _pad_heads(full['w_uq'], QK_DIM)
    ukv = full['w_ukv'].reshape(kv_w, N_HEADS, QK_NOPE + V_DIM)
    wk_p = _pad_heads(ukv[:, :, :QK_NOPE].reshape(kv_w, -1), QK_NOPE)
    wv_p = _pad_heads(ukv[:, :, QK_NOPE:].reshape(kv_w, -1), V_DIM)
    wo_p = _pad_heads(full['w_o_attn'].T, V_DIM).T
    conv_b = w['conv_b']

    cmat = jnp.concatenate([c_all[:, 0, :], w['c_ctx'][None, :], jnp.zeros((SUBLANES - 1, d), F32)], axis=0)
    mcols = w['w_mod'].shape[1]
    b_cols = lax.dynamic_slice(w['b_mod'], (0, me * mcols), (1, mcols))
    mod_part = mod_fwd(cmat, w['w_mod'], b_cols, "mod_fwd")
    (mod_all,) = exchange([mod_part], ['gather'], "gather_mod")
    mod_me = lax.dynamic_index_in_dim(mod_all, me, axis=1, keepdims=False).reshape(6, d)
    mod_ctx = mod_all[:, SUBLANES, :].reshape(6, d)
    sh1, sc1, g1, sh2, sc2, g2 = [mod_me[k:k + 1] for k in range(6)]
    mods1 = jnp.concatenate([sc1, sh1, mod_ctx[1:2], mod_ctx[0:1], jnp.zeros((4, d), F32)], axis=0)
    mods2 = jnp.concatenate([sc2, sh2, jnp.zeros((6, d), F32)], axis=0)

    xa = jnp.concatenate([x, ctx], axis=0)
    h = norm_mod_fwd(xa, w['norm1_g'], mods1, n, tm, "norm1_fwd")
    proj = matmul(h, w_in_p, 'nn', F32, "in_proj")
    tabs = _rope_tables(n, nc)
    pad_g = lambda g: jnp.pad(g, ((0, 0), (0, HEAD_PAD - QK_DIM)))
    gains = (w['q_a_g'], w['kv_a_g'], pad_g(w['q_norm_g']), pad_g(w['k_norm_g']))
    q, k, v = qkv_fwd(proj, lay, gains, wuq_p, wk_p, wv_p, tabs, tm, "qkv_fwd")
    o, lse, *wg = attn_fwd(q, k, v, n, "attn_fwd", gather=[w[k].astype(BF16) for k in late_names])
    full.update({k: _from_shards(s, GATHERED[k]) for k, s in zip(late_names, wg)})
    w_glu, w_out, w_up, w_down = full['w_glu'], full['w_out'], full['w_up'], full['w_down']
    conv_w = jnp.pad(full['conv_w'].astype(F32), ((0, SUBLANES - 3), (0, 0)))

    b_t = lambda a: a.transpose(2, 0, 1).reshape(SSM_GROUP, gn)
    c_t = lambda a: a.transpose(1, 0, 2).reshape(SSM_GROUP, gn)
    bre_t, bim_t = b_t(w['b_re']), b_t(w['b_im'])
    ssm_in, prep = [], []
    for sfx in ('f', 'b'):
        lam_re, lam_im = w['lam_re_' + sfx].reshape(1, gn), w['lam_im_' + sfx].reshape(1, gn)
        logdt = jnp.repeat(w['log_dt_' + sfx], SSM_STATE, axis=1)
        ssm_in.append((lam_re, lam_im, logdt))
        prep.append(ssm_prep(lam_re, lam_im, logdt, bre_t, bim_t, "ssm_prep_" + sfx))

    def blk_b(bb):
        return _block_diag_rows(bb, n_grp).astype(BF16)

    def blk_c(cc):
        return _block_diag_cols(cc.transpose(0, 2, 1).reshape(gn, SSM_GROUP), n_grp).astype(BF16)

    t_len = _tile(math.gcd(n, nc), (256, 128))
    ssm = []
    for di, sfx in enumerate(('f', 'b')):
        reverse = di == 1
        pwr, pwi, bbr, bbi = prep[di]
        ssm.append(dict(
            sfx=sfx, reverse=reverse, blocks=(blk_b(bbr), blk_b(bbi), blk_c(w['c_re_' + sfx]),
                                              blk_c(-w['c_im_' + sfx])),
            tabs=_scan_tables(pwr, pwi, reverse), adj_tabs=_scan_tables(pwr, -pwi, not reverse)))
    for s in ssm:
        s['y'], s['xb'] = ssm_fwd(proj, lay, *s['blocks'], s['tabs'], n, t_len, s['reverse'], "ssm_fwd_" + s['sfx'])
    yf, yb = ssm[0]['y'], ssm[1]['y']
    s_l = glu_fwd(proj, lay, yf, yb, w['d_skip'], w_glu, n, tm, "glu_fwd")
    x1 = merge_fwd(o, s_l, proj, xa, g1, wo_p, w_out, n, tm, "merge_fwd")

    h2 = norm_mod_fwd(x1, w['norm2_g'], mods2, n, tm, "norm2_fwd")
    up = matmul(h2, w_up, 'nn', F32, "up_proj")
    tw = _tile(n, (128,))
    act = conv_act_fwd(up, conv_w, conv_b, tw, "conv_act_fwd")
    dy, ffn, loss_parts = down_loss(act, w_down, x1, g2, target, tm, "down_loss")
    loss = lax.psum(jnp.sum(loss_parts[:, 0, 0]), MESH_AXES)

    dup, dffn, dg2, dconv = ffn_bwd(dy, ffn, up, conv_w, conv_b, w_down, g2, tm, "ffn_bwd")
    g_w_down = matmul(act, dffn, 'tn', F32, "dw_down")
    dh2 = matmul(dup, w_up, 'nt', F32, "dh2")
    g_w_up = matmul(h2, dup, 'tn', F32, "dw_up")
    dx1, dn2g, dmods2 = norm_mod_bwd(x1, dh2, dy, w['norm2_g'], mods2, n, tm, "norm2_bwd")

    do, ds_l, dgl, dg1, g_wo_p, g_w_out = merge_bwd(o, s_l, proj, xa, g1, wo_p, w_out, dx1, n, tm, "merge_bwd")
    du_direct, dyr, dds, g_w_glu = glu_bwd(proj, lay, yf, yb, w['d_skip'], w_glu, ds_l, n, tm, "glu_bwd")
    for s in ssm:
        s['du'], s['gr'], s['gi'], s['xr'], s['xi'], s['dlam'] = ssm_bwd(
            proj, lay, dyr, s['xb'], *s['blocks'], s['tabs'], s['adj_tabs'], n, t_len, s['reverse'],
            "ssm_bwd_" + s['sfx'])

    early_g = {'w_o_attn': _unpad_heads(g_wo_p.T, V_DIM).T, 'w_glu': g_w_glu, 'w_out': g_w_out, 'w_up': g_w_up,
               'conv_w': dconv[0:3], 'w_down': g_w_down}
    dq, dk, dv, *early_recv = attn_bwd(q, k, v, o, do, lse, n, "attn_bwd",
                                       a2a=[_to_shards(g, GATHERED[k]).astype(BF16) for k, g in early_g.items()])
    (dproj, dqag, dkvag, dqng, dkng, g_wuq_p, g_wk_p, g_wv_p) = qkv_bwd(
        proj, lay, gains, wuq_p, wk_p, wv_p, tabs, dq, dk, dv, dgl, du_direct, ssm[0]['du'], ssm[1]['du'], n, tm,
        "qkv_bwd")
    g_w_in_p = matmul(h, dproj, 'tn', F32, "dw_in")
    ukv_g = jnp.concatenate([_unpad_heads(g_wk_p, QK_NOPE).reshape(kv_w, N_HEADS, QK_NOPE),
                             _unpad_heads(g_wv_p, V_DIM).reshape(kv_w, N_HEADS, V_DIM)], axis=2)
    late_g = {'w_in': _unpad_w_in(g_w_in_p, lay), 'w_uq': _unpad_heads(g_wuq_p, QK_DIM),
              'w_ukv': ukv_g.reshape(kv_w, -1)}
    dh, *late_recv = matmul(dproj, w_in_p, 'nt', F32, "dh",
                            a2a=[_to_shards(g, GATHERED[k]).astype(BF16) for k, g in late_g.items()])
    dxa, dn1g, dmods1 = norm_mod_bwd(xa, dh, dx1, w['norm1_g'], mods1, n, tm, "norm1_bwd")
    grad_x = dxa[:n]

    grads = {}
    d_bbar = [None, None]
    ka, kb, gp = sw // SSM_SPLIT, gn // SSM_SPLIT, n_grp // SSM_SPLIT
    products = []
    for s in ssm:
        for nm, a, a_col0, b in (("ssm_db_re", proj, lay['o_u'], s['gr']), ("ssm_db_im", proj, lay['o_u'], s['gi']),
                                 ("ssm_dc_re", dyr, 0, s['xr']), ("ssm_dc_im", dyr, 0, s['xi'])):
            products += [matmul(a, b, 'tn', F32, "%s_%s%d" % (nm, s['sfx'], p), a_col0=a_col0 + p * ka, a_cols=ka,
                                b_col0=p * kb, b_cols=kb) for p in range(SSM_SPLIT)]
    blocks = [_diag_blocks(m, gp) for m in products]
    for di, s in enumerate(ssm):
        sfx = s['sfx']
        g_bre, g_bim, g_cre, g_cim = [
            jnp.concatenate(blocks[(4 * di + k) * SSM_SPLIT:(4 * di + k + 1) * SSM_SPLIT], axis=0) for k in range(4)]
        grads['c_re_' + sfx] = g_cre
        grads['c_im_' + sfx] = -g_cim
        to_t = lambda a: a.transpose(1, 0, 2).reshape(SSM_GROUP, gn)
        dlam_re = jnp.sum(s['dlam'][0], axis=0, keepdims=True)
        dlam_im = jnp.sum(s['dlam'][1], axis=0, keepdims=True)
        lam_re, lam_im, logdt = ssm_in[di]
        g_lr, g_li, g_dt, g_br, g_bi = ssm_prep_bwd(lam_re, lam_im, logdt, bre_t, bim_t, dlam_re, dlam_im,
                                                    to_t(g_bre), to_t(g_bim), "ssm_prep_bwd_" + sfx)
        grads['lam_re_' + sfx] = g_lr.reshape(n_grp, SSM_STATE)
        grads['lam_im_' + sfx] = g_li.reshape(n_grp, SSM_STATE)
        grads['log_dt_' + sfx] = jnp.sum(g_dt.reshape(n_grp, SSM_STATE), axis=1)[None, :]
        d_bbar[di] = (g_br, g_bi)
    from_t = lambda a: a.reshape(SSM_GROUP, n_grp, SSM_STATE).transpose(1, 2, 0)
    grads['b_re'] = from_t(d_bbar[0][0]) + from_t(d_bbar[1][0])
    grads['b_im'] = from_t(d_bbar[0][1]) + from_t(d_bbar[1][1])

    dmod = jnp.concatenate([dmods1[1:2], dmods1[0:1], dg1[0:1], dmods2[1:2], dmods2[0:1], dg2[0:1]], axis=1)
    dmod_ctx = jnp.concatenate([dmods1[3:4], dmods1[2:3], jnp.zeros((1, 4 * d), F32)], axis=1)
    dm_send = jnp.concatenate([dmod, dmod_ctx, jnp.zeros((SUBLANES - 2, 6 * d), F32)], axis=0)
    (dm_all,) = exchange([dm_send], ['gather'], "gather_dmod")
    g_all = jnp.concatenate([dm_all[:, 0, :], dm_all[:, 1, :]], axis=0)
    cols = lax.dynamic_slice(g_all.reshape(2 * N_DEV, N_DEV, mcols), (0, me, 0), (2 * N_DEV, 1, mcols))[:, 0, :]
    g_w_mod, dcmat, g_b_mod = mod_bwd(cmat, w['w_mod'], cols[:N_DEV], cols[N_DEV:], g_all, "mod_bwd")

    grads.update({'c_ctx': dcmat[SUBLANES], 'b_mod': g_b_mod[0:1], 'norm1_g': dn1g[0:1], 'norm2_g': dn2g[0:1],
                  'q_a_g': dqag[0:1], 'kv_a_g': dkvag[0:1], 'q_norm_g': dqng[0:1, :QK_DIM],
                  'k_norm_g': dkng[0:1, :QK_DIM], 'd_skip': dds[0:1], 'conv_b': dconv[3:4]})
    first = (me == 0).astype(F32)
    rep_parts = [grads[k] * first if k == 'b_mod' else grads[k] for k in REPLICATED]
    rpack, rspans = _pack(rep_parts, SUBLANES)
    my_core = lax.axis_index("c").astype(jnp.int32).reshape(1)
    r_both = jnp.broadcast_to(rpack[None, None], (2, 1) + rpack.shape)
    (from_sibling,) = exchange([r_both], ['others'], "grads_in_chip", group='core')
    r_sum = add_own(r_both, my_core, from_sibling[0], F32, "chip_sum_replicated")[0]
    (r_recv,) = exchange([r_sum], ['gather'], "grads_between_chips", group='chips')

    outs = {}

    def update(parts, k, w_k, m_k, v_k):
        res = reduce_adamw(parts, w_k, m_k, v_k, "adamw_" + k)
        return dict(zip(('grad_', 'delta_', 'new_m_', 'new_v_'), res))

    per_tensor = list(zip(early_g, early_recv)) + list(zip(late_g, late_recv)) + [('w_mod', g_w_mod[None])]
    for k, parts in per_tensor:
        for kind, a in update(parts, k, w[k], strip(inp['m_' + k]), strip(inp['v_' + k])).items():
            outs[kind + k] = a[None]
    rep = lambda prefix: _pack([strip(inp[prefix + k]) for k in REPLICATED], SUBLANES)[0]
    for kind, buf in update(r_recv, "replicated", rep(''), rep('m_'), rep('v_')).items():
        for k, a in zip(REPLICATED, _unpack(buf, rspans, [w[k].shape for k in REPLICATED])):
            outs[kind + k] = a if inp[k].ndim <= 2 else a[None]
    result = [loss, grad_x[None]]
    for kind in ('grad_', 'delta_', 'new_m_', 'new_v_'):
        result += [outs[kind + k] for k in WEIGHT_NAMES]
    return tuple(result)


_ARG_NAMES = (['x', 'c', 'ctx'] + WEIGHT_NAMES + ['loss_target'] + ['m_' + k for k in WEIGHT_NAMES]
              + ['v_' + k for k in WEIGHT_NAMES])


def kernel(*args):
    assert len(args) == len(_ARG_NAMES)
    return _step(dict(zip(_ARG_NAMES, args)))
```

```python
import functools
import math

import jax
import jax.numpy as jnp
from jax import lax
from jax.experimental import pallas as pl
from jax.experimental.pallas import tpu as pltpu

F32 = jnp.float32
BF16 = jnp.bfloat16

N_DEV = 8
MESH_AXES = ("x", "y", "c")
N_HEADS = 8
QK_NOPE = 64
QK_ROPE = 32
QK_DIM = QK_NOPE + QK_ROPE
V_DIM = 64
HEAD_PAD = 128
HEADS_W = N_HEADS * HEAD_PAD
GRID_W = 64
ROPE_THETA = 10000.0
SSM_GROUP = 16
SSM_STATE = 64
EPS = 1e-6
LANES = 128
SUBLANES = 8
PACK_W = 1024
VMEM_LIMIT = 56 * 1024 * 1024
MM_TILES = (1024, 768, 1408, 512, 384, 256, 128)

ADAM_LR = 0.001
ADAM_B1 = 0.9
ADAM_B2 = 0.999
ADAM_EPS = 1e-08
ADAM_WD = 0.01
ADAM_STEP = 10

WEIGHT_NAMES = ['c_ctx', 'w_mod', 'b_mod', 'norm1_g', 'norm2_g', 'w_in', 'q_a_g', 'w_uq', 'kv_a_g', 'w_ukv',
                'q_norm_g', 'k_norm_g', 'w_o_attn', 'lam_re_f', 'lam_im_f', 'log_dt_f', 'c_re_f', 'c_im_f',
                'lam_re_b', 'lam_im_b', 'log_dt_b', 'c_re_b', 'c_im_b', 'b_re', 'b_im', 'd_skip', 'w_glu',
                'w_out', 'w_up', 'conv_w', 'conv_b', 'w_down']
GATHERED = {'w_in': 1, 'w_uq': 1, 'w_ukv': 1, 'w_o_attn': 1, 'w_glu': 1, 'w_out': 0, 'w_up': 1, 'conv_w': 1,
            'w_down': 0}
REPLICATED = [n for n in WEIGHT_NAMES if n not in GATHERED and n != 'w_mod']


def _tile(n, prefs):
    for t in prefs:
        if n % t == 0:
            return t
    return n


def _cparams(sem=None):
    return pltpu.CompilerParams(dimension_semantics=sem, vmem_limit_bytes=VMEM_LIMIT)


@jax.custom_vjp
def bdot(a, w):
    return jnp.dot(a.astype(BF16), w.astype(BF16), preferred_element_type=F32)


def _bdot_fwd(a, w):
    return bdot(a, w), (a, w)


def _bdot_bwd(res, g):
    a, w = res
    gb = g.astype(BF16)
    da = lax.dot_general(gb, w.astype(BF16), (((1,), (1,)), ((), ())), preferred_element_type=F32)
    dw = lax.dot_general(a.astype(BF16), gb, (((0,), (0,)), ((), ())), preferred_element_type=F32)
    return da.astype(a.dtype), dw.astype(w.dtype)


bdot.defvjp(_bdot_fwd, _bdot_bwd)


def _dot_nt(a, b):
    return lax.dot_general(a, b, (((1,), (1,)), ((), ())), preferred_element_type=F32)


def _dot_tn(a, b):
    return lax.dot_general(a, b, (((0,), (0,)), ((), ())), preferred_element_type=F32)


def matmul(a, b, mode, out_dtype, name, a2a=()):
    n_x = len(a2a)
    kinds = ['a2a'] * n_x
    if mode == 'nn':
        (m, k), n = a.shape, b.shape[1]
    elif mode == 'nt':
        (m, k), n = a.shape, b.shape[0]
    else:
        (k, m), n = a.shape, b.shape[1]
    tm = _tile(m, MM_TILES)
    tn = _tile(n, MM_TILES)
    tk = _tile(k, MM_TILES)
    nk = k // tk

    def body(a_ref, b_ref, *rest):
        o_ref, acc_ref = rest[n_x], rest[2 * n_x + 1]
        kk = pl.program_id(2)
        if n_x:
            start, finish = _exchange_ops(rest[:n_x], rest[n_x + 1:2 * n_x + 1], *rest[2 * n_x + 2:], kinds, 'all')
            step = (pl.program_id(0) * (n // tn) + pl.program_id(1)) * nk + kk
            pl.when(step == 0)(start)

        @pl.when(kk == 0)
        def _():
            acc_ref[...] = jnp.zeros_like(acc_ref)

        av, bv = a_ref[...].astype(BF16), b_ref[...].astype(BF16)
        if mode == 'nn':
            acc_ref[...] += jnp.dot(av, bv, preferred_element_type=F32)
        elif mode == 'nt':
            acc_ref[...] += _dot_nt(av, bv)
        else:
            acc_ref[...] += _dot_tn(av, bv)

        @pl.when(kk == nk - 1)
        def _():
            o_ref[...] = acc_ref[...].astype(o_ref.dtype)

        if n_x:
            pl.when(step == (m // tm) * (n // tn) * nk - 1)(finish)

    if mode == 'nn':
        a_spec = pl.BlockSpec((tm, tk), lambda i, j, kk: (i, kk))
        b_spec = pl.BlockSpec((tk, tn), lambda i, j, kk: (kk, j))
    elif mode == 'nt':
        a_spec = pl.BlockSpec((tm, tk), lambda i, j, kk: (i, kk))
        b_spec = pl.BlockSpec((tn, tk), lambda i, j, kk: (j, kk))
    else:
        a_spec = pl.BlockSpec((tk, tm), lambda i, j, kk: (kk, i))
        b_spec = pl.BlockSpec((tk, tn), lambda i, j, kk: (kk, j))
    any_spec = pl.BlockSpec(memory_space=pl.ANY)
    res = pl.pallas_call(
        body, name=name, grid=(m // tm, n // tn, nk),
        in_specs=[a_spec, b_spec] + [any_spec] * n_x,
        out_specs=[pl.BlockSpec((tm, tn), lambda i, j, kk: (i, j))] + [any_spec] * n_x,
        out_shape=[jax.ShapeDtypeStruct((m, n), out_dtype)] + (_exchange_shapes(a2a, kinds, 'all') if n_x else []),
        scratch_shapes=[pltpu.VMEM((tm, tn), F32)] + (_exchange_scratch(n_x, 'all') if n_x else []),
        compiler_params=_cparams(("arbitrary",) * 3 if n_x else ("parallel", "parallel", "arbitrary")),
    )(a, b, *a2a)
    return res if n_x else res[0]


def diag_outer(a, a_col0, a_cols, bs, parts, name):
    k = a.shape[0]
    ka, kb = a_cols // parts, bs[0].shape[1] // parts
    tk = _tile(k, MM_TILES)
    nk = k // tk
    n_b = len(bs)
    assert a_col0 % ka == 0

    def body(a_ref, *rest):
        b_refs, o_refs, acc_refs = rest[:n_b], rest[n_b:2 * n_b], rest[2 * n_b:]
        kk = pl.program_id(1)
        av = a_ref[...].astype(BF16)
        for b_ref, o_ref, acc_ref in zip(b_refs, o_refs, acc_refs):
            @pl.when(kk == 0)
            def _():
                acc_ref[...] = jnp.zeros_like(acc_ref)

            acc_ref[...] += _dot_tn(av, b_ref[...].astype(BF16))

            @pl.when(kk == nk - 1)
            def _():
                o_ref[...] = acc_ref[...]

    return pl.pallas_call(
        body, name=name, grid=(parts, nk),
        in_specs=[pl.BlockSpec((tk, ka), lambda p, kk: (kk, a_col0 // ka + p))]
        + [pl.BlockSpec((tk, kb), lambda p, kk: (kk, p))] * n_b,
        out_specs=[pl.BlockSpec((None, ka, kb), lambda p, kk: (p, 0, 0))] * n_b,
        out_shape=[jax.ShapeDtypeStruct((parts, ka, kb), F32)] * n_b,
        scratch_shapes=[pltpu.VMEM((ka, kb), F32)] * n_b,
        compiler_params=_cparams(("parallel", "arbitrary")),
    )(a, *bs)


N_CHIPS = 4


def _group(group):
    x, y, c = lax.axis_index("x"), lax.axis_index("y"), lax.axis_index("c")
    flips = {'all': [(r & 4, r & 2, r & 1) for r in range(1, 8)],
             'chips': [(0, 1, 0), (1, 0, 0), (1, 1, 0)], 'core': [(0, 0, 1)]}[group]
    index = {'all': lambda px, py, pc: 4 * px + 2 * py + pc, 'chips': lambda px, py, pc: 2 * px + py,
             'core': lambda px, py, pc: pc}[group]
    peers = []
    for fx, fy, fc in flips:
        p = (1 - x if fx else x, 1 - y if fy else y, 1 - c if fc else c)
        peers.append((p, index(*p)))
    return len(flips) + 1, index(x, y, c), peers


def exchange(arrays, kinds, name, group='all'):
    n_arr = len(arrays)

    def body(*refs):
        start, finish = _exchange_ops(refs[:n_arr], refs[n_arr:2 * n_arr], *refs[2 * n_arr:], kinds, group,
                                      own_slot=False)
        start()
        finish()

    any_spec = pl.BlockSpec(memory_space=pl.ANY)
    outs = pl.pallas_call(
        body, name=name,
        in_specs=[any_spec] * n_arr, out_specs=[any_spec] * n_arr, out_shape=_exchange_shapes(arrays, kinds, group),
        scratch_shapes=_exchange_scratch(n_arr, group),
        compiler_params=pltpu.CompilerParams(has_side_effects=True),
    )(*arrays)
    _, me, _ = _group(group)
    own = {'gather': lambda arr: arr, 'a2a': lambda arr: lax.dynamic_index_in_dim(arr, me, 0, keepdims=False)}
    return [out if kind == 'others' else lax.dynamic_update_index_in_dim(out, own[kind](arr), me, 0)
            for arr, kind, out in zip(arrays, kinds, outs)]


GROUP_SIZE = {'all': N_DEV, 'chips': N_CHIPS, 'core': 2}


def _exchange_shapes(arrays, kinds, group):
    size = GROUP_SIZE[group]
    return [jax.ShapeDtypeStruct({'gather': (size,) + tuple(arr.shape), 'a2a': tuple(arr.shape),
                                  'others': (size - 1,) + tuple(arr.shape[1:])}[kind], arr.dtype)
            for arr, kind in zip(arrays, kinds)]


def _exchange_scratch(n_arr, group):
    size = GROUP_SIZE[group]
    return [pltpu.SemaphoreType.DMA((n_arr * size,)), pltpu.SemaphoreType.DMA((n_arr * size,)),
            pltpu.SemaphoreType.DMA((n_arr,))]


def _exchange_ops(srcs, dsts, send_sems, recv_sems, local_sems, kinds, group, own_slot=True):
    n_arr = len(srcs)
    size, me, peers = _group(group)
    has_local = [own_slot and kind != 'others' for kind in kinds]

    def copy(a, r, peer, peer_idx, receiving):
        src = srcs[a] if kinds[a] == 'gather' else srcs[a].at[peer_idx]
        if kinds[a] == 'others':
            dst = dsts[a].at[r]
        else:
            dst = dsts[a].at[peer_idx if receiving else me]
        return pltpu.make_async_remote_copy(
            src_ref=src, dst_ref=dst, send_sem=send_sems.at[a * size + r], recv_sem=recv_sems.at[a * size + r],
            device_id=peer, device_id_type=pl.DeviceIdType.MESH)

    def local(a):
        mine = srcs[a] if kinds[a] == 'gather' else srcs[a].at[me]
        return pltpu.make_async_copy(mine, dsts[a].at[me], local_sems.at[a])

    def start():
        for a in range(n_arr):
            if has_local[a]:
                local(a).start()
            for r, (peer, peer_idx) in enumerate(peers):
                copy(a, r, peer, peer_idx, False).start()

    def finish():
        for a in range(n_arr):
            for r, (peer, peer_idx) in enumerate(peers):
                cp = copy(a, r, peer, peer_idx, True)
                cp.wait_send()
                cp.wait_recv()
            if has_local[a]:
                local(a).wait()

    return start, finish


GATHER_COPIES = 7


def _two_level_gather(srcs, outs, send_sems, recv_sems, local_sems, own_slot=True):
    n_arr = len(srcs)
    x, y, c = lax.axis_index("x"), lax.axis_index("y"), lax.axis_index("c")
    me, sibling = (x, y, c), (x, y, 1 - c)
    chips = [(1 - x, y), (x, 1 - y), (1 - x, 1 - y)]
    slot = lambda p: 4 * p[0] + 2 * p[1] + p[2]

    def copy(a, k, block, to, own=False):
        dst = outs[a].at[slot(block)]
        return pltpu.make_async_remote_copy(
            src_ref=srcs[a] if own else dst, dst_ref=dst,
            send_sem=send_sems.at[a * GATHER_COPIES + k], recv_sem=recv_sems.at[a * GATHER_COPIES + k],
            device_id=to, device_id_type=pl.DeviceIdType.MESH)

    def own_copies(a):
        return [copy(a, 0, me, sibling, own=True)] + [copy(a, 1 + j, me, (*chip, c), own=True)
                                                      for j, chip in enumerate(chips)]

    local = lambda a: pltpu.make_async_copy(srcs[a], outs[a].at[slot(me)], local_sems.at[a])

    def start():
        for a in range(n_arr):
            if own_slot:
                local(a).start()
            for cp in own_copies(a):
                cp.start()

    def finish():
        passed = []
        for j, chip in enumerate(chips):
            for a in range(n_arr):
                copy(a, 1 + j, (*chip, c), me).wait_recv()
                passed.append(copy(a, 4 + j, (*chip, c), sibling))
                passed[-1].start()
        for a in range(n_arr):
            copy(a, 0, sibling, me).wait_recv()
            for j, chip in enumerate(chips):
                copy(a, 4 + j, (*chip, 1 - c), me).wait_recv()
            for cp in own_copies(a):
                cp.wait_send()
        for cp in passed:
            cp.wait_send()
        if own_slot:
            for a in range(n_arr):
                local(a).wait()

    return start, finish


def _gather_scratch(n_arr):
    return [pltpu.SemaphoreType.DMA((n_arr * GATHER_COPIES,)), pltpu.SemaphoreType.DMA((n_arr * GATHER_COPIES,)),
            pltpu.SemaphoreType.DMA((n_arr,))]


def gather_two_level(arrays, name):
    n_arr = len(arrays)

    def body(*refs):
        start, finish = _two_level_gather(refs[:n_arr], refs[n_arr:2 * n_arr], *refs[2 * n_arr:], own_slot=False)
        start()
        finish()

    any_spec = pl.BlockSpec(memory_space=pl.ANY)
    outs = pl.pallas_call(
        body, name=name,
        in_specs=[any_spec] * n_arr, out_specs=[any_spec] * n_arr,
        out_shape=[jax.ShapeDtypeStruct((N_DEV,) + tuple(a.shape), a.dtype) for a in arrays],
        scratch_shapes=_gather_scratch(n_arr),
        compiler_params=pltpu.CompilerParams(has_side_effects=True),
    )(*arrays)
    _, me, _ = _group('all')
    return [lax.dynamic_update_index_in_dim(out, arr, me, 0) for arr, out in zip(arrays, outs)]


def _pack(parts, row_mult):
    rows, spans, r = [], [], 0
    for p in parts:
        flat = p.reshape(-1)
        nr = -(-flat.shape[0] // (PACK_W * row_mult)) * row_mult
        flat = jnp.pad(flat, (0, nr * PACK_W - flat.shape[0]))
        rows.append(flat.reshape(nr, PACK_W))
        spans.append((r, nr))
        r += nr
    return jnp.concatenate(rows, axis=0), spans


def _unpack(buf, spans, shapes):
    out = []
    for (r, nr), shp in zip(spans, shapes):
        size = math.prod(shp)
        out.append(buf[..., r:r + nr, :].reshape(buf.shape[:-2] + (nr * PACK_W,))[..., :size]
                   .reshape(buf.shape[:-2] + tuple(shp)))
    return out


def _to_shards(full, axis):
    r, c = full.shape
    if axis == 0:
        return full.reshape(N_DEV, r // N_DEV, c)
    return full.reshape(r, N_DEV, c // N_DEV).transpose(1, 0, 2)


def _from_shards(sh, axis):
    _, r, c = sh.shape
    if axis == 0:
        return sh.reshape(N_DEV * r, c)
    return sh.transpose(1, 0, 2).reshape(r, N_DEV * c)


def _rms(x, g, n):
    ms = jnp.sum(x * x, axis=-1, keepdims=True) * (1.0 / n)
    return x * lax.rsqrt(ms + EPS) * g


def _norm_mod_fn(x, g, sc, sh):
    return _rms(x, g, x.shape[-1]) * (1.0 + sc) + sh


@jax.custom_vjp
def _rope(t, ct, s1, s2):
    return t * ct + pltpu.roll(t, 16, 1) * s1 + pltpu.roll(t, HEAD_PAD - 16, 1) * s2


def _rope_fwd(t, ct, s1, s2):
    return _rope(t, ct, s1, s2), (ct, s1, s2)


def _rope_bwd(res, d):
    ct, s1, s2 = res
    dt = d * ct + pltpu.roll(d * s1, HEAD_PAD - 16, 1) + pltpu.roll(d * s2, 16, 1)
    return dt, jnp.zeros_like(ct), jnp.zeros_like(s1), jnp.zeros_like(s2)


_rope.defvjp(_rope_fwd, _rope_bwd)


def _qkv_fn(cq, ckv, krsec, qag, kvag, qng, kng, wuq, wk, wv, ct, s1, s2):
    q_raw = bdot(_rms(cq, qag, cq.shape[-1]), wuq)
    ckvn = _rms(ckv, kvag, ckv.shape[-1])
    k_raw = bdot(ckvn, wk)
    v = bdot(ckvn, wv)
    qs, ks = [], []
    for h in range(N_HEADS):
        sl = slice(h * HEAD_PAD, (h + 1) * HEAD_PAD)
        qs.append(_rope(_rms(q_raw[:, sl], qng, QK_DIM), ct, s1, s2))
        ks.append(_rope(_rms(k_raw[:, sl] + krsec, kng, QK_DIM), ct, s1, s2))
    return jnp.concatenate(qs, axis=1), jnp.concatenate(ks, axis=1), v


def _glu_fn(u, yf, yb, dskip, wglu):
    y = u * dskip + yf + yb
    vg = bdot(jax.nn.gelu(y), wglu)
    d = vg.shape[-1] // 2
    return vg[:, :d] * jax.nn.sigmoid(vg[:, d:])


def _merge_fn(o, s_l, gl, x, g1, wo, wout):
    d = x.shape[-1]
    a = bdot(o, wo)
    mix = jax.nn.sigmoid(gl[:, :d]) * a + jax.nn.sigmoid(gl[:, d:]) * s_l
    return x + g1 * bdot(mix, wout)


def _mod_fn(cmat, w):
    return bdot(jax.nn.silu(cmat), w)


def _ssm_prep_fn(lam_re, lam_im, logdt, bre, bim):
    dt = jnp.exp(logdt)
    ar, ai = lam_re * dt, lam_im * dt
    e = jnp.exp(ar)
    lbr, lbi = e * jnp.cos(ai), e * jnp.sin(ai)
    nr, ni = lbr - 1.0, lbi
    den = lam_re * lam_re + lam_im * lam_im
    qr = (nr * lam_re + ni * lam_im) / den
    qi = (ni * lam_re - nr * lam_im) / den
    return lbr, lbi, qr * bre - qi * bim, qr * bim + qi * bre


def _rows(tm, w, col=0):
    return pl.BlockSpec((tm, w), lambda i: (i, col))


def _full(shape):
    nd = len(shape)
    return pl.BlockSpec(tuple(shape), lambda i: (0,) * nd)


def _acc_add(i, ref, val):
    @pl.when(i == 0)
    def _():
        ref[...] = jnp.zeros_like(ref)
    ref[...] += val


def _pad_rows(v, rows=SUBLANES):
    sel = lax.broadcasted_iota(jnp.int32, (rows, v.shape[-1]), 0) == 0
    return jnp.where(sel, jnp.broadcast_to(v, (rows, v.shape[-1])), 0.0)


def norm_mod_fwd(xa, g, mods, n_lat, tm, name):
    r, d = xa.shape
    lat_tiles = n_lat // tm

    def body(x_ref, g_ref, m_ref, o_ref):
        lat = pl.program_id(0) < lat_tiles
        sc = jnp.where(lat, m_ref[0:1, :], m_ref[2:3, :])
        sh = jnp.where(lat, m_ref[1:2, :], m_ref[3:4, :])
        o_ref[...] = _norm_mod_fn(x_ref[...], g_ref[...], sc, sh).astype(o_ref.dtype)

    return pl.pallas_call(
        body, name=name, grid=(r // tm,),
        in_specs=[_rows(tm, d), _full(g.shape), _full(mods.shape)],
        out_specs=_rows(tm, d), out_shape=jax.ShapeDtypeStruct((r, d), BF16),
        compiler_params=_cparams(("parallel",)),
    )(xa, g, mods)


def norm_mod_bwd(xa, dh, dres, g, mods, n_lat, tm, name):
    r, d = xa.shape
    lat_tiles = n_lat // tm

    def body(x_ref, dh_ref, dres_ref, g_ref, m_ref, dx_ref, dg_ref, dm_ref):
        i = pl.program_id(0)
        lat = i < lat_tiles
        sc = jnp.where(lat, m_ref[0:1, :], m_ref[2:3, :])
        sh = jnp.where(lat, m_ref[1:2, :], m_ref[3:4, :])
        _, vjp = jax.vjp(_norm_mod_fn, x_ref[...], g_ref[...], sc, sh)
        dx, dg, dsc, dsh = vjp(dh_ref[...])
        dx_ref[...] = dx + jnp.where(lat, dres_ref[...], 0.0)
        _acc_add(i, dg_ref, _pad_rows(dg))
        row = lax.broadcasted_iota(jnp.int32, (SUBLANES, d), 0)
        base = jnp.where(lat, 0, 2)
        upd = jnp.where(row == base, jnp.broadcast_to(dsc, (SUBLANES, d)), 0.0)
        upd = upd + jnp.where(row == base + 1, jnp.broadcast_to(dsh, (SUBLANES, d)), 0.0)
        _acc_add(i, dm_ref, upd)

    return pl.pallas_call(
        body, name=name, grid=(r // tm,),
        in_specs=[_rows(tm, d), _rows(tm, d),
                  pl.BlockSpec((tm, d), lambda i: (jnp.minimum(i, lat_tiles - 1), 0)),
                  _full(g.shape), _full(mods.shape)],
        out_specs=[_rows(tm, d), _full((SUBLANES, d)), _full((SUBLANES, d))],
        out_shape=[jax.ShapeDtypeStruct((r, d), F32), jax.ShapeDtypeStruct((SUBLANES, d), F32),
                   jax.ShapeDtypeStruct((SUBLANES, d), F32)],
        compiler_params=_cparams(("arbitrary",)),
    )(xa, dh, dres, g, mods)


def qkv_fwd(proj, lay, gains, wuq, wk, wv, tabs, tm, name):
    r = proj.shape[0]
    q_w, kv_w = lay['q'], lay['kv']

    def body(cq_ref, ckv_ref, kr_ref, qag, kvag, qng, kng, wuq_ref, wk_ref, wv_ref, ct, s1, s2, q_ref, k_ref,
             v_ref):
        q, k, v = _qkv_fn(cq_ref[...], ckv_ref[...], kr_ref[...], qag[...], kvag[...], qng[...], kng[...],
                          wuq_ref[...], wk_ref[...], wv_ref[...], ct[...], s1[...], s2[...])
        q_ref[...] = q.astype(BF16)
        k_ref[...] = k.astype(BF16)
        v_ref[...] = v.astype(BF16)

    out = jax.ShapeDtypeStruct((r, HEADS_W), BF16)
    return pl.pallas_call(
        body, name=name, grid=(r // tm,),
        in_specs=[_rows(tm, q_w, lay['o_cq'] // q_w), _rows(tm, kv_w, lay['o_ckv'] // kv_w),
                  _rows(tm, LANES, lay['o_kr'] // LANES)]
        + [_full(a.shape) for a in gains] + [_full(wuq.shape), _full(wk.shape), _full(wv.shape)]
        + [_rows(tm, HEAD_PAD)] * 3,
        out_specs=[_rows(tm, HEADS_W)] * 3, out_shape=[out, out, out],
        compiler_params=_cparams(("parallel",)),
    )(proj, proj, proj, *gains, wuq, wk, wv, *tabs)


def qkv_bwd(proj, lay, gains, wuq, wk, wv, tabs, dq, dk, dv, dgl, du_direct, du_f, du_b, n, tm, name):
    r = proj.shape[0]
    q_w, kv_w, sw, d = lay['q'], lay['kv'], lay['sw'], lay['d']
    lat_tiles = n // tm

    def body(cq_ref, ckv_ref, kr_ref, qag, kvag, qng, kng, wuq_ref, wk_ref, wv_ref, ct, s1, s2, dq_ref, dk_ref,
             dv_ref, dgl_ref, dud_ref, duf_ref, dub_ref, dp_ref, dqag, dkvag, dqng, dkng, dwuq, dwk, dwv):
        i = pl.program_id(0)
        lat = i < lat_tiles
        tables = (ct[...], s1[...], s2[...])
        fn = lambda *a: _qkv_fn(*a, *tables)
        _, vjp = jax.vjp(fn, cq_ref[...], ckv_ref[...], kr_ref[...], qag[...], kvag[...], qng[...], kng[...],
                         wuq_ref[...].astype(F32), wk_ref[...].astype(F32), wv_ref[...].astype(F32))
        g = vjp((jnp.where(lat, dq_ref[...], 0.0), dk_ref[...], dv_ref[...]))
        dp_ref[:, 0:2 * d] = jnp.where(lat, dgl_ref[...], 0.0).astype(BF16)
        dp_ref[:, lay['o_u']:lay['o_u'] + sw] = (duf_ref[...] + dub_ref[...]
                                                 + jnp.where(lat, dud_ref[...], 0.0)).astype(BF16)
        dp_ref[:, lay['o_ckv']:lay['o_ckv'] + kv_w] = g[1].astype(BF16)
        dp_ref[:, lay['o_kr']:lay['o_kr'] + LANES] = g[2].astype(BF16)
        hole0 = lay['o_kr'] + LANES
        if lay['o_cq'] > hole0:
            dp_ref[:, hole0:lay['o_cq']] = jnp.zeros((tm, lay['o_cq'] - hole0), BF16)
        dp_ref[:, lay['o_cq']:lay['o_cq'] + q_w] = g[0].astype(BF16)
        for ref, val in zip((dqag, dkvag, dqng, dkng), g[3:7]):
            _acc_add(i, ref, _pad_rows(val))
        for ref, val in zip((dwuq, dwk, dwv), g[7:10]):
            _acc_add(i, ref, val)

    def lat_rows(w):
        return pl.BlockSpec((tm, w), lambda i: (jnp.minimum(i, lat_tiles - 1), 0))

    acc_shapes = [(SUBLANES, a.shape[1]) for a in gains] + [wuq.shape, wk.shape, wv.shape]
    return pl.pallas_call(
        body, name=name, grid=(r // tm,),
        in_specs=[_rows(tm, q_w, lay['o_cq'] // q_w), _rows(tm, kv_w, lay['o_ckv'] // kv_w),
                  _rows(tm, LANES, lay['o_kr'] // LANES)]
        + [_full(a.shape) for a in gains] + [_full(wuq.shape), _full(wk.shape), _full(wv.shape)]
        + [_rows(tm, HEAD_PAD)] * 3 + [lat_rows(HEADS_W), _rows(tm, HEADS_W), _rows(tm, HEADS_W)]
        + [lat_rows(2 * d), lat_rows(sw), _rows(tm, sw), _rows(tm, sw)],
        out_specs=[_rows(tm, lay['width'])] + [_full(s) for s in acc_shapes],
        out_shape=[jax.ShapeDtypeStruct((r, lay['width']), BF16)] + [jax.ShapeDtypeStruct(s, F32) for s in acc_shapes],
        compiler_params=_cparams(("arbitrary",)),
    )(proj, proj, proj, *gains, wuq, wk, wv, *tabs, dq, dk, dv, dgl, du_direct, du_f, du_b)


def glu_fwd(proj, lay, yf, yb, dskip, wglu, n, tm, name):
    sw, d = wglu.shape[0], wglu.shape[1] // 2

    def body(u_ref, yf_ref, yb_ref, ds_ref, w_ref, o_ref):
        o_ref[...] = _glu_fn(u_ref[...], yf_ref[...], yb_ref[...], ds_ref[...], w_ref[...])

    return pl.pallas_call(
        body, name=name, grid=(n // tm,),
        in_specs=[_rows(tm, sw, lay['o_u'] // sw), _rows(tm, sw), _rows(tm, sw), _full(dskip.shape),
                  _full(wglu.shape)],
        out_specs=_rows(tm, d), out_shape=jax.ShapeDtypeStruct((n, d), F32),
        compiler_params=_cparams(("parallel",)),
    )(proj, yf, yb, dskip, wglu)


def glu_bwd(proj, lay, yf, yb, dskip, wglu, ds_l, n, tm, name):
    sw, d = wglu.shape[0], wglu.shape[1] // 2

    def body(u_ref, yf_ref, yb_ref, ds_ref, w_ref, g_ref, du_ref, dy_ref, dds_ref, dw_ref):
        i = pl.program_id(0)
        _, vjp = jax.vjp(_glu_fn, u_ref[...], yf_ref[...], yb_ref[...], ds_ref[...], w_ref[...].astype(F32))
        du, dyf, _, dds, dw = vjp(g_ref[...])
        du_ref[...] = du
        dy_ref[...] = dyf
        _acc_add(i, dds_ref, _pad_rows(dds))
        _acc_add(i, dw_ref, dw)

    return pl.pallas_call(
        body, name=name, grid=(n // tm,),
        in_specs=[_rows(tm, sw, lay['o_u'] // sw), _rows(tm, sw), _rows(tm, sw), _full(dskip.shape),
                  _full(wglu.shape), _rows(tm, d)],
        out_specs=[_rows(tm, sw), _rows(tm, sw), _full((SUBLANES, sw)), _full(wglu.shape)],
        out_shape=[jax.ShapeDtypeStruct((n, sw), F32), jax.ShapeDtypeStruct((n, sw), F32),
                   jax.ShapeDtypeStruct((SUBLANES, sw), F32), jax.ShapeDtypeStruct(wglu.shape, F32)],
        compiler_params=_cparams(("arbitrary",)),
    )(proj, yf, yb, dskip, wglu, ds_l)


def merge_fwd(o, s_l, proj, xa, g1, wo, wout, n, tm, name):
    d = xa.shape[1]

    def body(o_ref, s_ref, gl_ref, x_ref, g1_ref, wo_ref, wout_ref, x1_ref):
        x1_ref[...] = _merge_fn(o_ref[...], s_ref[...], gl_ref[...], x_ref[...], g1_ref[...], wo_ref[...],
                                wout_ref[...])

    return pl.pallas_call(
        body, name=name, grid=(n // tm,),
        in_specs=[_rows(tm, HEADS_W), _rows(tm, d), _rows(tm, 2 * d), _rows(tm, d), _full(g1.shape),
                  _full(wo.shape), _full(wout.shape)],
        out_specs=_rows(tm, d), out_shape=jax.ShapeDtypeStruct((n, d), F32),
        compiler_params=_cparams(("parallel",)),
    )(o, s_l, proj, xa, g1, wo, wout)


def merge_bwd(o, s_l, proj, xa, g1, wo, wout, dx1, n, tm, name):
    d = xa.shape[1]

    def body(o_ref, s_ref, gl_ref, x_ref, g1_ref, wo_ref, wout_ref, dx1_ref, do_ref, ds_ref, dgl_ref, dg1_ref,
             dwo_ref, dwout_ref):
        i = pl.program_id(0)
        _, vjp = jax.vjp(_merge_fn, o_ref[...], s_ref[...], gl_ref[...], x_ref[...], g1_ref[...],
                         wo_ref[...].astype(F32), wout_ref[...].astype(F32))
        do, ds, dgl, _, dg1, dwo, dwout = vjp(dx1_ref[...])
        do_ref[...] = do
        ds_ref[...] = ds
        dgl_ref[...] = dgl
        _acc_add(i, dg1_ref, _pad_rows(dg1))
        _acc_add(i, dwo_ref, dwo)
        _acc_add(i, dwout_ref, dwout)

    return pl.pallas_call(
        body, name=name, grid=(n // tm,),
        in_specs=[_rows(tm, HEADS_W), _rows(tm, d), _rows(tm, 2 * d), _rows(tm, d), _full(g1.shape),
                  _full(wo.shape), _full(wout.shape), _rows(tm, d)],
        out_specs=[_rows(tm, HEADS_W), _rows(tm, d), _rows(tm, 2 * d), _full((SUBLANES, d)), _full(wo.shape),
                   _full(wout.shape)],
        out_shape=[jax.ShapeDtypeStruct((n, HEADS_W), BF16), jax.ShapeDtypeStruct((n, d), F32),
                   jax.ShapeDtypeStruct((n, 2 * d), F32), jax.ShapeDtypeStruct((SUBLANES, d), F32),
                   jax.ShapeDtypeStruct(wo.shape, F32), jax.ShapeDtypeStruct(wout.shape, F32)],
        compiler_params=_cparams(("arbitrary",)),
    )(o, s_l, proj, xa, g1, wo, wout, dx1)


def _halo_specs(tm, w, n):
    nb = n // SUBLANES
    per = tm // SUBLANES
    prev = pl.BlockSpec((SUBLANES, w), lambda i: (jnp.maximum(i * per - 1, 0), 0))
    nxt = pl.BlockSpec((SUBLANES, w), lambda i: (jnp.minimum((i + 1) * per, nb - 1), 0))
    return prev, nxt


def _shifted(t, prev_blk, next_blk, i, n_tiles):
    tm = t.shape[0]
    row = lax.broadcasted_iota(jnp.int32, t.shape, 0)
    prev_row = jnp.where(i > 0, prev_blk[SUBLANES - 1:SUBLANES, :], 0.0)
    next_row = jnp.where(i < n_tiles - 1, next_blk[0:1, :], 0.0)
    before = jnp.where(row == 0, prev_row, pltpu.roll(t, 1, 0))
    after = jnp.where(row == tm - 1, next_row, pltpu.roll(t, tm - 1, 0))
    return before, after


def _conv_u2(up, before, after, cw, cb):
    return before * cw[0:1, :] + up * cw[1:2, :] + after * cw[2:3, :] + cb


def conv_act_fwd(up, cw, cb, tm, name):
    n, w2 = up.shape
    f = w2 // 2
    n_tiles = n // tm
    prev, nxt = _halo_specs(tm, w2, n)

    def body(up_ref, prev_ref, next_ref, cw_ref, cb_ref, act_ref):
        i = pl.program_id(0)
        t = up_ref[...]
        before, after = _shifted(t, prev_ref[...], next_ref[...], i, n_tiles)
        u2 = _conv_u2(t, before, after, cw_ref[...], cb_ref[...])
        act_ref[...] = (jax.nn.silu(u2[:, f:]) * u2[:, :f]).astype(BF16)

    return pl.pallas_call(
        body, name=name, grid=(n_tiles,),
        in_specs=[_rows(tm, w2), prev, nxt, _full(cw.shape), _full(cb.shape)],
        out_specs=_rows(tm, f), out_shape=jax.ShapeDtypeStruct((n, f), BF16),
        compiler_params=_cparams(("parallel",)),
    )(up, up, up, cw, cb)


def down_loss(act, wdown, x1, g2, target, tm, name):
    n, d = x1.shape
    f = act.shape[1]

    def body(act_ref, w_ref, x1_ref, g2_ref, t_ref, dy_ref, ffn_ref, loss_ref):
        ffn = jnp.dot(act_ref[...], w_ref[...], preferred_element_type=F32)
        err = x1_ref[...] + g2_ref[...] * ffn - t_ref[...]
        ffn_ref[...] = ffn
        dy_ref[...] = err * (1.0 / d)
        part = 0.5 * jnp.sum(jnp.sum(err * err, axis=-1, keepdims=True) * (1.0 / d), axis=0, keepdims=True)
        loss_ref[0] = jnp.broadcast_to(part, (SUBLANES, LANES))

    return pl.pallas_call(
        body, name=name, grid=(n // tm,),
        in_specs=[_rows(tm, f), _full(wdown.shape), _rows(tm, d), _full(g2.shape), _rows(tm, d)],
        out_specs=[_rows(tm, d), _rows(tm, d), pl.BlockSpec((1, SUBLANES, LANES), lambda i: (i, 0, 0))],
        out_shape=[jax.ShapeDtypeStruct((n, d), F32), jax.ShapeDtypeStruct((n, d), F32),
                   jax.ShapeDtypeStruct((n // tm, SUBLANES, LANES), F32)],
        compiler_params=_cparams(("parallel",)),
    )(act, wdown, x1, g2, target)


FFN_BWD_PARTS = 2


def ffn_bwd(dy, ffn, up, cw, cb, wdown, g2, tm, name):
    n, d = dy.shape
    w2 = up.shape[1]
    f = w2 // 2
    fc = f // FFN_BWD_PARTS
    assert fc % LANES == 0
    n_tiles = n // tm
    ext = tm + 2 * SUBLANES
    inner = slice(SUBLANES, SUBLANES + tm)
    prev_w, next_w = _halo_specs(tm, w2, n)
    prev_d, next_d = _halo_specs(tm, d, n)

    def body(dy_ref, dyp_ref, dyn_ref, ffn_ref, up_ref, upp_ref, upn_ref, cw_ref, cb_ref, w_ref, g2_ref,
             dup_ref, dffn_ref, dg2_ref, dcw_ref):
        i = pl.program_id(0)
        has_prev, has_next = i > 0, i < n_tiles - 1

        def extended(prev, tile, nxt):
            return jnp.concatenate([jnp.where(has_prev, prev, 0.0), tile, jnp.where(has_next, nxt, 0.0)], axis=0)

        dyv = dy_ref[...]
        dffn = extended(dyp_ref[...], dyv, dyn_ref[...]) * g2_ref[...]
        dffn_ref[...] = dffn[inner].astype(BF16)
        dffn = dffn.astype(BF16)
        _acc_add(i, dg2_ref, _pad_rows(jnp.sum(dyv * ffn_ref[...], axis=0, keepdims=True)))
        row = lax.broadcasted_iota(jnp.int32, (SUBLANES, fc), 0)
        shift = lambda t: (pltpu.roll(t, 1, 0), pltpu.roll(t, ext - 1, 0))
        for part in range(FFN_BWD_PARTS):
            halves = []
            for col0 in (part * fc, f + part * fc):
                cols = slice(col0, col0 + fc)
                t = extended(upp_ref[:, cols], up_ref[:, cols], upn_ref[:, cols])
                before, after = shift(t)
                halves.append((cols, t, _conv_u2(t, before, after, cw_ref[:, cols], cb_ref[:, cols])))
            (_, _, val), (_, _, gate) = halves
            dact = _dot_nt(dffn, w_ref[part * fc:(part + 1) * fc, :])
            sg = jax.nn.sigmoid(gate)
            for (cols, t, _), du2 in zip(halves, (dact * (gate * sg),
                                                   dact * val * (sg * (1.0 + gate * (1.0 - sg))))):
                before, after = shift(du2)
                cwv = cw_ref[:, cols]
                dup_ref[:, cols] = (after * cwv[0:1, :] + du2 * cwv[1:2, :] + before * cwv[2:3, :])[inner].astype(BF16)
                upd = jnp.zeros((SUBLANES, fc), F32)
                for k, term in enumerate((after * t, du2 * t, before * t, du2)):
                    upd = upd + jnp.where(row == k, jnp.broadcast_to(
                        jnp.sum(term[inner], axis=0, keepdims=True), (SUBLANES, fc)), 0.0)

                @pl.when(i == 0)
                def _():
                    dcw_ref[:, cols] = jnp.zeros((SUBLANES, fc), F32)

                dcw_ref[:, cols] += upd

    return pl.pallas_call(
        body, name=name, grid=(n_tiles,),
        in_specs=[_rows(tm, d), prev_d, next_d, _rows(tm, d), _rows(tm, w2), prev_w, next_w, _full(cw.shape),
                  _full(cb.shape), _full(wdown.shape), _full(g2.shape)],
        out_specs=[_rows(tm, w2), _rows(tm, d), _full((SUBLANES, d)), _full((SUBLANES, w2))],
        out_shape=[jax.ShapeDtypeStruct((n, w2), BF16), jax.ShapeDtypeStruct((n, d), BF16),
                   jax.ShapeDtypeStruct((SUBLANES, d), F32), jax.ShapeDtypeStruct((SUBLANES, w2), F32)],
        compiler_params=_cparams(("arbitrary",)),
    )(dy, dy, dy, ffn, up, up, up, cw, cb, wdown, g2)


def attn_fwd(q, k, v, n, name, gather=()):
    nk = k.shape[0]
    tq = _tile(n, (512, 256, 128))
    tk = _tile(nk, (768, 384, 256, 128))
    n_kv = nk // tk
    n_q = n // tq
    n_g = len(gather)
    scale = QK_DIM ** -0.5
    c2 = scale * math.log2(math.e)

    def body(q_ref, k_ref, v_ref, *rest):
        o_ref, lse_ref = rest[n_g:n_g + 2]
        if n_g:
            start, finish = _two_level_gather(rest[:n_g], rest[n_g + 2:2 * n_g + 2], *rest[2 * n_g + 2:])
            step = pl.program_id(0) * n_q + pl.program_id(1)
            pl.when(step == 0)(start)
        qv = q_ref[...]
        ones_col = (lax.broadcasted_iota(jnp.int32, (tk, HEAD_PAD), 1) == V_DIM).astype(BF16)

        def chunk(j, carry):
            m, acc = carry
            rows = pl.ds(pl.multiple_of(j * tk, tk), tk)
            s = _dot_nt(qv, k_ref[rows, :])
            m_new = jnp.maximum(m, jnp.max(s, axis=-1, keepdims=True))
            p = jnp.exp2(s * c2 - m_new * c2)
            alpha = jnp.exp2((m - m_new) * c2)
            pv = jnp.dot(p.astype(BF16), v_ref[rows, :] + ones_col, preferred_element_type=F32)
            return m_new, alpha * acc + pv

        m, acc = lax.fori_loop(0, n_kv, chunk, (jnp.full((tq, 1), -jnp.inf, F32),
                                                jnp.zeros((tq, HEAD_PAD), F32)), unroll=True)
        l = acc[:, V_DIM:V_DIM + 1]
        lane = lax.broadcasted_iota(jnp.int32, (tq, HEAD_PAD), 1)
        o_ref[...] = jnp.where(lane < V_DIM, acc / l, 0.0).astype(BF16)
        lse_ref[...] = jnp.broadcast_to(m * scale + jnp.log(l), (tq, HEAD_PAD))
        if n_g:
            pl.when(step == N_HEADS * n_q - 1)(finish)

    qspec = pl.BlockSpec((tq, HEAD_PAD), lambda h, i: (i, h))
    kspec = pl.BlockSpec((nk, HEAD_PAD), lambda h, i: (0, h))
    any_spec = pl.BlockSpec(memory_space=pl.ANY)
    return pl.pallas_call(
        body, name=name, grid=(N_HEADS, n_q),
        in_specs=[qspec, kspec, kspec] + [any_spec] * n_g, out_specs=[qspec, qspec] + [any_spec] * n_g,
        out_shape=[jax.ShapeDtypeStruct((n, HEADS_W), BF16), jax.ShapeDtypeStruct((n, HEADS_W), F32)]
        + [jax.ShapeDtypeStruct((N_DEV,) + tuple(a.shape), a.dtype) for a in gather],
        scratch_shapes=_gather_scratch(n_g) if n_g else [],
        compiler_params=_cparams(("arbitrary", "arbitrary") if n_g else ("parallel", "parallel")),
    )(q, k, v, *gather)


def attn_bwd(q, k, v, o, do, lse, n, name, a2a=()):
    nk = k.shape[0]
    tq = _tile(n, (512, 256, 128))
    tk = _tile(nk, (768, 384, 256, 128))
    n_kv = nk // tk
    n_x = len(a2a)
    kinds = ['a2a'] * n_x
    scale = QK_DIM ** -0.5
    log2e = math.log2(math.e)
    c2 = scale * log2e

    def body(q_ref, k_ref, v_ref, o_ref, do_ref, lse_ref, *rest):
        dq_ref, dk_ref, dv_ref = rest[n_x:n_x + 3]
        if n_x:
            start, finish = _exchange_ops(rest[:n_x], rest[n_x + 3:2 * n_x + 3], *rest[2 * n_x + 3:], kinds, 'all')
            step = pl.program_id(0) * n_kv + pl.program_id(1)
            pl.when(step == 0)(start)

        @pl.when(pl.program_id(1) == 0)
        def _():
            dq_ref[...] = jnp.zeros_like(dq_ref)

        kv, vv = k_ref[...], v_ref[...]

        def q_tile(i, carry):
            dk, dv = carry
            rows = pl.ds(pl.multiple_of(i * tq, tq), tq)
            qv, dov = q_ref[rows, :], do_ref[rows, :]
            p = jnp.exp2(_dot_nt(qv, kv) * c2 - lse_ref[rows, 0:1] * log2e)
            dv = dv + _dot_tn(p.astype(BF16), dov)
            dp = _dot_nt(dov, vv)
            delta = jnp.sum(dov.astype(F32) * o_ref[rows, :].astype(F32), axis=-1, keepdims=True)
            ds = (p * (dp - delta) * scale).astype(BF16)
            dk = dk + _dot_tn(ds, qv)
            dq_ref[rows, :] += jnp.dot(ds, kv, preferred_element_type=F32)
            return dk, dv

        zero = jnp.zeros((tk, HEAD_PAD), F32)
        dk, dv = lax.fori_loop(0, n // tq, q_tile, (zero, zero), unroll=2)
        dk_ref[...] = dk
        dv_ref[...] = dv
        if n_x:
            pl.when(step == N_HEADS * n_kv - 1)(finish)

    qspec = pl.BlockSpec((n, HEAD_PAD), lambda h, j: (0, h))
    kspec = pl.BlockSpec((tk, HEAD_PAD), lambda h, j: (j, h))
    any_spec = pl.BlockSpec(memory_space=pl.ANY)
    return pl.pallas_call(
        body, name=name, grid=(N_HEADS, n_kv),
        in_specs=[qspec, kspec, kspec, qspec, qspec, qspec] + [any_spec] * n_x,
        out_specs=[qspec, kspec, kspec] + [any_spec] * n_x,
        out_shape=[jax.ShapeDtypeStruct((n, HEADS_W), F32), jax.ShapeDtypeStruct((nk, HEADS_W), F32),
                   jax.ShapeDtypeStruct((nk, HEADS_W), F32)] + _exchange_shapes(a2a, kinds, 'all'),
        scratch_shapes=_exchange_scratch(n_x, 'all') if n_x else [],
        compiler_params=_cparams(("arbitrary", "arbitrary") if n_x else ("parallel", "arbitrary")),
    )(q, k, v, o, do, lse, *a2a)


SCAN_LEVELS = (1, 2, 4)
SCAN_LANES = 512


def _scan_chunk(xr, xi, car, m_ref, p_ref, reverse):
    t_len, gn = xr.shape
    n_slab = t_len // SUBLANES
    for lb in range(gn // SCAN_LANES):
        ls = pl.ds(lb * SCAN_LANES, SCAN_LANES)

        def step(s, carry, ls=ls):
            cr, ci = carry
            slab = (n_slab - 1 - s) if reverse else s
            rows = pl.ds(pl.multiple_of(slab * SUBLANES, SUBLANES), SUBLANES)
            br, bi = xr[rows, ls], xi[rows, ls]
            for lvl, d in enumerate(SCAN_LEVELS):
                shift = SUBLANES - d if reverse else d
                sr, si = pltpu.roll(br, shift, 0), pltpu.roll(bi, shift, 0)
                mr, mi = m_ref[lvl, 0, :, ls], m_ref[lvl, 1, :, ls]
                br, bi = br + mr * sr - mi * si, bi + mr * si + mi * sr
            pr, pi = p_ref[0, :, ls], p_ref[1, :, ls]
            br, bi = br + pr * cr - pi * ci, bi + pr * ci + pi * cr
            xr[rows, ls] = br
            xi[rows, ls] = bi
            last = 0 if reverse else SUBLANES - 1
            return br[last:last + 1, :], bi[last:last + 1, :]

        cr, ci = lax.fori_loop(0, n_slab, step, (car[0:1, ls], car[1:2, ls]))
        car[0:1, ls] = cr
        car[1:2, ls] = ci


SSM_SPLIT = 2


def _bd_dot(a, w_ref):
    k, n = w_ref.shape[0] // SSM_SPLIT, w_ref.shape[1] // SSM_SPLIT
    return jnp.concatenate([jnp.dot(a[:, p * k:(p + 1) * k], w_ref[p * k:(p + 1) * k, p * n:(p + 1) * n],
                                    preferred_element_type=F32) for p in range(SSM_SPLIT)], axis=1)


def _bd_dot_nt(a, w_ref):
    k, n = w_ref.shape[0] // SSM_SPLIT, w_ref.shape[1] // SSM_SPLIT
    return jnp.concatenate([_dot_nt(a[:, p * n:(p + 1) * n], w_ref[p * k:(p + 1) * k, p * n:(p + 1) * n])
                            for p in range(SSM_SPLIT)], axis=1)


def _seq_block(step, n_chunk, lat_chunks, reverse):
    if reverse:
        return n_chunk - 1 - step
    return (step + lat_chunks) % n_chunk


def ssm_fwd(proj, lay, bre, bim, cre, ncim, tabs, n, t_len, reverse, name):
    l, sw = proj.shape[0], lay['sw']
    gn = bre.shape[-1]
    n_chunk, lat_chunks = l // t_len, n // t_len
    blk = lambda k: _seq_block(k, n_chunk, lat_chunks, reverse)
    mtab, ptab = tabs

    def body(u_ref, bre_ref, bim_ref, cre_ref, ncim_ref, m_ref, p_ref, y_ref, xb_ref, xr, xi, car):
        @pl.when(pl.program_id(0) == 0)
        def _():
            car[...] = jnp.zeros_like(car)

        xb_ref[...] = car[...]
        u = u_ref[...].astype(BF16)
        xr[...] = _bd_dot(u, bre_ref)
        xi[...] = _bd_dot(u, bim_ref)
        _scan_chunk(xr, xi, car, m_ref, p_ref, reverse)
        y_ref[...] = _bd_dot(xr[...].astype(BF16), cre_ref) + _bd_dot(xi[...].astype(BF16), ncim_ref)

    return pl.pallas_call(
        body, name=name, grid=(n_chunk,),
        in_specs=[pl.BlockSpec((t_len, sw), lambda k: (blk(k), lay['o_u'] // sw)), _full(bre.shape),
                  _full(bim.shape), _full(cre.shape), _full(ncim.shape), _full(mtab.shape), _full(ptab.shape)],
        out_specs=[pl.BlockSpec((t_len, sw), lambda k: (blk(k), 0)),
                   pl.BlockSpec((None, 2, gn), lambda k: (k, 0, 0))],
        out_shape=[jax.ShapeDtypeStruct((l, sw), F32), jax.ShapeDtypeStruct((n_chunk, 2, gn), F32)],
        scratch_shapes=[pltpu.VMEM((t_len, gn), F32), pltpu.VMEM((t_len, gn), F32), pltpu.VMEM((2, gn), F32)],
        compiler_params=_cparams(("arbitrary",)),
    )(proj, bre, bim, cre, ncim, mtab, ptab)


def ssm_bwd(proj, lay, dyr, xb, bre, bim, cre, ncim, tabs, adj_tabs, n, t_len, reverse, name):
    l, sw = proj.shape[0], lay['sw']
    gn = bre.shape[-1]
    n_chunk, lat_chunks = l // t_len, n // t_len
    fwd_step = lambda k: n_chunk - 1 - k
    blk = lambda k: _seq_block(fwd_step(k), n_chunk, lat_chunks, reverse)
    (mtab, ptab), (mtab_r, ptab_r) = tabs, adj_tabs

    def body(u_ref, dy_ref, xb_ref, bre_ref, bim_ref, cre_ref, ncim_ref, m_ref, p_ref, mr_ref, pr_ref,
             du_ref, gr_ref, gi_ref, xr_ref, xi_ref, dlam_ref, xr, xi, gr, gi, car, acar):
        k = pl.program_id(0)

        @pl.when(k == 0)
        def _():
            acar[...] = jnp.zeros_like(acar)
            dlam_ref[...] = jnp.zeros_like(dlam_ref)

        u = u_ref[...].astype(BF16)
        dy = jnp.where(blk(k) < lat_chunks, dy_ref[...], 0.0).astype(BF16)
        xr[...] = _bd_dot(u, bre_ref)
        xi[...] = _bd_dot(u, bim_ref)
        car[...] = xb_ref[...]
        _scan_chunk(xr, xi, car, m_ref, p_ref, reverse)
        gr[...] = _bd_dot_nt(dy, cre_ref)
        gi[...] = _bd_dot_nt(dy, ncim_ref)
        _scan_chunk(gr, gi, acar, mr_ref, pr_ref, not reverse)
        xrv, xiv, grv, giv = xr[...], xi[...], gr[...], gi[...]
        row = lax.broadcasted_iota(jnp.int32, (t_len, gn), 0)
        first, shift = (t_len - 1, t_len - 1) if reverse else (0, 1)
        xpr = jnp.where(row == first, xb_ref[0:1, :], pltpu.roll(xrv, shift, 0))
        xpi = jnp.where(row == first, xb_ref[1:2, :], pltpu.roll(xiv, shift, 0))
        dlr = grv * xpr + giv * xpi
        dli = giv * xpr - grv * xpi
        dlam_ref[0] += jnp.sum(dlr.reshape(t_len // SUBLANES, SUBLANES, gn), axis=0)
        dlam_ref[1] += jnp.sum(dli.reshape(t_len // SUBLANES, SUBLANES, gn), axis=0)
        grb, gib = grv.astype(BF16), giv.astype(BF16)
        du_ref[...] = _bd_dot_nt(grb, bre_ref) + _bd_dot_nt(gib, bim_ref)
        gr_ref[...] = grb
        gi_ref[...] = gib
        xr_ref[...] = xrv.astype(BF16)
        xi_ref[...] = xiv.astype(BF16)

    def at_blk(width, col=0):
        return pl.BlockSpec((t_len, width), lambda k: (blk(k), col))

    state = jax.ShapeDtypeStruct((l, gn), BF16)
    return pl.pallas_call(
        body, name=name, grid=(n_chunk,),
        in_specs=[at_blk(sw, lay['o_u'] // sw),
                  pl.BlockSpec((t_len, sw), lambda k: (jnp.minimum(blk(k), lat_chunks - 1), 0)),
                  pl.BlockSpec((None, 2, gn), lambda k: (fwd_step(k), 0, 0)),
                  _full(bre.shape), _full(bim.shape), _full(cre.shape), _full(ncim.shape), _full(mtab.shape),
                  _full(ptab.shape), _full(mtab_r.shape), _full(ptab_r.shape)],
        out_specs=[at_blk(sw), at_blk(gn), at_blk(gn), at_blk(gn), at_blk(gn), _full((2, SUBLANES, gn))],
        out_shape=[jax.ShapeDtypeStruct((l, sw), F32), state, state, state, state,
                   jax.ShapeDtypeStruct((2, SUBLANES, gn), F32)],
        scratch_shapes=[pltpu.VMEM((t_len, gn), F32)] * 4 + [pltpu.VMEM((2, gn), F32)] * 2,
        compiler_params=_cparams(("arbitrary",)),
    )(proj, dyr, xb, bre, bim, cre, ncim, mtab, ptab, mtab_r, ptab_r)


def ssm_prep(lam_re, lam_im, logdt, bre, bim, name):
    gn = lam_re.shape[1]

    def body(lr_ref, li_ref, dt_ref, br_ref, bi_ref, pwr_ref, pwi_ref, bbr_ref, bbi_ref):
        _, _, bbr, bbi = _ssm_prep_fn(lr_ref[...], li_ref[...], dt_ref[...], br_ref[...], bi_ref[...])
        bbr_ref[...] = bbr
        bbi_ref[...] = bbi
        kk = (lax.broadcasted_iota(jnp.int32, (SUBLANES, gn), 0) + 1).astype(F32)
        dt = jnp.exp(dt_ref[...])
        ar, ai = lr_ref[...] * dt * kk, li_ref[...] * dt * kk
        e = jnp.exp(ar)
        pwr_ref[...] = e * jnp.cos(ai)
        pwi_ref[...] = e * jnp.sin(ai)

    ins = (lam_re, lam_im, logdt, bre, bim)
    return pl.pallas_call(
        body, name=name,
        out_shape=[jax.ShapeDtypeStruct((SUBLANES, gn), F32)] * 2 + [jax.ShapeDtypeStruct(bre.shape, F32)] * 2,
        compiler_params=_cparams(),
    )(*ins)


def ssm_prep_bwd(lam_re, lam_im, logdt, bre, bim, dlbr, dlbi, dbbr, dbbi, name):
    def body(lr_ref, li_ref, dt_ref, br_ref, bi_ref, g0, g1, g2, g3, o0, o1, o2, o3, o4):
        _, vjp = jax.vjp(_ssm_prep_fn, lr_ref[...], li_ref[...], dt_ref[...], br_ref[...], bi_ref[...])
        for ref, val in zip((o0, o1, o2, o3, o4), vjp((g0[...], g1[...], g2[...], g3[...]))):
            ref[...] = val

    ins = (lam_re, lam_im, logdt, bre, bim)
    return pl.pallas_call(
        body, name=name,
        out_shape=[jax.ShapeDtypeStruct(a.shape, F32) for a in ins],
        compiler_params=_cparams(),
    )(*ins, dlbr, dlbi, dbbr, dbbi)


def mod_fwd(cmat, w, b, name):
    def body(c_ref, w_ref, b_ref, o_ref):
        o_ref[...] = _mod_fn(c_ref[...], w_ref[...]) + b_ref[...]

    return pl.pallas_call(body, name=name, out_shape=jax.ShapeDtypeStruct((cmat.shape[0], w.shape[1]), F32),
                          compiler_params=_cparams())(cmat, w, b)


def mod_bwd(cmat, w, g_lat, g_ctx, g_all, name):
    def body(c_ref, w_ref, gl_ref, gc_ref, ga_ref, dw_ref, dc_ref, db_ref):
        gc = jnp.sum(gc_ref[...], axis=0, keepdims=True)
        dm = jnp.concatenate([gl_ref[...], _pad_rows(gc)], axis=0)
        _, vjp = jax.vjp(_mod_fn, c_ref[...], w_ref[...])
        dc, dw = vjp(dm)
        dw_ref[...] = dw
        dc_ref[...] = dc
        db_ref[...] = _pad_rows(jnp.sum(ga_ref[...], axis=0, keepdims=True))

    return pl.pallas_call(
        body, name=name,
        out_shape=[jax.ShapeDtypeStruct(w.shape, F32), jax.ShapeDtypeStruct(cmat.shape, F32),
                   jax.ShapeDtypeStruct((SUBLANES, g_all.shape[1]), F32)],
        compiler_params=_cparams(),
    )(cmat, w, g_lat, g_ctx, g_all)


def add_own(own, idx, recv, out_dtype, name):
    _, s, r, c = own.shape

    def body(idx_ref, own_ref, recv_ref, o_ref):
        o_ref[...] = (own_ref[...] + recv_ref[...]).astype(o_ref.dtype)

    return pl.pallas_call(
        body, name=name,
        grid_spec=pltpu.PrefetchScalarGridSpec(
            num_scalar_prefetch=1, grid=(s,),
            in_specs=[pl.BlockSpec((None, None, r, c), lambda k, idx_ref: (idx_ref[0], k, 0, 0)),
                      pl.BlockSpec((None, r, c), lambda k, idx_ref: (k, 0, 0))],
            out_specs=pl.BlockSpec((None, r, c), lambda k, idx_ref: (k, 0, 0))),
        out_shape=jax.ShapeDtypeStruct((s, r, c), out_dtype),
        compiler_params=_cparams(("parallel",)),
    )(idx, own, recv)


def reduce_adamw(parts, w, m, v, name):
    s, r, c = parts.shape
    tr = _tile(r, (256, 128, 64, 32, 16, 8))

    def body(p_ref, w_ref, m_ref, v_ref, g_ref, d_ref, nm_ref, nv_ref):
        g = p_ref[0].astype(F32)
        for k in range(1, s):
            g = g + p_ref[k].astype(F32)
        mm = ADAM_B1 * m_ref[...] + (1.0 - ADAM_B1) * g
        vv = ADAM_B2 * v_ref[...] + (1.0 - ADAM_B2) * jnp.square(g)
        m_hat = mm / (1.0 - ADAM_B1 ** ADAM_STEP)
        v_hat = vv / (1.0 - ADAM_B2 ** ADAM_STEP)
        g_ref[...] = g
        d_ref[...] = -ADAM_LR * (m_hat / (jnp.sqrt(v_hat) + ADAM_EPS) + ADAM_WD * w_ref[...])
        nm_ref[...] = mm
        nv_ref[...] = vv

    out = jax.ShapeDtypeStruct((r, c), F32)
    blk = _rows(tr, c)
    return pl.pallas_call(
        body, name=name, grid=(r // tr,),
        in_specs=[pl.BlockSpec((s, tr, c), lambda i: (0, i, 0)), blk, blk, blk],
        out_specs=[blk] * 4, out_shape=[out] * 4,
        compiler_params=_cparams(("parallel",)),
    )(parts, w, m, v)


def _in_layout(d, q, kv, sw):
    o_u = 2 * d
    o_ckv = o_u + sw
    o_kr = o_ckv + kv
    o_cq = -(-(o_kr + LANES) // q) * q
    assert o_u % sw == 0 and o_ckv % kv == 0 and o_kr % LANES == 0
    assert q % LANES == 0 and kv % LANES == 0 and sw % LANES == 0
    return dict(d=d, q=q, kv=kv, sw=sw, o_gl=0, o_u=o_u, o_ckv=o_ckv, o_kr=o_kr, o_cq=o_cq, width=o_cq + q)


def _pad_w_in(w_in, lay):
    q, kv, sw, d = lay['q'], lay['kv'], lay['sw'], lay['d']
    cq, ckv, kr, u, gl = jnp.split(w_in, [q, q + kv, q + kv + QK_ROPE, q + kv + QK_ROPE + sw], axis=1)
    z = lambda w: jnp.zeros((w_in.shape[0], w), w_in.dtype)
    hole = lay['o_cq'] - lay['o_kr'] - LANES
    return jnp.concatenate([gl, u, ckv, z(QK_NOPE), kr, z(LANES - QK_DIM), z(hole), cq], axis=1)


def _unpad_w_in(g, lay):
    q, kv, sw, d = lay['q'], lay['kv'], lay['sw'], lay['d']
    kr0 = lay['o_kr'] + QK_NOPE
    return jnp.concatenate([g[:, lay['o_cq']:lay['o_cq'] + q], g[:, lay['o_ckv']:lay['o_ckv'] + kv],
                            g[:, kr0:kr0 + QK_ROPE], g[:, lay['o_u']:lay['o_u'] + sw], g[:, :2 * d]], axis=1)


def _pad_heads(w, width):
    k = w.shape[0]
    return jnp.pad(w.reshape(k, N_HEADS, width), ((0, 0), (0, 0), (0, HEAD_PAD - width))).reshape(k, HEADS_W)


def _unpad_heads(w, width):
    k = w.shape[0]
    return w.reshape(k, N_HEADS, HEAD_PAD)[:, :, :width].reshape(k, N_HEADS * width)


def _rope_tables(n, nc):
    rows = n // GRID_W
    row = jnp.repeat(jnp.arange(rows), GRID_W)
    col = jnp.tile(jnp.arange(GRID_W), rows)
    pairs = QK_ROPE // 4
    freqs = ROPE_THETA ** (-jnp.arange(pairs, dtype=F32) / pairs)
    ang = jnp.concatenate([row[:, None] * freqs, col[:, None] * freqs], axis=-1)
    cos = jnp.concatenate([jnp.cos(ang), jnp.ones((nc, 2 * pairs), F32)], axis=0)
    sin = jnp.concatenate([jnp.sin(ang), jnp.zeros((nc, 2 * pairs), F32)], axis=0)
    l = n + nc
    half = QK_ROPE // 2
    ct = jnp.concatenate([jnp.ones((l, QK_NOPE), F32), cos, cos, jnp.zeros((l, HEAD_PAD - QK_DIM), F32)], axis=1)
    s1 = jnp.concatenate([jnp.zeros((l, QK_NOPE + half), F32), sin, jnp.zeros((l, HEAD_PAD - QK_DIM), F32)],
                         axis=1)
    s2 = jnp.concatenate([jnp.zeros((l, QK_NOPE), F32), -sin, jnp.zeros((l, HEAD_PAD - QK_NOPE - half), F32)],
                         axis=1)
    return ct, s1, s2


def _group_mask(rows, cols, g):
    return (jnp.arange(rows)[:, None] // (rows // g)) == (jnp.arange(cols)[None, :] // (cols // g))


def _block_diag_rows(m, g):
    return jnp.where(_group_mask(g * m.shape[0], m.shape[1], g), jnp.tile(m, (g, 1)), 0)


def _block_diag_cols(m, g):
    return jnp.where(_group_mask(m.shape[0], g * m.shape[1], g), jnp.tile(m, (1, g)), 0)


def _diag_blocks(m, g):
    a, b = m.shape[0] // g, m.shape[1] // g
    masked = jnp.where(_group_mask(m.shape[0], m.shape[1], g), m, 0)
    return masked.reshape(m.shape[0], g, b).sum(axis=1).reshape(g, a, b)


def _scan_tables(pwr, pwi, reverse):
    row = jnp.arange(SUBLANES)[:, None]
    zero = jnp.zeros_like(pwr)
    levels = []
    for d in SCAN_LEVELS:
        keep = (row < SUBLANES - d) if reverse else (row >= d)
        levels.append(jnp.stack([jnp.where(keep, pwr[d - 1:d, :], zero), jnp.where(keep, pwi[d - 1:d, :], zero)]))
    carry = jnp.stack([pwr[::-1], pwi[::-1]]) if reverse else jnp.stack([pwr, pwi])
    return jnp.stack(levels), carry


def _step(inp):
    x, c, ctx = inp['x'][0], inp['c'], inp['ctx'][0]
    target = inp['loss_target'][0]
    n, d = x.shape
    nc = ctx.shape[0]
    l = n + nc
    q_w, kv_w = inp['q_a_g'].shape[1], inp['kv_a_g'].shape[1]
    sw = inp['d_skip'].shape[1]
    n_grp = sw // SSM_GROUP
    gn = n_grp * SSM_STATE
    lay = _in_layout(d, q_w, kv_w, sw)
    tm = _tile(math.gcd(n, nc), (256, 128))
    me = 4 * lax.axis_index("x") + 2 * lax.axis_index("y") + lax.axis_index("c")
    strip = lambda a: a if a.ndim <= 2 else a[0]
    w = {k: strip(inp[k]) for k in WEIGHT_NAMES}

    gathered_names = list(GATHERED)
    early_names = ['w_in', 'w_uq', 'w_ukv', 'w_o_attn']
    late_names = [k for k in gathered_names if k not in early_names]
    c_all, *wg = gather_two_level([jnp.broadcast_to(c, (SUBLANES, d))] + [w[k].astype(BF16) for k in early_names],
                                  "gather_weights")
    full = {k: _from_shards(s, GATHERED[k]) for k, s in zip(early_names, wg)}

    w_in_p = _pad_w_in(full['w_in'], lay)
    wuq_p = _pad_heads(full['w_uq'], QK_DIM)
    ukv = full['w_ukv'].reshape(kv_w, N_HEADS, QK_NOPE + V_DIM)
    wk_p = _pad_heads(ukv[:, :, :QK_NOPE].reshape(kv_w, -1), QK_NOPE)
    wv_p = _pad_heads(ukv[:, :, QK_NOPE:].reshape(kv_w, -1), V_DIM)
    wo_p = _pad_heads(full['w_o_attn'].T, V_DIM).T
    conv_b = w['conv_b']

    cmat = jnp.concatenate([c_all[:, 0, :], w['c_ctx'][None, :], jnp.zeros((SUBLANES - 1, d), F32)], axis=0)
    mcols = w['w_mod'].shape[1]
    b_cols = lax.dynamic_slice(w['b_mod'], (0, me * mcols), (1, mcols))
    mod_part = mod_fwd(cmat, w['w_mod'], b_cols, "mod_fwd")
    (mod_all,) = exchange([mod_part], ['gather'], "gather_mod")
    mod_me = lax.dynamic_index_in_dim(mod_all, me, axis=1, keepdims=False).reshape(6, d)
    mod_ctx = mod_all[:, SUBLANES, :].reshape(6, d)
    sh1, sc1, g1, sh2, sc2, g2 = [mod_me[k:k + 1] for k in range(6)]
    mods1 = jnp.concatenate([sc1, sh1, mod_ctx[1:2], mod_ctx[0:1], jnp.zeros((4, d), F32)], axis=0)
    mods2 = jnp.concatenate([sc2, sh2, jnp.zeros((6, d), F32)], axis=0)

    xa = jnp.concatenate([x, ctx], axis=0)
    h = norm_mod_fwd(xa, w['norm1_g'], mods1, n, tm, "norm1_fwd")
    proj = matmul(h, w_in_p, 'nn', F32, "in_proj")
    tabs = _rope_tables(n, nc)
    pad_g = lambda g: jnp.pad(g, ((0, 0), (0, HEAD_PAD - QK_DIM)))
    gains = (w['q_a_g'], w['kv_a_g'], pad_g(w['q_norm_g']), pad_g(w['k_norm_g']))
    q, k, v = qkv_fwd(proj, lay, gains, wuq_p, wk_p, wv_p, tabs, tm, "qkv_fwd")
    o, lse, *wg = attn_fwd(q, k, v, n, "attn_fwd", gather=[w[k].astype(BF16) for k in late_names])
    full.update({k: _from_shards(s, GATHERED[k]) for k, s in zip(late_names, wg)})
    w_glu, w_out, w_up, w_down = full['w_glu'], full['w_out'], full['w_up'], full['w_down']
    conv_w = jnp.pad(full['conv_w'].astype(F32), ((0, SUBLANES - 3), (0, 0)))

    b_t = lambda a: a.transpose(2, 0, 1).reshape(SSM_GROUP, gn)
    c_t = lambda a: a.transpose(1, 0, 2).reshape(SSM_GROUP, gn)
    bre_t, bim_t = b_t(w['b_re']), b_t(w['b_im'])
    ssm_in, prep = [], []
    for sfx in ('f', 'b'):
        lam_re, lam_im = w['lam_re_' + sfx].reshape(1, gn), w['lam_im_' + sfx].reshape(1, gn)
        logdt = jnp.repeat(w['log_dt_' + sfx], SSM_STATE, axis=1)
        ssm_in.append((lam_re, lam_im, logdt))
        prep.append(ssm_prep(lam_re, lam_im, logdt, bre_t, bim_t, "ssm_prep_" + sfx))

    def blk_b(bb):
        return _block_diag_rows(bb, n_grp).astype(BF16)

    def blk_c(cc):
        return _block_diag_cols(cc.transpose(0, 2, 1).reshape(gn, SSM_GROUP), n_grp).astype(BF16)

    t_len = _tile(math.gcd(n, nc), (256, 128))
    ssm = []
    for di, sfx in enumerate(('f', 'b')):
        reverse = di == 1
        pwr, pwi, bbr, bbi = prep[di]
        ssm.append(dict(
            sfx=sfx, reverse=reverse, blocks=(blk_b(bbr), blk_b(bbi), blk_c(w['c_re_' + sfx]),
                                              blk_c(-w['c_im_' + sfx])),
            tabs=_scan_tables(pwr, pwi, reverse), adj_tabs=_scan_tables(pwr, -pwi, not reverse)))
    for s in ssm:
        s['y'], s['xb'] = ssm_fwd(proj, lay, *s['blocks'], s['tabs'], n, t_len, s['reverse'], "ssm_fwd_" + s['sfx'])
    yf, yb = ssm[0]['y'], ssm[1]['y']
    s_l = glu_fwd(proj, lay, yf, yb, w['d_skip'], w_glu, n, tm, "glu_fwd")
    x1 = merge_fwd(o, s_l, proj, xa, g1, wo_p, w_out, n, tm, "merge_fwd")

    h2 = norm_mod_fwd(x1, w['norm2_g'], mods2, n, tm, "norm2_fwd")
    up = matmul(h2, w_up, 'nn', F32, "up_proj")
    tw = _tile(n, (128,))
    act = conv_act_fwd(up, conv_w, conv_b, tw, "conv_act_fwd")
    dy, ffn, loss_parts = down_loss(act, w_down, x1, g2, target, tm, "down_loss")
    loss = lax.psum(jnp.sum(loss_parts[:, 0, 0]), MESH_AXES)

    dup, dffn, dg2, dconv = ffn_bwd(dy, ffn, up, conv_w, conv_b, w_down, g2, tm, "ffn_bwd")
    g_w_down = matmul(act, dffn, 'tn', F32, "dw_down")
    dh2 = matmul(dup, w_up, 'nt', F32, "dh2")
    g_w_up = matmul(h2, dup, 'tn', F32, "dw_up")
    dx1, dn2g, dmods2 = norm_mod_bwd(x1, dh2, dy, w['norm2_g'], mods2, n, tm, "norm2_bwd")

    do, ds_l, dgl, dg1, g_wo_p, g_w_out = merge_bwd(o, s_l, proj, xa, g1, wo_p, w_out, dx1, n, tm, "merge_bwd")
    du_direct, dyr, dds, g_w_glu = glu_bwd(proj, lay, yf, yb, w['d_skip'], w_glu, ds_l, n, tm, "glu_bwd")
    for s in ssm:
        s['du'], s['gr'], s['gi'], s['xr'], s['xi'], s['dlam'] = ssm_bwd(
            proj, lay, dyr, s['xb'], *s['blocks'], s['tabs'], s['adj_tabs'], n, t_len, s['reverse'],
            "ssm_bwd_" + s['sfx'])

    early_g = {'w_o_attn': _unpad_heads(g_wo_p.T, V_DIM).T, 'w_glu': g_w_glu, 'w_out': g_w_out, 'w_up': g_w_up,
               'conv_w': dconv[0:3], 'w_down': g_w_down}
    dq, dk, dv, *early_recv = attn_bwd(q, k, v, o, do, lse, n, "attn_bwd",
                                       a2a=[_to_shards(g, GATHERED[k]).astype(BF16) for k, g in early_g.items()])
    (dproj, dqag, dkvag, dqng, dkng, g_wuq_p, g_wk_p, g_wv_p) = qkv_bwd(
        proj, lay, gains, wuq_p, wk_p, wv_p, tabs, dq, dk, dv, dgl, du_direct, ssm[0]['du'], ssm[1]['du'], n, tm,
        "qkv_bwd")
    g_w_in_p = matmul(h, dproj, 'tn', F32, "dw_in")
    ukv_g = jnp.concatenate([_unpad_heads(g_wk_p, QK_NOPE).reshape(kv_w, N_HEADS, QK_NOPE),
                             _unpad_heads(g_wv_p, V_DIM).reshape(kv_w, N_HEADS, V_DIM)], axis=2)
    late_g = {'w_in': _unpad_w_in(g_w_in_p, lay), 'w_uq': _unpad_heads(g_wuq_p, QK_DIM),
              'w_ukv': ukv_g.reshape(kv_w, -1)}
    dh, *late_recv = matmul(dproj, w_in_p, 'nt', F32, "dh",
                            a2a=[_to_shards(g, GATHERED[k]).astype(BF16) for k, g in late_g.items()])
    dxa, dn1g, dmods1 = norm_mod_bwd(xa, dh, dx1, w['norm1_g'], mods1, n, tm, "norm1_bwd")
    grad_x = dxa[:n]

    grads = {}
    d_bbar = [None, None]
    gp = n_grp // SSM_SPLIT
    for di, s in enumerate(ssm):
        sfx = s['sfx']
        products = (diag_outer(proj, lay['o_u'], sw, [s['gr'], s['gi']], SSM_SPLIT, "ssm_db_" + sfx)
                    + diag_outer(dyr, 0, sw, [s['xr'], s['xi']], SSM_SPLIT, "ssm_dc_" + sfx))
        g_bre, g_bim, g_cre, g_cim = [
            jnp.concatenate([_diag_blocks(m[p], gp) for p in range(SSM_SPLIT)], axis=0) for m in products]
        grads['c_re_' + sfx] = g_cre
        grads['c_im_' + sfx] = -g_cim
        to_t = lambda a: a.transpose(1, 0, 2).reshape(SSM_GROUP, gn)
        dlam_re = jnp.sum(s['dlam'][0], axis=0, keepdims=True)
        dlam_im = jnp.sum(s['dlam'][1], axis=0, keepdims=True)
        lam_re, lam_im, logdt = ssm_in[di]
        g_lr, g_li, g_dt, g_br, g_bi = ssm_prep_bwd(lam_re, lam_im, logdt, bre_t, bim_t, dlam_re, dlam_im,
                                                    to_t(g_bre), to_t(g_bim), "ssm_prep_bwd_" + sfx)
        grads['lam_re_' + sfx] = g_lr.reshape(n_grp, SSM_STATE)
        grads['lam_im_' + sfx] = g_li.reshape(n_grp, SSM_STATE)
        grads['log_dt_' + sfx] = jnp.sum(g_dt.reshape(n_grp, SSM_STATE), axis=1)[None, :]
        d_bbar[di] = (g_br, g_bi)
    from_t = lambda a: a.reshape(SSM_GROUP, n_grp, SSM_STATE).transpose(1, 2, 0)
    grads['b_re'] = from_t(d_bbar[0][0]) + from_t(d_bbar[1][0])
    grads['b_im'] = from_t(d_bbar[0][1]) + from_t(d_bbar[1][1])

    dmod = jnp.concatenate([dmods1[1:2], dmods1[0:1], dg1[0:1], dmods2[1:2], dmods2[0:1], dg2[0:1]], axis=1)
    dmod_ctx = jnp.concatenate([dmods1[3:4], dmods1[2:3], jnp.zeros((1, 4 * d), F32)], axis=1)
    dm_send = jnp.concatenate([dmod, dmod_ctx, jnp.zeros((SUBLANES - 2, 6 * d), F32)], axis=0)
    (dm_all,) = exchange([dm_send], ['gather'], "gather_dmod")
    g_all = jnp.concatenate([dm_all[:, 0, :], dm_all[:, 1, :]], axis=0)
    cols = lax.dynamic_slice(g_all.reshape(2 * N_DEV, N_DEV, mcols), (0, me, 0), (2 * N_DEV, 1, mcols))[:, 0, :]
    g_w_mod, dcmat, g_b_mod = mod_bwd(cmat, w['w_mod'], cols[:N_DEV], cols[N_DEV:], g_all, "mod_bwd")

    grads.update({'c_ctx': dcmat[SUBLANES], 'b_mod': g_b_mod[0:1], 'norm1_g': dn1g[0:1], 'norm2_g': dn2g[0:1],
                  'q_a_g': dqag[0:1], 'kv_a_g': dkvag[0:1], 'q_norm_g': dqng[0:1, :QK_DIM],
                  'k_norm_g': dkng[0:1, :QK_DIM], 'd_skip': dds[0:1], 'conv_b': dconv[3:4]})
    first = (me == 0).astype(F32)
    rep_parts = [grads[k] * first if k == 'b_mod' else grads[k] for k in REPLICATED]
    rpack, rspans = _pack(rep_parts, SUBLANES)
    my_core = lax.axis_index("c").astype(jnp.int32).reshape(1)
    r_both = jnp.broadcast_to(rpack[None, None], (2, 1) + rpack.shape)
    (from_sibling,) = exchange([r_both], ['others'], "grads_in_chip", group='core')
    r_sum = add_own(r_both, my_core, from_sibling[0], F32, "chip_sum_replicated")[0]
    (r_recv,) = exchange([r_sum], ['gather'], "grads_between_chips", group='chips')

    outs = {}

    def update(parts, k, w_k, m_k, v_k):
        res = reduce_adamw(parts, w_k, m_k, v_k, "adamw_" + k)
        return dict(zip(('grad_', 'delta_', 'new_m_', 'new_v_'), res))

    per_tensor = list(zip(early_g, early_recv)) + list(zip(late_g, late_recv)) + [('w_mod', g_w_mod[None])]
    for k, parts in per_tensor:
        for kind, a in update(parts, k, w[k], strip(inp['m_' + k]), strip(inp['v_' + k])).items():
            outs[kind + k] = a[None]
    rep = lambda prefix: _pack([strip(inp[prefix + k]) for k in REPLICATED], SUBLANES)[0]
    for kind, buf in update(r_recv, "replicated", rep(''), rep('m_'), rep('v_')).items():
        for k, a in zip(REPLICATED, _unpack(buf, rspans, [w[k].shape for k in REPLICATED])):
            outs[kind + k] = a if inp[k].ndim <= 2 else a[None]
    result = [loss, grad_x[None]]
    for kind in ('grad_', 'delta_', 'new_m_', 'new_v_'):
        result += [outs[kind + k] for k in WEIGHT_NAMES]
    return tuple(result)


_ARG_NAMES = (['x', 'c', 'ctx'] + WEIGHT_NAMES + ['loss_target'] + ['m_' + k for k in WEIGHT_NAMES]
              + ['v_' + k for k in WEIGHT_NAMES])


def kernel(*args):
    assert len(args) == len(_ARG_NAMES)
    return _step(dict(zip(_ARG_NAMES, args)))
```

```python
import functools
import math

import jax
import jax.numpy as jnp
from jax import lax
from jax.experimental import pallas as pl
from jax.experimental.pallas import tpu as pltpu

F32 = jnp.float32
BF16 = jnp.bfloat16

N_DEV = 8
MESH_AXES = ("x", "y", "c")
N_HEADS = 8
QK_NOPE = 64
QK_ROPE = 32
QK_DIM = QK_NOPE + QK_ROPE
V_DIM = 64
HEAD_PAD = 128
HEADS_W = N_HEADS * HEAD_PAD
GRID_W = 64
ROPE_THETA = 10000.0
SSM_GROUP = 16
SSM_STATE = 64
EPS = 1e-6
LANES = 128
SUBLANES = 8
PACK_W = 1024
VMEM_LIMIT = 56 * 1024 * 1024
MM_TILES = (1024, 768, 1408, 512, 384, 256, 128)

ADAM_LR = 0.001
ADAM_B1 = 0.9
ADAM_B2 = 0.999
ADAM_EPS = 1e-08
ADAM_WD = 0.01
ADAM_STEP = 10

WEIGHT_NAMES = ['c_ctx', 'w_mod', 'b_mod', 'norm1_g', 'norm2_g', 'w_in', 'q_a_g', 'w_uq', 'kv_a_g', 'w_ukv',
                'q_norm_g', 'k_norm_g', 'w_o_attn', 'lam_re_f', 'lam_im_f', 'log_dt_f', 'c_re_f', 'c_im_f',
                'lam_re_b', 'lam_im_b', 'log_dt_b', 'c_re_b', 'c_im_b', 'b_re', 'b_im', 'd_skip', 'w_glu',
                'w_out', 'w_up', 'conv_w', 'conv_b', 'w_down']
GATHERED = {'w_in': 1, 'w_uq': 1, 'w_ukv': 1, 'w_o_attn': 1, 'w_glu': 1, 'w_out': 0, 'w_up': 1, 'conv_w': 1,
            'w_down': 0}
REPLICATED = [n for n in WEIGHT_NAMES if n not in GATHERED and n != 'w_mod']


def _tile(n, prefs):
    for t in prefs:
        if n % t == 0:
            return t
    return n


def _cparams(sem=None):
    return pltpu.CompilerParams(dimension_semantics=sem, vmem_limit_bytes=VMEM_LIMIT)


@jax.custom_vjp
def bdot(a, w):
    return jnp.dot(a.astype(BF16), w.astype(BF16), preferred_element_type=F32)


def _bdot_fwd(a, w):
    return bdot(a, w), (a, w)


def _bdot_bwd(res, g):
    a, w = res
    gb = g.astype(BF16)
    da = lax.dot_general(gb, w.astype(BF16), (((1,), (1,)), ((), ())), preferred_element_type=F32)
    dw = lax.dot_general(a.astype(BF16), gb, (((0,), (0,)), ((), ())), preferred_element_type=F32)
    return da.astype(a.dtype), dw.astype(w.dtype)


bdot.defvjp(_bdot_fwd, _bdot_bwd)


def _dot_nt(a, b):
    return lax.dot_general(a, b, (((1,), (1,)), ((), ())), preferred_element_type=F32)


def _dot_tn(a, b):
    return lax.dot_general(a, b, (((0,), (0,)), ((), ())), preferred_element_type=F32)


def matmul(a, b, mode, out_dtype, name, a2a=()):
    n_x = len(a2a)
    kinds = ['a2a'] * n_x
    if mode == 'nn':
        (m, k), n = a.shape, b.shape[1]
    elif mode == 'nt':
        (m, k), n = a.shape, b.shape[0]
    else:
        (k, m), n = a.shape, b.shape[1]
    tm = _tile(m, MM_TILES)
    tn = _tile(n, MM_TILES)
    tk = _tile(k, MM_TILES)
    nk = k // tk

    def body(a_ref, b_ref, *rest):
        o_ref, acc_ref = rest[n_x], rest[2 * n_x + 1]
        kk = pl.program_id(2)
        if n_x:
            start, finish = _exchange_ops(rest[:n_x], rest[n_x + 1:2 * n_x + 1], *rest[2 * n_x + 2:], kinds, 'all')
            step = (pl.program_id(0) * (n // tn) + pl.program_id(1)) * nk + kk
            pl.when(step == 0)(start)

        @pl.when(kk == 0)
        def _():
            acc_ref[...] = jnp.zeros_like(acc_ref)

        av, bv = a_ref[...].astype(BF16), b_ref[...].astype(BF16)
        if mode == 'nn':
            acc_ref[...] += jnp.dot(av, bv, preferred_element_type=F32)
        elif mode == 'nt':
            acc_ref[...] += _dot_nt(av, bv)
        else:
            acc_ref[...] += _dot_tn(av, bv)

        @pl.when(kk == nk - 1)
        def _():
            o_ref[...] = acc_ref[...].astype(o_ref.dtype)

        if n_x:
            pl.when(step == (m // tm) * (n // tn) * nk - 1)(finish)

    if mode == 'nn':
        a_spec = pl.BlockSpec((tm, tk), lambda i, j, kk: (i, kk))
        b_spec = pl.BlockSpec((tk, tn), lambda i, j, kk: (kk, j))
    elif mode == 'nt':
        a_spec = pl.BlockSpec((tm, tk), lambda i, j, kk: (i, kk))
        b_spec = pl.BlockSpec((tn, tk), lambda i, j, kk: (j, kk))
    else:
        a_spec = pl.BlockSpec((tk, tm), lambda i, j, kk: (kk, i))
        b_spec = pl.BlockSpec((tk, tn), lambda i, j, kk: (kk, j))
    any_spec = pl.BlockSpec(memory_space=pl.ANY)
    res = pl.pallas_call(
        body, name=name, grid=(m // tm, n // tn, nk),
        in_specs=[a_spec, b_spec] + [any_spec] * n_x,
        out_specs=[pl.BlockSpec((tm, tn), lambda i, j, kk: (i, j))] + [any_spec] * n_x,
        out_shape=[jax.ShapeDtypeStruct((m, n), out_dtype)] + (_exchange_shapes(a2a, kinds, 'all') if n_x else []),
        scratch_shapes=[pltpu.VMEM((tm, tn), F32)] + (_exchange_scratch(n_x, 'all') if n_x else []),
        compiler_params=_cparams(("arbitrary",) * 3 if n_x else ("parallel", "parallel", "arbitrary")),
    )(a, b, *a2a)
    return res if n_x else res[0]


def diag_outer(a, a_col0, a_cols, bs, parts, name):
    k = a.shape[0]
    ka, kb = a_cols // parts, bs[0].shape[1] // parts
    tk = _tile(k, MM_TILES)
    nk = k // tk
    n_b = len(bs)
    assert a_col0 % ka == 0

    def body(a_ref, *rest):
        b_refs, o_refs, acc_refs = rest[:n_b], rest[n_b:2 * n_b], rest[2 * n_b:]
        kk = pl.program_id(1)
        av = a_ref[...].astype(BF16)
        for b_ref, o_ref, acc_ref in zip(b_refs, o_refs, acc_refs):
            @pl.when(kk == 0)
            def _():
                acc_ref[...] = jnp.zeros_like(acc_ref)

            acc_ref[...] += _dot_tn(av, b_ref[...].astype(BF16))

            @pl.when(kk == nk - 1)
            def _():
                o_ref[...] = acc_ref[...]

    return pl.pallas_call(
        body, name=name, grid=(parts, nk),
        in_specs=[pl.BlockSpec((tk, ka), lambda p, kk: (kk, a_col0 // ka + p))]
        + [pl.BlockSpec((tk, kb), lambda p, kk: (kk, p))] * n_b,
        out_specs=[pl.BlockSpec((None, ka, kb), lambda p, kk: (p, 0, 0))] * n_b,
        out_shape=[jax.ShapeDtypeStruct((parts, ka, kb), F32)] * n_b,
        scratch_shapes=[pltpu.VMEM((ka, kb), F32)] * n_b,
        compiler_params=_cparams(("parallel", "arbitrary")),
    )(a, *bs)


N_CHIPS = 4


def _group(group):
    x, y, c = lax.axis_index("x"), lax.axis_index("y"), lax.axis_index("c")
    flips = {'all': [(r & 4, r & 2, r & 1) for r in range(1, 8)],
             'chips': [(0, 1, 0), (1, 0, 0), (1, 1, 0)], 'core': [(0, 0, 1)]}[group]
    index = {'all': lambda px, py, pc: 4 * px + 2 * py + pc, 'chips': lambda px, py, pc: 2 * px + py,
             'core': lambda px, py, pc: pc}[group]
    peers = []
    for fx, fy, fc in flips:
        p = (1 - x if fx else x, 1 - y if fy else y, 1 - c if fc else c)
        peers.append((p, index(*p)))
    return len(flips) + 1, index(x, y, c), peers


def exchange(arrays, kinds, name, group='all'):
    n_arr = len(arrays)

    def body(*refs):
        start, finish = _exchange_ops(refs[:n_arr], refs[n_arr:2 * n_arr], *refs[2 * n_arr:], kinds, group,
                                      own_slot=False)
        start()
        finish()

    any_spec = pl.BlockSpec(memory_space=pl.ANY)
    outs = pl.pallas_call(
        body, name=name,
        in_specs=[any_spec] * n_arr, out_specs=[any_spec] * n_arr, out_shape=_exchange_shapes(arrays, kinds, group),
        scratch_shapes=_exchange_scratch(n_arr, group),
        compiler_params=pltpu.CompilerParams(has_side_effects=True),
    )(*arrays)
    _, me, _ = _group(group)
    own = {'gather': lambda arr: arr, 'a2a': lambda arr: lax.dynamic_index_in_dim(arr, me, 0, keepdims=False)}
    return [out if kind == 'others' else lax.dynamic_update_index_in_dim(out, own[kind](arr), me, 0)
            for arr, kind, out in zip(arrays, kinds, outs)]


GROUP_SIZE = {'all': N_DEV, 'chips': N_CHIPS, 'core': 2}


def _exchange_shapes(arrays, kinds, group):
    size = GROUP_SIZE[group]
    return [jax.ShapeDtypeStruct({'gather': (size,) + tuple(arr.shape), 'a2a': tuple(arr.shape),
                                  'others': (size - 1,) + tuple(arr.shape[1:])}[kind], arr.dtype)
            for arr, kind in zip(arrays, kinds)]


def _exchange_scratch(n_arr, group):
    size = GROUP_SIZE[group]
    return [pltpu.SemaphoreType.DMA((n_arr * size,)), pltpu.SemaphoreType.DMA((n_arr * size,)),
            pltpu.SemaphoreType.DMA((n_arr,))]


def _exchange_ops(srcs, dsts, send_sems, recv_sems, local_sems, kinds, group, own_slot=True):
    n_arr = len(srcs)
    size, me, peers = _group(group)
    has_local = [own_slot and kind != 'others' for kind in kinds]

    def copy(a, r, peer, peer_idx, receiving):
        src = srcs[a] if kinds[a] == 'gather' else srcs[a].at[peer_idx]
        if kinds[a] == 'others':
            dst = dsts[a].at[r]
        else:
            dst = dsts[a].at[peer_idx if receiving else me]
        return pltpu.make_async_remote_copy(
            src_ref=src, dst_ref=dst, send_sem=send_sems.at[a * size + r], recv_sem=recv_sems.at[a * size + r],
            device_id=peer, device_id_type=pl.DeviceIdType.MESH)

    def local(a):
        mine = srcs[a] if kinds[a] == 'gather' else srcs[a].at[me]
        return pltpu.make_async_copy(mine, dsts[a].at[me], local_sems.at[a])

    def start():
        for a in range(n_arr):
            if has_local[a]:
                local(a).start()
            for r, (peer, peer_idx) in enumerate(peers):
                copy(a, r, peer, peer_idx, False).start()

    def finish():
        for a in range(n_arr):
            for r, (peer, peer_idx) in enumerate(peers):
                cp = copy(a, r, peer, peer_idx, True)
                cp.wait_send()
                cp.wait_recv()
            if has_local[a]:
                local(a).wait()

    return start, finish


GATHER_COPIES = 7


def _two_level_gather(srcs, outs, send_sems, recv_sems, local_sems):
    n_arr = len(srcs)
    x, y, c = lax.axis_index("x"), lax.axis_index("y"), lax.axis_index("c")
    me, sibling = (x, y, c), (x, y, 1 - c)
    chips = [(1 - x, y), (x, 1 - y), (1 - x, 1 - y)]
    slot = lambda p: 4 * p[0] + 2 * p[1] + p[2]

    def copy(a, k, block, to, own=False):
        dst = outs[a].at[slot(block)]
        return pltpu.make_async_remote_copy(
            src_ref=srcs[a] if own else dst, dst_ref=dst,
            send_sem=send_sems.at[a * GATHER_COPIES + k], recv_sem=recv_sems.at[a * GATHER_COPIES + k],
            device_id=to, device_id_type=pl.DeviceIdType.MESH)

    def own_copies(a):
        return [copy(a, 0, me, sibling, own=True)] + [copy(a, 1 + j, me, (*chip, c), own=True)
                                                      for j, chip in enumerate(chips)]

    local = lambda a: pltpu.make_async_copy(srcs[a], outs[a].at[slot(me)], local_sems.at[a])

    def start():
        for a in range(n_arr):
            local(a).start()
            for cp in own_copies(a):
                cp.start()

    def finish():
        passed = []
        for j, chip in enumerate(chips):
            for a in range(n_arr):
                copy(a, 1 + j, (*chip, c), me).wait_recv()
                passed.append(copy(a, 4 + j, (*chip, c), sibling))
                passed[-1].start()
        for a in range(n_arr):
            copy(a, 0, sibling, me).wait_recv()
            for j, chip in enumerate(chips):
                copy(a, 4 + j, (*chip, 1 - c), me).wait_recv()
            for cp in own_copies(a):
                cp.wait_send()
        for cp in passed:
            cp.wait_send()
        for a in range(n_arr):
            local(a).wait()

    return start, finish


def _gather_scratch(n_arr):
    return [pltpu.SemaphoreType.DMA((n_arr * GATHER_COPIES,)), pltpu.SemaphoreType.DMA((n_arr * GATHER_COPIES,)),
            pltpu.SemaphoreType.DMA((n_arr,))]


def _pack(parts, row_mult):
    rows, spans, r = [], [], 0
    for p in parts:
        flat = p.reshape(-1)
        nr = -(-flat.shape[0] // (PACK_W * row_mult)) * row_mult
        flat = jnp.pad(flat, (0, nr * PACK_W - flat.shape[0]))
        rows.append(flat.reshape(nr, PACK_W))
        spans.append((r, nr))
        r += nr
    return jnp.concatenate(rows, axis=0), spans


def _unpack(buf, spans, shapes):
    out = []
    for (r, nr), shp in zip(spans, shapes):
        size = math.prod(shp)
        out.append(buf[..., r:r + nr, :].reshape(buf.shape[:-2] + (nr * PACK_W,))[..., :size]
                   .reshape(buf.shape[:-2] + tuple(shp)))
    return out


def _to_shards(full, axis):
    r, c = full.shape
    if axis == 0:
        return full.reshape(N_DEV, r // N_DEV, c)
    return full.reshape(r, N_DEV, c // N_DEV).transpose(1, 0, 2)


def _from_shards(sh, axis):
    _, r, c = sh.shape
    if axis == 0:
        return sh.reshape(N_DEV * r, c)
    return sh.transpose(1, 0, 2).reshape(r, N_DEV * c)


def _rms(x, g, n):
    ms = jnp.sum(x * x, axis=-1, keepdims=True) * (1.0 / n)
    return x * lax.rsqrt(ms + EPS) * g


def _norm_mod_fn(x, g, sc, sh):
    return _rms(x, g, x.shape[-1]) * (1.0 + sc) + sh


@jax.custom_vjp
def _rope(t, ct, s1, s2):
    return t * ct + pltpu.roll(t, 16, 1) * s1 + pltpu.roll(t, HEAD_PAD - 16, 1) * s2


def _rope_fwd(t, ct, s1, s2):
    return _rope(t, ct, s1, s2), (ct, s1, s2)


def _rope_bwd(res, d):
    ct, s1, s2 = res
    dt = d * ct + pltpu.roll(d * s1, HEAD_PAD - 16, 1) + pltpu.roll(d * s2, 16, 1)
    return dt, jnp.zeros_like(ct), jnp.zeros_like(s1), jnp.zeros_like(s2)


_rope.defvjp(_rope_fwd, _rope_bwd)


def _qkv_fn(cq, ckv, krsec, qag, kvag, qng, kng, wuq, wk, wv, ct, s1, s2):
    q_raw = bdot(_rms(cq, qag, cq.shape[-1]), wuq)
    ckvn = _rms(ckv, kvag, ckv.shape[-1])
    k_raw = bdot(ckvn, wk)
    v = bdot(ckvn, wv)
    qs, ks = [], []
    for h in range(N_HEADS):
        sl = slice(h * HEAD_PAD, (h + 1) * HEAD_PAD)
        qs.append(_rope(_rms(q_raw[:, sl], qng, QK_DIM), ct, s1, s2))
        ks.append(_rope(_rms(k_raw[:, sl] + krsec, kng, QK_DIM), ct, s1, s2))
    return jnp.concatenate(qs, axis=1), jnp.concatenate(ks, axis=1), v


def _glu_fn(u, yf, yb, dskip, wglu):
    y = u * dskip + yf + yb
    vg = bdot(jax.nn.gelu(y), wglu)
    d = vg.shape[-1] // 2
    return vg[:, :d] * jax.nn.sigmoid(vg[:, d:])


def _merge_fn(o, s_l, gl, x, g1, wo, wout):
    d = x.shape[-1]
    a = bdot(o, wo)
    mix = jax.nn.sigmoid(gl[:, :d]) * a + jax.nn.sigmoid(gl[:, d:]) * s_l
    return x + g1 * bdot(mix, wout)


def _mod_fn(cmat, w):
    return bdot(jax.nn.silu(cmat), w)


def _ssm_prep_fn(lam_re, lam_im, logdt, bre, bim):
    dt = jnp.exp(logdt)
    ar, ai = lam_re * dt, lam_im * dt
    e = jnp.exp(ar)
    lbr, lbi = e * jnp.cos(ai), e * jnp.sin(ai)
    nr, ni = lbr - 1.0, lbi
    den = lam_re * lam_re + lam_im * lam_im
    qr = (nr * lam_re + ni * lam_im) / den
    qi = (ni * lam_re - nr * lam_im) / den
    return lbr, lbi, qr * bre - qi * bim, qr * bim + qi * bre


def _rows(tm, w, col=0):
    return pl.BlockSpec((tm, w), lambda i: (i, col))


def _full(shape):
    nd = len(shape)
    return pl.BlockSpec(tuple(shape), lambda i: (0,) * nd)


def _acc_add(i, ref, val):
    @pl.when(i == 0)
    def _():
        ref[...] = jnp.zeros_like(ref)
    ref[...] += val


def _pad_rows(v, rows=SUBLANES):
    sel = lax.broadcasted_iota(jnp.int32, (rows, v.shape[-1]), 0) == 0
    return jnp.where(sel, jnp.broadcast_to(v, (rows, v.shape[-1])), 0.0)


def norm_mod_fwd(xa, g, mods, n_lat, tm, name, gather=()):
    r, d = xa.shape
    lat_tiles = n_lat // tm
    n_tiles = r // tm
    n_g = len(gather)

    def body(x_ref, g_ref, m_ref, *rest):
        o_ref = rest[n_g]
        i = pl.program_id(0)
        if n_g:
            start, finish = _two_level_gather(rest[:n_g], rest[n_g + 1:2 * n_g + 1], *rest[2 * n_g + 1:])
            pl.when(i == 0)(start)
        lat = i < lat_tiles
        sc = jnp.where(lat, m_ref[0:1, :], m_ref[2:3, :])
        sh = jnp.where(lat, m_ref[1:2, :], m_ref[3:4, :])
        o_ref[...] = _norm_mod_fn(x_ref[...], g_ref[...], sc, sh).astype(o_ref.dtype)
        if n_g:
            pl.when(i == n_tiles - 1)(finish)

    any_spec = pl.BlockSpec(memory_space=pl.ANY)
    res = pl.pallas_call(
        body, name=name, grid=(n_tiles,),
        in_specs=[_rows(tm, d), _full(g.shape), _full(mods.shape)] + [any_spec] * n_g,
        out_specs=[_rows(tm, d)] + [any_spec] * n_g,
        out_shape=[jax.ShapeDtypeStruct((r, d), BF16)]
        + [jax.ShapeDtypeStruct((N_DEV,) + tuple(a.shape), a.dtype) for a in gather],
        scratch_shapes=_gather_scratch(n_g) if n_g else [],
        compiler_params=_cparams(("arbitrary",) if n_g else ("parallel",)),
    )(xa, g, mods, *gather)
    return res if n_g else res[0]


def norm_mod_bwd(xa, dh, dres, g, mods, n_lat, tm, name):
    r, d = xa.shape
    lat_tiles = n_lat // tm

    def body(x_ref, dh_ref, dres_ref, g_ref, m_ref, dx_ref, dg_ref, dm_ref):
        i = pl.program_id(0)
        lat = i < lat_tiles
        sc = jnp.where(lat, m_ref[0:1, :], m_ref[2:3, :])
        sh = jnp.where(lat, m_ref[1:2, :], m_ref[3:4, :])
        _, vjp = jax.vjp(_norm_mod_fn, x_ref[...], g_ref[...], sc, sh)
        dx, dg, dsc, dsh = vjp(dh_ref[...])
        dx_ref[...] = dx + jnp.where(lat, dres_ref[...], 0.0)
        _acc_add(i, dg_ref, _pad_rows(dg))
        row = lax.broadcasted_iota(jnp.int32, (SUBLANES, d), 0)
        base = jnp.where(lat, 0, 2)
        upd = jnp.where(row == base, jnp.broadcast_to(dsc, (SUBLANES, d)), 0.0)
        upd = upd + jnp.where(row == base + 1, jnp.broadcast_to(dsh, (SUBLANES, d)), 0.0)
        _acc_add(i, dm_ref, upd)

    return pl.pallas_call(
        body, name=name, grid=(r // tm,),
        in_specs=[_rows(tm, d), _rows(tm, d),
                  pl.BlockSpec((tm, d), lambda i: (jnp.minimum(i, lat_tiles - 1), 0)),
                  _full(g.shape), _full(mods.shape)],
        out_specs=[_rows(tm, d), _full((SUBLANES, d)), _full((SUBLANES, d))],
        out_shape=[jax.ShapeDtypeStruct((r, d), F32), jax.ShapeDtypeStruct((SUBLANES, d), F32),
                   jax.ShapeDtypeStruct((SUBLANES, d), F32)],
        compiler_params=_cparams(("arbitrary",)),
    )(xa, dh, dres, g, mods)


def qkv_fwd(proj, lay, gains, wuq, wk, wv, tabs, tm, name):
    r = proj.shape[0]
    q_w, kv_w = lay['q'], lay['kv']

    def body(cq_ref, ckv_ref, kr_ref, qag, kvag, qng, kng, wuq_ref, wk_ref, wv_ref, ct, s1, s2, q_ref, k_ref,
             v_ref):
        q, k, v = _qkv_fn(cq_ref[...], ckv_ref[...], kr_ref[...], qag[...], kvag[...], qng[...], kng[...],
                          wuq_ref[...], wk_ref[...], wv_ref[...], ct[...], s1[...], s2[...])
        q_ref[...] = q.astype(BF16)
        k_ref[...] = k.astype(BF16)
        v_ref[...] = v.astype(BF16)

    out = jax.ShapeDtypeStruct((r, HEADS_W), BF16)
    return pl.pallas_call(
        body, name=name, grid=(r // tm,),
        in_specs=[_rows(tm, q_w, lay['o_cq'] // q_w), _rows(tm, kv_w, lay['o_ckv'] // kv_w),
                  _rows(tm, LANES, lay['o_kr'] // LANES)]
        + [_full(a.shape) for a in gains] + [_full(wuq.shape), _full(wk.shape), _full(wv.shape)]
        + [_rows(tm, HEAD_PAD)] * 3,
        out_specs=[_rows(tm, HEADS_W)] * 3, out_shape=[out, out, out],
        compiler_params=_cparams(("parallel",)),
    )(proj, proj, proj, *gains, wuq, wk, wv, *tabs)


def qkv_bwd(proj, lay, gains, wuq, wk, wv, tabs, dq, dk, dv, dgl, du_direct, du_f, du_b, n, tm, name):
    r = proj.shape[0]
    q_w, kv_w, sw, d = lay['q'], lay['kv'], lay['sw'], lay['d']
    lat_tiles = n // tm

    def body(cq_ref, ckv_ref, kr_ref, qag, kvag, qng, kng, wuq_ref, wk_ref, wv_ref, ct, s1, s2, dq_ref, dk_ref,
             dv_ref, dgl_ref, dud_ref, duf_ref, dub_ref, dp_ref, dqag, dkvag, dqng, dkng, dwuq, dwk, dwv):
        i = pl.program_id(0)
        lat = i < lat_tiles
        tables = (ct[...], s1[...], s2[...])
        fn = lambda *a: _qkv_fn(*a, *tables)
        _, vjp = jax.vjp(fn, cq_ref[...], ckv_ref[...], kr_ref[...], qag[...], kvag[...], qng[...], kng[...],
                         wuq_ref[...].astype(F32), wk_ref[...].astype(F32), wv_ref[...].astype(F32))
        g = vjp((jnp.where(lat, dq_ref[...], 0.0), dk_ref[...], dv_ref[...]))
        dp_ref[:, 0:2 * d] = jnp.where(lat, dgl_ref[...], 0.0).astype(BF16)
        dp_ref[:, lay['o_u']:lay['o_u'] + sw] = (duf_ref[...] + dub_ref[...]
                                                 + jnp.where(lat, dud_ref[...], 0.0)).astype(BF16)
        dp_ref[:, lay['o_ckv']:lay['o_ckv'] + kv_w] = g[1].astype(BF16)
        dp_ref[:, lay['o_kr']:lay['o_kr'] + LANES] = g[2].astype(BF16)
        hole0 = lay['o_kr'] + LANES
        if lay['o_cq'] > hole0:
            dp_ref[:, hole0:lay['o_cq']] = jnp.zeros((tm, lay['o_cq'] - hole0), BF16)
        dp_ref[:, lay['o_cq']:lay['o_cq'] + q_w] = g[0].astype(BF16)
        for ref, val in zip((dqag, dkvag, dqng, dkng), g[3:7]):
            _acc_add(i, ref, _pad_rows(val))
        for ref, val in zip((dwuq, dwk, dwv), g[7:10]):
            _acc_add(i, ref, val)

    def lat_rows(w):
        return pl.BlockSpec((tm, w), lambda i: (jnp.minimum(i, lat_tiles - 1), 0))

    acc_shapes = [(SUBLANES, a.shape[1]) for a in gains] + [wuq.shape, wk.shape, wv.shape]
    return pl.pallas_call(
        body, name=name, grid=(r // tm,),
        in_specs=[_rows(tm, q_w, lay['o_cq'] // q_w), _rows(tm, kv_w, lay['o_ckv'] // kv_w),
                  _rows(tm, LANES, lay['o_kr'] // LANES)]
        + [_full(a.shape) for a in gains] + [_full(wuq.shape), _full(wk.shape), _full(wv.shape)]
        + [_rows(tm, HEAD_PAD)] * 3 + [lat_rows(HEADS_W), _rows(tm, HEADS_W), _rows(tm, HEADS_W)]
        + [lat_rows(2 * d), lat_rows(sw), _rows(tm, sw), _rows(tm, sw)],
        out_specs=[_rows(tm, lay['width'])] + [_full(s) for s in acc_shapes],
        out_shape=[jax.ShapeDtypeStruct((r, lay['width']), BF16)] + [jax.ShapeDtypeStruct(s, F32) for s in acc_shapes],
        compiler_params=_cparams(("arbitrary",)),
    )(proj, proj, proj, *gains, wuq, wk, wv, *tabs, dq, dk, dv, dgl, du_direct, du_f, du_b)


def glu_fwd(proj, lay, yf, yb, dskip, wglu, n, tm, name):
    sw, d = wglu.shape[0], wglu.shape[1] // 2

    def body(u_ref, yf_ref, yb_ref, ds_ref, w_ref, o_ref):
        o_ref[...] = _glu_fn(u_ref[...], yf_ref[...], yb_ref[...], ds_ref[...], w_ref[...])

    return pl.pallas_call(
        body, name=name, grid=(n // tm,),
        in_specs=[_rows(tm, sw, lay['o_u'] // sw), _rows(tm, sw), _rows(tm, sw), _full(dskip.shape),
                  _full(wglu.shape)],
        out_specs=_rows(tm, d), out_shape=jax.ShapeDtypeStruct((n, d), F32),
        compiler_params=_cparams(("parallel",)),
    )(proj, yf, yb, dskip, wglu)


def glu_bwd(proj, lay, yf, yb, dskip, wglu, ds_l, n, tm, name):
    sw, d = wglu.shape[0], wglu.shape[1] // 2

    def body(u_ref, yf_ref, yb_ref, ds_ref, w_ref, g_ref, du_ref, dy_ref, dds_ref, dw_ref):
        i = pl.program_id(0)
        _, vjp = jax.vjp(_glu_fn, u_ref[...], yf_ref[...], yb_ref[...], ds_ref[...], w_ref[...].astype(F32))
        du, dyf, _, dds, dw = vjp(g_ref[...])
        du_ref[...] = du
        dy_ref[...] = dyf
        _acc_add(i, dds_ref, _pad_rows(dds))
        _acc_add(i, dw_ref, dw)

    return pl.pallas_call(
        body, name=name, grid=(n // tm,),
        in_specs=[_rows(tm, sw, lay['o_u'] // sw), _rows(tm, sw), _rows(tm, sw), _full(dskip.shape),
                  _full(wglu.shape), _rows(tm, d)],
        out_specs=[_rows(tm, sw), _rows(tm, sw), _full((SUBLANES, sw)), _full(wglu.shape)],
        out_shape=[jax.ShapeDtypeStruct((n, sw), F32), jax.ShapeDtypeStruct((n, sw), F32),
                   jax.ShapeDtypeStruct((SUBLANES, sw), F32), jax.ShapeDtypeStruct(wglu.shape, F32)],
        compiler_params=_cparams(("arbitrary",)),
    )(proj, yf, yb, dskip, wglu, ds_l)


def merge_fwd(o, s_l, proj, xa, g1, wo, wout, n, tm, name):
    d = xa.shape[1]

    def body(o_ref, s_ref, gl_ref, x_ref, g1_ref, wo_ref, wout_ref, x1_ref):
        x1_ref[...] = _merge_fn(o_ref[...], s_ref[...], gl_ref[...], x_ref[...], g1_ref[...], wo_ref[...],
                                wout_ref[...])

    return pl.pallas_call(
        body, name=name, grid=(n // tm,),
        in_specs=[_rows(tm, HEADS_W), _rows(tm, d), _rows(tm, 2 * d), _rows(tm, d), _full(g1.shape),
                  _full(wo.shape), _full(wout.shape)],
        out_specs=_rows(tm, d), out_shape=jax.ShapeDtypeStruct((n, d), F32),
        compiler_params=_cparams(("parallel",)),
    )(o, s_l, proj, xa, g1, wo, wout)


def merge_bwd(o, s_l, proj, xa, g1, wo, wout, dx1, n, tm, name):
    d = xa.shape[1]

    def body(o_ref, s_ref, gl_ref, x_ref, g1_ref, wo_ref, wout_ref, dx1_ref, do_ref, ds_ref, dgl_ref, dg1_ref,
             dwo_ref, dwout_ref):
        i = pl.program_id(0)
        _, vjp = jax.vjp(_merge_fn, o_ref[...], s_ref[...], gl_ref[...], x_ref[...], g1_ref[...],
                         wo_ref[...].astype(F32), wout_ref[...].astype(F32))
        do, ds, dgl, _, dg1, dwo, dwout = vjp(dx1_ref[...])
        do_ref[...] = do
        ds_ref[...] = ds
        dgl_ref[...] = dgl
        _acc_add(i, dg1_ref, _pad_rows(dg1))
        _acc_add(i, dwo_ref, dwo)
        _acc_add(i, dwout_ref, dwout)

    return pl.pallas_call(
        body, name=name, grid=(n // tm,),
        in_specs=[_rows(tm, HEADS_W), _rows(tm, d), _rows(tm, 2 * d), _rows(tm, d), _full(g1.shape),
                  _full(wo.shape), _full(wout.shape), _rows(tm, d)],
        out_specs=[_rows(tm, HEADS_W), _rows(tm, d), _rows(tm, 2 * d), _full((SUBLANES, d)), _full(wo.shape),
                   _full(wout.shape)],
        out_shape=[jax.ShapeDtypeStruct((n, HEADS_W), BF16), jax.ShapeDtypeStruct((n, d), F32),
                   jax.ShapeDtypeStruct((n, 2 * d), F32), jax.ShapeDtypeStruct((SUBLANES, d), F32),
                   jax.ShapeDtypeStruct(wo.shape, F32), jax.ShapeDtypeStruct(wout.shape, F32)],
        compiler_params=_cparams(("arbitrary",)),
    )(o, s_l, proj, xa, g1, wo, wout, dx1)


def _halo_specs(tm, w, n):
    nb = n // SUBLANES
    per = tm // SUBLANES
    prev = pl.BlockSpec((SUBLANES, w), lambda i: (jnp.maximum(i * per - 1, 0), 0))
    nxt = pl.BlockSpec((SUBLANES, w), lambda i: (jnp.minimum((i + 1) * per, nb - 1), 0))
    return prev, nxt


def _shifted(t, prev_blk, next_blk, i, n_tiles):
    tm = t.shape[0]
    row = lax.broadcasted_iota(jnp.int32, t.shape, 0)
    prev_row = jnp.where(i > 0, prev_blk[SUBLANES - 1:SUBLANES, :], 0.0)
    next_row = jnp.where(i < n_tiles - 1, next_blk[0:1, :], 0.0)
    before = jnp.where(row == 0, prev_row, pltpu.roll(t, 1, 0))
    after = jnp.where(row == tm - 1, next_row, pltpu.roll(t, tm - 1, 0))
    return before, after


def _conv_u2(up, before, after, cw, cb):
    return before * cw[0:1, :] + up * cw[1:2, :] + after * cw[2:3, :] + cb


def conv_act_fwd(up, cw, cb, tm, name):
    n, w2 = up.shape
    f = w2 // 2
    n_tiles = n // tm
    prev, nxt = _halo_specs(tm, w2, n)

    def body(up_ref, prev_ref, next_ref, cw_ref, cb_ref, act_ref):
        i = pl.program_id(0)
        t = up_ref[...]
        before, after = _shifted(t, prev_ref[...], next_ref[...], i, n_tiles)
        u2 = _conv_u2(t, before, after, cw_ref[...], cb_ref[...])
        act_ref[...] = (jax.nn.silu(u2[:, f:]) * u2[:, :f]).astype(BF16)

    return pl.pallas_call(
        body, name=name, grid=(n_tiles,),
        in_specs=[_rows(tm, w2), prev, nxt, _full(cw.shape), _full(cb.shape)],
        out_specs=_rows(tm, f), out_shape=jax.ShapeDtypeStruct((n, f), BF16),
        compiler_params=_cparams(("parallel",)),
    )(up, up, up, cw, cb)


def down_loss(act, wdown, x1, g2, target, tm, name):
    n, d = x1.shape
    f = act.shape[1]

    def body(act_ref, w_ref, x1_ref, g2_ref, t_ref, dy_ref, ffn_ref, loss_ref):
        ffn = jnp.dot(act_ref[...], w_ref[...], preferred_element_type=F32)
        err = x1_ref[...] + g2_ref[...] * ffn - t_ref[...]
        ffn_ref[...] = ffn
        dy_ref[...] = err * (1.0 / d)
        part = 0.5 * jnp.sum(jnp.sum(err * err, axis=-1, keepdims=True) * (1.0 / d), axis=0, keepdims=True)
        loss_ref[0] = jnp.broadcast_to(part, (SUBLANES, LANES))

    return pl.pallas_call(
        body, name=name, grid=(n // tm,),
        in_specs=[_rows(tm, f), _full(wdown.shape), _rows(tm, d), _full(g2.shape), _rows(tm, d)],
        out_specs=[_rows(tm, d), _rows(tm, d), pl.BlockSpec((1, SUBLANES, LANES), lambda i: (i, 0, 0))],
        out_shape=[jax.ShapeDtypeStruct((n, d), F32), jax.ShapeDtypeStruct((n, d), F32),
                   jax.ShapeDtypeStruct((n // tm, SUBLANES, LANES), F32)],
        compiler_params=_cparams(("parallel",)),
    )(act, wdown, x1, g2, target)


FFN_BWD_PARTS = 2


def ffn_bwd(dy, ffn, up, cw, cb, wdown, g2, tm, name):
    n, d = dy.shape
    w2 = up.shape[1]
    f = w2 // 2
    fc = f // FFN_BWD_PARTS
    assert fc % LANES == 0
    n_tiles = n // tm
    ext = tm + 2 * SUBLANES
    inner = slice(SUBLANES, SUBLANES + tm)
    prev_w, next_w = _halo_specs(tm, w2, n)
    prev_d, next_d = _halo_specs(tm, d, n)

    def body(dy_ref, dyp_ref, dyn_ref, ffn_ref, up_ref, upp_ref, upn_ref, cw_ref, cb_ref, w_ref, g2_ref,
             dup_ref, dffn_ref, dg2_ref, dcw_ref):
        i = pl.program_id(0)
        has_prev, has_next = i > 0, i < n_tiles - 1

        def extended(prev, tile, nxt):
            return jnp.concatenate([jnp.where(has_prev, prev, 0.0), tile, jnp.where(has_next, nxt, 0.0)], axis=0)

        dyv = dy_ref[...]
        dffn = extended(dyp_ref[...], dyv, dyn_ref[...]) * g2_ref[...]
        dffn_ref[...] = dffn[inner].astype(BF16)
        dffn = dffn.astype(BF16)
        _acc_add(i, dg2_ref, _pad_rows(jnp.sum(dyv * ffn_ref[...], axis=0, keepdims=True)))
        row = lax.broadcasted_iota(jnp.int32, (SUBLANES, fc), 0)
        shift = lambda t: (pltpu.roll(t, 1, 0), pltpu.roll(t, ext - 1, 0))
        for part in range(FFN_BWD_PARTS):
            halves = []
            for col0 in (part * fc, f + part * fc):
                cols = slice(col0, col0 + fc)
                t = extended(upp_ref[:, cols], up_ref[:, cols], upn_ref[:, cols])
                before, after = shift(t)
                halves.append((cols, t, _conv_u2(t, before, after, cw_ref[:, cols], cb_ref[:, cols])))
            (_, _, val), (_, _, gate) = halves
            dact = _dot_nt(dffn, w_ref[part * fc:(part + 1) * fc, :])
            sg = jax.nn.sigmoid(gate)
            for (cols, t, _), du2 in zip(halves, (dact * (gate * sg),
                                                   dact * val * (sg * (1.0 + gate * (1.0 - sg))))):
                before, after = shift(du2)
                cwv = cw_ref[:, cols]
                dup_ref[:, cols] = (after * cwv[0:1, :] + du2 * cwv[1:2, :] + before * cwv[2:3, :])[inner].astype(BF16)
                upd = jnp.zeros((SUBLANES, fc), F32)
                for k, term in enumerate((after * t, du2 * t, before * t, du2)):
                    upd = upd + jnp.where(row == k, jnp.broadcast_to(
                        jnp.sum(term[inner], axis=0, keepdims=True), (SUBLANES, fc)), 0.0)

                @pl.when(i == 0)
                def _():
                    dcw_ref[:, cols] = jnp.zeros((SUBLANES, fc), F32)

                dcw_ref[:, cols] += upd

    return pl.pallas_call(
        body, name=name, grid=(n_tiles,),
        in_specs=[_rows(tm, d), prev_d, next_d, _rows(tm, d), _rows(tm, w2), prev_w, next_w, _full(cw.shape),
                  _full(cb.shape), _full(wdown.shape), _full(g2.shape)],
        out_specs=[_rows(tm, w2), _rows(tm, d), _full((SUBLANES, d)), _full((SUBLANES, w2))],
        out_shape=[jax.ShapeDtypeStruct((n, w2), BF16), jax.ShapeDtypeStruct((n, d), BF16),
                   jax.ShapeDtypeStruct((SUBLANES, d), F32), jax.ShapeDtypeStruct((SUBLANES, w2), F32)],
        compiler_params=_cparams(("arbitrary",)),
    )(dy, dy, dy, ffn, up, up, up, cw, cb, wdown, g2)


def attn_fwd(q, k, v, n, name, gather=()):
    nk = k.shape[0]
    tq = _tile(n, (512, 256, 128))
    tk = _tile(nk, (768, 384, 256, 128))
    n_kv = nk // tk
    n_q = n // tq
    n_g = len(gather)
    scale = QK_DIM ** -0.5
    c2 = scale * math.log2(math.e)

    def body(q_ref, k_ref, v_ref, *rest):
        o_ref, lse_ref = rest[n_g:n_g + 2]
        if n_g:
            start, finish = _two_level_gather(rest[:n_g], rest[n_g + 2:2 * n_g + 2], *rest[2 * n_g + 2:])
            step = pl.program_id(0) * n_q + pl.program_id(1)
            pl.when(step == 0)(start)
        qv = q_ref[...]
        ones_col = (lax.broadcasted_iota(jnp.int32, (tk, HEAD_PAD), 1) == V_DIM).astype(BF16)

        def chunk(j, carry):
            m, acc = carry
            rows = pl.ds(pl.multiple_of(j * tk, tk), tk)
            s = _dot_nt(qv, k_ref[rows, :])
            m_new = jnp.maximum(m, jnp.max(s, axis=-1, keepdims=True))
            p = jnp.exp2(s * c2 - m_new * c2)
            alpha = jnp.exp2((m - m_new) * c2)
            pv = jnp.dot(p.astype(BF16), v_ref[rows, :] + ones_col, preferred_element_type=F32)
            return m_new, alpha * acc + pv

        m, acc = lax.fori_loop(0, n_kv, chunk, (jnp.full((tq, 1), -jnp.inf, F32),
                                                jnp.zeros((tq, HEAD_PAD), F32)), unroll=True)
        l = acc[:, V_DIM:V_DIM + 1]
        lane = lax.broadcasted_iota(jnp.int32, (tq, HEAD_PAD), 1)
        o_ref[...] = jnp.where(lane < V_DIM, acc / l, 0.0).astype(BF16)
        lse_ref[...] = jnp.broadcast_to(m * scale + jnp.log(l), (tq, HEAD_PAD))
        if n_g:
            pl.when(step == N_HEADS * n_q - 1)(finish)

    qspec = pl.BlockSpec((tq, HEAD_PAD), lambda h, i: (i, h))
    kspec = pl.BlockSpec((nk, HEAD_PAD), lambda h, i: (0, h))
    any_spec = pl.BlockSpec(memory_space=pl.ANY)
    return pl.pallas_call(
        body, name=name, grid=(N_HEADS, n_q),
        in_specs=[qspec, kspec, kspec] + [any_spec] * n_g, out_specs=[qspec, qspec] + [any_spec] * n_g,
        out_shape=[jax.ShapeDtypeStruct((n, HEADS_W), BF16), jax.ShapeDtypeStruct((n, HEADS_W), F32)]
        + [jax.ShapeDtypeStruct((N_DEV,) + tuple(a.shape), a.dtype) for a in gather],
        scratch_shapes=_gather_scratch(n_g) if n_g else [],
        compiler_params=_cparams(("arbitrary", "arbitrary") if n_g else ("parallel", "parallel")),
    )(q, k, v, *gather)


def attn_bwd(q, k, v, o, do, lse, n, name, a2a=()):
    nk = k.shape[0]
    tq = _tile(n, (512, 256, 128))
    tk = _tile(nk, (768, 384, 256, 128))
    n_kv = nk // tk
    n_x = len(a2a)
    kinds = ['a2a'] * n_x
    scale = QK_DIM ** -0.5
    log2e = math.log2(math.e)
    c2 = scale * log2e

    def body(q_ref, k_ref, v_ref, o_ref, do_ref, lse_ref, *rest):
        dq_ref, dk_ref, dv_ref = rest[n_x:n_x + 3]
        if n_x:
            start, finish = _exchange_ops(rest[:n_x], rest[n_x + 3:2 * n_x + 3], *rest[2 * n_x + 3:], kinds, 'all')
            step = pl.program_id(0) * n_kv + pl.program_id(1)
            pl.when(step == 0)(start)

        @pl.when(pl.program_id(1) == 0)
        def _():
            dq_ref[...] = jnp.zeros_like(dq_ref)

        kv, vv = k_ref[...], v_ref[...]

        def q_tile(i, carry):
            dk, dv = carry
            rows = pl.ds(pl.multiple_of(i * tq, tq), tq)
            qv, dov = q_ref[rows, :], do_ref[rows, :]
            p = jnp.exp2(_dot_nt(qv, kv) * c2 - lse_ref[rows, 0:1] * log2e)
            dv = dv + _dot_tn(p.astype(BF16), dov)
            dp = _dot_nt(dov, vv)
            delta = jnp.sum(dov.astype(F32) * o_ref[rows, :].astype(F32), axis=-1, keepdims=True)
            ds = (p * (dp - delta) * scale).astype(BF16)
            dk = dk + _dot_tn(ds, qv)
            dq_ref[rows, :] += jnp.dot(ds, kv, preferred_element_type=F32)
            return dk, dv

        zero = jnp.zeros((tk, HEAD_PAD), F32)
        dk, dv = lax.fori_loop(0, n // tq, q_tile, (zero, zero), unroll=2)
        dk_ref[...] = dk
        dv_ref[...] = dv
        if n_x:
            pl.when(step == N_HEADS * n_kv - 1)(finish)

    qspec = pl.BlockSpec((n, HEAD_PAD), lambda h, j: (0, h))
    kspec = pl.BlockSpec((tk, HEAD_PAD), lambda h, j: (j, h))
    any_spec = pl.BlockSpec(memory_space=pl.ANY)
    return pl.pallas_call(
        body, name=name, grid=(N_HEADS, n_kv),
        in_specs=[qspec, kspec, kspec, qspec, qspec, qspec] + [any_spec] * n_x,
        out_specs=[qspec, kspec, kspec] + [any_spec] * n_x,
        out_shape=[jax.ShapeDtypeStruct((n, HEADS_W), F32), jax.ShapeDtypeStruct((nk, HEADS_W), F32),
                   jax.ShapeDtypeStruct((nk, HEADS_W), F32)] + _exchange_shapes(a2a, kinds, 'all'),
        scratch_shapes=_exchange_scratch(n_x, 'all') if n_x else [],
        compiler_params=_cparams(("arbitrary", "arbitrary") if n_x else ("parallel", "arbitrary")),
    )(q, k, v, o, do, lse, *a2a)


SCAN_LEVELS = (1, 2, 4)
SCAN_LANES = 512


def _scan_chunk(xr, xi, car, m_ref, p_ref, reverse):
    t_len, gn = xr.shape
    n_slab = t_len // SUBLANES
    for lb in range(gn // SCAN_LANES):
        ls = pl.ds(lb * SCAN_LANES, SCAN_LANES)

        def step(s, carry, ls=ls):
            cr, ci = carry
            slab = (n_slab - 1 - s) if reverse else s
            rows = pl.ds(pl.multiple_of(slab * SUBLANES, SUBLANES), SUBLANES)
            br, bi = xr[rows, ls], xi[rows, ls]
            for lvl, d in enumerate(SCAN_LEVELS):
                shift = SUBLANES - d if reverse else d
                sr, si = pltpu.roll(br, shift, 0), pltpu.roll(bi, shift, 0)
                mr, mi = m_ref[lvl, 0, :, ls], m_ref[lvl, 1, :, ls]
                br, bi = br + mr * sr - mi * si, bi + mr * si + mi * sr
            pr, pi = p_ref[0, :, ls], p_ref[1, :, ls]
            br, bi = br + pr * cr - pi * ci, bi + pr * ci + pi * cr
            xr[rows, ls] = br
            xi[rows, ls] = bi
            last = 0 if reverse else SUBLANES - 1
            return br[last:last + 1, :], bi[last:last + 1, :]

        cr, ci = lax.fori_loop(0, n_slab, step, (car[0:1, ls], car[1:2, ls]))
        car[0:1, ls] = cr
        car[1:2, ls] = ci


SSM_SPLIT = 2


def _bd_dot(a, w_ref):
    k, n = w_ref.shape[0] // SSM_SPLIT, w_ref.shape[1] // SSM_SPLIT
    return jnp.concatenate([jnp.dot(a[:, p * k:(p + 1) * k], w_ref[p * k:(p + 1) * k, p * n:(p + 1) * n],
                                    preferred_element_type=F32) for p in range(SSM_SPLIT)], axis=1)


def _bd_dot_nt(a, w_ref):
    k, n = w_ref.shape[0] // SSM_SPLIT, w_ref.shape[1] // SSM_SPLIT
    return jnp.concatenate([_dot_nt(a[:, p * n:(p + 1) * n], w_ref[p * k:(p + 1) * k, p * n:(p + 1) * n])
                            for p in range(SSM_SPLIT)], axis=1)


def _seq_block(step, n_chunk, lat_chunks, reverse):
    if reverse:
        return n_chunk - 1 - step
    return (step + lat_chunks) % n_chunk


def ssm_fwd(proj, lay, bre, bim, cre, ncim, tabs, n, t_len, reverse, name):
    l, sw = proj.shape[0], lay['sw']
    gn = bre.shape[-1]
    n_chunk, lat_chunks = l // t_len, n // t_len
    blk = lambda k: _seq_block(k, n_chunk, lat_chunks, reverse)
    mtab, ptab = tabs

    def body(u_ref, bre_ref, bim_ref, cre_ref, ncim_ref, m_ref, p_ref, y_ref, xb_ref, xr, xi, car):
        @pl.when(pl.program_id(0) == 0)
        def _():
            car[...] = jnp.zeros_like(car)

        xb_ref[...] = car[...]
        u = u_ref[...].astype(BF16)
        xr[...] = _bd_dot(u, bre_ref)
        xi[...] = _bd_dot(u, bim_ref)
        _scan_chunk(xr, xi, car, m_ref, p_ref, reverse)
        y_ref[...] = _bd_dot(xr[...].astype(BF16), cre_ref) + _bd_dot(xi[...].astype(BF16), ncim_ref)

    return pl.pallas_call(
        body, name=name, grid=(n_chunk,),
        in_specs=[pl.BlockSpec((t_len, sw), lambda k: (blk(k), lay['o_u'] // sw)), _full(bre.shape),
                  _full(bim.shape), _full(cre.shape), _full(ncim.shape), _full(mtab.shape), _full(ptab.shape)],
        out_specs=[pl.BlockSpec((t_len, sw), lambda k: (blk(k), 0)),
                   pl.BlockSpec((None, 2, gn), lambda k: (k, 0, 0))],
        out_shape=[jax.ShapeDtypeStruct((l, sw), F32), jax.ShapeDtypeStruct((n_chunk, 2, gn), F32)],
        scratch_shapes=[pltpu.VMEM((t_len, gn), F32), pltpu.VMEM((t_len, gn), F32), pltpu.VMEM((2, gn), F32)],
        compiler_params=_cparams(("arbitrary",)),
    )(proj, bre, bim, cre, ncim, mtab, ptab)


def ssm_bwd(proj, lay, dyr, xb, bre, bim, cre, ncim, tabs, adj_tabs, n, t_len, reverse, name):
    l, sw = proj.shape[0], lay['sw']
    gn = bre.shape[-1]
    n_chunk, lat_chunks = l // t_len, n // t_len
    fwd_step = lambda k: n_chunk - 1 - k
    blk = lambda k: _seq_block(fwd_step(k), n_chunk, lat_chunks, reverse)
    (mtab, ptab), (mtab_r, ptab_r) = tabs, adj_tabs

    def body(u_ref, dy_ref, xb_ref, bre_ref, bim_ref, cre_ref, ncim_ref, m_ref, p_ref, mr_ref, pr_ref,
             du_ref, gr_ref, gi_ref, xr_ref, xi_ref, dlam_ref, xr, xi, gr, gi, car, acar):
        k = pl.program_id(0)

        @pl.when(k == 0)
        def _():
            acar[...] = jnp.zeros_like(acar)
            dlam_ref[...] = jnp.zeros_like(dlam_ref)

        u = u_ref[...].astype(BF16)
        dy = jnp.where(blk(k) < lat_chunks, dy_ref[...], 0.0).astype(BF16)
        xr[...] = _bd_dot(u, bre_ref)
        xi[...] = _bd_dot(u, bim_ref)
        car[...] = xb_ref[...]
        _scan_chunk(xr, xi, car, m_ref, p_ref, reverse)
        gr[...] = _bd_dot_nt(dy, cre_ref)
        gi[...] = _bd_dot_nt(dy, ncim_ref)
        _scan_chunk(gr, gi, acar, mr_ref, pr_ref, not reverse)
        xrv, xiv, grv, giv = xr[...], xi[...], gr[...], gi[...]
        row = lax.broadcasted_iota(jnp.int32, (t_len, gn), 0)
        first, shift = (t_len - 1, t_len - 1) if reverse else (0, 1)
        xpr = jnp.where(row == first, xb_ref[0:1, :], pltpu.roll(xrv, shift, 0))
        xpi = jnp.where(row == first, xb_ref[1:2, :], pltpu.roll(xiv, shift, 0))
        dlr = grv * xpr + giv * xpi
        dli = giv * xpr - grv * xpi
        dlam_ref[0] += jnp.sum(dlr.reshape(t_len // SUBLANES, SUBLANES, gn), axis=0)
        dlam_ref[1] += jnp.sum(dli.reshape(t_len // SUBLANES, SUBLANES, gn), axis=0)
        grb, gib = grv.astype(BF16), giv.astype(BF16)
        du_ref[...] = _bd_dot_nt(grb, bre_ref) + _bd_dot_nt(gib, bim_ref)
        gr_ref[...] = grb
        gi_ref[...] = gib
        xr_ref[...] = xrv.astype(BF16)
        xi_ref[...] = xiv.astype(BF16)

    def at_blk(width, col=0):
        return pl.BlockSpec((t_len, width), lambda k: (blk(k), col))

    state = jax.ShapeDtypeStruct((l, gn), BF16)
    return pl.pallas_call(
        body, name=name, grid=(n_chunk,),
        in_specs=[at_blk(sw, lay['o_u'] // sw),
                  pl.BlockSpec((t_len, sw), lambda k: (jnp.minimum(blk(k), lat_chunks - 1), 0)),
                  pl.BlockSpec((None, 2, gn), lambda k: (fwd_step(k), 0, 0)),
                  _full(bre.shape), _full(bim.shape), _full(cre.shape), _full(ncim.shape), _full(mtab.shape),
                  _full(ptab.shape), _full(mtab_r.shape), _full(ptab_r.shape)],
        out_specs=[at_blk(sw), at_blk(gn), at_blk(gn), at_blk(gn), at_blk(gn), _full((2, SUBLANES, gn))],
        out_shape=[jax.ShapeDtypeStruct((l, sw), F32), state, state, state, state,
                   jax.ShapeDtypeStruct((2, SUBLANES, gn), F32)],
        scratch_shapes=[pltpu.VMEM((t_len, gn), F32)] * 4 + [pltpu.VMEM((2, gn), F32)] * 2,
        compiler_params=_cparams(("arbitrary",)),
    )(proj, dyr, xb, bre, bim, cre, ncim, mtab, ptab, mtab_r, ptab_r)


def ssm_prep(lam_re, lam_im, logdt, bre, bim, name):
    gn = lam_re.shape[1]

    def body(lr_ref, li_ref, dt_ref, br_ref, bi_ref, pwr_ref, pwi_ref, bbr_ref, bbi_ref):
        _, _, bbr, bbi = _ssm_prep_fn(lr_ref[...], li_ref[...], dt_ref[...], br_ref[...], bi_ref[...])
        bbr_ref[...] = bbr
        bbi_ref[...] = bbi
        kk = (lax.broadcasted_iota(jnp.int32, (SUBLANES, gn), 0) + 1).astype(F32)
        dt = jnp.exp(dt_ref[...])
        ar, ai = lr_ref[...] * dt * kk, li_ref[...] * dt * kk
        e = jnp.exp(ar)
        pwr_ref[...] = e * jnp.cos(ai)
        pwi_ref[...] = e * jnp.sin(ai)

    ins = (lam_re, lam_im, logdt, bre, bim)
    return pl.pallas_call(
        body, name=name,
        out_shape=[jax.ShapeDtypeStruct((SUBLANES, gn), F32)] * 2 + [jax.ShapeDtypeStruct(bre.shape, F32)] * 2,
        compiler_params=_cparams(),
    )(*ins)


def ssm_prep_bwd(lam_re, lam_im, logdt, bre, bim, dlbr, dlbi, dbbr, dbbi, name):
    def body(lr_ref, li_ref, dt_ref, br_ref, bi_ref, g0, g1, g2, g3, o0, o1, o2, o3, o4):
        _, vjp = jax.vjp(_ssm_prep_fn, lr_ref[...], li_ref[...], dt_ref[...], br_ref[...], bi_ref[...])
        for ref, val in zip((o0, o1, o2, o3, o4), vjp((g0[...], g1[...], g2[...], g3[...]))):
            ref[...] = val

    ins = (lam_re, lam_im, logdt, bre, bim)
    return pl.pallas_call(
        body, name=name,
        out_shape=[jax.ShapeDtypeStruct(a.shape, F32) for a in ins],
        compiler_params=_cparams(),
    )(*ins, dlbr, dlbi, dbbr, dbbi)


def mod_fwd(cmat, w, b, name):
    def body(c_ref, w_ref, b_ref, o_ref):
        o_ref[...] = _mod_fn(c_ref[...], w_ref[...]) + b_ref[...]

    return pl.pallas_call(body, name=name, out_shape=jax.ShapeDtypeStruct((cmat.shape[0], w.shape[1]), F32),
                          compiler_params=_cparams())(cmat, w, b)


def mod_bwd(cmat, w, g_lat, g_ctx, g_all, name):
    def body(c_ref, w_ref, gl_ref, gc_ref, ga_ref, dw_ref, dc_ref, db_ref):
        gc = jnp.sum(gc_ref[...], axis=0, keepdims=True)
        dm = jnp.concatenate([gl_ref[...], _pad_rows(gc)], axis=0)
        _, vjp = jax.vjp(_mod_fn, c_ref[...], w_ref[...])
        dc, dw = vjp(dm)
        dw_ref[...] = dw
        dc_ref[...] = dc
        db_ref[...] = _pad_rows(jnp.sum(ga_ref[...], axis=0, keepdims=True))

    return pl.pallas_call(
        body, name=name,
        out_shape=[jax.ShapeDtypeStruct(w.shape, F32), jax.ShapeDtypeStruct(cmat.shape, F32),
                   jax.ShapeDtypeStruct((SUBLANES, g_all.shape[1]), F32)],
        compiler_params=_cparams(),
    )(cmat, w, g_lat, g_ctx, g_all)


def add_own(own, idx, recv, out_dtype, name):
    _, s, r, c = own.shape

    def body(idx_ref, own_ref, recv_ref, o_ref):
        o_ref[...] = (own_ref[...] + recv_ref[...]).astype(o_ref.dtype)

    return pl.pallas_call(
        body, name=name,
        grid_spec=pltpu.PrefetchScalarGridSpec(
            num_scalar_prefetch=1, grid=(s,),
            in_specs=[pl.BlockSpec((None, None, r, c), lambda k, idx_ref: (idx_ref[0], k, 0, 0)),
                      pl.BlockSpec((None, r, c), lambda k, idx_ref: (k, 0, 0))],
            out_specs=pl.BlockSpec((None, r, c), lambda k, idx_ref: (k, 0, 0))),
        out_shape=jax.ShapeDtypeStruct((s, r, c), out_dtype),
        compiler_params=_cparams(("parallel",)),
    )(idx, own, recv)


def reduce_adamw(parts, w, m, v, name):
    s, r, c = parts.shape
    tr = _tile(r, (256, 128, 64, 32, 16, 8))

    def body(p_ref, w_ref, m_ref, v_ref, g_ref, d_ref, nm_ref, nv_ref):
        g = p_ref[0].astype(F32)
        for k in range(1, s):
            g = g + p_ref[k].astype(F32)
        mm = ADAM_B1 * m_ref[...] + (1.0 - ADAM_B1) * g
        vv = ADAM_B2 * v_ref[...] + (1.0 - ADAM_B2) * jnp.square(g)
        m_hat = mm / (1.0 - ADAM_B1 ** ADAM_STEP)
        v_hat = vv / (1.0 - ADAM_B2 ** ADAM_STEP)
        g_ref[...] = g
        d_ref[...] = -ADAM_LR * (m_hat / (jnp.sqrt(v_hat) + ADAM_EPS) + ADAM_WD * w_ref[...])
        nm_ref[...] = mm
        nv_ref[...] = vv

    out = jax.ShapeDtypeStruct((r, c), F32)
    blk = _rows(tr, c)
    return pl.pallas_call(
        body, name=name, grid=(r // tr,),
        in_specs=[pl.BlockSpec((s, tr, c), lambda i: (0, i, 0)), blk, blk, blk],
        out_specs=[blk] * 4, out_shape=[out] * 4,
        compiler_params=_cparams(("parallel",)),
    )(parts, w, m, v)


def _in_layout(d, q, kv, sw):
    o_u = 2 * d
    o_ckv = o_u + sw
    o_kr = o_ckv + kv
    o_cq = -(-(o_kr + LANES) // q) * q
    assert o_u % sw == 0 and o_ckv % kv == 0 and o_kr % LANES == 0
    assert q % LANES == 0 and kv % LANES == 0 and sw % LANES == 0
    return dict(d=d, q=q, kv=kv, sw=sw, o_gl=0, o_u=o_u, o_ckv=o_ckv, o_kr=o_kr, o_cq=o_cq, width=o_cq + q)


def _pad_w_in(w_in, lay):
    q, kv, sw, d = lay['q'], lay['kv'], lay['sw'], lay['d']
    cq, ckv, kr, u, gl = jnp.split(w_in, [q, q + kv, q + kv + QK_ROPE, q + kv + QK_ROPE + sw], axis=1)
    z = lambda w: jnp.zeros((w_in.shape[0], w), w_in.dtype)
    hole = lay['o_cq'] - lay['o_kr'] - LANES
    return jnp.concatenate([gl, u, ckv, z(QK_NOPE), kr, z(LANES - QK_DIM), z(hole), cq], axis=1)


def _unpad_w_in(g, lay):
    q, kv, sw, d = lay['q'], lay['kv'], lay['sw'], lay['d']
    kr0 = lay['o_kr'] + QK_NOPE
    return jnp.concatenate([g[:, lay['o_cq']:lay['o_cq'] + q], g[:, lay['o_ckv']:lay['o_ckv'] + kv],
                            g[:, kr0:kr0 + QK_ROPE], g[:, lay['o_u']:lay['o_u'] + sw], g[:, :2 * d]], axis=1)


def _pad_heads(w, width):
    k = w.shape[0]
    return jnp.pad(w.reshape(k, N_HEADS, width), ((0, 0), (0, 0), (0, HEAD_PAD - width))).reshape(k, HEADS_W)


def _unpad_heads(w, width):
    k = w.shape[0]
    return w.reshape(k, N_HEADS, HEAD_PAD)[:, :, :width].reshape(k, N_HEADS * width)


def _rope_tables(n, nc):
    rows = n // GRID_W
    row = jnp.repeat(jnp.arange(rows), GRID_W)
    col = jnp.tile(jnp.arange(GRID_W), rows)
    pairs = QK_ROPE // 4
    freqs = ROPE_THETA ** (-jnp.arange(pairs, dtype=F32) / pairs)
    ang = jnp.concatenate([row[:, None] * freqs, col[:, None] * freqs], axis=-1)
    cos = jnp.concatenate([jnp.cos(ang), jnp.ones((nc, 2 * pairs), F32)], axis=0)
    sin = jnp.concatenate([jnp.sin(ang), jnp.zeros((nc, 2 * pairs), F32)], axis=0)
    l = n + nc
    half = QK_ROPE // 2
    ct = jnp.concatenate([jnp.ones((l, QK_NOPE), F32), cos, cos, jnp.zeros((l, HEAD_PAD - QK_DIM), F32)], axis=1)
    s1 = jnp.concatenate([jnp.zeros((l, QK_NOPE + half), F32), sin, jnp.zeros((l, HEAD_PAD - QK_DIM), F32)],
                         axis=1)
    s2 = jnp.concatenate([jnp.zeros((l, QK_NOPE), F32), -sin, jnp.zeros((l, HEAD_PAD - QK_NOPE - half), F32)],
                         axis=1)
    return ct, s1, s2


def _group_mask(rows, cols, g):
    return (jnp.arange(rows)[:, None] // (rows // g)) == (jnp.arange(cols)[None, :] // (cols // g))


def _block_diag_rows(m, g):
    return jnp.where(_group_mask(g * m.shape[0], m.shape[1], g), jnp.tile(m, (g, 1)), 0)


def _block_diag_cols(m, g):
    return jnp.where(_group_mask(m.shape[0], g * m.shape[1], g), jnp.tile(m, (1, g)), 0)


def _diag_blocks(m, g):
    a, b = m.shape[0] // g, m.shape[1] // g
    masked = jnp.where(_group_mask(m.shape[0], m.shape[1], g), m, 0)
    return masked.reshape(m.shape[0], g, b).sum(axis=1).reshape(g, a, b)


def _scan_tables(pwr, pwi, reverse):
    row = jnp.arange(SUBLANES)[:, None]
    zero = jnp.zeros_like(pwr)
    levels = []
    for d in SCAN_LEVELS:
        keep = (row < SUBLANES - d) if reverse else (row >= d)
        levels.append(jnp.stack([jnp.where(keep, pwr[d - 1:d, :], zero), jnp.where(keep, pwi[d - 1:d, :], zero)]))
    carry = jnp.stack([pwr[::-1], pwi[::-1]]) if reverse else jnp.stack([pwr, pwi])
    return jnp.stack(levels), carry


def _step(inp):
    x, c, ctx = inp['x'][0], inp['c'], inp['ctx'][0]
    target = inp['loss_target'][0]
    n, d = x.shape
    nc = ctx.shape[0]
    l = n + nc
    q_w, kv_w = inp['q_a_g'].shape[1], inp['kv_a_g'].shape[1]
    sw = inp['d_skip'].shape[1]
    n_grp = sw // SSM_GROUP
    gn = n_grp * SSM_STATE
    lay = _in_layout(d, q_w, kv_w, sw)
    tm = _tile(math.gcd(n, nc), (256, 128))
    me = 4 * lax.axis_index("x") + 2 * lax.axis_index("y") + lax.axis_index("c")
    strip = lambda a: a if a.ndim <= 2 else a[0]
    w = {k: strip(inp[k]) for k in WEIGHT_NAMES}

    gathered_names = list(GATHERED)
    early_names = ['w_in', 'w_uq', 'w_ukv', 'w_o_attn']
    late_names = [k for k in gathered_names if k not in early_names]
    conv_b = w['conv_b']

    (c_all,) = exchange([jnp.broadcast_to(c, (SUBLANES, d))], ['gather'], "gather_c")
    cmat = jnp.concatenate([c_all[:, 0, :], w['c_ctx'][None, :], jnp.zeros((SUBLANES - 1, d), F32)], axis=0)
    mcols = w['w_mod'].shape[1]
    b_cols = lax.dynamic_slice(w['b_mod'], (0, me * mcols), (1, mcols))
    mod_part = mod_fwd(cmat, w['w_mod'], b_cols, "mod_fwd")
    (mod_all,) = exchange([mod_part], ['gather'], "gather_mod")
    mod_me = lax.dynamic_index_in_dim(mod_all, me, axis=1, keepdims=False).reshape(6, d)
    mod_ctx = mod_all[:, SUBLANES, :].reshape(6, d)
    sh1, sc1, g1, sh2, sc2, g2 = [mod_me[k:k + 1] for k in range(6)]
    mods1 = jnp.concatenate([sc1, sh1, mod_ctx[1:2], mod_ctx[0:1], jnp.zeros((4, d), F32)], axis=0)
    mods2 = jnp.concatenate([sc2, sh2, jnp.zeros((6, d), F32)], axis=0)

    xa = jnp.concatenate([x, ctx], axis=0)
    h, *wg = norm_mod_fwd(xa, w['norm1_g'], mods1, n, tm, "norm1_fwd", gather=[w[k].astype(BF16) for k in early_names])
    full = {k: _from_shards(s, GATHERED[k]) for k, s in zip(early_names, wg)}
    w_in_p = _pad_w_in(full['w_in'], lay)
    wuq_p = _pad_heads(full['w_uq'], QK_DIM)
    ukv = full['w_ukv'].reshape(kv_w, N_HEADS, QK_NOPE + V_DIM)
    wk_p = _pad_heads(ukv[:, :, :QK_NOPE].reshape(kv_w, -1), QK_NOPE)
    wv_p = _pad_heads(ukv[:, :, QK_NOPE:].reshape(kv_w, -1), V_DIM)
    wo_p = _pad_heads(full['w_o_attn'].T, V_DIM).T
    proj = matmul(h, w_in_p, 'nn', F32, "in_proj")
    tabs = _rope_tables(n, nc)
    pad_g = lambda g: jnp.pad(g, ((0, 0), (0, HEAD_PAD - QK_DIM)))
    gains = (w['q_a_g'], w['kv_a_g'], pad_g(w['q_norm_g']), pad_g(w['k_norm_g']))
    q, k, v = qkv_fwd(proj, lay, gains, wuq_p, wk_p, wv_p, tabs, tm, "qkv_fwd")
    o, lse, *wg = attn_fwd(q, k, v, n, "attn_fwd", gather=[w[k].astype(BF16) for k in late_names])
    full.update({k: _from_shards(s, GATHERED[k]) for k, s in zip(late_names, wg)})
    w_glu, w_out, w_up, w_down = full['w_glu'], full['w_out'], full['w_up'], full['w_down']
    conv_w = jnp.pad(full['conv_w'].astype(F32), ((0, SUBLANES - 3), (0, 0)))

    b_t = lambda a: a.transpose(2, 0, 1).reshape(SSM_GROUP, gn)
    c_t = lambda a: a.transpose(1, 0, 2).reshape(SSM_GROUP, gn)
    bre_t, bim_t = b_t(w['b_re']), b_t(w['b_im'])
    ssm_in, prep = [], []
    for sfx in ('f', 'b'):
        lam_re, lam_im = w['lam_re_' + sfx].reshape(1, gn), w['lam_im_' + sfx].reshape(1, gn)
        logdt = jnp.repeat(w['log_dt_' + sfx], SSM_STATE, axis=1)
        ssm_in.append((lam_re, lam_im, logdt))
        prep.append(ssm_prep(lam_re, lam_im, logdt, bre_t, bim_t, "ssm_prep_" + sfx))

    def blk_b(bb):
        return _block_diag_rows(bb, n_grp).astype(BF16)

    def blk_c(cc):
        return _block_diag_cols(cc.transpose(0, 2, 1).reshape(gn, SSM_GROUP), n_grp).astype(BF16)

    t_len = _tile(math.gcd(n, nc), (256, 128))
    ssm = []
    for di, sfx in enumerate(('f', 'b')):
        reverse = di == 1
        pwr, pwi, bbr, bbi = prep[di]
        ssm.append(dict(
            sfx=sfx, reverse=reverse, blocks=(blk_b(bbr), blk_b(bbi), blk_c(w['c_re_' + sfx]),
                                              blk_c(-w['c_im_' + sfx])),
            tabs=_scan_tables(pwr, pwi, reverse), adj_tabs=_scan_tables(pwr, -pwi, not reverse)))
    for s in ssm:
        s['y'], s['xb'] = ssm_fwd(proj, lay, *s['blocks'], s['tabs'], n, t_len, s['reverse'], "ssm_fwd_" + s['sfx'])
    yf, yb = ssm[0]['y'], ssm[1]['y']
    s_l = glu_fwd(proj, lay, yf, yb, w['d_skip'], w_glu, n, tm, "glu_fwd")
    x1 = merge_fwd(o, s_l, proj, xa, g1, wo_p, w_out, n, tm, "merge_fwd")

    h2 = norm_mod_fwd(x1, w['norm2_g'], mods2, n, tm, "norm2_fwd")
    up = matmul(h2, w_up, 'nn', F32, "up_proj")
    tw = _tile(n, (256, 128))
    act = conv_act_fwd(up, conv_w, conv_b, tw, "conv_act_fwd")
    dy, ffn, loss_parts = down_loss(act, w_down, x1, g2, target, tm, "down_loss")
    loss = lax.psum(jnp.sum(loss_parts[:, 0, 0]), MESH_AXES)

    dup, dffn, dg2, dconv = ffn_bwd(dy, ffn, up, conv_w, conv_b, w_down, g2, tm, "ffn_bwd")
    g_w_down = matmul(act, dffn, 'tn', F32, "dw_down")
    dh2 = matmul(dup, w_up, 'nt', F32, "dh2")
    g_w_up = matmul(h2, dup, 'tn', F32, "dw_up")
    dx1, dn2g, dmods2 = norm_mod_bwd(x1, dh2, dy, w['norm2_g'], mods2, n, tm, "norm2_bwd")

    do, ds_l, dgl, dg1, g_wo_p, g_w_out = merge_bwd(o, s_l, proj, xa, g1, wo_p, w_out, dx1, n, tm, "merge_bwd")
    du_direct, dyr, dds, g_w_glu = glu_bwd(proj, lay, yf, yb, w['d_skip'], w_glu, ds_l, n, tm, "glu_bwd")
    for s in ssm:
        s['du'], s['gr'], s['gi'], s['xr'], s['xi'], s['dlam'] = ssm_bwd(
            proj, lay, dyr, s['xb'], *s['blocks'], s['tabs'], s['adj_tabs'], n, t_len, s['reverse'],
            "ssm_bwd_" + s['sfx'])

    early_g = {'w_o_attn': _unpad_heads(g_wo_p.T, V_DIM).T, 'w_glu': g_w_glu, 'w_out': g_w_out, 'w_up': g_w_up,
               'conv_w': dconv[0:3], 'w_down': g_w_down}
    dq, dk, dv, *early_recv = attn_bwd(q, k, v, o, do, lse, n, "attn_bwd",
                                       a2a=[_to_shards(g, GATHERED[k]).astype(BF16) for k, g in early_g.items()])
    (dproj, dqag, dkvag, dqng, dkng, g_wuq_p, g_wk_p, g_wv_p) = qkv_bwd(
        proj, lay, gains, wuq_p, wk_p, wv_p, tabs, dq, dk, dv, dgl, du_direct, ssm[0]['du'], ssm[1]['du'], n, tm,
        "qkv_bwd")
    g_w_in_p = matmul(h, dproj, 'tn', F32, "dw_in")
    ukv_g = jnp.concatenate([_unpad_heads(g_wk_p, QK_NOPE).reshape(kv_w, N_HEADS, QK_NOPE),
                             _unpad_heads(g_wv_p, V_DIM).reshape(kv_w, N_HEADS, V_DIM)], axis=2)
    late_g = {'w_in': _unpad_w_in(g_w_in_p, lay), 'w_uq': _unpad_heads(g_wuq_p, QK_DIM),
              'w_ukv': ukv_g.reshape(kv_w, -1)}
    dh, *late_recv = matmul(dproj, w_in_p, 'nt', F32, "dh",
                            a2a=[_to_shards(g, GATHERED[k]).astype(BF16) for k, g in late_g.items()])
    dxa, dn1g, dmods1 = norm_mod_bwd(xa, dh, dx1, w['norm1_g'], mods1, n, tm, "norm1_bwd")
    grad_x = dxa[:n]

    grads = {}
    d_bbar = [None, None]
    gp = n_grp // SSM_SPLIT
    for di, s in enumerate(ssm):
        sfx = s['sfx']
        products = (diag_outer(proj, lay['o_u'], sw, [s['gr'], s['gi']], SSM_SPLIT, "ssm_db_" + sfx)
                    + diag_outer(dyr, 0, sw, [s['xr'], s['xi']], SSM_SPLIT, "ssm_dc_" + sfx))
        g_bre, g_bim, g_cre, g_cim = [
            jnp.concatenate([_diag_blocks(m[p], gp) for p in range(SSM_SPLIT)], axis=0) for m in products]
        grads['c_re_' + sfx] = g_cre
        grads['c_im_' + sfx] = -g_cim
        to_t = lambda a: a.transpose(1, 0, 2).reshape(SSM_GROUP, gn)
        dlam_re = jnp.sum(s['dlam'][0], axis=0, keepdims=True)
        dlam_im = jnp.sum(s['dlam'][1], axis=0, keepdims=True)
        lam_re, lam_im, logdt = ssm_in[di]
        g_lr, g_li, g_dt, g_br, g_bi = ssm_prep_bwd(lam_re, lam_im, logdt, bre_t, bim_t, dlam_re, dlam_im,
                                                    to_t(g_bre), to_t(g_bim), "ssm_prep_bwd_" + sfx)
        grads['lam_re_' + sfx] = g_lr.reshape(n_grp, SSM_STATE)
        grads['lam_im_' + sfx] = g_li.reshape(n_grp, SSM_STATE)
        grads['log_dt_' + sfx] = jnp.sum(g_dt.reshape(n_grp, SSM_STATE), axis=1)[None, :]
        d_bbar[di] = (g_br, g_bi)
    from_t = lambda a: a.reshape(SSM_GROUP, n_grp, SSM_STATE).transpose(1, 2, 0)
    grads['b_re'] = from_t(d_bbar[0][0]) + from_t(d_bbar[1][0])
    grads['b_im'] = from_t(d_bbar[0][1]) + from_t(d_bbar[1][1])

    dmod = jnp.concatenate([dmods1[1:2], dmods1[0:1], dg1[0:1], dmods2[1:2], dmods2[0:1], dg2[0:1]], axis=1)
    dmod_ctx = jnp.concatenate([dmods1[3:4], dmods1[2:3], jnp.zeros((1, 4 * d), F32)], axis=1)
    dm_send = jnp.concatenate([dmod, dmod_ctx, jnp.zeros((SUBLANES - 2, 6 * d), F32)], axis=0)
    (dm_all,) = exchange([dm_send], ['gather'], "gather_dmod")
    g_all = jnp.concatenate([dm_all[:, 0, :], dm_all[:, 1, :]], axis=0)
    cols = lax.dynamic_slice(g_all.reshape(2 * N_DEV, N_DEV, mcols), (0, me, 0), (2 * N_DEV, 1, mcols))[:, 0, :]
    g_w_mod, dcmat, g_b_mod = mod_bwd(cmat, w['w_mod'], cols[:N_DEV], cols[N_DEV:], g_all, "mod_bwd")

    grads.update({'c_ctx': dcmat[SUBLANES], 'b_mod': g_b_mod[0:1], 'norm1_g': dn1g[0:1], 'norm2_g': dn2g[0:1],
                  'q_a_g': dqag[0:1], 'kv_a_g': dkvag[0:1], 'q_norm_g': dqng[0:1, :QK_DIM],
                  'k_norm_g': dkng[0:1, :QK_DIM], 'd_skip': dds[0:1], 'conv_b': dconv[3:4]})
    first = (me == 0).astype(F32)
    rep_parts = [grads[k] * first if k == 'b_mod' else grads[k] for k in REPLICATED]
    rpack, rspans = _pack(rep_parts, SUBLANES)
    my_core = lax.axis_index("c").astype(jnp.int32).reshape(1)
    r_both = jnp.broadcast_to(rpack[None, None], (2, 1) + rpack.shape)
    (from_sibling,) = exchange([r_both], ['others'], "grads_in_chip", group='core')
    r_sum = add_own(r_both, my_core, from_sibling[0], F32, "chip_sum_replicated")[0]
    (r_recv,) = exchange([r_sum], ['gather'], "grads_between_chips", group='chips')

    outs = {}

    def update(parts, k, w_k, m_k, v_k):
        res = reduce_adamw(parts, w_k, m_k, v_k, "adamw_" + k)
        return dict(zip(('grad_', 'delta_', 'new_m_', 'new_v_'), res))

    per_tensor = list(zip(early_g, early_recv)) + list(zip(late_g, late_recv)) + [('w_mod', g_w_mod[None])]
    for k, parts in per_tensor:
        for kind, a in update(parts, k, w[k], strip(inp['m_' + k]), strip(inp['v_' + k])).items():
            outs[kind + k] = a[None]
    rep = lambda prefix: _pack([strip(inp[prefix + k]) for k in REPLICATED], SUBLANES)[0]
    for kind, buf in update(r_recv, "replicated", rep(''), rep('m_'), rep('v_')).items():
        for k, a in zip(REPLICATED, _unpack(buf, rspans, [w[k].shape for k in REPLICATED])):
            outs[kind + k] = a if inp[k].ndim <= 2 else a[None]
    result = [loss, grad_x[None]]
    for kind in ('grad_', 'delta_', 'new_m_', 'new_v_'):
        result += [outs[kind + k] for k in WEIGHT_NAMES]
    return tuple(result)


_ARG_NAMES = (['x', 'c', 'ctx'] + WEIGHT_NAMES + ['loss_target'] + ['m_' + k for k in WEIGHT_NAMES]
              + ['v_' + k for k in WEIGHT_NAMES])


def kernel(*args):
    assert len(args) == len(_ARG_NAMES)
    return _step(dict(zip(_ARG_NAMES, args)))
```

```python
import functools
import math

import jax
import jax.numpy as jnp
from jax import lax
from jax.experimental import pallas as pl
from jax.experimental.pallas import tpu as pltpu

F32 = jnp.float32
BF16 = jnp.bfloat16

N_DEV = 8
MESH_AXES = ("x", "y", "c")
N_HEADS = 8
QK_NOPE = 64
QK_ROPE = 32
QK_DIM = QK_NOPE + QK_ROPE
V_DIM = 64
HEAD_PAD = 128
HEADS_W = N_HEADS * HEAD_PAD
GRID_W = 64
ROPE_THETA = 10000.0
SSM_GROUP = 16
SSM_STATE = 64
EPS = 1e-6
LANES = 128
SUBLANES = 8
PACK_W = 1024
VMEM_LIMIT = 56 * 1024 * 1024
MM_TILES = (1024, 768, 1408, 512, 384, 256, 128)

ADAM_LR = 0.001
ADAM_B1 = 0.9
ADAM_B2 = 0.999
ADAM_EPS = 1e-08
ADAM_WD = 0.01
ADAM_STEP = 10

WEIGHT_NAMES = ['c_ctx', 'w_mod', 'b_mod', 'norm1_g', 'norm2_g', 'w_in', 'q_a_g', 'w_uq', 'kv_a_g', 'w_ukv',
                'q_norm_g', 'k_norm_g', 'w_o_attn', 'lam_re_f', 'lam_im_f', 'log_dt_f', 'c_re_f', 'c_im_f',
                'lam_re_b', 'lam_im_b', 'log_dt_b', 'c_re_b', 'c_im_b', 'b_re', 'b_im', 'd_skip', 'w_glu',
                'w_out', 'w_up', 'conv_w', 'conv_b', 'w_down']
GATHERED = {'w_in': 1, 'w_uq': 1, 'w_ukv': 1, 'w_o_attn': 1, 'w_glu': 1, 'w_out': 0, 'w_up': 1, 'conv_w': 1,
            'w_down': 0}
REPLICATED = [n for n in WEIGHT_NAMES if n not in GATHERED and n != 'w_mod']


def _tile(n, prefs):
    for t in prefs:
        if n % t == 0:
            return t
    return n


def _cparams(sem=None):
    return pltpu.CompilerParams(dimension_semantics=sem, vmem_limit_bytes=VMEM_LIMIT)


@jax.custom_vjp
def bdot(a, w):
    return jnp.dot(a.astype(BF16), w.astype(BF16), preferred_element_type=F32)


def _bdot_fwd(a, w):
    return bdot(a, w), (a, w)


def _bdot_bwd(res, g):
    a, w = res
    gb = g.astype(BF16)
    da = lax.dot_general(gb, w.astype(BF16), (((1,), (1,)), ((), ())), preferred_element_type=F32)
    dw = lax.dot_general(a.astype(BF16), gb, (((0,), (0,)), ((), ())), preferred_element_type=F32)
    return da.astype(a.dtype), dw.astype(w.dtype)


bdot.defvjp(_bdot_fwd, _bdot_bwd)


def _dot_nt(a, b):
    return lax.dot_general(a, b, (((1,), (1,)), ((), ())), preferred_element_type=F32)


def _dot_tn(a, b):
    return lax.dot_general(a, b, (((0,), (0,)), ((), ())), preferred_element_type=F32)


def matmul(a, b, mode, out_dtype, name, a2a=()):
    n_x = len(a2a)
    kinds = ['a2a'] * n_x
    if mode == 'nn':
        (m, k), n = a.shape, b.shape[1]
    elif mode == 'nt':
        (m, k), n = a.shape, b.shape[0]
    else:
        (k, m), n = a.shape, b.shape[1]
    tm = _tile(m, MM_TILES)
    tn = _tile(n, MM_TILES)
    tk = _tile(k, MM_TILES)
    nk = k // tk

    def body(a_ref, b_ref, *rest):
        o_ref, acc_ref = rest[n_x], rest[2 * n_x + 1]
        kk = pl.program_id(2)
        if n_x:
            start, finish = _exchange_ops(rest[:n_x], rest[n_x + 1:2 * n_x + 1], *rest[2 * n_x + 2:], kinds, 'all')
            step = (pl.program_id(0) * (n // tn) + pl.program_id(1)) * nk + kk
            pl.when(step == 0)(start)

        @pl.when(kk == 0)
        def _():
            acc_ref[...] = jnp.zeros_like(acc_ref)

        av, bv = a_ref[...].astype(BF16), b_ref[...].astype(BF16)
        if mode == 'nn':
            acc_ref[...] += jnp.dot(av, bv, preferred_element_type=F32)
        elif mode == 'nt':
            acc_ref[...] += _dot_nt(av, bv)
        else:
            acc_ref[...] += _dot_tn(av, bv)

        @pl.when(kk == nk - 1)
        def _():
            o_ref[...] = acc_ref[...].astype(o_ref.dtype)

        if n_x:
            pl.when(step == (m // tm) * (n // tn) * nk - 1)(finish)

    if mode == 'nn':
        a_spec = pl.BlockSpec((tm, tk), lambda i, j, kk: (i, kk))
        b_spec = pl.BlockSpec((tk, tn), lambda i, j, kk: (kk, j))
    elif mode == 'nt':
        a_spec = pl.BlockSpec((tm, tk), lambda i, j, kk: (i, kk))
        b_spec = pl.BlockSpec((tn, tk), lambda i, j, kk: (j, kk))
    else:
        a_spec = pl.BlockSpec((tk, tm), lambda i, j, kk: (kk, i))
        b_spec = pl.BlockSpec((tk, tn), lambda i, j, kk: (kk, j))
    any_spec = pl.BlockSpec(memory_space=pl.ANY)
    res = pl.pallas_call(
        body, name=name, grid=(m // tm, n // tn, nk),
        in_specs=[a_spec, b_spec] + [any_spec] * n_x,
        out_specs=[pl.BlockSpec((tm, tn), lambda i, j, kk: (i, j))] + [any_spec] * n_x,
        out_shape=[jax.ShapeDtypeStruct((m, n), out_dtype)] + (_exchange_shapes(a2a, kinds, 'all') if n_x else []),
        scratch_shapes=[pltpu.VMEM((tm, tn), F32)] + (_exchange_scratch(n_x, 'all') if n_x else []),
        compiler_params=_cparams(("arbitrary",) * 3 if n_x else ("parallel", "parallel", "arbitrary")),
    )(a, b, *a2a)
    return res if n_x else res[0]


def diag_outer(a, a_col0, a_cols, bs, parts, name):
    k = a.shape[0]
    ka, kb = a_cols // parts, bs[0].shape[1] // parts
    tk = _tile(k, MM_TILES)
    nk = k // tk
    n_b = len(bs)
    assert a_col0 % ka == 0

    def body(a_ref, *rest):
        b_refs, o_refs, acc_refs = rest[:n_b], rest[n_b:2 * n_b], rest[2 * n_b:]
        kk = pl.program_id(1)
        av = a_ref[...].astype(BF16)
        for b_ref, o_ref, acc_ref in zip(b_refs, o_refs, acc_refs):
            @pl.when(kk == 0)
            def _():
                acc_ref[...] = jnp.zeros_like(acc_ref)

            acc_ref[...] += _dot_tn(av, b_ref[...].astype(BF16))

            @pl.when(kk == nk - 1)
            def _():
                o_ref[...] = acc_ref[...]

    return pl.pallas_call(
        body, name=name, grid=(parts, nk),
        in_specs=[pl.BlockSpec((tk, ka), lambda p, kk: (kk, a_col0 // ka + p))]
        + [pl.BlockSpec((tk, kb), lambda p, kk: (kk, p))] * n_b,
        out_specs=[pl.BlockSpec((None, ka, kb), lambda p, kk: (p, 0, 0))] * n_b,
        out_shape=[jax.ShapeDtypeStruct((parts, ka, kb), F32)] * n_b,
        scratch_shapes=[pltpu.VMEM((ka, kb), F32)] * n_b,
        compiler_params=_cparams(("parallel", "arbitrary")),
    )(a, *bs)


N_CHIPS = 4


def _group(group):
    x, y, c = lax.axis_index("x"), lax.axis_index("y"), lax.axis_index("c")
    flips = {'all': [(r & 4, r & 2, r & 1) for r in range(1, 8)],
             'chips': [(0, 1, 0), (1, 0, 0), (1, 1, 0)], 'core': [(0, 0, 1)]}[group]
    index = {'all': lambda px, py, pc: 4 * px + 2 * py + pc, 'chips': lambda px, py, pc: 2 * px + py,
             'core': lambda px, py, pc: pc}[group]
    peers = []
    for fx, fy, fc in flips:
        p = (1 - x if fx else x, 1 - y if fy else y, 1 - c if fc else c)
        peers.append((p, index(*p)))
    return len(flips) + 1, index(x, y, c), peers


def exchange(arrays, kinds, name, group='all'):
    n_arr = len(arrays)

    def body(*refs):
        start, finish = _exchange_ops(refs[:n_arr], refs[n_arr:2 * n_arr], *refs[2 * n_arr:], kinds, group,
                                      own_slot=False)
        start()
        finish()

    any_spec = pl.BlockSpec(memory_space=pl.ANY)
    outs = pl.pallas_call(
        body, name=name,
        in_specs=[any_spec] * n_arr, out_specs=[any_spec] * n_arr, out_shape=_exchange_shapes(arrays, kinds, group),
        scratch_shapes=_exchange_scratch(n_arr, group),
        compiler_params=pltpu.CompilerParams(has_side_effects=True),
    )(*arrays)
    _, me, _ = _group(group)
    own = {'gather': lambda arr: arr, 'a2a': lambda arr: lax.dynamic_index_in_dim(arr, me, 0, keepdims=False)}
    return [out if kind == 'others' else lax.dynamic_update_index_in_dim(out, own[kind](arr), me, 0)
            for arr, kind, out in zip(arrays, kinds, outs)]


GROUP_SIZE = {'all': N_DEV, 'chips': N_CHIPS, 'core': 2}


def _exchange_shapes(arrays, kinds, group):
    size = GROUP_SIZE[group]
    return [jax.ShapeDtypeStruct({'gather': (size,) + tuple(arr.shape), 'a2a': tuple(arr.shape),
                                  'others': (size - 1,) + tuple(arr.shape[1:])}[kind], arr.dtype)
            for arr, kind in zip(arrays, kinds)]


def _exchange_scratch(n_arr, group):
    size = GROUP_SIZE[group]
    return [pltpu.SemaphoreType.DMA((n_arr * size,)), pltpu.SemaphoreType.DMA((n_arr * size,)),
            pltpu.SemaphoreType.DMA((n_arr,))]


def _exchange_ops(srcs, dsts, send_sems, recv_sems, local_sems, kinds, group, own_slot=True):
    n_arr = len(srcs)
    size, me, peers = _group(group)
    has_local = [own_slot and kind != 'others' for kind in kinds]

    def copy(a, r, peer, peer_idx, receiving):
        src = srcs[a] if kinds[a] == 'gather' else srcs[a].at[peer_idx]
        if kinds[a] == 'others':
            dst = dsts[a].at[r]
        else:
            dst = dsts[a].at[peer_idx if receiving else me]
        return pltpu.make_async_remote_copy(
            src_ref=src, dst_ref=dst, send_sem=send_sems.at[a * size + r], recv_sem=recv_sems.at[a * size + r],
            device_id=peer, device_id_type=pl.DeviceIdType.MESH)

    def local(a):
        mine = srcs[a] if kinds[a] == 'gather' else srcs[a].at[me]
        return pltpu.make_async_copy(mine, dsts[a].at[me], local_sems.at[a])

    def start():
        for a in range(n_arr):
            if has_local[a]:
                local(a).start()
            for r, (peer, peer_idx) in enumerate(peers):
                copy(a, r, peer, peer_idx, False).start()

    def finish():
        for a in range(n_arr):
            for r, (peer, peer_idx) in enumerate(peers):
                cp = copy(a, r, peer, peer_idx, True)
                cp.wait_send()
                cp.wait_recv()
            if has_local[a]:
                local(a).wait()

    return start, finish


GATHER_COPIES = 7


def _two_level_gather(srcs, outs, send_sems, recv_sems, local_sems):
    n_arr = len(srcs)
    x, y, c = lax.axis_index("x"), lax.axis_index("y"), lax.axis_index("c")
    me, sibling = (x, y, c), (x, y, 1 - c)
    chips = [(1 - x, y), (x, 1 - y), (1 - x, 1 - y)]
    slot = lambda p: 4 * p[0] + 2 * p[1] + p[2]

    def copy(a, k, block, to, own=False):
        dst = outs[a].at[slot(block)]
        return pltpu.make_async_remote_copy(
            src_ref=srcs[a] if own else dst, dst_ref=dst,
            send_sem=send_sems.at[a * GATHER_COPIES + k], recv_sem=recv_sems.at[a * GATHER_COPIES + k],
            device_id=to, device_id_type=pl.DeviceIdType.MESH)

    def own_copies(a):
        return [copy(a, 0, me, sibling, own=True)] + [copy(a, 1 + j, me, (*chip, c), own=True)
                                                      for j, chip in enumerate(chips)]

    local = lambda a: pltpu.make_async_copy(srcs[a], outs[a].at[slot(me)], local_sems.at[a])

    def start():
        for a in range(n_arr):
            local(a).start()
            for cp in own_copies(a):
                cp.start()

    def finish():
        passed = []
        for j, chip in enumerate(chips):
            for a in range(n_arr):
                copy(a, 1 + j, (*chip, c), me).wait_recv()
                passed.append(copy(a, 4 + j, (*chip, c), sibling))
                passed[-1].start()
        for a in range(n_arr):
            copy(a, 0, sibling, me).wait_recv()
            for j, chip in enumerate(chips):
                copy(a, 4 + j, (*chip, 1 - c), me).wait_recv()
            for cp in own_copies(a):
                cp.wait_send()
        for cp in passed:
            cp.wait_send()
        for a in range(n_arr):
            local(a).wait()

    return start, finish


def _gather_scratch(n_arr):
    return [pltpu.SemaphoreType.DMA((n_arr * GATHER_COPIES,)), pltpu.SemaphoreType.DMA((n_arr * GATHER_COPIES,)),
            pltpu.SemaphoreType.DMA((n_arr,))]


def _pack(parts, row_mult):
    rows, spans, r = [], [], 0
    for p in parts:
        flat = p.reshape(-1)
        nr = -(-flat.shape[0] // (PACK_W * row_mult)) * row_mult
        flat = jnp.pad(flat, (0, nr * PACK_W - flat.shape[0]))
        rows.append(flat.reshape(nr, PACK_W))
        spans.append((r, nr))
        r += nr
    return jnp.concatenate(rows, axis=0), spans


def _unpack(buf, spans, shapes):
    out = []
    for (r, nr), shp in zip(spans, shapes):
        size = math.prod(shp)
        out.append(buf[..., r:r + nr, :].reshape(buf.shape[:-2] + (nr * PACK_W,))[..., :size]
                   .reshape(buf.shape[:-2] + tuple(shp)))
    return out


def _to_shards(full, axis):
    r, c = full.shape
    if axis == 0:
        return full.reshape(N_DEV, r // N_DEV, c)
    return full.reshape(r, N_DEV, c // N_DEV).transpose(1, 0, 2)


def _from_shards(sh, axis):
    _, r, c = sh.shape
    if axis == 0:
        return sh.reshape(N_DEV * r, c)
    return sh.transpose(1, 0, 2).reshape(r, N_DEV * c)


def _rms(x, g, n):
    ms = jnp.sum(x * x, axis=-1, keepdims=True) * (1.0 / n)
    return x * lax.rsqrt(ms + EPS) * g


def _norm_mod_fn(x, g, sc, sh):
    return _rms(x, g, x.shape[-1]) * (1.0 + sc) + sh


@jax.custom_vjp
def _rope(t, ct, s1, s2):
    return t * ct + pltpu.roll(t, 16, 1) * s1 + pltpu.roll(t, HEAD_PAD - 16, 1) * s2


def _rope_fwd(t, ct, s1, s2):
    return _rope(t, ct, s1, s2), (ct, s1, s2)


def _rope_bwd(res, d):
    ct, s1, s2 = res
    dt = d * ct + pltpu.roll(d * s1, HEAD_PAD - 16, 1) + pltpu.roll(d * s2, 16, 1)
    return dt, jnp.zeros_like(ct), jnp.zeros_like(s1), jnp.zeros_like(s2)


_rope.defvjp(_rope_fwd, _rope_bwd)


def _qkv_fn(cq, ckv, krsec, qag, kvag, qng, kng, wuq, wk, wv, ct, s1, s2):
    q_raw = bdot(_rms(cq, qag, cq.shape[-1]), wuq)
    ckvn = _rms(ckv, kvag, ckv.shape[-1])
    k_raw = bdot(ckvn, wk)
    v = bdot(ckvn, wv)
    qs, ks = [], []
    for h in range(N_HEADS):
        sl = slice(h * HEAD_PAD, (h + 1) * HEAD_PAD)
        qs.append(_rope(_rms(q_raw[:, sl], qng, QK_DIM), ct, s1, s2))
        ks.append(_rope(_rms(k_raw[:, sl] + krsec, kng, QK_DIM), ct, s1, s2))
    return jnp.concatenate(qs, axis=1), jnp.concatenate(ks, axis=1), v


def _glu_fn(u, yf, yb, dskip, wglu):
    y = u * dskip + yf + yb
    vg = bdot(jax.nn.gelu(y), wglu)
    d = vg.shape[-1] // 2
    return vg[:, :d] * jax.nn.sigmoid(vg[:, d:])


def _merge_fn(o, s_l, gl, x, g1, wo, wout):
    d = x.shape[-1]
    a = bdot(o, wo)
    mix = jax.nn.sigmoid(gl[:, :d]) * a + jax.nn.sigmoid(gl[:, d:]) * s_l
    return x + g1 * bdot(mix, wout)


def _mod_fn(cmat, w):
    return bdot(jax.nn.silu(cmat), w)


def _ssm_prep_fn(lam_re, lam_im, logdt, bre, bim):
    dt = jnp.exp(logdt)
    ar, ai = lam_re * dt, lam_im * dt
    e = jnp.exp(ar)
    lbr, lbi = e * jnp.cos(ai), e * jnp.sin(ai)
    nr, ni = lbr - 1.0, lbi
    den = lam_re * lam_re + lam_im * lam_im
    qr = (nr * lam_re + ni * lam_im) / den
    qi = (ni * lam_re - nr * lam_im) / den
    return lbr, lbi, qr * bre - qi * bim, qr * bim + qi * bre


def _rows(tm, w, col=0):
    return pl.BlockSpec((tm, w), lambda i: (i, col))


def _full(shape):
    nd = len(shape)
    return pl.BlockSpec(tuple(shape), lambda i: (0,) * nd)


def _acc_add(i, ref, val):
    @pl.when(i == 0)
    def _():
        ref[...] = jnp.zeros_like(ref)
    ref[...] += val


def _pad_rows(v, rows=SUBLANES):
    sel = lax.broadcasted_iota(jnp.int32, (rows, v.shape[-1]), 0) == 0
    return jnp.where(sel, jnp.broadcast_to(v, (rows, v.shape[-1])), 0.0)


def norm_mod_fwd(xa, g, mods, n_lat, tm, name, gather=()):
    r, d = xa.shape
    lat_tiles = n_lat // tm
    n_tiles = r // tm
    n_g = len(gather)

    def body(x_ref, g_ref, m_ref, *rest):
        o_ref = rest[n_g]
        i = pl.program_id(0)
        if n_g:
            start, finish = _two_level_gather(rest[:n_g], rest[n_g + 1:2 * n_g + 1], *rest[2 * n_g + 1:])
            pl.when(i == 0)(start)
        lat = i < lat_tiles
        sc = jnp.where(lat, m_ref[0:1, :], m_ref[2:3, :])
        sh = jnp.where(lat, m_ref[1:2, :], m_ref[3:4, :])
        o_ref[...] = _norm_mod_fn(x_ref[...], g_ref[...], sc, sh).astype(o_ref.dtype)
        if n_g:
            pl.when(i == n_tiles - 1)(finish)

    any_spec = pl.BlockSpec(memory_space=pl.ANY)
    res = pl.pallas_call(
        body, name=name, grid=(n_tiles,),
        in_specs=[_rows(tm, d), _full(g.shape), _full(mods.shape)] + [any_spec] * n_g,
        out_specs=[_rows(tm, d)] + [any_spec] * n_g,
        out_shape=[jax.ShapeDtypeStruct((r, d), BF16)]
        + [jax.ShapeDtypeStruct((N_DEV,) + tuple(a.shape), a.dtype) for a in gather],
        scratch_shapes=_gather_scratch(n_g) if n_g else [],
        compiler_params=_cparams(("arbitrary",) if n_g else ("parallel",)),
    )(xa, g, mods, *gather)
    return res if n_g else res[0]


def norm_mod_bwd(xa, dh, dres, g, mods, n_lat, tm, name):
    r, d = xa.shape
    lat_tiles = n_lat // tm

    def body(x_ref, dh_ref, dres_ref, g_ref, m_ref, dx_ref, dg_ref, dm_ref):
        i = pl.program_id(0)
        lat = i < lat_tiles
        sc = jnp.where(lat, m_ref[0:1, :], m_ref[2:3, :])
        sh = jnp.where(lat, m_ref[1:2, :], m_ref[3:4, :])
        _, vjp = jax.vjp(_norm_mod_fn, x_ref[...], g_ref[...], sc, sh)
        dx, dg, dsc, dsh = vjp(dh_ref[...])
        dx_ref[...] = dx + jnp.where(lat, dres_ref[...], 0.0)
        _acc_add(i, dg_ref, _pad_rows(dg))
        row = lax.broadcasted_iota(jnp.int32, (SUBLANES, d), 0)
        base = jnp.where(lat, 0, 2)
        upd = jnp.where(row == base, jnp.broadcast_to(dsc, (SUBLANES, d)), 0.0)
        upd = upd + jnp.where(row == base + 1, jnp.broadcast_to(dsh, (SUBLANES, d)), 0.0)
        _acc_add(i, dm_ref, upd)

    return pl.pallas_call(
        body, name=name, grid=(r // tm,),
        in_specs=[_rows(tm, d), _rows(tm, d),
                  pl.BlockSpec((tm, d), lambda i: (jnp.minimum(i, lat_tiles - 1), 0)),
                  _full(g.shape), _full(mods.shape)],
        out_specs=[_rows(tm, d), _full((SUBLANES, d)), _full((SUBLANES, d))],
        out_shape=[jax.ShapeDtypeStruct((r, d), F32), jax.ShapeDtypeStruct((SUBLANES, d), F32),
                   jax.ShapeDtypeStruct((SUBLANES, d), F32)],
        compiler_params=_cparams(("arbitrary",)),
    )(xa, dh, dres, g, mods)


def qkv_fwd(proj, lay, gains, wuq, wk, wv, tabs, tm, name):
    r = proj.shape[0]
    q_w, kv_w = lay['q'], lay['kv']

    def body(cq_ref, ckv_ref, kr_ref, qag, kvag, qng, kng, wuq_ref, wk_ref, wv_ref, ct, s1, s2, q_ref, k_ref,
             v_ref):
        q, k, v = _qkv_fn(cq_ref[...], ckv_ref[...], kr_ref[...], qag[...], kvag[...], qng[...], kng[...],
                          wuq_ref[...], wk_ref[...], wv_ref[...], ct[...], s1[...], s2[...])
        q_ref[...] = q.astype(BF16)
        k_ref[...] = k.astype(BF16)
        v_ref[...] = v.astype(BF16)

    out = jax.ShapeDtypeStruct((r, HEADS_W), BF16)
    return pl.pallas_call(
        body, name=name, grid=(r // tm,),
        in_specs=[_rows(tm, q_w, lay['o_cq'] // q_w), _rows(tm, kv_w, lay['o_ckv'] // kv_w),
                  _rows(tm, LANES, lay['o_kr'] // LANES)]
        + [_full(a.shape) for a in gains] + [_full(wuq.shape), _full(wk.shape), _full(wv.shape)]
        + [_rows(tm, HEAD_PAD)] * 3,
        out_specs=[_rows(tm, HEADS_W)] * 3, out_shape=[out, out, out],
        compiler_params=_cparams(("parallel",)),
    )(proj, proj, proj, *gains, wuq, wk, wv, *tabs)


def qkv_bwd(proj, lay, gains, wuq, wk, wv, tabs, dq, dk, dv, dgl, du_direct, du_f, du_b, n, tm, name):
    r = proj.shape[0]
    q_w, kv_w, sw, d = lay['q'], lay['kv'], lay['sw'], lay['d']
    lat_tiles = n // tm

    def body(cq_ref, ckv_ref, kr_ref, qag, kvag, qng, kng, wuq_ref, wk_ref, wv_ref, ct, s1, s2, dq_ref, dk_ref,
             dv_ref, dgl_ref, dud_ref, duf_ref, dub_ref, dp_ref, dqag, dkvag, dqng, dkng, dwuq, dwk, dwv):
        i = pl.program_id(0)
        lat = i < lat_tiles
        tables = (ct[...], s1[...], s2[...])
        fn = lambda *a: _qkv_fn(*a, *tables)
        _, vjp = jax.vjp(fn, cq_ref[...], ckv_ref[...], kr_ref[...], qag[...], kvag[...], qng[...], kng[...],
                         wuq_ref[...].astype(F32), wk_ref[...].astype(F32), wv_ref[...].astype(F32))
        g = vjp((jnp.where(lat, dq_ref[...], 0.0), dk_ref[...], dv_ref[...]))
        dp_ref[:, 0:2 * d] = jnp.where(lat, dgl_ref[...], 0.0).astype(BF16)
        dp_ref[:, lay['o_u']:lay['o_u'] + sw] = (duf_ref[...] + dub_ref[...]
                                                 + jnp.where(lat, dud_ref[...], 0.0)).astype(BF16)
        dp_ref[:, lay['o_ckv']:lay['o_ckv'] + kv_w] = g[1].astype(BF16)
        dp_ref[:, lay['o_kr']:lay['o_kr'] + LANES] = g[2].astype(BF16)
        hole0 = lay['o_kr'] + LANES
        if lay['o_cq'] > hole0:
            dp_ref[:, hole0:lay['o_cq']] = jnp.zeros((tm, lay['o_cq'] - hole0), BF16)
        dp_ref[:, lay['o_cq']:lay['o_cq'] + q_w] = g[0].astype(BF16)
        for ref, val in zip((dqag, dkvag, dqng, dkng), g[3:7]):
            _acc_add(i, ref, _pad_rows(val))
        for ref, val in zip((dwuq, dwk, dwv), g[7:10]):
            _acc_add(i, ref, val)

    def lat_rows(w):
        return pl.BlockSpec((tm, w), lambda i: (jnp.minimum(i, lat_tiles - 1), 0))

    acc_shapes = [(SUBLANES, a.shape[1]) for a in gains] + [wuq.shape, wk.shape, wv.shape]
    return pl.pallas_call(
        body, name=name, grid=(r // tm,),
        in_specs=[_rows(tm, q_w, lay['o_cq'] // q_w), _rows(tm, kv_w, lay['o_ckv'] // kv_w),
                  _rows(tm, LANES, lay['o_kr'] // LANES)]
        + [_full(a.shape) for a in gains] + [_full(wuq.shape), _full(wk.shape), _full(wv.shape)]
        + [_rows(tm, HEAD_PAD)] * 3 + [lat_rows(HEADS_W), _rows(tm, HEADS_W), _rows(tm, HEADS_W)]
        + [lat_rows(2 * d), lat_rows(sw), _rows(tm, sw), _rows(tm, sw)],
        out_specs=[_rows(tm, lay['width'])] + [_full(s) for s in acc_shapes],
        out_shape=[jax.ShapeDtypeStruct((r, lay['width']), BF16)] + [jax.ShapeDtypeStruct(s, F32) for s in acc_shapes],
        compiler_params=_cparams(("arbitrary",)),
    )(proj, proj, proj, *gains, wuq, wk, wv, *tabs, dq, dk, dv, dgl, du_direct, du_f, du_b)


def glu_fwd(proj, lay, yf, yb, dskip, wglu, n, tm, name):
    sw, d = wglu.shape[0], wglu.shape[1] // 2

    def body(u_ref, yf_ref, yb_ref, ds_ref, w_ref, o_ref):
        o_ref[...] = _glu_fn(u_ref[...], yf_ref[...], yb_ref[...], ds_ref[...], w_ref[...])

    return pl.pallas_call(
        body, name=name, grid=(n // tm,),
        in_specs=[_rows(tm, sw, lay['o_u'] // sw), _rows(tm, sw), _rows(tm, sw), _full(dskip.shape),
                  _full(wglu.shape)],
        out_specs=_rows(tm, d), out_shape=jax.ShapeDtypeStruct((n, d), F32),
        compiler_params=_cparams(("parallel",)),
    )(proj, yf, yb, dskip, wglu)


def glu_bwd(proj, lay, yf, yb, dskip, wglu, ds_l, n, tm, name):
    sw, d = wglu.shape[0], wglu.shape[1] // 2

    def body(u_ref, yf_ref, yb_ref, ds_ref, w_ref, g_ref, du_ref, dy_ref, dds_ref, dw_ref):
        i = pl.program_id(0)
        _, vjp = jax.vjp(_glu_fn, u_ref[...], yf_ref[...], yb_ref[...], ds_ref[...], w_ref[...].astype(F32))
        du, dyf, _, dds, dw = vjp(g_ref[...])
        du_ref[...] = du
        dy_ref[...] = dyf
        _acc_add(i, dds_ref, _pad_rows(dds))
        _acc_add(i, dw_ref, dw)

    return pl.pallas_call(
        body, name=name, grid=(n // tm,),
        in_specs=[_rows(tm, sw, lay['o_u'] // sw), _rows(tm, sw), _rows(tm, sw), _full(dskip.shape),
                  _full(wglu.shape), _rows(tm, d)],
        out_specs=[_rows(tm, sw), _rows(tm, sw), _full((SUBLANES, sw)), _full(wglu.shape)],
        out_shape=[jax.ShapeDtypeStruct((n, sw), F32), jax.ShapeDtypeStruct((n, sw), F32),
                   jax.ShapeDtypeStruct((SUBLANES, sw), F32), jax.ShapeDtypeStruct(wglu.shape, F32)],
        compiler_params=_cparams(("arbitrary",)),
    )(proj, yf, yb, dskip, wglu, ds_l)


def merge_fwd(o, s_l, proj, xa, g1, wo, wout, n, tm, name):
    d = xa.shape[1]

    def body(o_ref, s_ref, gl_ref, x_ref, g1_ref, wo_ref, wout_ref, x1_ref):
        x1_ref[...] = _merge_fn(o_ref[...], s_ref[...], gl_ref[...], x_ref[...], g1_ref[...], wo_ref[...],
                                wout_ref[...])

    return pl.pallas_call(
        body, name=name, grid=(n // tm,),
        in_specs=[_rows(tm, HEADS_W), _rows(tm, d), _rows(tm, 2 * d), _rows(tm, d), _full(g1.shape),
                  _full(wo.shape), _full(wout.shape)],
        out_specs=_rows(tm, d), out_shape=jax.ShapeDtypeStruct((n, d), F32),
        compiler_params=_cparams(("parallel",)),
    )(o, s_l, proj, xa, g1, wo, wout)


def merge_bwd(o, s_l, proj, xa, g1, wo, wout, dx1, n, tm, name):
    d = xa.shape[1]

    def body(o_ref, s_ref, gl_ref, x_ref, g1_ref, wo_ref, wout_ref, dx1_ref, do_ref, ds_ref, dgl_ref, dg1_ref,
             dwo_ref, dwout_ref):
        i = pl.program_id(0)
        _, vjp = jax.vjp(_merge_fn, o_ref[...], s_ref[...], gl_ref[...], x_ref[...], g1_ref[...],
                         wo_ref[...].astype(F32), wout_ref[...].astype(F32))
        do, ds, dgl, _, dg1, dwo, dwout = vjp(dx1_ref[...])
        do_ref[...] = do
        ds_ref[...] = ds
        dgl_ref[...] = dgl
        _acc_add(i, dg1_ref, _pad_rows(dg1))
        _acc_add(i, dwo_ref, dwo)
        _acc_add(i, dwout_ref, dwout)

    return pl.pallas_call(
        body, name=name, grid=(n // tm,),
        in_specs=[_rows(tm, HEADS_W), _rows(tm, d), _rows(tm, 2 * d), _rows(tm, d), _full(g1.shape),
                  _full(wo.shape), _full(wout.shape), _rows(tm, d)],
        out_specs=[_rows(tm, HEADS_W), _rows(tm, d), _rows(tm, 2 * d), _full((SUBLANES, d)), _full(wo.shape),
                   _full(wout.shape)],
        out_shape=[jax.ShapeDtypeStruct((n, HEADS_W), BF16), jax.ShapeDtypeStruct((n, d), F32),
                   jax.ShapeDtypeStruct((n, 2 * d), F32), jax.ShapeDtypeStruct((SUBLANES, d), F32),
                   jax.ShapeDtypeStruct(wo.shape, F32), jax.ShapeDtypeStruct(wout.shape, F32)],
        compiler_params=_cparams(("arbitrary",)),
    )(o, s_l, proj, xa, g1, wo, wout, dx1)


def _halo_specs(tm, w, n):
    nb = n // SUBLANES
    per = tm // SUBLANES
    prev = pl.BlockSpec((SUBLANES, w), lambda i: (jnp.maximum(i * per - 1, 0), 0))
    nxt = pl.BlockSpec((SUBLANES, w), lambda i: (jnp.minimum((i + 1) * per, nb - 1), 0))
    return prev, nxt


def _shifted(t, prev_blk, next_blk, i, n_tiles):
    tm = t.shape[0]
    row = lax.broadcasted_iota(jnp.int32, t.shape, 0)
    prev_row = jnp.where(i > 0, prev_blk[SUBLANES - 1:SUBLANES, :], 0.0)
    next_row = jnp.where(i < n_tiles - 1, next_blk[0:1, :], 0.0)
    before = jnp.where(row == 0, prev_row, pltpu.roll(t, 1, 0))
    after = jnp.where(row == tm - 1, next_row, pltpu.roll(t, tm - 1, 0))
    return before, after


def _conv_u2(up, before, after, cw, cb):
    return before * cw[0:1, :] + up * cw[1:2, :] + after * cw[2:3, :] + cb


def conv_act_fwd(up, cw, cb, tm, name):
    n, w2 = up.shape
    f = w2 // 2
    n_tiles = n // tm
    prev, nxt = _halo_specs(tm, w2, n)

    def body(up_ref, prev_ref, next_ref, cw_ref, cb_ref, act_ref):
        i = pl.program_id(0)
        t = up_ref[...]
        before, after = _shifted(t, prev_ref[...], next_ref[...], i, n_tiles)
        u2 = _conv_u2(t, before, after, cw_ref[...], cb_ref[...])
        act_ref[...] = (jax.nn.silu(u2[:, f:]) * u2[:, :f]).astype(BF16)

    return pl.pallas_call(
        body, name=name, grid=(n_tiles,),
        in_specs=[_rows(tm, w2), prev, nxt, _full(cw.shape), _full(cb.shape)],
        out_specs=_rows(tm, f), out_shape=jax.ShapeDtypeStruct((n, f), BF16),
        compiler_params=_cparams(("parallel",)),
    )(up, up, up, cw, cb)


def down_loss(act, wdown, x1, g2, target, tm, name):
    n, d = x1.shape
    f = act.shape[1]

    def body(act_ref, w_ref, x1_ref, g2_ref, t_ref, dy_ref, ffn_ref, loss_ref):
        ffn = jnp.dot(act_ref[...], w_ref[...], preferred_element_type=F32)
        err = x1_ref[...] + g2_ref[...] * ffn - t_ref[...]
        ffn_ref[...] = ffn
        dy_ref[...] = err * (1.0 / d)
        part = 0.5 * jnp.sum(jnp.sum(err * err, axis=-1, keepdims=True) * (1.0 / d), axis=0, keepdims=True)
        loss_ref[0] = jnp.broadcast_to(part, (SUBLANES, LANES))

    return pl.pallas_call(
        body, name=name, grid=(n // tm,),
        in_specs=[_rows(tm, f), _full(wdown.shape), _rows(tm, d), _full(g2.shape), _rows(tm, d)],
        out_specs=[_rows(tm, d), _rows(tm, d), pl.BlockSpec((1, SUBLANES, LANES), lambda i: (i, 0, 0))],
        out_shape=[jax.ShapeDtypeStruct((n, d), F32), jax.ShapeDtypeStruct((n, d), F32),
                   jax.ShapeDtypeStruct((n // tm, SUBLANES, LANES), F32)],
        compiler_params=_cparams(("parallel",)),
    )(act, wdown, x1, g2, target)


FFN_BWD_PARTS = 2


def ffn_bwd(dy, ffn, up, cw, cb, wdown, g2, tm, name):
    n, d = dy.shape
    w2 = up.shape[1]
    f = w2 // 2
    fc = f // FFN_BWD_PARTS
    assert fc % LANES == 0
    n_tiles = n // tm
    ext = tm + 2 * SUBLANES
    inner = slice(SUBLANES, SUBLANES + tm)
    prev_w, next_w = _halo_specs(tm, w2, n)
    prev_d, next_d = _halo_specs(tm, d, n)

    def body(dy_ref, dyp_ref, dyn_ref, ffn_ref, up_ref, upp_ref, upn_ref, cw_ref, cb_ref, w_ref, g2_ref,
             dup_ref, dffn_ref, dg2_ref, dcw_ref):
        i = pl.program_id(0)
        has_prev, has_next = i > 0, i < n_tiles - 1

        def extended(prev, tile, nxt):
            return jnp.concatenate([jnp.where(has_prev, prev, 0.0), tile, jnp.where(has_next, nxt, 0.0)], axis=0)

        dyv = dy_ref[...]
        dffn = extended(dyp_ref[...], dyv, dyn_ref[...]) * g2_ref[...]
        dffn_ref[...] = dffn[inner].astype(BF16)
        dffn = dffn.astype(BF16)
        _acc_add(i, dg2_ref, _pad_rows(jnp.sum(dyv * ffn_ref[...], axis=0, keepdims=True)))
        row = lax.broadcasted_iota(jnp.int32, (SUBLANES, fc), 0)
        shift = lambda t: (pltpu.roll(t, 1, 0), pltpu.roll(t, ext - 1, 0))
        for part in range(FFN_BWD_PARTS):
            halves = []
            for col0 in (part * fc, f + part * fc):
                cols = slice(col0, col0 + fc)
                t = extended(upp_ref[:, cols], up_ref[:, cols], upn_ref[:, cols])
                before, after = shift(t)
                halves.append((cols, t, _conv_u2(t, before, after, cw_ref[:, cols], cb_ref[:, cols])))
            (_, _, val), (_, _, gate) = halves
            dact = _dot_nt(dffn, w_ref[part * fc:(part + 1) * fc, :])
            sg = jax.nn.sigmoid(gate)
            for (cols, t, _), du2 in zip(halves, (dact * (gate * sg),
                                                   dact * val * (sg * (1.0 + gate * (1.0 - sg))))):
                before, after = shift(du2)
                cwv = cw_ref[:, cols]
                dup_ref[:, cols] = (after * cwv[0:1, :] + du2 * cwv[1:2, :] + before * cwv[2:3, :])[inner].astype(BF16)
                upd = jnp.zeros((SUBLANES, fc), F32)
                for k, term in enumerate((after * t, du2 * t, before * t, du2)):
                    upd = upd + jnp.where(row == k, jnp.broadcast_to(
                        jnp.sum(term[inner], axis=0, keepdims=True), (SUBLANES, fc)), 0.0)

                @pl.when(i == 0)
                def _():
                    dcw_ref[:, cols] = jnp.zeros((SUBLANES, fc), F32)

                dcw_ref[:, cols] += upd

    return pl.pallas_call(
        body, name=name, grid=(n_tiles,),
        in_specs=[_rows(tm, d), prev_d, next_d, _rows(tm, d), _rows(tm, w2), prev_w, next_w, _full(cw.shape),
                  _full(cb.shape), _full(wdown.shape), _full(g2.shape)],
        out_specs=[_rows(tm, w2), _rows(tm, d), _full((SUBLANES, d)), _full((SUBLANES, w2))],
        out_shape=[jax.ShapeDtypeStruct((n, w2), BF16), jax.ShapeDtypeStruct((n, d), BF16),
                   jax.ShapeDtypeStruct((SUBLANES, d), F32), jax.ShapeDtypeStruct((SUBLANES, w2), F32)],
        compiler_params=_cparams(("arbitrary",)),
    )(dy, dy, dy, ffn, up, up, up, cw, cb, wdown, g2)


def attn_fwd(q, k, v, n, name, gather=()):
    nk = k.shape[0]
    tq = _tile(n, (1024, 512, 256, 128))
    tk = _tile(nk, (768, 384, 256, 128))
    n_kv = nk // tk
    n_q = n // tq
    n_g = len(gather)
    scale = QK_DIM ** -0.5
    c2 = scale * math.log2(math.e)

    def body(q_ref, k_ref, v_ref, *rest):
        o_ref, lse_ref = rest[n_g:n_g + 2]
        if n_g:
            start, finish = _two_level_gather(rest[:n_g], rest[n_g + 2:2 * n_g + 2], *rest[2 * n_g + 2:])
            step = pl.program_id(0) * n_q + pl.program_id(1)
            pl.when(step == 0)(start)
        qv = q_ref[...]
        ones_col = (lax.broadcasted_iota(jnp.int32, (tk, HEAD_PAD), 1) == V_DIM).astype(BF16)

        def chunk(j, carry):
            m, acc = carry
            rows = pl.ds(pl.multiple_of(j * tk, tk), tk)
            s = _dot_nt(qv, k_ref[rows, :])
            m_new = jnp.maximum(m, jnp.max(s, axis=-1, keepdims=True))
            p = jnp.exp2(s * c2 - m_new * c2)
            alpha = jnp.exp2((m - m_new) * c2)
            pv = jnp.dot(p.astype(BF16), v_ref[rows, :] + ones_col, preferred_element_type=F32)
            return m_new, alpha * acc + pv

        m, acc = lax.fori_loop(0, n_kv, chunk, (jnp.full((tq, 1), -jnp.inf, F32),
                                                jnp.zeros((tq, HEAD_PAD), F32)), unroll=True)
        l = acc[:, V_DIM:V_DIM + 1]
        lane = lax.broadcasted_iota(jnp.int32, (tq, HEAD_PAD), 1)
        o_ref[...] = jnp.where(lane < V_DIM, acc / l, 0.0).astype(BF16)
        lse_ref[...] = jnp.broadcast_to(m * scale + jnp.log(l), (tq, HEAD_PAD))
        if n_g:
            pl.when(step == N_HEADS * n_q - 1)(finish)

    qspec = pl.BlockSpec((tq, HEAD_PAD), lambda h, i: (i, h))
    kspec = pl.BlockSpec((nk, HEAD_PAD), lambda h, i: (0, h))
    any_spec = pl.BlockSpec(memory_space=pl.ANY)
    return pl.pallas_call(
        body, name=name, grid=(N_HEADS, n_q),
        in_specs=[qspec, kspec, kspec] + [any_spec] * n_g, out_specs=[qspec, qspec] + [any_spec] * n_g,
        out_shape=[jax.ShapeDtypeStruct((n, HEADS_W), BF16), jax.ShapeDtypeStruct((n, HEADS_W), F32)]
        + [jax.ShapeDtypeStruct((N_DEV,) + tuple(a.shape), a.dtype) for a in gather],
        scratch_shapes=_gather_scratch(n_g) if n_g else [],
        compiler_params=_cparams(("arbitrary", "arbitrary") if n_g else ("parallel", "parallel")),
    )(q, k, v, *gather)


def attn_bwd(q, k, v, o, do, lse, n, name, a2a=()):
    nk = k.shape[0]
    tq = _tile(n, (1024, 512, 256, 128))
    tk = _tile(nk, (768, 384, 256, 128))
    n_kv = nk // tk
    n_x = len(a2a)
    kinds = ['a2a'] * n_x
    scale = QK_DIM ** -0.5
    log2e = math.log2(math.e)
    c2 = scale * log2e

    def body(q_ref, k_ref, v_ref, o_ref, do_ref, lse_ref, *rest):
        dq_ref, dk_ref, dv_ref = rest[n_x:n_x + 3]
        if n_x:
            start, finish = _exchange_ops(rest[:n_x], rest[n_x + 3:2 * n_x + 3], *rest[2 * n_x + 3:], kinds, 'all')
            step = pl.program_id(0) * n_kv + pl.program_id(1)
            pl.when(step == 0)(start)

        @pl.when(pl.program_id(1) == 0)
        def _():
            dq_ref[...] = jnp.zeros_like(dq_ref)

        kv, vv = k_ref[...], v_ref[...]

        def q_tile(i, carry):
            dk, dv = carry
            rows = pl.ds(pl.multiple_of(i * tq, tq), tq)
            qv, dov = q_ref[rows, :], do_ref[rows, :]
            p = jnp.exp2(_dot_nt(qv, kv) * c2 - lse_ref[rows, 0:1] * log2e)
            dv = dv + _dot_tn(p.astype(BF16), dov)
            dp = _dot_nt(dov, vv)
            delta = jnp.sum(dov.astype(F32) * o_ref[rows, :].astype(F32), axis=-1, keepdims=True)
            ds = (p * (dp - delta) * scale).astype(BF16)
            dk = dk + _dot_tn(ds, qv)
            dq_ref[rows, :] += jnp.dot(ds, kv, preferred_element_type=F32)
            return dk, dv

        zero = jnp.zeros((tk, HEAD_PAD), F32)
        dk, dv = lax.fori_loop(0, n // tq, q_tile, (zero, zero), unroll=2)
        dk_ref[...] = dk
        dv_ref[...] = dv
        if n_x:
            pl.when(step == N_HEADS * n_kv - 1)(finish)

    qspec = pl.BlockSpec((n, HEAD_PAD), lambda h, j: (0, h))
    kspec = pl.BlockSpec((tk, HEAD_PAD), lambda h, j: (j, h))
    any_spec = pl.BlockSpec(memory_space=pl.ANY)
    return pl.pallas_call(
        body, name=name, grid=(N_HEADS, n_kv),
        in_specs=[qspec, kspec, kspec, qspec, qspec, qspec] + [any_spec] * n_x,
        out_specs=[qspec, kspec, kspec] + [any_spec] * n_x,
        out_shape=[jax.ShapeDtypeStruct((n, HEADS_W), F32), jax.ShapeDtypeStruct((nk, HEADS_W), F32),
                   jax.ShapeDtypeStruct((nk, HEADS_W), F32)] + _exchange_shapes(a2a, kinds, 'all'),
        scratch_shapes=_exchange_scratch(n_x, 'all') if n_x else [],
        compiler_params=_cparams(("arbitrary", "arbitrary") if n_x else ("parallel", "arbitrary")),
    )(q, k, v, o, do, lse, *a2a)


SCAN_LEVELS = (1, 2, 4)
SCAN_LANES = 512


def _scan_chunk(xr, xi, car, m_ref, p_ref, reverse):
    t_len, gn = xr.shape
    n_slab = t_len // SUBLANES
    for lb in range(gn // SCAN_LANES):
        ls = pl.ds(lb * SCAN_LANES, SCAN_LANES)

        def step(s, carry, ls=ls):
            cr, ci = carry
            slab = (n_slab - 1 - s) if reverse else s
            rows = pl.ds(pl.multiple_of(slab * SUBLANES, SUBLANES), SUBLANES)
            br, bi = xr[rows, ls], xi[rows, ls]
            for lvl, d in enumerate(SCAN_LEVELS):
                shift = SUBLANES - d if reverse else d
                sr, si = pltpu.roll(br, shift, 0), pltpu.roll(bi, shift, 0)
                mr, mi = m_ref[lvl, 0, :, ls], m_ref[lvl, 1, :, ls]
                br, bi = br + mr * sr - mi * si, bi + mr * si + mi * sr
            pr, pi = p_ref[0, :, ls], p_ref[1, :, ls]
            br, bi = br + pr * cr - pi * ci, bi + pr * ci + pi * cr
            xr[rows, ls] = br
            xi[rows, ls] = bi
            last = 0 if reverse else SUBLANES - 1
            return br[last:last + 1, :], bi[last:last + 1, :]

        cr, ci = lax.fori_loop(0, n_slab, step, (car[0:1, ls], car[1:2, ls]))
        car[0:1, ls] = cr
        car[1:2, ls] = ci


SSM_SPLIT = 2


def _bd_dot(a, w_ref):
    k, n = w_ref.shape[0] // SSM_SPLIT, w_ref.shape[1] // SSM_SPLIT
    return jnp.concatenate([jnp.dot(a[:, p * k:(p + 1) * k], w_ref[p * k:(p + 1) * k, p * n:(p + 1) * n],
                                    preferred_element_type=F32) for p in range(SSM_SPLIT)], axis=1)


def _bd_dot_nt(a, w_ref):
    k, n = w_ref.shape[0] // SSM_SPLIT, w_ref.shape[1] // SSM_SPLIT
    return jnp.concatenate([_dot_nt(a[:, p * n:(p + 1) * n], w_ref[p * k:(p + 1) * k, p * n:(p + 1) * n])
                            for p in range(SSM_SPLIT)], axis=1)


def _seq_block(step, n_chunk, lat_chunks, reverse):
    if reverse:
        return n_chunk - 1 - step
    return (step + lat_chunks) % n_chunk


def ssm_fwd(proj, lay, bre, bim, cre, ncim, tabs, n, t_len, reverse, name):
    l, sw = proj.shape[0], lay['sw']
    gn = bre.shape[-1]
    n_chunk, lat_chunks = l // t_len, n // t_len
    blk = lambda k: _seq_block(k, n_chunk, lat_chunks, reverse)
    mtab, ptab = tabs

    def body(u_ref, bre_ref, bim_ref, cre_ref, ncim_ref, m_ref, p_ref, y_ref, xb_ref, xr, xi, car):
        @pl.when(pl.program_id(0) == 0)
        def _():
            car[...] = jnp.zeros_like(car)

        xb_ref[...] = car[...]
        u = u_ref[...].astype(BF16)
        xr[...] = _bd_dot(u, bre_ref)
        xi[...] = _bd_dot(u, bim_ref)
        _scan_chunk(xr, xi, car, m_ref, p_ref, reverse)
        y_ref[...] = _bd_dot(xr[...].astype(BF16), cre_ref) + _bd_dot(xi[...].astype(BF16), ncim_ref)

    return pl.pallas_call(
        body, name=name, grid=(n_chunk,),
        in_specs=[pl.BlockSpec((t_len, sw), lambda k: (blk(k), lay['o_u'] // sw)), _full(bre.shape),
                  _full(bim.shape), _full(cre.shape), _full(ncim.shape), _full(mtab.shape), _full(ptab.shape)],
        out_specs=[pl.BlockSpec((t_len, sw), lambda k: (blk(k), 0)),
                   pl.BlockSpec((None, 2, gn), lambda k: (k, 0, 0))],
        out_shape=[jax.ShapeDtypeStruct((l, sw), F32), jax.ShapeDtypeStruct((n_chunk, 2, gn), F32)],
        scratch_shapes=[pltpu.VMEM((t_len, gn), F32), pltpu.VMEM((t_len, gn), F32), pltpu.VMEM((2, gn), F32)],
        compiler_params=_cparams(("arbitrary",)),
    )(proj, bre, bim, cre, ncim, mtab, ptab)


def ssm_bwd(proj, lay, dyr, xb, bre, bim, cre, ncim, tabs, adj_tabs, n, t_len, reverse, name):
    l, sw = proj.shape[0], lay['sw']
    gn = bre.shape[-1]
    n_chunk, lat_chunks = l // t_len, n // t_len
    fwd_step = lambda k: n_chunk - 1 - k
    blk = lambda k: _seq_block(fwd_step(k), n_chunk, lat_chunks, reverse)
    (mtab, ptab), (mtab_r, ptab_r) = tabs, adj_tabs

    def body(u_ref, dy_ref, xb_ref, bre_ref, bim_ref, cre_ref, ncim_ref, m_ref, p_ref, mr_ref, pr_ref,
             du_ref, gr_ref, gi_ref, xr_ref, xi_ref, dlam_ref, xr, xi, gr, gi, car, acar):
        k = pl.program_id(0)

        @pl.when(k == 0)
        def _():
            acar[...] = jnp.zeros_like(acar)
            dlam_ref[...] = jnp.zeros_like(dlam_ref)

        u = u_ref[...].astype(BF16)
        dy = jnp.where(blk(k) < lat_chunks, dy_ref[...], 0.0).astype(BF16)
        xr[...] = _bd_dot(u, bre_ref)
        xi[...] = _bd_dot(u, bim_ref)
        car[...] = xb_ref[...]
        _scan_chunk(xr, xi, car, m_ref, p_ref, reverse)
        gr[...] = _bd_dot_nt(dy, cre_ref)
        gi[...] = _bd_dot_nt(dy, ncim_ref)
        _scan_chunk(gr, gi, acar, mr_ref, pr_ref, not reverse)
        xrv, xiv, grv, giv = xr[...], xi[...], gr[...], gi[...]
        row = lax.broadcasted_iota(jnp.int32, (t_len, gn), 0)
        first, shift = (t_len - 1, t_len - 1) if reverse else (0, 1)
        xpr = jnp.where(row == first, xb_ref[0:1, :], pltpu.roll(xrv, shift, 0))
        xpi = jnp.where(row == first, xb_ref[1:2, :], pltpu.roll(xiv, shift, 0))
        dlr = grv * xpr + giv * xpi
        dli = giv * xpr - grv * xpi
        dlam_ref[0] += jnp.sum(dlr.reshape(t_len // SUBLANES, SUBLANES, gn), axis=0)
        dlam_ref[1] += jnp.sum(dli.reshape(t_len // SUBLANES, SUBLANES, gn), axis=0)
        grb, gib = grv.astype(BF16), giv.astype(BF16)
        du_ref[...] = _bd_dot_nt(grb, bre_ref) + _bd_dot_nt(gib, bim_ref)
        gr_ref[...] = grb
        gi_ref[...] = gib
        xr_ref[...] = xrv.astype(BF16)
        xi_ref[...] = xiv.astype(BF16)

    def at_blk(width, col=0):
        return pl.BlockSpec((t_len, width), lambda k: (blk(k), col))

    state = jax.ShapeDtypeStruct((l, gn), BF16)
    return pl.pallas_call(
        body, name=name, grid=(n_chunk,),
        in_specs=[at_blk(sw, lay['o_u'] // sw),
                  pl.BlockSpec((t_len, sw), lambda k: (jnp.minimum(blk(k), lat_chunks - 1), 0)),
                  pl.BlockSpec((None, 2, gn), lambda k: (fwd_step(k), 0, 0)),
                  _full(bre.shape), _full(bim.shape), _full(cre.shape), _full(ncim.shape), _full(mtab.shape),
                  _full(ptab.shape), _full(mtab_r.shape), _full(ptab_r.shape)],
        out_specs=[at_blk(sw), at_blk(gn), at_blk(gn), at_blk(gn), at_blk(gn), _full((2, SUBLANES, gn))],
        out_shape=[jax.ShapeDtypeStruct((l, sw), F32), state, state, state, state,
                   jax.ShapeDtypeStruct((2, SUBLANES, gn), F32)],
        scratch_shapes=[pltpu.VMEM((t_len, gn), F32)] * 4 + [pltpu.VMEM((2, gn), F32)] * 2,
        compiler_params=_cparams(("arbitrary",)),
    )(proj, dyr, xb, bre, bim, cre, ncim, mtab, ptab, mtab_r, ptab_r)


def ssm_prep(lam_re, lam_im, logdt, bre, bim, name):
    gn = lam_re.shape[1]

    def body(lr_ref, li_ref, dt_ref, br_ref, bi_ref, pwr_ref, pwi_ref, bbr_ref, bbi_ref):
        _, _, bbr, bbi = _ssm_prep_fn(lr_ref[...], li_ref[...], dt_ref[...], br_ref[...], bi_ref[...])
        bbr_ref[...] = bbr
        bbi_ref[...] = bbi
        kk = (lax.broadcasted_iota(jnp.int32, (SUBLANES, gn), 0) + 1).astype(F32)
        dt = jnp.exp(dt_ref[...])
        ar, ai = lr_ref[...] * dt * kk, li_ref[...] * dt * kk
        e = jnp.exp(ar)
        pwr_ref[...] = e * jnp.cos(ai)
        pwi_ref[...] = e * jnp.sin(ai)

    ins = (lam_re, lam_im, logdt, bre, bim)
    return pl.pallas_call(
        body, name=name,
        out_shape=[jax.ShapeDtypeStruct((SUBLANES, gn), F32)] * 2 + [jax.ShapeDtypeStruct(bre.shape, F32)] * 2,
        compiler_params=_cparams(),
    )(*ins)


def ssm_prep_bwd(lam_re, lam_im, logdt, bre, bim, dlbr, dlbi, dbbr, dbbi, name):
    def body(lr_ref, li_ref, dt_ref, br_ref, bi_ref, g0, g1, g2, g3, o0, o1, o2, o3, o4):
        _, vjp = jax.vjp(_ssm_prep_fn, lr_ref[...], li_ref[...], dt_ref[...], br_ref[...], bi_ref[...])
        for ref, val in zip((o0, o1, o2, o3, o4), vjp((g0[...], g1[...], g2[...], g3[...]))):
            ref[...] = val

    ins = (lam_re, lam_im, logdt, bre, bim)
    return pl.pallas_call(
        body, name=name,
        out_shape=[jax.ShapeDtypeStruct(a.shape, F32) for a in ins],
        compiler_params=_cparams(),
    )(*ins, dlbr, dlbi, dbbr, dbbi)


def mod_fwd(cmat, w, b, name):
    def body(c_ref, w_ref, b_ref, o_ref):
        o_ref[...] = _mod_fn(c_ref[...], w_ref[...]) + b_ref[...]

    return pl.pallas_call(body, name=name, out_shape=jax.ShapeDtypeStruct((cmat.shape[0], w.shape[1]), F32),
                          compiler_params=_cparams())(cmat, w, b)


def mod_bwd(cmat, w, g_lat, g_ctx, g_all, name):
    def body(c_ref, w_ref, gl_ref, gc_ref, ga_ref, dw_ref, dc_ref, db_ref):
        gc = jnp.sum(gc_ref[...], axis=0, keepdims=True)
        dm = jnp.concatenate([gl_ref[...], _pad_rows(gc)], axis=0)
        _, vjp = jax.vjp(_mod_fn, c_ref[...], w_ref[...])
        dc, dw = vjp(dm)
        dw_ref[...] = dw
        dc_ref[...] = dc
        db_ref[...] = _pad_rows(jnp.sum(ga_ref[...], axis=0, keepdims=True))

    return pl.pallas_call(
        body, name=name,
        out_shape=[jax.ShapeDtypeStruct(w.shape, F32), jax.ShapeDtypeStruct(cmat.shape, F32),
                   jax.ShapeDtypeStruct((SUBLANES, g_all.shape[1]), F32)],
        compiler_params=_cparams(),
    )(cmat, w, g_lat, g_ctx, g_all)


def add_own(own, idx, recv, out_dtype, name):
    _, s, r, c = own.shape

    def body(idx_ref, own_ref, recv_ref, o_ref):
        o_ref[...] = (own_ref[...] + recv_ref[...]).astype(o_ref.dtype)

    return pl.pallas_call(
        body, name=name,
        grid_spec=pltpu.PrefetchScalarGridSpec(
            num_scalar_prefetch=1, grid=(s,),
            in_specs=[pl.BlockSpec((None, None, r, c), lambda k, idx_ref: (idx_ref[0], k, 0, 0)),
                      pl.BlockSpec((None, r, c), lambda k, idx_ref: (k, 0, 0))],
            out_specs=pl.BlockSpec((None, r, c), lambda k, idx_ref: (k, 0, 0))),
        out_shape=jax.ShapeDtypeStruct((s, r, c), out_dtype),
        compiler_params=_cparams(("parallel",)),
    )(idx, own, recv)


def reduce_adamw(parts, w, m, v, name):
    s, r, c = parts.shape
    tr = _tile(r, (256, 128, 64, 32, 16, 8))

    def body(p_ref, w_ref, m_ref, v_ref, g_ref, d_ref, nm_ref, nv_ref):
        g = p_ref[0].astype(F32)
        for k in range(1, s):
            g = g + p_ref[k].astype(F32)
        mm = ADAM_B1 * m_ref[...] + (1.0 - ADAM_B1) * g
        vv = ADAM_B2 * v_ref[...] + (1.0 - ADAM_B2) * jnp.square(g)
        m_hat = mm / (1.0 - ADAM_B1 ** ADAM_STEP)
        v_hat = vv / (1.0 - ADAM_B2 ** ADAM_STEP)
        g_ref[...] = g
        d_ref[...] = -ADAM_LR * (m_hat / (jnp.sqrt(v_hat) + ADAM_EPS) + ADAM_WD * w_ref[...])
        nm_ref[...] = mm
        nv_ref[...] = vv

    out = jax.ShapeDtypeStruct((r, c), F32)
    blk = _rows(tr, c)
    return pl.pallas_call(
        body, name=name, grid=(r // tr,),
        in_specs=[pl.BlockSpec((s, tr, c), lambda i: (0, i, 0)), blk, blk, blk],
        out_specs=[blk] * 4, out_shape=[out] * 4,
        compiler_params=_cparams(("parallel",)),
    )(parts, w, m, v)


def _in_layout(d, q, kv, sw):
    o_u = 2 * d
    o_ckv = o_u + sw
    o_kr = o_ckv + kv
    o_cq = -(-(o_kr + LANES) // q) * q
    assert o_u % sw == 0 and o_ckv % kv == 0 and o_kr % LANES == 0
    assert q % LANES == 0 and kv % LANES == 0 and sw % LANES == 0
    return dict(d=d, q=q, kv=kv, sw=sw, o_gl=0, o_u=o_u, o_ckv=o_ckv, o_kr=o_kr, o_cq=o_cq, width=o_cq + q)


def _pad_w_in(w_in, lay):
    q, kv, sw, d = lay['q'], lay['kv'], lay['sw'], lay['d']
    cq, ckv, kr, u, gl = jnp.split(w_in, [q, q + kv, q + kv + QK_ROPE, q + kv + QK_ROPE + sw], axis=1)
    z = lambda w: jnp.zeros((w_in.shape[0], w), w_in.dtype)
    hole = lay['o_cq'] - lay['o_kr'] - LANES
    return jnp.concatenate([gl, u, ckv, z(QK_NOPE), kr, z(LANES - QK_DIM), z(hole), cq], axis=1)


def _unpad_w_in(g, lay):
    q, kv, sw, d = lay['q'], lay['kv'], lay['sw'], lay['d']
    kr0 = lay['o_kr'] + QK_NOPE
    return jnp.concatenate([g[:, lay['o_cq']:lay['o_cq'] + q], g[:, lay['o_ckv']:lay['o_ckv'] + kv],
                            g[:, kr0:kr0 + QK_ROPE], g[:, lay['o_u']:lay['o_u'] + sw], g[:, :2 * d]], axis=1)


def _pad_heads(w, width):
    k = w.shape[0]
    return jnp.pad(w.reshape(k, N_HEADS, width), ((0, 0), (0, 0), (0, HEAD_PAD - width))).reshape(k, HEADS_W)


def _unpad_heads(w, width):
    k = w.shape[0]
    return w.reshape(k, N_HEADS, HEAD_PAD)[:, :, :width].reshape(k, N_HEADS * width)


def _rope_tables(n, nc):
    rows = n // GRID_W
    row = jnp.repeat(jnp.arange(rows), GRID_W)
    col = jnp.tile(jnp.arange(GRID_W), rows)
    pairs = QK_ROPE // 4
    freqs = ROPE_THETA ** (-jnp.arange(pairs, dtype=F32) / pairs)
    ang = jnp.concatenate([row[:, None] * freqs, col[:, None] * freqs], axis=-1)
    cos = jnp.concatenate([jnp.cos(ang), jnp.ones((nc, 2 * pairs), F32)], axis=0)
    sin = jnp.concatenate([jnp.sin(ang), jnp.zeros((nc, 2 * pairs), F32)], axis=0)
    l = n + nc
    half = QK_ROPE // 2
    ct = jnp.concatenate([jnp.ones((l, QK_NOPE), F32), cos, cos, jnp.zeros((l, HEAD_PAD - QK_DIM), F32)], axis=1)
    s1 = jnp.concatenate([jnp.zeros((l, QK_NOPE + half), F32), sin, jnp.zeros((l, HEAD_PAD - QK_DIM), F32)],
                         axis=1)
    s2 = jnp.concatenate([jnp.zeros((l, QK_NOPE), F32), -sin, jnp.zeros((l, HEAD_PAD - QK_NOPE - half), F32)],
                         axis=1)
    return ct, s1, s2


def _group_mask(rows, cols, g):
    return (jnp.arange(rows)[:, None] // (rows // g)) == (jnp.arange(cols)[None, :] // (cols // g))


def _block_diag_rows(m, g):
    return jnp.where(_group_mask(g * m.shape[0], m.shape[1], g), jnp.tile(m, (g, 1)), 0)


def _block_diag_cols(m, g):
    return jnp.where(_group_mask(m.shape[0], g * m.shape[1], g), jnp.tile(m, (1, g)), 0)


def _diag_blocks(m, g):
    a, b = m.shape[0] // g, m.shape[1] // g
    masked = jnp.where(_group_mask(m.shape[0], m.shape[1], g), m, 0)
    return masked.reshape(m.shape[0], g, b).sum(axis=1).reshape(g, a, b)


def _scan_tables(pwr, pwi, reverse):
    row = jnp.arange(SUBLANES)[:, None]
    zero = jnp.zeros_like(pwr)
    levels = []
    for d in SCAN_LEVELS:
        keep = (row < SUBLANES - d) if reverse else (row >= d)
        levels.append(jnp.stack([jnp.where(keep, pwr[d - 1:d, :], zero), jnp.where(keep, pwi[d - 1:d, :], zero)]))
    carry = jnp.stack([pwr[::-1], pwi[::-1]]) if reverse else jnp.stack([pwr, pwi])
    return jnp.stack(levels), carry


def _step(inp):
    x, c, ctx = inp['x'][0], inp['c'], inp['ctx'][0]
    target = inp['loss_target'][0]
    n, d = x.shape
    nc = ctx.shape[0]
    l = n + nc
    q_w, kv_w = inp['q_a_g'].shape[1], inp['kv_a_g'].shape[1]
    sw = inp['d_skip'].shape[1]
    n_grp = sw // SSM_GROUP
    gn = n_grp * SSM_STATE
    lay = _in_layout(d, q_w, kv_w, sw)
    tm = _tile(math.gcd(n, nc), (256, 128))
    me = 4 * lax.axis_index("x") + 2 * lax.axis_index("y") + lax.axis_index("c")
    strip = lambda a: a if a.ndim <= 2 else a[0]
    w = {k: strip(inp[k]) for k in WEIGHT_NAMES}

    gathered_names = list(GATHERED)
    early_names = ['w_in', 'w_uq', 'w_ukv', 'w_o_attn']
    late_names = [k for k in gathered_names if k not in early_names]
    conv_b = w['conv_b']

    (c_all,) = exchange([jnp.broadcast_to(c, (SUBLANES, d))], ['gather'], "gather_c")
    cmat = jnp.concatenate([c_all[:, 0, :], w['c_ctx'][None, :], jnp.zeros((SUBLANES - 1, d), F32)], axis=0)
    mcols = w['w_mod'].shape[1]
    b_cols = lax.dynamic_slice(w['b_mod'], (0, me * mcols), (1, mcols))
    mod_part = mod_fwd(cmat, w['w_mod'], b_cols, "mod_fwd")
    (mod_all,) = exchange([mod_part], ['gather'], "gather_mod")
    mod_me = lax.dynamic_index_in_dim(mod_all, me, axis=1, keepdims=False).reshape(6, d)
    mod_ctx = mod_all[:, SUBLANES, :].reshape(6, d)
    sh1, sc1, g1, sh2, sc2, g2 = [mod_me[k:k + 1] for k in range(6)]
    mods1 = jnp.concatenate([sc1, sh1, mod_ctx[1:2], mod_ctx[0:1], jnp.zeros((4, d), F32)], axis=0)
    mods2 = jnp.concatenate([sc2, sh2, jnp.zeros((6, d), F32)], axis=0)

    xa = jnp.concatenate([x, ctx], axis=0)
    h, *wg = norm_mod_fwd(xa, w['norm1_g'], mods1, n, tm, "norm1_fwd", gather=[w[k].astype(BF16) for k in early_names])
    full = {k: _from_shards(s, GATHERED[k]) for k, s in zip(early_names, wg)}
    w_in_p = _pad_w_in(full['w_in'], lay)
    wuq_p = _pad_heads(full['w_uq'], QK_DIM)
    ukv = full['w_ukv'].reshape(kv_w, N_HEADS, QK_NOPE + V_DIM)
    wk_p = _pad_heads(ukv[:, :, :QK_NOPE].reshape(kv_w, -1), QK_NOPE)
    wv_p = _pad_heads(ukv[:, :, QK_NOPE:].reshape(kv_w, -1), V_DIM)
    wo_p = _pad_heads(full['w_o_attn'].T, V_DIM).T
    proj = matmul(h, w_in_p, 'nn', F32, "in_proj")
    tabs = _rope_tables(n, nc)
    pad_g = lambda g: jnp.pad(g, ((0, 0), (0, HEAD_PAD - QK_DIM)))
    gains = (w['q_a_g'], w['kv_a_g'], pad_g(w['q_norm_g']), pad_g(w['k_norm_g']))
    q, k, v = qkv_fwd(proj, lay, gains, wuq_p, wk_p, wv_p, tabs, tm, "qkv_fwd")
    o, lse, *wg = attn_fwd(q, k, v, n, "attn_fwd", gather=[w[k].astype(BF16) for k in late_names])
    full.update({k: _from_shards(s, GATHERED[k]) for k, s in zip(late_names, wg)})
    w_glu, w_out, w_up, w_down = full['w_glu'], full['w_out'], full['w_up'], full['w_down']
    conv_w = jnp.pad(full['conv_w'].astype(F32), ((0, SUBLANES - 3), (0, 0)))

    b_t = lambda a: a.transpose(2, 0, 1).reshape(SSM_GROUP, gn)
    c_t = lambda a: a.transpose(1, 0, 2).reshape(SSM_GROUP, gn)
    bre_t, bim_t = b_t(w['b_re']), b_t(w['b_im'])
    ssm_in, prep = [], []
    for sfx in ('f', 'b'):
        lam_re, lam_im = w['lam_re_' + sfx].reshape(1, gn), w['lam_im_' + sfx].reshape(1, gn)
        logdt = jnp.repeat(w['log_dt_' + sfx], SSM_STATE, axis=1)
        ssm_in.append((lam_re, lam_im, logdt))
        prep.append(ssm_prep(lam_re, lam_im, logdt, bre_t, bim_t, "ssm_prep_" + sfx))

    def blk_b(bb):
        return _block_diag_rows(bb, n_grp).astype(BF16)

    def blk_c(cc):
        return _block_diag_cols(cc.transpose(0, 2, 1).reshape(gn, SSM_GROUP), n_grp).astype(BF16)

    t_len = _tile(math.gcd(n, nc), (256, 128))
    ssm = []
    for di, sfx in enumerate(('f', 'b')):
        reverse = di == 1
        pwr, pwi, bbr, bbi = prep[di]
        ssm.append(dict(
            sfx=sfx, reverse=reverse, blocks=(blk_b(bbr), blk_b(bbi), blk_c(w['c_re_' + sfx]),
                                              blk_c(-w['c_im_' + sfx])),
            tabs=_scan_tables(pwr, pwi, reverse), adj_tabs=_scan_tables(pwr, -pwi, not reverse)))
    for s in ssm:
        s['y'], s['xb'] = ssm_fwd(proj, lay, *s['blocks'], s['tabs'], n, t_len, s['reverse'], "ssm_fwd_" + s['sfx'])
    yf, yb = ssm[0]['y'], ssm[1]['y']
    s_l = glu_fwd(proj, lay, yf, yb, w['d_skip'], w_glu, n, tm, "glu_fwd")
    x1 = merge_fwd(o, s_l, proj, xa, g1, wo_p, w_out, n, tm, "merge_fwd")

    h2 = norm_mod_fwd(x1, w['norm2_g'], mods2, n, tm, "norm2_fwd")
    up = matmul(h2, w_up, 'nn', F32, "up_proj")
    tw = _tile(n, (256, 128))
    act = conv_act_fwd(up, conv_w, conv_b, tw, "conv_act_fwd")
    dy, ffn, loss_parts = down_loss(act, w_down, x1, g2, target, tm, "down_loss")
    loss = lax.psum(jnp.sum(loss_parts[:, 0, 0]), MESH_AXES)

    dup, dffn, dg2, dconv = ffn_bwd(dy, ffn, up, conv_w, conv_b, w_down, g2, tm, "ffn_bwd")
    g_w_down = matmul(act, dffn, 'tn', F32, "dw_down")
    dh2 = matmul(dup, w_up, 'nt', F32, "dh2")
    g_w_up = matmul(h2, dup, 'tn', F32, "dw_up")
    dx1, dn2g, dmods2 = norm_mod_bwd(x1, dh2, dy, w['norm2_g'], mods2, n, tm, "norm2_bwd")

    do, ds_l, dgl, dg1, g_wo_p, g_w_out = merge_bwd(o, s_l, proj, xa, g1, wo_p, w_out, dx1, n, tm, "merge_bwd")
    du_direct, dyr, dds, g_w_glu = glu_bwd(proj, lay, yf, yb, w['d_skip'], w_glu, ds_l, n, tm, "glu_bwd")
    for s in ssm:
        s['du'], s['gr'], s['gi'], s['xr'], s['xi'], s['dlam'] = ssm_bwd(
            proj, lay, dyr, s['xb'], *s['blocks'], s['tabs'], s['adj_tabs'], n, t_len, s['reverse'],
            "ssm_bwd_" + s['sfx'])

    early_g = {'w_o_attn': _unpad_heads(g_wo_p.T, V_DIM).T, 'w_glu': g_w_glu, 'w_out': g_w_out, 'w_up': g_w_up,
               'conv_w': dconv[0:3], 'w_down': g_w_down}
    dq, dk, dv, *early_recv = attn_bwd(q, k, v, o, do, lse, n, "attn_bwd",
                                       a2a=[_to_shards(g, GATHERED[k]).astype(BF16) for k, g in early_g.items()])
    (dproj, dqag, dkvag, dqng, dkng, g_wuq_p, g_wk_p, g_wv_p) = qkv_bwd(
        proj, lay, gains, wuq_p, wk_p, wv_p, tabs, dq, dk, dv, dgl, du_direct, ssm[0]['du'], ssm[1]['du'], n, tm,
        "qkv_bwd")
    g_w_in_p = matmul(h, dproj, 'tn', F32, "dw_in")
    ukv_g = jnp.concatenate([_unpad_heads(g_wk_p, QK_NOPE).reshape(kv_w, N_HEADS, QK_NOPE),
                             _unpad_heads(g_wv_p, V_DIM).reshape(kv_w, N_HEADS, V_DIM)], axis=2)
    late_g = {'w_in': _unpad_w_in(g_w_in_p, lay), 'w_uq': _unpad_heads(g_wuq_p, QK_DIM),
              'w_ukv': ukv_g.reshape(kv_w, -1)}
    dh, *late_recv = matmul(dproj, w_in_p, 'nt', F32, "dh",
                            a2a=[_to_shards(g, GATHERED[k]).astype(BF16) for k, g in late_g.items()])
    dxa, dn1g, dmods1 = norm_mod_bwd(xa, dh, dx1, w['norm1_g'], mods1, n, tm, "norm1_bwd")
    grad_x = dxa[:n]

    grads = {}
    d_bbar = [None, None]
    gp = n_grp // SSM_SPLIT
    for di, s in enumerate(ssm):
        sfx = s['sfx']
        products = (diag_outer(proj, lay['o_u'], sw, [s['gr'], s['gi']], SSM_SPLIT, "ssm_db_" + sfx)
                    + diag_outer(dyr, 0, sw, [s['xr'], s['xi']], SSM_SPLIT, "ssm_dc_" + sfx))
        g_bre, g_bim, g_cre, g_cim = [
            jnp.concatenate([_diag_blocks(m[p], gp) for p in range(SSM_SPLIT)], axis=0) for m in products]
        grads['c_re_' + sfx] = g_cre
        grads['c_im_' + sfx] = -g_cim
        to_t = lambda a: a.transpose(1, 0, 2).reshape(SSM_GROUP, gn)
        dlam_re = jnp.sum(s['dlam'][0], axis=0, keepdims=True)
        dlam_im = jnp.sum(s['dlam'][1], axis=0, keepdims=True)
        lam_re, lam_im, logdt = ssm_in[di]
        g_lr, g_li, g_dt, g_br, g_bi = ssm_prep_bwd(lam_re, lam_im, logdt, bre_t, bim_t, dlam_re, dlam_im,
                                                    to_t(g_bre), to_t(g_bim), "ssm_prep_bwd_" + sfx)
        grads['lam_re_' + sfx] = g_lr.reshape(n_grp, SSM_STATE)
        grads['lam_im_' + sfx] = g_li.reshape(n_grp, SSM_STATE)
        grads['log_dt_' + sfx] = jnp.sum(g_dt.reshape(n_grp, SSM_STATE), axis=1)[None, :]
        d_bbar[di] = (g_br, g_bi)
    from_t = lambda a: a.reshape(SSM_GROUP, n_grp, SSM_STATE).transpose(1, 2, 0)
    grads['b_re'] = from_t(d_bbar[0][0]) + from_t(d_bbar[1][0])
    grads['b_im'] = from_t(d_bbar[0][1]) + from_t(d_bbar[1][1])

    dmod = jnp.concatenate([dmods1[1:2], dmods1[0:1], dg1[0:1], dmods2[1:2], dmods2[0:1], dg2[0:1]], axis=1)
    dmod_ctx = jnp.concatenate([dmods1[3:4], dmods1[2:3], jnp.zeros((1, 4 * d), F32)], axis=1)
    dm_send = jnp.concatenate([dmod, dmod_ctx, jnp.zeros((SUBLANES - 2, 6 * d), F32)], axis=0)
    (dm_all,) = exchange([dm_send], ['gather'], "gather_dmod")
    g_all = jnp.concatenate([dm_all[:, 0, :], dm_all[:, 1, :]], axis=0)
    cols = lax.dynamic_slice(g_all.reshape(2 * N_DEV, N_DEV, mcols), (0, me, 0), (2 * N_DEV, 1, mcols))[:, 0, :]
    g_w_mod, dcmat, g_b_mod = mod_bwd(cmat, w['w_mod'], cols[:N_DEV], cols[N_DEV:], g_all, "mod_bwd")

    grads.update({'c_ctx': dcmat[SUBLANES], 'b_mod': g_b_mod[0:1], 'norm1_g': dn1g[0:1], 'norm2_g': dn2g[0:1],
                  'q_a_g': dqag[0:1], 'kv_a_g': dkvag[0:1], 'q_norm_g': dqng[0:1, :QK_DIM],
                  'k_norm_g': dkng[0:1, :QK_DIM], 'd_skip': dds[0:1], 'conv_b': dconv[3:4]})
    first = (me == 0).astype(F32)
    rep_parts = [grads[k] * first if k == 'b_mod' else grads[k] for k in REPLICATED]
    rpack, rspans = _pack(rep_parts, SUBLANES)
    my_core = lax.axis_index("c").astype(jnp.int32).reshape(1)
    r_both = jnp.broadcast_to(rpack[None, None], (2, 1) + rpack.shape)
    (from_sibling,) = exchange([r_both], ['others'], "grads_in_chip", group='core')
    r_sum = add_own(r_both, my_core, from_sibling[0], F32, "chip_sum_replicated")[0]
    (r_recv,) = exchange([r_sum], ['gather'], "grads_between_chips", group='chips')

    outs = {}

    def update(parts, k, w_k, m_k, v_k):
        res = reduce_adamw(parts, w_k, m_k, v_k, "adamw_" + k)
        return dict(zip(('grad_', 'delta_', 'new_m_', 'new_v_'), res))

    per_tensor = list(zip(early_g, early_recv)) + list(zip(late_g, late_recv)) + [('w_mod', g_w_mod[None])]
    for k, parts in per_tensor:
        for kind, a in update(parts, k, w[k], strip(inp['m_' + k]), strip(inp['v_' + k])).items():
            outs[kind + k] = a[None]
    rep = lambda prefix: _pack([strip(inp[prefix + k]) for k in REPLICATED], SUBLANES)[0]
    for kind, buf in update(r_recv, "replicated", rep(''), rep('m_'), rep('v_')).items():
        for k, a in zip(REPLICATED, _unpack(buf, rspans, [w[k].shape for k in REPLICATED])):
            outs[kind + k] = a if inp[k].ndim <= 2 else a[None]
    result = [loss, grad_x[None]]
    for kind in ('grad_', 'delta_', 'new_m_', 'new_v_'):
        result += [outs[kind + k] for k in WEIGHT_NAMES]
    return tuple(result)


_ARG_NAMES = (['x', 'c', 'ctx'] + WEIGHT_NAMES + ['loss_target'] + ['m_' + k for k in WEIGHT_NAMES]
              + ['v_' + k for k in WEIGHT_NAMES])


def kernel(*args):
    assert len(args) == len(_ARG_NAMES)
    return _step(dict(zip(_ARG_NAMES, args)))
```

```python
import functools
import math

import jax
import jax.numpy as jnp
from jax import lax
from jax.experimental import pallas as pl
from jax.experimental.pallas import tpu as pltpu

F32 = jnp.float32
BF16 = jnp.bfloat16

N_DEV = 8
MESH_AXES = ("x", "y", "c")
N_HEADS = 8
QK_NOPE = 64
QK_ROPE = 32
QK_DIM = QK_NOPE + QK_ROPE
V_DIM = 64
HEAD_PAD = 128
HEADS_W = N_HEADS * HEAD_PAD
GRID_W = 64
ROPE_THETA = 10000.0
SSM_GROUP = 16
SSM_STATE = 64
EPS = 1e-6
LANES = 128
SUBLANES = 8
PACK_W = 1024
VMEM_LIMIT = 56 * 1024 * 1024
MM_TILES = (1024, 768, 1408, 512, 384, 256, 128)

ADAM_LR = 0.001
ADAM_B1 = 0.9
ADAM_B2 = 0.999
ADAM_EPS = 1e-08
ADAM_WD = 0.01
ADAM_STEP = 10

WEIGHT_NAMES = ['c_ctx', 'w_mod', 'b_mod', 'norm1_g', 'norm2_g', 'w_in', 'q_a_g', 'w_uq', 'kv_a_g', 'w_ukv',
                'q_norm_g', 'k_norm_g', 'w_o_attn', 'lam_re_f', 'lam_im_f', 'log_dt_f', 'c_re_f', 'c_im_f',
                'lam_re_b', 'lam_im_b', 'log_dt_b', 'c_re_b', 'c_im_b', 'b_re', 'b_im', 'd_skip', 'w_glu',
                'w_out', 'w_up', 'conv_w', 'conv_b', 'w_down']
GATHERED = {'w_in': 1, 'w_uq': 1, 'w_ukv': 1, 'w_o_attn': 1, 'w_glu': 1, 'w_out': 0, 'w_up': 1, 'conv_w': 1,
            'w_down': 0}
REPLICATED = [n for n in WEIGHT_NAMES if n not in GATHERED and n != 'w_mod']


def _tile(n, prefs):
    for t in prefs:
        if n % t == 0:
            return t
    return n


def _cparams(sem=None):
    return pltpu.CompilerParams(dimension_semantics=sem, vmem_limit_bytes=VMEM_LIMIT)


@jax.custom_vjp
def bdot(a, w):
    return jnp.dot(a.astype(BF16), w.astype(BF16), preferred_element_type=F32)


def _bdot_fwd(a, w):
    return bdot(a, w), (a, w)


def _bdot_bwd(res, g):
    a, w = res
    gb = g.astype(BF16)
    da = lax.dot_general(gb, w.astype(BF16), (((1,), (1,)), ((), ())), preferred_element_type=F32)
    dw = lax.dot_general(a.astype(BF16), gb, (((0,), (0,)), ((), ())), preferred_element_type=F32)
    return da.astype(a.dtype), dw.astype(w.dtype)


bdot.defvjp(_bdot_fwd, _bdot_bwd)


def _dot_nt(a, b):
    return lax.dot_general(a, b, (((1,), (1,)), ((), ())), preferred_element_type=F32)


def _dot_tn(a, b):
    return lax.dot_general(a, b, (((0,), (0,)), ((), ())), preferred_element_type=F32)


def matmul(a, b, mode, out_dtype, name, a2a=()):
    n_x = len(a2a)
    kinds = ['a2a'] * n_x
    if mode == 'nn':
        (m, k), n = a.shape, b.shape[1]
    elif mode == 'nt':
        (m, k), n = a.shape, b.shape[0]
    else:
        (k, m), n = a.shape, b.shape[1]
    tm = _tile(m, MM_TILES)
    tn = _tile(n, MM_TILES)
    tk = _tile(k, MM_TILES)
    nk = k // tk

    def body(a_ref, b_ref, *rest):
        o_ref, acc_ref = rest[n_x], rest[2 * n_x + 1]
        kk = pl.program_id(2)
        if n_x:
            start, finish = _exchange_ops(rest[:n_x], rest[n_x + 1:2 * n_x + 1], *rest[2 * n_x + 2:], kinds, 'all')
            step = (pl.program_id(0) * (n // tn) + pl.program_id(1)) * nk + kk
            pl.when(step == 0)(start)

        @pl.when(kk == 0)
        def _():
            acc_ref[...] = jnp.zeros_like(acc_ref)

        av, bv = a_ref[...].astype(BF16), b_ref[...].astype(BF16)
        if mode == 'nn':
            acc_ref[...] += jnp.dot(av, bv, preferred_element_type=F32)
        elif mode == 'nt':
            acc_ref[...] += _dot_nt(av, bv)
        else:
            acc_ref[...] += _dot_tn(av, bv)

        @pl.when(kk == nk - 1)
        def _():
            o_ref[...] = acc_ref[...].astype(o_ref.dtype)

        if n_x:
            pl.when(step == (m // tm) * (n // tn) * nk - 1)(finish)

    if mode == 'nn':
        a_spec = pl.BlockSpec((tm, tk), lambda i, j, kk: (i, kk))
        b_spec = pl.BlockSpec((tk, tn), lambda i, j, kk: (kk, j))
    elif mode == 'nt':
        a_spec = pl.BlockSpec((tm, tk), lambda i, j, kk: (i, kk))
        b_spec = pl.BlockSpec((tn, tk), lambda i, j, kk: (j, kk))
    else:
        a_spec = pl.BlockSpec((tk, tm), lambda i, j, kk: (kk, i))
        b_spec = pl.BlockSpec((tk, tn), lambda i, j, kk: (kk, j))
    any_spec = pl.BlockSpec(memory_space=pl.ANY)
    res = pl.pallas_call(
        body, name=name, grid=(m // tm, n // tn, nk),
        in_specs=[a_spec, b_spec] + [any_spec] * n_x,
        out_specs=[pl.BlockSpec((tm, tn), lambda i, j, kk: (i, j))] + [any_spec] * n_x,
        out_shape=[jax.ShapeDtypeStruct((m, n), out_dtype)] + (_exchange_shapes(a2a, kinds, 'all') if n_x else []),
        scratch_shapes=[pltpu.VMEM((tm, tn), F32)] + (_exchange_scratch(n_x, 'all') if n_x else []),
        compiler_params=_cparams(("arbitrary",) * 3 if n_x else ("parallel", "parallel", "arbitrary")),
    )(a, b, *a2a)
    return res if n_x else res[0]


def diag_outer(a, a_col0, a_cols, bs, parts, name):
    k = a.shape[0]
    ka, kb = a_cols // parts, bs[0].shape[1] // parts
    tk = _tile(k, MM_TILES)
    nk = k // tk
    n_b = len(bs)
    assert a_col0 % ka == 0

    def body(a_ref, *rest):
        b_refs, o_refs, acc_refs = rest[:n_b], rest[n_b:2 * n_b], rest[2 * n_b:]
        kk = pl.program_id(1)
        av = a_ref[...].astype(BF16)
        for b_ref, o_ref, acc_ref in zip(b_refs, o_refs, acc_refs):
            @pl.when(kk == 0)
            def _():
                acc_ref[...] = jnp.zeros_like(acc_ref)

            acc_ref[...] += _dot_tn(av, b_ref[...].astype(BF16))

            @pl.when(kk == nk - 1)
            def _():
                o_ref[...] = acc_ref[...]

    return pl.pallas_call(
        body, name=name, grid=(parts, nk),
        in_specs=[pl.BlockSpec((tk, ka), lambda p, kk: (kk, a_col0 // ka + p))]
        + [pl.BlockSpec((tk, kb), lambda p, kk: (kk, p))] * n_b,
        out_specs=[pl.BlockSpec((None, ka, kb), lambda p, kk: (p, 0, 0))] * n_b,
        out_shape=[jax.ShapeDtypeStruct((parts, ka, kb), F32)] * n_b,
        scratch_shapes=[pltpu.VMEM((ka, kb), F32)] * n_b,
        compiler_params=_cparams(("parallel", "arbitrary")),
    )(a, *bs)


N_CHIPS = 4


def _group(group):
    x, y, c = lax.axis_index("x"), lax.axis_index("y"), lax.axis_index("c")
    flips = {'all': [(r & 4, r & 2, r & 1) for r in range(1, 8)],
             'chips': [(0, 1, 0), (1, 0, 0), (1, 1, 0)], 'core': [(0, 0, 1)]}[group]
    index = {'all': lambda px, py, pc: 4 * px + 2 * py + pc, 'chips': lambda px, py, pc: 2 * px + py,
             'core': lambda px, py, pc: pc}[group]
    peers = []
    for fx, fy, fc in flips:
        p = (1 - x if fx else x, 1 - y if fy else y, 1 - c if fc else c)
        peers.append((p, index(*p)))
    return len(flips) + 1, index(x, y, c), peers


def exchange(arrays, kinds, name, group='all'):
    n_arr = len(arrays)

    def body(*refs):
        start, finish = _exchange_ops(refs[:n_arr], refs[n_arr:2 * n_arr], *refs[2 * n_arr:], kinds, group,
                                      own_slot=False)
        start()
        finish()

    any_spec = pl.BlockSpec(memory_space=pl.ANY)
    outs = pl.pallas_call(
        body, name=name,
        in_specs=[any_spec] * n_arr, out_specs=[any_spec] * n_arr, out_shape=_exchange_shapes(arrays, kinds, group),
        scratch_shapes=_exchange_scratch(n_arr, group),
        compiler_params=pltpu.CompilerParams(has_side_effects=True),
    )(*arrays)
    _, me, _ = _group(group)
    own = {'gather': lambda arr: arr, 'a2a': lambda arr: lax.dynamic_index_in_dim(arr, me, 0, keepdims=False)}
    return [out if kind == 'others' else lax.dynamic_update_index_in_dim(out, own[kind](arr), me, 0)
            for arr, kind, out in zip(arrays, kinds, outs)]


GROUP_SIZE = {'all': N_DEV, 'chips': N_CHIPS, 'core': 2}


def _exchange_shapes(arrays, kinds, group):
    size = GROUP_SIZE[group]
    return [jax.ShapeDtypeStruct({'gather': (size,) + tuple(arr.shape), 'a2a': tuple(arr.shape),
                                  'others': (size - 1,) + tuple(arr.shape[1:])}[kind], arr.dtype)
            for arr, kind in zip(arrays, kinds)]


def _exchange_scratch(n_arr, group):
    size = GROUP_SIZE[group]
    return [pltpu.SemaphoreType.DMA((n_arr * size,)), pltpu.SemaphoreType.DMA((n_arr * size,)),
            pltpu.SemaphoreType.DMA((n_arr,))]


def _exchange_ops(srcs, dsts, send_sems, recv_sems, local_sems, kinds, group, own_slot=True):
    n_arr = len(srcs)
    size, me, peers = _group(group)
    has_local = [own_slot and kind != 'others' for kind in kinds]

    def copy(a, r, peer, peer_idx, receiving):
        src = srcs[a] if kinds[a] == 'gather' else srcs[a].at[peer_idx]
        if kinds[a] == 'others':
            dst = dsts[a].at[r]
        else:
            dst = dsts[a].at[peer_idx if receiving else me]
        return pltpu.make_async_remote_copy(
            src_ref=src, dst_ref=dst, send_sem=send_sems.at[a * size + r], recv_sem=recv_sems.at[a * size + r],
            device_id=peer, device_id_type=pl.DeviceIdType.MESH)

    def local(a):
        mine = srcs[a] if kinds[a] == 'gather' else srcs[a].at[me]
        return pltpu.make_async_copy(mine, dsts[a].at[me], local_sems.at[a])

    def start():
        for a in range(n_arr):
            if has_local[a]:
                local(a).start()
            for r, (peer, peer_idx) in enumerate(peers):
                copy(a, r, peer, peer_idx, False).start()

    def finish():
        for a in range(n_arr):
            for r, (peer, peer_idx) in enumerate(peers):
                cp = copy(a, r, peer, peer_idx, True)
                cp.wait_send()
                cp.wait_recv()
            if has_local[a]:
                local(a).wait()

    return start, finish


GATHER_COPIES = 7


def _two_level_gather(srcs, outs, send_sems, recv_sems, local_sems):
    n_arr = len(srcs)
    x, y, c = lax.axis_index("x"), lax.axis_index("y"), lax.axis_index("c")
    me, sibling = (x, y, c), (x, y, 1 - c)
    chips = [(1 - x, y), (x, 1 - y), (1 - x, 1 - y)]
    slot = lambda p: 4 * p[0] + 2 * p[1] + p[2]

    def copy(a, k, block, to, own=False):
        dst = outs[a].at[slot(block)]
        return pltpu.make_async_remote_copy(
            src_ref=srcs[a] if own else dst, dst_ref=dst,
            send_sem=send_sems.at[a * GATHER_COPIES + k], recv_sem=recv_sems.at[a * GATHER_COPIES + k],
            device_id=to, device_id_type=pl.DeviceIdType.MESH)

    def own_copies(a):
        return [copy(a, 0, me, sibling, own=True)] + [copy(a, 1 + j, me, (*chip, c), own=True)
                                                      for j, chip in enumerate(chips)]

    local = lambda a: pltpu.make_async_copy(srcs[a], outs[a].at[slot(me)], local_sems.at[a])

    def start():
        for a in range(n_arr):
            local(a).start()
            for cp in own_copies(a):
                cp.start()

    def finish():
        passed = []
        for j, chip in enumerate(chips):
            for a in range(n_arr):
                copy(a, 1 + j, (*chip, c), me).wait_recv()
                passed.append(copy(a, 4 + j, (*chip, c), sibling))
                passed[-1].start()
        for a in range(n_arr):
            copy(a, 0, sibling, me).wait_recv()
            for j, chip in enumerate(chips):
                copy(a, 4 + j, (*chip, 1 - c), me).wait_recv()
            for cp in own_copies(a):
                cp.wait_send()
        for cp in passed:
            cp.wait_send()
        for a in range(n_arr):
            local(a).wait()

    return start, finish


def _gather_scratch(n_arr):
    return [pltpu.SemaphoreType.DMA((n_arr * GATHER_COPIES,)), pltpu.SemaphoreType.DMA((n_arr * GATHER_COPIES,)),
            pltpu.SemaphoreType.DMA((n_arr,))]


def _pack(parts, row_mult):
    rows, spans, r = [], [], 0
    for p in parts:
        flat = p.reshape(-1)
        nr = -(-flat.shape[0] // (PACK_W * row_mult)) * row_mult
        flat = jnp.pad(flat, (0, nr * PACK_W - flat.shape[0]))
        rows.append(flat.reshape(nr, PACK_W))
        spans.append((r, nr))
        r += nr
    return jnp.concatenate(rows, axis=0), spans


def _unpack(buf, spans, shapes):
    out = []
    for (r, nr), shp in zip(spans, shapes):
        size = math.prod(shp)
        out.append(buf[..., r:r + nr, :].reshape(buf.shape[:-2] + (nr * PACK_W,))[..., :size]
                   .reshape(buf.shape[:-2] + tuple(shp)))
    return out


def _to_shards(full, axis):
    r, c = full.shape
    if axis == 0:
        return full.reshape(N_DEV, r // N_DEV, c)
    return full.reshape(r, N_DEV, c // N_DEV).transpose(1, 0, 2)


def _from_shards(sh, axis):
    _, r, c = sh.shape
    if axis == 0:
        return sh.reshape(N_DEV * r, c)
    return sh.transpose(1, 0, 2).reshape(r, N_DEV * c)


def _rms(x, g, n):
    ms = jnp.sum(x * x, axis=-1, keepdims=True) * (1.0 / n)
    return x * lax.rsqrt(ms + EPS) * g


def _norm_mod_fn(x, g, sc, sh):
    return _rms(x, g, x.shape[-1]) * (1.0 + sc) + sh


@jax.custom_vjp
def _rope(t, ct, s1, s2):
    return t * ct + pltpu.roll(t, 16, 1) * s1 + pltpu.roll(t, HEAD_PAD - 16, 1) * s2


def _rope_fwd(t, ct, s1, s2):
    return _rope(t, ct, s1, s2), (ct, s1, s2)


def _rope_bwd(res, d):
    ct, s1, s2 = res
    dt = d * ct + pltpu.roll(d * s1, HEAD_PAD - 16, 1) + pltpu.roll(d * s2, 16, 1)
    return dt, jnp.zeros_like(ct), jnp.zeros_like(s1), jnp.zeros_like(s2)


_rope.defvjp(_rope_fwd, _rope_bwd)


def _qkv_fn(cq, ckv, krsec, qag, kvag, qng, kng, wuq, wk, wv, ct, s1, s2):
    q_raw = bdot(_rms(cq, qag, cq.shape[-1]), wuq)
    ckvn = _rms(ckv, kvag, ckv.shape[-1])
    k_raw = bdot(ckvn, wk)
    v = bdot(ckvn, wv)
    qs, ks = [], []
    for h in range(N_HEADS):
        sl = slice(h * HEAD_PAD, (h + 1) * HEAD_PAD)
        qs.append(_rope(_rms(q_raw[:, sl], qng, QK_DIM), ct, s1, s2))
        ks.append(_rope(_rms(k_raw[:, sl] + krsec, kng, QK_DIM), ct, s1, s2))
    return jnp.concatenate(qs, axis=1), jnp.concatenate(ks, axis=1), v


def _glu_fn(u, yf, yb, dskip, wglu):
    y = u * dskip + yf + yb
    vg = bdot(jax.nn.gelu(y), wglu)
    d = vg.shape[-1] // 2
    return vg[:, :d] * jax.nn.sigmoid(vg[:, d:])


def _merge_fn(o, s_l, gl, x, g1, wo, wout):
    d = x.shape[-1]
    a = bdot(o, wo)
    mix = jax.nn.sigmoid(gl[:, :d]) * a + jax.nn.sigmoid(gl[:, d:]) * s_l
    return x + g1 * bdot(mix, wout)


def _mod_fn(cmat, w):
    return bdot(jax.nn.silu(cmat), w)


def _ssm_prep_fn(lam_re, lam_im, logdt, bre, bim):
    dt = jnp.exp(logdt)
    ar, ai = lam_re * dt, lam_im * dt
    e = jnp.exp(ar)
    lbr, lbi = e * jnp.cos(ai), e * jnp.sin(ai)
    nr, ni = lbr - 1.0, lbi
    den = lam_re * lam_re + lam_im * lam_im
    qr = (nr * lam_re + ni * lam_im) / den
    qi = (ni * lam_re - nr * lam_im) / den
    return lbr, lbi, qr * bre - qi * bim, qr * bim + qi * bre


def _rows(tm, w, col=0):
    return pl.BlockSpec((tm, w), lambda i: (i, col))


def _full(shape):
    nd = len(shape)
    return pl.BlockSpec(tuple(shape), lambda i: (0,) * nd)


def _acc_add(i, ref, val):
    @pl.when(i == 0)
    def _():
        ref[...] = jnp.zeros_like(ref)
    ref[...] += val


def _pad_rows(v, rows=SUBLANES):
    sel = lax.broadcasted_iota(jnp.int32, (rows, v.shape[-1]), 0) == 0
    return jnp.where(sel, jnp.broadcast_to(v, (rows, v.shape[-1])), 0.0)


def norm_mod_fwd(xa, g, mods, n_lat, tm, name, gather=()):
    r, d = xa.shape
    lat_tiles = n_lat // tm
    n_tiles = r // tm
    n_g = len(gather)

    def body(x_ref, g_ref, m_ref, *rest):
        o_ref = rest[n_g]
        i = pl.program_id(0)
        if n_g:
            start, finish = _two_level_gather(rest[:n_g], rest[n_g + 1:2 * n_g + 1], *rest[2 * n_g + 1:])
            pl.when(i == 0)(start)
        lat = i < lat_tiles
        sc = jnp.where(lat, m_ref[0:1, :], m_ref[2:3, :])
        sh = jnp.where(lat, m_ref[1:2, :], m_ref[3:4, :])
        o_ref[...] = _norm_mod_fn(x_ref[...], g_ref[...], sc, sh).astype(o_ref.dtype)
        if n_g:
            pl.when(i == n_tiles - 1)(finish)

    any_spec = pl.BlockSpec(memory_space=pl.ANY)
    res = pl.pallas_call(
        body, name=name, grid=(n_tiles,),
        in_specs=[_rows(tm, d), _full(g.shape), _full(mods.shape)] + [any_spec] * n_g,
        out_specs=[_rows(tm, d)] + [any_spec] * n_g,
        out_shape=[jax.ShapeDtypeStruct((r, d), BF16)]
        + [jax.ShapeDtypeStruct((N_DEV,) + tuple(a.shape), a.dtype) for a in gather],
        scratch_shapes=_gather_scratch(n_g) if n_g else [],
        compiler_params=_cparams(("arbitrary",) if n_g else ("parallel",)),
    )(xa, g, mods, *gather)
    return res if n_g else res[0]


def norm_mod_bwd(xa, dh, dres, g, mods, n_lat, tm, name):
    r, d = xa.shape
    lat_tiles = n_lat // tm

    def body(x_ref, dh_ref, dres_ref, g_ref, m_ref, dx_ref, dg_ref, dm_ref):
        i = pl.program_id(0)
        lat = i < lat_tiles
        sc = jnp.where(lat, m_ref[0:1, :], m_ref[2:3, :])
        sh = jnp.where(lat, m_ref[1:2, :], m_ref[3:4, :])
        _, vjp = jax.vjp(_norm_mod_fn, x_ref[...], g_ref[...], sc, sh)
        dx, dg, dsc, dsh = vjp(dh_ref[...])
        dx_ref[...] = dx + jnp.where(lat, dres_ref[...], 0.0)
        _acc_add(i, dg_ref, _pad_rows(dg))
        row = lax.broadcasted_iota(jnp.int32, (SUBLANES, d), 0)
        base = jnp.where(lat, 0, 2)
        upd = jnp.where(row == base, jnp.broadcast_to(dsc, (SUBLANES, d)), 0.0)
        upd = upd + jnp.where(row == base + 1, jnp.broadcast_to(dsh, (SUBLANES, d)), 0.0)
        _acc_add(i, dm_ref, upd)

    return pl.pallas_call(
        body, name=name, grid=(r // tm,),
        in_specs=[_rows(tm, d), _rows(tm, d),
                  pl.BlockSpec((tm, d), lambda i: (jnp.minimum(i, lat_tiles - 1), 0)),
                  _full(g.shape), _full(mods.shape)],
        out_specs=[_rows(tm, d), _full((SUBLANES, d)), _full((SUBLANES, d))],
        out_shape=[jax.ShapeDtypeStruct((r, d), F32), jax.ShapeDtypeStruct((SUBLANES, d), F32),
                   jax.ShapeDtypeStruct((SUBLANES, d), F32)],
        compiler_params=_cparams(("arbitrary",)),
    )(xa, dh, dres, g, mods)


def qkv_fwd(proj, lay, gains, wuq, wk, wv, tabs, tm, name):
    r = proj.shape[0]
    q_w, kv_w = lay['q'], lay['kv']

    def body(cq_ref, ckv_ref, kr_ref, qag, kvag, qng, kng, wuq_ref, wk_ref, wv_ref, ct, s1, s2, q_ref, k_ref,
             v_ref):
        q, k, v = _qkv_fn(cq_ref[...], ckv_ref[...], kr_ref[...], qag[...], kvag[...], qng[...], kng[...],
                          wuq_ref[...], wk_ref[...], wv_ref[...], ct[...], s1[...], s2[...])
        q_ref[...] = q.astype(BF16)
        k_ref[...] = k.astype(BF16)
        v_ref[...] = v.astype(BF16)

    out = jax.ShapeDtypeStruct((r, HEADS_W), BF16)
    return pl.pallas_call(
        body, name=name, grid=(r // tm,),
        in_specs=[_rows(tm, q_w, lay['o_cq'] // q_w), _rows(tm, kv_w, lay['o_ckv'] // kv_w),
                  _rows(tm, LANES, lay['o_kr'] // LANES)]
        + [_full(a.shape) for a in gains] + [_full(wuq.shape), _full(wk.shape), _full(wv.shape)]
        + [_rows(tm, HEAD_PAD)] * 3,
        out_specs=[_rows(tm, HEADS_W)] * 3, out_shape=[out, out, out],
        compiler_params=_cparams(("parallel",)),
    )(proj, proj, proj, *gains, wuq, wk, wv, *tabs)


def qkv_bwd(proj, lay, gains, wuq, wk, wv, tabs, dq, dk, dv, dgl, du_direct, du_f, du_b, n, tm, name):
    r = proj.shape[0]
    q_w, kv_w, sw, d = lay['q'], lay['kv'], lay['sw'], lay['d']
    lat_tiles = n // tm

    def body(cq_ref, ckv_ref, kr_ref, qag, kvag, qng, kng, wuq_ref, wk_ref, wv_ref, ct, s1, s2, dq_ref, dk_ref,
             dv_ref, dgl_ref, dud_ref, duf_ref, dub_ref, dp_ref, dqag, dkvag, dqng, dkng, dwuq, dwk, dwv):
        i = pl.program_id(0)
        lat = i < lat_tiles
        tables = (ct[...], s1[...], s2[...])
        fn = lambda *a: _qkv_fn(*a, *tables)
        _, vjp = jax.vjp(fn, cq_ref[...], ckv_ref[...], kr_ref[...], qag[...], kvag[...], qng[...], kng[...],
                         wuq_ref[...].astype(F32), wk_ref[...].astype(F32), wv_ref[...].astype(F32))
        g = vjp((jnp.where(lat, dq_ref[...], 0.0), dk_ref[...], dv_ref[...]))
        dp_ref[:, 0:2 * d] = jnp.where(lat, dgl_ref[...], 0.0).astype(BF16)
        dp_ref[:, lay['o_u']:lay['o_u'] + sw] = (duf_ref[...] + dub_ref[...]
                                                 + jnp.where(lat, dud_ref[...], 0.0)).astype(BF16)
        dp_ref[:, lay['o_ckv']:lay['o_ckv'] + kv_w] = g[1].astype(BF16)
        dp_ref[:, lay['o_kr']:lay['o_kr'] + LANES] = g[2].astype(BF16)
        hole0 = lay['o_kr'] + LANES
        if lay['o_cq'] > hole0:
            dp_ref[:, hole0:lay['o_cq']] = jnp.zeros((tm, lay['o_cq'] - hole0), BF16)
        dp_ref[:, lay['o_cq']:lay['o_cq'] + q_w] = g[0].astype(BF16)
        for ref, val in zip((dqag, dkvag, dqng, dkng), g[3:7]):
            _acc_add(i, ref, _pad_rows(val))
        for ref, val in zip((dwuq, dwk, dwv), g[7:10]):
            _acc_add(i, ref, val)

    def lat_rows(w):
        return pl.BlockSpec((tm, w), lambda i: (jnp.minimum(i, lat_tiles - 1), 0))

    acc_shapes = [(SUBLANES, a.shape[1]) for a in gains] + [wuq.shape, wk.shape, wv.shape]
    return pl.pallas_call(
        body, name=name, grid=(r // tm,),
        in_specs=[_rows(tm, q_w, lay['o_cq'] // q_w), _rows(tm, kv_w, lay['o_ckv'] // kv_w),
                  _rows(tm, LANES, lay['o_kr'] // LANES)]
        + [_full(a.shape) for a in gains] + [_full(wuq.shape), _full(wk.shape), _full(wv.shape)]
        + [_rows(tm, HEAD_PAD)] * 3 + [lat_rows(HEADS_W), _rows(tm, HEADS_W), _rows(tm, HEADS_W)]
        + [lat_rows(2 * d), lat_rows(sw), _rows(tm, sw), _rows(tm, sw)],
        out_specs=[_rows(tm, lay['width'])] + [_full(s) for s in acc_shapes],
        out_shape=[jax.ShapeDtypeStruct((r, lay['width']), BF16)] + [jax.ShapeDtypeStruct(s, F32) for s in acc_shapes],
        compiler_params=_cparams(("arbitrary",)),
    )(proj, proj, proj, *gains, wuq, wk, wv, *tabs, dq, dk, dv, dgl, du_direct, du_f, du_b)


def glu_fwd(proj, lay, yf, yb, dskip, wglu, n, tm, name):
    sw, d = wglu.shape[0], wglu.shape[1] // 2

    def body(u_ref, yf_ref, yb_ref, ds_ref, w_ref, o_ref):
        o_ref[...] = _glu_fn(u_ref[...], yf_ref[...], yb_ref[...], ds_ref[...], w_ref[...])

    return pl.pallas_call(
        body, name=name, grid=(n // tm,),
        in_specs=[_rows(tm, sw, lay['o_u'] // sw), _rows(tm, sw), _rows(tm, sw), _full(dskip.shape),
                  _full(wglu.shape)],
        out_specs=_rows(tm, d), out_shape=jax.ShapeDtypeStruct((n, d), F32),
        compiler_params=_cparams(("parallel",)),
    )(proj, yf, yb, dskip, wglu)


def glu_bwd(proj, lay, yf, yb, dskip, wglu, ds_l, n, tm, name):
    sw, d = wglu.shape[0], wglu.shape[1] // 2

    def body(u_ref, yf_ref, yb_ref, ds_ref, w_ref, g_ref, du_ref, dy_ref, dds_ref, dw_ref):
        i = pl.program_id(0)
        _, vjp = jax.vjp(_glu_fn, u_ref[...], yf_ref[...], yb_ref[...], ds_ref[...], w_ref[...].astype(F32))
        du, dyf, _, dds, dw = vjp(g_ref[...])
        du_ref[...] = du
        dy_ref[...] = dyf
        _acc_add(i, dds_ref, _pad_rows(dds))
        _acc_add(i, dw_ref, dw)

    return pl.pallas_call(
        body, name=name, grid=(n // tm,),
        in_specs=[_rows(tm, sw, lay['o_u'] // sw), _rows(tm, sw), _rows(tm, sw), _full(dskip.shape),
                  _full(wglu.shape), _rows(tm, d)],
        out_specs=[_rows(tm, sw), _rows(tm, sw), _full((SUBLANES, sw)), _full(wglu.shape)],
        out_shape=[jax.ShapeDtypeStruct((n, sw), F32), jax.ShapeDtypeStruct((n, sw), F32),
                   jax.ShapeDtypeStruct((SUBLANES, sw), F32), jax.ShapeDtypeStruct(wglu.shape, F32)],
        compiler_params=_cparams(("arbitrary",)),
    )(proj, yf, yb, dskip, wglu, ds_l)


def merge_fwd(o, s_l, proj, xa, g1, wo, wout, n, tm, name):
    d = xa.shape[1]

    def body(o_ref, s_ref, gl_ref, x_ref, g1_ref, wo_ref, wout_ref, x1_ref):
        x1_ref[...] = _merge_fn(o_ref[...], s_ref[...], gl_ref[...], x_ref[...], g1_ref[...], wo_ref[...],
                                wout_ref[...])

    return pl.pallas_call(
        body, name=name, grid=(n // tm,),
        in_specs=[_rows(tm, HEADS_W), _rows(tm, d), _rows(tm, 2 * d), _rows(tm, d), _full(g1.shape),
                  _full(wo.shape), _full(wout.shape)],
        out_specs=_rows(tm, d), out_shape=jax.ShapeDtypeStruct((n, d), F32),
        compiler_params=_cparams(("parallel",)),
    )(o, s_l, proj, xa, g1, wo, wout)


def merge_bwd(o, s_l, proj, xa, g1, wo, wout, dx1, n, tm, name):
    d = xa.shape[1]

    def body(o_ref, s_ref, gl_ref, x_ref, g1_ref, wo_ref, wout_ref, dx1_ref, do_ref, ds_ref, dgl_ref, dg1_ref,
             dwo_ref, dwout_ref):
        i = pl.program_id(0)
        _, vjp = jax.vjp(_merge_fn, o_ref[...], s_ref[...], gl_ref[...], x_ref[...], g1_ref[...],
                         wo_ref[...].astype(F32), wout_ref[...].astype(F32))
        do, ds, dgl, _, dg1, dwo, dwout = vjp(dx1_ref[...])
        do_ref[...] = do
        ds_ref[...] = ds
        dgl_ref[...] = dgl
        _acc_add(i, dg1_ref, _pad_rows(dg1))
        _acc_add(i, dwo_ref, dwo)
        _acc_add(i, dwout_ref, dwout)

    return pl.pallas_call(
        body, name=name, grid=(n // tm,),
        in_specs=[_rows(tm, HEADS_W), _rows(tm, d), _rows(tm, 2 * d), _rows(tm, d), _full(g1.shape),
                  _full(wo.shape), _full(wout.shape), _rows(tm, d)],
        out_specs=[_rows(tm, HEADS_W), _rows(tm, d), _rows(tm, 2 * d), _full((SUBLANES, d)), _full(wo.shape),
                   _full(wout.shape)],
        out_shape=[jax.ShapeDtypeStruct((n, HEADS_W), BF16), jax.ShapeDtypeStruct((n, d), F32),
                   jax.ShapeDtypeStruct((n, 2 * d), F32), jax.ShapeDtypeStruct((SUBLANES, d), F32),
                   jax.ShapeDtypeStruct(wo.shape, F32), jax.ShapeDtypeStruct(wout.shape, F32)],
        compiler_params=_cparams(("arbitrary",)),
    )(o, s_l, proj, xa, g1, wo, wout, dx1)


def _halo_specs(tm, w, n):
    nb = n // SUBLANES
    per = tm // SUBLANES
    prev = pl.BlockSpec((SUBLANES, w), lambda i: (jnp.maximum(i * per - 1, 0), 0))
    nxt = pl.BlockSpec((SUBLANES, w), lambda i: (jnp.minimum((i + 1) * per, nb - 1), 0))
    return prev, nxt


def _shifted(t, prev_blk, next_blk, i, n_tiles):
    tm = t.shape[0]
    row = lax.broadcasted_iota(jnp.int32, t.shape, 0)
    prev_row = jnp.where(i > 0, prev_blk[SUBLANES - 1:SUBLANES, :], 0.0)
    next_row = jnp.where(i < n_tiles - 1, next_blk[0:1, :], 0.0)
    before = jnp.where(row == 0, prev_row, pltpu.roll(t, 1, 0))
    after = jnp.where(row == tm - 1, next_row, pltpu.roll(t, tm - 1, 0))
    return before, after


def _conv_u2(up, before, after, cw, cb):
    return before * cw[0:1, :] + up * cw[1:2, :] + after * cw[2:3, :] + cb


def conv_act_fwd(up, cw, cb, tm, name):
    n, w2 = up.shape
    f = w2 // 2
    n_tiles = n // tm
    prev, nxt = _halo_specs(tm, w2, n)

    def body(up_ref, prev_ref, next_ref, cw_ref, cb_ref, act_ref):
        i = pl.program_id(0)
        t = up_ref[...]
        before, after = _shifted(t, prev_ref[...], next_ref[...], i, n_tiles)
        u2 = _conv_u2(t, before, after, cw_ref[...], cb_ref[...])
        act_ref[...] = (jax.nn.silu(u2[:, f:]) * u2[:, :f]).astype(BF16)

    return pl.pallas_call(
        body, name=name, grid=(n_tiles,),
        in_specs=[_rows(tm, w2), prev, nxt, _full(cw.shape), _full(cb.shape)],
        out_specs=_rows(tm, f), out_shape=jax.ShapeDtypeStruct((n, f), BF16),
        compiler_params=_cparams(("parallel",)),
    )(up, up, up, cw, cb)


def down_loss(act, wdown, x1, g2, target, tm, name):
    n, d = x1.shape
    f = act.shape[1]

    def body(act_ref, w_ref, x1_ref, g2_ref, t_ref, dy_ref, ffn_ref, loss_ref):
        ffn = jnp.dot(act_ref[...], w_ref[...], preferred_element_type=F32)
        err = x1_ref[...] + g2_ref[...] * ffn - t_ref[...]
        ffn_ref[...] = ffn
        dy_ref[...] = err * (1.0 / d)
        part = 0.5 * jnp.sum(jnp.sum(err * err, axis=-1, keepdims=True) * (1.0 / d), axis=0, keepdims=True)
        loss_ref[0] = jnp.broadcast_to(part, (SUBLANES, LANES))

    return pl.pallas_call(
        body, name=name, grid=(n // tm,),
        in_specs=[_rows(tm, f), _full(wdown.shape), _rows(tm, d), _full(g2.shape), _rows(tm, d)],
        out_specs=[_rows(tm, d), _rows(tm, d), pl.BlockSpec((1, SUBLANES, LANES), lambda i: (i, 0, 0))],
        out_shape=[jax.ShapeDtypeStruct((n, d), F32), jax.ShapeDtypeStruct((n, d), F32),
                   jax.ShapeDtypeStruct((n // tm, SUBLANES, LANES), F32)],
        compiler_params=_cparams(("parallel",)),
    )(act, wdown, x1, g2, target)


FFN_BWD_PARTS = 2


def ffn_bwd(dy, ffn, up, cw, cb, wdown, g2, tm, name):
    n, d = dy.shape
    w2 = up.shape[1]
    f = w2 // 2
    fc = f // FFN_BWD_PARTS
    assert fc % LANES == 0
    n_tiles = n // tm
    ext = tm + 2 * SUBLANES
    inner = slice(SUBLANES, SUBLANES + tm)
    prev_w, next_w = _halo_specs(tm, w2, n)
    prev_d, next_d = _halo_specs(tm, d, n)

    def body(dy_ref, dyp_ref, dyn_ref, ffn_ref, up_ref, upp_ref, upn_ref, cw_ref, cb_ref, w_ref, g2_ref,
             dup_ref, dffn_ref, dg2_ref, dcw_ref):
        i = pl.program_id(0)
        has_prev, has_next = i > 0, i < n_tiles - 1

        def extended(prev, tile, nxt):
            return jnp.concatenate([jnp.where(has_prev, prev, 0.0), tile, jnp.where(has_next, nxt, 0.0)], axis=0)

        dyv = dy_ref[...]
        dffn = extended(dyp_ref[...], dyv, dyn_ref[...]) * g2_ref[...]
        dffn_ref[...] = dffn[inner].astype(BF16)
        dffn = dffn.astype(BF16)
        _acc_add(i, dg2_ref, _pad_rows(jnp.sum(dyv * ffn_ref[...], axis=0, keepdims=True)))
        row = lax.broadcasted_iota(jnp.int32, (SUBLANES, fc), 0)
        shift = lambda t: (pltpu.roll(t, 1, 0), pltpu.roll(t, ext - 1, 0))
        for part in range(FFN_BWD_PARTS):
            halves = []
            for col0 in (part * fc, f + part * fc):
                cols = slice(col0, col0 + fc)
                t = extended(upp_ref[:, cols], up_ref[:, cols], upn_ref[:, cols])
                before, after = shift(t)
                halves.append((cols, t, _conv_u2(t, before, after, cw_ref[:, cols], cb_ref[:, cols])))
            (_, _, val), (_, _, gate) = halves
            dact = _dot_nt(dffn, w_ref[part * fc:(part + 1) * fc, :])
            sg = jax.nn.sigmoid(gate)
            for (cols, t, _), du2 in zip(halves, (dact * (gate * sg),
                                                   dact * val * (sg * (1.0 + gate * (1.0 - sg))))):
                before, after = shift(du2)
                cwv = cw_ref[:, cols]
                dup_ref[:, cols] = (after * cwv[0:1, :] + du2 * cwv[1:2, :] + before * cwv[2:3, :])[inner].astype(BF16)
                upd = jnp.zeros((SUBLANES, fc), F32)
                for k, term in enumerate((after * t, du2 * t, before * t, du2)):
                    upd = upd + jnp.where(row == k, jnp.broadcast_to(
                        jnp.sum(term[inner], axis=0, keepdims=True), (SUBLANES, fc)), 0.0)

                @pl.when(i == 0)
                def _():
                    dcw_ref[:, cols] = jnp.zeros((SUBLANES, fc), F32)

                dcw_ref[:, cols] += upd

    return pl.pallas_call(
        body, name=name, grid=(n_tiles,),
        in_specs=[_rows(tm, d), prev_d, next_d, _rows(tm, d), _rows(tm, w2), prev_w, next_w, _full(cw.shape),
                  _full(cb.shape), _full(wdown.shape), _full(g2.shape)],
        out_specs=[_rows(tm, w2), _rows(tm, d), _full((SUBLANES, d)), _full((SUBLANES, w2))],
        out_shape=[jax.ShapeDtypeStruct((n, w2), BF16), jax.ShapeDtypeStruct((n, d), BF16),
                   jax.ShapeDtypeStruct((SUBLANES, d), F32), jax.ShapeDtypeStruct((SUBLANES, w2), F32)],
        compiler_params=_cparams(("arbitrary",)),
    )(dy, dy, dy, ffn, up, up, up, cw, cb, wdown, g2)


def attn_fwd(q, k, v, n, name, gather=()):
    nk = k.shape[0]
    tq = _tile(n, (2048, 1024, 512, 256, 128))
    tk = _tile(nk, (768, 384, 256, 128))
    n_kv = nk // tk
    n_q = n // tq
    n_g = len(gather)
    scale = QK_DIM ** -0.5
    c2 = scale * math.log2(math.e)

    def body(q_ref, k_ref, v_ref, *rest):
        o_ref, lse_ref = rest[n_g:n_g + 2]
        if n_g:
            start, finish = _two_level_gather(rest[:n_g], rest[n_g + 2:2 * n_g + 2], *rest[2 * n_g + 2:])
            step = pl.program_id(0) * n_q + pl.program_id(1)
            pl.when(step == 0)(start)
        qv = q_ref[...]
        ones_col = (lax.broadcasted_iota(jnp.int32, (tk, HEAD_PAD), 1) == V_DIM).astype(BF16)

        def chunk(j, carry):
            m, acc = carry
            rows = pl.ds(pl.multiple_of(j * tk, tk), tk)
            s = _dot_nt(qv, k_ref[rows, :])
            m_new = jnp.maximum(m, jnp.max(s, axis=-1, keepdims=True))
            p = jnp.exp2(s * c2 - m_new * c2)
            alpha = jnp.exp2((m - m_new) * c2)
            pv = jnp.dot(p.astype(BF16), v_ref[rows, :] + ones_col, preferred_element_type=F32)
            return m_new, alpha * acc + pv

        m, acc = lax.fori_loop(0, n_kv, chunk, (jnp.full((tq, 1), -jnp.inf, F32),
                                                jnp.zeros((tq, HEAD_PAD), F32)), unroll=True)
        l = acc[:, V_DIM:V_DIM + 1]
        lane = lax.broadcasted_iota(jnp.int32, (tq, HEAD_PAD), 1)
        o_ref[...] = jnp.where(lane < V_DIM, acc / l, 0.0).astype(BF16)
        lse_ref[...] = jnp.broadcast_to(m * scale + jnp.log(l), (tq, HEAD_PAD))
        if n_g:
            pl.when(step == N_HEADS * n_q - 1)(finish)

    qspec = pl.BlockSpec((tq, HEAD_PAD), lambda h, i: (i, h))
    kspec = pl.BlockSpec((nk, HEAD_PAD), lambda h, i: (0, h))
    any_spec = pl.BlockSpec(memory_space=pl.ANY)
    return pl.pallas_call(
        body, name=name, grid=(N_HEADS, n_q),
        in_specs=[qspec, kspec, kspec] + [any_spec] * n_g, out_specs=[qspec, qspec] + [any_spec] * n_g,
        out_shape=[jax.ShapeDtypeStruct((n, HEADS_W), BF16), jax.ShapeDtypeStruct((n, HEADS_W), F32)]
        + [jax.ShapeDtypeStruct((N_DEV,) + tuple(a.shape), a.dtype) for a in gather],
        scratch_shapes=_gather_scratch(n_g) if n_g else [],
        compiler_params=_cparams(("arbitrary", "arbitrary") if n_g else ("parallel", "parallel")),
    )(q, k, v, *gather)


def attn_bwd(q, k, v, o, do, lse, n, name, a2a=()):
    nk = k.shape[0]
    tq = _tile(n, (1024, 512, 256, 128))
    tk = _tile(nk, (768, 384, 256, 128))
    n_kv = nk // tk
    n_x = len(a2a)
    kinds = ['a2a'] * n_x
    scale = QK_DIM ** -0.5
    log2e = math.log2(math.e)
    c2 = scale * log2e

    def body(q_ref, k_ref, v_ref, o_ref, do_ref, lse_ref, *rest):
        dq_ref, dk_ref, dv_ref = rest[n_x:n_x + 3]
        if n_x:
            start, finish = _exchange_ops(rest[:n_x], rest[n_x + 3:2 * n_x + 3], *rest[2 * n_x + 3:], kinds, 'all')
            step = pl.program_id(0) * n_kv + pl.program_id(1)
            pl.when(step == 0)(start)

        @pl.when(pl.program_id(1) == 0)
        def _():
            dq_ref[...] = jnp.zeros_like(dq_ref)

        kv, vv = k_ref[...], v_ref[...]

        def q_tile(i, carry):
            dk, dv = carry
            rows = pl.ds(pl.multiple_of(i * tq, tq), tq)
            qv, dov = q_ref[rows, :], do_ref[rows, :]
            p = jnp.exp2(_dot_nt(qv, kv) * c2 - lse_ref[rows, 0:1] * log2e)
            dv = dv + _dot_tn(p.astype(BF16), dov)
            dp = _dot_nt(dov, vv)
            delta = jnp.sum(dov.astype(F32) * o_ref[rows, :].astype(F32), axis=-1, keepdims=True)
            ds = (p * (dp - delta) * scale).astype(BF16)
            dk = dk + _dot_tn(ds, qv)
            dq_ref[rows, :] += jnp.dot(ds, kv, preferred_element_type=F32)
            return dk, dv

        zero = jnp.zeros((tk, HEAD_PAD), F32)
        dk, dv = lax.fori_loop(0, n // tq, q_tile, (zero, zero), unroll=2)
        dk_ref[...] = dk
        dv_ref[...] = dv
        if n_x:
            pl.when(step == N_HEADS * n_kv - 1)(finish)

    qspec = pl.BlockSpec((n, HEAD_PAD), lambda h, j: (0, h))
    kspec = pl.BlockSpec((tk, HEAD_PAD), lambda h, j: (j, h))
    any_spec = pl.BlockSpec(memory_space=pl.ANY)
    return pl.pallas_call(
        body, name=name, grid=(N_HEADS, n_kv),
        in_specs=[qspec, kspec, kspec, qspec, qspec, qspec] + [any_spec] * n_x,
        out_specs=[qspec, kspec, kspec] + [any_spec] * n_x,
        out_shape=[jax.ShapeDtypeStruct((n, HEADS_W), F32), jax.ShapeDtypeStruct((nk, HEADS_W), F32),
                   jax.ShapeDtypeStruct((nk, HEADS_W), F32)] + _exchange_shapes(a2a, kinds, 'all'),
        scratch_shapes=_exchange_scratch(n_x, 'all') if n_x else [],
        compiler_params=_cparams(("arbitrary", "arbitrary") if n_x else ("parallel", "arbitrary")),
    )(q, k, v, o, do, lse, *a2a)


SCAN_LEVELS = (1, 2, 4)
SCAN_LANES = 512


def _scan_chunk(xr, xi, car, m_ref, p_ref, reverse):
    t_len, gn = xr.shape
    n_slab = t_len // SUBLANES
    for lb in range(gn // SCAN_LANES):
        ls = pl.ds(lb * SCAN_LANES, SCAN_LANES)

        def step(s, carry, ls=ls):
            cr, ci = carry
            slab = (n_slab - 1 - s) if reverse else s
            rows = pl.ds(pl.multiple_of(slab * SUBLANES, SUBLANES), SUBLANES)
            br, bi = xr[rows, ls], xi[rows, ls]
            for lvl, d in enumerate(SCAN_LEVELS):
                shift = SUBLANES - d if reverse else d
                sr, si = pltpu.roll(br, shift, 0), pltpu.roll(bi, shift, 0)
                mr, mi = m_ref[lvl, 0, :, ls], m_ref[lvl, 1, :, ls]
                br, bi = br + mr * sr - mi * si, bi + mr * si + mi * sr
            pr, pi = p_ref[0, :, ls], p_ref[1, :, ls]
            br, bi = br + pr * cr - pi * ci, bi + pr * ci + pi * cr
            xr[rows, ls] = br
            xi[rows, ls] = bi
            last = 0 if reverse else SUBLANES - 1
            return br[last:last + 1, :], bi[last:last + 1, :]

        cr, ci = lax.fori_loop(0, n_slab, step, (car[0:1, ls], car[1:2, ls]))
        car[0:1, ls] = cr
        car[1:2, ls] = ci


SSM_SPLIT = 2


def _bd_dot(a, w_ref):
    k, n = w_ref.shape[0] // SSM_SPLIT, w_ref.shape[1] // SSM_SPLIT
    return jnp.concatenate([jnp.dot(a[:, p * k:(p + 1) * k], w_ref[p * k:(p + 1) * k, p * n:(p + 1) * n],
                                    preferred_element_type=F32) for p in range(SSM_SPLIT)], axis=1)


def _bd_dot_nt(a, w_ref):
    k, n = w_ref.shape[0] // SSM_SPLIT, w_ref.shape[1] // SSM_SPLIT
    return jnp.concatenate([_dot_nt(a[:, p * n:(p + 1) * n], w_ref[p * k:(p + 1) * k, p * n:(p + 1) * n])
                            for p in range(SSM_SPLIT)], axis=1)


def _seq_block(step, n_chunk, lat_chunks, reverse):
    if reverse:
        return n_chunk - 1 - step
    return (step + lat_chunks) % n_chunk


def ssm_fwd(proj, lay, bre, bim, cre, ncim, tabs, n, t_len, reverse, name):
    l, sw = proj.shape[0], lay['sw']
    gn = bre.shape[-1]
    n_chunk, lat_chunks = l // t_len, n // t_len
    blk = lambda k: _seq_block(k, n_chunk, lat_chunks, reverse)
    mtab, ptab = tabs

    def body(u_ref, bre_ref, bim_ref, cre_ref, ncim_ref, m_ref, p_ref, y_ref, xb_ref, xr, xi, car):
        @pl.when(pl.program_id(0) == 0)
        def _():
            car[...] = jnp.zeros_like(car)

        xb_ref[...] = car[...]
        u = u_ref[...].astype(BF16)
        xr[...] = _bd_dot(u, bre_ref)
        xi[...] = _bd_dot(u, bim_ref)
        _scan_chunk(xr, xi, car, m_ref, p_ref, reverse)
        y_ref[...] = _bd_dot(xr[...].astype(BF16), cre_ref) + _bd_dot(xi[...].astype(BF16), ncim_ref)

    return pl.pallas_call(
        body, name=name, grid=(n_chunk,),
        in_specs=[pl.BlockSpec((t_len, sw), lambda k: (blk(k), lay['o_u'] // sw)), _full(bre.shape),
                  _full(bim.shape), _full(cre.shape), _full(ncim.shape), _full(mtab.shape), _full(ptab.shape)],
        out_specs=[pl.BlockSpec((t_len, sw), lambda k: (blk(k), 0)),
                   pl.BlockSpec((None, 2, gn), lambda k: (k, 0, 0))],
        out_shape=[jax.ShapeDtypeStruct((l, sw), F32), jax.ShapeDtypeStruct((n_chunk, 2, gn), F32)],
        scratch_shapes=[pltpu.VMEM((t_len, gn), F32), pltpu.VMEM((t_len, gn), F32), pltpu.VMEM((2, gn), F32)],
        compiler_params=_cparams(("arbitrary",)),
    )(proj, bre, bim, cre, ncim, mtab, ptab)


def ssm_bwd(proj, lay, dyr, xb, bre, bim, cre, ncim, tabs, adj_tabs, n, t_len, reverse, name):
    l, sw = proj.shape[0], lay['sw']
    gn = bre.shape[-1]
    n_chunk, lat_chunks = l // t_len, n // t_len
    fwd_step = lambda k: n_chunk - 1 - k
    blk = lambda k: _seq_block(fwd_step(k), n_chunk, lat_chunks, reverse)
    (mtab, ptab), (mtab_r, ptab_r) = tabs, adj_tabs

    def body(u_ref, dy_ref, xb_ref, bre_ref, bim_ref, cre_ref, ncim_ref, m_ref, p_ref, mr_ref, pr_ref,
             du_ref, gr_ref, gi_ref, xr_ref, xi_ref, dlam_ref, xr, xi, gr, gi, car, acar):
        k = pl.program_id(0)

        @pl.when(k == 0)
        def _():
            acar[...] = jnp.zeros_like(acar)
            dlam_ref[...] = jnp.zeros_like(dlam_ref)

        u = u_ref[...].astype(BF16)
        dy = jnp.where(blk(k) < lat_chunks, dy_ref[...], 0.0).astype(BF16)
        xr[...] = _bd_dot(u, bre_ref)
        xi[...] = _bd_dot(u, bim_ref)
        car[...] = xb_ref[...]
        _scan_chunk(xr, xi, car, m_ref, p_ref, reverse)
        gr[...] = _bd_dot_nt(dy, cre_ref)
        gi[...] = _bd_dot_nt(dy, ncim_ref)
        _scan_chunk(gr, gi, acar, mr_ref, pr_ref, not reverse)
        xrv, xiv, grv, giv = xr[...], xi[...], gr[...], gi[...]
        row = lax.broadcasted_iota(jnp.int32, (t_len, gn), 0)
        first, shift = (t_len - 1, t_len - 1) if reverse else (0, 1)
        xpr = jnp.where(row == first, xb_ref[0:1, :], pltpu.roll(xrv, shift, 0))
        xpi = jnp.where(row == first, xb_ref[1:2, :], pltpu.roll(xiv, shift, 0))
        dlr = grv * xpr + giv * xpi
        dli = giv * xpr - grv * xpi
        dlam_ref[0] += jnp.sum(dlr.reshape(t_len // SUBLANES, SUBLANES, gn), axis=0)
        dlam_ref[1] += jnp.sum(dli.reshape(t_len // SUBLANES, SUBLANES, gn), axis=0)
        grb, gib = grv.astype(BF16), giv.astype(BF16)
        du_ref[...] = _bd_dot_nt(grb, bre_ref) + _bd_dot_nt(gib, bim_ref)
        gr_ref[...] = grb
        gi_ref[...] = gib
        xr_ref[...] = xrv.astype(BF16)
        xi_ref[...] = xiv.astype(BF16)

    def at_blk(width, col=0):
        return pl.BlockSpec((t_len, width), lambda k: (blk(k), col))

    state = jax.ShapeDtypeStruct((l, gn), BF16)
    return pl.pallas_call(
        body, name=name, grid=(n_chunk,),
        in_specs=[at_blk(sw, lay['o_u'] // sw),
                  pl.BlockSpec((t_len, sw), lambda k: (jnp.minimum(blk(k), lat_chunks - 1), 0)),
                  pl.BlockSpec((None, 2, gn), lambda k: (fwd_step(k), 0, 0)),
                  _full(bre.shape), _full(bim.shape), _full(cre.shape), _full(ncim.shape), _full(mtab.shape),
                  _full(ptab.shape), _full(mtab_r.shape), _full(ptab_r.shape)],
        out_specs=[at_blk(sw), at_blk(gn), at_blk(gn), at_blk(gn), at_blk(gn), _full((2, SUBLANES, gn))],
        out_shape=[jax.ShapeDtypeStruct((l, sw), F32), state, state, state, state,
                   jax.ShapeDtypeStruct((2, SUBLANES, gn), F32)],
        scratch_shapes=[pltpu.VMEM((t_len, gn), F32)] * 4 + [pltpu.VMEM((2, gn), F32)] * 2,
        compiler_params=_cparams(("arbitrary",)),
    )(proj, dyr, xb, bre, bim, cre, ncim, mtab, ptab, mtab_r, ptab_r)


def ssm_prep(lam_re, lam_im, logdt, bre, bim, name):
    gn = lam_re.shape[1]

    def body(lr_ref, li_ref, dt_ref, br_ref, bi_ref, pwr_ref, pwi_ref, bbr_ref, bbi_ref):
        _, _, bbr, bbi = _ssm_prep_fn(lr_ref[...], li_ref[...], dt_ref[...], br_ref[...], bi_ref[...])
        bbr_ref[...] = bbr
        bbi_ref[...] = bbi
        kk = (lax.broadcasted_iota(jnp.int32, (SUBLANES, gn), 0) + 1).astype(F32)
        dt = jnp.exp(dt_ref[...])
        ar, ai = lr_ref[...] * dt * kk, li_ref[...] * dt * kk
        e = jnp.exp(ar)
        pwr_ref[...] = e * jnp.cos(ai)
        pwi_ref[...] = e * jnp.sin(ai)

    ins = (lam_re, lam_im, logdt, bre, bim)
    return pl.pallas_call(
        body, name=name,
        out_shape=[jax.ShapeDtypeStruct((SUBLANES, gn), F32)] * 2 + [jax.ShapeDtypeStruct(bre.shape, F32)] * 2,
        compiler_params=_cparams(),
    )(*ins)


def ssm_prep_bwd(lam_re, lam_im, logdt, bre, bim, dlbr, dlbi, dbbr, dbbi, name):
    def body(lr_ref, li_ref, dt_ref, br_ref, bi_ref, g0, g1, g2, g3, o0, o1, o2, o3, o4):
        _, vjp = jax.vjp(_ssm_prep_fn, lr_ref[...], li_ref[...], dt_ref[...], br_ref[...], bi_ref[...])
        for ref, val in zip((o0, o1, o2, o3, o4), vjp((g0[...], g1[...], g2[...], g3[...]))):
            ref[...] = val

    ins = (lam_re, lam_im, logdt, bre, bim)
    return pl.pallas_call(
        body, name=name,
        out_shape=[jax.ShapeDtypeStruct(a.shape, F32) for a in ins],
        compiler_params=_cparams(),
    )(*ins, dlbr, dlbi, dbbr, dbbi)


def mod_fwd(cmat, w, b, name):
    def body(c_ref, w_ref, b_ref, o_ref):
        o_ref[...] = _mod_fn(c_ref[...], w_ref[...]) + b_ref[...]

    return pl.pallas_call(body, name=name, out_shape=jax.ShapeDtypeStruct((cmat.shape[0], w.shape[1]), F32),
                          compiler_params=_cparams())(cmat, w, b)


def mod_bwd(cmat, w, g_lat, g_ctx, g_all, name):
    def body(c_ref, w_ref, gl_ref, gc_ref, ga_ref, dw_ref, dc_ref, db_ref):
        gc = jnp.sum(gc_ref[...], axis=0, keepdims=True)
        dm = jnp.concatenate([gl_ref[...], _pad_rows(gc)], axis=0)
        _, vjp = jax.vjp(_mod_fn, c_ref[...], w_ref[...])
        dc, dw = vjp(dm)
        dw_ref[...] = dw
        dc_ref[...] = dc
        db_ref[...] = _pad_rows(jnp.sum(ga_ref[...], axis=0, keepdims=True))

    return pl.pallas_call(
        body, name=name,
        out_shape=[jax.ShapeDtypeStruct(w.shape, F32), jax.ShapeDtypeStruct(cmat.shape, F32),
                   jax.ShapeDtypeStruct((SUBLANES, g_all.shape[1]), F32)],
        compiler_params=_cparams(),
    )(cmat, w, g_lat, g_ctx, g_all)


def add_own(own, idx, recv, out_dtype, name):
    _, s, r, c = own.shape

    def body(idx_ref, own_ref, recv_ref, o_ref):
        o_ref[...] = (own_ref[...] + recv_ref[...]).astype(o_ref.dtype)

    return pl.pallas_call(
        body, name=name,
        grid_spec=pltpu.PrefetchScalarGridSpec(
            num_scalar_prefetch=1, grid=(s,),
            in_specs=[pl.BlockSpec((None, None, r, c), lambda k, idx_ref: (idx_ref[0], k, 0, 0)),
                      pl.BlockSpec((None, r, c), lambda k, idx_ref: (k, 0, 0))],
            out_specs=pl.BlockSpec((None, r, c), lambda k, idx_ref: (k, 0, 0))),
        out_shape=jax.ShapeDtypeStruct((s, r, c), out_dtype),
        compiler_params=_cparams(("parallel",)),
    )(idx, own, recv)


def reduce_adamw(parts, w, m, v, name):
    s, r, c = parts.shape
    tr = _tile(r, (256, 128, 64, 32, 16, 8))

    def body(p_ref, w_ref, m_ref, v_ref, g_ref, d_ref, nm_ref, nv_ref):
        g = p_ref[0].astype(F32)
        for k in range(1, s):
            g = g + p_ref[k].astype(F32)
        mm = ADAM_B1 * m_ref[...] + (1.0 - ADAM_B1) * g
        vv = ADAM_B2 * v_ref[...] + (1.0 - ADAM_B2) * jnp.square(g)
        m_hat = mm / (1.0 - ADAM_B1 ** ADAM_STEP)
        v_hat = vv / (1.0 - ADAM_B2 ** ADAM_STEP)
        g_ref[...] = g
        d_ref[...] = -ADAM_LR * (m_hat / (jnp.sqrt(v_hat) + ADAM_EPS) + ADAM_WD * w_ref[...])
        nm_ref[...] = mm
        nv_ref[...] = vv

    out = jax.ShapeDtypeStruct((r, c), F32)
    blk = _rows(tr, c)
    return pl.pallas_call(
        body, name=name, grid=(r // tr,),
        in_specs=[pl.BlockSpec((s, tr, c), lambda i: (0, i, 0)), blk, blk, blk],
        out_specs=[blk] * 4, out_shape=[out] * 4,
        compiler_params=_cparams(("parallel",)),
    )(parts, w, m, v)


def _in_layout(d, q, kv, sw):
    o_u = 2 * d
    o_ckv = o_u + sw
    o_kr = o_ckv + kv
    o_cq = -(-(o_kr + LANES) // q) * q
    assert o_u % sw == 0 and o_ckv % kv == 0 and o_kr % LANES == 0
    assert q % LANES == 0 and kv % LANES == 0 and sw % LANES == 0
    return dict(d=d, q=q, kv=kv, sw=sw, o_gl=0, o_u=o_u, o_ckv=o_ckv, o_kr=o_kr, o_cq=o_cq, width=o_cq + q)


def _pad_w_in(w_in, lay):
    q, kv, sw, d = lay['q'], lay['kv'], lay['sw'], lay['d']
    cq, ckv, kr, u, gl = jnp.split(w_in, [q, q + kv, q + kv + QK_ROPE, q + kv + QK_ROPE + sw], axis=1)
    z = lambda w: jnp.zeros((w_in.shape[0], w), w_in.dtype)
    hole = lay['o_cq'] - lay['o_kr'] - LANES
    return jnp.concatenate([gl, u, ckv, z(QK_NOPE), kr, z(LANES - QK_DIM), z(hole), cq], axis=1)


def _unpad_w_in(g, lay):
    q, kv, sw, d = lay['q'], lay['kv'], lay['sw'], lay['d']
    kr0 = lay['o_kr'] + QK_NOPE
    return jnp.concatenate([g[:, lay['o_cq']:lay['o_cq'] + q], g[:, lay['o_ckv']:lay['o_ckv'] + kv],
                            g[:, kr0:kr0 + QK_ROPE], g[:, lay['o_u']:lay['o_u'] + sw], g[:, :2 * d]], axis=1)


def _pad_heads(w, width):
    k = w.shape[0]
    return jnp.pad(w.reshape(k, N_HEADS, width), ((0, 0), (0, 0), (0, HEAD_PAD - width))).reshape(k, HEADS_W)


def _unpad_heads(w, width):
    k = w.shape[0]
    return w.reshape(k, N_HEADS, HEAD_PAD)[:, :, :width].reshape(k, N_HEADS * width)


def _rope_tables(n, nc):
    rows = n // GRID_W
    row = jnp.repeat(jnp.arange(rows), GRID_W)
    col = jnp.tile(jnp.arange(GRID_W), rows)
    pairs = QK_ROPE // 4
    freqs = ROPE_THETA ** (-jnp.arange(pairs, dtype=F32) / pairs)
    ang = jnp.concatenate([row[:, None] * freqs, col[:, None] * freqs], axis=-1)
    cos = jnp.concatenate([jnp.cos(ang), jnp.ones((nc, 2 * pairs), F32)], axis=0)
    sin = jnp.concatenate([jnp.sin(ang), jnp.zeros((nc, 2 * pairs), F32)], axis=0)
    l = n + nc
    half = QK_ROPE // 2
    ct = jnp.concatenate([jnp.ones((l, QK_NOPE), F32), cos, cos, jnp.zeros((l, HEAD_PAD - QK_DIM), F32)], axis=1)
    s1 = jnp.concatenate([jnp.zeros((l, QK_NOPE + half), F32), sin, jnp.zeros((l, HEAD_PAD - QK_DIM), F32)],
                         axis=1)
    s2 = jnp.concatenate([jnp.zeros((l, QK_NOPE), F32), -sin, jnp.zeros((l, HEAD_PAD - QK_NOPE - half), F32)],
                         axis=1)
    return ct, s1, s2


def _group_mask(rows, cols, g):
    return (jnp.arange(rows)[:, None] // (rows // g)) == (jnp.arange(cols)[None, :] // (cols // g))


def _block_diag_rows(m, g):
    return jnp.where(_group_mask(g * m.shape[0], m.shape[1], g), jnp.tile(m, (g, 1)), 0)


def _block_diag_cols(m, g):
    return jnp.where(_group_mask(m.shape[0], g * m.shape[1], g), jnp.tile(m, (1, g)), 0)


def _diag_blocks(m, g):
    a, b = m.shape[0] // g, m.shape[1] // g
    masked = jnp.where(_group_mask(m.shape[0], m.shape[1], g), m, 0)
    return masked.reshape(m.shape[0], g, b).sum(axis=1).reshape(g, a, b)


def _scan_tables(pwr, pwi, reverse):
    row = jnp.arange(SUBLANES)[:, None]
    zero = jnp.zeros_like(pwr)
    levels = []
    for d in SCAN_LEVELS:
        keep = (row < SUBLANES - d) if reverse else (row >= d)
        levels.append(jnp.stack([jnp.where(keep, pwr[d - 1:d, :], zero), jnp.where(keep, pwi[d - 1:d, :], zero)]))
    carry = jnp.stack([pwr[::-1], pwi[::-1]]) if reverse else jnp.stack([pwr, pwi])
    return jnp.stack(levels), carry


def _step(inp):
    x, c, ctx = inp['x'][0], inp['c'], inp['ctx'][0]
    target = inp['loss_target'][0]
    n, d = x.shape
    nc = ctx.shape[0]
    l = n + nc
    q_w, kv_w = inp['q_a_g'].shape[1], inp['kv_a_g'].shape[1]
    sw = inp['d_skip'].shape[1]
    n_grp = sw // SSM_GROUP
    gn = n_grp * SSM_STATE
    lay = _in_layout(d, q_w, kv_w, sw)
    tm = _tile(math.gcd(n, nc), (256, 128))
    me = 4 * lax.axis_index("x") + 2 * lax.axis_index("y") + lax.axis_index("c")
    strip = lambda a: a if a.ndim <= 2 else a[0]
    w = {k: strip(inp[k]) for k in WEIGHT_NAMES}

    gathered_names = list(GATHERED)
    early_names = ['w_in', 'w_uq', 'w_ukv', 'w_o_attn']
    late_names = [k for k in gathered_names if k not in early_names]
    conv_b = w['conv_b']

    (c_all,) = exchange([jnp.broadcast_to(c, (SUBLANES, d))], ['gather'], "gather_c")
    cmat = jnp.concatenate([c_all[:, 0, :], w['c_ctx'][None, :], jnp.zeros((SUBLANES - 1, d), F32)], axis=0)
    mcols = w['w_mod'].shape[1]
    b_cols = lax.dynamic_slice(w['b_mod'], (0, me * mcols), (1, mcols))
    mod_part = mod_fwd(cmat, w['w_mod'], b_cols, "mod_fwd")
    (mod_all,) = exchange([mod_part], ['gather'], "gather_mod")
    mod_me = lax.dynamic_index_in_dim(mod_all, me, axis=1, keepdims=False).reshape(6, d)
    mod_ctx = mod_all[:, SUBLANES, :].reshape(6, d)
    sh1, sc1, g1, sh2, sc2, g2 = [mod_me[k:k + 1] for k in range(6)]
    mods1 = jnp.concatenate([sc1, sh1, mod_ctx[1:2], mod_ctx[0:1], jnp.zeros((4, d), F32)], axis=0)
    mods2 = jnp.concatenate([sc2, sh2, jnp.zeros((6, d), F32)], axis=0)

    xa = jnp.concatenate([x, ctx], axis=0)
    h, *wg = norm_mod_fwd(xa, w['norm1_g'], mods1, n, tm, "norm1_fwd", gather=[w[k].astype(BF16) for k in early_names])
    full = {k: _from_shards(s, GATHERED[k]) for k, s in zip(early_names, wg)}
    w_in_p = _pad_w_in(full['w_in'], lay)
    wuq_p = _pad_heads(full['w_uq'], QK_DIM)
    ukv = full['w_ukv'].reshape(kv_w, N_HEADS, QK_NOPE + V_DIM)
    wk_p = _pad_heads(ukv[:, :, :QK_NOPE].reshape(kv_w, -1), QK_NOPE)
    wv_p = _pad_heads(ukv[:, :, QK_NOPE:].reshape(kv_w, -1), V_DIM)
    wo_p = _pad_heads(full['w_o_attn'].T, V_DIM).T
    proj = matmul(h, w_in_p, 'nn', F32, "in_proj")
    tabs = _rope_tables(n, nc)
    pad_g = lambda g: jnp.pad(g, ((0, 0), (0, HEAD_PAD - QK_DIM)))
    gains = (w['q_a_g'], w['kv_a_g'], pad_g(w['q_norm_g']), pad_g(w['k_norm_g']))
    q, k, v = qkv_fwd(proj, lay, gains, wuq_p, wk_p, wv_p, tabs, tm, "qkv_fwd")
    o, lse, *wg = attn_fwd(q, k, v, n, "attn_fwd", gather=[w[k].astype(BF16) for k in late_names])
    full.update({k: _from_shards(s, GATHERED[k]) for k, s in zip(late_names, wg)})
    w_glu, w_out, w_up, w_down = full['w_glu'], full['w_out'], full['w_up'], full['w_down']
    conv_w = jnp.pad(full['conv_w'].astype(F32), ((0, SUBLANES - 3), (0, 0)))

    b_t = lambda a: a.transpose(2, 0, 1).reshape(SSM_GROUP, gn)
    c_t = lambda a: a.transpose(1, 0, 2).reshape(SSM_GROUP, gn)
    bre_t, bim_t = b_t(w['b_re']), b_t(w['b_im'])
    ssm_in, prep = [], []
    for sfx in ('f', 'b'):
        lam_re, lam_im = w['lam_re_' + sfx].reshape(1, gn), w['lam_im_' + sfx].reshape(1, gn)
        logdt = jnp.repeat(w['log_dt_' + sfx], SSM_STATE, axis=1)
        ssm_in.append((lam_re, lam_im, logdt))
        prep.append(ssm_prep(lam_re, lam_im, logdt, bre_t, bim_t, "ssm_prep_" + sfx))

    def blk_b(bb):
        return _block_diag_rows(bb, n_grp).astype(BF16)

    def blk_c(cc):
        return _block_diag_cols(cc.transpose(0, 2, 1).reshape(gn, SSM_GROUP), n_grp).astype(BF16)

    t_len = _tile(math.gcd(n, nc), (256, 128))
    ssm = []
    for di, sfx in enumerate(('f', 'b')):
        reverse = di == 1
        pwr, pwi, bbr, bbi = prep[di]
        ssm.append(dict(
            sfx=sfx, reverse=reverse, blocks=(blk_b(bbr), blk_b(bbi), blk_c(w['c_re_' + sfx]),
                                              blk_c(-w['c_im_' + sfx])),
            tabs=_scan_tables(pwr, pwi, reverse), adj_tabs=_scan_tables(pwr, -pwi, not reverse)))
    for s in ssm:
        s['y'], s['xb'] = ssm_fwd(proj, lay, *s['blocks'], s['tabs'], n, t_len, s['reverse'], "ssm_fwd_" + s['sfx'])
    yf, yb = ssm[0]['y'], ssm[1]['y']
    s_l = glu_fwd(proj, lay, yf, yb, w['d_skip'], w_glu, n, tm, "glu_fwd")
    x1 = merge_fwd(o, s_l, proj, xa, g1, wo_p, w_out, n, tm, "merge_fwd")

    h2 = norm_mod_fwd(x1, w['norm2_g'], mods2, n, tm, "norm2_fwd")
    up = matmul(h2, w_up, 'nn', F32, "up_proj")
    tw = _tile(n, (256, 128))
    act = conv_act_fwd(up, conv_w, conv_b, tw, "conv_act_fwd")
    dy, ffn, loss_parts = down_loss(act, w_down, x1, g2, target, tm, "down_loss")
    loss = lax.psum(jnp.sum(loss_parts[:, 0, 0]), MESH_AXES)

    dup, dffn, dg2, dconv = ffn_bwd(dy, ffn, up, conv_w, conv_b, w_down, g2, tm, "ffn_bwd")
    g_w_down = matmul(act, dffn, 'tn', F32, "dw_down")
    dh2 = matmul(dup, w_up, 'nt', F32, "dh2")
    g_w_up = matmul(h2, dup, 'tn', F32, "dw_up")
    dx1, dn2g, dmods2 = norm_mod_bwd(x1, dh2, dy, w['norm2_g'], mods2, n, tm, "norm2_bwd")

    do, ds_l, dgl, dg1, g_wo_p, g_w_out = merge_bwd(o, s_l, proj, xa, g1, wo_p, w_out, dx1, n, tm, "merge_bwd")
    du_direct, dyr, dds, g_w_glu = glu_bwd(proj, lay, yf, yb, w['d_skip'], w_glu, ds_l, n, tm, "glu_bwd")
    for s in ssm:
        s['du'], s['gr'], s['gi'], s['xr'], s['xi'], s['dlam'] = ssm_bwd(
            proj, lay, dyr, s['xb'], *s['blocks'], s['tabs'], s['adj_tabs'], n, t_len, s['reverse'],
            "ssm_bwd_" + s['sfx'])

    early_g = {'w_o_attn': _unpad_heads(g_wo_p.T, V_DIM).T, 'w_glu': g_w_glu, 'w_out': g_w_out, 'w_up': g_w_up,
               'conv_w': dconv[0:3], 'w_down': g_w_down}
    dq, dk, dv, *early_recv = attn_bwd(q, k, v, o, do, lse, n, "attn_bwd",
                                       a2a=[_to_shards(g, GATHERED[k]).astype(BF16) for k, g in early_g.items()])
    (dproj, dqag, dkvag, dqng, dkng, g_wuq_p, g_wk_p, g_wv_p) = qkv_bwd(
        proj, lay, gains, wuq_p, wk_p, wv_p, tabs, dq, dk, dv, dgl, du_direct, ssm[0]['du'], ssm[1]['du'], n, tm,
        "qkv_bwd")
    g_w_in_p = matmul(h, dproj, 'tn', F32, "dw_in")
    ukv_g = jnp.concatenate([_unpad_heads(g_wk_p, QK_NOPE).reshape(kv_w, N_HEADS, QK_NOPE),
                             _unpad_heads(g_wv_p, V_DIM).reshape(kv_w, N_HEADS, V_DIM)], axis=2)
    late_g = {'w_in': _unpad_w_in(g_w_in_p, lay), 'w_uq': _unpad_heads(g_wuq_p, QK_DIM),
              'w_ukv': ukv_g.reshape(kv_w, -1)}
    dh, *late_recv = matmul(dproj, w_in_p, 'nt', F32, "dh",
                            a2a=[_to_shards(g, GATHERED[k]).astype(BF16) for k, g in late_g.items()])
    dxa, dn1g, dmods1 = norm_mod_bwd(xa, dh, dx1, w['norm1_g'], mods1, n, tm, "norm1_bwd")
    grad_x = dxa[:n]

    grads = {}
    d_bbar = [None, None]
    gp = n_grp // SSM_SPLIT
    for di, s in enumerate(ssm):
        sfx = s['sfx']
        products = (diag_outer(proj, lay['o_u'], sw, [s['gr'], s['gi']], SSM_SPLIT, "ssm_db_" + sfx)
                    + diag_outer(dyr, 0, sw, [s['xr'], s['xi']], SSM_SPLIT, "ssm_dc_" + sfx))
        g_bre, g_bim, g_cre, g_cim = [
            jnp.concatenate([_diag_blocks(m[p], gp) for p in range(SSM_SPLIT)], axis=0) for m in products]
        grads['c_re_' + sfx] = g_cre
        grads['c_im_' + sfx] = -g_cim
        to_t = lambda a: a.transpose(1, 0, 2).reshape(SSM_GROUP, gn)
        dlam_re = jnp.sum(s['dlam'][0], axis=0, keepdims=True)
        dlam_im = jnp.sum(s['dlam'][1], axis=0, keepdims=True)
        lam_re, lam_im, logdt = ssm_in[di]
        g_lr, g_li, g_dt, g_br, g_bi = ssm_prep_bwd(lam_re, lam_im, logdt, bre_t, bim_t, dlam_re, dlam_im,
                                                    to_t(g_bre), to_t(g_bim), "ssm_prep_bwd_" + sfx)
        grads['lam_re_' + sfx] = g_lr.reshape(n_grp, SSM_STATE)
        grads['lam_im_' + sfx] = g_li.reshape(n_grp, SSM_STATE)
        grads['log_dt_' + sfx] = jnp.sum(g_dt.reshape(n_grp, SSM_STATE), axis=1)[None, :]
        d_bbar[di] = (g_br, g_bi)
    from_t = lambda a: a.reshape(SSM_GROUP, n_grp, SSM_STATE).transpose(1, 2, 0)
    grads['b_re'] = from_t(d_bbar[0][0]) + from_t(d_bbar[1][0])
    grads['b_im'] = from_t(d_bbar[0][1]) + from_t(d_bbar[1][1])

    dmod = jnp.concatenate([dmods1[1:2], dmods1[0:1], dg1[0:1], dmods2[1:2], dmods2[0:1], dg2[0:1]], axis=1)
    dmod_ctx = jnp.concatenate([dmods1[3:4], dmods1[2:3], jnp.zeros((1, 4 * d), F32)], axis=1)
    dm_send = jnp.concatenate([dmod, dmod_ctx, jnp.zeros((SUBLANES - 2, 6 * d), F32)], axis=0)
    (dm_all,) = exchange([dm_send], ['gather'], "gather_dmod")
    g_all = jnp.concatenate([dm_all[:, 0, :], dm_all[:, 1, :]], axis=0)
    cols = lax.dynamic_slice(g_all.reshape(2 * N_DEV, N_DEV, mcols), (0, me, 0), (2 * N_DEV, 1, mcols))[:, 0, :]
    g_w_mod, dcmat, g_b_mod = mod_bwd(cmat, w['w_mod'], cols[:N_DEV], cols[N_DEV:], g_all, "mod_bwd")

    grads.update({'c_ctx': dcmat[SUBLANES], 'b_mod': g_b_mod[0:1], 'norm1_g': dn1g[0:1], 'norm2_g': dn2g[0:1],
                  'q_a_g': dqag[0:1], 'kv_a_g': dkvag[0:1], 'q_norm_g': dqng[0:1, :QK_DIM],
                  'k_norm_g': dkng[0:1, :QK_DIM], 'd_skip': dds[0:1], 'conv_b': dconv[3:4]})
    first = (me == 0).astype(F32)
    rep_parts = [grads[k] * first if k == 'b_mod' else grads[k] for k in REPLICATED]
    rpack, rspans = _pack(rep_parts, SUBLANES)
    my_core = lax.axis_index("c").astype(jnp.int32).reshape(1)
    r_both = jnp.broadcast_to(rpack[None, None], (2, 1) + rpack.shape)
    (from_sibling,) = exchange([r_both], ['others'], "grads_in_chip", group='core')
    r_sum = add_own(r_both, my_core, from_sibling[0], F32, "chip_sum_replicated")[0]
    (r_recv,) = exchange([r_sum], ['gather'], "grads_between_chips", group='chips')

    outs = {}

    def update(parts, k, w_k, m_k, v_k):
        res = reduce_adamw(parts, w_k, m_k, v_k, "adamw_" + k)
        return dict(zip(('grad_', 'delta_', 'new_m_', 'new_v_'), res))

    per_tensor = list(zip(early_g, early_recv)) + list(zip(late_g, late_recv)) + [('w_mod', g_w_mod[None])]
    for k, parts in per_tensor:
        for kind, a in update(parts, k, w[k], strip(inp['m_' + k]), strip(inp['v_' + k])).items():
            outs[kind + k] = a[None]
    rep = lambda prefix: _pack([strip(inp[prefix + k]) for k in REPLICATED], SUBLANES)[0]
    for kind, buf in update(r_recv, "replicated", rep(''), rep('m_'), rep('v_')).items():
        for k, a in zip(REPLICATED, _unpack(buf, rspans, [w[k].shape for k in REPLICATED])):
            outs[kind + k] = a if inp[k].ndim <= 2 else a[None]
    result = [loss, grad_x[None]]
    for kind in ('grad_', 'delta_', 'new_m_', 'new_v_'):
        result += [outs[kind + k] for k in WEIGHT_NAMES]
    return tuple(result)


_ARG_NAMES = (['x', 'c', 'ctx'] + WEIGHT_NAMES + ['loss_target'] + ['m_' + k for k in WEIGHT_NAMES]
              + ['v_' + k for k in WEIGHT_NAMES])


def kernel(*args):
    assert len(args) == len(_ARG_NAMES)
    return _step(dict(zip(_ARG_NAMES, args)))
```

```python
import functools
import math

import jax
import jax.numpy as jnp
from jax import lax
from jax.experimental import pallas as pl
from jax.experimental.pallas import tpu as pltpu

F32 = jnp.float32
BF16 = jnp.bfloat16

N_DEV = 8
MESH_AXES = ("x", "y", "c")
N_HEADS = 8
QK_NOPE = 64
QK_ROPE = 32
QK_DIM = QK_NOPE + QK_ROPE
V_DIM = 64
HEAD_PAD = 128
HEADS_W = N_HEADS * HEAD_PAD
GRID_W = 64
ROPE_THETA = 10000.0
SSM_GROUP = 16
SSM_STATE = 64
EPS = 1e-6
LANES = 128
SUBLANES = 8
PACK_W = 1024
VMEM_LIMIT = 56 * 1024 * 1024
MM_TILES = (1024, 768, 1408, 512, 384, 256, 128)

ADAM_LR = 0.001
ADAM_B1 = 0.9
ADAM_B2 = 0.999
ADAM_EPS = 1e-08
ADAM_WD = 0.01
ADAM_STEP = 10

WEIGHT_NAMES = ['c_ctx', 'w_mod', 'b_mod', 'norm1_g', 'norm2_g', 'w_in', 'q_a_g', 'w_uq', 'kv_a_g', 'w_ukv',
                'q_norm_g', 'k_norm_g', 'w_o_attn', 'lam_re_f', 'lam_im_f', 'log_dt_f', 'c_re_f', 'c_im_f',
                'lam_re_b', 'lam_im_b', 'log_dt_b', 'c_re_b', 'c_im_b', 'b_re', 'b_im', 'd_skip', 'w_glu',
                'w_out', 'w_up', 'conv_w', 'conv_b', 'w_down']
GATHERED = {'w_in': 1, 'w_uq': 1, 'w_ukv': 1, 'w_o_attn': 1, 'w_glu': 1, 'w_out': 0, 'w_up': 1, 'conv_w': 1,
            'w_down': 0}
REPLICATED = [n for n in WEIGHT_NAMES if n not in GATHERED and n != 'w_mod']


def _tile(n, prefs):
    for t in prefs:
        if n % t == 0:
            return t
    return n


def _cparams(sem=None):
    return pltpu.CompilerParams(dimension_semantics=sem, vmem_limit_bytes=VMEM_LIMIT)


@jax.custom_vjp
def bdot(a, w):
    return jnp.dot(a.astype(BF16), w.astype(BF16), preferred_element_type=F32)


def _bdot_fwd(a, w):
    return bdot(a, w), (a, w)


def _bdot_bwd(res, g):
    a, w = res
    gb = g.astype(BF16)
    da = lax.dot_general(gb, w.astype(BF16), (((1,), (1,)), ((), ())), preferred_element_type=F32)
    dw = lax.dot_general(a.astype(BF16), gb, (((0,), (0,)), ((), ())), preferred_element_type=F32)
    return da.astype(a.dtype), dw.astype(w.dtype)


bdot.defvjp(_bdot_fwd, _bdot_bwd)


def _dot_nt(a, b):
    return lax.dot_general(a, b, (((1,), (1,)), ((), ())), preferred_element_type=F32)


def _dot_tn(a, b):
    return lax.dot_general(a, b, (((0,), (0,)), ((), ())), preferred_element_type=F32)


def matmul(a, b, mode, out_dtype, name, a2a=()):
    n_x = len(a2a)
    kinds = ['a2a'] * n_x
    if mode == 'nn':
        (m, k), n = a.shape, b.shape[1]
    elif mode == 'nt':
        (m, k), n = a.shape, b.shape[0]
    else:
        (k, m), n = a.shape, b.shape[1]
    tm = _tile(m, MM_TILES)
    tn = _tile(n, MM_TILES)
    tk = _tile(k, MM_TILES)
    nk = k // tk

    def body(a_ref, b_ref, *rest):
        o_ref, acc_ref = rest[n_x], rest[2 * n_x + 1]
        kk = pl.program_id(2)
        if n_x:
            start, finish = _exchange_ops(rest[:n_x], rest[n_x + 1:2 * n_x + 1], *rest[2 * n_x + 2:], kinds, 'all')
            step = (pl.program_id(0) * (n // tn) + pl.program_id(1)) * nk + kk
            pl.when(step == 0)(start)

        @pl.when(kk == 0)
        def _():
            acc_ref[...] = jnp.zeros_like(acc_ref)

        av, bv = a_ref[...].astype(BF16), b_ref[...].astype(BF16)
        if mode == 'nn':
            acc_ref[...] += jnp.dot(av, bv, preferred_element_type=F32)
        elif mode == 'nt':
            acc_ref[...] += _dot_nt(av, bv)
        else:
            acc_ref[...] += _dot_tn(av, bv)

        @pl.when(kk == nk - 1)
        def _():
            o_ref[...] = acc_ref[...].astype(o_ref.dtype)

        if n_x:
            pl.when(step == (m // tm) * (n // tn) * nk - 1)(finish)

    if mode == 'nn':
        a_spec = pl.BlockSpec((tm, tk), lambda i, j, kk: (i, kk))
        b_spec = pl.BlockSpec((tk, tn), lambda i, j, kk: (kk, j))
    elif mode == 'nt':
        a_spec = pl.BlockSpec((tm, tk), lambda i, j, kk: (i, kk))
        b_spec = pl.BlockSpec((tn, tk), lambda i, j, kk: (j, kk))
    else:
        a_spec = pl.BlockSpec((tk, tm), lambda i, j, kk: (kk, i))
        b_spec = pl.BlockSpec((tk, tn), lambda i, j, kk: (kk, j))
    any_spec = pl.BlockSpec(memory_space=pl.ANY)
    res = pl.pallas_call(
        body, name=name, grid=(m // tm, n // tn, nk),
        in_specs=[a_spec, b_spec] + [any_spec] * n_x,
        out_specs=[pl.BlockSpec((tm, tn), lambda i, j, kk: (i, j))] + [any_spec] * n_x,
        out_shape=[jax.ShapeDtypeStruct((m, n), out_dtype)] + (_exchange_shapes(a2a, kinds, 'all') if n_x else []),
        scratch_shapes=[pltpu.VMEM((tm, tn), F32)] + (_exchange_scratch(n_x, 'all') if n_x else []),
        compiler_params=_cparams(("arbitrary",) * 3 if n_x else ("parallel", "parallel", "arbitrary")),
    )(a, b, *a2a)
    return res if n_x else res[0]


def diag_outer(a, a_col0, a_cols, bs, parts, name):
    k = a.shape[0]
    ka, kb = a_cols // parts, bs[0].shape[1] // parts
    tk = _tile(k, MM_TILES)
    nk = k // tk
    n_b = len(bs)
    assert a_col0 % ka == 0

    def body(a_ref, *rest):
        b_refs, o_refs, acc_refs = rest[:n_b], rest[n_b:2 * n_b], rest[2 * n_b:]
        kk = pl.program_id(1)
        av = a_ref[...].astype(BF16)
        for b_ref, o_ref, acc_ref in zip(b_refs, o_refs, acc_refs):
            @pl.when(kk == 0)
            def _():
                acc_ref[...] = jnp.zeros_like(acc_ref)

            acc_ref[...] += _dot_tn(av, b_ref[...].astype(BF16))

            @pl.when(kk == nk - 1)
            def _():
                o_ref[...] = acc_ref[...]

    return pl.pallas_call(
        body, name=name, grid=(parts, nk),
        in_specs=[pl.BlockSpec((tk, ka), lambda p, kk: (kk, a_col0 // ka + p))]
        + [pl.BlockSpec((tk, kb), lambda p, kk: (kk, p))] * n_b,
        out_specs=[pl.BlockSpec((None, ka, kb), lambda p, kk: (p, 0, 0))] * n_b,
        out_shape=[jax.ShapeDtypeStruct((parts, ka, kb), F32)] * n_b,
        scratch_shapes=[pltpu.VMEM((ka, kb), F32)] * n_b,
        compiler_params=_cparams(("parallel", "arbitrary")),
    )(a, *bs)


N_CHIPS = 4


def _group(group):
    x, y, c = lax.axis_index("x"), lax.axis_index("y"), lax.axis_index("c")
    flips = {'all': [(r & 4, r & 2, r & 1) for r in range(1, 8)],
             'chips': [(0, 1, 0), (1, 0, 0), (1, 1, 0)], 'core': [(0, 0, 1)]}[group]
    index = {'all': lambda px, py, pc: 4 * px + 2 * py + pc, 'chips': lambda px, py, pc: 2 * px + py,
             'core': lambda px, py, pc: pc}[group]
    peers = []
    for fx, fy, fc in flips:
        p = (1 - x if fx else x, 1 - y if fy else y, 1 - c if fc else c)
        peers.append((p, index(*p)))
    return len(flips) + 1, index(x, y, c), peers


def exchange(arrays, kinds, name, group='all'):
    n_arr = len(arrays)

    def body(*refs):
        start, finish = _exchange_ops(refs[:n_arr], refs[n_arr:2 * n_arr], *refs[2 * n_arr:], kinds, group,
                                      own_slot=False)
        start()
        finish()

    any_spec = pl.BlockSpec(memory_space=pl.ANY)
    outs = pl.pallas_call(
        body, name=name,
        in_specs=[any_spec] * n_arr, out_specs=[any_spec] * n_arr, out_shape=_exchange_shapes(arrays, kinds, group),
        scratch_shapes=_exchange_scratch(n_arr, group),
        compiler_params=pltpu.CompilerParams(has_side_effects=True),
    )(*arrays)
    _, me, _ = _group(group)
    own = {'gather': lambda arr: arr, 'a2a': lambda arr: lax.dynamic_index_in_dim(arr, me, 0, keepdims=False)}
    return [out if kind == 'others' else lax.dynamic_update_index_in_dim(out, own[kind](arr), me, 0)
            for arr, kind, out in zip(arrays, kinds, outs)]


GROUP_SIZE = {'all': N_DEV, 'chips': N_CHIPS, 'core': 2}


def _exchange_shapes(arrays, kinds, group):
    size = GROUP_SIZE[group]
    return [jax.ShapeDtypeStruct({'gather': (size,) + tuple(arr.shape), 'a2a': tuple(arr.shape),
                                  'others': (size - 1,) + tuple(arr.shape[1:])}[kind], arr.dtype)
            for arr, kind in zip(arrays, kinds)]


def _exchange_scratch(n_arr, group):
    size = GROUP_SIZE[group]
    return [pltpu.SemaphoreType.DMA((n_arr * size,)), pltpu.SemaphoreType.DMA((n_arr * size,)),
            pltpu.SemaphoreType.DMA((n_arr,))]


def _exchange_ops(srcs, dsts, send_sems, recv_sems, local_sems, kinds, group, own_slot=True):
    n_arr = len(srcs)
    size, me, peers = _group(group)
    has_local = [own_slot and kind != 'others' for kind in kinds]

    def copy(a, r, peer, peer_idx, receiving):
        src = srcs[a] if kinds[a] == 'gather' else srcs[a].at[peer_idx]
        if kinds[a] == 'others':
            dst = dsts[a].at[r]
        else:
            dst = dsts[a].at[peer_idx if receiving else me]
        return pltpu.make_async_remote_copy(
            src_ref=src, dst_ref=dst, send_sem=send_sems.at[a * size + r], recv_sem=recv_sems.at[a * size + r],
            device_id=peer, device_id_type=pl.DeviceIdType.MESH)

    def local(a):
        mine = srcs[a] if kinds[a] == 'gather' else srcs[a].at[me]
        return pltpu.make_async_copy(mine, dsts[a].at[me], local_sems.at[a])

    def start():
        for a in range(n_arr):
            if has_local[a]:
                local(a).start()
            for r, (peer, peer_idx) in enumerate(peers):
                copy(a, r, peer, peer_idx, False).start()

    def finish():
        for a in range(n_arr):
            for r, (peer, peer_idx) in enumerate(peers):
                cp = copy(a, r, peer, peer_idx, True)
                cp.wait_send()
                cp.wait_recv()
            if has_local[a]:
                local(a).wait()

    return start, finish


GATHER_COPIES = 7


def _two_level_gather(srcs, outs, send_sems, recv_sems, local_sems):
    n_arr = len(srcs)
    x, y, c = lax.axis_index("x"), lax.axis_index("y"), lax.axis_index("c")
    me, sibling = (x, y, c), (x, y, 1 - c)
    chips = [(1 - x, y), (x, 1 - y), (1 - x, 1 - y)]
    slot = lambda p: 4 * p[0] + 2 * p[1] + p[2]

    def copy(a, k, block, to, own=False):
        dst = outs[a].at[slot(block)]
        return pltpu.make_async_remote_copy(
            src_ref=srcs[a] if own else dst, dst_ref=dst,
            send_sem=send_sems.at[a * GATHER_COPIES + k], recv_sem=recv_sems.at[a * GATHER_COPIES + k],
            device_id=to, device_id_type=pl.DeviceIdType.MESH)

    def own_copies(a):
        return [copy(a, 0, me, sibling, own=True)] + [copy(a, 1 + j, me, (*chip, c), own=True)
                                                      for j, chip in enumerate(chips)]

    local = lambda a: pltpu.make_async_copy(srcs[a], outs[a].at[slot(me)], local_sems.at[a])

    def start():
        for a in range(n_arr):
            local(a).start()
            for cp in own_copies(a):
                cp.start()

    def finish():
        passed = []
        for j, chip in enumerate(chips):
            for a in range(n_arr):
                copy(a, 1 + j, (*chip, c), me).wait_recv()
                passed.append(copy(a, 4 + j, (*chip, c), sibling))
                passed[-1].start()
        for a in range(n_arr):
            copy(a, 0, sibling, me).wait_recv()
            for j, chip in enumerate(chips):
                copy(a, 4 + j, (*chip, 1 - c), me).wait_recv()
            for cp in own_copies(a):
                cp.wait_send()
        for cp in passed:
            cp.wait_send()
        for a in range(n_arr):
            local(a).wait()

    return start, finish


def _gather_scratch(n_arr):
    return [pltpu.SemaphoreType.DMA((n_arr * GATHER_COPIES,)), pltpu.SemaphoreType.DMA((n_arr * GATHER_COPIES,)),
            pltpu.SemaphoreType.DMA((n_arr,))]


def _pack(parts, row_mult):
    rows, spans, r = [], [], 0
    for p in parts:
        flat = p.reshape(-1)
        nr = -(-flat.shape[0] // (PACK_W * row_mult)) * row_mult
        flat = jnp.pad(flat, (0, nr * PACK_W - flat.shape[0]))
        rows.append(flat.reshape(nr, PACK_W))
        spans.append((r, nr))
        r += nr
    return jnp.concatenate(rows, axis=0), spans


def _unpack(buf, spans, shapes):
    out = []
    for (r, nr), shp in zip(spans, shapes):
        size = math.prod(shp)
        out.append(buf[..., r:r + nr, :].reshape(buf.shape[:-2] + (nr * PACK_W,))[..., :size]
                   .reshape(buf.shape[:-2] + tuple(shp)))
    return out


def _to_shards(full, axis):
    r, c = full.shape
    if axis == 0:
        return full.reshape(N_DEV, r // N_DEV, c)
    return full.reshape(r, N_DEV, c // N_DEV).transpose(1, 0, 2)


def _from_shards(sh, axis):
    _, r, c = sh.shape
    if axis == 0:
        return sh.reshape(N_DEV * r, c)
    return sh.transpose(1, 0, 2).reshape(r, N_DEV * c)


def _rms(x, g, n):
    ms = jnp.sum(x * x, axis=-1, keepdims=True) * (1.0 / n)
    return x * lax.rsqrt(ms + EPS) * g


def _norm_mod_fn(x, g, sc, sh):
    return _rms(x, g, x.shape[-1]) * (1.0 + sc) + sh


@jax.custom_vjp
def _rope(t, ct, s1, s2):
    return t * ct + pltpu.roll(t, 16, 1) * s1 + pltpu.roll(t, HEAD_PAD - 16, 1) * s2


def _rope_fwd(t, ct, s1, s2):
    return _rope(t, ct, s1, s2), (ct, s1, s2)


def _rope_bwd(res, d):
    ct, s1, s2 = res
    dt = d * ct + pltpu.roll(d * s1, HEAD_PAD - 16, 1) + pltpu.roll(d * s2, 16, 1)
    return dt, jnp.zeros_like(ct), jnp.zeros_like(s1), jnp.zeros_like(s2)


_rope.defvjp(_rope_fwd, _rope_bwd)


def _qkv_fn(cq, ckv, krsec, qag, kvag, qng, kng, wuq, wk, wv, ct, s1, s2):
    q_raw = bdot(_rms(cq, qag, cq.shape[-1]), wuq)
    ckvn = _rms(ckv, kvag, ckv.shape[-1])
    k_raw = bdot(ckvn, wk)
    v = bdot(ckvn, wv)
    qs, ks = [], []
    for h in range(N_HEADS):
        sl = slice(h * HEAD_PAD, (h + 1) * HEAD_PAD)
        qs.append(_rope(_rms(q_raw[:, sl], qng, QK_DIM), ct, s1, s2))
        ks.append(_rope(_rms(k_raw[:, sl] + krsec, kng, QK_DIM), ct, s1, s2))
    return jnp.concatenate(qs, axis=1), jnp.concatenate(ks, axis=1), v


def _glu_fn(u, yf, yb, dskip, wglu):
    y = u * dskip + yf + yb
    vg = bdot(jax.nn.gelu(y), wglu)
    d = vg.shape[-1] // 2
    return vg[:, :d] * jax.nn.sigmoid(vg[:, d:])


def _merge_fn(o, s_l, gl, x, g1, wo, wout):
    d = x.shape[-1]
    a = bdot(o, wo)
    mix = jax.nn.sigmoid(gl[:, :d]) * a + jax.nn.sigmoid(gl[:, d:]) * s_l
    return x + g1 * bdot(mix, wout)


def _mod_fn(cmat, w):
    return bdot(jax.nn.silu(cmat), w)


def _ssm_prep_fn(lam_re, lam_im, logdt, bre, bim):
    dt = jnp.exp(logdt)
    ar, ai = lam_re * dt, lam_im * dt
    e = jnp.exp(ar)
    lbr, lbi = e * jnp.cos(ai), e * jnp.sin(ai)
    nr, ni = lbr - 1.0, lbi
    den = lam_re * lam_re + lam_im * lam_im
    qr = (nr * lam_re + ni * lam_im) / den
    qi = (ni * lam_re - nr * lam_im) / den
    return lbr, lbi, qr * bre - qi * bim, qr * bim + qi * bre


def _rows(tm, w, col=0):
    return pl.BlockSpec((tm, w), lambda i: (i, col))


def _full(shape):
    nd = len(shape)
    return pl.BlockSpec(tuple(shape), lambda i: (0,) * nd)


def _acc_add(i, ref, val):
    @pl.when(i == 0)
    def _():
        ref[...] = jnp.zeros_like(ref)
    ref[...] += val


def _pad_rows(v, rows=SUBLANES):
    sel = lax.broadcasted_iota(jnp.int32, (rows, v.shape[-1]), 0) == 0
    return jnp.where(sel, jnp.broadcast_to(v, (rows, v.shape[-1])), 0.0)


def norm_mod_fwd(xa, g, mods, n_lat, tm, name, gather=()):
    r, d = xa.shape
    lat_tiles = n_lat // tm
    n_tiles = r // tm
    n_g = len(gather)

    def body(x_ref, g_ref, m_ref, *rest):
        o_ref = rest[n_g]
        i = pl.program_id(0)
        if n_g:
            start, finish = _two_level_gather(rest[:n_g], rest[n_g + 1:2 * n_g + 1], *rest[2 * n_g + 1:])
            pl.when(i == 0)(start)
        lat = i < lat_tiles
        sc = jnp.where(lat, m_ref[0:1, :], m_ref[2:3, :])
        sh = jnp.where(lat, m_ref[1:2, :], m_ref[3:4, :])
        o_ref[...] = _norm_mod_fn(x_ref[...], g_ref[...], sc, sh).astype(o_ref.dtype)
        if n_g:
            pl.when(i == n_tiles - 1)(finish)

    any_spec = pl.BlockSpec(memory_space=pl.ANY)
    res = pl.pallas_call(
        body, name=name, grid=(n_tiles,),
        in_specs=[_rows(tm, d), _full(g.shape), _full(mods.shape)] + [any_spec] * n_g,
        out_specs=[_rows(tm, d)] + [any_spec] * n_g,
        out_shape=[jax.ShapeDtypeStruct((r, d), BF16)]
        + [jax.ShapeDtypeStruct((N_DEV,) + tuple(a.shape), a.dtype) for a in gather],
        scratch_shapes=_gather_scratch(n_g) if n_g else [],
        compiler_params=_cparams(("arbitrary",) if n_g else ("parallel",)),
    )(xa, g, mods, *gather)
    return res if n_g else res[0]


def norm_mod_bwd(xa, dh, dres, g, mods, n_lat, tm, name):
    r, d = xa.shape
    lat_tiles = n_lat // tm

    def body(x_ref, dh_ref, dres_ref, g_ref, m_ref, dx_ref, dg_ref, dm_ref):
        i = pl.program_id(0)
        lat = i < lat_tiles
        sc = jnp.where(lat, m_ref[0:1, :], m_ref[2:3, :])
        sh = jnp.where(lat, m_ref[1:2, :], m_ref[3:4, :])
        _, vjp = jax.vjp(_norm_mod_fn, x_ref[...], g_ref[...], sc, sh)
        dx, dg, dsc, dsh = vjp(dh_ref[...])
        dx_ref[...] = dx + jnp.where(lat, dres_ref[...], 0.0)
        _acc_add(i, dg_ref, _pad_rows(dg))
        row = lax.broadcasted_iota(jnp.int32, (SUBLANES, d), 0)
        base = jnp.where(lat, 0, 2)
        upd = jnp.where(row == base, jnp.broadcast_to(dsc, (SUBLANES, d)), 0.0)
        upd = upd + jnp.where(row == base + 1, jnp.broadcast_to(dsh, (SUBLANES, d)), 0.0)
        _acc_add(i, dm_ref, upd)

    return pl.pallas_call(
        body, name=name, grid=(r // tm,),
        in_specs=[_rows(tm, d), _rows(tm, d),
                  pl.BlockSpec((tm, d), lambda i: (jnp.minimum(i, lat_tiles - 1), 0)),
                  _full(g.shape), _full(mods.shape)],
        out_specs=[_rows(tm, d), _full((SUBLANES, d)), _full((SUBLANES, d))],
        out_shape=[jax.ShapeDtypeStruct((r, d), F32), jax.ShapeDtypeStruct((SUBLANES, d), F32),
                   jax.ShapeDtypeStruct((SUBLANES, d), F32)],
        compiler_params=_cparams(("arbitrary",)),
    )(xa, dh, dres, g, mods)


def qkv_fwd(proj, lay, gains, wuq, wk, wv, tabs, tm, name):
    r = proj.shape[0]
    q_w, kv_w = lay['q'], lay['kv']

    def body(cq_ref, ckv_ref, kr_ref, qag, kvag, qng, kng, wuq_ref, wk_ref, wv_ref, ct, s1, s2, q_ref, k_ref,
             v_ref):
        q, k, v = _qkv_fn(cq_ref[...], ckv_ref[...], kr_ref[...], qag[...], kvag[...], qng[...], kng[...],
                          wuq_ref[...], wk_ref[...], wv_ref[...], ct[...], s1[...], s2[...])
        q_ref[...] = q.astype(BF16)
        k_ref[...] = k.astype(BF16)
        v_ref[...] = v.astype(BF16)

    out = jax.ShapeDtypeStruct((r, HEADS_W), BF16)
    return pl.pallas_call(
        body, name=name, grid=(r // tm,),
        in_specs=[_rows(tm, q_w, lay['o_cq'] // q_w), _rows(tm, kv_w, lay['o_ckv'] // kv_w),
                  _rows(tm, LANES, lay['o_kr'] // LANES)]
        + [_full(a.shape) for a in gains] + [_full(wuq.shape), _full(wk.shape), _full(wv.shape)]
        + [_rows(tm, HEAD_PAD)] * 3,
        out_specs=[_rows(tm, HEADS_W)] * 3, out_shape=[out, out, out],
        compiler_params=_cparams(("parallel",)),
    )(proj, proj, proj, *gains, wuq, wk, wv, *tabs)


def qkv_bwd(proj, lay, gains, wuq, wk, wv, tabs, dq, dk, dv, dgl, du_direct, du_f, du_b, n, tm, name):
    r = proj.shape[0]
    q_w, kv_w, sw, d = lay['q'], lay['kv'], lay['sw'], lay['d']
    lat_tiles = n // tm

    def body(cq_ref, ckv_ref, kr_ref, qag, kvag, qng, kng, wuq_ref, wk_ref, wv_ref, ct, s1, s2, dq_ref, dk_ref,
             dv_ref, dgl_ref, dud_ref, duf_ref, dub_ref, dp_ref, dqag, dkvag, dqng, dkng, dwuq, dwk, dwv):
        i = pl.program_id(0)
        lat = i < lat_tiles
        tables = (ct[...], s1[...], s2[...])
        fn = lambda *a: _qkv_fn(*a, *tables)
        _, vjp = jax.vjp(fn, cq_ref[...], ckv_ref[...], kr_ref[...], qag[...], kvag[...], qng[...], kng[...],
                         wuq_ref[...].astype(F32), wk_ref[...].astype(F32), wv_ref[...].astype(F32))
        g = vjp((jnp.where(lat, dq_ref[...], 0.0), dk_ref[...], dv_ref[...]))
        dp_ref[:, 0:2 * d] = jnp.where(lat, dgl_ref[...], 0.0).astype(BF16)
        dp_ref[:, lay['o_u']:lay['o_u'] + sw] = (duf_ref[...] + dub_ref[...]
                                                 + jnp.where(lat, dud_ref[...], 0.0)).astype(BF16)
        dp_ref[:, lay['o_ckv']:lay['o_ckv'] + kv_w] = g[1].astype(BF16)
        dp_ref[:, lay['o_kr']:lay['o_kr'] + LANES] = g[2].astype(BF16)
        hole0 = lay['o_kr'] + LANES
        if lay['o_cq'] > hole0:
            dp_ref[:, hole0:lay['o_cq']] = jnp.zeros((tm, lay['o_cq'] - hole0), BF16)
        dp_ref[:, lay['o_cq']:lay['o_cq'] + q_w] = g[0].astype(BF16)
        for ref, val in zip((dqag, dkvag, dqng, dkng), g[3:7]):
            _acc_add(i, ref, _pad_rows(val))
        for ref, val in zip((dwuq, dwk, dwv), g[7:10]):
            _acc_add(i, ref, val)

    def lat_rows(w):
        return pl.BlockSpec((tm, w), lambda i: (jnp.minimum(i, lat_tiles - 1), 0))

    acc_shapes = [(SUBLANES, a.shape[1]) for a in gains] + [wuq.shape, wk.shape, wv.shape]
    return pl.pallas_call(
        body, name=name, grid=(r // tm,),
        in_specs=[_rows(tm, q_w, lay['o_cq'] // q_w), _rows(tm, kv_w, lay['o_ckv'] // kv_w),
                  _rows(tm, LANES, lay['o_kr'] // LANES)]
        + [_full(a.shape) for a in gains] + [_full(wuq.shape), _full(wk.shape), _full(wv.shape)]
        + [_rows(tm, HEAD_PAD)] * 3 + [lat_rows(HEADS_W), _rows(tm, HEADS_W), _rows(tm, HEADS_W)]
        + [lat_rows(2 * d), lat_rows(sw), _rows(tm, sw), _rows(tm, sw)],
        out_specs=[_rows(tm, lay['width'])] + [_full(s) for s in acc_shapes],
        out_shape=[jax.ShapeDtypeStruct((r, lay['width']), BF16)] + [jax.ShapeDtypeStruct(s, F32) for s in acc_shapes],
        compiler_params=_cparams(("arbitrary",)),
    )(proj, proj, proj, *gains, wuq, wk, wv, *tabs, dq, dk, dv, dgl, du_direct, du_f, du_b)


def glu_fwd(proj, lay, yf, yb, dskip, wglu, n, tm, name):
    sw, d = wglu.shape[0], wglu.shape[1] // 2

    def body(u_ref, yf_ref, yb_ref, ds_ref, w_ref, o_ref):
        o_ref[...] = _glu_fn(u_ref[...], yf_ref[...], yb_ref[...], ds_ref[...], w_ref[...])

    return pl.pallas_call(
        body, name=name, grid=(n // tm,),
        in_specs=[_rows(tm, sw, lay['o_u'] // sw), _rows(tm, sw), _rows(tm, sw), _full(dskip.shape),
                  _full(wglu.shape)],
        out_specs=_rows(tm, d), out_shape=jax.ShapeDtypeStruct((n, d), F32),
        compiler_params=_cparams(("parallel",)),
    )(proj, yf, yb, dskip, wglu)


def glu_bwd(proj, lay, yf, yb, dskip, wglu, ds_l, n, tm, name):
    sw, d = wglu.shape[0], wglu.shape[1] // 2

    def body(u_ref, yf_ref, yb_ref, ds_ref, w_ref, g_ref, du_ref, dy_ref, dds_ref, dw_ref):
        i = pl.program_id(0)
        _, vjp = jax.vjp(_glu_fn, u_ref[...], yf_ref[...], yb_ref[...], ds_ref[...], w_ref[...].astype(F32))
        du, dyf, _, dds, dw = vjp(g_ref[...])
        du_ref[...] = du
        dy_ref[...] = dyf
        _acc_add(i, dds_ref, _pad_rows(dds))
        _acc_add(i, dw_ref, dw)

    return pl.pallas_call(
        body, name=name, grid=(n // tm,),
        in_specs=[_rows(tm, sw, lay['o_u'] // sw), _rows(tm, sw), _rows(tm, sw), _full(dskip.shape),
                  _full(wglu.shape), _rows(tm, d)],
        out_specs=[_rows(tm, sw), _rows(tm, sw), _full((SUBLANES, sw)), _full(wglu.shape)],
        out_shape=[jax.ShapeDtypeStruct((n, sw), F32), jax.ShapeDtypeStruct((n, sw), F32),
                   jax.ShapeDtypeStruct((SUBLANES, sw), F32), jax.ShapeDtypeStruct(wglu.shape, F32)],
        compiler_params=_cparams(("arbitrary",)),
    )(proj, yf, yb, dskip, wglu, ds_l)


def merge_fwd(o, s_l, proj, xa, g1, wo, wout, n, tm, name):
    d = xa.shape[1]

    def body(o_ref, s_ref, gl_ref, x_ref, g1_ref, wo_ref, wout_ref, x1_ref):
        x1_ref[...] = _merge_fn(o_ref[...], s_ref[...], gl_ref[...], x_ref[...], g1_ref[...], wo_ref[...],
                                wout_ref[...])

    return pl.pallas_call(
        body, name=name, grid=(n // tm,),
        in_specs=[_rows(tm, HEADS_W), _rows(tm, d), _rows(tm, 2 * d), _rows(tm, d), _full(g1.shape),
                  _full(wo.shape), _full(wout.shape)],
        out_specs=_rows(tm, d), out_shape=jax.ShapeDtypeStruct((n, d), F32),
        compiler_params=_cparams(("parallel",)),
    )(o, s_l, proj, xa, g1, wo, wout)


def merge_bwd(o, s_l, proj, xa, g1, wo, wout, dx1, n, tm, name):
    d = xa.shape[1]

    def body(o_ref, s_ref, gl_ref, x_ref, g1_ref, wo_ref, wout_ref, dx1_ref, do_ref, ds_ref, dgl_ref, dg1_ref,
             dwo_ref, dwout_ref):
        i = pl.program_id(0)
        _, vjp = jax.vjp(_merge_fn, o_ref[...], s_ref[...], gl_ref[...], x_ref[...], g1_ref[...],
                         wo_ref[...].astype(F32), wout_ref[...].astype(F32))
        do, ds, dgl, _, dg1, dwo, dwout = vjp(dx1_ref[...])
        do_ref[...] = do
        ds_ref[...] = ds
        dgl_ref[...] = dgl
        _acc_add(i, dg1_ref, _pad_rows(dg1))
        _acc_add(i, dwo_ref, dwo)
        _acc_add(i, dwout_ref, dwout)

    return pl.pallas_call(
        body, name=name, grid=(n // tm,),
        in_specs=[_rows(tm, HEADS_W), _rows(tm, d), _rows(tm, 2 * d), _rows(tm, d), _full(g1.shape),
                  _full(wo.shape), _full(wout.shape), _rows(tm, d)],
        out_specs=[_rows(tm, HEADS_W), _rows(tm, d), _rows(tm, 2 * d), _full((SUBLANES, d)), _full(wo.shape),
                   _full(wout.shape)],
        out_shape=[jax.ShapeDtypeStruct((n, HEADS_W), BF16), jax.ShapeDtypeStruct((n, d), F32),
                   jax.ShapeDtypeStruct((n, 2 * d), F32), jax.ShapeDtypeStruct((SUBLANES, d), F32),
                   jax.ShapeDtypeStruct(wo.shape, F32), jax.ShapeDtypeStruct(wout.shape, F32)],
        compiler_params=_cparams(("arbitrary",)),
    )(o, s_l, proj, xa, g1, wo, wout, dx1)


def _halo_specs(tm, w, n):
    nb = n // SUBLANES
    per = tm // SUBLANES
    prev = pl.BlockSpec((SUBLANES, w), lambda i: (jnp.maximum(i * per - 1, 0), 0))
    nxt = pl.BlockSpec((SUBLANES, w), lambda i: (jnp.minimum((i + 1) * per, nb - 1), 0))
    return prev, nxt


def _shifted(t, prev_blk, next_blk, i, n_tiles):
    tm = t.shape[0]
    row = lax.broadcasted_iota(jnp.int32, t.shape, 0)
    prev_row = jnp.where(i > 0, prev_blk[SUBLANES - 1:SUBLANES, :], 0.0)
    next_row = jnp.where(i < n_tiles - 1, next_blk[0:1, :], 0.0)
    before = jnp.where(row == 0, prev_row, pltpu.roll(t, 1, 0))
    after = jnp.where(row == tm - 1, next_row, pltpu.roll(t, tm - 1, 0))
    return before, after


def _conv_u2(up, before, after, cw, cb):
    return before * cw[0:1, :] + up * cw[1:2, :] + after * cw[2:3, :] + cb


def conv_act_fwd(up, cw, cb, tm, name):
    n, w2 = up.shape
    f = w2 // 2
    n_tiles = n // tm
    prev, nxt = _halo_specs(tm, w2, n)

    def body(up_ref, prev_ref, next_ref, cw_ref, cb_ref, act_ref):
        i = pl.program_id(0)
        t = up_ref[...]
        before, after = _shifted(t, prev_ref[...], next_ref[...], i, n_tiles)
        u2 = _conv_u2(t, before, after, cw_ref[...], cb_ref[...])
        act_ref[...] = (jax.nn.silu(u2[:, f:]) * u2[:, :f]).astype(BF16)

    return pl.pallas_call(
        body, name=name, grid=(n_tiles,),
        in_specs=[_rows(tm, w2), prev, nxt, _full(cw.shape), _full(cb.shape)],
        out_specs=_rows(tm, f), out_shape=jax.ShapeDtypeStruct((n, f), BF16),
        compiler_params=_cparams(("parallel",)),
    )(up, up, up, cw, cb)


def down_loss(act, wdown, x1, g2, target, tm, name):
    n, d = x1.shape
    f = act.shape[1]

    def body(act_ref, w_ref, x1_ref, g2_ref, t_ref, dy_ref, ffn_ref, loss_ref):
        ffn = jnp.dot(act_ref[...], w_ref[...], preferred_element_type=F32)
        err = x1_ref[...] + g2_ref[...] * ffn - t_ref[...]
        ffn_ref[...] = ffn
        dy_ref[...] = err * (1.0 / d)
        part = 0.5 * jnp.sum(jnp.sum(err * err, axis=-1, keepdims=True) * (1.0 / d), axis=0, keepdims=True)
        loss_ref[0] = jnp.broadcast_to(part, (SUBLANES, LANES))

    return pl.pallas_call(
        body, name=name, grid=(n // tm,),
        in_specs=[_rows(tm, f), _full(wdown.shape), _rows(tm, d), _full(g2.shape), _rows(tm, d)],
        out_specs=[_rows(tm, d), _rows(tm, d), pl.BlockSpec((1, SUBLANES, LANES), lambda i: (i, 0, 0))],
        out_shape=[jax.ShapeDtypeStruct((n, d), F32), jax.ShapeDtypeStruct((n, d), F32),
                   jax.ShapeDtypeStruct((n // tm, SUBLANES, LANES), F32)],
        compiler_params=_cparams(("parallel",)),
    )(act, wdown, x1, g2, target)


FFN_BWD_PARTS = 2


def ffn_bwd(dy, ffn, up, cw, cb, wdown, g2, tm, name):
    n, d = dy.shape
    w2 = up.shape[1]
    f = w2 // 2
    fc = f // FFN_BWD_PARTS
    assert fc % LANES == 0
    n_tiles = n // tm
    ext = tm + 2 * SUBLANES
    inner = slice(SUBLANES, SUBLANES + tm)
    prev_w, next_w = _halo_specs(tm, w2, n)
    prev_d, next_d = _halo_specs(tm, d, n)

    def body(dy_ref, dyp_ref, dyn_ref, ffn_ref, up_ref, upp_ref, upn_ref, cw_ref, cb_ref, w_ref, g2_ref,
             dup_ref, dffn_ref, dg2_ref, dcw_ref):
        i = pl.program_id(0)
        has_prev, has_next = i > 0, i < n_tiles - 1

        def extended(prev, tile, nxt):
            return jnp.concatenate([jnp.where(has_prev, prev, 0.0), tile, jnp.where(has_next, nxt, 0.0)], axis=0)

        dyv = dy_ref[...]
        dffn = extended(dyp_ref[...], dyv, dyn_ref[...]) * g2_ref[...]
        dffn_ref[...] = dffn[inner].astype(BF16)
        dffn = dffn.astype(BF16)
        _acc_add(i, dg2_ref, _pad_rows(jnp.sum(dyv * ffn_ref[...], axis=0, keepdims=True)))
        row = lax.broadcasted_iota(jnp.int32, (SUBLANES, fc), 0)
        shift = lambda t: (pltpu.roll(t, 1, 0), pltpu.roll(t, ext - 1, 0))
        for part in range(FFN_BWD_PARTS):
            halves = []
            for col0 in (part * fc, f + part * fc):
                cols = slice(col0, col0 + fc)
                t = extended(upp_ref[:, cols], up_ref[:, cols], upn_ref[:, cols])
                before, after = shift(t)
                halves.append((cols, t, _conv_u2(t, before, after, cw_ref[:, cols], cb_ref[:, cols])))
            (_, _, val), (_, _, gate) = halves
            dact = _dot_nt(dffn, w_ref[part * fc:(part + 1) * fc, :])
            sg = jax.nn.sigmoid(gate)
            for (cols, t, _), du2 in zip(halves, (dact * (gate * sg),
                                                   dact * val * (sg * (1.0 + gate * (1.0 - sg))))):
                before, after = shift(du2)
                cwv = cw_ref[:, cols]
                dup_ref[:, cols] = (after * cwv[0:1, :] + du2 * cwv[1:2, :] + before * cwv[2:3, :])[inner].astype(BF16)
                upd = jnp.zeros((SUBLANES, fc), F32)
                for k, term in enumerate((after * t, du2 * t, before * t, du2)):
                    upd = upd + jnp.where(row == k, jnp.broadcast_to(
                        jnp.sum(term[inner], axis=0, keepdims=True), (SUBLANES, fc)), 0.0)

                @pl.when(i == 0)
                def _():
                    dcw_ref[:, cols] = jnp.zeros((SUBLANES, fc), F32)

                dcw_ref[:, cols] += upd

    return pl.pallas_call(
        body, name=name, grid=(n_tiles,),
        in_specs=[_rows(tm, d), prev_d, next_d, _rows(tm, d), _rows(tm, w2), prev_w, next_w, _full(cw.shape),
                  _full(cb.shape), _full(wdown.shape), _full(g2.shape)],
        out_specs=[_rows(tm, w2), _rows(tm, d), _full((SUBLANES, d)), _full((SUBLANES, w2))],
        out_shape=[jax.ShapeDtypeStruct((n, w2), BF16), jax.ShapeDtypeStruct((n, d), BF16),
                   jax.ShapeDtypeStruct((SUBLANES, d), F32), jax.ShapeDtypeStruct((SUBLANES, w2), F32)],
        compiler_params=_cparams(("arbitrary",)),
    )(dy, dy, dy, ffn, up, up, up, cw, cb, wdown, g2)


def attn_fwd(q, k, v, n, name, gather=()):
    nk = k.shape[0]
    tq = _tile(n, (1024, 512, 256, 128))
    tk = _tile(nk, (768, 384, 256, 128))
    n_kv = nk // tk
    n_q = n // tq
    n_g = len(gather)
    scale = QK_DIM ** -0.5
    c2 = scale * math.log2(math.e)

    def body(q_ref, k_ref, v_ref, *rest):
        o_ref, lse_ref = rest[n_g:n_g + 2]
        if n_g:
            start, finish = _two_level_gather(rest[:n_g], rest[n_g + 2:2 * n_g + 2], *rest[2 * n_g + 2:])
            step = pl.program_id(0) * n_q + pl.program_id(1)
            pl.when(step == 0)(start)
        qv = q_ref[...]
        ones_col = (lax.broadcasted_iota(jnp.int32, (tk, HEAD_PAD), 1) == V_DIM).astype(BF16)

        def chunk(j, carry):
            m, acc = carry
            rows = pl.ds(pl.multiple_of(j * tk, tk), tk)
            s = _dot_nt(qv, k_ref[rows, :])
            m_new = jnp.maximum(m, jnp.max(s, axis=-1, keepdims=True))
            p = jnp.exp2(s * c2 - m_new * c2)
            alpha = jnp.exp2((m - m_new) * c2)
            pv = jnp.dot(p.astype(BF16), v_ref[rows, :] + ones_col, preferred_element_type=F32)
            return m_new, alpha * acc + pv

        m, acc = lax.fori_loop(0, n_kv, chunk, (jnp.full((tq, 1), -jnp.inf, F32),
                                                jnp.zeros((tq, HEAD_PAD), F32)), unroll=True)
        l = acc[:, V_DIM:V_DIM + 1]
        lane = lax.broadcasted_iota(jnp.int32, (tq, HEAD_PAD), 1)
        o_ref[...] = jnp.where(lane < V_DIM, acc / l, 0.0).astype(BF16)
        lse_ref[...] = jnp.broadcast_to(m * scale + jnp.log(l), (tq, HEAD_PAD))
        if n_g:
            pl.when(step == N_HEADS * n_q - 1)(finish)

    qspec = pl.BlockSpec((tq, HEAD_PAD), lambda h, i: (i, h))
    kspec = pl.BlockSpec((nk, HEAD_PAD), lambda h, i: (0, h))
    any_spec = pl.BlockSpec(memory_space=pl.ANY)
    return pl.pallas_call(
        body, name=name, grid=(N_HEADS, n_q),
        in_specs=[qspec, kspec, kspec] + [any_spec] * n_g, out_specs=[qspec, qspec] + [any_spec] * n_g,
        out_shape=[jax.ShapeDtypeStruct((n, HEADS_W), BF16), jax.ShapeDtypeStruct((n, HEADS_W), F32)]
        + [jax.ShapeDtypeStruct((N_DEV,) + tuple(a.shape), a.dtype) for a in gather],
        scratch_shapes=_gather_scratch(n_g) if n_g else [],
        compiler_params=_cparams(("arbitrary", "arbitrary") if n_g else ("parallel", "parallel")),
    )(q, k, v, *gather)


def attn_bwd(q, k, v, o, do, lse, n, name, a2a=()):
    nk = k.shape[0]
    tq = _tile(n, (1024, 512, 256, 128))
    tk = _tile(nk, (768, 384, 256, 128))
    n_kv = nk // tk
    n_x = len(a2a)
    kinds = ['a2a'] * n_x
    scale = QK_DIM ** -0.5
    log2e = math.log2(math.e)
    c2 = scale * log2e

    def body(q_ref, k_ref, v_ref, o_ref, do_ref, lse_ref, *rest):
        dq_ref, dk_ref, dv_ref = rest[n_x:n_x + 3]
        if n_x:
            start, finish = _exchange_ops(rest[:n_x], rest[n_x + 3:2 * n_x + 3], *rest[2 * n_x + 3:], kinds, 'all')
            step = pl.program_id(0) * n_kv + pl.program_id(1)
            pl.when(step == 0)(start)

        @pl.when(pl.program_id(1) == 0)
        def _():
            dq_ref[...] = jnp.zeros_like(dq_ref)

        kv, vv = k_ref[...], v_ref[...]

        def q_tile(i, carry):
            dk, dv = carry
            rows = pl.ds(pl.multiple_of(i * tq, tq), tq)
            qv, dov = q_ref[rows, :], do_ref[rows, :]
            p = jnp.exp2(_dot_nt(qv, kv) * c2 - lse_ref[rows, 0:1] * log2e)
            dv = dv + _dot_tn(p.astype(BF16), dov)
            dp = _dot_nt(dov, vv)
            delta = jnp.sum(dov.astype(F32) * o_ref[rows, :].astype(F32), axis=-1, keepdims=True)
            ds = (p * (dp - delta) * scale).astype(BF16)
            dk = dk + _dot_tn(ds, qv)
            dq_ref[rows, :] += jnp.dot(ds, kv, preferred_element_type=F32)
            return dk, dv

        zero = jnp.zeros((tk, HEAD_PAD), F32)
        dk, dv = lax.fori_loop(0, n // tq, q_tile, (zero, zero), unroll=True)
        dk_ref[...] = dk
        dv_ref[...] = dv
        if n_x:
            pl.when(step == N_HEADS * n_kv - 1)(finish)

    qspec = pl.BlockSpec((n, HEAD_PAD), lambda h, j: (0, h))
    kspec = pl.BlockSpec((tk, HEAD_PAD), lambda h, j: (j, h))
    any_spec = pl.BlockSpec(memory_space=pl.ANY)
    return pl.pallas_call(
        body, name=name, grid=(N_HEADS, n_kv),
        in_specs=[qspec, kspec, kspec, qspec, qspec, qspec] + [any_spec] * n_x,
        out_specs=[qspec, kspec, kspec] + [any_spec] * n_x,
        out_shape=[jax.ShapeDtypeStruct((n, HEADS_W), F32), jax.ShapeDtypeStruct((nk, HEADS_W), F32),
                   jax.ShapeDtypeStruct((nk, HEADS_W), F32)] + _exchange_shapes(a2a, kinds, 'all'),
        scratch_shapes=_exchange_scratch(n_x, 'all') if n_x else [],
        compiler_params=_cparams(("arbitrary", "arbitrary") if n_x else ("parallel", "arbitrary")),
    )(q, k, v, o, do, lse, *a2a)


SCAN_LEVELS = (1, 2, 4)
SCAN_LANES = 512


def _scan_chunk(xr, xi, car, m_ref, p_ref, reverse):
    t_len, gn = xr.shape
    n_slab = t_len // SUBLANES
    for lb in range(gn // SCAN_LANES):
        ls = pl.ds(lb * SCAN_LANES, SCAN_LANES)

        def step(s, carry, ls=ls):
            cr, ci = carry
            slab = (n_slab - 1 - s) if reverse else s
            rows = pl.ds(pl.multiple_of(slab * SUBLANES, SUBLANES), SUBLANES)
            br, bi = xr[rows, ls], xi[rows, ls]
            for lvl, d in enumerate(SCAN_LEVELS):
                shift = SUBLANES - d if reverse else d
                sr, si = pltpu.roll(br, shift, 0), pltpu.roll(bi, shift, 0)
                mr, mi = m_ref[lvl, 0, :, ls], m_ref[lvl, 1, :, ls]
                br, bi = br + mr * sr - mi * si, bi + mr * si + mi * sr
            pr, pi = p_ref[0, :, ls], p_ref[1, :, ls]
            br, bi = br + pr * cr - pi * ci, bi + pr * ci + pi * cr
            xr[rows, ls] = br
            xi[rows, ls] = bi
            last = 0 if reverse else SUBLANES - 1
            return br[last:last + 1, :], bi[last:last + 1, :]

        cr, ci = lax.fori_loop(0, n_slab, step, (car[0:1, ls], car[1:2, ls]))
        car[0:1, ls] = cr
        car[1:2, ls] = ci


SSM_SPLIT = 2


def _bd_dot(a, w_ref):
    k, n = w_ref.shape[0] // SSM_SPLIT, w_ref.shape[1] // SSM_SPLIT
    return jnp.concatenate([jnp.dot(a[:, p * k:(p + 1) * k], w_ref[p * k:(p + 1) * k, p * n:(p + 1) * n],
                                    preferred_element_type=F32) for p in range(SSM_SPLIT)], axis=1)


def _bd_dot_nt(a, w_ref):
    k, n = w_ref.shape[0] // SSM_SPLIT, w_ref.shape[1] // SSM_SPLIT
    return jnp.concatenate([_dot_nt(a[:, p * n:(p + 1) * n], w_ref[p * k:(p + 1) * k, p * n:(p + 1) * n])
                            for p in range(SSM_SPLIT)], axis=1)


def _seq_block(step, n_chunk, lat_chunks, reverse):
    if reverse:
        return n_chunk - 1 - step
    return (step + lat_chunks) % n_chunk


def ssm_fwd(proj, lay, bre, bim, cre, ncim, tabs, n, t_len, reverse, name):
    l, sw = proj.shape[0], lay['sw']
    gn = bre.shape[-1]
    n_chunk, lat_chunks = l // t_len, n // t_len
    blk = lambda k: _seq_block(k, n_chunk, lat_chunks, reverse)
    mtab, ptab = tabs

    def body(u_ref, bre_ref, bim_ref, cre_ref, ncim_ref, m_ref, p_ref, y_ref, xb_ref, xr, xi, car):
        @pl.when(pl.program_id(0) == 0)
        def _():
            car[...] = jnp.zeros_like(car)

        xb_ref[...] = car[...]
        u = u_ref[...].astype(BF16)
        xr[...] = _bd_dot(u, bre_ref)
        xi[...] = _bd_dot(u, bim_ref)
        _scan_chunk(xr, xi, car, m_ref, p_ref, reverse)
        y_ref[...] = _bd_dot(xr[...].astype(BF16), cre_ref) + _bd_dot(xi[...].astype(BF16), ncim_ref)

    return pl.pallas_call(
        body, name=name, grid=(n_chunk,),
        in_specs=[pl.BlockSpec((t_len, sw), lambda k: (blk(k), lay['o_u'] // sw)), _full(bre.shape),
                  _full(bim.shape), _full(cre.shape), _full(ncim.shape), _full(mtab.shape), _full(ptab.shape)],
        out_specs=[pl.BlockSpec((t_len, sw), lambda k: (blk(k), 0)),
                   pl.BlockSpec((None, 2, gn), lambda k: (k, 0, 0))],
        out_shape=[jax.ShapeDtypeStruct((l, sw), F32), jax.ShapeDtypeStruct((n_chunk, 2, gn), F32)],
        scratch_shapes=[pltpu.VMEM((t_len, gn), F32), pltpu.VMEM((t_len, gn), F32), pltpu.VMEM((2, gn), F32)],
        compiler_params=_cparams(("arbitrary",)),
    )(proj, bre, bim, cre, ncim, mtab, ptab)


def ssm_bwd(proj, lay, dyr, xb, bre, bim, cre, ncim, tabs, adj_tabs, n, t_len, reverse, name):
    l, sw = proj.shape[0], lay['sw']
    gn = bre.shape[-1]
    n_chunk, lat_chunks = l // t_len, n // t_len
    fwd_step = lambda k: n_chunk - 1 - k
    blk = lambda k: _seq_block(fwd_step(k), n_chunk, lat_chunks, reverse)
    (mtab, ptab), (mtab_r, ptab_r) = tabs, adj_tabs

    def body(u_ref, dy_ref, xb_ref, bre_ref, bim_ref, cre_ref, ncim_ref, m_ref, p_ref, mr_ref, pr_ref,
             du_ref, gr_ref, gi_ref, xr_ref, xi_ref, dlam_ref, xr, xi, gr, gi, car, acar):
        k = pl.program_id(0)

        @pl.when(k == 0)
        def _():
            acar[...] = jnp.zeros_like(acar)
            dlam_ref[...] = jnp.zeros_like(dlam_ref)

        u = u_ref[...].astype(BF16)
        dy = jnp.where(blk(k) < lat_chunks, dy_ref[...], 0.0).astype(BF16)
        xr[...] = _bd_dot(u, bre_ref)
        xi[...] = _bd_dot(u, bim_ref)
        car[...] = xb_ref[...]
        _scan_chunk(xr, xi, car, m_ref, p_ref, reverse)
        gr[...] = _bd_dot_nt(dy, cre_ref)
        gi[...] = _bd_dot_nt(dy, ncim_ref)
        _scan_chunk(gr, gi, acar, mr_ref, pr_ref, not reverse)
        xrv, xiv, grv, giv = xr[...], xi[...], gr[...], gi[...]
        row = lax.broadcasted_iota(jnp.int32, (t_len, gn), 0)
        first, shift = (t_len - 1, t_len - 1) if reverse else (0, 1)
        xpr = jnp.where(row == first, xb_ref[0:1, :], pltpu.roll(xrv, shift, 0))
        xpi = jnp.where(row == first, xb_ref[1:2, :], pltpu.roll(xiv, shift, 0))
        dlr = grv * xpr + giv * xpi
        dli = giv * xpr - grv * xpi
        dlam_ref[0] += jnp.sum(dlr.reshape(t_len // SUBLANES, SUBLANES, gn), axis=0)
        dlam_ref[1] += jnp.sum(dli.reshape(t_len // SUBLANES, SUBLANES, gn), axis=0)
        grb, gib = grv.astype(BF16), giv.astype(BF16)
        du_ref[...] = _bd_dot_nt(grb, bre_ref) + _bd_dot_nt(gib, bim_ref)
        gr_ref[...] = grb
        gi_ref[...] = gib
        xr_ref[...] = xrv.astype(BF16)
        xi_ref[...] = xiv.astype(BF16)

    def at_blk(width, col=0):
        return pl.BlockSpec((t_len, width), lambda k: (blk(k), col))

    state = jax.ShapeDtypeStruct((l, gn), BF16)
    return pl.pallas_call(
        body, name=name, grid=(n_chunk,),
        in_specs=[at_blk(sw, lay['o_u'] // sw),
                  pl.BlockSpec((t_len, sw), lambda k: (jnp.minimum(blk(k), lat_chunks - 1), 0)),
                  pl.BlockSpec((None, 2, gn), lambda k: (fwd_step(k), 0, 0)),
                  _full(bre.shape), _full(bim.shape), _full(cre.shape), _full(ncim.shape), _full(mtab.shape),
                  _full(ptab.shape), _full(mtab_r.shape), _full(ptab_r.shape)],
        out_specs=[at_blk(sw), at_blk(gn), at_blk(gn), at_blk(gn), at_blk(gn), _full((2, SUBLANES, gn))],
        out_shape=[jax.ShapeDtypeStruct((l, sw), F32), state, state, state, state,
                   jax.ShapeDtypeStruct((2, SUBLANES, gn), F32)],
        scratch_shapes=[pltpu.VMEM((t_len, gn), F32)] * 4 + [pltpu.VMEM((2, gn), F32)] * 2,
        compiler_params=_cparams(("arbitrary",)),
    )(proj, dyr, xb, bre, bim, cre, ncim, mtab, ptab, mtab_r, ptab_r)


def ssm_prep(lam_re, lam_im, logdt, bre, bim, name):
    gn = lam_re.shape[1]

    def body(lr_ref, li_ref, dt_ref, br_ref, bi_ref, pwr_ref, pwi_ref, bbr_ref, bbi_ref):
        _, _, bbr, bbi = _ssm_prep_fn(lr_ref[...], li_ref[...], dt_ref[...], br_ref[...], bi_ref[...])
        bbr_ref[...] = bbr
        bbi_ref[...] = bbi
        kk = (lax.broadcasted_iota(jnp.int32, (SUBLANES, gn), 0) + 1).astype(F32)
        dt = jnp.exp(dt_ref[...])
        ar, ai = lr_ref[...] * dt * kk, li_ref[...] * dt * kk
        e = jnp.exp(ar)
        pwr_ref[...] = e * jnp.cos(ai)
        pwi_ref[...] = e * jnp.sin(ai)

    ins = (lam_re, lam_im, logdt, bre, bim)
    return pl.pallas_call(
        body, name=name,
        out_shape=[jax.ShapeDtypeStruct((SUBLANES, gn), F32)] * 2 + [jax.ShapeDtypeStruct(bre.shape, F32)] * 2,
        compiler_params=_cparams(),
    )(*ins)


def ssm_prep_bwd(lam_re, lam_im, logdt, bre, bim, dlbr, dlbi, dbbr, dbbi, name):
    def body(lr_ref, li_ref, dt_ref, br_ref, bi_ref, g0, g1, g2, g3, o0, o1, o2, o3, o4):
        _, vjp = jax.vjp(_ssm_prep_fn, lr_ref[...], li_ref[...], dt_ref[...], br_ref[...], bi_ref[...])
        for ref, val in zip((o0, o1, o2, o3, o4), vjp((g0[...], g1[...], g2[...], g3[...]))):
            ref[...] = val

    ins = (lam_re, lam_im, logdt, bre, bim)
    return pl.pallas_call(
        body, name=name,
        out_shape=[jax.ShapeDtypeStruct(a.shape, F32) for a in ins],
        compiler_params=_cparams(),
    )(*ins, dlbr, dlbi, dbbr, dbbi)


def mod_fwd(cmat, w, b, name):
    def body(c_ref, w_ref, b_ref, o_ref):
        o_ref[...] = _mod_fn(c_ref[...], w_ref[...]) + b_ref[...]

    return pl.pallas_call(body, name=name, out_shape=jax.ShapeDtypeStruct((cmat.shape[0], w.shape[1]), F32),
                          compiler_params=_cparams())(cmat, w, b)


def mod_bwd(cmat, w, g_lat, g_ctx, g_all, name):
    def body(c_ref, w_ref, gl_ref, gc_ref, ga_ref, dw_ref, dc_ref, db_ref):
        gc = jnp.sum(gc_ref[...], axis=0, keepdims=True)
        dm = jnp.concatenate([gl_ref[...], _pad_rows(gc)], axis=0)
        _, vjp = jax.vjp(_mod_fn, c_ref[...], w_ref[...])
        dc, dw = vjp(dm)
        dw_ref[...] = dw
        dc_ref[...] = dc
        db_ref[...] = _pad_rows(jnp.sum(ga_ref[...], axis=0, keepdims=True))

    return pl.pallas_call(
        body, name=name,
        out_shape=[jax.ShapeDtypeStruct(w.shape, F32), jax.ShapeDtypeStruct(cmat.shape, F32),
                   jax.ShapeDtypeStruct((SUBLANES, g_all.shape[1]), F32)],
        compiler_params=_cparams(),
    )(cmat, w, g_lat, g_ctx, g_all)


def add_own(own, idx, recv, out_dtype, name):
    _, s, r, c = own.shape

    def body(idx_ref, own_ref, recv_ref, o_ref):
        o_ref[...] = (own_ref[...] + recv_ref[...]).astype(o_ref.dtype)

    return pl.pallas_call(
        body, name=name,
        grid_spec=pltpu.PrefetchScalarGridSpec(
            num_scalar_prefetch=1, grid=(s,),
            in_specs=[pl.BlockSpec((None, None, r, c), lambda k, idx_ref: (idx_ref[0], k, 0, 0)),
                      pl.BlockSpec((None, r, c), lambda k, idx_ref: (k, 0, 0))],
            out_specs=pl.BlockSpec((None, r, c), lambda k, idx_ref: (k, 0, 0))),
        out_shape=jax.ShapeDtypeStruct((s, r, c), out_dtype),
        compiler_params=_cparams(("parallel",)),
    )(idx, own, recv)


def reduce_adamw(parts, w, m, v, name):
    s, r, c = parts.shape
    tr = _tile(r, (256, 128, 64, 32, 16, 8))

    def body(p_ref, w_ref, m_ref, v_ref, g_ref, d_ref, nm_ref, nv_ref):
        g = p_ref[0].astype(F32)
        for k in range(1, s):
            g = g + p_ref[k].astype(F32)
        mm = ADAM_B1 * m_ref[...] + (1.0 - ADAM_B1) * g
        vv = ADAM_B2 * v_ref[...] + (1.0 - ADAM_B2) * jnp.square(g)
        m_hat = mm / (1.0 - ADAM_B1 ** ADAM_STEP)
        v_hat = vv / (1.0 - ADAM_B2 ** ADAM_STEP)
        g_ref[...] = g
        d_ref[...] = -ADAM_LR * (m_hat / (jnp.sqrt(v_hat) + ADAM_EPS) + ADAM_WD * w_ref[...])
        nm_ref[...] = mm
        nv_ref[...] = vv

    out = jax.ShapeDtypeStruct((r, c), F32)
    blk = _rows(tr, c)
    return pl.pallas_call(
        body, name=name, grid=(r // tr,),
        in_specs=[pl.BlockSpec((s, tr, c), lambda i: (0, i, 0)), blk, blk, blk],
        out_specs=[blk] * 4, out_shape=[out] * 4,
        compiler_params=_cparams(("parallel",)),
    )(parts, w, m, v)


def _in_layout(d, q, kv, sw):
    o_u = 2 * d
    o_ckv = o_u + sw
    o_kr = o_ckv + kv
    o_cq = -(-(o_kr + LANES) // q) * q
    assert o_u % sw == 0 and o_ckv % kv == 0 and o_kr % LANES == 0
    assert q % LANES == 0 and kv % LANES == 0 and sw % LANES == 0
    return dict(d=d, q=q, kv=kv, sw=sw, o_gl=0, o_u=o_u, o_ckv=o_ckv, o_kr=o_kr, o_cq=o_cq, width=o_cq + q)


def _pad_w_in(w_in, lay):
    q, kv, sw, d = lay['q'], lay['kv'], lay['sw'], lay['d']
    cq, ckv, kr, u, gl = jnp.split(w_in, [q, q + kv, q + kv + QK_ROPE, q + kv + QK_ROPE + sw], axis=1)
    z = lambda w: jnp.zeros((w_in.shape[0], w), w_in.dtype)
    hole = lay['o_cq'] - lay['o_kr'] - LANES
    return jnp.concatenate([gl, u, ckv, z(QK_NOPE), kr, z(LANES - QK_DIM), z(hole), cq], axis=1)


def _unpad_w_in(g, lay):
    q, kv, sw, d = lay['q'], lay['kv'], lay['sw'], lay['d']
    kr0 = lay['o_kr'] + QK_NOPE
    return jnp.concatenate([g[:, lay['o_cq']:lay['o_cq'] + q], g[:, lay['o_ckv']:lay['o_ckv'] + kv],
                            g[:, kr0:kr0 + QK_ROPE], g[:, lay['o_u']:lay['o_u'] + sw], g[:, :2 * d]], axis=1)


def _pad_heads(w, width):
    k = w.shape[0]
    return jnp.pad(w.reshape(k, N_HEADS, width), ((0, 0), (0, 0), (0, HEAD_PAD - width))).reshape(k, HEADS_W)


def _unpad_heads(w, width):
    k = w.shape[0]
    return w.reshape(k, N_HEADS, HEAD_PAD)[:, :, :width].reshape(k, N_HEADS * width)


def _rope_tables(n, nc):
    rows = n // GRID_W
    row = jnp.repeat(jnp.arange(rows), GRID_W)
    col = jnp.tile(jnp.arange(GRID_W), rows)
    pairs = QK_ROPE // 4
    freqs = ROPE_THETA ** (-jnp.arange(pairs, dtype=F32) / pairs)
    ang = jnp.concatenate([row[:, None] * freqs, col[:, None] * freqs], axis=-1)
    cos = jnp.concatenate([jnp.cos(ang), jnp.ones((nc, 2 * pairs), F32)], axis=0)
    sin = jnp.concatenate([jnp.sin(ang), jnp.zeros((nc, 2 * pairs), F32)], axis=0)
    l = n + nc
    half = QK_ROPE // 2
    ct = jnp.concatenate([jnp.ones((l, QK_NOPE), F32), cos, cos, jnp.zeros((l, HEAD_PAD - QK_DIM), F32)], axis=1)
    s1 = jnp.concatenate([jnp.zeros((l, QK_NOPE + half), F32), sin, jnp.zeros((l, HEAD_PAD - QK_DIM), F32)],
                         axis=1)
    s2 = jnp.concatenate([jnp.zeros((l, QK_NOPE), F32), -sin, jnp.zeros((l, HEAD_PAD - QK_NOPE - half), F32)],
                         axis=1)
    return ct, s1, s2


def _group_mask(rows, cols, g):
    return (jnp.arange(rows)[:, None] // (rows // g)) == (jnp.arange(cols)[None, :] // (cols // g))


def _block_diag_rows(m, g):
    return jnp.where(_group_mask(g * m.shape[0], m.shape[1], g), jnp.tile(m, (g, 1)), 0)


def _block_diag_cols(m, g):
    return jnp.where(_group_mask(m.shape[0], g * m.shape[1], g), jnp.tile(m, (1, g)), 0)


def _diag_blocks(m, g):
    a, b = m.shape[0] // g, m.shape[1] // g
    masked = jnp.where(_group_mask(m.shape[0], m.shape[1], g), m, 0)
    return masked.reshape(m.shape[0], g, b).sum(axis=1).reshape(g, a, b)


def _scan_tables(pwr, pwi, reverse):
    row = jnp.arange(SUBLANES)[:, None]
    zero = jnp.zeros_like(pwr)
    levels = []
    for d in SCAN_LEVELS:
        keep = (row < SUBLANES - d) if reverse else (row >= d)
        levels.append(jnp.stack([jnp.where(keep, pwr[d - 1:d, :], zero), jnp.where(keep, pwi[d - 1:d, :], zero)]))
    carry = jnp.stack([pwr[::-1], pwi[::-1]]) if reverse else jnp.stack([pwr, pwi])
    return jnp.stack(levels), carry


def _step(inp):
    x, c, ctx = inp['x'][0], inp['c'], inp['ctx'][0]
    target = inp['loss_target'][0]
    n, d = x.shape
    nc = ctx.shape[0]
    l = n + nc
    q_w, kv_w = inp['q_a_g'].shape[1], inp['kv_a_g'].shape[1]
    sw = inp['d_skip'].shape[1]
    n_grp = sw // SSM_GROUP
    gn = n_grp * SSM_STATE
    lay = _in_layout(d, q_w, kv_w, sw)
    tm = _tile(math.gcd(n, nc), (256, 128))
    me = 4 * lax.axis_index("x") + 2 * lax.axis_index("y") + lax.axis_index("c")
    strip = lambda a: a if a.ndim <= 2 else a[0]
    w = {k: strip(inp[k]) for k in WEIGHT_NAMES}

    gathered_names = list(GATHERED)
    early_names = ['w_in', 'w_uq', 'w_ukv', 'w_o_attn']
    late_names = [k for k in gathered_names if k not in early_names]
    conv_b = w['conv_b']

    (c_all,) = exchange([jnp.broadcast_to(c, (SUBLANES, d))], ['gather'], "gather_c")
    cmat = jnp.concatenate([c_all[:, 0, :], w['c_ctx'][None, :], jnp.zeros((SUBLANES - 1, d), F32)], axis=0)
    mcols = w['w_mod'].shape[1]
    b_cols = lax.dynamic_slice(w['b_mod'], (0, me * mcols), (1, mcols))
    mod_part = mod_fwd(cmat, w['w_mod'], b_cols, "mod_fwd")
    (mod_all,) = exchange([mod_part], ['gather'], "gather_mod")
    mod_me = lax.dynamic_index_in_dim(mod_all, me, axis=1, keepdims=False).reshape(6, d)
    mod_ctx = mod_all[:, SUBLANES, :].reshape(6, d)
    sh1, sc1, g1, sh2, sc2, g2 = [mod_me[k:k + 1] for k in range(6)]
    mods1 = jnp.concatenate([sc1, sh1, mod_ctx[1:2], mod_ctx[0:1], jnp.zeros((4, d), F32)], axis=0)
    mods2 = jnp.concatenate([sc2, sh2, jnp.zeros((6, d), F32)], axis=0)

    xa = jnp.concatenate([x, ctx], axis=0)
    h, *wg = norm_mod_fwd(xa, w['norm1_g'], mods1, n, tm, "norm1_fwd", gather=[w[k].astype(BF16) for k in early_names])
    full = {k: _from_shards(s, GATHERED[k]) for k, s in zip(early_names, wg)}
    w_in_p = _pad_w_in(full['w_in'], lay)
    wuq_p = _pad_heads(full['w_uq'], QK_DIM)
    ukv = full['w_ukv'].reshape(kv_w, N_HEADS, QK_NOPE + V_DIM)
    wk_p = _pad_heads(ukv[:, :, :QK_NOPE].reshape(kv_w, -1), QK_NOPE)
    wv_p = _pad_heads(ukv[:, :, QK_NOPE:].reshape(kv_w, -1), V_DIM)
    wo_p = _pad_heads(full['w_o_attn'].T, V_DIM).T
    proj = matmul(h, w_in_p, 'nn', F32, "in_proj")
    tabs = _rope_tables(n, nc)
    pad_g = lambda g: jnp.pad(g, ((0, 0), (0, HEAD_PAD - QK_DIM)))
    gains = (w['q_a_g'], w['kv_a_g'], pad_g(w['q_norm_g']), pad_g(w['k_norm_g']))
    q, k, v = qkv_fwd(proj, lay, gains, wuq_p, wk_p, wv_p, tabs, tm, "qkv_fwd")
    o, lse, *wg = attn_fwd(q, k, v, n, "attn_fwd", gather=[w[k].astype(BF16) for k in late_names])
    full.update({k: _from_shards(s, GATHERED[k]) for k, s in zip(late_names, wg)})
    w_glu, w_out, w_up, w_down = full['w_glu'], full['w_out'], full['w_up'], full['w_down']
    conv_w = jnp.pad(full['conv_w'].astype(F32), ((0, SUBLANES - 3), (0, 0)))

    b_t = lambda a: a.transpose(2, 0, 1).reshape(SSM_GROUP, gn)
    c_t = lambda a: a.transpose(1, 0, 2).reshape(SSM_GROUP, gn)
    bre_t, bim_t = b_t(w['b_re']), b_t(w['b_im'])
    ssm_in, prep = [], []
    for sfx in ('f', 'b'):
        lam_re, lam_im = w['lam_re_' + sfx].reshape(1, gn), w['lam_im_' + sfx].reshape(1, gn)
        logdt = jnp.repeat(w['log_dt_' + sfx], SSM_STATE, axis=1)
        ssm_in.append((lam_re, lam_im, logdt))
        prep.append(ssm_prep(lam_re, lam_im, logdt, bre_t, bim_t, "ssm_prep_" + sfx))

    def blk_b(bb):
        return _block_diag_rows(bb, n_grp).astype(BF16)

    def blk_c(cc):
        return _block_diag_cols(cc.transpose(0, 2, 1).reshape(gn, SSM_GROUP), n_grp).astype(BF16)

    t_len = _tile(math.gcd(n, nc), (256, 128))
    ssm = []
    for di, sfx in enumerate(('f', 'b')):
        reverse = di == 1
        pwr, pwi, bbr, bbi = prep[di]
        ssm.append(dict(
            sfx=sfx, reverse=reverse, blocks=(blk_b(bbr), blk_b(bbi), blk_c(w['c_re_' + sfx]),
                                              blk_c(-w['c_im_' + sfx])),
            tabs=_scan_tables(pwr, pwi, reverse), adj_tabs=_scan_tables(pwr, -pwi, not reverse)))
    for s in ssm:
        s['y'], s['xb'] = ssm_fwd(proj, lay, *s['blocks'], s['tabs'], n, t_len, s['reverse'], "ssm_fwd_" + s['sfx'])
    yf, yb = ssm[0]['y'], ssm[1]['y']
    s_l = glu_fwd(proj, lay, yf, yb, w['d_skip'], w_glu, n, tm, "glu_fwd")
    x1 = merge_fwd(o, s_l, proj, xa, g1, wo_p, w_out, n, tm, "merge_fwd")

    h2 = norm_mod_fwd(x1, w['norm2_g'], mods2, n, tm, "norm2_fwd")
    up = matmul(h2, w_up, 'nn', F32, "up_proj")
    tw = _tile(n, (256, 128))
    act = conv_act_fwd(up, conv_w, conv_b, tw, "conv_act_fwd")
    dy, ffn, loss_parts = down_loss(act, w_down, x1, g2, target, tm, "down_loss")
    loss = lax.psum(jnp.sum(loss_parts[:, 0, 0]), MESH_AXES)

    dup, dffn, dg2, dconv = ffn_bwd(dy, ffn, up, conv_w, conv_b, w_down, g2, tm, "ffn_bwd")
    g_w_down = matmul(act, dffn, 'tn', F32, "dw_down")
    dh2 = matmul(dup, w_up, 'nt', F32, "dh2")
    g_w_up = matmul(h2, dup, 'tn', F32, "dw_up")
    dx1, dn2g, dmods2 = norm_mod_bwd(x1, dh2, dy, w['norm2_g'], mods2, n, tm, "norm2_bwd")

    do, ds_l, dgl, dg1, g_wo_p, g_w_out = merge_bwd(o, s_l, proj, xa, g1, wo_p, w_out, dx1, n, tm, "merge_bwd")
    du_direct, dyr, dds, g_w_glu = glu_bwd(proj, lay, yf, yb, w['d_skip'], w_glu, ds_l, n, tm, "glu_bwd")
    for s in ssm:
        s['du'], s['gr'], s['gi'], s['xr'], s['xi'], s['dlam'] = ssm_bwd(
            proj, lay, dyr, s['xb'], *s['blocks'], s['tabs'], s['adj_tabs'], n, t_len, s['reverse'],
            "ssm_bwd_" + s['sfx'])

    early_g = {'w_o_attn': _unpad_heads(g_wo_p.T, V_DIM).T, 'w_glu': g_w_glu, 'w_out': g_w_out, 'w_up': g_w_up,
               'conv_w': dconv[0:3], 'w_down': g_w_down}
    dq, dk, dv, *early_recv = attn_bwd(q, k, v, o, do, lse, n, "attn_bwd",
                                       a2a=[_to_shards(g, GATHERED[k]).astype(BF16) for k, g in early_g.items()])
    (dproj, dqag, dkvag, dqng, dkng, g_wuq_p, g_wk_p, g_wv_p) = qkv_bwd(
        proj, lay, gains, wuq_p, wk_p, wv_p, tabs, dq, dk, dv, dgl, du_direct, ssm[0]['du'], ssm[1]['du'], n, tm,
        "qkv_bwd")
    g_w_in_p = matmul(h, dproj, 'tn', F32, "dw_in")
    ukv_g = jnp.concatenate([_unpad_heads(g_wk_p, QK_NOPE).reshape(kv_w, N_HEADS, QK_NOPE),
                             _unpad_heads(g_wv_p, V_DIM).reshape(kv_w, N_HEADS, V_DIM)], axis=2)
    late_g = {'w_in': _unpad_w_in(g_w_in_p, lay), 'w_uq': _unpad_heads(g_wuq_p, QK_DIM),
              'w_ukv': ukv_g.reshape(kv_w, -1)}
    dh, *late_recv = matmul(dproj, w_in_p, 'nt', F32, "dh",
                            a2a=[_to_shards(g, GATHERED[k]).astype(BF16) for k, g in late_g.items()])
    dxa, dn1g, dmods1 = norm_mod_bwd(xa, dh, dx1, w['norm1_g'], mods1, n, tm, "norm1_bwd")
    grad_x = dxa[:n]

    grads = {}
    d_bbar = [None, None]
    gp = n_grp // SSM_SPLIT
    for di, s in enumerate(ssm):
        sfx = s['sfx']
        products = (diag_outer(proj, lay['o_u'], sw, [s['gr'], s['gi']], SSM_SPLIT, "ssm_db_" + sfx)
                    + diag_outer(dyr, 0, sw, [s['xr'], s['xi']], SSM_SPLIT, "ssm_dc_" + sfx))
        g_bre, g_bim, g_cre, g_cim = [
            jnp.concatenate([_diag_blocks(m[p], gp) for p in range(SSM_SPLIT)], axis=0) for m in products]
        grads['c_re_' + sfx] = g_cre
        grads['c_im_' + sfx] = -g_cim
        to_t = lambda a: a.transpose(1, 0, 2).reshape(SSM_GROUP, gn)
        dlam_re = jnp.sum(s['dlam'][0], axis=0, keepdims=True)
        dlam_im = jnp.sum(s['dlam'][1], axis=0, keepdims=True)
        lam_re, lam_im, logdt = ssm_in[di]
        g_lr, g_li, g_dt, g_br, g_bi = ssm_prep_bwd(lam_re, lam_im, logdt, bre_t, bim_t, dlam_re, dlam_im,
                                                    to_t(g_bre), to_t(g_bim), "ssm_prep_bwd_" + sfx)
        grads['lam_re_' + sfx] = g_lr.reshape(n_grp, SSM_STATE)
        grads['lam_im_' + sfx] = g_li.reshape(n_grp, SSM_STATE)
        grads['log_dt_' + sfx] = jnp.sum(g_dt.reshape(n_grp, SSM_STATE), axis=1)[None, :]
        d_bbar[di] = (g_br, g_bi)
    from_t = lambda a: a.reshape(SSM_GROUP, n_grp, SSM_STATE).transpose(1, 2, 0)
    grads['b_re'] = from_t(d_bbar[0][0]) + from_t(d_bbar[1][0])
    grads['b_im'] = from_t(d_bbar[0][1]) + from_t(d_bbar[1][1])

    dmod = jnp.concatenate([dmods1[1:2], dmods1[0:1], dg1[0:1], dmods2[1:2], dmods2[0:1], dg2[0:1]], axis=1)
    dmod_ctx = jnp.concatenate([dmods1[3:4], dmods1[2:3], jnp.zeros((1, 4 * d), F32)], axis=1)
    dm_send = jnp.concatenate([dmod, dmod_ctx, jnp.zeros((SUBLANES - 2, 6 * d), F32)], axis=0)
    (dm_all,) = exchange([dm_send], ['gather'], "gather_dmod")
    g_all = jnp.concatenate([dm_all[:, 0, :], dm_all[:, 1, :]], axis=0)
    cols = lax.dynamic_slice(g_all.reshape(2 * N_DEV, N_DEV, mcols), (0, me, 0), (2 * N_DEV, 1, mcols))[:, 0, :]
    g_w_mod, dcmat, g_b_mod = mod_bwd(cmat, w['w_mod'], cols[:N_DEV], cols[N_DEV:], g_all, "mod_bwd")

    grads.update({'c_ctx': dcmat[SUBLANES], 'b_mod': g_b_mod[0:1], 'norm1_g': dn1g[0:1], 'norm2_g': dn2g[0:1],
                  'q_a_g': dqag[0:1], 'kv_a_g': dkvag[0:1], 'q_norm_g': dqng[0:1, :QK_DIM],
                  'k_norm_g': dkng[0:1, :QK_DIM], 'd_skip': dds[0:1], 'conv_b': dconv[3:4]})
    first = (me == 0).astype(F32)
    rep_parts = [grads[k] * first if k == 'b_mod' else grads[k] for k in REPLICATED]
    rpack, rspans = _pack(rep_parts, SUBLANES)
    my_core = lax.axis_index("c").astype(jnp.int32).reshape(1)
    r_both = jnp.broadcast_to(rpack[None, None], (2, 1) + rpack.shape)
    (from_sibling,) = exchange([r_both], ['others'], "grads_in_chip", group='core')
    r_sum = add_own(r_both, my_core, from_sibling[0], F32, "chip_sum_replicated")[0]
    (r_recv,) = exchange([r_sum], ['gather'], "grads_between_chips", group='chips')

    outs = {}

    def update(parts, k, w_k, m_k, v_k):
        res = reduce_adamw(parts, w_k, m_k, v_k, "adamw_" + k)
        return dict(zip(('grad_', 'delta_', 'new_m_', 'new_v_'), res))

    per_tensor = list(zip(early_g, early_recv)) + list(zip(late_g, late_recv)) + [('w_mod', g_w_mod[None])]
    for k, parts in per_tensor:
        for kind, a in update(parts, k, w[k], strip(inp['m_' + k]), strip(inp['v_' + k])).items():
            outs[kind + k] = a[None]
    rep = lambda prefix: _pack([strip(inp[prefix + k]) for k in REPLICATED], SUBLANES)[0]
    for kind, buf in update(r_recv, "replicated", rep(''), rep('m_'), rep('v_')).items():
        for k, a in zip(REPLICATED, _unpack(buf, rspans, [w[k].shape for k in REPLICATED])):
            outs[kind + k] = a if inp[k].ndim <= 2 else a[None]
    result = [loss, grad_x[None]]
    for kind in ('grad_', 'delta_', 'new_m_', 'new_v_'):
        result += [outs[kind + k] for k in WEIGHT_NAMES]
    return tuple(result)


_ARG_NAMES = (['x', 'c', 'ctx'] + WEIGHT_NAMES + ['loss_target'] + ['m_' + k for k in WEIGHT_NAMES]
              + ['v_' + k for k in WEIGHT_NAMES])


def kernel(*args):
    assert len(args) == len(_ARG_NAMES)
    return _step(dict(zip(_ARG_NAMES, args)))
```
